```python
import math
import jax, jax.numpy as jnp
from jax import lax
import numpy as np

D_MODEL = 1024
BATCH = 8
SEQ = 4096
DEPTH = 2

N_META = 16
HEAD_DIM = 64
N_Q_HEADS = 8
N_KV_HEADS = 2
Q_PER_KV = N_Q_HEADS // N_KV_HEADS
ATTN_WIDTH = N_Q_HEADS * HEAD_DIM
KV_WIDTH = N_KV_HEADS * HEAD_DIM
WINDOW = 128
BLOCK = 128
ROPE_THETA = 500000.0
ROT_DIM = HEAD_DIM // 4
SSM_WIDTH = D_MODEL // 2
SSM_GROUP = 16
SSM_GROUPS = SSM_WIDTH // SSM_GROUP
SSM_STATE = 64
N_BRANCH = 2
D_FF = 2816
IN_WIDTH = ATTN_WIDTH + 2 * KV_WIDTH + SSM_WIDTH + N_BRANCH * D_MODEL
EPS = 1e-6
NEG_INF = -1e30

kernel_name = "hybrid_swa_s5_gated_macaron"


def rmsnorm(x, g):
    x32 = x.astype(jnp.float32)
    r = x32 * lax.rsqrt(jnp.mean(x32 * x32, axis=-1, keepdims=True) + EPS)
    return (r * g.astype(jnp.float32)).astype(x.dtype)


def swiglu(h, w_gate, w_up, w_down):
    return (jax.nn.silu(h @ w_gate) * (h @ w_up)) @ w_down


def rope_tables(l):
    pos = jnp.arange(l, dtype=jnp.float32)
    inv_freq = ROPE_THETA ** (-jnp.arange(0, ROT_DIM, 2, dtype=jnp.float32) / ROT_DIM)
    ang = pos[:, None] * inv_freq[None, :]
    return jnp.cos(ang)[None, :, None, :], jnp.sin(ang)[None, :, None, :]


def apply_partial_rope(x, cos, sin):
    x32 = x.astype(jnp.float32)
    half = ROT_DIM // 2
    x1 = x32[..., :half]
    x2 = x32[..., half:ROT_DIM]
    rest = x32[..., ROT_DIM:]
    out = jnp.concatenate([x1 * cos - x2 * sin, x2 * cos + x1 * sin, rest], axis=-1)
    return out.astype(x.dtype)


def sliding_window_attention(q, k, v, sinks):
    b, l = q.shape[0], q.shape[1]
    pad_front = (-N_META) % BLOCK
    pad_back = (-(l + pad_front)) % BLOCK
    lp = l + pad_front + pad_back
    nb = lp // BLOCK
    padw = ((0, 0), (pad_front, pad_back), (0, 0), (0, 0))
    qb = jnp.pad(q, padw).reshape(b, nb, BLOCK, N_KV_HEADS, Q_PER_KV, HEAD_DIM)
    kb = jnp.pad(k, padw).reshape(b, nb, BLOCK, N_KV_HEADS, HEAD_DIM)
    vb = jnp.pad(v, padw).reshape(b, nb, BLOCK, N_KV_HEADS, HEAD_DIM)
    k_prev = jnp.pad(kb, ((0, 0), (1, 0), (0, 0), (0, 0), (0, 0)))[:, :-1]
    v_prev = jnp.pad(vb, ((0, 0), (1, 0), (0, 0), (0, 0), (0, 0)))[:, :-1]
    k_band = jnp.concatenate([k_prev, kb], axis=2)
    v_band = jnp.concatenate([v_prev, vb], axis=2)
    k_meta = k[:, :N_META]
    v_meta = v[:, :N_META]

    pos_q = (jnp.arange(lp) - pad_front).reshape(nb, BLOCK)
    pos_k = jnp.concatenate([pos_q - BLOCK, pos_q], axis=1)
    dist = pos_q[:, :, None] - pos_k[:, None, :]
    band_mask = (pos_k[:, None, :] >= N_META) & (dist >= 0) & (dist < WINDOW)
    meta_mask = jnp.arange(N_META)[None, None, :] <= pos_q[:, :, None]

    scale = HEAD_DIM ** -0.5
    s_band = jnp.einsum('bnqhgd,bnkhd->bnhgqk', qb, k_band).astype(jnp.float32) * scale
    s_meta = jnp.einsum('bnqhgd,bmhd->bnhgqm', qb, k_meta).astype(jnp.float32) * scale
    s_band = jnp.where(band_mask[None, :, None, None], s_band, NEG_INF)
    s_meta = jnp.where(meta_mask[None, :, None, None], s_meta, NEG_INF)
    sink = jnp.broadcast_to(
        sinks.astype(jnp.float32).reshape(N_KV_HEADS, Q_PER_KV)[None, None, :, :, None, None],
        s_band.shape[:-1] + (1,))
    probs = jax.nn.softmax(jnp.concatenate([s_band, s_meta, sink], axis=-1), axis=-1)
    p_band = probs[..., :2 * BLOCK].astype(v.dtype)
    p_meta = probs[..., 2 * BLOCK:2 * BLOCK + N_META].astype(v.dtype)
    out = (jnp.einsum('bnhgqk,bnkhd->bnqhgd', p_band, v_band)
           + jnp.einsum('bnhgqm,bmhd->bnqhgd', p_meta, v_meta))
    out = out.reshape(b, lp, ATTN_WIDTH)
    return out[:, pad_front:pad_front + l]


def s5_ssm(u, a_re, a_im, log_dt, b_re, b_im, c_re, c_im, d_skip):
    bsz, l = u.shape[0], u.shape[1]
    u32 = u.astype(jnp.float32).reshape(bsz, l, SSM_GROUPS, SSM_GROUP)
    dt = jnp.exp(log_dt.astype(jnp.float32))[:, None]
    ar = a_re.astype(jnp.float32)
    ai = a_im.astype(jnp.float32)
    mag = jnp.exp(ar * dt)
    lb_re = mag * jnp.cos(ai * dt)
    lb_im = mag * jnp.sin(ai * dt)
    den = ar * ar + ai * ai
    num_re = lb_re - 1.0
    coef_re = (num_re * ar + lb_im * ai) / den
    coef_im = (lb_im * ar - num_re * ai) / den
    br = b_re.astype(jnp.float32)
    bi = b_im.astype(jnp.float32)
    bb_re = coef_re[..., None] * br - coef_im[..., None] * bi
    bb_im = coef_re[..., None] * bi + coef_im[..., None] * br
    bu_re = jnp.einsum('blgc,gpc->blgp', u32, bb_re)
    bu_im = jnp.einsum('blgc,gpc->blgp', u32, bb_im)
    la_re = jnp.broadcast_to(lb_re, bu_re.shape)
    la_im = jnp.broadcast_to(lb_im, bu_im.shape)

    def combine(e1, e2):
        a1r, a1i, b1r, b1i = e1
        a2r, a2i, b2r, b2i = e2
        return (a1r * a2r - a1i * a2i,
                a1r * a2i + a1i * a2r,
                a2r * b1r - a2i * b1i + b2r,
                a2r * b1i + a2i * b1r + b2i)

    _, _, h_re, h_im = lax.associative_scan(combine, (la_re, la_im, bu_re, bu_im), axis=1)
    y = (jnp.einsum('blgp,gcp->blgc', h_re, c_re.astype(jnp.float32))
         - jnp.einsum('blgp,gcp->blgc', h_im, c_im.astype(jnp.float32)))
    y = y.reshape(bsz, l, SSM_WIDTH) + d_skip.astype(jnp.float32) * u32.reshape(bsz, l, SSM_WIDTH)
    return y.astype(u.dtype)


def hybrid_mixer(h, cos, sin, w_in, attn_sinks, ssm_a_re, ssm_a_im, ssm_log_dt,
                 ssm_b_re, ssm_b_im, ssm_c_re, ssm_c_im, ssm_d,
                 w_attn_proj, w_glu_v, w_glu_g, w_out):
    bsz, l = h.shape[0], h.shape[1]
    z = h @ w_in
    o1 = ATTN_WIDTH
    o2 = o1 + KV_WIDTH
    o3 = o2 + KV_WIDTH
    o4 = o3 + SSM_WIDTH
    o5 = o4 + D_MODEL
    q = z[..., :o1].reshape(bsz, l, N_Q_HEADS, HEAD_DIM)
    k = z[..., o1:o2].reshape(bsz, l, N_KV_HEADS, HEAD_DIM)
    v = z[..., o2:o3].reshape(bsz, l, N_KV_HEADS, HEAD_DIM)
    u = z[..., o3:o4]
    g_attn = z[..., o4:o5]
    g_ssm = z[..., o5:]
    q = apply_partial_rope(q, cos, sin)
    k = apply_partial_rope(k, cos, sin)
    attn = sliding_window_attention(q, k, v, attn_sinks) @ w_attn_proj
    y = jax.nn.gelu(s5_ssm(u, ssm_a_re, ssm_a_im, ssm_log_dt, ssm_b_re, ssm_b_im,
                           ssm_c_re, ssm_c_im, ssm_d))
    ssm = (y @ w_glu_v) * jax.nn.sigmoid(y @ w_glu_g)
    merged = jax.nn.sigmoid(g_attn) * attn + jax.nn.sigmoid(g_ssm) * ssm
    return merged @ w_out


def _fwd_setup_inputs(seed: int = 0) -> dict:
    key = jax.random.key(seed)
    ks = jax.random.split(key, 40)
    f32 = jnp.float32

    def nrm(k, shape, fan_in):
        return jax.random.normal(k, shape, f32) * (fan_in ** -0.5)

    def gain(k, shape):
        return 1.0 + 0.01 * jax.random.normal(k, shape, f32)

    L, G, P, C = DEPTH, SSM_GROUPS, SSM_STATE, SSM_GROUP
    a_re = -0.5 * jnp.exp(0.01 * jax.random.normal(ks[20], (L, G, P), f32))
    a_im = jnp.broadcast_to(jnp.pi * jnp.arange(P, dtype=f32), (L, G, P)) \
        + 0.001 * jax.random.normal(ks[21], (L, G, P), f32)
    log_dt = jax.random.uniform(ks[22], (L, G), f32, math.log(0.001), math.log(0.1))
    return {
        "x": jax.random.normal(ks[0], (BATCH, SEQ, D_MODEL), f32),
        "meta_tokens": jax.random.normal(ks[1], (N_META, D_MODEL), f32),
        "ffn1_norm": gain(ks[2], (L, D_MODEL)),
        "ffn1_w_gate": nrm(ks[3], (L, D_MODEL, D_FF), D_MODEL),
        "ffn1_w_up": nrm(ks[4], (L, D_MODEL, D_FF), D_MODEL),
        "ffn1_w_down": nrm(ks[5], (L, D_FF, D_MODEL), D_FF),
        "mix_norm": gain(ks[6], (L, D_MODEL)),
        "w_in": nrm(ks[7], (L, D_MODEL, IN_WIDTH), D_MODEL),
        "attn_sinks": 0.5 * jax.random.normal(ks[8], (L, N_Q_HEADS), f32),
        "ssm_a_re": a_re,
        "ssm_a_im": a_im,
        "ssm_log_dt": log_dt,
        "ssm_b_re": nrm(ks[9], (L, G, P, C), 2 * C),
        "ssm_b_im": nrm(ks[10], (L, G, P, C), 2 * C),
        "ssm_c_re": nrm(ks[11], (L, G, C, P), 2 * P),
        "ssm_c_im": nrm(ks[12], (L, G, C, P), 2 * P),
        "ssm_d": jax.random.normal(ks[13], (L, SSM_WIDTH), f32),
        "w_attn_proj": nrm(ks[14], (L, ATTN_WIDTH, D_MODEL), ATTN_WIDTH),
        "w_glu_v": nrm(ks[15], (L, SSM_WIDTH, D_MODEL), SSM_WIDTH),
        "w_glu_g": nrm(ks[16], (L, SSM_WIDTH, D_MODEL), SSM_WIDTH),
        "w_out": nrm(ks[17], (L, D_MODEL, D_MODEL), D_MODEL),
        "ffn2_norm": gain(ks[18], (L, D_MODEL)),
        "ffn2_w_gate": nrm(ks[19], (L, D_MODEL, D_FF), D_MODEL),
        "ffn2_w_up": nrm(ks[23], (L, D_MODEL, D_FF), D_MODEL),
        "ffn2_w_down": nrm(ks[24], (L, D_FF, D_MODEL), D_FF),
        "final_norm": gain(ks[25], (D_MODEL,)),
    }


def _fwd_reference(x, meta_tokens, ffn1_norm, ffn1_w_gate, ffn1_w_up, ffn1_w_down,
              mix_norm, w_in, attn_sinks, ssm_a_re, ssm_a_im, ssm_log_dt,
              ssm_b_re, ssm_b_im, ssm_c_re, ssm_c_im, ssm_d,
              w_attn_proj, w_glu_v, w_glu_g, w_out,
              ffn2_norm, ffn2_w_gate, ffn2_w_up, ffn2_w_down, final_norm):
    bsz, seq = x.shape[0], x.shape[1]
    meta = jnp.broadcast_to(meta_tokens.astype(x.dtype)[None], (bsz, N_META, D_MODEL))
    h = jnp.concatenate([meta, x], axis=1)
    cos, sin = rope_tables(seq + N_META)
    for i in range(DEPTH):
        h = h + 0.5 * swiglu(rmsnorm(h, ffn1_norm[i]), ffn1_w_gate[i], ffn1_w_up[i], ffn1_w_down[i])
        h = h + hybrid_mixer(rmsnorm(h, mix_norm[i]), cos, sin, w_in[i], attn_sinks[i],
                             ssm_a_re[i], ssm_a_im[i], ssm_log_dt[i],
                             ssm_b_re[i], ssm_b_im[i], ssm_c_re[i], ssm_c_im[i], ssm_d[i],
                             w_attn_proj[i], w_glu_v[i], w_glu_g[i], w_out[i])
        h = h + 0.5 * swiglu(rmsnorm(h, ffn2_norm[i]), ffn2_w_gate[i], ffn2_w_up[i], ffn2_w_down[i])
    h = rmsnorm(h, final_norm)
    return h[:, N_META:]


import jax as _jax
import jax.numpy as _jnp

TWIN_FORMAT = 'train_step'
FWD_PARAMS = ['x', 'meta_tokens', 'ffn1_norm', 'ffn1_w_gate', 'ffn1_w_up', 'ffn1_w_down', 'mix_norm', 'w_in', 'attn_sinks', 'ssm_a_re', 'ssm_a_im', 'ssm_log_dt', 'ssm_b_re', 'ssm_b_im', 'ssm_c_re', 'ssm_c_im', 'ssm_d', 'w_attn_proj', 'w_glu_v', 'w_glu_g', 'w_out', 'ffn2_norm', 'ffn2_w_gate', 'ffn2_w_up', 'ffn2_w_down', 'final_norm']
TWIN_WEIGHTS = ['meta_tokens', 'ffn1_norm', 'ffn1_w_gate', 'ffn1_w_up', 'ffn1_w_down', 'mix_norm', 'w_in', 'attn_sinks', 'ssm_a_re', 'ssm_a_im', 'ssm_log_dt', 'ssm_b_re', 'ssm_b_im', 'ssm_c_re', 'ssm_c_im', 'ssm_d', 'w_attn_proj', 'w_glu_v', 'w_glu_g', 'w_out', 'ffn2_norm', 'ffn2_w_gate', 'ffn2_w_up', 'ffn2_w_down', 'final_norm']
TWIN_DIFF_INPUT = 'x'
TWIN_INPUTS = ['x', 'meta_tokens', 'ffn1_norm', 'ffn1_w_gate', 'ffn1_w_up', 'ffn1_w_down', 'mix_norm', 'w_in', 'attn_sinks', 'ssm_a_re', 'ssm_a_im', 'ssm_log_dt', 'ssm_b_re', 'ssm_b_im', 'ssm_c_re', 'ssm_c_im', 'ssm_d', 'w_attn_proj', 'w_glu_v', 'w_glu_g', 'w_out', 'ffn2_norm', 'ffn2_w_gate', 'ffn2_w_up', 'ffn2_w_down', 'final_norm', 'loss_target', 'm_meta_tokens', 'm_ffn1_norm', 'm_ffn1_w_gate', 'm_ffn1_w_up', 'm_ffn1_w_down', 'm_mix_norm', 'm_w_in', 'm_attn_sinks', 'm_ssm_a_re', 'm_ssm_a_im', 'm_ssm_log_dt', 'm_ssm_b_re', 'm_ssm_b_im', 'm_ssm_c_re', 'm_ssm_c_im', 'm_ssm_d', 'm_w_attn_proj', 'm_w_glu_v', 'm_w_glu_g', 'm_w_out', 'm_ffn2_norm', 'm_ffn2_w_gate', 'm_ffn2_w_up', 'm_ffn2_w_down', 'm_final_norm', 'v_meta_tokens', 'v_ffn1_norm', 'v_ffn1_w_gate', 'v_ffn1_w_up', 'v_ffn1_w_down', 'v_mix_norm', 'v_w_in', 'v_attn_sinks', 'v_ssm_a_re', 'v_ssm_a_im', 'v_ssm_log_dt', 'v_ssm_b_re', 'v_ssm_b_im', 'v_ssm_c_re', 'v_ssm_c_im', 'v_ssm_d', 'v_w_attn_proj', 'v_w_glu_v', 'v_w_glu_g', 'v_w_out', 'v_ffn2_norm', 'v_ffn2_w_gate', 'v_ffn2_w_up', 'v_ffn2_w_down', 'v_final_norm']
TWIN_OUTPUTS = ['loss', 'grad_x', 'grad_meta_tokens', 'grad_ffn1_norm', 'grad_ffn1_w_gate', 'grad_ffn1_w_up', 'grad_ffn1_w_down', 'grad_mix_norm', 'grad_w_in', 'grad_attn_sinks', 'grad_ssm_a_re', 'grad_ssm_a_im', 'grad_ssm_log_dt', 'grad_ssm_b_re', 'grad_ssm_b_im', 'grad_ssm_c_re', 'grad_ssm_c_im', 'grad_ssm_d', 'grad_w_attn_proj', 'grad_w_glu_v', 'grad_w_glu_g', 'grad_w_out', 'grad_ffn2_norm', 'grad_ffn2_w_gate', 'grad_ffn2_w_up', 'grad_ffn2_w_down', 'grad_final_norm', 'delta_meta_tokens', 'delta_ffn1_norm', 'delta_ffn1_w_gate', 'delta_ffn1_w_up', 'delta_ffn1_w_down', 'delta_mix_norm', 'delta_w_in', 'delta_attn_sinks', 'delta_ssm_a_re', 'delta_ssm_a_im', 'delta_ssm_log_dt', 'delta_ssm_b_re', 'delta_ssm_b_im', 'delta_ssm_c_re', 'delta_ssm_c_im', 'delta_ssm_d', 'delta_w_attn_proj', 'delta_w_glu_v', 'delta_w_glu_g', 'delta_w_out', 'delta_ffn2_norm', 'delta_ffn2_w_gate', 'delta_ffn2_w_up', 'delta_ffn2_w_down', 'delta_final_norm', 'new_m_meta_tokens', 'new_m_ffn1_norm', 'new_m_ffn1_w_gate', 'new_m_ffn1_w_up', 'new_m_ffn1_w_down', 'new_m_mix_norm', 'new_m_w_in', 'new_m_attn_sinks', 'new_m_ssm_a_re', 'new_m_ssm_a_im', 'new_m_ssm_log_dt', 'new_m_ssm_b_re', 'new_m_ssm_b_im', 'new_m_ssm_c_re', 'new_m_ssm_c_im', 'new_m_ssm_d', 'new_m_w_attn_proj', 'new_m_w_glu_v', 'new_m_w_glu_g', 'new_m_w_out', 'new_m_ffn2_norm', 'new_m_ffn2_w_gate', 'new_m_ffn2_w_up', 'new_m_ffn2_w_down', 'new_m_final_norm', 'new_v_meta_tokens', 'new_v_ffn1_norm', 'new_v_ffn1_w_gate', 'new_v_ffn1_w_up', 'new_v_ffn1_w_down', 'new_v_mix_norm', 'new_v_w_in', 'new_v_attn_sinks', 'new_v_ssm_a_re', 'new_v_ssm_a_im', 'new_v_ssm_log_dt', 'new_v_ssm_b_re', 'new_v_ssm_b_im', 'new_v_ssm_c_re', 'new_v_ssm_c_im', 'new_v_ssm_d', 'new_v_w_attn_proj', 'new_v_w_glu_v', 'new_v_w_glu_g', 'new_v_w_out', 'new_v_ffn2_norm', 'new_v_ffn2_w_gate', 'new_v_ffn2_w_up', 'new_v_ffn2_w_down', 'new_v_final_norm']
TWIN_LEAF_KINDS = {'loss': 'loss', 'grad_x': 'grad_x', 'grad_meta_tokens': 'grad_w', 'grad_ffn1_norm': 'grad_w', 'grad_ffn1_w_gate': 'grad_w', 'grad_ffn1_w_up': 'grad_w', 'grad_ffn1_w_down': 'grad_w', 'grad_mix_norm': 'grad_w', 'grad_w_in': 'grad_w', 'grad_attn_sinks': 'grad_w', 'grad_ssm_a_re': 'grad_w', 'grad_ssm_a_im': 'grad_w', 'grad_ssm_log_dt': 'grad_w', 'grad_ssm_b_re': 'grad_w', 'grad_ssm_b_im': 'grad_w', 'grad_ssm_c_re': 'grad_w', 'grad_ssm_c_im': 'grad_w', 'grad_ssm_d': 'grad_w', 'grad_w_attn_proj': 'grad_w', 'grad_w_glu_v': 'grad_w', 'grad_w_glu_g': 'grad_w', 'grad_w_out': 'grad_w', 'grad_ffn2_norm': 'grad_w', 'grad_ffn2_w_gate': 'grad_w', 'grad_ffn2_w_up': 'grad_w', 'grad_ffn2_w_down': 'grad_w', 'grad_final_norm': 'grad_w', 'delta_meta_tokens': 'delta_w', 'delta_ffn1_norm': 'delta_w', 'delta_ffn1_w_gate': 'delta_w', 'delta_ffn1_w_up': 'delta_w', 'delta_ffn1_w_down': 'delta_w', 'delta_mix_norm': 'delta_w', 'delta_w_in': 'delta_w', 'delta_attn_sinks': 'delta_w', 'delta_ssm_a_re': 'delta_w', 'delta_ssm_a_im': 'delta_w', 'delta_ssm_log_dt': 'delta_w', 'delta_ssm_b_re': 'delta_w', 'delta_ssm_b_im': 'delta_w', 'delta_ssm_c_re': 'delta_w', 'delta_ssm_c_im': 'delta_w', 'delta_ssm_d': 'delta_w', 'delta_w_attn_proj': 'delta_w', 'delta_w_glu_v': 'delta_w', 'delta_w_glu_g': 'delta_w', 'delta_w_out': 'delta_w', 'delta_ffn2_norm': 'delta_w', 'delta_ffn2_w_gate': 'delta_w', 'delta_ffn2_w_up': 'delta_w', 'delta_ffn2_w_down': 'delta_w', 'delta_final_norm': 'delta_w', 'new_m_meta_tokens': 'new_m', 'new_m_ffn1_norm': 'new_m', 'new_m_ffn1_w_gate': 'new_m', 'new_m_ffn1_w_up': 'new_m', 'new_m_ffn1_w_down': 'new_m', 'new_m_mix_norm': 'new_m', 'new_m_w_in': 'new_m', 'new_m_attn_sinks': 'new_m', 'new_m_ssm_a_re': 'new_m', 'new_m_ssm_a_im': 'new_m', 'new_m_ssm_log_dt': 'new_m', 'new_m_ssm_b_re': 'new_m', 'new_m_ssm_b_im': 'new_m', 'new_m_ssm_c_re': 'new_m', 'new_m_ssm_c_im': 'new_m', 'new_m_ssm_d': 'new_m', 'new_m_w_attn_proj': 'new_m', 'new_m_w_glu_v': 'new_m', 'new_m_w_glu_g': 'new_m', 'new_m_w_out': 'new_m', 'new_m_ffn2_norm': 'new_m', 'new_m_ffn2_w_gate': 'new_m', 'new_m_ffn2_w_up': 'new_m', 'new_m_ffn2_w_down': 'new_m', 'new_m_final_norm': 'new_m', 'new_v_meta_tokens': 'new_v', 'new_v_ffn1_norm': 'new_v', 'new_v_ffn1_w_gate': 'new_v', 'new_v_ffn1_w_up': 'new_v', 'new_v_ffn1_w_down': 'new_v', 'new_v_mix_norm': 'new_v', 'new_v_w_in': 'new_v', 'new_v_attn_sinks': 'new_v', 'new_v_ssm_a_re': 'new_v', 'new_v_ssm_a_im': 'new_v', 'new_v_ssm_log_dt': 'new_v', 'new_v_ssm_b_re': 'new_v', 'new_v_ssm_b_im': 'new_v', 'new_v_ssm_c_re': 'new_v', 'new_v_ssm_c_im': 'new_v', 'new_v_ssm_d': 'new_v', 'new_v_w_attn_proj': 'new_v', 'new_v_w_glu_v': 'new_v', 'new_v_w_glu_g': 'new_v', 'new_v_w_out': 'new_v', 'new_v_ffn2_norm': 'new_v', 'new_v_ffn2_w_gate': 'new_v', 'new_v_ffn2_w_up': 'new_v', 'new_v_ffn2_w_down': 'new_v', 'new_v_final_norm': 'new_v'}


def _forward(args):
    return _fwd_reference(*[args[k] for k in FWD_PARAMS])


def _output_shape():
    out = _jax.eval_shape(lambda: _forward(_fwd_setup_inputs(0)))
    return out.shape, out.dtype

N_MICROBATCH = 1
ADAM_LR = 0.001
ADAM_B1 = 0.9
ADAM_B2 = 0.999
ADAM_EPS = 1e-08
ADAM_WD = 0.01
ADAM_STEP = 10
PER_EXAMPLE_BATCH_AXIS = {'x': 0, 'loss_target': 0}
SHARED_INPUTS = []
_WEIGHT_DTYPES = {'meta_tokens': _jnp.float32, 'ffn1_norm': _jnp.float32, 'ffn1_w_gate': _jnp.float32, 'ffn1_w_up': _jnp.float32, 'ffn1_w_down': _jnp.float32, 'mix_norm': _jnp.float32, 'w_in': _jnp.float32, 'attn_sinks': _jnp.float32, 'ssm_a_re': _jnp.float32, 'ssm_a_im': _jnp.float32, 'ssm_log_dt': _jnp.float32, 'ssm_b_re': _jnp.float32, 'ssm_b_im': _jnp.float32, 'ssm_c_re': _jnp.float32, 'ssm_c_im': _jnp.float32, 'ssm_d': _jnp.float32, 'w_attn_proj': _jnp.float32, 'w_glu_v': _jnp.float32, 'w_glu_g': _jnp.float32, 'w_out': _jnp.float32, 'ffn2_norm': _jnp.float32, 'ffn2_w_gate': _jnp.float32, 'ffn2_w_up': _jnp.float32, 'ffn2_w_down': _jnp.float32, 'final_norm': _jnp.float32}
MOMENT_SCALE = {'meta_tokens': 4.613650e-03, 'ffn1_norm': 7.635327e-02, 'ffn1_w_gate': 3.302127e-02, 'ffn1_w_up': 3.200908e-02, 'ffn1_w_down': 5.288118e-02, 'mix_norm': 4.717569e-02, 'w_in': 2.650798e-02, 'attn_sinks': 3.926438e-03, 'ssm_a_re': 2.460550e-03, 'ssm_a_im': 2.399164e-03, 'ssm_log_dt': 1.948702e+00, 'ssm_b_re': 1.586070e-03, 'ssm_b_im': 1.576880e-03, 'ssm_c_re': 3.135826e-03, 'ssm_c_im': 3.190225e-03, 'ssm_d': 5.474646e-02, 'w_attn_proj': 1.740582e-02, 'w_glu_v': 3.498771e-02, 'w_glu_g': 1.007316e-02, 'w_out': 3.796941e-02, 'ffn2_norm': 7.008397e-02, 'ffn2_w_gate': 2.977214e-02, 'ffn2_w_up': 2.883308e-02, 'ffn2_w_down': 4.780614e-02, 'final_norm': 3.194293e+01}


def _to_microbatches(a, axis):
    t = _jnp.moveaxis(a, axis, 0)
    t = t.reshape((N_MICROBATCH, t.shape[0] // N_MICROBATCH) + t.shape[1:])
    return _jnp.moveaxis(t, 1, axis + 1)


def setup_inputs(seed: int = 0) -> dict:
    inp = _fwd_setup_inputs(seed)
    key = _jax.random.fold_in(_jax.random.key(seed), 7919)
    shape, _ = _output_shape()
    out = dict(inp)
    out["loss_target"] = _jax.random.normal(_jax.random.fold_in(key, 0), shape, _jnp.float32)
    for i, name in enumerate(TWIN_WEIGHTS):
        w = inp[name].astype(_jnp.float32)
        if MOMENT_SCALE is None:
            s = _jnp.sqrt(_jnp.mean(_jnp.square(w)) + 1e-30)
        else:
            s = MOMENT_SCALE[name]
        km, kv = _jax.random.split(_jax.random.fold_in(key, i + 1))
        out[name] = w
        out["m_" + name] = s * _jax.random.normal(km, w.shape, _jnp.float32)
        out["v_" + name] = (s * s) * _jax.random.uniform(kv, w.shape, _jnp.float32, 0.5, 1.5)
    if N_MICROBATCH > 1:
        for name, axis in PER_EXAMPLE_BATCH_AXIS.items():
            out[name] = _to_microbatches(out[name], axis)
    return {'x': out['x'], 'meta_tokens': out['meta_tokens'], 'ffn1_norm': out['ffn1_norm'], 'ffn1_w_gate': out['ffn1_w_gate'], 'ffn1_w_up': out['ffn1_w_up'], 'ffn1_w_down': out['ffn1_w_down'], 'mix_norm': out['mix_norm'], 'w_in': out['w_in'], 'attn_sinks': out['attn_sinks'], 'ssm_a_re': out['ssm_a_re'], 'ssm_a_im': out['ssm_a_im'], 'ssm_log_dt': out['ssm_log_dt'], 'ssm_b_re': out['ssm_b_re'], 'ssm_b_im': out['ssm_b_im'], 'ssm_c_re': out['ssm_c_re'], 'ssm_c_im': out['ssm_c_im'], 'ssm_d': out['ssm_d'], 'w_attn_proj': out['w_attn_proj'], 'w_glu_v': out['w_glu_v'], 'w_glu_g': out['w_glu_g'], 'w_out': out['w_out'], 'ffn2_norm': out['ffn2_norm'], 'ffn2_w_gate': out['ffn2_w_gate'], 'ffn2_w_up': out['ffn2_w_up'], 'ffn2_w_down': out['ffn2_w_down'], 'final_norm': out['final_norm'], 'loss_target': out['loss_target'], 'm_meta_tokens': out['m_meta_tokens'], 'm_ffn1_norm': out['m_ffn1_norm'], 'm_ffn1_w_gate': out['m_ffn1_w_gate'], 'm_ffn1_w_up': out['m_ffn1_w_up'], 'm_ffn1_w_down': out['m_ffn1_w_down'], 'm_mix_norm': out['m_mix_norm'], 'm_w_in': out['m_w_in'], 'm_attn_sinks': out['m_attn_sinks'], 'm_ssm_a_re': out['m_ssm_a_re'], 'm_ssm_a_im': out['m_ssm_a_im'], 'm_ssm_log_dt': out['m_ssm_log_dt'], 'm_ssm_b_re': out['m_ssm_b_re'], 'm_ssm_b_im': out['m_ssm_b_im'], 'm_ssm_c_re': out['m_ssm_c_re'], 'm_ssm_c_im': out['m_ssm_c_im'], 'm_ssm_d': out['m_ssm_d'], 'm_w_attn_proj': out['m_w_attn_proj'], 'm_w_glu_v': out['m_w_glu_v'], 'm_w_glu_g': out['m_w_glu_g'], 'm_w_out': out['m_w_out'], 'm_ffn2_norm': out['m_ffn2_norm'], 'm_ffn2_w_gate': out['m_ffn2_w_gate'], 'm_ffn2_w_up': out['m_ffn2_w_up'], 'm_ffn2_w_down': out['m_ffn2_w_down'], 'm_final_norm': out['m_final_norm'], 'v_meta_tokens': out['v_meta_tokens'], 'v_ffn1_norm': out['v_ffn1_norm'], 'v_ffn1_w_gate': out['v_ffn1_w_gate'], 'v_ffn1_w_up': out['v_ffn1_w_up'], 'v_ffn1_w_down': out['v_ffn1_w_down'], 'v_mix_norm': out['v_mix_norm'], 'v_w_in': out['v_w_in'], 'v_attn_sinks': out['v_attn_sinks'], 'v_ssm_a_re': out['v_ssm_a_re'], 'v_ssm_a_im': out['v_ssm_a_im'], 'v_ssm_log_dt': out['v_ssm_log_dt'], 'v_ssm_b_re': out['v_ssm_b_re'], 'v_ssm_b_im': out['v_ssm_b_im'], 'v_ssm_c_re': out['v_ssm_c_re'], 'v_ssm_c_im': out['v_ssm_c_im'], 'v_ssm_d': out['v_ssm_d'], 'v_w_attn_proj': out['v_w_attn_proj'], 'v_w_glu_v': out['v_w_glu_v'], 'v_w_glu_g': out['v_w_glu_g'], 'v_w_out': out['v_w_out'], 'v_ffn2_norm': out['v_ffn2_norm'], 'v_ffn2_w_gate': out['v_ffn2_w_gate'], 'v_ffn2_w_up': out['v_ffn2_w_up'], 'v_ffn2_w_down': out['v_ffn2_w_down'], 'v_final_norm': out['v_final_norm']}


def _loss(weights, diff, rest, loss_target):
    with _jax.named_scope("forward"):
        args = {**rest, TWIN_DIFF_INPUT: diff, **{k: w.astype(_WEIGHT_DTYPES[k]) for k, w in weights.items()}}
        y = _forward(args)
    with _jax.named_scope("loss_head"):
        err = _jnp.square(y.astype(_jnp.float32) - loss_target)
        return 0.5 * _jnp.sum(_jnp.mean(err, axis=-1)) if err.ndim else 0.5 * err


def _adamw(w, g, m, v):
    m = ADAM_B1 * m + (1.0 - ADAM_B1) * g
    v = ADAM_B2 * v + (1.0 - ADAM_B2) * _jnp.square(g)
    m_hat = m / (1.0 - ADAM_B1 ** ADAM_STEP)
    v_hat = v / (1.0 - ADAM_B2 ** ADAM_STEP)
    delta = -ADAM_LR * (m_hat / (_jnp.sqrt(v_hat) + ADAM_EPS) + ADAM_WD * w)
    return delta, m, v


def reference(x, meta_tokens, ffn1_norm, ffn1_w_gate, ffn1_w_up, ffn1_w_down, mix_norm, w_in, attn_sinks, ssm_a_re, ssm_a_im, ssm_log_dt, ssm_b_re, ssm_b_im, ssm_c_re, ssm_c_im, ssm_d, w_attn_proj, w_glu_v, w_glu_g, w_out, ffn2_norm, ffn2_w_gate, ffn2_w_up, ffn2_w_down, final_norm, loss_target, m_meta_tokens, m_ffn1_norm, m_ffn1_w_gate, m_ffn1_w_up, m_ffn1_w_down, m_mix_norm, m_w_in, m_attn_sinks, m_ssm_a_re, m_ssm_a_im, m_ssm_log_dt, m_ssm_b_re, m_ssm_b_im, m_ssm_c_re, m_ssm_c_im, m_ssm_d, m_w_attn_proj, m_w_glu_v, m_w_glu_g, m_w_out, m_ffn2_norm, m_ffn2_w_gate, m_ffn2_w_up, m_ffn2_w_down, m_final_norm, v_meta_tokens, v_ffn1_norm, v_ffn1_w_gate, v_ffn1_w_up, v_ffn1_w_down, v_mix_norm, v_w_in, v_attn_sinks, v_ssm_a_re, v_ssm_a_im, v_ssm_log_dt, v_ssm_b_re, v_ssm_b_im, v_ssm_c_re, v_ssm_c_im, v_ssm_d, v_w_attn_proj, v_w_glu_v, v_w_glu_g, v_w_out, v_ffn2_norm, v_ffn2_w_gate, v_ffn2_w_up, v_ffn2_w_down, v_final_norm):
    given = dict(x=x, meta_tokens=meta_tokens, ffn1_norm=ffn1_norm, ffn1_w_gate=ffn1_w_gate, ffn1_w_up=ffn1_w_up, ffn1_w_down=ffn1_w_down, mix_norm=mix_norm, w_in=w_in, attn_sinks=attn_sinks, ssm_a_re=ssm_a_re, ssm_a_im=ssm_a_im, ssm_log_dt=ssm_log_dt, ssm_b_re=ssm_b_re, ssm_b_im=ssm_b_im, ssm_c_re=ssm_c_re, ssm_c_im=ssm_c_im, ssm_d=ssm_d, w_attn_proj=w_attn_proj, w_glu_v=w_glu_v, w_glu_g=w_glu_g, w_out=w_out, ffn2_norm=ffn2_norm, ffn2_w_gate=ffn2_w_gate, ffn2_w_up=ffn2_w_up, ffn2_w_down=ffn2_w_down, final_norm=final_norm, loss_target=loss_target, m_meta_tokens=m_meta_tokens, m_ffn1_norm=m_ffn1_norm, m_ffn1_w_gate=m_ffn1_w_gate, m_ffn1_w_up=m_ffn1_w_up, m_ffn1_w_down=m_ffn1_w_down, m_mix_norm=m_mix_norm, m_w_in=m_w_in, m_attn_sinks=m_attn_sinks, m_ssm_a_re=m_ssm_a_re, m_ssm_a_im=m_ssm_a_im, m_ssm_log_dt=m_ssm_log_dt, m_ssm_b_re=m_ssm_b_re, m_ssm_b_im=m_ssm_b_im, m_ssm_c_re=m_ssm_c_re, m_ssm_c_im=m_ssm_c_im, m_ssm_d=m_ssm_d, m_w_attn_proj=m_w_attn_proj, m_w_glu_v=m_w_glu_v, m_w_glu_g=m_w_glu_g, m_w_out=m_w_out, m_ffn2_norm=m_ffn2_norm, m_ffn2_w_gate=m_ffn2_w_gate, m_ffn2_w_up=m_ffn2_w_up, m_ffn2_w_down=m_ffn2_w_down, m_final_norm=m_final_norm, v_meta_tokens=v_meta_tokens, v_ffn1_norm=v_ffn1_norm, v_ffn1_w_gate=v_ffn1_w_gate, v_ffn1_w_up=v_ffn1_w_up, v_ffn1_w_down=v_ffn1_w_down, v_mix_norm=v_mix_norm, v_w_in=v_w_in, v_attn_sinks=v_attn_sinks, v_ssm_a_re=v_ssm_a_re, v_ssm_a_im=v_ssm_a_im, v_ssm_log_dt=v_ssm_log_dt, v_ssm_b_re=v_ssm_b_re, v_ssm_b_im=v_ssm_b_im, v_ssm_c_re=v_ssm_c_re, v_ssm_c_im=v_ssm_c_im, v_ssm_d=v_ssm_d, v_w_attn_proj=v_w_attn_proj, v_w_glu_v=v_w_glu_v, v_w_glu_g=v_w_glu_g, v_w_out=v_w_out, v_ffn2_norm=v_ffn2_norm, v_ffn2_w_gate=v_ffn2_w_gate, v_ffn2_w_up=v_ffn2_w_up, v_ffn2_w_down=v_ffn2_w_down, v_final_norm=v_final_norm)
    weights = {n: given[n] for n in TWIN_WEIGHTS}
    shared = {n: given[n] for n in SHARED_INPUTS}
    per_example = {n: given[n] for n in ['x']}
    grad_fn = _jax.value_and_grad(_loss, argnums=(0, 1))

    def one_microbatch(ex, loss_target):
        ex = dict(ex)
        diff = ex.pop(TWIN_DIFF_INPUT)
        return grad_fn(weights, diff, {**shared, **ex}, loss_target)

    if N_MICROBATCH == 1:
        loss, (grad_w, grad_x) = one_microbatch(per_example, given["loss_target"])
    else:
        def body(carry, xs):
            loss_sum, grad_sum = carry
            l_k, (gw_k, gx_k) = one_microbatch(xs[0], xs[1])
            with _jax.named_scope("update"):
                return (loss_sum + l_k, _jax.tree.map(_jnp.add, grad_sum, gw_k)), gx_k

        init = (_jnp.zeros((), _jnp.float32), _jax.tree.map(_jnp.zeros_like, weights))
        (loss, grad_w), grad_x = _jax.lax.scan(body, init, (per_example, given["loss_target"]))
    with _jax.named_scope("update"):
        delta_w, new_m, new_v = {}, {}, {}
        for n in TWIN_WEIGHTS:
            delta_w[n], new_m[n], new_v[n] = _adamw(weights[n], grad_w[n], given["m_" + n], given["v_" + n])
    return (loss, grad_x, *[grad_w[n] for n in TWIN_WEIGHTS], *[delta_w[n] for n in TWIN_WEIGHTS],
            *[new_m[n] for n in TWIN_WEIGHTS], *[new_v[n] for n in TWIN_WEIGHTS])
```

```python
import functools
import math

import jax
import jax.numpy as jnp
from jax import lax
from jax.experimental import pallas as pl
from jax.experimental.pallas import tpu as pltpu

F32 = jnp.float32
BF16 = jnp.bfloat16

D_MODEL = 1024
DEPTH = 2
N_META = 16
HEAD_DIM = 64
N_Q_HEADS = 8
ATTN_WIDTH = 512
KV_WIDTH = 128
QKV_WIDTH = ATTN_WIDTH + 2 * KV_WIDTH
WINDOW = 128
BLK = 128
ROPE_THETA = 500000.0
ROT_DIM = 16
SSM_WIDTH = 512
SSM_GROUP = 16
SSM_GROUPS = 32
SSM_STATE = 64
STATE_WIDTH = SSM_GROUPS * SSM_STATE
D_FF = 2816
IN_WIDTH = 3328
EPS = 1e-6
NEG_INF = -1e30
PAD_FRONT = (-N_META) % BLK
N_DEV = 8

ADAM_LR = 0.001
ADAM_B1 = 0.9
ADAM_B2 = 0.999
ADAM_EPS = 1e-08
ADAM_WD = 0.01
ADAM_STEP = 10

VMEM_LIMIT = 56 * 1024 * 1024
TOKEN_TILE = 384
_VMEM = pl.BlockSpec(memory_space=pltpu.VMEM)
_SMEM = pl.BlockSpec(memory_space=pltpu.SMEM)
_ANY = pl.BlockSpec(memory_space=pl.ANY)
MESH = pl.DeviceIdType.MESH


def _params(sem=None):
    return pltpu.CompilerParams(dimension_semantics=sem, vmem_limit_bytes=VMEM_LIMIT)


def _nt(a, b):
    return lax.dot_general(a, b, (((1,), (1,)), ((), ())), preferred_element_type=F32)


def _nn(a, b):
    return jnp.dot(a, b, preferred_element_type=F32)


def _tn(a, b):
    return lax.dot_general(a, b, (((0,), (0,)), ((), ())), preferred_element_type=F32)


def _row_spec(tm, width):
    return pl.BlockSpec((tm, width), lambda i: (i, 0))


def _acc_spec(shape):
    return pl.BlockSpec(shape, lambda i: (0,) * len(shape))


def _rms_stats(x):
    r = lax.rsqrt(jnp.mean(x * x, axis=-1, keepdims=True) + EPS)
    return x * r, r


def _rms_bwd(dn, xh, r, g):
    dg = jnp.sum(dn * xh, axis=0, keepdims=True)
    dxh = dn * g
    dx = r * (dxh - xh * jnp.mean(dxh * xh, axis=-1, keepdims=True))
    return dx, dg


def ffn_fwd(h, g, wg_t, wu_t, wd, name):
    t, d = h.shape
    f = wd.shape[0]
    tm = TOKEN_TILE

    def body(h_ref, g_ref, wg_ref, wu_ref, wd_ref, ho_ref, n_ref, a_ref, b_ref):
        x = h_ref[...]
        xh, _ = _rms_stats(x)
        n = (xh * g_ref[...]).astype(BF16)
        n_ref[...] = n
        a = _nt(n, wg_ref[...])
        b = _nt(n, wu_ref[...])
        a_ref[...] = a.astype(BF16)
        b_ref[...] = b.astype(BF16)
        s = (a * jax.nn.sigmoid(a) * b).astype(BF16)
        ho_ref[...] = x + 0.5 * _nn(s, wd_ref[...])

    return pl.pallas_call(
        body, name=name, grid=(t // tm,),
        in_specs=[_row_spec(tm, d), _acc_spec((1, d)), _VMEM, _VMEM, _VMEM],
        out_specs=[_row_spec(tm, d), _row_spec(tm, d), _row_spec(tm, f), _row_spec(tm, f)],
        out_shape=[jax.ShapeDtypeStruct((t, d), F32), jax.ShapeDtypeStruct((t, d), BF16),
                   jax.ShapeDtypeStruct((t, f), BF16), jax.ShapeDtypeStruct((t, f), BF16)],
        compiler_params=_params(("arbitrary",)),
    )(h, g, wg_t, wu_t, wd)


def ffn_bwd(dh, h, g, a, b, wg_t, wu_t, wd, name):
    t, d = h.shape
    f = wd.shape[0]
    tm = TOKEN_TILE // 2

    def body(dh_ref, h_ref, g_ref, a_ref, b_ref, wg_ref, wu_ref, wd_ref,
             dhi_ref, da_ref, db_ref, s_ref, dhb_ref, dg_ref):
        dh_t = dh_ref[...]
        dhb = (0.5 * dh_t).astype(BF16)
        dhb_ref[...] = dhb
        ds = _nt(dhb, wd_ref[...])
        av = a_ref[...].astype(F32)
        bv = b_ref[...].astype(F32)
        sig = jax.nn.sigmoid(av)
        sl = av * sig
        s_ref[...] = (sl * bv).astype(BF16)
        da = (ds * bv * (sig * (1.0 + av * (1.0 - sig)))).astype(BF16)
        db = (ds * sl).astype(BF16)
        da_ref[...] = da
        db_ref[...] = db
        dn = _nn(da, wg_ref[...]) + _nn(db, wu_ref[...])
        xh, r = _rms_stats(h_ref[...])
        dx, dg = _rms_bwd(dn, xh, r, g_ref[...])
        dhi_ref[...] = dh_t + dx

        @pl.when(pl.program_id(0) == 0)
        def _():
            dg_ref[...] = jnp.zeros_like(dg_ref)

        dg_ref[...] += dg

    return pl.pallas_call(
        body, name=name, grid=(t // tm,),
        in_specs=[_row_spec(tm, d), _row_spec(tm, d), _acc_spec((1, d)), _row_spec(tm, f), _row_spec(tm, f),
                  _VMEM, _VMEM, _VMEM],
        out_specs=[_row_spec(tm, d), _row_spec(tm, f), _row_spec(tm, f), _row_spec(tm, f), _row_spec(tm, d),
                   _acc_spec((1, d))],
        out_shape=[jax.ShapeDtypeStruct((t, d), F32), jax.ShapeDtypeStruct((t, f), BF16),
                   jax.ShapeDtypeStruct((t, f), BF16), jax.ShapeDtypeStruct((t, f), BF16),
                   jax.ShapeDtypeStruct((t, d), BF16), jax.ShapeDtypeStruct((1, d), F32)],
        compiler_params=_params(("arbitrary",)),
    )(dh, h, g, a, b, wg_t, wu_t, wd)


def _col_tile(m):
    for bm in (1408, 1664, 1024, 768, 512):
        if m % bm == 0:
            return bm
    raise ValueError(m)


def tn_matmul(x, y, name):
    t, m = x.shape
    n = y.shape[1]
    bm = _col_tile(m)
    bt = TOKEN_TILE

    def body(x_ref, y_ref, o_ref):
        @pl.when(pl.program_id(1) == 0)
        def _():
            o_ref[...] = jnp.zeros_like(o_ref)

        o_ref[...] += _tn(x_ref[...], y_ref[...])

    return pl.pallas_call(
        body, name=name, grid=(m // bm, t // bt),
        in_specs=[pl.BlockSpec((bt, bm), lambda i, k: (k, i)), pl.BlockSpec((bt, n), lambda i, k: (k, 0))],
        out_specs=pl.BlockSpec((bm, n), lambda i, k: (i, 0)),
        out_shape=jax.ShapeDtypeStruct((m, n), F32),
        compiler_params=_params(("arbitrary", "arbitrary")),
    )(x, y)


def head_fwd_bwd(h, g, tgt):
    t, d = h.shape

    def body(h_ref, g_ref, t_ref, loss_ref, dh_ref, dg_ref):
        i = pl.program_id(0)
        xh, r = _rms_stats(h_ref[...])
        gv = g_ref[...]
        valid = (i > 0).astype(F32)
        e = (xh * gv - t_ref[...]) * valid
        dx, dg = _rms_bwd(e * (1.0 / d), xh, r, gv)
        dh_ref[...] = dx

        @pl.when(i == 0)
        def _():
            dg_ref[...] = jnp.zeros_like(dg_ref)
            loss_ref[...] = jnp.zeros_like(loss_ref)

        dg_ref[...] += dg
        loss_ref[...] += jnp.sum(e * e) * (0.5 / d)

    return pl.pallas_call(
        body, name="head", grid=(t // BLK,),
        in_specs=[_row_spec(BLK, d), _acc_spec((1, d)),
                  pl.BlockSpec((BLK, d), lambda i: (jnp.maximum(i - 1, 0), 0))],
        out_specs=[_acc_spec((1, 128)), _row_spec(BLK, d), _acc_spec((1, d))],
        out_shape=[jax.ShapeDtypeStruct((1, 128), F32), jax.ShapeDtypeStruct((t, d), F32),
                   jax.ShapeDtypeStruct((1, d), F32)],
        compiler_params=_params(("arbitrary",)),
    )(h, g, tgt)


def rope_tables(t):
    pos = jnp.arange(t, dtype=F32) - PAD_FRONT
    inv_freq = ROPE_THETA ** (-jnp.arange(0, ROT_DIM, 2, dtype=F32) / ROT_DIM)
    ang = pos[:, None] * inv_freq[None, :]
    cos, sin = jnp.cos(ang), jnp.sin(ang)
    ones = jnp.ones((t, HEAD_DIM - ROT_DIM), F32)
    cos_h = jnp.concatenate([cos, cos, ones], axis=1)
    sin_h = jnp.concatenate([-sin, sin, 0.0 * ones], axis=1)
    return jnp.concatenate([cos_h, cos_h], axis=1), jnp.concatenate([sin_h, sin_h], axis=1)


def _swap_halves(x):
    n = x.shape[1]
    lane = lax.broadcasted_iota(jnp.int32, x.shape, 1)
    return jnp.where(lane % HEAD_DIM < ROT_DIM // 2, pltpu.roll(x, n - ROT_DIM // 2, 1), pltpu.roll(x, ROT_DIM // 2, 1))


def _rope(x, cos_t, sin_t, sign):
    return x * cos_t + sign * (_swap_halves(x) * sin_t)


def win_fwd(h, g, win_t, cos_t, sin_t, name):
    t, d = h.shape
    tm = TOKEN_TILE

    def body(h_ref, g_ref, w_ref, c_ref, s_ref, n_ref, qkv_ref, u_ref, gates_ref):
        xh, _ = _rms_stats(h_ref[...])
        n = (xh * g_ref[...]).astype(BF16)
        n_ref[...] = n
        z = _nt(n, w_ref[...])
        c, s = c_ref[...], s_ref[...]
        for j in range((ATTN_WIDTH + KV_WIDTH) // 128):
            qkv_ref[:, j * 128:(j + 1) * 128] = _rope(z[:, j * 128:(j + 1) * 128], c, s, 1.0).astype(BF16)
        qkv_ref[:, ATTN_WIDTH + KV_WIDTH:QKV_WIDTH] = z[:, ATTN_WIDTH + KV_WIDTH:QKV_WIDTH].astype(BF16)
        u_ref[...] = z[:, QKV_WIDTH:QKV_WIDTH + SSM_WIDTH]
        gates_ref[...] = z[:, QKV_WIDTH + SSM_WIDTH:]

    return pl.pallas_call(
        body, name=name, grid=(t // tm,),
        in_specs=[_row_spec(tm, d), _acc_spec((1, d)), _VMEM, _row_spec(tm, 128), _row_spec(tm, 128)],
        out_specs=[_row_spec(tm, d), _row_spec(tm, QKV_WIDTH), _row_spec(tm, SSM_WIDTH), _row_spec(tm, 2 * d)],
        out_shape=[jax.ShapeDtypeStruct((t, d), BF16), jax.ShapeDtypeStruct((t, QKV_WIDTH), BF16),
                   jax.ShapeDtypeStruct((t, SSM_WIDTH), F32), jax.ShapeDtypeStruct((t, 2 * d), F32)],
        compiler_params=_params(("arbitrary",)),
    )(h, g, win_t, cos_t, sin_t)


def win_bwd(dh, h, g, dqkv, du, dgates, win_t, name):
    t, d = h.shape
    tm = TOKEN_TILE

    def body(dh_ref, h_ref, g_ref, dqkv_ref, du_ref, dgt_ref, w_ref, dhi_ref, dg_ref):
        dn = (_nn(dqkv_ref[...], w_ref[0:QKV_WIDTH, :])
              + _nn(du_ref[...], w_ref[QKV_WIDTH:QKV_WIDTH + SSM_WIDTH, :])
              + _nn(dgt_ref[...], w_ref[QKV_WIDTH + SSM_WIDTH:, :]))
        xh, r = _rms_stats(h_ref[...])
        dx, dg = _rms_bwd(dn, xh, r, g_ref[...])
        dhi_ref[...] = dh_ref[...] + dx

        @pl.when(pl.program_id(0) == 0)
        def _():
            dg_ref[...] = jnp.zeros_like(dg_ref)

        dg_ref[...] += dg

    return pl.pallas_call(
        body, name=name, grid=(t // tm,),
        in_specs=[_row_spec(tm, d), _row_spec(tm, d), _acc_spec((1, d)), _row_spec(tm, QKV_WIDTH),
                  _row_spec(tm, SSM_WIDTH), _row_spec(tm, 2 * d), _VMEM],
        out_specs=[_row_spec(tm, d), _acc_spec((1, d))],
        out_shape=[jax.ShapeDtypeStruct((t, d), F32), jax.ShapeDtypeStruct((1, d), F32)],
        compiler_params=_params(("arbitrary",)),
    )(dh, h, g, dqkv, du, dgates, win_t)


def _attn_mask(blk):
    q_pos = blk * BLK + lax.broadcasted_iota(jnp.int32, (BLK, 3 * BLK), 0) - PAD_FRONT
    col = lax.broadcasted_iota(jnp.int32, (BLK, 3 * BLK), 1)
    part = col // BLK
    k_pos = jnp.where(part == 0, col, (blk + part - 2) * BLK + (col - part * BLK)) - PAD_FRONT
    dist = q_pos - k_pos
    meta_ok = (part == 0) & (k_pos >= 0) & (dist >= 0)
    band_ok = (part > 0) & (k_pos >= N_META) & (dist >= 0) & (dist < WINDOW)
    return meta_ok | band_ok


def _head_halves(x128, kv):
    x = x128.astype(F32)
    lane = lax.broadcasted_iota(jnp.int32, x.shape, 1)
    swapped = pltpu.roll(x, HEAD_DIM, 1)
    lo, hi = (x, swapped) if kv == 0 else (swapped, x)
    return jnp.where(lane < HEAD_DIM, lo, 0.0).astype(BF16), jnp.where(lane >= HEAD_DIM, hi, 0.0).astype(BF16)


def _gather_keys(meta_ref, prev_ref, cur_ref, lo):
    return jnp.concatenate([meta_ref[:, lo:lo + 128], prev_ref[:, lo:lo + 128], cur_ref[:, lo:lo + 128]], axis=0)


def _softmax_with_sink(s, mask, sink):
    s = jnp.where(mask, s * (HEAD_DIM ** -0.5), NEG_INF)
    m = jnp.maximum(jnp.max(s, axis=-1, keepdims=True), sink)
    p = jnp.exp(s - m)
    p_sink = jnp.exp(sink - m)
    inv = 1.0 / (jnp.sum(p, axis=-1, keepdims=True) + p_sink)
    return p * inv, p_sink * inv


def attn_fwd(qkv, sinks, name):
    t = qkv.shape[0]
    nb = t // BLK

    def body(sink_ref, meta_ref, prev_ref, cur_ref, o_ref):
        blk = pl.program_id(0)
        mask = _attn_mask(blk)
        k128 = _gather_keys(meta_ref, prev_ref, cur_ref, ATTN_WIDTH)
        v128 = _gather_keys(meta_ref, prev_ref, cur_ref, ATTN_WIDTH + KV_WIDTH)
        for kv in range(2):
            k_lo, k_hi = _head_halves(k128, kv)
            v_lo, v_hi = _head_halves(v128, kv)
            for pair in range(2):
                lanes = slice((2 * kv + pair) * 128, (2 * kv + pair + 1) * 128)
                q128 = cur_ref[:, lanes]
                head = 4 * kv + 2 * pair
                p_a, _ = _softmax_with_sink(_nt(q128, k_lo), mask, sink_ref[0, head])
                p_b, _ = _softmax_with_sink(_nt(q128, k_hi), mask, sink_ref[0, head + 1])
                o_ref[:, lanes] = (_nn(p_a.astype(BF16), v_lo) + _nn(p_b.astype(BF16), v_hi)).astype(BF16)

    blk_spec = lambda f: pl.BlockSpec((BLK, QKV_WIDTH), f)
    return pl.pallas_call(
        body, name=name, grid=(nb,),
        in_specs=[_SMEM, blk_spec(lambda i: (0, 0)), blk_spec(lambda i: (jnp.maximum(i - 1, 0), 0)),
                  blk_spec(lambda i: (i, 0))],
        out_specs=_row_spec(BLK, ATTN_WIDTH),
        out_shape=jax.ShapeDtypeStruct((t, ATTN_WIDTH), BF16),
        compiler_params=_params(("arbitrary",)),
    )(sinks, qkv, qkv, qkv)


def attn_bwd(qkv, do, sinks, cos_t, sin_t, name):
    t = qkv.shape[0]
    nb = t // BLK

    def body(sink_ref, meta_ref, prev_ref, cur_ref, do_ref, c_ref, s_ref, dqkv_ref, dsink_ref, carry_ref, macc_ref):
        step = pl.program_id(0)
        blk = nb - 1 - step

        @pl.when(step == 0)
        def _():
            dsink_ref[...] = jnp.zeros_like(dsink_ref)
            carry_ref[...] = jnp.zeros_like(carry_ref)
            macc_ref[...] = jnp.zeros_like(macc_ref)

        mask = _attn_mask(blk)
        lane = lax.broadcasted_iota(jnp.int32, (3 * BLK, 128), 1)
        k128 = _gather_keys(meta_ref, prev_ref, cur_ref, ATTN_WIDTH)
        v128 = _gather_keys(meta_ref, prev_ref, cur_ref, ATTN_WIDTH + KV_WIDTH)
        cos_b, sin_b = c_ref[...], s_ref[...]
        dk_heads, dv_heads = [], []
        for kv in range(2):
            k_lo, k_hi = _head_halves(k128, kv)
            v_lo, v_hi = _head_halves(v128, kv)
            dk_acc = jnp.zeros((3 * BLK, 128), F32)
            dv_acc = jnp.zeros((3 * BLK, 128), F32)
            for pair in range(2):
                lanes = slice((2 * kv + pair) * 128, (2 * kv + pair + 1) * 128)
                q128 = cur_ref[:, lanes]
                do128 = do_ref[:, lanes]
                head = 4 * kv + 2 * pair
                ds_pair, p_pair = [], []
                for half, (k_h, v_h) in enumerate(((k_lo, v_lo), (k_hi, v_hi))):
                    p, p_sink = _softmax_with_sink(_nt(q128, k_h), mask, sink_ref[0, head + half])
                    dp = _nt(do128, v_h)
                    dsum = jnp.sum(p * dp, axis=-1, keepdims=True)
                    ds_pair.append((p * (dp - dsum) * (HEAD_DIM ** -0.5)).astype(BF16))
                    p_pair.append(p.astype(BF16))
                    dsink = -jnp.sum(p_sink * dsum, axis=0, keepdims=True)
                    dsink_ref[head + half:head + half + 1, :] += jnp.broadcast_to(dsink, (1, 128))
                dq = _nn(ds_pair[0], k_lo) + _nn(ds_pair[1], k_hi)
                dqkv_ref[:, lanes] = _rope(dq, cos_b, sin_b, -1.0).astype(BF16)
                dk_acc += jnp.where(lane < HEAD_DIM, _tn(ds_pair[0], q128), _tn(ds_pair[1], q128))
                dv_acc += jnp.where(lane < HEAD_DIM, _tn(p_pair[0], do128), _tn(p_pair[1], do128))
            dk_heads.append(dk_acc + pltpu.roll(dk_acc, HEAD_DIM, 1))
            dv_heads.append(dv_acc + pltpu.roll(dv_acc, HEAD_DIM, 1))
        dkv = jnp.concatenate([jnp.where(lane < HEAD_DIM, dk_heads[0], dk_heads[1]),
                               jnp.where(lane < HEAD_DIM, dv_heads[0], dv_heads[1])], axis=1)
        macc_ref[...] += dkv[0:BLK]
        is_last = (blk == 0).astype(F32)
        mine = dkv[2 * BLK:3 * BLK] + carry_ref[...] + is_last * macc_ref[...]
        carry_ref[...] = dkv[BLK:2 * BLK]
        dqkv_ref[:, ATTN_WIDTH:ATTN_WIDTH + KV_WIDTH] = _rope(mine[:, 0:128], cos_b, sin_b, -1.0).astype(BF16)
        dqkv_ref[:, ATTN_WIDTH + KV_WIDTH:QKV_WIDTH] = mine[:, 128:256].astype(BF16)

    rev = lambda i: nb - 1 - i
    blk_spec = lambda f: pl.BlockSpec((BLK, QKV_WIDTH), f)
    return pl.pallas_call(
        body, name=name, grid=(nb,),
        in_specs=[_SMEM, blk_spec(lambda i: (0, 0)), blk_spec(lambda i: (jnp.maximum(rev(i) - 1, 0), 0)),
                  blk_spec(lambda i: (rev(i), 0)), pl.BlockSpec((BLK, ATTN_WIDTH), lambda i: (rev(i), 0)),
                  pl.BlockSpec((BLK, 128), lambda i: (rev(i), 0)), pl.BlockSpec((BLK, 128), lambda i: (rev(i), 0))],
        out_specs=[pl.BlockSpec((BLK, QKV_WIDTH), lambda i: (rev(i), 0)), _acc_spec((N_Q_HEADS, 128))],
        out_shape=[jax.ShapeDtypeStruct((t, QKV_WIDTH), BF16), jax.ShapeDtypeStruct((N_Q_HEADS, 128), F32)],
        scratch_shapes=[pltpu.VMEM((BLK, 256), F32), pltpu.VMEM((BLK, 256), F32)],
        compiler_params=_params(("arbitrary",)),
    )(sinks, qkv, qkv, qkv, do, cos_t, sin_t)


def _cmul(ar, ai, br, bi):
    return ar * br - ai * bi, ar * bi + ai * br


def ssm_prep(a_re, a_im, log_dt, b_re_t, b_im_t, name):
    def body(ar_ref, ai_ref, ldt_ref, br_ref, bi_ref, lr_ref, li_ref, bbr_ref, bbi_ref):
        ar, ai = ar_ref[...], ai_ref[...]
        dt = jnp.exp(ldt_ref[...])
        mag = jnp.exp(ar * dt)
        lr = mag * jnp.cos(ai * dt)
        li = mag * jnp.sin(ai * dt)
        den = ar * ar + ai * ai
        nr = lr - 1.0
        cr = ((nr * ar + li * ai) / den)[:, None, :]
        ci = ((li * ar - nr * ai) / den)[:, None, :]
        br, bi = br_ref[...], bi_ref[...]
        lr_ref[...] = lr
        li_ref[...] = li
        bbr_ref[...] = cr * br - ci * bi
        bbi_ref[...] = cr * bi + ci * br

    gp = jax.ShapeDtypeStruct(a_re.shape, F32)
    gcp = jax.ShapeDtypeStruct(b_re_t.shape, F32)
    return pl.pallas_call(body, name=name, out_shape=[gp, gp, gcp, gcp],
                          in_specs=[_VMEM] * 5, out_specs=[_VMEM] * 4)(a_re, a_im, log_dt, b_re_t, b_im_t)


def ssm_prep_bwd(a_re, a_im, log_dt, b_re_t, b_im_t, dl_re, dl_im, dbb_re, dbb_im, name):
    def body(ar_ref, ai_ref, ldt_ref, br_ref, bi_ref, dlr_ref, dli_ref, dbbr_ref, dbbi_ref,
             dar_ref, dai_ref, dldt_ref, dbr_ref, dbi_ref):
        ar, ai = ar_ref[...], ai_ref[...]
        dt = jnp.exp(ldt_ref[...])
        mag = jnp.exp(ar * dt)
        lr = mag * jnp.cos(ai * dt)
        li = mag * jnp.sin(ai * dt)
        den = ar * ar + ai * ai
        nr = lr - 1.0
        cr = (nr * ar + li * ai) / den
        ci = (li * ar - nr * ai) / den
        br, bi = br_ref[...], bi_ref[...]
        dbbr, dbbi = dbbr_ref[...], dbbi_ref[...]
        dbr_ref[...] = cr[:, None, :] * dbbr + ci[:, None, :] * dbbi
        dbi_ref[...] = cr[:, None, :] * dbbi - ci[:, None, :] * dbbr
        dcr = jnp.sum(br * dbbr + bi * dbbi, axis=1)
        dci = jnp.sum(br * dbbi - bi * dbbr, axis=1)
        d_num_r = dcr / den
        d_num_i = dci / den
        d_den = -(dcr * cr + dci * ci) / den
        d_lr = dlr_ref[...] + d_num_r * ar - d_num_i * ai
        d_li = dli_ref[...] + d_num_r * ai + d_num_i * ar
        d_ar = d_num_r * nr + d_num_i * li + d_den * 2.0 * ar
        d_ai = d_num_r * li - d_num_i * nr + d_den * 2.0 * ai
        d_mag = (d_lr * lr + d_li * li) / mag
        d_theta = d_li * lr - d_lr * li
        d_ardt = d_mag * mag
        dar_ref[...] = d_ar + d_ardt * dt
        dai_ref[...] = d_ai + d_theta * dt
        d_dt = jnp.sum(d_ardt * ar + d_theta * ai, axis=1, keepdims=True)
        dldt_ref[...] = d_dt * dt

    gp = jax.ShapeDtypeStruct(a_re.shape, F32)
    gcp = jax.ShapeDtypeStruct(b_re_t.shape, F32)
    return pl.pallas_call(body, name=name, out_shape=[gp, gp, jax.ShapeDtypeStruct(log_dt.shape, F32), gcp, gcp],
                          in_specs=[_VMEM] * 9, out_specs=[_VMEM] * 5,
                          )(a_re, a_im, log_dt, b_re_t, b_im_t, dl_re, dl_im, dbb_re, dbb_im)


N_CHUNK = 4
U_CHUNK = SSM_WIDTH // N_CHUNK
H_CHUNK = STATE_WIDTH // N_CHUNK
SUB = 8


def _block_diag_b(bb):
    x = bb.reshape(N_CHUNK, 8, SSM_GROUP, 1, SSM_STATE)
    same = (jnp.arange(8)[:, None] == jnp.arange(8)[None, :])[None, :, None, :, None]
    return jnp.where(same, x, 0.0).reshape(N_CHUNK, U_CHUNK, H_CHUNK)


def _block_diag_c(c):
    x = jnp.swapaxes(c.reshape(N_CHUNK, 8, SSM_GROUP, SSM_STATE), 2, 3)[:, :, :, None, :]
    same = (jnp.arange(8)[:, None] == jnp.arange(8)[None, :])[None, :, None, :, None]
    return jnp.where(same, x, 0.0).reshape(N_CHUNK, H_CHUNK, U_CHUNK)


def _diag_of_b(m):
    x = m.reshape(N_CHUNK, 8, SSM_GROUP, 8, SSM_STATE)
    return jnp.stack([x[:, g, :, g, :] for g in range(8)], axis=1).reshape(SSM_GROUPS, SSM_GROUP, SSM_STATE)


def _diag_of_c(m):
    x = m.reshape(N_CHUNK, 8, SSM_STATE, 8, SSM_GROUP)
    d = jnp.stack([x[:, g, :, g, :] for g in range(8)], axis=1)
    return jnp.swapaxes(d, 2, 3).reshape(SSM_GROUPS, SSM_GROUP, SSM_STATE)


def _lambda_tables(lr, li, reverse):
    p1 = (lr, li)
    p2 = _cmul(*p1, *p1)
    p4 = _cmul(*p2, *p2)
    rows = [p1]
    for _ in range(SUB - 1):
        rows.append(_cmul(*rows[-1], *p1))
    if reverse:
        rows = rows[::-1]
    return p1, p2, p4, (jnp.concatenate([r[0] for r in rows], axis=0), jnp.concatenate([r[1] for r in rows], axis=0))


def _scan8(xr, xi, pows, table, cr, ci, reverse):
    row = lax.broadcasted_iota(jnp.int32, xr.shape, 0)
    for d, (pr, pi) in zip((1, 2, 4), pows):
        if reverse:
            sr, si = pltpu.roll(xr, SUB - d, 0), pltpu.roll(xi, SUB - d, 0)
            keep = row < SUB - d
        else:
            sr, si = pltpu.roll(xr, d, 0), pltpu.roll(xi, d, 0)
            keep = row >= d
        sr = jnp.where(keep, sr, 0.0)
        si = jnp.where(keep, si, 0.0)
        xr, xi = xr + pr * sr - pi * si, xi + pr * si + pi * sr
    tr, ti = table
    return xr + tr * cr - ti * ci, xi + tr * ci + ti * cr


def _gelu_and_grad(y):
    k0 = math.sqrt(2.0 / math.pi)
    inner = k0 * (y + 0.044715 * y * y * y)
    th = jnp.tanh(inner)
    g = 0.5 * y * (1.0 + th)
    dg = 0.5 * (1.0 + th) + 0.5 * y * (1.0 - th * th) * k0 * (1.0 + 3.0 * 0.044715 * y * y)
    return g, dg


def ssm_fwd(u, lam_re, lam_im, bb_re, bb_im, cc_re, cc_im, d_skip, name):
    t = u.shape[0]
    tt = BLK

    def body(u_ref, lr_ref, li_ref, bbr_ref, bbi_ref, ccr_ref, cci_ref, d_ref, yg_ref, hr_ref, hi_ref, cr_ref, ci_ref):
        @pl.when(pl.program_id(0) == 0)
        def _():
            cr_ref[...] = jnp.zeros_like(cr_ref)
            ci_ref[...] = jnp.zeros_like(ci_ref)

        uv = u_ref[...]
        ub = uv.astype(BF16)
        for j in range(N_CHUNK):
            hs = slice(j * H_CHUNK, (j + 1) * H_CHUNK)
            us = slice(j * U_CHUNK, (j + 1) * U_CHUNK)
            hr_ref[:, hs] = _nn(ub[:, us], bbr_ref[j])
            hi_ref[:, hs] = _nn(ub[:, us], bbi_ref[j])
        p1, p2, p4, table = _lambda_tables(lr_ref[...], li_ref[...], False)

        def group(i, carry):
            cr, ci = carry
            rows = pl.ds(pl.multiple_of(i * SUB, SUB), SUB)
            xr, xi = _scan8(hr_ref[rows, :], hi_ref[rows, :], (p1, p2, p4), table, cr, ci, False)
            hr_ref[rows, :] = xr
            hi_ref[rows, :] = xi
            return xr[SUB - 1:SUB, :], xi[SUB - 1:SUB, :]

        cr, ci = lax.fori_loop(0, tt // SUB, group, (cr_ref[...], ci_ref[...]))
        cr_ref[...] = cr
        ci_ref[...] = ci
        for j in range(N_CHUNK):
            hs = slice(j * H_CHUNK, (j + 1) * H_CHUNK)
            us = slice(j * U_CHUNK, (j + 1) * U_CHUNK)
            y = (_nn(hr_ref[:, hs].astype(BF16), ccr_ref[j]) - _nn(hi_ref[:, hs].astype(BF16), cci_ref[j])
                 + d_ref[:, us] * uv[:, us])
            yg_ref[:, us] = _gelu_and_grad(y)[0].astype(BF16)

    return pl.pallas_call(
        body, name=name, grid=(t // tt,),
        in_specs=[_row_spec(tt, SSM_WIDTH), _VMEM, _VMEM, _VMEM, _VMEM, _VMEM, _VMEM, _VMEM],
        out_specs=[_row_spec(tt, SSM_WIDTH), _row_spec(tt, STATE_WIDTH), _row_spec(tt, STATE_WIDTH)],
        out_shape=[jax.ShapeDtypeStruct((t, SSM_WIDTH), BF16), jax.ShapeDtypeStruct((t, STATE_WIDTH), F32),
                   jax.ShapeDtypeStruct((t, STATE_WIDTH), F32)],
        scratch_shapes=[pltpu.VMEM((1, STATE_WIDTH), F32), pltpu.VMEM((1, STATE_WIDTH), F32)],
        compiler_params=_params(("arbitrary",)),
    )(u, lam_re, lam_im, bb_re, bb_im, cc_re, cc_im, d_skip)


def ssm_bwd(dyg, u, h_re, h_im, lam_re, lam_im, bb_re, bb_im, cc_re, cc_im, d_skip, name):
    t = u.shape[0]
    tt = BLK
    nt = t // tt

    def body(dyg_ref, u_ref, hr_ref, hi_ref, lr_ref, li_ref, bbr_ref, bbi_ref, ccr_ref, cci_ref, d_ref,
             du_ref, dlr_ref, dli_ref, dbbr_ref, dbbi_ref, dccr_ref, dcci_ref, dd_ref,
             ar_ref, ai_ref, cr_ref, ci_ref):
        step = pl.program_id(0)
        tile = nt - 1 - step

        @pl.when(step == 0)
        def _():
            for ref in (cr_ref, ci_ref, dlr_ref, dli_ref, dbbr_ref, dbbi_ref, dccr_ref, dcci_ref, dd_ref):
                ref[...] = jnp.zeros_like(ref)

        uv = u_ref[...]
        ub = uv.astype(BF16)
        dskip = d_ref[...]
        dy_chunks = []
        for j in range(N_CHUNK):
            hs = slice(j * H_CHUNK, (j + 1) * H_CHUNK)
            us = slice(j * U_CHUNK, (j + 1) * U_CHUNK)
            hrb = hr_ref[:, hs].astype(BF16)
            hib = hi_ref[:, hs].astype(BF16)
            y = _nn(hrb, ccr_ref[j]) - _nn(hib, cci_ref[j]) + dskip[:, us] * uv[:, us]
            dy = dyg_ref[:, us] * _gelu_and_grad(y)[1]
            dy_chunks.append(dy)
            dyb = dy.astype(BF16)
            dccr_ref[j] += _tn(hrb, dyb)
            dcci_ref[j] -= _tn(hib, dyb)
            ar_ref[:, hs] = _nt(dyb, ccr_ref[j])
            ai_ref[:, hs] = -_nt(dyb, cci_ref[j])
        dy_all = jnp.concatenate(dy_chunks, axis=1)
        dd_ref[...] += jnp.sum(dy_all * uv, axis=0, keepdims=True)

        lr, li = lr_ref[...], li_ref[...]
        p1, p2, p4, table = _lambda_tables(lr, -li, True)
        last_row = lax.broadcasted_iota(jnp.int32, (SUB, STATE_WIDTH), 0) == SUB - 1

        def group(k, carry):
            cr, ci, accr, acci = carry
            i = tt // SUB - 1 - k
            rows = pl.ds(pl.multiple_of(i * SUB, SUB), SUB)
            xr, xi = _scan8(ar_ref[rows, :], ai_ref[rows, :], (p1, p2, p4), table, cr, ci, True)
            ar_ref[rows, :] = xr
            ai_ref[rows, :] = xi
            nr = jnp.where(last_row, cr, pltpu.roll(xr, SUB - 1, 0))
            ni = jnp.where(last_row, ci, pltpu.roll(xi, SUB - 1, 0))
            hr, hi = hr_ref[rows, :], hi_ref[rows, :]
            return xr[0:1, :], xi[0:1, :], accr + nr * hr + ni * hi, acci + ni * hr - nr * hi

        zero = jnp.zeros((SUB, STATE_WIDTH), F32)
        cr, ci, accr, acci = lax.fori_loop(0, tt // SUB, group, (cr_ref[...], ci_ref[...], zero, zero))
        cr_ref[...] = cr
        ci_ref[...] = ci
        dlr_ref[...] += accr
        dli_ref[...] += acci

        row = tile * tt + lax.broadcasted_iota(jnp.int32, (tt, U_CHUNK), 0)
        for j in range(N_CHUNK):
            hs = slice(j * H_CHUNK, (j + 1) * H_CHUNK)
            us = slice(j * U_CHUNK, (j + 1) * U_CHUNK)
            arb = ar_ref[:, hs].astype(BF16)
            aib = ai_ref[:, hs].astype(BF16)
            dbbr_ref[j] += _tn(ub[:, us], arb)
            dbbi_ref[j] += _tn(ub[:, us], aib)
            du = _nt(arb, bbr_ref[j]) + _nt(aib, bbi_ref[j]) + dy_chunks[j] * dskip[:, us]
            du_ref[:, us] = jnp.where(row >= PAD_FRONT, du, 0.0).astype(BF16)

    rev = lambda i: (nt - 1 - i, 0)
    full = lambda shape: pl.BlockSpec(shape, lambda i: (0,) * len(shape))
    return pl.pallas_call(
        body, name=name, grid=(nt,),
        in_specs=[pl.BlockSpec((tt, SSM_WIDTH), rev), pl.BlockSpec((tt, SSM_WIDTH), rev),
                  pl.BlockSpec((tt, STATE_WIDTH), rev), pl.BlockSpec((tt, STATE_WIDTH), rev),
                  _VMEM, _VMEM, _VMEM, _VMEM, _VMEM, _VMEM, _VMEM],
        out_specs=[pl.BlockSpec((tt, SSM_WIDTH), rev), full((SUB, STATE_WIDTH)), full((SUB, STATE_WIDTH)),
                   full((N_CHUNK, U_CHUNK, H_CHUNK)), full((N_CHUNK, U_CHUNK, H_CHUNK)),
                   full((N_CHUNK, H_CHUNK, U_CHUNK)), full((N_CHUNK, H_CHUNK, U_CHUNK)), full((1, SSM_WIDTH))],
        out_shape=[jax.ShapeDtypeStruct((t, SSM_WIDTH), BF16),
                   jax.ShapeDtypeStruct((SUB, STATE_WIDTH), F32), jax.ShapeDtypeStruct((SUB, STATE_WIDTH), F32),
                   jax.ShapeDtypeStruct((N_CHUNK, U_CHUNK, H_CHUNK), F32),
                   jax.ShapeDtypeStruct((N_CHUNK, U_CHUNK, H_CHUNK), F32),
                   jax.ShapeDtypeStruct((N_CHUNK, H_CHUNK, U_CHUNK), F32),
                   jax.ShapeDtypeStruct((N_CHUNK, H_CHUNK, U_CHUNK), F32),
                   jax.ShapeDtypeStruct((1, SSM_WIDTH), F32)],
        scratch_shapes=[pltpu.VMEM((tt, STATE_WIDTH), F32), pltpu.VMEM((tt, STATE_WIDTH), F32),
                        pltpu.VMEM((1, STATE_WIDTH), F32), pltpu.VMEM((1, STATE_WIDTH), F32)],
        compiler_params=_params(("arbitrary",)),
    )(dyg, u, h_re, h_im, lam_re, lam_im, bb_re, bb_im, cc_re, cc_im, d_skip)


def merge_fwd(h, o, yg, gates, wap_t, wv_t, wgg_t, wout, name):
    t, d = h.shape
    tm = TOKEN_TILE

    def body(h_ref, o_ref, yg_ref, gt_ref, wap_ref, wv_ref, wgg_ref, wout_ref, ho_ref, mg_ref, a_ref, sv_ref, sg_ref):
        att = _nt(o_ref[...], wap_ref[...])
        ygv = yg_ref[...]
        sv = _nt(ygv, wv_ref[...])
        sg = _nt(ygv, wgg_ref[...])
        a_ref[...] = att
        sv_ref[...] = sv
        sg_ref[...] = sg
        merged = (jax.nn.sigmoid(gt_ref[:, 0:d]) * att
                  + jax.nn.sigmoid(gt_ref[:, d:2 * d]) * (sv * jax.nn.sigmoid(sg))).astype(BF16)
        mg_ref[...] = merged
        ho_ref[...] = h_ref[...] + _nn(merged, wout_ref[...])

    return pl.pallas_call(
        body, name=name, grid=(t // tm,),
        in_specs=[_row_spec(tm, d), _row_spec(tm, ATTN_WIDTH), _row_spec(tm, SSM_WIDTH), _row_spec(tm, 2 * d),
                  _VMEM, _VMEM, _VMEM, _VMEM],
        out_specs=[_row_spec(tm, d), _row_spec(tm, d), _row_spec(tm, d), _row_spec(tm, d), _row_spec(tm, d)],
        out_shape=[jax.ShapeDtypeStruct((t, d), F32), jax.ShapeDtypeStruct((t, d), BF16),
                   jax.ShapeDtypeStruct((t, d), F32), jax.ShapeDtypeStruct((t, d), F32),
                   jax.ShapeDtypeStruct((t, d), F32)],
        compiler_params=_params(("arbitrary",)),
    )(h, o, yg, gates, wap_t, wv_t, wgg_t, wout)


def merge_bwd(dh, gates, att, sv, sg, wap_t, wv_t, wgg_t, wout, name):
    t, d = dh.shape
    tm = TOKEN_TILE

    def body(dh_ref, gt_ref, a_ref, sv_ref, sg_ref, wap_ref, wv_ref, wgg_ref, wout_ref,
             dgt_ref, da_ref, dsv_ref, dsg_ref, do_ref, dyg_ref, dhb_ref):
        dhb = dh_ref[...].astype(BF16)
        dhb_ref[...] = dhb
        dm = _nt(dhb, wout_ref[...])
        sig_a = jax.nn.sigmoid(gt_ref[:, 0:d])
        sig_s = jax.nn.sigmoid(gt_ref[:, d:2 * d])
        sig_g = jax.nn.sigmoid(sg_ref[...])
        svv = sv_ref[...]
        dgt_ref[:, 0:d] = (dm * a_ref[...] * sig_a * (1.0 - sig_a)).astype(BF16)
        dgt_ref[:, d:2 * d] = (dm * (svv * sig_g) * sig_s * (1.0 - sig_s)).astype(BF16)
        da = (dm * sig_a).astype(BF16)
        d_s = dm * sig_s
        dsv = (d_s * sig_g).astype(BF16)
        dsg = (d_s * svv * sig_g * (1.0 - sig_g)).astype(BF16)
        da_ref[...] = da
        dsv_ref[...] = dsv
        dsg_ref[...] = dsg
        do_ref[...] = _nn(da, wap_ref[...]).astype(BF16)
        dyg_ref[...] = _nn(dsv, wv_ref[...]) + _nn(dsg, wgg_ref[...])

    return pl.pallas_call(
        body, name=name, grid=(t // tm,),
        in_specs=[_row_spec(tm, d), _row_spec(tm, 2 * d), _row_spec(tm, d), _row_spec(tm, d), _row_spec(tm, d),
                  _VMEM, _VMEM, _VMEM, _VMEM],
        out_specs=[_row_spec(tm, 2 * d), _row_spec(tm, d), _row_spec(tm, d), _row_spec(tm, d),
                   _row_spec(tm, ATTN_WIDTH), _row_spec(tm, SSM_WIDTH), _row_spec(tm, d)],
        out_shape=[jax.ShapeDtypeStruct((t, 2 * d), BF16), jax.ShapeDtypeStruct((t, d), BF16),
                   jax.ShapeDtypeStruct((t, d), BF16), jax.ShapeDtypeStruct((t, d), BF16),
                   jax.ShapeDtypeStruct((t, ATTN_WIDTH), BF16), jax.ShapeDtypeStruct((t, SSM_WIDTH), F32),
                   jax.ShapeDtypeStruct((t, d), BF16)],
        compiler_params=_params(("arbitrary",)),
    )(dh, gates, att, sv, sg, wap_t, wv_t, wgg_t, wout)


def adamw(w, g, m, v, name):
    shape = w.shape
    as2d = lambda a: a.reshape(-1, shape[-1]) if a.ndim >= 2 else a.reshape(1, -1)
    w2, g2, m2, v2 = as2d(w), as2d(g), as2d(m), as2d(v)
    rows, cols = w2.shape
    tr = rows
    for cand in (1024, 704, 512, 256):
        if rows > cand and rows % cand == 0:
            tr = cand
            break

    def body(w_ref, g_ref, m_ref, v_ref, d_ref, mo_ref, vo_ref):
        gv = g_ref[...]
        mn = ADAM_B1 * m_ref[...] + (1.0 - ADAM_B1) * gv
        vn = ADAM_B2 * v_ref[...] + (1.0 - ADAM_B2) * (gv * gv)
        m_hat = mn / (1.0 - ADAM_B1 ** ADAM_STEP)
        v_hat = vn / (1.0 - ADAM_B2 ** ADAM_STEP)
        d_ref[...] = -ADAM_LR * (m_hat / (jnp.sqrt(v_hat) + ADAM_EPS) + ADAM_WD * w_ref[...])
        mo_ref[...] = mn
        vo_ref[...] = vn

    spec = _row_spec(tr, cols)
    out = jax.ShapeDtypeStruct((rows, cols), F32)
    d, mn, vn = pl.pallas_call(
        body, name=name, grid=(rows // tr,), in_specs=[spec] * 4, out_specs=[spec] * 3, out_shape=[out] * 3,
        compiler_params=_params(("arbitrary",)),
    )(w2, g2, m2, v2)
    return d.reshape(shape), mn.reshape(shape), vn.reshape(shape)


def _my_index():
    return 4 * lax.axis_index("x") + 2 * lax.axis_index("y") + lax.axis_index("c")


def _peer(p):
    return (lax.axis_index("x") ^ ((p >> 2) & 1), lax.axis_index("y") ^ ((p >> 1) & 1), lax.axis_index("c") ^ (p & 1))


def all_gather(parts, name):
    n = len(parts)

    def body(*refs):
        srcs, dsts = refs[:n], refs[n:2 * n]
        send_sems, recv_sems, local_sems = refs[2 * n:]
        me = _my_index()
        local = [pltpu.make_async_copy(srcs[k], dsts[k].at[me], local_sems.at[k]) for k in range(n)]
        for cp in local:
            cp.start()
        copies = []
        for p in range(1, N_DEV):
            for k in range(n):
                copies.append(pltpu.make_async_remote_copy(
                    src_ref=srcs[k], dst_ref=dsts[k].at[me], send_sem=send_sems.at[p - 1, k],
                    recv_sem=recv_sems.at[p - 1, k], device_id=_peer(p), device_id_type=MESH))
        for cp in copies:
            cp.start()
        for cp in copies:
            cp.wait()
        for cp in local:
            cp.wait()

    return pl.pallas_call(
        body, name=name,
        in_specs=[_ANY] * n, out_specs=[_ANY] * n,
        out_shape=[jax.ShapeDtypeStruct((N_DEV,) + p.shape, p.dtype) for p in parts],
        scratch_shapes=[pltpu.SemaphoreType.DMA((N_DEV - 1, n)), pltpu.SemaphoreType.DMA((N_DEV - 1, n)),
                        pltpu.SemaphoreType.DMA((n,))],
    )(*parts)


def exchange_chunks(chunks):
    def body(src, dst, send_sems, recv_sems, local_sem):
        me = _my_index()
        local = pltpu.make_async_copy(src.at[me], dst.at[me], local_sem)
        local.start()
        copies = []
        for p in range(1, N_DEV):
            peer = me ^ p
            copies.append(pltpu.make_async_remote_copy(
                src_ref=src.at[peer], dst_ref=dst.at[me], send_sem=send_sems.at[p - 1],
                recv_sem=recv_sems.at[p - 1], device_id=_peer(p), device_id_type=MESH))
        for cp in copies:
            cp.start()
        for cp in copies:
            cp.wait()
        local.wait()

    return pl.pallas_call(
        body, name="exchange_chunks", in_specs=[_ANY], out_specs=_ANY,
        out_shape=jax.ShapeDtypeStruct(chunks.shape, chunks.dtype),
        scratch_shapes=[pltpu.SemaphoreType.DMA((N_DEV - 1,)), pltpu.SemaphoreType.DMA((N_DEV - 1,)),
                        pltpu.SemaphoreType.DMA],
    )(chunks)


def sum_slots(slots, name):
    _, rows, cols = slots.shape
    tr = rows
    for cand in (256, 128, 64, 32, 16, 8):
        if rows > cand and rows % cand == 0:
            tr = cand
            break

    def body(s_ref, o_ref):
        acc = s_ref[0]
        for j in range(1, N_DEV):
            acc = acc + s_ref[j]
        o_ref[...] = acc

    return pl.pallas_call(
        body, name=name, grid=(rows // tr,),
        in_specs=[pl.BlockSpec((N_DEV, tr, cols), lambda i: (0, i, 0))], out_specs=_row_spec(tr, cols),
        out_shape=jax.ShapeDtypeStruct((rows, cols), F32), compiler_params=_params(("arbitrary",)),
    )(slots)


BIG_T = ("ffn1_w_gate", "ffn1_w_up", "w_in", "ffn2_w_gate", "ffn2_w_up")
BIG_N = ("ffn1_w_down", "w_out", "ffn2_w_down")
HALF_T = ("w_attn_proj", "w_glu_v", "w_glu_g")
WIDE = ("ffn1_w_gate", "ffn1_w_up", "ffn1_w_down", "w_in", "w_out", "ffn2_w_gate", "ffn2_w_up", "ffn2_w_down")
SMALL = ("ffn1_norm", "mix_norm", "attn_sinks", "ssm_a_re", "ssm_a_im", "ssm_log_dt", "ssm_b_re", "ssm_b_im",
         "ssm_c_re", "ssm_c_im", "ssm_d", "ffn2_norm", "final_norm")


def _shard_rows(name, shard):
    if name in BIG_N:
        return shard
    return jnp.swapaxes(shard, 1, 2)


def local_step(x, tgt, meta, w, small):
    seq, d = x.shape
    t = PAD_FRONT + N_META + seq
    h = jnp.concatenate([jnp.zeros((PAD_FRONT, d), F32), meta, x], axis=0)
    cos_t, sin_t = rope_tables(t)
    row = lambda a: a.reshape(1, -1)
    saved = []
    for i in range(DEPTH):
        s = {}
        s["h0"] = h
        h, s["n1"], s["a1"], s["b1"] = ffn_fwd(h, row(small["ffn1_norm"][i]), w["ffn1_w_gate"][i], w["ffn1_w_up"][i],
                                               w["ffn1_w_down"][i], f"ffn1_fwd_{i}")
        s["h1"] = h
        s["n2"], s["qkv"], s["u"], s["gates"] = win_fwd(h, row(small["mix_norm"][i]), w["w_in"][i], cos_t, sin_t,
                                                        f"win_fwd_{i}")
        b_re_t = jnp.swapaxes(small["ssm_b_re"][i], 1, 2)
        b_im_t = jnp.swapaxes(small["ssm_b_im"][i], 1, 2)
        s["b_t"] = (b_re_t, b_im_t)
        lam_re, lam_im, bbar_re, bbar_im = ssm_prep(small["ssm_a_re"][i], small["ssm_a_im"][i],
                                                    small["ssm_log_dt"][i].reshape(-1, 1), b_re_t, b_im_t, f"ssm_prep_{i}")
        s["ssm"] = (row(lam_re), row(lam_im), _block_diag_b(bbar_re).astype(BF16), _block_diag_b(bbar_im).astype(BF16),
                    _block_diag_c(small["ssm_c_re"][i]).astype(BF16), _block_diag_c(small["ssm_c_im"][i]).astype(BF16),
                    row(small["ssm_d"][i]))
        s["yg"], s["h_re"], s["h_im"] = ssm_fwd(s["u"], *s["ssm"], f"ssm_fwd_{i}")
        s["o"] = attn_fwd(s["qkv"], row(small["attn_sinks"][i]), f"attn_fwd_{i}")
        h, s["merged"], s["att"], s["sv"], s["sg"] = merge_fwd(
            h, s["o"], s["yg"], s["gates"], w["w_attn_proj"][i], w["w_glu_v"][i], w["w_glu_g"][i], w["w_out"][i],
            f"merge_fwd_{i}")
        s["h2"] = h
        h, s["n3"], s["a3"], s["b3"] = ffn_fwd(h, row(small["ffn2_norm"][i]), w["ffn2_w_gate"][i], w["ffn2_w_up"][i],
                                               w["ffn2_w_down"][i], f"ffn2_fwd_{i}")
        saved.append(s)

    loss, dh, d_final = head_fwd_bwd(h, row(small["final_norm"]), tgt)
    gw = {k: [None] * DEPTH for k in WIDE + HALF_T}
    gs = {k: [None] * DEPTH for k in SMALL if k != "final_norm"}
    for i in reversed(range(DEPTH)):
        s = saved[i]
        dh, da, db, sact, dhb, dg = ffn_bwd(dh, s["h2"], row(small["ffn2_norm"][i]), s["a3"], s["b3"], w["ffn2_w_gate"][i],
                                            w["ffn2_w_up"][i], w["ffn2_w_down"][i], f"ffn2_bwd_{i}")
        gs["ffn2_norm"][i] = dg[0]
        gw["ffn2_w_gate"][i] = tn_matmul(da, s["n3"], f"ffn2_dwg_{i}")
        gw["ffn2_w_up"][i] = tn_matmul(db, s["n3"], f"ffn2_dwu_{i}")
        gw["ffn2_w_down"][i] = tn_matmul(sact, dhb, f"ffn2_dwd_{i}")

        dgates, datt, dsv, dsg, do, dyg, dhb = merge_bwd(dh, s["gates"], s["att"], s["sv"], s["sg"], w["w_attn_proj"][i],
                                                         w["w_glu_v"][i], w["w_glu_g"][i], w["w_out"][i], f"merge_bwd_{i}")
        gw["w_out"][i] = tn_matmul(s["merged"], dhb, f"dwout_{i}")
        gw["w_attn_proj"][i] = tn_matmul(datt, s["o"], f"dwap_{i}")
        gw["w_glu_v"][i] = tn_matmul(dsv, s["yg"], f"dwv_{i}")
        gw["w_glu_g"][i] = tn_matmul(dsg, s["yg"], f"dwgg_{i}")
        dqkv, dsink = attn_bwd(s["qkv"], do, row(small["attn_sinks"][i]), cos_t, sin_t, f"attn_bwd_{i}")
        gs["attn_sinks"][i] = dsink[:, 0]
        du, dl_re, dl_im, dbb_re, dbb_im, dcc_re, dcc_im, dd = ssm_bwd(dyg, s["u"], s["h_re"], s["h_im"], *s["ssm"],
                                                                      f"ssm_bwd_{i}")
        fold = lambda a: jnp.sum(a, axis=0).reshape(SSM_GROUPS, SSM_STATE)
        da_re, da_im, dldt, db_re_t, db_im_t = ssm_prep_bwd(
            small["ssm_a_re"][i], small["ssm_a_im"][i], small["ssm_log_dt"][i].reshape(-1, 1), *s["b_t"],
            fold(dl_re), fold(dl_im), _diag_of_b(dbb_re), _diag_of_b(dbb_im), f"ssm_prep_bwd_{i}")
        gs["ssm_a_re"][i], gs["ssm_a_im"][i], gs["ssm_log_dt"][i] = da_re, da_im, dldt[:, 0]
        gs["ssm_b_re"][i], gs["ssm_b_im"][i] = jnp.swapaxes(db_re_t, 1, 2), jnp.swapaxes(db_im_t, 1, 2)
        gs["ssm_c_re"][i], gs["ssm_c_im"][i] = _diag_of_c(dcc_re), _diag_of_c(dcc_im)
        gs["ssm_d"][i] = dd[0]
        gw["w_in"][i] = jnp.concatenate([tn_matmul(dqkv, s["n2"], f"dwin_qkv_{i}"), tn_matmul(du, s["n2"], f"dwin_u_{i}"),
                                         tn_matmul(dgates, s["n2"], f"dwin_g_{i}")], axis=0)
        dh, dg = win_bwd(dh, s["h1"], row(small["mix_norm"][i]), dqkv, du, dgates, w["w_in"][i], f"win_bwd_{i}")
        gs["mix_norm"][i] = dg[0]

        dh, da, db, sact, dhb, dg = ffn_bwd(dh, s["h0"], row(small["ffn1_norm"][i]), s["a1"], s["b1"], w["ffn1_w_gate"][i],
                                            w["ffn1_w_up"][i], w["ffn1_w_down"][i], f"ffn1_bwd_{i}")
        gs["ffn1_norm"][i] = dg[0]
        gw["ffn1_w_gate"][i] = tn_matmul(da, s["n1"], f"ffn1_dwg_{i}")
        gw["ffn1_w_up"][i] = tn_matmul(db, s["n1"], f"ffn1_dwu_{i}")
        gw["ffn1_w_down"][i] = tn_matmul(sact, dhb, f"ffn1_dwd_{i}")

    gs = {k: jnp.stack(v) for k, v in gs.items()}
    gs["final_norm"] = d_final[0]
    gw = {k: jnp.stack(v) for k, v in gw.items()}
    return loss[0, 0], dh[PAD_FRONT + N_META:], dh[PAD_FRONT:PAD_FRONT + N_META], gw, gs


def _pack_rows(arrays, cols):
    flat = jnp.concatenate([a.reshape(-1) for a in arrays])
    rows = -(-flat.shape[0] // cols)
    rows = -(-rows // 8) * 8
    return jnp.pad(flat, (0, rows * cols - flat.shape[0])).reshape(rows, cols)


def _unpack_rows(packed, shapes):
    flat = packed.reshape(-1)
    out, off = [], 0
    for shp in shapes:
        n = math.prod(shp)
        out.append(flat[off:off + n].reshape(shp))
        off += n
    return out


def kernel(x, meta_tokens, ffn1_norm, ffn1_w_gate, ffn1_w_up, ffn1_w_down, mix_norm, w_in, attn_sinks, ssm_a_re, ssm_a_im, ssm_log_dt, ssm_b_re, ssm_b_im, ssm_c_re, ssm_c_im, ssm_d, w_attn_proj, w_glu_v, w_glu_g, w_out, ffn2_norm, ffn2_w_gate, ffn2_w_up, ffn2_w_down, final_norm, loss_target, m_meta_tokens, m_ffn1_norm, m_ffn1_w_gate, m_ffn1_w_up, m_ffn1_w_down, m_mix_norm, m_w_in, m_attn_sinks, m_ssm_a_re, m_ssm_a_im, m_ssm_log_dt, m_ssm_b_re, m_ssm_b_im, m_ssm_c_re, m_ssm_c_im, m_ssm_d, m_w_attn_proj, m_w_glu_v, m_w_glu_g, m_w_out, m_ffn2_norm, m_ffn2_w_gate, m_ffn2_w_up, m_ffn2_w_down, m_final_norm, v_meta_tokens, v_ffn1_norm, v_ffn1_w_gate, v_ffn1_w_up, v_ffn1_w_down, v_mix_norm, v_w_in, v_attn_sinks, v_ssm_a_re, v_ssm_a_im, v_ssm_log_dt, v_ssm_b_re, v_ssm_b_im, v_ssm_c_re, v_ssm_c_im, v_ssm_d, v_w_attn_proj, v_w_glu_v, v_w_glu_g, v_w_out, v_ffn2_norm, v_ffn2_w_gate, v_ffn2_w_up, v_ffn2_w_down, v_final_norm):
    names = ("meta_tokens", "ffn1_norm", "ffn1_w_gate", "ffn1_w_up", "ffn1_w_down", "mix_norm", "w_in", "attn_sinks",
             "ssm_a_re", "ssm_a_im", "ssm_log_dt", "ssm_b_re", "ssm_b_im", "ssm_c_re", "ssm_c_im", "ssm_d",
             "w_attn_proj", "w_glu_v", "w_glu_g", "w_out", "ffn2_norm", "ffn2_w_gate", "ffn2_w_up", "ffn2_w_down",
             "final_norm")
    weights = dict(zip(names, (meta_tokens, ffn1_norm, ffn1_w_gate, ffn1_w_up, ffn1_w_down, mix_norm, w_in, attn_sinks, ssm_a_re, ssm_a_im, ssm_log_dt, ssm_b_re, ssm_b_im, ssm_c_re, ssm_c_im, ssm_d, w_attn_proj, w_glu_v, w_glu_g, w_out, ffn2_norm, ffn2_w_gate, ffn2_w_up, ffn2_w_down, final_norm)))
    moments_m = dict(zip(names, (m_meta_tokens, m_ffn1_norm, m_ffn1_w_gate, m_ffn1_w_up, m_ffn1_w_down, m_mix_norm, m_w_in, m_attn_sinks, m_ssm_a_re, m_ssm_a_im, m_ssm_log_dt, m_ssm_b_re, m_ssm_b_im, m_ssm_c_re, m_ssm_c_im, m_ssm_d, m_w_attn_proj, m_w_glu_v, m_w_glu_g, m_w_out, m_ffn2_norm, m_ffn2_w_gate, m_ffn2_w_up, m_ffn2_w_down, m_final_norm)))
    moments_v = dict(zip(names, (v_meta_tokens, v_ffn1_norm, v_ffn1_w_gate, v_ffn1_w_up, v_ffn1_w_down, v_mix_norm, v_w_in, v_attn_sinks, v_ssm_a_re, v_ssm_a_im, v_ssm_log_dt, v_ssm_b_re, v_ssm_b_im, v_ssm_c_re, v_ssm_c_im, v_ssm_d, v_w_attn_proj, v_w_glu_v, v_w_glu_g, v_w_out, v_ffn2_norm, v_ffn2_w_gate, v_ffn2_w_up, v_ffn2_w_down, v_final_norm)))
    me = _my_index()

    wide_rows = [_shard_rows(k, weights[k]).astype(BF16) for k in WIDE]
    half_rows = [_shard_rows(k, weights[k]).astype(BF16) for k in HALF_T]
    wide_counts = [a.shape[1] for a in wide_rows]
    wide_send = jnp.concatenate([a.reshape(-1, D_MODEL) for a in wide_rows], axis=0)
    half_send = jnp.concatenate([a.reshape(-1, SSM_WIDTH) for a in half_rows], axis=0)
    wide_all, half_all, meta_all = all_gather([wide_send, half_send, meta_tokens], "gather_weights")
    full = {}
    off = 0
    for k, cnt in zip(WIDE, wide_counts):
        piece = wide_all[:, off:off + DEPTH * cnt].reshape(N_DEV, DEPTH, cnt, D_MODEL)
        full[k] = jnp.swapaxes(piece, 0, 1).reshape(DEPTH, N_DEV * cnt, D_MODEL)
        off += DEPTH * cnt
    off = 0
    for k in HALF_T:
        piece = half_all[:, off:off + DEPTH * 128].reshape(N_DEV, DEPTH, 128, SSM_WIDTH)
        full[k] = jnp.swapaxes(piece, 0, 1).reshape(DEPTH, D_MODEL, SSM_WIDTH)
        off += DEPTH * 128
    meta_full = jnp.swapaxes(meta_all, 0, 1).reshape(N_META, D_MODEL)

    small = {k: weights[k] for k in SMALL}
    loss, dx, dmeta, gw, gs = local_step(x[0], loss_target[0], meta_full, full, small)

    wide_chunks = [gw[k].reshape(DEPTH, N_DEV, cnt, D_MODEL) for k, cnt in zip(WIDE, wide_counts)]
    half_chunks = [gw[k].reshape(DEPTH, N_DEV, 64, D_MODEL) for k in HALF_T]
    per_dev = jnp.concatenate([jnp.swapaxes(a, 0, 1).reshape(N_DEV, -1, D_MODEL) for a in wide_chunks + half_chunks],
                              axis=1)
    mine = sum_slots(exchange_chunks(per_dev), "sum_big")
    grads = {}
    off = 0
    for k, cnt in zip(WIDE, wide_counts):
        g = mine[off:off + DEPTH * cnt].reshape(DEPTH, cnt, D_MODEL)
        grads[k] = g if k in BIG_N else jnp.swapaxes(g, 1, 2)
        off += DEPTH * cnt
    for k in HALF_T:
        g = mine[off:off + DEPTH * 64].reshape(DEPTH, 128, SSM_WIDTH)
        grads[k] = jnp.swapaxes(g, 1, 2)
        off += DEPTH * 64

    small_list = [loss.reshape(1), dmeta] + [gs[k] for k in SMALL]
    packed = _pack_rows(small_list, D_MODEL)
    (packed_all,) = all_gather([packed], "gather_small")
    total = sum_slots(packed_all, "sum_small")
    pieces = _unpack_rows(total, [a.shape for a in small_list])
    loss_out = pieces[0][0]
    grads["meta_tokens"] = lax.dynamic_slice_in_dim(pieces[1], me * 128, 128, axis=1)
    for k, p in zip(SMALL, pieces[2:]):
        grads[k] = p

    deltas, new_m, new_v = {}, {}, {}
    for k in names:
        deltas[k], new_m[k], new_v[k] = adamw(weights[k], grads[k], moments_m[k], moments_v[k], f"adamw_{k}")
    return (loss_out, dx[None], *[grads[k] for k in names], *[deltas[k] for k in names],
            *[new_m[k] for k in names], *[new_v[k] for k in names])
```

```python
import functools
import math

import jax
import jax.numpy as jnp
from jax import lax
from jax.experimental import pallas as pl
from jax.experimental.pallas import tpu as pltpu

F32 = jnp.float32
BF16 = jnp.bfloat16

D_MODEL = 1024
DEPTH = 2
N_META = 16
HEAD_DIM = 64
N_Q_HEADS = 8
ATTN_WIDTH = 512
KV_WIDTH = 128
QKV_WIDTH = ATTN_WIDTH + 2 * KV_WIDTH
WINDOW = 128
BLK = 128
ROPE_THETA = 500000.0
ROT_DIM = 16
SSM_WIDTH = 512
SSM_GROUP = 16
SSM_GROUPS = 32
SSM_STATE = 64
STATE_WIDTH = SSM_GROUPS * SSM_STATE
D_FF = 2816
IN_WIDTH = 3328
EPS = 1e-6
NEG_INF = -1e30
PAD_FRONT = (-N_META) % BLK
N_DEV = 8

ADAM_LR = 0.001
ADAM_B1 = 0.9
ADAM_B2 = 0.999
ADAM_EPS = 1e-08
ADAM_WD = 0.01
ADAM_STEP = 10

VMEM_LIMIT = 56 * 1024 * 1024
TOKEN_TILE = 384
_VMEM = pl.BlockSpec(memory_space=pltpu.VMEM)
_SMEM = pl.BlockSpec(memory_space=pltpu.SMEM)
_ANY = pl.BlockSpec(memory_space=pl.ANY)
MESH = pl.DeviceIdType.MESH


def _params(sem=None):
    return pltpu.CompilerParams(dimension_semantics=sem, vmem_limit_bytes=VMEM_LIMIT)


def _nt(a, b):
    return lax.dot_general(a, b, (((1,), (1,)), ((), ())), preferred_element_type=F32)


def _nn(a, b):
    return jnp.dot(a, b, preferred_element_type=F32)


def _tn(a, b):
    return lax.dot_general(a, b, (((0,), (0,)), ((), ())), preferred_element_type=F32)


def _row_spec(tm, width):
    return pl.BlockSpec((tm, width), lambda i: (i, 0))


def _acc_spec(shape):
    return pl.BlockSpec(shape, lambda i: (0,) * len(shape))


def _rms_stats(x):
    r = lax.rsqrt(jnp.mean(x * x, axis=-1, keepdims=True) + EPS)
    return x * r, r


def _rms_bwd(dn, xh, r, g):
    dg = jnp.sum(dn * xh, axis=0, keepdims=True)
    dxh = dn * g
    dx = r * (dxh - xh * jnp.mean(dxh * xh, axis=-1, keepdims=True))
    return dx, dg


def ffn_fwd(h, g, wg_t, wu_t, wd, name):
    t, d = h.shape
    f = wd.shape[0]
    tm = TOKEN_TILE

    def body(h_ref, g_ref, wg_ref, wu_ref, wd_ref, ho_ref, n_ref, a_ref, b_ref):
        x = h_ref[...]
        xh, _ = _rms_stats(x)
        n = (xh * g_ref[...]).astype(BF16)
        n_ref[...] = n
        a = _nt(n, wg_ref[...])
        b = _nt(n, wu_ref[...])
        a_ref[...] = a.astype(BF16)
        b_ref[...] = b.astype(BF16)
        s = (a * jax.nn.sigmoid(a) * b).astype(BF16)
        ho_ref[...] = x + 0.5 * _nn(s, wd_ref[...])

    return pl.pallas_call(
        body, name=name, grid=(t // tm,),
        in_specs=[_row_spec(tm, d), _acc_spec((1, d)), _VMEM, _VMEM, _VMEM],
        out_specs=[_row_spec(tm, d), _row_spec(tm, d), _row_spec(tm, f), _row_spec(tm, f)],
        out_shape=[jax.ShapeDtypeStruct((t, d), F32), jax.ShapeDtypeStruct((t, d), BF16),
                   jax.ShapeDtypeStruct((t, f), BF16), jax.ShapeDtypeStruct((t, f), BF16)],
        compiler_params=_params(("arbitrary",)),
    )(h, g, wg_t, wu_t, wd)


def ffn_bwd(dh, h, g, a, b, wg_t, wu_t, wd, name):
    t, d = h.shape
    f = wd.shape[0]
    tm = TOKEN_TILE // 2

    def body(dh_ref, h_ref, g_ref, a_ref, b_ref, wg_ref, wu_ref, wd_ref,
             dhi_ref, da_ref, db_ref, s_ref, dhb_ref, dg_ref):
        dh_t = dh_ref[...]
        dhb = (0.5 * dh_t).astype(BF16)
        dhb_ref[...] = dhb
        ds = _nt(dhb, wd_ref[...])
        av = a_ref[...].astype(F32)
        bv = b_ref[...].astype(F32)
        sig = jax.nn.sigmoid(av)
        sl = av * sig
        s_ref[...] = (sl * bv).astype(BF16)
        da = (ds * bv * (sig * (1.0 + av * (1.0 - sig)))).astype(BF16)
        db = (ds * sl).astype(BF16)
        da_ref[...] = da
        db_ref[...] = db
        dn = _nn(da, wg_ref[...]) + _nn(db, wu_ref[...])
        xh, r = _rms_stats(h_ref[...])
        dx, dg = _rms_bwd(dn, xh, r, g_ref[...])
        dhi_ref[...] = dh_t + dx

        @pl.when(pl.program_id(0) == 0)
        def _():
            dg_ref[...] = jnp.zeros_like(dg_ref)

        dg_ref[...] += dg

    return pl.pallas_call(
        body, name=name, grid=(t // tm,),
        in_specs=[_row_spec(tm, d), _row_spec(tm, d), _acc_spec((1, d)), _row_spec(tm, f), _row_spec(tm, f),
                  _VMEM, _VMEM, _VMEM],
        out_specs=[_row_spec(tm, d), _row_spec(tm, f), _row_spec(tm, f), _row_spec(tm, f), _row_spec(tm, d),
                   _acc_spec((1, d))],
        out_shape=[jax.ShapeDtypeStruct((t, d), F32), jax.ShapeDtypeStruct((t, f), BF16),
                   jax.ShapeDtypeStruct((t, f), BF16), jax.ShapeDtypeStruct((t, f), BF16),
                   jax.ShapeDtypeStruct((t, d), BF16), jax.ShapeDtypeStruct((1, d), F32)],
        compiler_params=_params(("arbitrary",)),
    )(dh, h, g, a, b, wg_t, wu_t, wd)


def _col_tile(m):
    for bm in (1408, 1664, 1024, 768, 512):
        if m % bm == 0:
            return bm
    raise ValueError(m)


def tn_matmul(x, y, name):
    t, m = x.shape
    n = y.shape[1]
    bm = _col_tile(m)
    bt = TOKEN_TILE
    nk = t // bt

    def body(x_ref, y_ref, o_ref, acc_ref):
        k = pl.program_id(1)

        @pl.when(k == 0)
        def _():
            acc_ref[...] = jnp.zeros_like(acc_ref)

        acc_ref[...] += _tn(x_ref[...], y_ref[...])

        @pl.when(k == nk - 1)
        def _():
            o_ref[...] = acc_ref[...].astype(BF16)

    return pl.pallas_call(
        body, name=name, grid=(m // bm, nk),
        in_specs=[pl.BlockSpec((bt, bm), lambda i, k: (k, i)), pl.BlockSpec((bt, n), lambda i, k: (k, 0))],
        out_specs=pl.BlockSpec((bm, n), lambda i, k: (i, 0)),
        out_shape=jax.ShapeDtypeStruct((m, n), BF16),
        scratch_shapes=[pltpu.VMEM((bm, n), F32)],
        compiler_params=_params(("arbitrary", "arbitrary")),
    )(x, y)


def head_fwd_bwd(h, g, tgt):
    t, d = h.shape

    def body(h_ref, g_ref, t_ref, loss_ref, dh_ref, dg_ref):
        i = pl.program_id(0)
        xh, r = _rms_stats(h_ref[...])
        gv = g_ref[...]
        valid = (i > 0).astype(F32)
        e = (xh * gv - t_ref[...]) * valid
        dx, dg = _rms_bwd(e * (1.0 / d), xh, r, gv)
        dh_ref[...] = dx

        @pl.when(i == 0)
        def _():
            dg_ref[...] = jnp.zeros_like(dg_ref)
            loss_ref[...] = jnp.zeros_like(loss_ref)

        dg_ref[...] += dg
        loss_ref[...] += jnp.sum(e * e) * (0.5 / d)

    return pl.pallas_call(
        body, name="head", grid=(t // BLK,),
        in_specs=[_row_spec(BLK, d), _acc_spec((1, d)),
                  pl.BlockSpec((BLK, d), lambda i: (jnp.maximum(i - 1, 0), 0))],
        out_specs=[_acc_spec((1, 128)), _row_spec(BLK, d), _acc_spec((1, d))],
        out_shape=[jax.ShapeDtypeStruct((1, 128), F32), jax.ShapeDtypeStruct((t, d), F32),
                   jax.ShapeDtypeStruct((1, d), F32)],
        compiler_params=_params(("arbitrary",)),
    )(h, g, tgt)


def rope_tables(t):
    pos = jnp.arange(t, dtype=F32) - PAD_FRONT
    inv_freq = ROPE_THETA ** (-jnp.arange(0, ROT_DIM, 2, dtype=F32) / ROT_DIM)
    ang = pos[:, None] * inv_freq[None, :]
    cos, sin = jnp.cos(ang), jnp.sin(ang)
    ones = jnp.ones((t, HEAD_DIM - ROT_DIM), F32)
    cos_h = jnp.concatenate([cos, cos, ones], axis=1)
    sin_h = jnp.concatenate([-sin, sin, 0.0 * ones], axis=1)
    return jnp.concatenate([cos_h, cos_h], axis=1), jnp.concatenate([sin_h, sin_h], axis=1)


def _swap_halves(x):
    n = x.shape[1]
    lane = lax.broadcasted_iota(jnp.int32, x.shape, 1)
    return jnp.where(lane % HEAD_DIM < ROT_DIM // 2, pltpu.roll(x, n - ROT_DIM // 2, 1), pltpu.roll(x, ROT_DIM // 2, 1))


def _rope(x, cos_t, sin_t, sign):
    return x * cos_t + sign * (_swap_halves(x) * sin_t)


def win_fwd(h, g, win_t, cos_t, sin_t, name):
    t, d = h.shape
    tm = TOKEN_TILE

    def body(h_ref, g_ref, w_ref, c_ref, s_ref, n_ref, qkv_ref, u_ref, gates_ref):
        xh, _ = _rms_stats(h_ref[...])
        n = (xh * g_ref[...]).astype(BF16)
        n_ref[...] = n
        z = _nt(n, w_ref[...])
        c, s = c_ref[...], s_ref[...]
        for j in range((ATTN_WIDTH + KV_WIDTH) // 128):
            qkv_ref[:, j * 128:(j + 1) * 128] = _rope(z[:, j * 128:(j + 1) * 128], c, s, 1.0).astype(BF16)
        qkv_ref[:, ATTN_WIDTH + KV_WIDTH:QKV_WIDTH] = z[:, ATTN_WIDTH + KV_WIDTH:QKV_WIDTH].astype(BF16)
        u_ref[...] = z[:, QKV_WIDTH:QKV_WIDTH + SSM_WIDTH]
        gates_ref[...] = z[:, QKV_WIDTH + SSM_WIDTH:]

    return pl.pallas_call(
        body, name=name, grid=(t // tm,),
        in_specs=[_row_spec(tm, d), _acc_spec((1, d)), _VMEM, _row_spec(tm, 128), _row_spec(tm, 128)],
        out_specs=[_row_spec(tm, d), _row_spec(tm, QKV_WIDTH), _row_spec(tm, SSM_WIDTH), _row_spec(tm, 2 * d)],
        out_shape=[jax.ShapeDtypeStruct((t, d), BF16), jax.ShapeDtypeStruct((t, QKV_WIDTH), BF16),
                   jax.ShapeDtypeStruct((t, SSM_WIDTH), F32), jax.ShapeDtypeStruct((t, 2 * d), F32)],
        compiler_params=_params(("arbitrary",)),
    )(h, g, win_t, cos_t, sin_t)


def win_bwd(dh, h, g, dz, win_t, name):
    t, d = h.shape
    tm = TOKEN_TILE

    def body(dh_ref, h_ref, g_ref, dz_ref, w_ref, dhi_ref, dg_ref):
        dn = _nn(dz_ref[...], w_ref[...])
        xh, r = _rms_stats(h_ref[...])
        dx, dg = _rms_bwd(dn, xh, r, g_ref[...])
        dhi_ref[...] = dh_ref[...] + dx

        @pl.when(pl.program_id(0) == 0)
        def _():
            dg_ref[...] = jnp.zeros_like(dg_ref)

        dg_ref[...] += dg

    return pl.pallas_call(
        body, name=name, grid=(t // tm,),
        in_specs=[_row_spec(tm, d), _row_spec(tm, d), _acc_spec((1, d)), _row_spec(tm, IN_WIDTH), _VMEM],
        out_specs=[_row_spec(tm, d), _acc_spec((1, d))],
        out_shape=[jax.ShapeDtypeStruct((t, d), F32), jax.ShapeDtypeStruct((1, d), F32)],
        compiler_params=_params(("arbitrary",)),
    )(dh, h, g, dz, win_t)


def _attn_mask(blk):
    q_pos = blk * BLK + lax.broadcasted_iota(jnp.int32, (BLK, 3 * BLK), 0) - PAD_FRONT
    col = lax.broadcasted_iota(jnp.int32, (BLK, 3 * BLK), 1)
    part = col // BLK
    k_pos = jnp.where(part == 0, col, (blk + part - 2) * BLK + (col - part * BLK)) - PAD_FRONT
    dist = q_pos - k_pos
    meta_ok = (part == 0) & (k_pos >= 0) & (dist >= 0)
    band_ok = (part > 0) & (k_pos >= N_META) & (dist >= 0) & (dist < WINDOW)
    return meta_ok | band_ok


def _head_halves(x128, kv):
    x = x128.astype(F32)
    lane = lax.broadcasted_iota(jnp.int32, x.shape, 1)
    swapped = pltpu.roll(x, HEAD_DIM, 1)
    lo, hi = (x, swapped) if kv == 0 else (swapped, x)
    return jnp.where(lane < HEAD_DIM, lo, 0.0).astype(BF16), jnp.where(lane >= HEAD_DIM, hi, 0.0).astype(BF16)


def _gather_keys(meta_ref, prev_ref, cur_ref, lo):
    return jnp.concatenate([meta_ref[:, lo:lo + 128], prev_ref[:, lo:lo + 128], cur_ref[:, lo:lo + 128]], axis=0)


def _softmax_with_sink(s, mask, sink):
    s = jnp.where(mask, s * (HEAD_DIM ** -0.5), NEG_INF)
    m = jnp.maximum(jnp.max(s, axis=-1, keepdims=True), sink)
    p = jnp.exp(s - m)
    p_sink = jnp.exp(sink - m)
    inv = 1.0 / (jnp.sum(p, axis=-1, keepdims=True) + p_sink)
    return p * inv, p_sink * inv


def attn_fwd(qkv, sinks, name):
    t = qkv.shape[0]
    nb = t // BLK

    def body(sink_ref, meta_ref, prev_ref, cur_ref, o_ref):
        blk = pl.program_id(0)
        mask = _attn_mask(blk)
        k128 = _gather_keys(meta_ref, prev_ref, cur_ref, ATTN_WIDTH)
        v128 = _gather_keys(meta_ref, prev_ref, cur_ref, ATTN_WIDTH + KV_WIDTH)
        for kv in range(2):
            k_lo, k_hi = _head_halves(k128, kv)
            v_lo, v_hi = _head_halves(v128, kv)
            for pair in range(2):
                lanes = slice((2 * kv + pair) * 128, (2 * kv + pair + 1) * 128)
                q128 = cur_ref[:, lanes]
                head = 4 * kv + 2 * pair
                p_a, _ = _softmax_with_sink(_nt(q128, k_lo), mask, sink_ref[0, head])
                p_b, _ = _softmax_with_sink(_nt(q128, k_hi), mask, sink_ref[0, head + 1])
                o_ref[:, lanes] = (_nn(p_a.astype(BF16), v_lo) + _nn(p_b.astype(BF16), v_hi)).astype(BF16)

    blk_spec = lambda f: pl.BlockSpec((BLK, QKV_WIDTH), f)
    return pl.pallas_call(
        body, name=name, grid=(nb,),
        in_specs=[_SMEM, blk_spec(lambda i: (0, 0)), blk_spec(lambda i: (jnp.maximum(i - 1, 0), 0)),
                  blk_spec(lambda i: (i, 0))],
        out_specs=_row_spec(BLK, ATTN_WIDTH),
        out_shape=jax.ShapeDtypeStruct((t, ATTN_WIDTH), BF16),
        compiler_params=_params(("arbitrary",)),
    )(sinks, qkv, qkv, qkv)


def attn_bwd(qkv, do, sinks, cos_t, sin_t, name):
    t = qkv.shape[0]
    nb = t // BLK

    def body(sink_ref, meta_ref, prev_ref, cur_ref, do_ref, c_ref, s_ref, dqkv_ref, dsink_ref, carry_ref, macc_ref):
        step = pl.program_id(0)
        blk = nb - 1 - step

        @pl.when(step == 0)
        def _():
            dsink_ref[...] = jnp.zeros_like(dsink_ref)
            carry_ref[...] = jnp.zeros_like(carry_ref)
            macc_ref[...] = jnp.zeros_like(macc_ref)

        mask = _attn_mask(blk)
        lane = lax.broadcasted_iota(jnp.int32, (3 * BLK, 128), 1)
        k128 = _gather_keys(meta_ref, prev_ref, cur_ref, ATTN_WIDTH)
        v128 = _gather_keys(meta_ref, prev_ref, cur_ref, ATTN_WIDTH + KV_WIDTH)
        cos_b, sin_b = c_ref[...], s_ref[...]
        dk_heads, dv_heads = [], []
        for kv in range(2):
            k_lo, k_hi = _head_halves(k128, kv)
            v_lo, v_hi = _head_halves(v128, kv)
            dk_acc = jnp.zeros((3 * BLK, 128), F32)
            dv_acc = jnp.zeros((3 * BLK, 128), F32)
            for pair in range(2):
                lanes = slice((2 * kv + pair) * 128, (2 * kv + pair + 1) * 128)
                q128 = cur_ref[:, lanes]
                do128 = do_ref[:, lanes]
                head = 4 * kv + 2 * pair
                ds_pair, p_pair = [], []
                for half, (k_h, v_h) in enumerate(((k_lo, v_lo), (k_hi, v_hi))):
                    p, p_sink = _softmax_with_sink(_nt(q128, k_h), mask, sink_ref[0, head + half])
                    dp = _nt(do128, v_h)
                    dsum = jnp.sum(p * dp, axis=-1, keepdims=True)
                    ds_pair.append((p * (dp - dsum) * (HEAD_DIM ** -0.5)).astype(BF16))
                    p_pair.append(p.astype(BF16))
                    dsink = -jnp.sum(p_sink * dsum, axis=0, keepdims=True)
                    dsink_ref[head + half:head + half + 1, :] += jnp.broadcast_to(dsink, (1, 128))
                dq = _nn(ds_pair[0], k_lo) + _nn(ds_pair[1], k_hi)
                dqkv_ref[:, lanes] = _rope(dq, cos_b, sin_b, -1.0).astype(BF16)
                dk_acc += jnp.where(lane < HEAD_DIM, _tn(ds_pair[0], q128), _tn(ds_pair[1], q128))
                dv_acc += jnp.where(lane < HEAD_DIM, _tn(p_pair[0], do128), _tn(p_pair[1], do128))
            dk_heads.append(dk_acc + pltpu.roll(dk_acc, HEAD_DIM, 1))
            dv_heads.append(dv_acc + pltpu.roll(dv_acc, HEAD_DIM, 1))
        dkv = jnp.concatenate([jnp.where(lane < HEAD_DIM, dk_heads[0], dk_heads[1]),
                               jnp.where(lane < HEAD_DIM, dv_heads[0], dv_heads[1])], axis=1)
        macc_ref[...] += dkv[0:BLK]
        is_last = (blk == 0).astype(F32)
        mine = dkv[2 * BLK:3 * BLK] + carry_ref[...] + is_last * macc_ref[...]
        carry_ref[...] = dkv[BLK:2 * BLK]
        dqkv_ref[:, ATTN_WIDTH:ATTN_WIDTH + KV_WIDTH] = _rope(mine[:, 0:128], cos_b, sin_b, -1.0).astype(BF16)
        dqkv_ref[:, ATTN_WIDTH + KV_WIDTH:QKV_WIDTH] = mine[:, 128:256].astype(BF16)

    rev = lambda i: nb - 1 - i
    blk_spec = lambda f: pl.BlockSpec((BLK, QKV_WIDTH), f)
    return pl.pallas_call(
        body, name=name, grid=(nb,),
        in_specs=[_SMEM, blk_spec(lambda i: (0, 0)), blk_spec(lambda i: (jnp.maximum(rev(i) - 1, 0), 0)),
                  blk_spec(lambda i: (rev(i), 0)), pl.BlockSpec((BLK, ATTN_WIDTH), lambda i: (rev(i), 0)),
                  pl.BlockSpec((BLK, 128), lambda i: (rev(i), 0)), pl.BlockSpec((BLK, 128), lambda i: (rev(i), 0))],
        out_specs=[pl.BlockSpec((BLK, QKV_WIDTH), lambda i: (rev(i), 0)), _acc_spec((N_Q_HEADS, 128))],
        out_shape=[jax.ShapeDtypeStruct((t, QKV_WIDTH), BF16), jax.ShapeDtypeStruct((N_Q_HEADS, 128), F32)],
        scratch_shapes=[pltpu.VMEM((BLK, 256), F32), pltpu.VMEM((BLK, 256), F32)],
        compiler_params=_params(("arbitrary",)),
    )(sinks, qkv, qkv, qkv, do, cos_t, sin_t)


def _cmul(ar, ai, br, bi):
    return ar * br - ai * bi, ar * bi + ai * br


def ssm_prep(a_re, a_im, log_dt, b_re_t, b_im_t, name):
    def body(ar_ref, ai_ref, ldt_ref, br_ref, bi_ref, lr_ref, li_ref, bbr_ref, bbi_ref):
        ar, ai = ar_ref[...], ai_ref[...]
        dt = jnp.exp(ldt_ref[...])
        mag = jnp.exp(ar * dt)
        lr = mag * jnp.cos(ai * dt)
        li = mag * jnp.sin(ai * dt)
        den = ar * ar + ai * ai
        nr = lr - 1.0
        cr = ((nr * ar + li * ai) / den)[:, None, :]
        ci = ((li * ar - nr * ai) / den)[:, None, :]
        br, bi = br_ref[...], bi_ref[...]
        lr_ref[...] = lr
        li_ref[...] = li
        bbr_ref[...] = cr * br - ci * bi
        bbi_ref[...] = cr * bi + ci * br

    gp = jax.ShapeDtypeStruct(a_re.shape, F32)
    gcp = jax.ShapeDtypeStruct(b_re_t.shape, F32)
    return pl.pallas_call(body, name=name, out_shape=[gp, gp, gcp, gcp],
                          in_specs=[_VMEM] * 5, out_specs=[_VMEM] * 4)(a_re, a_im, log_dt, b_re_t, b_im_t)


def ssm_prep_bwd(a_re, a_im, log_dt, b_re_t, b_im_t, dl_re, dl_im, dbb_re, dbb_im, name):
    def body(ar_ref, ai_ref, ldt_ref, br_ref, bi_ref, dlr_ref, dli_ref, dbbr_ref, dbbi_ref,
             dar_ref, dai_ref, dldt_ref, dbr_ref, dbi_ref):
        ar, ai = ar_ref[...], ai_ref[...]
        dt = jnp.exp(ldt_ref[...])
        mag = jnp.exp(ar * dt)
        lr = mag * jnp.cos(ai * dt)
        li = mag * jnp.sin(ai * dt)
        den = ar * ar + ai * ai
        nr = lr - 1.0
        cr = (nr * ar + li * ai) / den
        ci = (li * ar - nr * ai) / den
        br, bi = br_ref[...], bi_ref[...]
        dbbr, dbbi = dbbr_ref[...], dbbi_ref[...]
        dbr_ref[...] = cr[:, None, :] * dbbr + ci[:, None, :] * dbbi
        dbi_ref[...] = cr[:, None, :] * dbbi - ci[:, None, :] * dbbr
        dcr = jnp.sum(br * dbbr + bi * dbbi, axis=1)
        dci = jnp.sum(br * dbbi - bi * dbbr, axis=1)
        d_num_r = dcr / den
        d_num_i = dci / den
        d_den = -(dcr * cr + dci * ci) / den
        d_lr = dlr_ref[...] + d_num_r * ar - d_num_i * ai
        d_li = dli_ref[...] + d_num_r * ai + d_num_i * ar
        d_ar = d_num_r * nr + d_num_i * li + d_den * 2.0 * ar
        d_ai = d_num_r * li - d_num_i * nr + d_den * 2.0 * ai
        d_mag = (d_lr * lr + d_li * li) / mag
        d_theta = d_li * lr - d_lr * li
        d_ardt = d_mag * mag
        dar_ref[...] = d_ar + d_ardt * dt
        dai_ref[...] = d_ai + d_theta * dt
        d_dt = jnp.sum(d_ardt * ar + d_theta * ai, axis=1, keepdims=True)
        dldt_ref[...] = d_dt * dt

    gp = jax.ShapeDtypeStruct(a_re.shape, F32)
    gcp = jax.ShapeDtypeStruct(b_re_t.shape, F32)
    return pl.pallas_call(body, name=name, out_shape=[gp, gp, jax.ShapeDtypeStruct(log_dt.shape, F32), gcp, gcp],
                          in_specs=[_VMEM] * 9, out_specs=[_VMEM] * 5,
                          )(a_re, a_im, log_dt, b_re_t, b_im_t, dl_re, dl_im, dbb_re, dbb_im)


N_CHUNK = 4
U_CHUNK = SSM_WIDTH // N_CHUNK
H_CHUNK = STATE_WIDTH // N_CHUNK
SUB = 8


def _block_diag_b(bb):
    x = bb.reshape(N_CHUNK, 8, SSM_GROUP, 1, SSM_STATE)
    same = (jnp.arange(8)[:, None] == jnp.arange(8)[None, :])[None, :, None, :, None]
    return jnp.where(same, x, 0.0).reshape(N_CHUNK, U_CHUNK, H_CHUNK)


def _block_diag_c(c):
    x = jnp.swapaxes(c.reshape(N_CHUNK, 8, SSM_GROUP, SSM_STATE), 2, 3)[:, :, :, None, :]
    same = (jnp.arange(8)[:, None] == jnp.arange(8)[None, :])[None, :, None, :, None]
    return jnp.where(same, x, 0.0).reshape(N_CHUNK, H_CHUNK, U_CHUNK)


def _diag_of_b(m):
    x = m.reshape(N_CHUNK, 8, SSM_GROUP, 8, SSM_STATE)
    return jnp.stack([x[:, g, :, g, :] for g in range(8)], axis=1).reshape(SSM_GROUPS, SSM_GROUP, SSM_STATE)


def _diag_of_c(m):
    x = m.reshape(N_CHUNK, 8, SSM_STATE, 8, SSM_GROUP)
    d = jnp.stack([x[:, g, :, g, :] for g in range(8)], axis=1)
    return jnp.swapaxes(d, 2, 3).reshape(SSM_GROUPS, SSM_GROUP, SSM_STATE)


def _lambda_tables(lr, li, reverse):
    p1 = (lr, li)
    p2 = _cmul(*p1, *p1)
    p4 = _cmul(*p2, *p2)
    rows = [p1]
    for _ in range(SUB - 1):
        rows.append(_cmul(*rows[-1], *p1))
    if reverse:
        rows = rows[::-1]
    return p1, p2, p4, (jnp.concatenate([r[0] for r in rows], axis=0), jnp.concatenate([r[1] for r in rows], axis=0))


def _scan8(xr, xi, pows, table, cr, ci, reverse):
    row = lax.broadcasted_iota(jnp.int32, xr.shape, 0)
    for d, (pr, pi) in zip((1, 2, 4), pows):
        if reverse:
            sr, si = pltpu.roll(xr, SUB - d, 0), pltpu.roll(xi, SUB - d, 0)
            keep = row < SUB - d
        else:
            sr, si = pltpu.roll(xr, d, 0), pltpu.roll(xi, d, 0)
            keep = row >= d
        sr = jnp.where(keep, sr, 0.0)
        si = jnp.where(keep, si, 0.0)
        xr, xi = xr + pr * sr - pi * si, xi + pr * si + pi * sr
    tr, ti = table
    return xr + tr * cr - ti * ci, xi + tr * ci + ti * cr


def _gelu_and_grad(y):
    k0 = math.sqrt(2.0 / math.pi)
    inner = k0 * (y + 0.044715 * y * y * y)
    th = jnp.tanh(inner)
    g = 0.5 * y * (1.0 + th)
    dg = 0.5 * (1.0 + th) + 0.5 * y * (1.0 - th * th) * k0 * (1.0 + 3.0 * 0.044715 * y * y)
    return g, dg


def ssm_fwd(u, lam_re, lam_im, bb_re, bb_im, cc_re, cc_im, d_skip, name):
    t = u.shape[0]
    tt = BLK

    def body(u_ref, lr_ref, li_ref, bbr_ref, bbi_ref, ccr_ref, cci_ref, d_ref, yg_ref, hr_ref, hi_ref, cr_ref, ci_ref):
        @pl.when(pl.program_id(0) == 0)
        def _():
            cr_ref[...] = jnp.zeros_like(cr_ref)
            ci_ref[...] = jnp.zeros_like(ci_ref)

        uv = u_ref[...]
        ub = uv.astype(BF16)
        for j in range(N_CHUNK):
            hs = slice(j * H_CHUNK, (j + 1) * H_CHUNK)
            us = slice(j * U_CHUNK, (j + 1) * U_CHUNK)
            hr_ref[:, hs] = _nn(ub[:, us], bbr_ref[j])
            hi_ref[:, hs] = _nn(ub[:, us], bbi_ref[j])
        p1, p2, p4, table = _lambda_tables(lr_ref[...], li_ref[...], False)

        def group(i, carry):
            cr, ci = carry
            rows = pl.ds(pl.multiple_of(i * SUB, SUB), SUB)
            xr, xi = _scan8(hr_ref[rows, :], hi_ref[rows, :], (p1, p2, p4), table, cr, ci, False)
            hr_ref[rows, :] = xr
            hi_ref[rows, :] = xi
            return xr[SUB - 1:SUB, :], xi[SUB - 1:SUB, :]

        cr, ci = lax.fori_loop(0, tt // SUB, group, (cr_ref[...], ci_ref[...]))
        cr_ref[...] = cr
        ci_ref[...] = ci
        for j in range(N_CHUNK):
            hs = slice(j * H_CHUNK, (j + 1) * H_CHUNK)
            us = slice(j * U_CHUNK, (j + 1) * U_CHUNK)
            y = (_nn(hr_ref[:, hs].astype(BF16), ccr_ref[j]) - _nn(hi_ref[:, hs].astype(BF16), cci_ref[j])
                 + d_ref[:, us] * uv[:, us])
            yg_ref[:, us] = _gelu_and_grad(y)[0].astype(BF16)

    return pl.pallas_call(
        body, name=name, grid=(t // tt,),
        in_specs=[_row_spec(tt, SSM_WIDTH), _VMEM, _VMEM, _VMEM, _VMEM, _VMEM, _VMEM, _VMEM],
        out_specs=[_row_spec(tt, SSM_WIDTH), _row_spec(tt, STATE_WIDTH), _row_spec(tt, STATE_WIDTH)],
        out_shape=[jax.ShapeDtypeStruct((t, SSM_WIDTH), BF16), jax.ShapeDtypeStruct((t, STATE_WIDTH), F32),
                   jax.ShapeDtypeStruct((t, STATE_WIDTH), F32)],
        scratch_shapes=[pltpu.VMEM((1, STATE_WIDTH), F32), pltpu.VMEM((1, STATE_WIDTH), F32)],
        compiler_params=_params(("arbitrary",)),
    )(u, lam_re, lam_im, bb_re, bb_im, cc_re, cc_im, d_skip)


def ssm_bwd(dyg, u, h_re, h_im, lam_re, lam_im, bb_re, bb_im, cc_re, cc_im, d_skip, name):
    t = u.shape[0]
    tt = BLK
    nt = t // tt

    def body(dyg_ref, u_ref, hr_ref, hi_ref, lr_ref, li_ref, bbr_ref, bbi_ref, ccr_ref, cci_ref, d_ref,
             du_ref, dlr_ref, dli_ref, dbbr_ref, dbbi_ref, dccr_ref, dcci_ref, dd_ref,
             ar_ref, ai_ref, cr_ref, ci_ref):
        step = pl.program_id(0)
        tile = nt - 1 - step

        @pl.when(step == 0)
        def _():
            for ref in (cr_ref, ci_ref, dlr_ref, dli_ref, dbbr_ref, dbbi_ref, dccr_ref, dcci_ref, dd_ref):
                ref[...] = jnp.zeros_like(ref)

        uv = u_ref[...]
        ub = uv.astype(BF16)
        dskip = d_ref[...]
        dy_chunks = []
        for j in range(N_CHUNK):
            hs = slice(j * H_CHUNK, (j + 1) * H_CHUNK)
            us = slice(j * U_CHUNK, (j + 1) * U_CHUNK)
            hrb = hr_ref[:, hs].astype(BF16)
            hib = hi_ref[:, hs].astype(BF16)
            y = _nn(hrb, ccr_ref[j]) - _nn(hib, cci_ref[j]) + dskip[:, us] * uv[:, us]
            dy = dyg_ref[:, us] * _gelu_and_grad(y)[1]
            dy_chunks.append(dy)
            dyb = dy.astype(BF16)
            dccr_ref[j] += _tn(hrb, dyb)
            dcci_ref[j] -= _tn(hib, dyb)
            ar_ref[:, hs] = _nt(dyb, ccr_ref[j])
            ai_ref[:, hs] = -_nt(dyb, cci_ref[j])
        dy_all = jnp.concatenate(dy_chunks, axis=1)
        dd_ref[...] += jnp.sum(dy_all * uv, axis=0, keepdims=True)

        lr, li = lr_ref[...], li_ref[...]
        p1, p2, p4, table = _lambda_tables(lr, -li, True)
        last_row = lax.broadcasted_iota(jnp.int32, (SUB, STATE_WIDTH), 0) == SUB - 1

        def group(k, carry):
            cr, ci, accr, acci = carry
            i = tt // SUB - 1 - k
            rows = pl.ds(pl.multiple_of(i * SUB, SUB), SUB)
            xr, xi = _scan8(ar_ref[rows, :], ai_ref[rows, :], (p1, p2, p4), table, cr, ci, True)
            ar_ref[rows, :] = xr
            ai_ref[rows, :] = xi
            nr = jnp.where(last_row, cr, pltpu.roll(xr, SUB - 1, 0))
            ni = jnp.where(last_row, ci, pltpu.roll(xi, SUB - 1, 0))
            hr, hi = hr_ref[rows, :], hi_ref[rows, :]
            return xr[0:1, :], xi[0:1, :], accr + nr * hr + ni * hi, acci + ni * hr - nr * hi

        zero = jnp.zeros((SUB, STATE_WIDTH), F32)
        cr, ci, accr, acci = lax.fori_loop(0, tt // SUB, group, (cr_ref[...], ci_ref[...], zero, zero))
        cr_ref[...] = cr
        ci_ref[...] = ci
        dlr_ref[...] += accr
        dli_ref[...] += acci

        row = tile * tt + lax.broadcasted_iota(jnp.int32, (tt, U_CHUNK), 0)
        for j in range(N_CHUNK):
            hs = slice(j * H_CHUNK, (j + 1) * H_CHUNK)
            us = slice(j * U_CHUNK, (j + 1) * U_CHUNK)
            arb = ar_ref[:, hs].astype(BF16)
            aib = ai_ref[:, hs].astype(BF16)
            dbbr_ref[j] += _tn(ub[:, us], arb)
            dbbi_ref[j] += _tn(ub[:, us], aib)
            du = _nt(arb, bbr_ref[j]) + _nt(aib, bbi_ref[j]) + dy_chunks[j] * dskip[:, us]
            du_ref[:, us] = jnp.where(row >= PAD_FRONT, du, 0.0).astype(BF16)

    rev = lambda i: (nt - 1 - i, 0)
    full = lambda shape: pl.BlockSpec(shape, lambda i: (0,) * len(shape))
    return pl.pallas_call(
        body, name=name, grid=(nt,),
        in_specs=[pl.BlockSpec((tt, SSM_WIDTH), rev), pl.BlockSpec((tt, SSM_WIDTH), rev),
                  pl.BlockSpec((tt, STATE_WIDTH), rev), pl.BlockSpec((tt, STATE_WIDTH), rev),
                  _VMEM, _VMEM, _VMEM, _VMEM, _VMEM, _VMEM, _VMEM],
        out_specs=[pl.BlockSpec((tt, SSM_WIDTH), rev), full((SUB, STATE_WIDTH)), full((SUB, STATE_WIDTH)),
                   full((N_CHUNK, U_CHUNK, H_CHUNK)), full((N_CHUNK, U_CHUNK, H_CHUNK)),
                   full((N_CHUNK, H_CHUNK, U_CHUNK)), full((N_CHUNK, H_CHUNK, U_CHUNK)), full((1, SSM_WIDTH))],
        out_shape=[jax.ShapeDtypeStruct((t, SSM_WIDTH), BF16),
                   jax.ShapeDtypeStruct((SUB, STATE_WIDTH), F32), jax.ShapeDtypeStruct((SUB, STATE_WIDTH), F32),
                   jax.ShapeDtypeStruct((N_CHUNK, U_CHUNK, H_CHUNK), F32),
                   jax.ShapeDtypeStruct((N_CHUNK, U_CHUNK, H_CHUNK), F32),
                   jax.ShapeDtypeStruct((N_CHUNK, H_CHUNK, U_CHUNK), F32),
                   jax.ShapeDtypeStruct((N_CHUNK, H_CHUNK, U_CHUNK), F32),
                   jax.ShapeDtypeStruct((1, SSM_WIDTH), F32)],
        scratch_shapes=[pltpu.VMEM((tt, STATE_WIDTH), F32), pltpu.VMEM((tt, STATE_WIDTH), F32),
                        pltpu.VMEM((1, STATE_WIDTH), F32), pltpu.VMEM((1, STATE_WIDTH), F32)],
        compiler_params=_params(("arbitrary",)),
    )(dyg, u, h_re, h_im, lam_re, lam_im, bb_re, bb_im, cc_re, cc_im, d_skip)


def merge_fwd(h, o, yg, gates, wap_t, wv_t, wgg_t, wout, name):
    t, d = h.shape
    tm = TOKEN_TILE

    def body(h_ref, o_ref, yg_ref, gt_ref, wap_ref, wv_ref, wgg_ref, wout_ref, ho_ref, mg_ref, a_ref, sv_ref, sg_ref):
        att = _nt(o_ref[...], wap_ref[...])
        ygv = yg_ref[...]
        sv = _nt(ygv, wv_ref[...])
        sg = _nt(ygv, wgg_ref[...])
        a_ref[...] = att
        sv_ref[...] = sv
        sg_ref[...] = sg
        merged = (jax.nn.sigmoid(gt_ref[:, 0:d]) * att
                  + jax.nn.sigmoid(gt_ref[:, d:2 * d]) * (sv * jax.nn.sigmoid(sg))).astype(BF16)
        mg_ref[...] = merged
        ho_ref[...] = h_ref[...] + _nn(merged, wout_ref[...])

    return pl.pallas_call(
        body, name=name, grid=(t // tm,),
        in_specs=[_row_spec(tm, d), _row_spec(tm, ATTN_WIDTH), _row_spec(tm, SSM_WIDTH), _row_spec(tm, 2 * d),
                  _VMEM, _VMEM, _VMEM, _VMEM],
        out_specs=[_row_spec(tm, d), _row_spec(tm, d), _row_spec(tm, d), _row_spec(tm, d), _row_spec(tm, d)],
        out_shape=[jax.ShapeDtypeStruct((t, d), F32), jax.ShapeDtypeStruct((t, d), BF16),
                   jax.ShapeDtypeStruct((t, d), F32), jax.ShapeDtypeStruct((t, d), F32),
                   jax.ShapeDtypeStruct((t, d), F32)],
        compiler_params=_params(("arbitrary",)),
    )(h, o, yg, gates, wap_t, wv_t, wgg_t, wout)


def merge_bwd(dh, gates, att, sv, sg, wap_t, wv_t, wgg_t, wout, name):
    t, d = dh.shape
    tm = TOKEN_TILE

    def body(dh_ref, gt_ref, a_ref, sv_ref, sg_ref, wap_ref, wv_ref, wgg_ref, wout_ref,
             dgt_ref, da_ref, dsv_ref, dsg_ref, do_ref, dyg_ref, dhb_ref):
        dhb = dh_ref[...].astype(BF16)
        dhb_ref[...] = dhb
        dm = _nt(dhb, wout_ref[...])
        sig_a = jax.nn.sigmoid(gt_ref[:, 0:d])
        sig_s = jax.nn.sigmoid(gt_ref[:, d:2 * d])
        sig_g = jax.nn.sigmoid(sg_ref[...])
        svv = sv_ref[...]
        dgt_ref[:, 0:d] = (dm * a_ref[...] * sig_a * (1.0 - sig_a)).astype(BF16)
        dgt_ref[:, d:2 * d] = (dm * (svv * sig_g) * sig_s * (1.0 - sig_s)).astype(BF16)
        da = (dm * sig_a).astype(BF16)
        d_s = dm * sig_s
        dsv = (d_s * sig_g).astype(BF16)
        dsg = (d_s * svv * sig_g * (1.0 - sig_g)).astype(BF16)
        da_ref[...] = da
        dsv_ref[...] = dsv
        dsg_ref[...] = dsg
        do_ref[...] = _nn(da, wap_ref[...]).astype(BF16)
        dyg_ref[...] = _nn(dsv, wv_ref[...]) + _nn(dsg, wgg_ref[...])

    return pl.pallas_call(
        body, name=name, grid=(t // tm,),
        in_specs=[_row_spec(tm, d), _row_spec(tm, 2 * d), _row_spec(tm, d), _row_spec(tm, d), _row_spec(tm, d),
                  _VMEM, _VMEM, _VMEM, _VMEM],
        out_specs=[_row_spec(tm, 2 * d), _row_spec(tm, d), _row_spec(tm, d), _row_spec(tm, d),
                   _row_spec(tm, ATTN_WIDTH), _row_spec(tm, SSM_WIDTH), _row_spec(tm, d)],
        out_shape=[jax.ShapeDtypeStruct((t, 2 * d), BF16), jax.ShapeDtypeStruct((t, d), BF16),
                   jax.ShapeDtypeStruct((t, d), BF16), jax.ShapeDtypeStruct((t, d), BF16),
                   jax.ShapeDtypeStruct((t, ATTN_WIDTH), BF16), jax.ShapeDtypeStruct((t, SSM_WIDTH), F32),
                   jax.ShapeDtypeStruct((t, d), BF16)],
        compiler_params=_params(("arbitrary",)),
    )(dh, gates, att, sv, sg, wap_t, wv_t, wgg_t, wout)


def adamw(w, g, m, v, name):
    shape = w.shape
    as2d = lambda a: a.reshape(-1, shape[-1]) if a.ndim >= 2 else a.reshape(1, -1)
    w2, g2, m2, v2 = as2d(w), as2d(g), as2d(m), as2d(v)
    rows, cols = w2.shape
    tr = rows
    for cand in (1024, 704, 512, 256):
        if rows > cand and rows % cand == 0:
            tr = cand
            break

    def body(w_ref, g_ref, m_ref, v_ref, d_ref, mo_ref, vo_ref):
        gv = g_ref[...]
        mn = ADAM_B1 * m_ref[...] + (1.0 - ADAM_B1) * gv
        vn = ADAM_B2 * v_ref[...] + (1.0 - ADAM_B2) * (gv * gv)
        m_hat = mn / (1.0 - ADAM_B1 ** ADAM_STEP)
        v_hat = vn / (1.0 - ADAM_B2 ** ADAM_STEP)
        d_ref[...] = -ADAM_LR * (m_hat / (jnp.sqrt(v_hat) + ADAM_EPS) + ADAM_WD * w_ref[...])
        mo_ref[...] = mn
        vo_ref[...] = vn

    spec = _row_spec(tr, cols)
    out = jax.ShapeDtypeStruct((rows, cols), F32)
    d, mn, vn = pl.pallas_call(
        body, name=name, grid=(rows // tr,), in_specs=[spec] * 4, out_specs=[spec] * 3, out_shape=[out] * 3,
        compiler_params=_params(("arbitrary",)),
    )(w2, g2, m2, v2)
    return d.reshape(shape), mn.reshape(shape), vn.reshape(shape)


def _my_index():
    return 4 * lax.axis_index("x") + 2 * lax.axis_index("y") + lax.axis_index("c")


def _peer(p):
    return (lax.axis_index("x") ^ ((p >> 2) & 1), lax.axis_index("y") ^ ((p >> 1) & 1), lax.axis_index("c") ^ (p & 1))


def all_gather(parts, name):
    n = len(parts)

    def body(*refs):
        srcs, dsts = refs[:n], refs[n:2 * n]
        send_sems, recv_sems, local_sems = refs[2 * n:]
        me = _my_index()
        local = [pltpu.make_async_copy(srcs[k], dsts[k].at[me], local_sems.at[k]) for k in range(n)]
        for cp in local:
            cp.start()
        copies = []
        for p in range(1, N_DEV):
            for k in range(n):
                copies.append(pltpu.make_async_remote_copy(
                    src_ref=srcs[k], dst_ref=dsts[k].at[me], send_sem=send_sems.at[p - 1, k],
                    recv_sem=recv_sems.at[p - 1, k], device_id=_peer(p), device_id_type=MESH))
        for cp in copies:
            cp.start()
        for cp in copies:
            cp.wait()
        for cp in local:
            cp.wait()

    return pl.pallas_call(
        body, name=name,
        in_specs=[_ANY] * n, out_specs=[_ANY] * n,
        out_shape=[jax.ShapeDtypeStruct((N_DEV,) + p.shape, p.dtype) for p in parts],
        scratch_shapes=[pltpu.SemaphoreType.DMA((N_DEV - 1, n)), pltpu.SemaphoreType.DMA((N_DEV - 1, n)),
                        pltpu.SemaphoreType.DMA((n,))],
    )(*parts)


def exchange_rows(srcs, scatter, name):
    n = len(srcs)
    widths = sorted({s.shape[1] for s in srcs}, reverse=True)
    cls = [widths.index(s.shape[1]) for s in srcs]
    cnts = [s.shape[0] // N_DEV if scatter else s.shape[0] for s in srcs]
    totals = [sum(c for c, k in zip(cnts, cls) if k == w) for w in range(len(widths))]
    sizer = [max((k for k in range(n) if cls[k] == w), key=lambda k: cnts[k]) for w in range(len(widths))]
    assert all(N_DEV * cnts[sizer[w]] >= totals[w] for w in range(len(widths)))

    def body(*refs):
        src, dst = refs[:n], refs[n:2 * n]
        send_sems, recv_sems, local_sems = refs[2 * n:]
        me = _my_index()

        def block(k, who):
            return pl.ds(pl.multiple_of(who * cnts[k], 16), cnts[k])

        local = []
        for k in range(n):
            if scatter:
                local.append(pltpu.make_async_copy(src[k].at[block(k, me), :], dst[k].at[me], local_sems.at[k]))
            else:
                local.append(pltpu.make_async_copy(src[k], dst[k].at[block(k, me), :], local_sems.at[k]))
        for cp in local:
            cp.start()
        for p in range(1, N_DEV):
            for k in range(n):
                if scatter:
                    s_ref, d_ref = src[k].at[block(k, me ^ p), :], dst[k].at[me]
                else:
                    s_ref, d_ref = src[k], dst[k].at[block(k, me), :]
                pltpu.make_async_remote_copy(
                    src_ref=s_ref, dst_ref=d_ref, send_sem=send_sems.at[p - 1, cls[k]],
                    recv_sem=recv_sems.at[p - 1, cls[k]], device_id=_peer(p), device_id_type=MESH).start()
        for p in range(1, N_DEV):
            for w in range(len(widths)):
                big = src[sizer[w]] if scatter else dst[sizer[w]]
                span = big.at[pl.ds(0, totals[w]), :]
                pltpu.make_async_remote_copy(
                    src_ref=span, dst_ref=span, send_sem=send_sems.at[p - 1, w], recv_sem=recv_sems.at[p - 1, w],
                    device_id=_peer(p), device_id_type=MESH).wait()
        for cp in local:
            cp.wait()

    if scatter:
        out_shape = [jax.ShapeDtypeStruct((N_DEV, c, s.shape[1]), s.dtype) for s, c in zip(srcs, cnts)]
    else:
        out_shape = [jax.ShapeDtypeStruct((N_DEV * c, s.shape[1]), s.dtype) for s, c in zip(srcs, cnts)]
    return pl.pallas_call(
        body, name=name, in_specs=[_ANY] * n, out_specs=[_ANY] * n, out_shape=out_shape,
        scratch_shapes=[pltpu.SemaphoreType.DMA((N_DEV - 1, len(widths))),
                        pltpu.SemaphoreType.DMA((N_DEV - 1, len(widths))), pltpu.SemaphoreType.DMA((n,))],
    )(*srcs)


def sum_slots(slots, name):
    _, rows, cols = slots.shape
    tr = rows
    if rows > 512:
        for cand in (256, 128, 64, 32, 16, 8):
            if rows % cand == 0:
                tr = cand
                break

    def body(s_ref, o_ref):
        acc = s_ref[0].astype(F32)
        for j in range(1, N_DEV):
            acc = acc + s_ref[j].astype(F32)
        o_ref[...] = acc

    return pl.pallas_call(
        body, name=name, grid=(rows // tr,),
        in_specs=[pl.BlockSpec((N_DEV, tr, cols), lambda i: (0, i, 0))], out_specs=_row_spec(tr, cols),
        out_shape=jax.ShapeDtypeStruct((rows, cols), F32), compiler_params=_params(("arbitrary",)),
    )(slots)


BIG_T = ("ffn1_w_gate", "ffn1_w_up", "w_in", "ffn2_w_gate", "ffn2_w_up")
BIG_N = ("ffn1_w_down", "w_out", "ffn2_w_down")
HALF_T = ("w_attn_proj", "w_glu_v", "w_glu_g")
SMALL = ("ffn1_norm", "mix_norm", "attn_sinks", "ssm_a_re", "ssm_a_im", "ssm_log_dt", "ssm_b_re", "ssm_b_im",
         "ssm_c_re", "ssm_c_im", "ssm_d", "ffn2_norm", "final_norm")
PARTS = {"ffn1": ("ffn1_w_gate", "ffn1_w_up", "ffn1_w_down"),
         "mix": ("w_in", "w_out", "w_attn_proj", "w_glu_v", "w_glu_g"),
         "ffn2": ("ffn2_w_gate", "ffn2_w_up", "ffn2_w_down")}


def _to_rows(name, a):
    return a if name in BIG_N else jnp.swapaxes(a, -1, -2)


def local_step(x, tgt, get_weights, put_grads, small):
    seq, d = x.shape
    t = PAD_FRONT + N_META + seq
    cos_t, sin_t = rope_tables(t)
    row = lambda a: a.reshape(1, -1)
    saved = []
    for i in range(DEPTH):
        s = {}
        w = dict(get_weights(i, "ffn1"))
        if i == 0:
            h = jnp.concatenate([jnp.zeros((PAD_FRONT, d), F32), w["meta_tokens"], x], axis=0)
        s["h0"] = h
        h, s["n1"], s["a1"], s["b1"] = ffn_fwd(h, row(small["ffn1_norm"][i]), w["ffn1_w_gate"], w["ffn1_w_up"],
                                               w["ffn1_w_down"], f"ffn1_fwd_{i}")
        s["h1"] = h
        w.update(get_weights(i, "mix"))
        s["n2"], s["qkv"], s["u"], s["gates"] = win_fwd(h, row(small["mix_norm"][i]), w["w_in"], cos_t, sin_t,
                                                        f"win_fwd_{i}")
        b_re_t = jnp.swapaxes(small["ssm_b_re"][i], 1, 2)
        b_im_t = jnp.swapaxes(small["ssm_b_im"][i], 1, 2)
        s["b_t"] = (b_re_t, b_im_t)
        lam_re, lam_im, bbar_re, bbar_im = ssm_prep(small["ssm_a_re"][i], small["ssm_a_im"][i],
                                                    small["ssm_log_dt"][i].reshape(-1, 1), b_re_t, b_im_t, f"ssm_prep_{i}")
        s["ssm"] = (row(lam_re), row(lam_im), _block_diag_b(bbar_re).astype(BF16), _block_diag_b(bbar_im).astype(BF16),
                    _block_diag_c(small["ssm_c_re"][i]).astype(BF16), _block_diag_c(small["ssm_c_im"][i]).astype(BF16),
                    row(small["ssm_d"][i]))
        s["yg"], s["h_re"], s["h_im"] = ssm_fwd(s["u"], *s["ssm"], f"ssm_fwd_{i}")
        s["o"] = attn_fwd(s["qkv"], row(small["attn_sinks"][i]), f"attn_fwd_{i}")
        h, s["merged"], s["att"], s["sv"], s["sg"] = merge_fwd(
            h, s["o"], s["yg"], s["gates"], w["w_attn_proj"], w["w_glu_v"], w["w_glu_g"], w["w_out"],
            f"merge_fwd_{i}")
        s["h2"] = h
        w.update(get_weights(i, "ffn2"))
        h, s["n3"], s["a3"], s["b3"] = ffn_fwd(h, row(small["ffn2_norm"][i]), w["ffn2_w_gate"], w["ffn2_w_up"],
                                               w["ffn2_w_down"], f"ffn2_fwd_{i}")
        s["w"] = w
        saved.append(s)

    loss, dh, d_final = head_fwd_bwd(h, row(small["final_norm"]), tgt)
    gs = {k: [None] * DEPTH for k in SMALL if k != "final_norm"}
    for i in reversed(range(DEPTH)):
        s = saved[i]
        w = s["w"]
        dh, da, db, sact, dhb, dg = ffn_bwd(dh, s["h2"], row(small["ffn2_norm"][i]), s["a3"], s["b3"], w["ffn2_w_gate"],
                                            w["ffn2_w_up"], w["ffn2_w_down"], f"ffn2_bwd_{i}")
        gs["ffn2_norm"][i] = dg[0]
        put_grads(i, "ffn2", {"ffn2_w_gate": tn_matmul(da, s["n3"], f"ffn2_dwg_{i}"),
                              "ffn2_w_up": tn_matmul(db, s["n3"], f"ffn2_dwu_{i}"),
                              "ffn2_w_down": tn_matmul(sact, dhb, f"ffn2_dwd_{i}")})

        dgates, datt, dsv, dsg, do, dyg, dhb = merge_bwd(dh, s["gates"], s["att"], s["sv"], s["sg"], w["w_attn_proj"],
                                                         w["w_glu_v"], w["w_glu_g"], w["w_out"], f"merge_bwd_{i}")
        gmix = {"w_out": tn_matmul(s["merged"], dhb, f"dwout_{i}"),
                "w_attn_proj": tn_matmul(datt, s["o"], f"dwap_{i}"),
                "w_glu_v": tn_matmul(dsv, s["yg"], f"dwv_{i}"),
                "w_glu_g": tn_matmul(dsg, s["yg"], f"dwgg_{i}")}
        dqkv, dsink = attn_bwd(s["qkv"], do, row(small["attn_sinks"][i]), cos_t, sin_t, f"attn_bwd_{i}")
        gs["attn_sinks"][i] = dsink[:, 0]
        du, dl_re, dl_im, dbb_re, dbb_im, dcc_re, dcc_im, dd = ssm_bwd(dyg, s["u"], s["h_re"], s["h_im"], *s["ssm"],
                                                                      f"ssm_bwd_{i}")
        fold = lambda a: jnp.sum(a, axis=0).reshape(SSM_GROUPS, SSM_STATE)
        da_re, da_im, dldt, db_re_t, db_im_t = ssm_prep_bwd(
            small["ssm_a_re"][i], small["ssm_a_im"][i], small["ssm_log_dt"][i].reshape(-1, 1), *s["b_t"],
            fold(dl_re), fold(dl_im), _diag_of_b(dbb_re), _diag_of_b(dbb_im), f"ssm_prep_bwd_{i}")
        gs["ssm_a_re"][i], gs["ssm_a_im"][i], gs["ssm_log_dt"][i] = da_re, da_im, dldt[:, 0]
        gs["ssm_b_re"][i], gs["ssm_b_im"][i] = jnp.swapaxes(db_re_t, 1, 2), jnp.swapaxes(db_im_t, 1, 2)
        gs["ssm_c_re"][i], gs["ssm_c_im"][i] = _diag_of_c(dcc_re), _diag_of_c(dcc_im)
        gs["ssm_d"][i] = dd[0]
        dz = jnp.concatenate([dqkv, du, dgates], axis=1)
        gmix["w_in"] = tn_matmul(dz, s["n2"], f"dwin_{i}")
        put_grads(i, "mix", gmix)
        dh, dg = win_bwd(dh, s["h1"], row(small["mix_norm"][i]), dz, w["w_in"], f"win_bwd_{i}")
        gs["mix_norm"][i] = dg[0]

        dh, da, db, sact, dhb, dg = ffn_bwd(dh, s["h0"], row(small["ffn1_norm"][i]), s["a1"], s["b1"], w["ffn1_w_gate"],
                                            w["ffn1_w_up"], w["ffn1_w_down"], f"ffn1_bwd_{i}")
        gs["ffn1_norm"][i] = dg[0]
        put_grads(i, "ffn1", {"ffn1_w_gate": tn_matmul(da, s["n1"], f"ffn1_dwg_{i}"),
                              "ffn1_w_up": tn_matmul(db, s["n1"], f"ffn1_dwu_{i}"),
                              "ffn1_w_down": tn_matmul(sact, dhb, f"ffn1_dwd_{i}")})

    gs = {k: jnp.stack(v) for k, v in gs.items()}
    gs["final_norm"] = d_final[0]
    return loss[0, 0], dh[PAD_FRONT + N_META:], dh[PAD_FRONT:PAD_FRONT + N_META], gs


def _pack_rows(arrays, cols):
    flat = jnp.concatenate([a.reshape(-1) for a in arrays])
    rows = -(-flat.shape[0] // cols)
    rows = -(-rows // 8) * 8
    return jnp.pad(flat, (0, rows * cols - flat.shape[0])).reshape(rows, cols)


def _unpack_rows(packed, shapes):
    flat = packed.reshape(-1)
    out, off = [], 0
    for shp in shapes:
        n = math.prod(shp)
        out.append(flat[off:off + n].reshape(shp))
        off += n
    return out


def kernel(x, meta_tokens, ffn1_norm, ffn1_w_gate, ffn1_w_up, ffn1_w_down, mix_norm, w_in, attn_sinks, ssm_a_re, ssm_a_im, ssm_log_dt, ssm_b_re, ssm_b_im, ssm_c_re, ssm_c_im, ssm_d, w_attn_proj, w_glu_v, w_glu_g, w_out, ffn2_norm, ffn2_w_gate, ffn2_w_up, ffn2_w_down, final_norm, loss_target, m_meta_tokens, m_ffn1_norm, m_ffn1_w_gate, m_ffn1_w_up, m_ffn1_w_down, m_mix_norm, m_w_in, m_attn_sinks, m_ssm_a_re, m_ssm_a_im, m_ssm_log_dt, m_ssm_b_re, m_ssm_b_im, m_ssm_c_re, m_ssm_c_im, m_ssm_d, m_w_attn_proj, m_w_glu_v, m_w_glu_g, m_w_out, m_ffn2_norm, m_ffn2_w_gate, m_ffn2_w_up, m_ffn2_w_down, m_final_norm, v_meta_tokens, v_ffn1_norm, v_ffn1_w_gate, v_ffn1_w_up, v_ffn1_w_down, v_mix_norm, v_w_in, v_attn_sinks, v_ssm_a_re, v_ssm_a_im, v_ssm_log_dt, v_ssm_b_re, v_ssm_b_im, v_ssm_c_re, v_ssm_c_im, v_ssm_d, v_w_attn_proj, v_w_glu_v, v_w_glu_g, v_w_out, v_ffn2_norm, v_ffn2_w_gate, v_ffn2_w_up, v_ffn2_w_down, v_final_norm):
    names = ("meta_tokens", "ffn1_norm", "ffn1_w_gate", "ffn1_w_up", "ffn1_w_down", "mix_norm", "w_in", "attn_sinks",
             "ssm_a_re", "ssm_a_im", "ssm_log_dt", "ssm_b_re", "ssm_b_im", "ssm_c_re", "ssm_c_im", "ssm_d",
             "w_attn_proj", "w_glu_v", "w_glu_g", "w_out", "ffn2_norm", "ffn2_w_gate", "ffn2_w_up", "ffn2_w_down",
             "final_norm")
    weights = dict(zip(names, (meta_tokens, ffn1_norm, ffn1_w_gate, ffn1_w_up, ffn1_w_down, mix_norm, w_in, attn_sinks, ssm_a_re, ssm_a_im, ssm_log_dt, ssm_b_re, ssm_b_im, ssm_c_re, ssm_c_im, ssm_d, w_attn_proj, w_glu_v, w_glu_g, w_out, ffn2_norm, ffn2_w_gate, ffn2_w_up, ffn2_w_down, final_norm)))
    moments_m = dict(zip(names, (m_meta_tokens, m_ffn1_norm, m_ffn1_w_gate, m_ffn1_w_up, m_ffn1_w_down, m_mix_norm, m_w_in, m_attn_sinks, m_ssm_a_re, m_ssm_a_im, m_ssm_log_dt, m_ssm_b_re, m_ssm_b_im, m_ssm_c_re, m_ssm_c_im, m_ssm_d, m_w_attn_proj, m_w_glu_v, m_w_glu_g, m_w_out, m_ffn2_norm, m_ffn2_w_gate, m_ffn2_w_up, m_ffn2_w_down, m_final_norm)))
    moments_v = dict(zip(names, (v_meta_tokens, v_ffn1_norm, v_ffn1_w_gate, v_ffn1_w_up, v_ffn1_w_down, v_mix_norm, v_w_in, v_attn_sinks, v_ssm_a_re, v_ssm_a_im, v_ssm_log_dt, v_ssm_b_re, v_ssm_b_im, v_ssm_c_re, v_ssm_c_im, v_ssm_d, v_w_attn_proj, v_w_glu_v, v_w_glu_g, v_w_out, v_ffn2_norm, v_ffn2_w_gate, v_ffn2_w_up, v_ffn2_w_down, v_final_norm)))
    me = _my_index()

    def get_weights(i, part):
        ks = PARTS[part]
        shards = [_to_rows(k, weights[k][i]).astype(BF16) for k in ks]
        if (i, part) == (0, "ffn1"):
            *fulls, meta_rows = exchange_rows(shards + [meta_tokens], False, f"gather_{part}_{i}")
            got = dict(zip(ks, fulls))
            got["meta_tokens"] = jnp.swapaxes(meta_rows.reshape(N_DEV, N_META, 128), 0, 1).reshape(N_META, D_MODEL)
            return got
        return dict(zip(ks, exchange_rows(shards, False, f"gather_{part}_{i}")))

    landed = {}

    def put_grads(i, part, gdict):
        ks = PARTS[part]
        for k, slots in zip(ks, exchange_rows([gdict[k] for k in ks], True, f"scatter_{part}_{i}")):
            landed[k, i] = slots

    small = {k: weights[k] for k in SMALL}
    loss, dx, dmeta, gs = local_step(x[0], loss_target[0], get_weights, put_grads, small)

    grads = {}
    for part in PARTS.values():
        for k in part:
            rows = jnp.stack([sum_slots(landed[k, i], f"sum_{k}_{i}") for i in range(DEPTH)])
            grads[k] = _to_rows(k, rows)

    small_list = [loss.reshape(1), dmeta] + [gs[k] for k in SMALL]
    packed = _pack_rows(small_list, D_MODEL)
    (packed_all,) = all_gather([packed], "gather_small")
    total = sum_slots(packed_all, "sum_small")
    pieces = _unpack_rows(total, [a.shape for a in small_list])
    loss_out = pieces[0][0]
    grads["meta_tokens"] = lax.dynamic_slice_in_dim(pieces[1], me * 128, 128, axis=1)
    for k, p in zip(SMALL, pieces[2:]):
        grads[k] = p

    deltas, new_m, new_v = {}, {}, {}
    for k in names:
        deltas[k], new_m[k], new_v[k] = adamw(weights[k], grads[k], moments_m[k], moments_v[k], f"adamw_{k}")
    return (loss_out, dx[None], *[grads[k] for k in names], *[deltas[k] for k in names],
            *[new_m[k] for k in names], *[new_v[k] for k in names])
```

```python
import functools
import math

import jax
import jax.numpy as jnp
from jax import lax
from jax.experimental import pallas as pl
from jax.experimental.pallas import tpu as pltpu

F32 = jnp.float32
BF16 = jnp.bfloat16

D_MODEL = 1024
DEPTH = 2
N_META = 16
HEAD_DIM = 64
N_Q_HEADS = 8
ATTN_WIDTH = 512
KV_WIDTH = 128
QKV_WIDTH = ATTN_WIDTH + 2 * KV_WIDTH
WINDOW = 128
BLK = 128
ROPE_THETA = 500000.0
ROT_DIM = 16
SSM_WIDTH = 512
SSM_GROUP = 16
SSM_GROUPS = 32
SSM_STATE = 64
STATE_WIDTH = SSM_GROUPS * SSM_STATE
D_FF = 2816
IN_WIDTH = 3328
EPS = 1e-6
NEG_INF = -1e30
PAD_FRONT = (-N_META) % BLK
N_DEV = 8

ADAM_LR = 0.001
ADAM_B1 = 0.9
ADAM_B2 = 0.999
ADAM_EPS = 1e-08
ADAM_WD = 0.01
ADAM_STEP = 10

VMEM_LIMIT = 56 * 1024 * 1024
TOKEN_TILE = 384
_VMEM = pl.BlockSpec(memory_space=pltpu.VMEM)
_SMEM = pl.BlockSpec(memory_space=pltpu.SMEM)
_ANY = pl.BlockSpec(memory_space=pl.ANY)
MESH = pl.DeviceIdType.MESH


def _params(sem=None):
    return pltpu.CompilerParams(dimension_semantics=sem, vmem_limit_bytes=VMEM_LIMIT)


def _nt(a, b):
    return lax.dot_general(a, b, (((1,), (1,)), ((), ())), preferred_element_type=F32)


def _nn(a, b):
    return jnp.dot(a, b, preferred_element_type=F32)


def _tn(a, b):
    return lax.dot_general(a, b, (((0,), (0,)), ((), ())), preferred_element_type=F32)


def _row_spec(tm, width):
    return pl.BlockSpec((tm, width), lambda i: (i, 0))


def _acc_spec(shape):
    return pl.BlockSpec(shape, lambda i: (0,) * len(shape))


def _rms_stats(x):
    r = lax.rsqrt(jnp.mean(x * x, axis=-1, keepdims=True) + EPS)
    return x * r, r


def _rms_bwd(dn, xh, r, g):
    dg = jnp.sum(dn * xh, axis=0, keepdims=True)
    dxh = dn * g
    dx = r * (dxh - xh * jnp.mean(dxh * xh, axis=-1, keepdims=True))
    return dx, dg


def ffn_fwd(h, g, wg_t, wu_t, wd, name):
    t, d = h.shape
    f = wd.shape[0]
    tm = TOKEN_TILE

    def body(h_ref, g_ref, wg_ref, wu_ref, wd_ref, ho_ref, n_ref, a_ref, b_ref):
        x = h_ref[...]
        xh, _ = _rms_stats(x)
        n = (xh * g_ref[...]).astype(BF16)
        n_ref[...] = n
        a = _nt(n, wg_ref[...])
        b = _nt(n, wu_ref[...])
        a_ref[...] = a.astype(BF16)
        b_ref[...] = b.astype(BF16)
        s = (a * jax.nn.sigmoid(a) * b).astype(BF16)
        ho_ref[...] = x + 0.5 * _nn(s, wd_ref[...])

    return pl.pallas_call(
        body, name=name, grid=(t // tm,),
        in_specs=[_row_spec(tm, d), _acc_spec((1, d)), _VMEM, _VMEM, _VMEM],
        out_specs=[_row_spec(tm, d), _row_spec(tm, d), _row_spec(tm, f), _row_spec(tm, f)],
        out_shape=[jax.ShapeDtypeStruct((t, d), F32), jax.ShapeDtypeStruct((t, d), BF16),
                   jax.ShapeDtypeStruct((t, f), BF16), jax.ShapeDtypeStruct((t, f), BF16)],
        compiler_params=_params(("arbitrary",)),
    )(h, g, wg_t, wu_t, wd)


def ffn_bwd(dh, h, g, a, b, wg_t, wu_t, wd, dep, name):
    t, d = h.shape
    f = wd.shape[0]
    tm = TOKEN_TILE // 2

    def body(dh_ref, h_ref, g_ref, a_ref, b_ref, wg_ref, wu_ref, wd_ref, dep_ref,
             dhi_ref, da_ref, db_ref, s_ref, dhb_ref, dg_ref):
        dh_t = dh_ref[...]
        dhb = (0.5 * dh_t).astype(BF16)
        dhb_ref[...] = dhb
        ds = _nt(dhb, wd_ref[...])
        av = a_ref[...].astype(F32)
        bv = b_ref[...].astype(F32)
        sig = jax.nn.sigmoid(av)
        sl = av * sig
        s_ref[...] = (sl * bv).astype(BF16)
        da = (ds * bv * (sig * (1.0 + av * (1.0 - sig)))).astype(BF16)
        db = (ds * sl).astype(BF16)
        da_ref[...] = da
        db_ref[...] = db
        dn = _nn(da, wg_ref[...]) + _nn(db, wu_ref[...])
        xh, r = _rms_stats(h_ref[...])
        dx, dg = _rms_bwd(dn, xh, r, g_ref[...])
        dhi_ref[...] = dh_t + dx

        @pl.when(pl.program_id(0) == 0)
        def _():
            dg_ref[...] = jnp.zeros_like(dg_ref)

        dg_ref[...] += dg

    return pl.pallas_call(
        body, name=name, grid=(t // tm,),
        in_specs=[_row_spec(tm, d), _row_spec(tm, d), _acc_spec((1, d)), _row_spec(tm, f), _row_spec(tm, f),
                  _VMEM, _VMEM, _VMEM, _ANY],
        out_specs=[_row_spec(tm, d), _row_spec(tm, f), _row_spec(tm, f), _row_spec(tm, f), _row_spec(tm, d),
                   _acc_spec((1, d))],
        out_shape=[jax.ShapeDtypeStruct((t, d), F32), jax.ShapeDtypeStruct((t, f), BF16),
                   jax.ShapeDtypeStruct((t, f), BF16), jax.ShapeDtypeStruct((t, f), BF16),
                   jax.ShapeDtypeStruct((t, d), BF16), jax.ShapeDtypeStruct((1, d), F32)],
        compiler_params=_params(("arbitrary",)),
    )(dh, h, g, a, b, wg_t, wu_t, wd, dep)


def _col_tile(m):
    for bm in (1408, 1664, 1024, 768, 512):
        if m % bm == 0:
            return bm
    raise ValueError(m)


def tn_matmul(x, y, name):
    t, m = x.shape
    n = y.shape[1]
    bm = _col_tile(m)
    bt = TOKEN_TILE
    nk = t // bt

    def body(x_ref, y_ref, o_ref, acc_ref):
        k = pl.program_id(1)

        @pl.when(k == 0)
        def _():
            acc_ref[...] = jnp.zeros_like(acc_ref)

        acc_ref[...] += _tn(x_ref[...], y_ref[...])

        @pl.when(k == nk - 1)
        def _():
            o_ref[...] = acc_ref[...].astype(BF16)

    return pl.pallas_call(
        body, name=name, grid=(m // bm, nk),
        in_specs=[pl.BlockSpec((bt, bm), lambda i, k: (k, i)), pl.BlockSpec((bt, n), lambda i, k: (k, 0))],
        out_specs=pl.BlockSpec((bm, n), lambda i, k: (i, 0)),
        out_shape=jax.ShapeDtypeStruct((m, n), BF16),
        scratch_shapes=[pltpu.VMEM((bm, n), F32)],
        compiler_params=_params(("arbitrary", "arbitrary")),
    )(x, y)


def head_fwd_bwd(h, g, tgt):
    t, d = h.shape

    def body(h_ref, g_ref, t_ref, loss_ref, dh_ref, dg_ref):
        i = pl.program_id(0)
        xh, r = _rms_stats(h_ref[...])
        gv = g_ref[...]
        valid = (i > 0).astype(F32)
        e = (xh * gv - t_ref[...]) * valid
        dx, dg = _rms_bwd(e * (1.0 / d), xh, r, gv)
        dh_ref[...] = dx

        @pl.when(i == 0)
        def _():
            dg_ref[...] = jnp.zeros_like(dg_ref)
            loss_ref[...] = jnp.zeros_like(loss_ref)

        dg_ref[...] += dg
        loss_ref[...] += jnp.sum(e * e) * (0.5 / d)

    return pl.pallas_call(
        body, name="head", grid=(t // BLK,),
        in_specs=[_row_spec(BLK, d), _acc_spec((1, d)),
                  pl.BlockSpec((BLK, d), lambda i: (jnp.maximum(i - 1, 0), 0))],
        out_specs=[_acc_spec((1, 128)), _row_spec(BLK, d), _acc_spec((1, d))],
        out_shape=[jax.ShapeDtypeStruct((1, 128), F32), jax.ShapeDtypeStruct((t, d), F32),
                   jax.ShapeDtypeStruct((1, d), F32)],
        compiler_params=_params(("arbitrary",)),
    )(h, g, tgt)


def rope_tables(t):
    pos = jnp.arange(t, dtype=F32) - PAD_FRONT
    inv_freq = ROPE_THETA ** (-jnp.arange(0, ROT_DIM, 2, dtype=F32) / ROT_DIM)
    ang = pos[:, None] * inv_freq[None, :]
    cos, sin = jnp.cos(ang), jnp.sin(ang)
    ones = jnp.ones((t, HEAD_DIM - ROT_DIM), F32)
    cos_h = jnp.concatenate([cos, cos, ones], axis=1)
    sin_h = jnp.concatenate([-sin, sin, 0.0 * ones], axis=1)
    return jnp.concatenate([cos_h, cos_h], axis=1), jnp.concatenate([sin_h, sin_h], axis=1)


def _swap_halves(x):
    n = x.shape[1]
    lane = lax.broadcasted_iota(jnp.int32, x.shape, 1)
    return jnp.where(lane % HEAD_DIM < ROT_DIM // 2, pltpu.roll(x, n - ROT_DIM // 2, 1), pltpu.roll(x, ROT_DIM // 2, 1))


def _rope(x, cos_t, sin_t, sign):
    return x * cos_t + sign * (_swap_halves(x) * sin_t)


def win_fwd(h, g, win_t, cos_t, sin_t, name):
    t, d = h.shape
    tm = TOKEN_TILE

    def body(h_ref, g_ref, w_ref, c_ref, s_ref, n_ref, qkv_ref, u_ref, gates_ref):
        xh, _ = _rms_stats(h_ref[...])
        n = (xh * g_ref[...]).astype(BF16)
        n_ref[...] = n
        z = _nt(n, w_ref[...])
        c, s = c_ref[...], s_ref[...]
        for j in range((ATTN_WIDTH + KV_WIDTH) // 128):
            qkv_ref[:, j * 128:(j + 1) * 128] = _rope(z[:, j * 128:(j + 1) * 128], c, s, 1.0).astype(BF16)
        qkv_ref[:, ATTN_WIDTH + KV_WIDTH:QKV_WIDTH] = z[:, ATTN_WIDTH + KV_WIDTH:QKV_WIDTH].astype(BF16)
        u_ref[...] = z[:, QKV_WIDTH:QKV_WIDTH + SSM_WIDTH]
        gates_ref[...] = z[:, QKV_WIDTH + SSM_WIDTH:]

    return pl.pallas_call(
        body, name=name, grid=(t // tm,),
        in_specs=[_row_spec(tm, d), _acc_spec((1, d)), _VMEM, _row_spec(tm, 128), _row_spec(tm, 128)],
        out_specs=[_row_spec(tm, d), _row_spec(tm, QKV_WIDTH), _row_spec(tm, SSM_WIDTH), _row_spec(tm, 2 * d)],
        out_shape=[jax.ShapeDtypeStruct((t, d), BF16), jax.ShapeDtypeStruct((t, QKV_WIDTH), BF16),
                   jax.ShapeDtypeStruct((t, SSM_WIDTH), F32), jax.ShapeDtypeStruct((t, 2 * d), F32)],
        compiler_params=_params(("arbitrary",)),
    )(h, g, win_t, cos_t, sin_t)


def win_bwd(dh, h, g, dz, win_t, dep, name):
    t, d = h.shape
    tm = TOKEN_TILE

    def body(dh_ref, h_ref, g_ref, dz_ref, w_ref, dep_ref, dhi_ref, dg_ref):
        dn = _nn(dz_ref[...], w_ref[...])
        xh, r = _rms_stats(h_ref[...])
        dx, dg = _rms_bwd(dn, xh, r, g_ref[...])
        dhi_ref[...] = dh_ref[...] + dx

        @pl.when(pl.program_id(0) == 0)
        def _():
            dg_ref[...] = jnp.zeros_like(dg_ref)

        dg_ref[...] += dg

    return pl.pallas_call(
        body, name=name, grid=(t // tm,),
        in_specs=[_row_spec(tm, d), _row_spec(tm, d), _acc_spec((1, d)), _row_spec(tm, IN_WIDTH), _VMEM, _ANY],
        out_specs=[_row_spec(tm, d), _acc_spec((1, d))],
        out_shape=[jax.ShapeDtypeStruct((t, d), F32), jax.ShapeDtypeStruct((1, d), F32)],
        compiler_params=_params(("arbitrary",)),
    )(dh, h, g, dz, win_t, dep)


def _attn_mask(blk):
    q_pos = blk * BLK + lax.broadcasted_iota(jnp.int32, (BLK, 3 * BLK), 0) - PAD_FRONT
    col = lax.broadcasted_iota(jnp.int32, (BLK, 3 * BLK), 1)
    part = col // BLK
    k_pos = jnp.where(part == 0, col, (blk + part - 2) * BLK + (col - part * BLK)) - PAD_FRONT
    dist = q_pos - k_pos
    meta_ok = (part == 0) & (k_pos >= 0) & (dist >= 0)
    band_ok = (part > 0) & (k_pos >= N_META) & (dist >= 0) & (dist < WINDOW)
    return meta_ok | band_ok


def _head_halves(x128, kv):
    x = x128.astype(F32)
    lane = lax.broadcasted_iota(jnp.int32, x.shape, 1)
    swapped = pltpu.roll(x, HEAD_DIM, 1)
    lo, hi = (x, swapped) if kv == 0 else (swapped, x)
    return jnp.where(lane < HEAD_DIM, lo, 0.0).astype(BF16), jnp.where(lane >= HEAD_DIM, hi, 0.0).astype(BF16)


def _gather_keys(meta_ref, prev_ref, cur_ref, lo):
    return jnp.concatenate([meta_ref[:, lo:lo + 128], prev_ref[:, lo:lo + 128], cur_ref[:, lo:lo + 128]], axis=0)


def _softmax_with_sink(s, mask, sink):
    s = jnp.where(mask, s * (HEAD_DIM ** -0.5), NEG_INF)
    m = jnp.maximum(jnp.max(s, axis=-1, keepdims=True), sink)
    p = jnp.exp(s - m)
    p_sink = jnp.exp(sink - m)
    inv = 1.0 / (jnp.sum(p, axis=-1, keepdims=True) + p_sink)
    return p * inv, p_sink * inv


def attn_fwd(qkv, sinks, name):
    t = qkv.shape[0]
    nb = t // BLK

    def body(sink_ref, meta_ref, prev_ref, cur_ref, o_ref):
        blk = pl.program_id(0)
        mask = _attn_mask(blk)
        k128 = _gather_keys(meta_ref, prev_ref, cur_ref, ATTN_WIDTH)
        v128 = _gather_keys(meta_ref, prev_ref, cur_ref, ATTN_WIDTH + KV_WIDTH)
        for kv in range(2):
            k_lo, k_hi = _head_halves(k128, kv)
            v_lo, v_hi = _head_halves(v128, kv)
            for pair in range(2):
                lanes = slice((2 * kv + pair) * 128, (2 * kv + pair + 1) * 128)
                q128 = cur_ref[:, lanes]
                head = 4 * kv + 2 * pair
                p_a, _ = _softmax_with_sink(_nt(q128, k_lo), mask, sink_ref[0, head])
                p_b, _ = _softmax_with_sink(_nt(q128, k_hi), mask, sink_ref[0, head + 1])
                o_ref[:, lanes] = (_nn(p_a.astype(BF16), v_lo) + _nn(p_b.astype(BF16), v_hi)).astype(BF16)

    blk_spec = lambda f: pl.BlockSpec((BLK, QKV_WIDTH), f)
    return pl.pallas_call(
        body, name=name, grid=(nb,),
        in_specs=[_SMEM, blk_spec(lambda i: (0, 0)), blk_spec(lambda i: (jnp.maximum(i - 1, 0), 0)),
                  blk_spec(lambda i: (i, 0))],
        out_specs=_row_spec(BLK, ATTN_WIDTH),
        out_shape=jax.ShapeDtypeStruct((t, ATTN_WIDTH), BF16),
        compiler_params=_params(("arbitrary",)),
    )(sinks, qkv, qkv, qkv)


def attn_bwd(qkv, do, sinks, cos_t, sin_t, name):
    t = qkv.shape[0]
    nb = t // BLK

    def body(sink_ref, meta_ref, prev_ref, cur_ref, do_ref, c_ref, s_ref, dqkv_ref, dsink_ref, carry_ref, macc_ref):
        step = pl.program_id(0)
        blk = nb - 1 - step

        @pl.when(step == 0)
        def _():
            dsink_ref[...] = jnp.zeros_like(dsink_ref)
            carry_ref[...] = jnp.zeros_like(carry_ref)
            macc_ref[...] = jnp.zeros_like(macc_ref)

        mask = _attn_mask(blk)
        lane = lax.broadcasted_iota(jnp.int32, (3 * BLK, 128), 1)
        k128 = _gather_keys(meta_ref, prev_ref, cur_ref, ATTN_WIDTH)
        v128 = _gather_keys(meta_ref, prev_ref, cur_ref, ATTN_WIDTH + KV_WIDTH)
        cos_b, sin_b = c_ref[...], s_ref[...]
        dk_heads, dv_heads = [], []
        for kv in range(2):
            k_lo, k_hi = _head_halves(k128, kv)
            v_lo, v_hi = _head_halves(v128, kv)
            dk_acc = jnp.zeros((3 * BLK, 128), F32)
            dv_acc = jnp.zeros((3 * BLK, 128), F32)
            for pair in range(2):
                lanes = slice((2 * kv + pair) * 128, (2 * kv + pair + 1) * 128)
                q128 = cur_ref[:, lanes]
                do128 = do_ref[:, lanes]
                head = 4 * kv + 2 * pair
                ds_pair, p_pair = [], []
                for half, (k_h, v_h) in enumerate(((k_lo, v_lo), (k_hi, v_hi))):
                    p, p_sink = _softmax_with_sink(_nt(q128, k_h), mask, sink_ref[0, head + half])
                    dp = _nt(do128, v_h)
                    dsum = jnp.sum(p * dp, axis=-1, keepdims=True)
                    ds_pair.append((p * (dp - dsum) * (HEAD_DIM ** -0.5)).astype(BF16))
                    p_pair.append(p.astype(BF16))
                    dsink = -jnp.sum(p_sink * dsum, axis=0, keepdims=True)
                    dsink_ref[head + half:head + half + 1, :] += jnp.broadcast_to(dsink, (1, 128))
                dq = _nn(ds_pair[0], k_lo) + _nn(ds_pair[1], k_hi)
                dqkv_ref[:, lanes] = _rope(dq, cos_b, sin_b, -1.0).astype(BF16)
                dk_acc += jnp.where(lane < HEAD_DIM, _tn(ds_pair[0], q128), _tn(ds_pair[1], q128))
                dv_acc += jnp.where(lane < HEAD_DIM, _tn(p_pair[0], do128), _tn(p_pair[1], do128))
            dk_heads.append(dk_acc + pltpu.roll(dk_acc, HEAD_DIM, 1))
            dv_heads.append(dv_acc + pltpu.roll(dv_acc, HEAD_DIM, 1))
        dkv = jnp.concatenate([jnp.where(lane < HEAD_DIM, dk_heads[0], dk_heads[1]),
                               jnp.where(lane < HEAD_DIM, dv_heads[0], dv_heads[1])], axis=1)
        macc_ref[...] += dkv[0:BLK]
        is_last = (blk == 0).astype(F32)
        mine = dkv[2 * BLK:3 * BLK] + carry_ref[...] + is_last * macc_ref[...]
        carry_ref[...] = dkv[BLK:2 * BLK]
        dqkv_ref[:, ATTN_WIDTH:ATTN_WIDTH + KV_WIDTH] = _rope(mine[:, 0:128], cos_b, sin_b, -1.0).astype(BF16)
        dqkv_ref[:, ATTN_WIDTH + KV_WIDTH:QKV_WIDTH] = mine[:, 128:256].astype(BF16)

    rev = lambda i: nb - 1 - i
    blk_spec = lambda f: pl.BlockSpec((BLK, QKV_WIDTH), f)
    return pl.pallas_call(
        body, name=name, grid=(nb,),
        in_specs=[_SMEM, blk_spec(lambda i: (0, 0)), blk_spec(lambda i: (jnp.maximum(rev(i) - 1, 0), 0)),
                  blk_spec(lambda i: (rev(i), 0)), pl.BlockSpec((BLK, ATTN_WIDTH), lambda i: (rev(i), 0)),
                  pl.BlockSpec((BLK, 128), lambda i: (rev(i), 0)), pl.BlockSpec((BLK, 128), lambda i: (rev(i), 0))],
        out_specs=[pl.BlockSpec((BLK, QKV_WIDTH), lambda i: (rev(i), 0)), _acc_spec((N_Q_HEADS, 128))],
        out_shape=[jax.ShapeDtypeStruct((t, QKV_WIDTH), BF16), jax.ShapeDtypeStruct((N_Q_HEADS, 128), F32)],
        scratch_shapes=[pltpu.VMEM((BLK, 256), F32), pltpu.VMEM((BLK, 256), F32)],
        compiler_params=_params(("arbitrary",)),
    )(sinks, qkv, qkv, qkv, do, cos_t, sin_t)


def _cmul(ar, ai, br, bi):
    return ar * br - ai * bi, ar * bi + ai * br


def ssm_prep(a_re, a_im, log_dt, b_re_t, b_im_t, name):
    def body(ar_ref, ai_ref, ldt_ref, br_ref, bi_ref, lr_ref, li_ref, bbr_ref, bbi_ref):
        ar, ai = ar_ref[...], ai_ref[...]
        dt = jnp.exp(ldt_ref[...])
        mag = jnp.exp(ar * dt)
        lr = mag * jnp.cos(ai * dt)
        li = mag * jnp.sin(ai * dt)
        den = ar * ar + ai * ai
        nr = lr - 1.0
        cr = ((nr * ar + li * ai) / den)[:, None, :]
        ci = ((li * ar - nr * ai) / den)[:, None, :]
        br, bi = br_ref[...], bi_ref[...]
        lr_ref[...] = lr
        li_ref[...] = li
        bbr_ref[...] = cr * br - ci * bi
        bbi_ref[...] = cr * bi + ci * br

    gp = jax.ShapeDtypeStruct(a_re.shape, F32)
    gcp = jax.ShapeDtypeStruct(b_re_t.shape, F32)
    return pl.pallas_call(body, name=name, out_shape=[gp, gp, gcp, gcp],
                          in_specs=[_VMEM] * 5, out_specs=[_VMEM] * 4)(a_re, a_im, log_dt, b_re_t, b_im_t)


def ssm_prep_bwd(a_re, a_im, log_dt, b_re_t, b_im_t, dl_re, dl_im, dbb_re, dbb_im, name):
    def body(ar_ref, ai_ref, ldt_ref, br_ref, bi_ref, dlr_ref, dli_ref, dbbr_ref, dbbi_ref,
             dar_ref, dai_ref, dldt_ref, dbr_ref, dbi_ref):
        ar, ai = ar_ref[...], ai_ref[...]
        dt = jnp.exp(ldt_ref[...])
        mag = jnp.exp(ar * dt)
        lr = mag * jnp.cos(ai * dt)
        li = mag * jnp.sin(ai * dt)
        den = ar * ar + ai * ai
        nr = lr - 1.0
        cr = (nr * ar + li * ai) / den
        ci = (li * ar - nr * ai) / den
        br, bi = br_ref[...], bi_ref[...]
        dbbr, dbbi = dbbr_ref[...], dbbi_ref[...]
        dbr_ref[...] = cr[:, None, :] * dbbr + ci[:, None, :] * dbbi
        dbi_ref[...] = cr[:, None, :] * dbbi - ci[:, None, :] * dbbr
        dcr = jnp.sum(br * dbbr + bi * dbbi, axis=1)
        dci = jnp.sum(br * dbbi - bi * dbbr, axis=1)
        d_num_r = dcr / den
        d_num_i = dci / den
        d_den = -(dcr * cr + dci * ci) / den
        d_lr = dlr_ref[...] + d_num_r * ar - d_num_i * ai
        d_li = dli_ref[...] + d_num_r * ai + d_num_i * ar
        d_ar = d_num_r * nr + d_num_i * li + d_den * 2.0 * ar
        d_ai = d_num_r * li - d_num_i * nr + d_den * 2.0 * ai
        d_mag = (d_lr * lr + d_li * li) / mag
        d_theta = d_li * lr - d_lr * li
        d_ardt = d_mag * mag
        dar_ref[...] = d_ar + d_ardt * dt
        dai_ref[...] = d_ai + d_theta * dt
        d_dt = jnp.sum(d_ardt * ar + d_theta * ai, axis=1, keepdims=True)
        dldt_ref[...] = d_dt * dt

    gp = jax.ShapeDtypeStruct(a_re.shape, F32)
    gcp = jax.ShapeDtypeStruct(b_re_t.shape, F32)
    return pl.pallas_call(body, name=name, out_shape=[gp, gp, jax.ShapeDtypeStruct(log_dt.shape, F32), gcp, gcp],
                          in_specs=[_VMEM] * 9, out_specs=[_VMEM] * 5,
                          )(a_re, a_im, log_dt, b_re_t, b_im_t, dl_re, dl_im, dbb_re, dbb_im)


N_CHUNK = 4
U_CHUNK = SSM_WIDTH // N_CHUNK
H_CHUNK = STATE_WIDTH // N_CHUNK
SUB = 8


def _block_diag_b(bb):
    x = bb.reshape(N_CHUNK, 8, SSM_GROUP, 1, SSM_STATE)
    same = (jnp.arange(8)[:, None] == jnp.arange(8)[None, :])[None, :, None, :, None]
    return jnp.where(same, x, 0.0).reshape(N_CHUNK, U_CHUNK, H_CHUNK)


def _block_diag_c(c):
    x = jnp.swapaxes(c.reshape(N_CHUNK, 8, SSM_GROUP, SSM_STATE), 2, 3)[:, :, :, None, :]
    same = (jnp.arange(8)[:, None] == jnp.arange(8)[None, :])[None, :, None, :, None]
    return jnp.where(same, x, 0.0).reshape(N_CHUNK, H_CHUNK, U_CHUNK)


def _diag_of_b(m):
    x = m.reshape(N_CHUNK, 8, SSM_GROUP, 8, SSM_STATE)
    return jnp.stack([x[:, g, :, g, :] for g in range(8)], axis=1).reshape(SSM_GROUPS, SSM_GROUP, SSM_STATE)


def _diag_of_c(m):
    x = m.reshape(N_CHUNK, 8, SSM_STATE, 8, SSM_GROUP)
    d = jnp.stack([x[:, g, :, g, :] for g in range(8)], axis=1)
    return jnp.swapaxes(d, 2, 3).reshape(SSM_GROUPS, SSM_GROUP, SSM_STATE)


def _lambda_tables(lr, li, reverse):
    p1 = (lr, li)
    p2 = _cmul(*p1, *p1)
    p4 = _cmul(*p2, *p2)
    rows = [p1]
    for _ in range(SUB - 1):
        rows.append(_cmul(*rows[-1], *p1))
    if reverse:
        rows = rows[::-1]
    return p1, p2, p4, (jnp.concatenate([r[0] for r in rows], axis=0), jnp.concatenate([r[1] for r in rows], axis=0))


def _scan8(xr, xi, pows, table, cr, ci, reverse):
    row = lax.broadcasted_iota(jnp.int32, xr.shape, 0)
    for d, (pr, pi) in zip((1, 2, 4), pows):
        if reverse:
            sr, si = pltpu.roll(xr, SUB - d, 0), pltpu.roll(xi, SUB - d, 0)
            keep = row < SUB - d
        else:
            sr, si = pltpu.roll(xr, d, 0), pltpu.roll(xi, d, 0)
            keep = row >= d
        sr = jnp.where(keep, sr, 0.0)
        si = jnp.where(keep, si, 0.0)
        xr, xi = xr + pr * sr - pi * si, xi + pr * si + pi * sr
    tr, ti = table
    return xr + tr * cr - ti * ci, xi + tr * ci + ti * cr


def _gelu_and_grad(y):
    k0 = math.sqrt(2.0 / math.pi)
    inner = k0 * (y + 0.044715 * y * y * y)
    th = jnp.tanh(inner)
    g = 0.5 * y * (1.0 + th)
    dg = 0.5 * (1.0 + th) + 0.5 * y * (1.0 - th * th) * k0 * (1.0 + 3.0 * 0.044715 * y * y)
    return g, dg


def ssm_fwd(u, lam_re, lam_im, bb_re, bb_im, cc_re, cc_im, d_skip, name):
    t = u.shape[0]
    tt = BLK

    def body(u_ref, lr_ref, li_ref, bbr_ref, bbi_ref, ccr_ref, cci_ref, d_ref, yg_ref, hr_ref, hi_ref, cr_ref, ci_ref):
        @pl.when(pl.program_id(0) == 0)
        def _():
            cr_ref[...] = jnp.zeros_like(cr_ref)
            ci_ref[...] = jnp.zeros_like(ci_ref)

        uv = u_ref[...]
        ub = uv.astype(BF16)
        for j in range(N_CHUNK):
            hs = slice(j * H_CHUNK, (j + 1) * H_CHUNK)
            us = slice(j * U_CHUNK, (j + 1) * U_CHUNK)
            hr_ref[:, hs] = _nn(ub[:, us], bbr_ref[j])
            hi_ref[:, hs] = _nn(ub[:, us], bbi_ref[j])
        p1, p2, p4, table = _lambda_tables(lr_ref[...], li_ref[...], False)

        def group(i, carry):
            cr, ci = carry
            rows = pl.ds(pl.multiple_of(i * SUB, SUB), SUB)
            xr, xi = _scan8(hr_ref[rows, :], hi_ref[rows, :], (p1, p2, p4), table, cr, ci, False)
            hr_ref[rows, :] = xr
            hi_ref[rows, :] = xi
            return xr[SUB - 1:SUB, :], xi[SUB - 1:SUB, :]

        cr, ci = lax.fori_loop(0, tt // SUB, group, (cr_ref[...], ci_ref[...]))
        cr_ref[...] = cr
        ci_ref[...] = ci
        for j in range(N_CHUNK):
            hs = slice(j * H_CHUNK, (j + 1) * H_CHUNK)
            us = slice(j * U_CHUNK, (j + 1) * U_CHUNK)
            y = (_nn(hr_ref[:, hs].astype(BF16), ccr_ref[j]) - _nn(hi_ref[:, hs].astype(BF16), cci_ref[j])
                 + d_ref[:, us] * uv[:, us])
            yg_ref[:, us] = _gelu_and_grad(y)[0].astype(BF16)

    return pl.pallas_call(
        body, name=name, grid=(t // tt,),
        in_specs=[_row_spec(tt, SSM_WIDTH), _VMEM, _VMEM, _VMEM, _VMEM, _VMEM, _VMEM, _VMEM],
        out_specs=[_row_spec(tt, SSM_WIDTH), _row_spec(tt, STATE_WIDTH), _row_spec(tt, STATE_WIDTH)],
        out_shape=[jax.ShapeDtypeStruct((t, SSM_WIDTH), BF16), jax.ShapeDtypeStruct((t, STATE_WIDTH), F32),
                   jax.ShapeDtypeStruct((t, STATE_WIDTH), F32)],
        scratch_shapes=[pltpu.VMEM((1, STATE_WIDTH), F32), pltpu.VMEM((1, STATE_WIDTH), F32)],
        compiler_params=_params(("arbitrary",)),
    )(u, lam_re, lam_im, bb_re, bb_im, cc_re, cc_im, d_skip)


def ssm_bwd(dyg, u, h_re, h_im, lam_re, lam_im, bb_re, bb_im, cc_re, cc_im, d_skip, name):
    t = u.shape[0]
    tt = BLK
    nt = t // tt

    def body(dyg_ref, u_ref, hr_ref, hi_ref, lr_ref, li_ref, bbr_ref, bbi_ref, ccr_ref, cci_ref, d_ref,
             du_ref, dlr_ref, dli_ref, dbbr_ref, dbbi_ref, dccr_ref, dcci_ref, dd_ref,
             ar_ref, ai_ref, cr_ref, ci_ref):
        step = pl.program_id(0)
        tile = nt - 1 - step

        @pl.when(step == 0)
        def _():
            for ref in (cr_ref, ci_ref, dlr_ref, dli_ref, dbbr_ref, dbbi_ref, dccr_ref, dcci_ref, dd_ref):
                ref[...] = jnp.zeros_like(ref)

        uv = u_ref[...]
        ub = uv.astype(BF16)
        dskip = d_ref[...]
        dy_chunks = []
        for j in range(N_CHUNK):
            hs = slice(j * H_CHUNK, (j + 1) * H_CHUNK)
            us = slice(j * U_CHUNK, (j + 1) * U_CHUNK)
            hrb = hr_ref[:, hs].astype(BF16)
            hib = hi_ref[:, hs].astype(BF16)
            y = _nn(hrb, ccr_ref[j]) - _nn(hib, cci_ref[j]) + dskip[:, us] * uv[:, us]
            dy = dyg_ref[:, us] * _gelu_and_grad(y)[1]
            dy_chunks.append(dy)
            dyb = dy.astype(BF16)
            dccr_ref[j] += _tn(hrb, dyb)
            dcci_ref[j] -= _tn(hib, dyb)
            ar_ref[:, hs] = _nt(dyb, ccr_ref[j])
            ai_ref[:, hs] = -_nt(dyb, cci_ref[j])
        dy_all = jnp.concatenate(dy_chunks, axis=1)
        dd_ref[...] += jnp.sum(dy_all * uv, axis=0, keepdims=True)

        lr, li = lr_ref[...], li_ref[...]
        p1, p2, p4, table = _lambda_tables(lr, -li, True)
        last_row = lax.broadcasted_iota(jnp.int32, (SUB, STATE_WIDTH), 0) == SUB - 1

        def group(k, carry):
            cr, ci, accr, acci = carry
            i = tt // SUB - 1 - k
            rows = pl.ds(pl.multiple_of(i * SUB, SUB), SUB)
            xr, xi = _scan8(ar_ref[rows, :], ai_ref[rows, :], (p1, p2, p4), table, cr, ci, True)
            ar_ref[rows, :] = xr
            ai_ref[rows, :] = xi
            nr = jnp.where(last_row, cr, pltpu.roll(xr, SUB - 1, 0))
            ni = jnp.where(last_row, ci, pltpu.roll(xi, SUB - 1, 0))
            hr, hi = hr_ref[rows, :], hi_ref[rows, :]
            return xr[0:1, :], xi[0:1, :], accr + nr * hr + ni * hi, acci + ni * hr - nr * hi

        zero = jnp.zeros((SUB, STATE_WIDTH), F32)
        cr, ci, accr, acci = lax.fori_loop(0, tt // SUB, group, (cr_ref[...], ci_ref[...], zero, zero))
        cr_ref[...] = cr
        ci_ref[...] = ci
        dlr_ref[...] += accr
        dli_ref[...] += acci

        row = tile * tt + lax.broadcasted_iota(jnp.int32, (tt, U_CHUNK), 0)
        for j in range(N_CHUNK):
            hs = slice(j * H_CHUNK, (j + 1) * H_CHUNK)
            us = slice(j * U_CHUNK, (j + 1) * U_CHUNK)
            arb = ar_ref[:, hs].astype(BF16)
            aib = ai_ref[:, hs].astype(BF16)
            dbbr_ref[j] += _tn(ub[:, us], arb)
            dbbi_ref[j] += _tn(ub[:, us], aib)
            du = _nt(arb, bbr_ref[j]) + _nt(aib, bbi_ref[j]) + dy_chunks[j] * dskip[:, us]
            du_ref[:, us] = jnp.where(row >= PAD_FRONT, du, 0.0).astype(BF16)

    rev = lambda i: (nt - 1 - i, 0)
    full = lambda shape: pl.BlockSpec(shape, lambda i: (0,) * len(shape))
    return pl.pallas_call(
        body, name=name, grid=(nt,),
        in_specs=[pl.BlockSpec((tt, SSM_WIDTH), rev), pl.BlockSpec((tt, SSM_WIDTH), rev),
                  pl.BlockSpec((tt, STATE_WIDTH), rev), pl.BlockSpec((tt, STATE_WIDTH), rev),
                  _VMEM, _VMEM, _VMEM, _VMEM, _VMEM, _VMEM, _VMEM],
        out_specs=[pl.BlockSpec((tt, SSM_WIDTH), rev), full((SUB, STATE_WIDTH)), full((SUB, STATE_WIDTH)),
                   full((N_CHUNK, U_CHUNK, H_CHUNK)), full((N_CHUNK, U_CHUNK, H_CHUNK)),
                   full((N_CHUNK, H_CHUNK, U_CHUNK)), full((N_CHUNK, H_CHUNK, U_CHUNK)), full((1, SSM_WIDTH))],
        out_shape=[jax.ShapeDtypeStruct((t, SSM_WIDTH), BF16),
                   jax.ShapeDtypeStruct((SUB, STATE_WIDTH), F32), jax.ShapeDtypeStruct((SUB, STATE_WIDTH), F32),
                   jax.ShapeDtypeStruct((N_CHUNK, U_CHUNK, H_CHUNK), F32),
                   jax.ShapeDtypeStruct((N_CHUNK, U_CHUNK, H_CHUNK), F32),
                   jax.ShapeDtypeStruct((N_CHUNK, H_CHUNK, U_CHUNK), F32),
                   jax.ShapeDtypeStruct((N_CHUNK, H_CHUNK, U_CHUNK), F32),
                   jax.ShapeDtypeStruct((1, SSM_WIDTH), F32)],
        scratch_shapes=[pltpu.VMEM((tt, STATE_WIDTH), F32), pltpu.VMEM((tt, STATE_WIDTH), F32),
                        pltpu.VMEM((1, STATE_WIDTH), F32), pltpu.VMEM((1, STATE_WIDTH), F32)],
        compiler_params=_params(("arbitrary",)),
    )(dyg, u, h_re, h_im, lam_re, lam_im, bb_re, bb_im, cc_re, cc_im, d_skip)


def merge_fwd(h, o, yg, gates, wap_t, wv_t, wgg_t, wout, name):
    t, d = h.shape
    tm = TOKEN_TILE

    def body(h_ref, o_ref, yg_ref, gt_ref, wap_ref, wv_ref, wgg_ref, wout_ref, ho_ref, mg_ref, a_ref, sv_ref, sg_ref):
        att = _nt(o_ref[...], wap_ref[...])
        ygv = yg_ref[...]
        sv = _nt(ygv, wv_ref[...])
        sg = _nt(ygv, wgg_ref[...])
        a_ref[...] = att
        sv_ref[...] = sv
        sg_ref[...] = sg
        merged = (jax.nn.sigmoid(gt_ref[:, 0:d]) * att
                  + jax.nn.sigmoid(gt_ref[:, d:2 * d]) * (sv * jax.nn.sigmoid(sg))).astype(BF16)
        mg_ref[...] = merged
        ho_ref[...] = h_ref[...] + _nn(merged, wout_ref[...])

    return pl.pallas_call(
        body, name=name, grid=(t // tm,),
        in_specs=[_row_spec(tm, d), _row_spec(tm, ATTN_WIDTH), _row_spec(tm, SSM_WIDTH), _row_spec(tm, 2 * d),
                  _VMEM, _VMEM, _VMEM, _VMEM],
        out_specs=[_row_spec(tm, d), _row_spec(tm, d), _row_spec(tm, d), _row_spec(tm, d), _row_spec(tm, d)],
        out_shape=[jax.ShapeDtypeStruct((t, d), F32), jax.ShapeDtypeStruct((t, d), BF16),
                   jax.ShapeDtypeStruct((t, d), F32), jax.ShapeDtypeStruct((t, d), F32),
                   jax.ShapeDtypeStruct((t, d), F32)],
        compiler_params=_params(("arbitrary",)),
    )(h, o, yg, gates, wap_t, wv_t, wgg_t, wout)


def merge_bwd(dh, gates, att, sv, sg, wap_t, wv_t, wgg_t, wout, dep, name):
    t, d = dh.shape
    tm = TOKEN_TILE

    def body(dh_ref, gt_ref, a_ref, sv_ref, sg_ref, wap_ref, wv_ref, wgg_ref, wout_ref, dep_ref,
             dgt_ref, da_ref, dsv_ref, dsg_ref, do_ref, dyg_ref, dhb_ref):
        dhb = dh_ref[...].astype(BF16)
        dhb_ref[...] = dhb
        dm = _nt(dhb, wout_ref[...])
        sig_a = jax.nn.sigmoid(gt_ref[:, 0:d])
        sig_s = jax.nn.sigmoid(gt_ref[:, d:2 * d])
        sig_g = jax.nn.sigmoid(sg_ref[...])
        svv = sv_ref[...]
        dgt_ref[:, 0:d] = (dm * a_ref[...] * sig_a * (1.0 - sig_a)).astype(BF16)
        dgt_ref[:, d:2 * d] = (dm * (svv * sig_g) * sig_s * (1.0 - sig_s)).astype(BF16)
        da = (dm * sig_a).astype(BF16)
        d_s = dm * sig_s
        dsv = (d_s * sig_g).astype(BF16)
        dsg = (d_s * svv * sig_g * (1.0 - sig_g)).astype(BF16)
        da_ref[...] = da
        dsv_ref[...] = dsv
        dsg_ref[...] = dsg
        do_ref[...] = _nn(da, wap_ref[...]).astype(BF16)
        dyg_ref[...] = _nn(dsv, wv_ref[...]) + _nn(dsg, wgg_ref[...])

    return pl.pallas_call(
        body, name=name, grid=(t // tm,),
        in_specs=[_row_spec(tm, d), _row_spec(tm, 2 * d), _row_spec(tm, d), _row_spec(tm, d), _row_spec(tm, d),
                  _VMEM, _VMEM, _VMEM, _VMEM, _ANY],
        out_specs=[_row_spec(tm, 2 * d), _row_spec(tm, d), _row_spec(tm, d), _row_spec(tm, d),
                   _row_spec(tm, ATTN_WIDTH), _row_spec(tm, SSM_WIDTH), _row_spec(tm, d)],
        out_shape=[jax.ShapeDtypeStruct((t, 2 * d), BF16), jax.ShapeDtypeStruct((t, d), BF16),
                   jax.ShapeDtypeStruct((t, d), BF16), jax.ShapeDtypeStruct((t, d), BF16),
                   jax.ShapeDtypeStruct((t, ATTN_WIDTH), BF16), jax.ShapeDtypeStruct((t, SSM_WIDTH), F32),
                   jax.ShapeDtypeStruct((t, d), BF16)],
        compiler_params=_params(("arbitrary",)),
    )(dh, gates, att, sv, sg, wap_t, wv_t, wgg_t, wout, dep)


def adamw(w, g, m, v, name):
    shape = w.shape
    as2d = lambda a: a.reshape(-1, shape[-1]) if a.ndim >= 2 else a.reshape(1, -1)
    w2, g2, m2, v2 = as2d(w), as2d(g), as2d(m), as2d(v)
    rows, cols = w2.shape
    tr = rows
    for cand in (1024, 704, 512, 256):
        if rows > cand and rows % cand == 0:
            tr = cand
            break

    def body(w_ref, g_ref, m_ref, v_ref, d_ref, mo_ref, vo_ref):
        gv = g_ref[...]
        mn = ADAM_B1 * m_ref[...] + (1.0 - ADAM_B1) * gv
        vn = ADAM_B2 * v_ref[...] + (1.0 - ADAM_B2) * (gv * gv)
        m_hat = mn / (1.0 - ADAM_B1 ** ADAM_STEP)
        v_hat = vn / (1.0 - ADAM_B2 ** ADAM_STEP)
        d_ref[...] = -ADAM_LR * (m_hat / (jnp.sqrt(v_hat) + ADAM_EPS) + ADAM_WD * w_ref[...])
        mo_ref[...] = mn
        vo_ref[...] = vn

    spec = _row_spec(tr, cols)
    out = jax.ShapeDtypeStruct((rows, cols), F32)
    d, mn, vn = pl.pallas_call(
        body, name=name, grid=(rows // tr,), in_specs=[spec] * 4, out_specs=[spec] * 3, out_shape=[out] * 3,
        compiler_params=_params(("arbitrary",)),
    )(w2, g2, m2, v2)
    return d.reshape(shape), mn.reshape(shape), vn.reshape(shape)


def _my_index():
    return 4 * lax.axis_index("x") + 2 * lax.axis_index("y") + lax.axis_index("c")


def _peer(p):
    return (lax.axis_index("x") ^ ((p >> 2) & 1), lax.axis_index("y") ^ ((p >> 1) & 1), lax.axis_index("c") ^ (p & 1))


def all_gather(parts, name):
    n = len(parts)

    def body(*refs):
        srcs, dsts = refs[:n], refs[n:2 * n]
        send_sems, recv_sems, local_sems = refs[2 * n:]
        me = _my_index()
        local = [pltpu.make_async_copy(srcs[k], dsts[k].at[me], local_sems.at[k]) for k in range(n)]
        for cp in local:
            cp.start()
        copies = []
        for p in range(1, N_DEV):
            for k in range(n):
                copies.append(pltpu.make_async_remote_copy(
                    src_ref=srcs[k], dst_ref=dsts[k].at[me], send_sem=send_sems.at[p - 1, k],
                    recv_sem=recv_sems.at[p - 1, k], device_id=_peer(p), device_id_type=MESH))
        for cp in copies:
            cp.start()
        for cp in copies:
            cp.wait()
        for cp in local:
            cp.wait()

    return pl.pallas_call(
        body, name=name,
        in_specs=[_ANY] * n, out_specs=[_ANY] * n,
        out_shape=[jax.ShapeDtypeStruct((N_DEV,) + p.shape, p.dtype) for p in parts],
        scratch_shapes=[pltpu.SemaphoreType.DMA((N_DEV - 1, n)), pltpu.SemaphoreType.DMA((N_DEV - 1, n)),
                        pltpu.SemaphoreType.DMA((n,))],
    )(*parts)


_HBM = pl.BlockSpec(memory_space=pltpu.HBM)
_SEM = pl.BlockSpec(memory_space=pltpu.SEMAPHORE)
_EFFECT = pltpu.SideEffectType.DATAFLOW_SIDE_EFFECTING


class Exchange:
    def __init__(self, srcs, scatter, name):
        self.n = n = len(srcs)
        self.scatter = scatter
        self.name = name
        widths = sorted({s.shape[1] for s in srcs}, reverse=True)
        self.ncls = len(widths)
        self.cls = [widths.index(s.shape[1]) for s in srcs]
        self.cnts = [s.shape[0] // N_DEV if scatter else s.shape[0] for s in srcs]
        self.totals = [sum(c for c, k in zip(self.cnts, self.cls) if k == w) for w in range(self.ncls)]
        self.sizer = [max((k for k in range(n) if self.cls[k] == w), key=lambda k: self.cnts[k])
                      for w in range(self.ncls)]
        assert all(N_DEV * self.cnts[self.sizer[w]] >= self.totals[w] for w in range(self.ncls))
        if scatter:
            self.land_shapes = [(N_DEV, c, s.shape[1]) for s, c in zip(srcs, self.cnts)]
        else:
            self.land_shapes = [(N_DEV * c, s.shape[1]) for s, c in zip(srcs, self.cnts)]
        self.dtypes = [s.dtype for s in srcs]

    def _block(self, k, who):
        return pl.ds(pl.multiple_of(who * self.cnts[k], 16), self.cnts[k])

    def _sem(self, p, w):
        return (p - 1) * self.ncls + w

    def start(self, srcs, after):
        n = self.n

        def body(*refs):
            src, land = refs[:n], refs[n:2 * n]
            send_sems, recv_sems = refs[2 * n + 1], refs[2 * n + 2]
            token = refs[-1]
            me = _my_index()
            for p in range(1, N_DEV):
                for k in range(n):
                    if self.scatter:
                        s_ref, d_ref = src[k].at[self._block(k, me ^ p), :], land[k].at[me]
                    else:
                        s_ref, d_ref = src[k], land[k].at[self._block(k, me), :]
                    pltpu.make_async_remote_copy(
                        src_ref=s_ref, dst_ref=d_ref, send_sem=send_sems.at[self._sem(p, self.cls[k])],
                        recv_sem=recv_sems.at[self._sem(p, self.cls[k])], device_id=_peer(p),
                        device_id_type=MESH).start()
            token[...] = jnp.zeros_like(token)

        sems = pltpu.SemaphoreType.DMA(((N_DEV - 1) * self.ncls,))
        thru = [pltpu.HBM(s.shape, s.dtype) for s in srcs] + [pltpu.HBM(shp, dt) for shp, dt in
                                                               zip(self.land_shapes, self.dtypes)]
        lands = [pltpu.with_memory_space_constraint(lax.empty(shp, dt), pltpu.HBM)
                 for shp, dt in zip(self.land_shapes, self.dtypes)]
        out = pl.pallas_call(
            body, name=self.name + "_start",
            in_specs=[_HBM] * (2 * n) + [_ANY],
            out_shape=[sems, sems] + thru + [jax.ShapeDtypeStruct((8, 128), F32)],
            out_specs=[_SEM, _SEM] + [_HBM] * (2 * n) + [_VMEM],
            input_output_aliases={j: 2 + j for j in range(2 * n)},
            compiler_params=pltpu.CompilerParams(has_side_effects=_EFFECT),
        )(*[pltpu.with_memory_space_constraint(s, pltpu.HBM) for s in srcs], *lands, after)
        return out[:-1], out[-1]

    def wait(self, state, after):
        n = self.n
        send_sems, recv_sems = state[0], state[1]
        thru = state[2:]

        def body(*refs):
            src, land = refs[:n], refs[n:2 * n]
            send_sems, recv_sems = refs[2 * n], refs[2 * n + 1]
            for p in range(1, N_DEV):
                for w in range(self.ncls):
                    big = src[self.sizer[w]] if self.scatter else land[self.sizer[w]]
                    span = big.at[pl.ds(0, self.totals[w]), :]
                    copy = pltpu.make_async_remote_copy(
                        src_ref=span, dst_ref=span, send_sem=send_sems.at[self._sem(p, w)],
                        recv_sem=recv_sems.at[self._sem(p, w)],
                        device_id=_peer(p), device_id_type=MESH)
                    copy.wait_send()
                    copy.wait_recv()

        out = pl.pallas_call(
            body, name=self.name + "_wait",
            in_specs=[_HBM] * (2 * n) + [_SEM, _SEM, _ANY],
            out_shape=[pltpu.HBM(a.shape, a.dtype) for a in thru], out_specs=[_HBM] * (2 * n),
            input_output_aliases={j: j for j in range(2 * n)},
            compiler_params=pltpu.CompilerParams(has_side_effects=_EFFECT),
        )(*thru, send_sems, recv_sems, after)
        return self.place(out[n:], out[:n])

    def place(self, lands, srcs):
        n = self.n

        def body(*refs):
            land, src, sems = refs[:n], refs[n:2 * n], refs[-1]
            me = _my_index()
            copies = []
            for k in range(n):
                if self.scatter:
                    copies.append(pltpu.make_async_copy(src[k].at[self._block(k, me), :], land[k].at[me], sems.at[k]))
                else:
                    copies.append(pltpu.make_async_copy(src[k], land[k].at[self._block(k, me), :], sems.at[k]))
            for cp in copies:
                cp.start()
            for cp in copies:
                cp.wait()

        return pl.pallas_call(
            body, name=self.name + "_place", in_specs=[_ANY] * (2 * n), out_specs=[_ANY] * n,
            out_shape=[jax.ShapeDtypeStruct(a.shape, a.dtype) for a in lands],
            input_output_aliases={j: j for j in range(n)},
            scratch_shapes=[pltpu.SemaphoreType.DMA((n,))],
        )(*lands, *srcs)


def sum_slots(slots, name):
    _, rows, cols = slots.shape
    tr = rows
    if rows > 512:
        for cand in (256, 128, 64, 32, 16, 8):
            if rows % cand == 0:
                tr = cand
                break

    def body(s_ref, o_ref):
        acc = s_ref[0].astype(F32)
        for j in range(1, N_DEV):
            acc = acc + s_ref[j].astype(F32)
        o_ref[...] = acc

    return pl.pallas_call(
        body, name=name, grid=(rows // tr,),
        in_specs=[pl.BlockSpec((N_DEV, tr, cols), lambda i: (0, i, 0))], out_specs=_row_spec(tr, cols),
        out_shape=jax.ShapeDtypeStruct((rows, cols), F32), compiler_params=_params(("arbitrary",)),
    )(slots)


BIG_T = ("ffn1_w_gate", "ffn1_w_up", "w_in", "ffn2_w_gate", "ffn2_w_up")
BIG_N = ("ffn1_w_down", "w_out", "ffn2_w_down")
HALF_T = ("w_attn_proj", "w_glu_v", "w_glu_g")
SMALL = ("ffn1_norm", "mix_norm", "attn_sinks", "ssm_a_re", "ssm_a_im", "ssm_log_dt", "ssm_b_re", "ssm_b_im",
         "ssm_c_re", "ssm_c_im", "ssm_d", "ffn2_norm", "final_norm")
PARTS = {"ffn1": ("ffn1_w_gate", "ffn1_w_up", "ffn1_w_down"),
         "mix": ("w_in", "w_out", "w_attn_proj", "w_glu_v", "w_glu_g"),
         "ffn2": ("ffn2_w_gate", "ffn2_w_up", "ffn2_w_down")}


def _to_rows(name, a):
    return a if name in BIG_N else jnp.swapaxes(a, -1, -2)


def local_step(x, tgt, get_weights, put_grads, small):
    seq, d = x.shape
    t = PAD_FRONT + N_META + seq
    cos_t, sin_t = rope_tables(t)
    row = lambda a: a.reshape(1, -1)
    saved = []
    h = None
    for i in range(DEPTH):
        s = {}
        w = dict(get_weights(i, "ffn1", h))
        if i == 0:
            h = jnp.concatenate([jnp.zeros((PAD_FRONT, d), F32), w["meta_tokens"], x], axis=0)
        s["h0"] = h
        h, s["n1"], s["a1"], s["b1"] = ffn_fwd(h, row(small["ffn1_norm"][i]), w["ffn1_w_gate"], w["ffn1_w_up"],
                                               w["ffn1_w_down"], f"ffn1_fwd_{i}")
        s["h1"] = h
        w.update(get_weights(i, "mix", h))
        s["n2"], s["qkv"], s["u"], s["gates"] = win_fwd(h, row(small["mix_norm"][i]), w["w_in"], cos_t, sin_t,
                                                        f"win_fwd_{i}")
        b_re_t = jnp.swapaxes(small["ssm_b_re"][i], 1, 2)
        b_im_t = jnp.swapaxes(small["ssm_b_im"][i], 1, 2)
        s["b_t"] = (b_re_t, b_im_t)
        lam_re, lam_im, bbar_re, bbar_im = ssm_prep(small["ssm_a_re"][i], small["ssm_a_im"][i],
                                                    small["ssm_log_dt"][i].reshape(-1, 1), b_re_t, b_im_t, f"ssm_prep_{i}")
        s["ssm"] = (row(lam_re), row(lam_im), _block_diag_b(bbar_re).astype(BF16), _block_diag_b(bbar_im).astype(BF16),
                    _block_diag_c(small["ssm_c_re"][i]).astype(BF16), _block_diag_c(small["ssm_c_im"][i]).astype(BF16),
                    row(small["ssm_d"][i]))
        s["yg"], s["h_re"], s["h_im"] = ssm_fwd(s["u"], *s["ssm"], f"ssm_fwd_{i}")
        s["o"] = attn_fwd(s["qkv"], row(small["attn_sinks"][i]), f"attn_fwd_{i}")
        h, s["merged"], s["att"], s["sv"], s["sg"] = merge_fwd(
            h, s["o"], s["yg"], s["gates"], w["w_attn_proj"], w["w_glu_v"], w["w_glu_g"], w["w_out"],
            f"merge_fwd_{i}")
        s["h2"] = h
        w.update(get_weights(i, "ffn2", h))
        h, s["n3"], s["a3"], s["b3"] = ffn_fwd(h, row(small["ffn2_norm"][i]), w["ffn2_w_gate"], w["ffn2_w_up"],
                                               w["ffn2_w_down"], f"ffn2_fwd_{i}")
        s["w"] = w
        saved.append(s)

    loss, dh, d_final = head_fwd_bwd(h, row(small["final_norm"]), tgt)
    gs = {k: [None] * DEPTH for k in SMALL if k != "final_norm"}
    dep = loss
    for i in reversed(range(DEPTH)):
        s = saved[i]
        w = s["w"]
        dh, da, db, sact, dhb, dg = ffn_bwd(dh, s["h2"], row(small["ffn2_norm"][i]), s["a3"], s["b3"], w["ffn2_w_gate"],
                                            w["ffn2_w_up"], w["ffn2_w_down"], dep, f"ffn2_bwd_{i}")
        gs["ffn2_norm"][i] = dg[0]
        dep = put_grads(i, "ffn2", {"ffn2_w_gate": tn_matmul(da, s["n3"], f"ffn2_dwg_{i}"),
                                    "ffn2_w_up": tn_matmul(db, s["n3"], f"ffn2_dwu_{i}"),
                                    "ffn2_w_down": tn_matmul(sact, dhb, f"ffn2_dwd_{i}")})

        dgates, datt, dsv, dsg, do, dyg, dhb = merge_bwd(dh, s["gates"], s["att"], s["sv"], s["sg"], w["w_attn_proj"],
                                                         w["w_glu_v"], w["w_glu_g"], w["w_out"], dep, f"merge_bwd_{i}")
        gmix = {"w_out": tn_matmul(s["merged"], dhb, f"dwout_{i}"),
                "w_attn_proj": tn_matmul(datt, s["o"], f"dwap_{i}"),
                "w_glu_v": tn_matmul(dsv, s["yg"], f"dwv_{i}"),
                "w_glu_g": tn_matmul(dsg, s["yg"], f"dwgg_{i}")}
        dqkv, dsink = attn_bwd(s["qkv"], do, row(small["attn_sinks"][i]), cos_t, sin_t, f"attn_bwd_{i}")
        gs["attn_sinks"][i] = dsink[:, 0]
        du, dl_re, dl_im, dbb_re, dbb_im, dcc_re, dcc_im, dd = ssm_bwd(dyg, s["u"], s["h_re"], s["h_im"], *s["ssm"],
                                                                      f"ssm_bwd_{i}")
        fold = lambda a: jnp.sum(a, axis=0).reshape(SSM_GROUPS, SSM_STATE)
        da_re, da_im, dldt, db_re_t, db_im_t = ssm_prep_bwd(
            small["ssm_a_re"][i], small["ssm_a_im"][i], small["ssm_log_dt"][i].reshape(-1, 1), *s["b_t"],
            fold(dl_re), fold(dl_im), _diag_of_b(dbb_re), _diag_of_b(dbb_im), f"ssm_prep_bwd_{i}")
        gs["ssm_a_re"][i], gs["ssm_a_im"][i], gs["ssm_log_dt"][i] = da_re, da_im, dldt[:, 0]
        gs["ssm_b_re"][i], gs["ssm_b_im"][i] = jnp.swapaxes(db_re_t, 1, 2), jnp.swapaxes(db_im_t, 1, 2)
        gs["ssm_c_re"][i], gs["ssm_c_im"][i] = _diag_of_c(dcc_re), _diag_of_c(dcc_im)
        gs["ssm_d"][i] = dd[0]
        dz = jnp.concatenate([dqkv, du, dgates], axis=1)
        gmix["w_in"] = tn_matmul(dz, s["n2"], f"dwin_{i}")
        dep = put_grads(i, "mix", gmix)
        dh, dg = win_bwd(dh, s["h1"], row(small["mix_norm"][i]), dz, w["w_in"], dep, f"win_bwd_{i}")
        gs["mix_norm"][i] = dg[0]

        dh, da, db, sact, dhb, dg = ffn_bwd(dh, s["h0"], row(small["ffn1_norm"][i]), s["a1"], s["b1"], w["ffn1_w_gate"],
                                            w["ffn1_w_up"], w["ffn1_w_down"], dep, f"ffn1_bwd_{i}")
        gs["ffn1_norm"][i] = dg[0]
        dep = put_grads(i, "ffn1", {"ffn1_w_gate": tn_matmul(da, s["n1"], f"ffn1_dwg_{i}"),
                                    "ffn1_w_up": tn_matmul(db, s["n1"], f"ffn1_dwu_{i}"),
                                    "ffn1_w_down": tn_matmul(sact, dhb, f"ffn1_dwd_{i}")})

    gs = {k: jnp.stack(v) for k, v in gs.items()}
    gs["final_norm"] = d_final[0]
    return loss[0, 0], dh[PAD_FRONT + N_META:], dh[PAD_FRONT:PAD_FRONT + N_META], gs


def _pack_rows(arrays, cols):
    flat = jnp.concatenate([a.reshape(-1) for a in arrays])
    rows = -(-flat.shape[0] // cols)
    rows = -(-rows // 8) * 8
    return jnp.pad(flat, (0, rows * cols - flat.shape[0])).reshape(rows, cols)


def _unpack_rows(packed, shapes):
    flat = packed.reshape(-1)
    out, off = [], 0
    for shp in shapes:
        n = math.prod(shp)
        out.append(flat[off:off + n].reshape(shp))
        off += n
    return out


def kernel(x, meta_tokens, ffn1_norm, ffn1_w_gate, ffn1_w_up, ffn1_w_down, mix_norm, w_in, attn_sinks, ssm_a_re, ssm_a_im, ssm_log_dt, ssm_b_re, ssm_b_im, ssm_c_re, ssm_c_im, ssm_d, w_attn_proj, w_glu_v, w_glu_g, w_out, ffn2_norm, ffn2_w_gate, ffn2_w_up, ffn2_w_down, final_norm, loss_target, m_meta_tokens, m_ffn1_norm, m_ffn1_w_gate, m_ffn1_w_up, m_ffn1_w_down, m_mix_norm, m_w_in, m_attn_sinks, m_ssm_a_re, m_ssm_a_im, m_ssm_log_dt, m_ssm_b_re, m_ssm_b_im, m_ssm_c_re, m_ssm_c_im, m_ssm_d, m_w_attn_proj, m_w_glu_v, m_w_glu_g, m_w_out, m_ffn2_norm, m_ffn2_w_gate, m_ffn2_w_up, m_ffn2_w_down, m_final_norm, v_meta_tokens, v_ffn1_norm, v_ffn1_w_gate, v_ffn1_w_up, v_ffn1_w_down, v_mix_norm, v_w_in, v_attn_sinks, v_ssm_a_re, v_ssm_a_im, v_ssm_log_dt, v_ssm_b_re, v_ssm_b_im, v_ssm_c_re, v_ssm_c_im, v_ssm_d, v_w_attn_proj, v_w_glu_v, v_w_glu_g, v_w_out, v_ffn2_norm, v_ffn2_w_gate, v_ffn2_w_up, v_ffn2_w_down, v_final_norm):
    names = ("meta_tokens", "ffn1_norm", "ffn1_w_gate", "ffn1_w_up", "ffn1_w_down", "mix_norm", "w_in", "attn_sinks",
             "ssm_a_re", "ssm_a_im", "ssm_log_dt", "ssm_b_re", "ssm_b_im", "ssm_c_re", "ssm_c_im", "ssm_d",
             "w_attn_proj", "w_glu_v", "w_glu_g", "w_out", "ffn2_norm", "ffn2_w_gate", "ffn2_w_up", "ffn2_w_down",
             "final_norm")
    weights = dict(zip(names, (meta_tokens, ffn1_norm, ffn1_w_gate, ffn1_w_up, ffn1_w_down, mix_norm, w_in, attn_sinks, ssm_a_re, ssm_a_im, ssm_log_dt, ssm_b_re, ssm_b_im, ssm_c_re, ssm_c_im, ssm_d, w_attn_proj, w_glu_v, w_glu_g, w_out, ffn2_norm, ffn2_w_gate, ffn2_w_up, ffn2_w_down, final_norm)))
    moments_m = dict(zip(names, (m_meta_tokens, m_ffn1_norm, m_ffn1_w_gate, m_ffn1_w_up, m_ffn1_w_down, m_mix_norm, m_w_in, m_attn_sinks, m_ssm_a_re, m_ssm_a_im, m_ssm_log_dt, m_ssm_b_re, m_ssm_b_im, m_ssm_c_re, m_ssm_c_im, m_ssm_d, m_w_attn_proj, m_w_glu_v, m_w_glu_g, m_w_out, m_ffn2_norm, m_ffn2_w_gate, m_ffn2_w_up, m_ffn2_w_down, m_final_norm)))
    moments_v = dict(zip(names, (v_meta_tokens, v_ffn1_norm, v_ffn1_w_gate, v_ffn1_w_up, v_ffn1_w_down, v_mix_norm, v_w_in, v_attn_sinks, v_ssm_a_re, v_ssm_a_im, v_ssm_log_dt, v_ssm_b_re, v_ssm_b_im, v_ssm_c_re, v_ssm_c_im, v_ssm_d, v_w_attn_proj, v_w_glu_v, v_w_glu_g, v_w_out, v_ffn2_norm, v_ffn2_w_gate, v_ffn2_w_up, v_ffn2_w_down, v_final_norm)))
    me = _my_index()

    gathers = {}
    token = jnp.zeros((8, 128), F32)
    for i in range(DEPTH):
        for part, ks in PARTS.items():
            shards = [_to_rows(k, weights[k][i]).astype(BF16) for k in ks]
            if (i, part) == (0, "ffn1"):
                shards.append(meta_tokens)
            ex = Exchange(shards, False, f"gather_{part}_{i}")
            state, token = ex.start(shards, token)
            gathers[i, part] = (ex, state, shards)
    all_started = token

    def get_weights(i, part, after):
        ex, state, shards = gathers[i, part]
        fulls = ex.wait(state, all_started if after is None else after)
        got = dict(zip(PARTS[part], fulls))
        if (i, part) == (0, "ffn1"):
            got["meta_tokens"] = jnp.swapaxes(fulls[-1].reshape(N_DEV, N_META, 128), 0, 1).reshape(N_META, D_MODEL)
        return got

    scatters = []

    def put_grads(i, part, gdict):
        ks = PARTS[part]
        srcs = [gdict[k] for k in ks]
        ex = Exchange(srcs, True, f"scatter_{part}_{i}")
        state, tok = ex.start(srcs, all_started)
        scatters.append((i, ks, ex, state))
        return tok

    small = {k: weights[k] for k in SMALL}
    loss, dx, dmeta, gs = local_step(x[0], loss_target[0], get_weights, put_grads, small)

    landed = {}
    for i, ks, ex, state in scatters:
        for k, slots in zip(ks, ex.wait(state, dx)):
            landed[k, i] = slots
    grads = {}
    for part in PARTS.values():
        for k in part:
            rows = jnp.stack([sum_slots(landed[k, i], f"sum_{k}_{i}") for i in range(DEPTH)])
            grads[k] = _to_rows(k, rows)

    small_list = [loss.reshape(1), dmeta] + [gs[k] for k in SMALL]
    packed = _pack_rows(small_list, D_MODEL)
    (packed_all,) = all_gather([packed], "gather_small")
    total = sum_slots(packed_all, "sum_small")
    pieces = _unpack_rows(total, [a.shape for a in small_list])
    loss_out = pieces[0][0]
    grads["meta_tokens"] = lax.dynamic_slice_in_dim(pieces[1], me * 128, 128, axis=1)
    for k, p in zip(SMALL, pieces[2:]):
        grads[k] = p

    deltas, new_m, new_v = {}, {}, {}
    for k in names:
        deltas[k], new_m[k], new_v[k] = adamw(weights[k], grads[k], moments_m[k], moments_v[k], f"adamw_{k}")
    return (loss_out, dx[None], *[grads[k] for k in names], *[deltas[k] for k in names],
            *[new_m[k] for k in names], *[new_v[k] for k in names])
```

```python
import functools
import math

import jax
import jax.numpy as jnp
from jax import lax
from jax.experimental import pallas as pl
from jax.experimental.pallas import tpu as pltpu

F32 = jnp.float32
BF16 = jnp.bfloat16

D_MODEL = 1024
DEPTH = 2
N_META = 16
HEAD_DIM = 64
N_Q_HEADS = 8
ATTN_WIDTH = 512
KV_WIDTH = 128
QKV_WIDTH = ATTN_WIDTH + 2 * KV_WIDTH
WINDOW = 128
BLK = 128
ROPE_THETA = 500000.0
ROT_DIM = 16
SSM_WIDTH = 512
SSM_GROUP = 16
SSM_GROUPS = 32
SSM_STATE = 64
STATE_WIDTH = SSM_GROUPS * SSM_STATE
D_FF = 2816
IN_WIDTH = 3328
EPS = 1e-6
NEG_INF = -1e30
PAD_FRONT = (-N_META) % BLK
N_DEV = 8

ADAM_LR = 0.001
ADAM_B1 = 0.9
ADAM_B2 = 0.999
ADAM_EPS = 1e-08
ADAM_WD = 0.01
ADAM_STEP = 10

VMEM_LIMIT = 56 * 1024 * 1024
TOKEN_TILE = 384
_VMEM = pl.BlockSpec(memory_space=pltpu.VMEM)
_SMEM = pl.BlockSpec(memory_space=pltpu.SMEM)
_ANY = pl.BlockSpec(memory_space=pl.ANY)
MESH = pl.DeviceIdType.MESH


def _params(sem=None):
    return pltpu.CompilerParams(dimension_semantics=sem, vmem_limit_bytes=VMEM_LIMIT)


def _nt(a, b):
    return lax.dot_general(a, b, (((1,), (1,)), ((), ())), preferred_element_type=F32)


def _nn(a, b):
    return jnp.dot(a, b, preferred_element_type=F32)


def _tn(a, b):
    return lax.dot_general(a, b, (((0,), (0,)), ((), ())), preferred_element_type=F32)


def _row_spec(tm, width):
    return pl.BlockSpec((tm, width), lambda i: (i, 0))


def _acc_spec(shape):
    return pl.BlockSpec(shape, lambda i: (0,) * len(shape))


def _rms_stats(x):
    r = lax.rsqrt(jnp.mean(x * x, axis=-1, keepdims=True) + EPS)
    return x * r, r


def _rms_bwd(dn, xh, r, g):
    dg = jnp.sum(dn * xh, axis=0, keepdims=True)
    dxh = dn * g
    dx = r * (dxh - xh * jnp.mean(dxh * xh, axis=-1, keepdims=True))
    return dx, dg


def ffn_fwd(h, g, wg_t, wu_t, wd, name):
    t, d = h.shape
    f = wd.shape[0]
    tm = TOKEN_TILE

    def body(h_ref, g_ref, wg_ref, wu_ref, wd_ref, ho_ref, n_ref, a_ref, b_ref):
        x = h_ref[...]
        xh, _ = _rms_stats(x)
        n = (xh * g_ref[...]).astype(BF16)
        n_ref[...] = n
        a = _nt(n, wg_ref[...])
        b = _nt(n, wu_ref[...])
        a_ref[...] = a.astype(BF16)
        b_ref[...] = b.astype(BF16)
        s = (a * jax.nn.sigmoid(a) * b).astype(BF16)
        ho_ref[...] = x + 0.5 * _nn(s, wd_ref[...])

    return pl.pallas_call(
        body, name=name, grid=(t // tm,),
        in_specs=[_row_spec(tm, d), _acc_spec((1, d)), _VMEM, _VMEM, _VMEM],
        out_specs=[_row_spec(tm, d), _row_spec(tm, d), _row_spec(tm, f), _row_spec(tm, f)],
        out_shape=[jax.ShapeDtypeStruct((t, d), F32), jax.ShapeDtypeStruct((t, d), BF16),
                   jax.ShapeDtypeStruct((t, f), BF16), jax.ShapeDtypeStruct((t, f), BF16)],
        compiler_params=_params(("arbitrary",)),
    )(h, g, wg_t, wu_t, wd)


def ffn_bwd(dh, h, g, a, b, wg_t, wu_t, wd, dep, name):
    t, d = h.shape
    f = wd.shape[0]
    tm = TOKEN_TILE // 2

    def body(dh_ref, h_ref, g_ref, a_ref, b_ref, wg_ref, wu_ref, wd_ref, dep_ref,
             dhi_ref, da_ref, db_ref, s_ref, dhb_ref, dg_ref):
        dh_t = dh_ref[...]
        dhb = (0.5 * dh_t).astype(BF16)
        dhb_ref[...] = dhb
        ds = _nt(dhb, wd_ref[...])
        av = a_ref[...].astype(F32)
        bv = b_ref[...].astype(F32)
        sig = jax.nn.sigmoid(av)
        sl = av * sig
        s_ref[...] = (sl * bv).astype(BF16)
        da = (ds * bv * (sig * (1.0 + av * (1.0 - sig)))).astype(BF16)
        db = (ds * sl).astype(BF16)
        da_ref[...] = da
        db_ref[...] = db
        dn = _nn(da, wg_ref[...]) + _nn(db, wu_ref[...])
        xh, r = _rms_stats(h_ref[...])
        dx, dg = _rms_bwd(dn, xh, r, g_ref[...])
        dhi_ref[...] = dh_t + dx

        @pl.when(pl.program_id(0) == 0)
        def _():
            dg_ref[...] = jnp.zeros_like(dg_ref)

        dg_ref[...] += dg

    return pl.pallas_call(
        body, name=name, grid=(t // tm,),
        in_specs=[_row_spec(tm, d), _row_spec(tm, d), _acc_spec((1, d)), _row_spec(tm, f), _row_spec(tm, f),
                  _VMEM, _VMEM, _VMEM, _ANY],
        out_specs=[_row_spec(tm, d), _row_spec(tm, f), _row_spec(tm, f), _row_spec(tm, f), _row_spec(tm, d),
                   _acc_spec((1, d))],
        out_shape=[jax.ShapeDtypeStruct((t, d), F32), jax.ShapeDtypeStruct((t, f), BF16),
                   jax.ShapeDtypeStruct((t, f), BF16), jax.ShapeDtypeStruct((t, f), BF16),
                   jax.ShapeDtypeStruct((t, d), BF16), jax.ShapeDtypeStruct((1, d), F32)],
        compiler_params=_params(("arbitrary",)),
    )(dh, h, g, a, b, wg_t, wu_t, wd, dep)


def _col_tile(m):
    for bm in (1408, 1664, 1024, 768, 512):
        if m % bm == 0:
            return bm
    raise ValueError(m)


def tn_matmul(x, y, name):
    t, m = x.shape
    n = y.shape[1]
    bm = _col_tile(m)
    bt = TOKEN_TILE
    nk = t // bt

    def body(x_ref, y_ref, o_ref, acc_ref):
        k = pl.program_id(1)

        @pl.when(k == 0)
        def _():
            acc_ref[...] = jnp.zeros_like(acc_ref)

        acc_ref[...] += _tn(x_ref[...], y_ref[...])

        @pl.when(k == nk - 1)
        def _():
            o_ref[...] = acc_ref[...].astype(BF16)

    return pl.pallas_call(
        body, name=name, grid=(m // bm, nk),
        in_specs=[pl.BlockSpec((bt, bm), lambda i, k: (k, i)), pl.BlockSpec((bt, n), lambda i, k: (k, 0))],
        out_specs=pl.BlockSpec((bm, n), lambda i, k: (i, 0)),
        out_shape=jax.ShapeDtypeStruct((m, n), BF16),
        scratch_shapes=[pltpu.VMEM((bm, n), F32)],
        compiler_params=_params(("arbitrary", "arbitrary")),
    )(x, y)


def head_fwd_bwd(h, g, tgt):
    t, d = h.shape

    def body(h_ref, g_ref, t_ref, loss_ref, dh_ref, dg_ref):
        i = pl.program_id(0)
        xh, r = _rms_stats(h_ref[...])
        gv = g_ref[...]
        valid = (i > 0).astype(F32)
        e = (xh * gv - t_ref[...]) * valid
        dx, dg = _rms_bwd(e * (1.0 / d), xh, r, gv)
        dh_ref[...] = dx

        @pl.when(i == 0)
        def _():
            dg_ref[...] = jnp.zeros_like(dg_ref)
            loss_ref[...] = jnp.zeros_like(loss_ref)

        dg_ref[...] += dg
        loss_ref[...] += jnp.sum(e * e) * (0.5 / d)

    return pl.pallas_call(
        body, name="head", grid=(t // BLK,),
        in_specs=[_row_spec(BLK, d), _acc_spec((1, d)),
                  pl.BlockSpec((BLK, d), lambda i: (jnp.maximum(i - 1, 0), 0))],
        out_specs=[_acc_spec((1, 128)), _row_spec(BLK, d), _acc_spec((1, d))],
        out_shape=[jax.ShapeDtypeStruct((1, 128), F32), jax.ShapeDtypeStruct((t, d), F32),
                   jax.ShapeDtypeStruct((1, d), F32)],
        compiler_params=_params(("arbitrary",)),
    )(h, g, tgt)


def rope_tables(t):
    pos = jnp.arange(t, dtype=F32) - PAD_FRONT
    inv_freq = ROPE_THETA ** (-jnp.arange(0, ROT_DIM, 2, dtype=F32) / ROT_DIM)
    ang = pos[:, None] * inv_freq[None, :]
    cos, sin = jnp.cos(ang), jnp.sin(ang)
    ones = jnp.ones((t, HEAD_DIM - ROT_DIM), F32)
    cos_h = jnp.concatenate([cos, cos, ones], axis=1)
    sin_h = jnp.concatenate([-sin, sin, 0.0 * ones], axis=1)
    return jnp.concatenate([cos_h, cos_h], axis=1), jnp.concatenate([sin_h, sin_h], axis=1)


def _swap_halves(x):
    n = x.shape[1]
    lane = lax.broadcasted_iota(jnp.int32, x.shape, 1)
    return jnp.where(lane % HEAD_DIM < ROT_DIM // 2, pltpu.roll(x, n - ROT_DIM // 2, 1), pltpu.roll(x, ROT_DIM // 2, 1))


def _rope(x, cos_t, sin_t, sign):
    return x * cos_t + sign * (_swap_halves(x) * sin_t)


def win_fwd(h, g, win_t, cos_t, sin_t, name):
    t, d = h.shape
    tm = TOKEN_TILE

    def body(h_ref, g_ref, w_ref, c_ref, s_ref, n_ref, qkv_ref, u_ref, gates_ref):
        xh, _ = _rms_stats(h_ref[...])
        n = (xh * g_ref[...]).astype(BF16)
        n_ref[...] = n
        z = _nt(n, w_ref[...])
        c, s = c_ref[...], s_ref[...]
        for j in range((ATTN_WIDTH + KV_WIDTH) // 128):
            qkv_ref[:, j * 128:(j + 1) * 128] = _rope(z[:, j * 128:(j + 1) * 128], c, s, 1.0).astype(BF16)
        qkv_ref[:, ATTN_WIDTH + KV_WIDTH:QKV_WIDTH] = z[:, ATTN_WIDTH + KV_WIDTH:QKV_WIDTH].astype(BF16)
        u_ref[...] = z[:, QKV_WIDTH:QKV_WIDTH + SSM_WIDTH]
        gates_ref[...] = z[:, QKV_WIDTH + SSM_WIDTH:]

    return pl.pallas_call(
        body, name=name, grid=(t // tm,),
        in_specs=[_row_spec(tm, d), _acc_spec((1, d)), _VMEM, _row_spec(tm, 128), _row_spec(tm, 128)],
        out_specs=[_row_spec(tm, d), _row_spec(tm, QKV_WIDTH), _row_spec(tm, SSM_WIDTH), _row_spec(tm, 2 * d)],
        out_shape=[jax.ShapeDtypeStruct((t, d), BF16), jax.ShapeDtypeStruct((t, QKV_WIDTH), BF16),
                   jax.ShapeDtypeStruct((t, SSM_WIDTH), F32), jax.ShapeDtypeStruct((t, 2 * d), F32)],
        compiler_params=_params(("arbitrary",)),
    )(h, g, win_t, cos_t, sin_t)


def win_bwd(dh, h, g, dz, win_t, dep, name):
    t, d = h.shape
    tm = TOKEN_TILE

    def body(dh_ref, h_ref, g_ref, dz_ref, w_ref, dep_ref, dhi_ref, dg_ref):
        dn = _nn(dz_ref[...], w_ref[...])
        xh, r = _rms_stats(h_ref[...])
        dx, dg = _rms_bwd(dn, xh, r, g_ref[...])
        dhi_ref[...] = dh_ref[...] + dx

        @pl.when(pl.program_id(0) == 0)
        def _():
            dg_ref[...] = jnp.zeros_like(dg_ref)

        dg_ref[...] += dg

    return pl.pallas_call(
        body, name=name, grid=(t // tm,),
        in_specs=[_row_spec(tm, d), _row_spec(tm, d), _acc_spec((1, d)), _row_spec(tm, IN_WIDTH), _VMEM, _ANY],
        out_specs=[_row_spec(tm, d), _acc_spec((1, d))],
        out_shape=[jax.ShapeDtypeStruct((t, d), F32), jax.ShapeDtypeStruct((1, d), F32)],
        compiler_params=_params(("arbitrary",)),
    )(dh, h, g, dz, win_t, dep)


def _attn_mask(blk):
    q_pos = blk * BLK + lax.broadcasted_iota(jnp.int32, (BLK, 3 * BLK), 0) - PAD_FRONT
    col = lax.broadcasted_iota(jnp.int32, (BLK, 3 * BLK), 1)
    part = col // BLK
    k_pos = jnp.where(part == 0, col, (blk + part - 2) * BLK + (col - part * BLK)) - PAD_FRONT
    dist = q_pos - k_pos
    meta_ok = (part == 0) & (k_pos >= 0) & (dist >= 0)
    band_ok = (part > 0) & (k_pos >= N_META) & (dist >= 0) & (dist < WINDOW)
    return meta_ok | band_ok


def _head_halves(x128, kv):
    x = x128.astype(F32)
    lane = lax.broadcasted_iota(jnp.int32, x.shape, 1)
    swapped = pltpu.roll(x, HEAD_DIM, 1)
    lo, hi = (x, swapped) if kv == 0 else (swapped, x)
    return jnp.where(lane < HEAD_DIM, lo, 0.0).astype(BF16), jnp.where(lane >= HEAD_DIM, hi, 0.0).astype(BF16)


def _gather_keys(meta_ref, prev_ref, cur_ref, lo):
    return jnp.concatenate([meta_ref[:, lo:lo + 128], prev_ref[:, lo:lo + 128], cur_ref[:, lo:lo + 128]], axis=0)


def _softmax_with_sink(s, mask, sink):
    s = jnp.where(mask, s * (HEAD_DIM ** -0.5), NEG_INF)
    m = jnp.maximum(jnp.max(s, axis=-1, keepdims=True), sink)
    p = jnp.exp(s - m)
    p_sink = jnp.exp(sink - m)
    inv = 1.0 / (jnp.sum(p, axis=-1, keepdims=True) + p_sink)
    return p * inv, p_sink * inv


def attn_fwd(qkv, sinks, name):
    t = qkv.shape[0]
    nb = t // BLK

    def body(sink_ref, meta_ref, prev_ref, cur_ref, o_ref):
        blk = pl.program_id(0)
        mask = _attn_mask(blk)
        k128 = _gather_keys(meta_ref, prev_ref, cur_ref, ATTN_WIDTH)
        v128 = _gather_keys(meta_ref, prev_ref, cur_ref, ATTN_WIDTH + KV_WIDTH)
        for kv in range(2):
            k_lo, k_hi = _head_halves(k128, kv)
            v_lo, v_hi = _head_halves(v128, kv)
            for pair in range(2):
                lanes = slice((2 * kv + pair) * 128, (2 * kv + pair + 1) * 128)
                q128 = cur_ref[:, lanes]
                head = 4 * kv + 2 * pair
                p_a, _ = _softmax_with_sink(_nt(q128, k_lo), mask, sink_ref[0, head])
                p_b, _ = _softmax_with_sink(_nt(q128, k_hi), mask, sink_ref[0, head + 1])
                o_ref[:, lanes] = (_nn(p_a.astype(BF16), v_lo) + _nn(p_b.astype(BF16), v_hi)).astype(BF16)

    blk_spec = lambda f: pl.BlockSpec((BLK, QKV_WIDTH), f)
    return pl.pallas_call(
        body, name=name, grid=(nb,),
        in_specs=[_SMEM, blk_spec(lambda i: (0, 0)), blk_spec(lambda i: (jnp.maximum(i - 1, 0), 0)),
                  blk_spec(lambda i: (i, 0))],
        out_specs=_row_spec(BLK, ATTN_WIDTH),
        out_shape=jax.ShapeDtypeStruct((t, ATTN_WIDTH), BF16),
        compiler_params=_params(("arbitrary",)),
    )(sinks, qkv, qkv, qkv)


def attn_bwd(qkv, do, sinks, cos_t, sin_t, name):
    t = qkv.shape[0]
    nb = t // BLK

    def body(sink_ref, meta_ref, prev_ref, cur_ref, do_ref, c_ref, s_ref, dqkv_ref, dsink_ref, carry_ref, macc_ref):
        step = pl.program_id(0)
        blk = nb - 1 - step

        @pl.when(step == 0)
        def _():
            dsink_ref[...] = jnp.zeros_like(dsink_ref)
            carry_ref[...] = jnp.zeros_like(carry_ref)
            macc_ref[...] = jnp.zeros_like(macc_ref)

        mask = _attn_mask(blk)
        lane = lax.broadcasted_iota(jnp.int32, (3 * BLK, 128), 1)
        k128 = _gather_keys(meta_ref, prev_ref, cur_ref, ATTN_WIDTH)
        v128 = _gather_keys(meta_ref, prev_ref, cur_ref, ATTN_WIDTH + KV_WIDTH)
        cos_b, sin_b = c_ref[...], s_ref[...]
        dk_heads, dv_heads = [], []
        for kv in range(2):
            k_lo, k_hi = _head_halves(k128, kv)
            v_lo, v_hi = _head_halves(v128, kv)
            dk_acc = jnp.zeros((3 * BLK, 128), F32)
            dv_acc = jnp.zeros((3 * BLK, 128), F32)
            for pair in range(2):
                lanes = slice((2 * kv + pair) * 128, (2 * kv + pair + 1) * 128)
                q128 = cur_ref[:, lanes]
                do128 = do_ref[:, lanes]
                head = 4 * kv + 2 * pair
                ds_pair, p_pair = [], []
                for half, (k_h, v_h) in enumerate(((k_lo, v_lo), (k_hi, v_hi))):
                    p, p_sink = _softmax_with_sink(_nt(q128, k_h), mask, sink_ref[0, head + half])
                    dp = _nt(do128, v_h)
                    dsum = jnp.sum(p * dp, axis=-1, keepdims=True)
                    ds_pair.append((p * (dp - dsum) * (HEAD_DIM ** -0.5)).astype(BF16))
                    p_pair.append(p.astype(BF16))
                    dsink = -jnp.sum(p_sink * dsum, axis=0, keepdims=True)
                    dsink_ref[head + half:head + half + 1, :] += jnp.broadcast_to(dsink, (1, 128))
                dq = _nn(ds_pair[0], k_lo) + _nn(ds_pair[1], k_hi)
                dqkv_ref[:, lanes] = _rope(dq, cos_b, sin_b, -1.0).astype(BF16)
                dk_acc += jnp.where(lane < HEAD_DIM, _tn(ds_pair[0], q128), _tn(ds_pair[1], q128))
                dv_acc += jnp.where(lane < HEAD_DIM, _tn(p_pair[0], do128), _tn(p_pair[1], do128))
            dk_heads.append(dk_acc + pltpu.roll(dk_acc, HEAD_DIM, 1))
            dv_heads.append(dv_acc + pltpu.roll(dv_acc, HEAD_DIM, 1))
        dkv = jnp.concatenate([jnp.where(lane < HEAD_DIM, dk_heads[0], dk_heads[1]),
                               jnp.where(lane < HEAD_DIM, dv_heads[0], dv_heads[1])], axis=1)
        macc_ref[...] += dkv[0:BLK]
        is_last = (blk == 0).astype(F32)
        mine = dkv[2 * BLK:3 * BLK] + carry_ref[...] + is_last * macc_ref[...]
        carry_ref[...] = dkv[BLK:2 * BLK]
        dqkv_ref[:, ATTN_WIDTH:ATTN_WIDTH + KV_WIDTH] = _rope(mine[:, 0:128], cos_b, sin_b, -1.0).astype(BF16)
        dqkv_ref[:, ATTN_WIDTH + KV_WIDTH:QKV_WIDTH] = mine[:, 128:256].astype(BF16)

    rev = lambda i: nb - 1 - i
    blk_spec = lambda f: pl.BlockSpec((BLK, QKV_WIDTH), f)
    return pl.pallas_call(
        body, name=name, grid=(nb,),
        in_specs=[_SMEM, blk_spec(lambda i: (0, 0)), blk_spec(lambda i: (jnp.maximum(rev(i) - 1, 0), 0)),
                  blk_spec(lambda i: (rev(i), 0)), pl.BlockSpec((BLK, ATTN_WIDTH), lambda i: (rev(i), 0)),
                  pl.BlockSpec((BLK, 128), lambda i: (rev(i), 0)), pl.BlockSpec((BLK, 128), lambda i: (rev(i), 0))],
        out_specs=[pl.BlockSpec((BLK, QKV_WIDTH), lambda i: (rev(i), 0)), _acc_spec((N_Q_HEADS, 128))],
        out_shape=[jax.ShapeDtypeStruct((t, QKV_WIDTH), BF16), jax.ShapeDtypeStruct((N_Q_HEADS, 128), F32)],
        scratch_shapes=[pltpu.VMEM((BLK, 256), F32), pltpu.VMEM((BLK, 256), F32)],
        compiler_params=_params(("arbitrary",)),
    )(sinks, qkv, qkv, qkv, do, cos_t, sin_t)


def _cmul(ar, ai, br, bi):
    return ar * br - ai * bi, ar * bi + ai * br


def ssm_prep(a_re, a_im, log_dt, b_re_t, b_im_t, name):
    def body(ar_ref, ai_ref, ldt_ref, br_ref, bi_ref, lr_ref, li_ref, bbr_ref, bbi_ref):
        ar, ai = ar_ref[...], ai_ref[...]
        dt = jnp.exp(ldt_ref[...])
        mag = jnp.exp(ar * dt)
        lr = mag * jnp.cos(ai * dt)
        li = mag * jnp.sin(ai * dt)
        den = ar * ar + ai * ai
        nr = lr - 1.0
        cr = ((nr * ar + li * ai) / den)[:, None, :]
        ci = ((li * ar - nr * ai) / den)[:, None, :]
        br, bi = br_ref[...], bi_ref[...]
        lr_ref[...] = lr
        li_ref[...] = li
        bbr_ref[...] = cr * br - ci * bi
        bbi_ref[...] = cr * bi + ci * br

    gp = jax.ShapeDtypeStruct(a_re.shape, F32)
    gcp = jax.ShapeDtypeStruct(b_re_t.shape, F32)
    return pl.pallas_call(body, name=name, out_shape=[gp, gp, gcp, gcp],
                          in_specs=[_VMEM] * 5, out_specs=[_VMEM] * 4)(a_re, a_im, log_dt, b_re_t, b_im_t)


def ssm_prep_bwd(a_re, a_im, log_dt, b_re_t, b_im_t, dl_re, dl_im, dbb_re, dbb_im, name):
    def body(ar_ref, ai_ref, ldt_ref, br_ref, bi_ref, dlr_ref, dli_ref, dbbr_ref, dbbi_ref,
             dar_ref, dai_ref, dldt_ref, dbr_ref, dbi_ref):
        ar, ai = ar_ref[...], ai_ref[...]
        dt = jnp.exp(ldt_ref[...])
        mag = jnp.exp(ar * dt)
        lr = mag * jnp.cos(ai * dt)
        li = mag * jnp.sin(ai * dt)
        den = ar * ar + ai * ai
        nr = lr - 1.0
        cr = (nr * ar + li * ai) / den
        ci = (li * ar - nr * ai) / den
        br, bi = br_ref[...], bi_ref[...]
        dbbr, dbbi = dbbr_ref[...], dbbi_ref[...]
        dbr_ref[...] = cr[:, None, :] * dbbr + ci[:, None, :] * dbbi
        dbi_ref[...] = cr[:, None, :] * dbbi - ci[:, None, :] * dbbr
        dcr = jnp.sum(br * dbbr + bi * dbbi, axis=1)
        dci = jnp.sum(br * dbbi - bi * dbbr, axis=1)
        d_num_r = dcr / den
        d_num_i = dci / den
        d_den = -(dcr * cr + dci * ci) / den
        d_lr = dlr_ref[...] + d_num_r * ar - d_num_i * ai
        d_li = dli_ref[...] + d_num_r * ai + d_num_i * ar
        d_ar = d_num_r * nr + d_num_i * li + d_den * 2.0 * ar
        d_ai = d_num_r * li - d_num_i * nr + d_den * 2.0 * ai
        d_mag = (d_lr * lr + d_li * li) / mag
        d_theta = d_li * lr - d_lr * li
        d_ardt = d_mag * mag
        dar_ref[...] = d_ar + d_ardt * dt
        dai_ref[...] = d_ai + d_theta * dt
        d_dt = jnp.sum(d_ardt * ar + d_theta * ai, axis=1, keepdims=True)
        dldt_ref[...] = d_dt * dt

    gp = jax.ShapeDtypeStruct(a_re.shape, F32)
    gcp = jax.ShapeDtypeStruct(b_re_t.shape, F32)
    return pl.pallas_call(body, name=name, out_shape=[gp, gp, jax.ShapeDtypeStruct(log_dt.shape, F32), gcp, gcp],
                          in_specs=[_VMEM] * 9, out_specs=[_VMEM] * 5,
                          )(a_re, a_im, log_dt, b_re_t, b_im_t, dl_re, dl_im, dbb_re, dbb_im)


N_CHUNK = 4
U_CHUNK = SSM_WIDTH // N_CHUNK
H_CHUNK = STATE_WIDTH // N_CHUNK
SUB = 8


def _block_diag_b(bb):
    x = bb.reshape(N_CHUNK, 8, SSM_GROUP, 1, SSM_STATE)
    same = (jnp.arange(8)[:, None] == jnp.arange(8)[None, :])[None, :, None, :, None]
    return jnp.where(same, x, 0.0).reshape(N_CHUNK, U_CHUNK, H_CHUNK)


def _block_diag_c(c):
    x = jnp.swapaxes(c.reshape(N_CHUNK, 8, SSM_GROUP, SSM_STATE), 2, 3)[:, :, :, None, :]
    same = (jnp.arange(8)[:, None] == jnp.arange(8)[None, :])[None, :, None, :, None]
    return jnp.where(same, x, 0.0).reshape(N_CHUNK, H_CHUNK, U_CHUNK)


def _diag_of_b(m):
    x = m.reshape(N_CHUNK, 8, SSM_GROUP, 8, SSM_STATE)
    return jnp.stack([x[:, g, :, g, :] for g in range(8)], axis=1).reshape(SSM_GROUPS, SSM_GROUP, SSM_STATE)


def _diag_of_c(m):
    x = m.reshape(N_CHUNK, 8, SSM_STATE, 8, SSM_GROUP)
    d = jnp.stack([x[:, g, :, g, :] for g in range(8)], axis=1)
    return jnp.swapaxes(d, 2, 3).reshape(SSM_GROUPS, SSM_GROUP, SSM_STATE)


def _lambda_tables(lr, li, reverse):
    p1 = (lr, li)
    p2 = _cmul(*p1, *p1)
    p4 = _cmul(*p2, *p2)
    rows = [p1]
    for _ in range(SUB - 1):
        rows.append(_cmul(*rows[-1], *p1))
    if reverse:
        rows = rows[::-1]
    return p1, p2, p4, (jnp.concatenate([r[0] for r in rows], axis=0), jnp.concatenate([r[1] for r in rows], axis=0))


def _scan8(xr, xi, pows, table, cr, ci, reverse):
    row = lax.broadcasted_iota(jnp.int32, xr.shape, 0)
    for d, (pr, pi) in zip((1, 2, 4), pows):
        if reverse:
            sr, si = pltpu.roll(xr, SUB - d, 0), pltpu.roll(xi, SUB - d, 0)
            keep = row < SUB - d
        else:
            sr, si = pltpu.roll(xr, d, 0), pltpu.roll(xi, d, 0)
            keep = row >= d
        sr = jnp.where(keep, sr, 0.0)
        si = jnp.where(keep, si, 0.0)
        xr, xi = xr + pr * sr - pi * si, xi + pr * si + pi * sr
    tr, ti = table
    return xr + tr * cr - ti * ci, xi + tr * ci + ti * cr


def _gelu_and_grad(y):
    k0 = math.sqrt(2.0 / math.pi)
    inner = k0 * (y + 0.044715 * y * y * y)
    th = jnp.tanh(inner)
    g = 0.5 * y * (1.0 + th)
    dg = 0.5 * (1.0 + th) + 0.5 * y * (1.0 - th * th) * k0 * (1.0 + 3.0 * 0.044715 * y * y)
    return g, dg


def ssm_fwd(u, lam_re, lam_im, bb_re, bb_im, cc_re, cc_im, d_skip, name):
    t = u.shape[0]
    tt = BLK

    def body(u_ref, lr_ref, li_ref, bbr_ref, bbi_ref, ccr_ref, cci_ref, d_ref, yg_ref, hr_ref, hi_ref, cr_ref, ci_ref):
        @pl.when(pl.program_id(0) == 0)
        def _():
            cr_ref[...] = jnp.zeros_like(cr_ref)
            ci_ref[...] = jnp.zeros_like(ci_ref)

        uv = u_ref[...]
        ub = uv.astype(BF16)
        for j in range(N_CHUNK):
            hs = slice(j * H_CHUNK, (j + 1) * H_CHUNK)
            us = slice(j * U_CHUNK, (j + 1) * U_CHUNK)
            hr_ref[:, hs] = _nn(ub[:, us], bbr_ref[j])
            hi_ref[:, hs] = _nn(ub[:, us], bbi_ref[j])
        p1, p2, p4, table = _lambda_tables(lr_ref[...], li_ref[...], False)

        def group(i, carry):
            cr, ci = carry
            rows = pl.ds(pl.multiple_of(i * SUB, SUB), SUB)
            xr, xi = _scan8(hr_ref[rows, :], hi_ref[rows, :], (p1, p2, p4), table, cr, ci, False)
            hr_ref[rows, :] = xr
            hi_ref[rows, :] = xi
            return xr[SUB - 1:SUB, :], xi[SUB - 1:SUB, :]

        cr, ci = lax.fori_loop(0, tt // SUB, group, (cr_ref[...], ci_ref[...]))
        cr_ref[...] = cr
        ci_ref[...] = ci
        for j in range(N_CHUNK):
            hs = slice(j * H_CHUNK, (j + 1) * H_CHUNK)
            us = slice(j * U_CHUNK, (j + 1) * U_CHUNK)
            y = (_nn(hr_ref[:, hs].astype(BF16), ccr_ref[j]) - _nn(hi_ref[:, hs].astype(BF16), cci_ref[j])
                 + d_ref[:, us] * uv[:, us])
            yg_ref[:, us] = _gelu_and_grad(y)[0].astype(BF16)

    return pl.pallas_call(
        body, name=name, grid=(t // tt,),
        in_specs=[_row_spec(tt, SSM_WIDTH), _VMEM, _VMEM, _VMEM, _VMEM, _VMEM, _VMEM, _VMEM],
        out_specs=[_row_spec(tt, SSM_WIDTH), _row_spec(tt, STATE_WIDTH), _row_spec(tt, STATE_WIDTH)],
        out_shape=[jax.ShapeDtypeStruct((t, SSM_WIDTH), BF16), jax.ShapeDtypeStruct((t, STATE_WIDTH), F32),
                   jax.ShapeDtypeStruct((t, STATE_WIDTH), F32)],
        scratch_shapes=[pltpu.VMEM((1, STATE_WIDTH), F32), pltpu.VMEM((1, STATE_WIDTH), F32)],
        compiler_params=_params(("arbitrary",)),
    )(u, lam_re, lam_im, bb_re, bb_im, cc_re, cc_im, d_skip)


def ssm_bwd(dyg, u, h_re, h_im, lam_re, lam_im, bb_re, bb_im, cc_re, cc_im, d_skip, name):
    t = u.shape[0]
    tt = BLK
    nt = t // tt

    def body(dyg_ref, u_ref, hr_ref, hi_ref, lr_ref, li_ref, bbr_ref, bbi_ref, ccr_ref, cci_ref, d_ref,
             du_ref, dlr_ref, dli_ref, dbbr_ref, dbbi_ref, dccr_ref, dcci_ref, dd_ref,
             ar_ref, ai_ref, cr_ref, ci_ref):
        step = pl.program_id(0)
        tile = nt - 1 - step

        @pl.when(step == 0)
        def _():
            for ref in (cr_ref, ci_ref, dlr_ref, dli_ref, dbbr_ref, dbbi_ref, dccr_ref, dcci_ref, dd_ref):
                ref[...] = jnp.zeros_like(ref)

        uv = u_ref[...]
        ub = uv.astype(BF16)
        dskip = d_ref[...]
        dy_chunks = []
        for j in range(N_CHUNK):
            hs = slice(j * H_CHUNK, (j + 1) * H_CHUNK)
            us = slice(j * U_CHUNK, (j + 1) * U_CHUNK)
            hrb = hr_ref[:, hs].astype(BF16)
            hib = hi_ref[:, hs].astype(BF16)
            y = _nn(hrb, ccr_ref[j]) - _nn(hib, cci_ref[j]) + dskip[:, us] * uv[:, us]
            dy = dyg_ref[:, us] * _gelu_and_grad(y)[1]
            dy_chunks.append(dy)
            dyb = dy.astype(BF16)
            dccr_ref[j] += _tn(hrb, dyb)
            dcci_ref[j] -= _tn(hib, dyb)
            ar_ref[:, hs] = _nt(dyb, ccr_ref[j])
            ai_ref[:, hs] = -_nt(dyb, cci_ref[j])
        dy_all = jnp.concatenate(dy_chunks, axis=1)
        dd_ref[...] += jnp.sum(dy_all * uv, axis=0, keepdims=True)

        lr, li = lr_ref[...], li_ref[...]
        p1, p2, p4, table = _lambda_tables(lr, -li, True)
        last_row = lax.broadcasted_iota(jnp.int32, (SUB, STATE_WIDTH), 0) == SUB - 1

        def group(k, carry):
            cr, ci, accr, acci = carry
            i = tt // SUB - 1 - k
            rows = pl.ds(pl.multiple_of(i * SUB, SUB), SUB)
            xr, xi = _scan8(ar_ref[rows, :], ai_ref[rows, :], (p1, p2, p4), table, cr, ci, True)
            ar_ref[rows, :] = xr
            ai_ref[rows, :] = xi
            nr = jnp.where(last_row, cr, pltpu.roll(xr, SUB - 1, 0))
            ni = jnp.where(last_row, ci, pltpu.roll(xi, SUB - 1, 0))
            hr, hi = hr_ref[rows, :], hi_ref[rows, :]
            return xr[0:1, :], xi[0:1, :], accr + nr * hr + ni * hi, acci + ni * hr - nr * hi

        zero = jnp.zeros((SUB, STATE_WIDTH), F32)
        cr, ci, accr, acci = lax.fori_loop(0, tt // SUB, group, (cr_ref[...], ci_ref[...], zero, zero))
        cr_ref[...] = cr
        ci_ref[...] = ci
        dlr_ref[...] += accr
        dli_ref[...] += acci

        row = tile * tt + lax.broadcasted_iota(jnp.int32, (tt, U_CHUNK), 0)
        for j in range(N_CHUNK):
            hs = slice(j * H_CHUNK, (j + 1) * H_CHUNK)
            us = slice(j * U_CHUNK, (j + 1) * U_CHUNK)
            arb = ar_ref[:, hs].astype(BF16)
            aib = ai_ref[:, hs].astype(BF16)
            dbbr_ref[j] += _tn(ub[:, us], arb)
            dbbi_ref[j] += _tn(ub[:, us], aib)
            du = _nt(arb, bbr_ref[j]) + _nt(aib, bbi_ref[j]) + dy_chunks[j] * dskip[:, us]
            du_ref[:, us] = jnp.where(row >= PAD_FRONT, du, 0.0).astype(BF16)

    rev = lambda i: (nt - 1 - i, 0)
    full = lambda shape: pl.BlockSpec(shape, lambda i: (0,) * len(shape))
    return pl.pallas_call(
        body, name=name, grid=(nt,),
        in_specs=[pl.BlockSpec((tt, SSM_WIDTH), rev), pl.BlockSpec((tt, SSM_WIDTH), rev),
                  pl.BlockSpec((tt, STATE_WIDTH), rev), pl.BlockSpec((tt, STATE_WIDTH), rev),
                  _VMEM, _VMEM, _VMEM, _VMEM, _VMEM, _VMEM, _VMEM],
        out_specs=[pl.BlockSpec((tt, SSM_WIDTH), rev), full((SUB, STATE_WIDTH)), full((SUB, STATE_WIDTH)),
                   full((N_CHUNK, U_CHUNK, H_CHUNK)), full((N_CHUNK, U_CHUNK, H_CHUNK)),
                   full((N_CHUNK, H_CHUNK, U_CHUNK)), full((N_CHUNK, H_CHUNK, U_CHUNK)), full((1, SSM_WIDTH))],
        out_shape=[jax.ShapeDtypeStruct((t, SSM_WIDTH), BF16),
                   jax.ShapeDtypeStruct((SUB, STATE_WIDTH), F32), jax.ShapeDtypeStruct((SUB, STATE_WIDTH), F32),
                   jax.ShapeDtypeStruct((N_CHUNK, U_CHUNK, H_CHUNK), F32),
                   jax.ShapeDtypeStruct((N_CHUNK, U_CHUNK, H_CHUNK), F32),
                   jax.ShapeDtypeStruct((N_CHUNK, H_CHUNK, U_CHUNK), F32),
                   jax.ShapeDtypeStruct((N_CHUNK, H_CHUNK, U_CHUNK), F32),
                   jax.ShapeDtypeStruct((1, SSM_WIDTH), F32)],
        scratch_shapes=[pltpu.VMEM((tt, STATE_WIDTH), F32), pltpu.VMEM((tt, STATE_WIDTH), F32),
                        pltpu.VMEM((1, STATE_WIDTH), F32), pltpu.VMEM((1, STATE_WIDTH), F32)],
        compiler_params=_params(("arbitrary",)),
    )(dyg, u, h_re, h_im, lam_re, lam_im, bb_re, bb_im, cc_re, cc_im, d_skip)


def merge_fwd(h, o, yg, gates, wap_t, wv_t, wgg_t, wout, name):
    t, d = h.shape
    tm = TOKEN_TILE

    def body(h_ref, o_ref, yg_ref, gt_ref, wap_ref, wv_ref, wgg_ref, wout_ref, ho_ref, mg_ref, a_ref, sv_ref, sg_ref):
        att = _nt(o_ref[...], wap_ref[...])
        ygv = yg_ref[...]
        sv = _nt(ygv, wv_ref[...])
        sg = _nt(ygv, wgg_ref[...])
        a_ref[...] = att
        sv_ref[...] = sv
        sg_ref[...] = sg
        merged = (jax.nn.sigmoid(gt_ref[:, 0:d]) * att
                  + jax.nn.sigmoid(gt_ref[:, d:2 * d]) * (sv * jax.nn.sigmoid(sg))).astype(BF16)
        mg_ref[...] = merged
        ho_ref[...] = h_ref[...] + _nn(merged, wout_ref[...])

    return pl.pallas_call(
        body, name=name, grid=(t // tm,),
        in_specs=[_row_spec(tm, d), _row_spec(tm, ATTN_WIDTH), _row_spec(tm, SSM_WIDTH), _row_spec(tm, 2 * d),
                  _VMEM, _VMEM, _VMEM, _VMEM],
        out_specs=[_row_spec(tm, d), _row_spec(tm, d), _row_spec(tm, d), _row_spec(tm, d), _row_spec(tm, d)],
        out_shape=[jax.ShapeDtypeStruct((t, d), F32), jax.ShapeDtypeStruct((t, d), BF16),
                   jax.ShapeDtypeStruct((t, d), F32), jax.ShapeDtypeStruct((t, d), F32),
                   jax.ShapeDtypeStruct((t, d), F32)],
        compiler_params=_params(("arbitrary",)),
    )(h, o, yg, gates, wap_t, wv_t, wgg_t, wout)


def merge_bwd(dh, gates, att, sv, sg, wap_t, wv_t, wgg_t, wout, dep, name):
    t, d = dh.shape
    tm = TOKEN_TILE

    def body(dh_ref, gt_ref, a_ref, sv_ref, sg_ref, wap_ref, wv_ref, wgg_ref, wout_ref, dep_ref,
             dgt_ref, da_ref, dsv_ref, dsg_ref, do_ref, dyg_ref, dhb_ref):
        dhb = dh_ref[...].astype(BF16)
        dhb_ref[...] = dhb
        dm = _nt(dhb, wout_ref[...])
        sig_a = jax.nn.sigmoid(gt_ref[:, 0:d])
        sig_s = jax.nn.sigmoid(gt_ref[:, d:2 * d])
        sig_g = jax.nn.sigmoid(sg_ref[...])
        svv = sv_ref[...]
        dgt_ref[:, 0:d] = (dm * a_ref[...] * sig_a * (1.0 - sig_a)).astype(BF16)
        dgt_ref[:, d:2 * d] = (dm * (svv * sig_g) * sig_s * (1.0 - sig_s)).astype(BF16)
        da = (dm * sig_a).astype(BF16)
        d_s = dm * sig_s
        dsv = (d_s * sig_g).astype(BF16)
        dsg = (d_s * svv * sig_g * (1.0 - sig_g)).astype(BF16)
        da_ref[...] = da
        dsv_ref[...] = dsv
        dsg_ref[...] = dsg
        do_ref[...] = _nn(da, wap_ref[...]).astype(BF16)
        dyg_ref[...] = _nn(dsv, wv_ref[...]) + _nn(dsg, wgg_ref[...])

    return pl.pallas_call(
        body, name=name, grid=(t // tm,),
        in_specs=[_row_spec(tm, d), _row_spec(tm, 2 * d), _row_spec(tm, d), _row_spec(tm, d), _row_spec(tm, d),
                  _VMEM, _VMEM, _VMEM, _VMEM, _ANY],
        out_specs=[_row_spec(tm, 2 * d), _row_spec(tm, d), _row_spec(tm, d), _row_spec(tm, d),
                   _row_spec(tm, ATTN_WIDTH), _row_spec(tm, SSM_WIDTH), _row_spec(tm, d)],
        out_shape=[jax.ShapeDtypeStruct((t, 2 * d), BF16), jax.ShapeDtypeStruct((t, d), BF16),
                   jax.ShapeDtypeStruct((t, d), BF16), jax.ShapeDtypeStruct((t, d), BF16),
                   jax.ShapeDtypeStruct((t, ATTN_WIDTH), BF16), jax.ShapeDtypeStruct((t, SSM_WIDTH), F32),
                   jax.ShapeDtypeStruct((t, d), BF16)],
        compiler_params=_params(("arbitrary",)),
    )(dh, gates, att, sv, sg, wap_t, wv_t, wgg_t, wout, dep)


def adamw(w, g, m, v, name):
    shape = w.shape
    as2d = lambda a: a.reshape(-1, shape[-1]) if a.ndim >= 2 else a.reshape(1, -1)
    w2, g2, m2, v2 = as2d(w), as2d(g), as2d(m), as2d(v)
    rows, cols = w2.shape
    tr = rows
    for cand in (1024, 704, 512, 256):
        if rows > cand and rows % cand == 0:
            tr = cand
            break

    def body(w_ref, g_ref, m_ref, v_ref, d_ref, mo_ref, vo_ref):
        gv = g_ref[...]
        mn = ADAM_B1 * m_ref[...] + (1.0 - ADAM_B1) * gv
        vn = ADAM_B2 * v_ref[...] + (1.0 - ADAM_B2) * (gv * gv)
        m_hat = mn / (1.0 - ADAM_B1 ** ADAM_STEP)
        v_hat = vn / (1.0 - ADAM_B2 ** ADAM_STEP)
        d_ref[...] = -ADAM_LR * (m_hat / (jnp.sqrt(v_hat) + ADAM_EPS) + ADAM_WD * w_ref[...])
        mo_ref[...] = mn
        vo_ref[...] = vn

    spec = _row_spec(tr, cols)
    out = jax.ShapeDtypeStruct((rows, cols), F32)
    d, mn, vn = pl.pallas_call(
        body, name=name, grid=(rows // tr,), in_specs=[spec] * 4, out_specs=[spec] * 3, out_shape=[out] * 3,
        compiler_params=_params(("arbitrary",)),
    )(w2, g2, m2, v2)
    return d.reshape(shape), mn.reshape(shape), vn.reshape(shape)


def _my_index():
    return 4 * lax.axis_index("x") + 2 * lax.axis_index("y") + lax.axis_index("c")


def _peer(p):
    return (lax.axis_index("x") ^ ((p >> 2) & 1), lax.axis_index("y") ^ ((p >> 1) & 1), lax.axis_index("c") ^ (p & 1))


def all_gather(parts, name):
    n = len(parts)

    def body(*refs):
        srcs, dsts = refs[:n], refs[n:2 * n]
        send_sems, recv_sems, local_sems = refs[2 * n:]
        me = _my_index()
        local = [pltpu.make_async_copy(srcs[k], dsts[k].at[me], local_sems.at[k]) for k in range(n)]
        for cp in local:
            cp.start()
        copies = []
        for p in range(1, N_DEV):
            for k in range(n):
                copies.append(pltpu.make_async_remote_copy(
                    src_ref=srcs[k], dst_ref=dsts[k].at[me], send_sem=send_sems.at[p - 1, k],
                    recv_sem=recv_sems.at[p - 1, k], device_id=_peer(p), device_id_type=MESH))
        for cp in copies:
            cp.start()
        for cp in copies:
            cp.wait()
        for cp in local:
            cp.wait()

    return pl.pallas_call(
        body, name=name,
        in_specs=[_ANY] * n, out_specs=[_ANY] * n,
        out_shape=[jax.ShapeDtypeStruct((N_DEV,) + p.shape, p.dtype) for p in parts],
        scratch_shapes=[pltpu.SemaphoreType.DMA((N_DEV - 1, n)), pltpu.SemaphoreType.DMA((N_DEV - 1, n)),
                        pltpu.SemaphoreType.DMA((n,))],
    )(*parts)


_HBM = pl.BlockSpec(memory_space=pltpu.HBM)
_SEM = pl.BlockSpec(memory_space=pltpu.SEMAPHORE)
_EFFECT = pltpu.SideEffectType.DATAFLOW_SIDE_EFFECTING


class Exchange:
    def __init__(self, srcs, scatter, name):
        self.n = n = len(srcs)
        self.scatter = scatter
        self.name = name
        widths = sorted({s.shape[1] for s in srcs}, reverse=True)
        self.ncls = len(widths)
        self.cls = [widths.index(s.shape[1]) for s in srcs]
        self.cnts = [s.shape[0] // N_DEV if scatter else s.shape[0] for s in srcs]
        self.totals = [sum(c for c, k in zip(self.cnts, self.cls) if k == w) for w in range(self.ncls)]
        self.sizer = [max((k for k in range(n) if self.cls[k] == w), key=lambda k: self.cnts[k])
                      for w in range(self.ncls)]
        assert all(N_DEV * self.cnts[self.sizer[w]] >= self.totals[w] for w in range(self.ncls))
        if scatter:
            self.land_shapes = [(N_DEV, c, s.shape[1]) for s, c in zip(srcs, self.cnts)]
        else:
            self.land_shapes = [(N_DEV * c, s.shape[1]) for s, c in zip(srcs, self.cnts)]
        self.dtypes = [s.dtype for s in srcs]

    def _block(self, k, who):
        return pl.ds(pl.multiple_of(who * self.cnts[k], 16), self.cnts[k])

    def _sem(self, p, w):
        return (p - 1) * self.ncls + w

    def start(self, srcs, after):
        n = self.n

        def body(*refs):
            src, land = refs[:n], refs[n:2 * n]
            send_sems, recv_sems = refs[2 * n + 1], refs[2 * n + 2]
            token = refs[-1]
            me = _my_index()
            for p in range(1, N_DEV):
                for k in range(n):
                    if self.scatter:
                        s_ref, d_ref = src[k].at[self._block(k, me ^ p), :], land[k].at[me]
                    else:
                        s_ref, d_ref = src[k], land[k].at[self._block(k, me), :]
                    pltpu.make_async_remote_copy(
                        src_ref=s_ref, dst_ref=d_ref, send_sem=send_sems.at[self._sem(p, self.cls[k])],
                        recv_sem=recv_sems.at[self._sem(p, self.cls[k])], device_id=_peer(p),
                        device_id_type=MESH).start()
            token[...] = jnp.zeros_like(token)

        sems = pltpu.SemaphoreType.DMA(((N_DEV - 1) * self.ncls,))
        thru = [pltpu.HBM(s.shape, s.dtype) for s in srcs] + [pltpu.HBM(shp, dt) for shp, dt in
                                                               zip(self.land_shapes, self.dtypes)]
        lands = [pltpu.with_memory_space_constraint(lax.empty(shp, dt), pltpu.HBM)
                 for shp, dt in zip(self.land_shapes, self.dtypes)]
        out = pl.pallas_call(
            body, name=self.name + "_start",
            in_specs=[_HBM] * (2 * n) + [_ANY],
            out_shape=[sems, sems] + thru + [jax.ShapeDtypeStruct((8, 128), F32)],
            out_specs=[_SEM, _SEM] + [_HBM] * (2 * n) + [_VMEM],
            input_output_aliases={j: 2 + j for j in range(2 * n)},
            compiler_params=pltpu.CompilerParams(has_side_effects=_EFFECT),
        )(*[pltpu.with_memory_space_constraint(s, pltpu.HBM) for s in srcs], *lands, after)
        return out[:-1], out[-1]

    def wait(self, state, after):
        n = self.n
        send_sems, recv_sems = state[0], state[1]
        thru = state[2:]

        def body(*refs):
            src, land = refs[:n], refs[n:2 * n]
            send_sems, recv_sems = refs[2 * n], refs[2 * n + 1]
            for p in range(1, N_DEV):
                for w in range(self.ncls):
                    big = src[self.sizer[w]] if self.scatter else land[self.sizer[w]]
                    span = big.at[pl.ds(0, self.totals[w]), :]
                    copy = pltpu.make_async_remote_copy(
                        src_ref=span, dst_ref=span, send_sem=send_sems.at[self._sem(p, w)],
                        recv_sem=recv_sems.at[self._sem(p, w)],
                        device_id=_peer(p), device_id_type=MESH)
                    copy.wait_send()
                    copy.wait_recv()

        out = pl.pallas_call(
            body, name=self.name + "_wait",
            in_specs=[_HBM] * (2 * n) + [_SEM, _SEM, _ANY],
            out_shape=[pltpu.HBM(a.shape, a.dtype) for a in thru], out_specs=[_HBM] * (2 * n),
            input_output_aliases={j: j for j in range(2 * n)},
            compiler_params=pltpu.CompilerParams(has_side_effects=_EFFECT),
        )(*thru, send_sems, recv_sems, after)
        return out[:n], out[n:]

    def place(self, lands, srcs):
        n = self.n
        assert not self.scatter

        def body(*refs):
            src, land = refs[n:2 * n], refs[2 * n:3 * n]
            bufs, sems = refs[3 * n:4 * n], refs[-1]
            me = _my_index()
            loads = [pltpu.make_async_copy(src[k], bufs[k], sems.at[k]) for k in range(n)]
            stores = [pltpu.make_async_copy(bufs[k], land[k].at[self._block(k, me), :], sems.at[k]) for k in range(n)]
            for cp in loads:
                cp.start()
            for k in range(n):
                loads[k].wait()
                stores[k].start()
            for cp in stores:
                cp.wait()

        return pl.pallas_call(
            body, name=self.name + "_place", in_specs=[_ANY] * (2 * n), out_specs=[_ANY] * n,
            out_shape=[jax.ShapeDtypeStruct(a.shape, a.dtype) for a in lands],
            input_output_aliases={j: j for j in range(n)},
            scratch_shapes=[pltpu.VMEM(s.shape, s.dtype) for s in srcs] + [pltpu.SemaphoreType.DMA((n,))],
        )(*lands, *srcs)


def sum_blocks(landed, full, name):
    _, cnt, cols = landed.shape

    def body(land_ref, full_ref, o_ref, own_ref, sem):
        me = _my_index()
        own = pltpu.make_async_copy(full_ref.at[pl.ds(pl.multiple_of(me * cnt, 16), cnt), :], own_ref, sem)
        own.start()
        acc = land_ref[me ^ 1].astype(F32)
        for p in range(2, N_DEV):
            acc = acc + land_ref[me ^ p].astype(F32)
        own.wait()
        o_ref[...] = acc + own_ref[...].astype(F32)

    return pl.pallas_call(
        body, name=name, in_specs=[_VMEM, _ANY], out_specs=_VMEM,
        out_shape=jax.ShapeDtypeStruct((cnt, cols), F32),
        scratch_shapes=[pltpu.VMEM((cnt, cols), landed.dtype), pltpu.SemaphoreType.DMA],
        compiler_params=_params(),
    )(landed, full)


def sum_slots(slots, name):
    _, rows, cols = slots.shape
    tr = rows
    if rows > 512:
        for cand in (256, 128, 64, 32, 16, 8):
            if rows % cand == 0:
                tr = cand
                break

    def body(s_ref, o_ref):
        acc = s_ref[0].astype(F32)
        for j in range(1, N_DEV):
            acc = acc + s_ref[j].astype(F32)
        o_ref[...] = acc

    return pl.pallas_call(
        body, name=name, grid=(rows // tr,),
        in_specs=[pl.BlockSpec((N_DEV, tr, cols), lambda i: (0, i, 0))], out_specs=_row_spec(tr, cols),
        out_shape=jax.ShapeDtypeStruct((rows, cols), F32), compiler_params=_params(("arbitrary",)),
    )(slots)


BIG_T = ("ffn1_w_gate", "ffn1_w_up", "w_in", "ffn2_w_gate", "ffn2_w_up")
BIG_N = ("ffn1_w_down", "w_out", "ffn2_w_down")
HALF_T = ("w_attn_proj", "w_glu_v", "w_glu_g")
SMALL = ("ffn1_norm", "mix_norm", "attn_sinks", "ssm_a_re", "ssm_a_im", "ssm_log_dt", "ssm_b_re", "ssm_b_im",
         "ssm_c_re", "ssm_c_im", "ssm_d", "ffn2_norm", "final_norm")
PARTS = {"ffn1": ("ffn1_w_gate", "ffn1_w_up", "ffn1_w_down"),
         "mix": ("w_in", "w_out", "w_attn_proj", "w_glu_v", "w_glu_g"),
         "ffn2": ("ffn2_w_gate", "ffn2_w_up", "ffn2_w_down")}


def _to_rows(name, a):
    return a if name in BIG_N else jnp.swapaxes(a, -1, -2)


def local_step(x, tgt, get_weights, put_grads, small):
    seq, d = x.shape
    t = PAD_FRONT + N_META + seq
    cos_t, sin_t = rope_tables(t)
    row = lambda a: a.reshape(1, -1)
    saved = []
    h = None
    for i in range(DEPTH):
        s = {}
        w = dict(get_weights(i, "ffn1", h))
        if i == 0:
            h = jnp.concatenate([jnp.zeros((PAD_FRONT, d), F32), w["meta_tokens"], x], axis=0)
        s["h0"] = h
        h, s["n1"], s["a1"], s["b1"] = ffn_fwd(h, row(small["ffn1_norm"][i]), w["ffn1_w_gate"], w["ffn1_w_up"],
                                               w["ffn1_w_down"], f"ffn1_fwd_{i}")
        s["h1"] = h
        w.update(get_weights(i, "mix", h))
        s["n2"], s["qkv"], s["u"], s["gates"] = win_fwd(h, row(small["mix_norm"][i]), w["w_in"], cos_t, sin_t,
                                                        f"win_fwd_{i}")
        b_re_t = jnp.swapaxes(small["ssm_b_re"][i], 1, 2)
        b_im_t = jnp.swapaxes(small["ssm_b_im"][i], 1, 2)
        s["b_t"] = (b_re_t, b_im_t)
        lam_re, lam_im, bbar_re, bbar_im = ssm_prep(small["ssm_a_re"][i], small["ssm_a_im"][i],
                                                    small["ssm_log_dt"][i].reshape(-1, 1), b_re_t, b_im_t, f"ssm_prep_{i}")
        s["ssm"] = (row(lam_re), row(lam_im), _block_diag_b(bbar_re).astype(BF16), _block_diag_b(bbar_im).astype(BF16),
                    _block_diag_c(small["ssm_c_re"][i]).astype(BF16), _block_diag_c(small["ssm_c_im"][i]).astype(BF16),
                    row(small["ssm_d"][i]))
        s["yg"], s["h_re"], s["h_im"] = ssm_fwd(s["u"], *s["ssm"], f"ssm_fwd_{i}")
        s["o"] = attn_fwd(s["qkv"], row(small["attn_sinks"][i]), f"attn_fwd_{i}")
        h, s["merged"], s["att"], s["sv"], s["sg"] = merge_fwd(
            h, s["o"], s["yg"], s["gates"], w["w_attn_proj"], w["w_glu_v"], w["w_glu_g"], w["w_out"],
            f"merge_fwd_{i}")
        s["h2"] = h
        w.update(get_weights(i, "ffn2", h))
        h, s["n3"], s["a3"], s["b3"] = ffn_fwd(h, row(small["ffn2_norm"][i]), w["ffn2_w_gate"], w["ffn2_w_up"],
                                               w["ffn2_w_down"], f"ffn2_fwd_{i}")
        s["w"] = w
        saved.append(s)

    loss, dh, d_final = head_fwd_bwd(h, row(small["final_norm"]), tgt)
    gs = {k: [None] * DEPTH for k in SMALL if k != "final_norm"}
    dep = loss
    for i in reversed(range(DEPTH)):
        s = saved[i]
        w = s["w"]
        dh, da, db, sact, dhb, dg = ffn_bwd(dh, s["h2"], row(small["ffn2_norm"][i]), s["a3"], s["b3"], w["ffn2_w_gate"],
                                            w["ffn2_w_up"], w["ffn2_w_down"], dep, f"ffn2_bwd_{i}")
        gs["ffn2_norm"][i] = dg[0]
        dep = put_grads(i, "ffn2", {"ffn2_w_gate": tn_matmul(da, s["n3"], f"ffn2_dwg_{i}"),
                                    "ffn2_w_up": tn_matmul(db, s["n3"], f"ffn2_dwu_{i}"),
                                    "ffn2_w_down": tn_matmul(sact, dhb, f"ffn2_dwd_{i}")})

        dgates, datt, dsv, dsg, do, dyg, dhb = merge_bwd(dh, s["gates"], s["att"], s["sv"], s["sg"], w["w_attn_proj"],
                                                         w["w_glu_v"], w["w_glu_g"], w["w_out"], dep, f"merge_bwd_{i}")
        gmix = {"w_out": tn_matmul(s["merged"], dhb, f"dwout_{i}"),
                "w_attn_proj": tn_matmul(datt, s["o"], f"dwap_{i}"),
                "w_glu_v": tn_matmul(dsv, s["yg"], f"dwv_{i}"),
                "w_glu_g": tn_matmul(dsg, s["yg"], f"dwgg_{i}")}
        dqkv, dsink = attn_bwd(s["qkv"], do, row(small["attn_sinks"][i]), cos_t, sin_t, f"attn_bwd_{i}")
        gs["attn_sinks"][i] = dsink[:, 0]
        du, dl_re, dl_im, dbb_re, dbb_im, dcc_re, dcc_im, dd = ssm_bwd(dyg, s["u"], s["h_re"], s["h_im"], *s["ssm"],
                                                                      f"ssm_bwd_{i}")
        fold = lambda a: jnp.sum(a, axis=0).reshape(SSM_GROUPS, SSM_STATE)
        da_re, da_im, dldt, db_re_t, db_im_t = ssm_prep_bwd(
            small["ssm_a_re"][i], small["ssm_a_im"][i], small["ssm_log_dt"][i].reshape(-1, 1), *s["b_t"],
            fold(dl_re), fold(dl_im), _diag_of_b(dbb_re), _diag_of_b(dbb_im), f"ssm_prep_bwd_{i}")
        gs["ssm_a_re"][i], gs["ssm_a_im"][i], gs["ssm_log_dt"][i] = da_re, da_im, dldt[:, 0]
        gs["ssm_b_re"][i], gs["ssm_b_im"][i] = jnp.swapaxes(db_re_t, 1, 2), jnp.swapaxes(db_im_t, 1, 2)
        gs["ssm_c_re"][i], gs["ssm_c_im"][i] = _diag_of_c(dcc_re), _diag_of_c(dcc_im)
        gs["ssm_d"][i] = dd[0]
        dz = jnp.concatenate([dqkv, du, dgates], axis=1)
        gmix["w_in"] = tn_matmul(dz, s["n2"], f"dwin_{i}")
        dep = put_grads(i, "mix", gmix)
        dh, dg = win_bwd(dh, s["h1"], row(small["mix_norm"][i]), dz, w["w_in"], dep, f"win_bwd_{i}")
        gs["mix_norm"][i] = dg[0]

        dh, da, db, sact, dhb, dg = ffn_bwd(dh, s["h0"], row(small["ffn1_norm"][i]), s["a1"], s["b1"], w["ffn1_w_gate"],
                                            w["ffn1_w_up"], w["ffn1_w_down"], dep, f"ffn1_bwd_{i}")
        gs["ffn1_norm"][i] = dg[0]
        dep = put_grads(i, "ffn1", {"ffn1_w_gate": tn_matmul(da, s["n1"], f"ffn1_dwg_{i}"),
                                    "ffn1_w_up": tn_matmul(db, s["n1"], f"ffn1_dwu_{i}"),
                                    "ffn1_w_down": tn_matmul(sact, dhb, f"ffn1_dwd_{i}")})

    gs = {k: jnp.stack(v) for k, v in gs.items()}
    gs["final_norm"] = d_final[0]
    return loss[0, 0], dh[PAD_FRONT + N_META:], dh[PAD_FRONT:PAD_FRONT + N_META], gs


def _pack_rows(arrays, cols):
    flat = jnp.concatenate([a.reshape(-1) for a in arrays])
    rows = -(-flat.shape[0] // cols)
    rows = -(-rows // 8) * 8
    return jnp.pad(flat, (0, rows * cols - flat.shape[0])).reshape(rows, cols)


def _unpack_rows(packed, shapes):
    flat = packed.reshape(-1)
    out, off = [], 0
    for shp in shapes:
        n = math.prod(shp)
        out.append(flat[off:off + n].reshape(shp))
        off += n
    return out


def kernel(x, meta_tokens, ffn1_norm, ffn1_w_gate, ffn1_w_up, ffn1_w_down, mix_norm, w_in, attn_sinks, ssm_a_re, ssm_a_im, ssm_log_dt, ssm_b_re, ssm_b_im, ssm_c_re, ssm_c_im, ssm_d, w_attn_proj, w_glu_v, w_glu_g, w_out, ffn2_norm, ffn2_w_gate, ffn2_w_up, ffn2_w_down, final_norm, loss_target, m_meta_tokens, m_ffn1_norm, m_ffn1_w_gate, m_ffn1_w_up, m_ffn1_w_down, m_mix_norm, m_w_in, m_attn_sinks, m_ssm_a_re, m_ssm_a_im, m_ssm_log_dt, m_ssm_b_re, m_ssm_b_im, m_ssm_c_re, m_ssm_c_im, m_ssm_d, m_w_attn_proj, m_w_glu_v, m_w_glu_g, m_w_out, m_ffn2_norm, m_ffn2_w_gate, m_ffn2_w_up, m_ffn2_w_down, m_final_norm, v_meta_tokens, v_ffn1_norm, v_ffn1_w_gate, v_ffn1_w_up, v_ffn1_w_down, v_mix_norm, v_w_in, v_attn_sinks, v_ssm_a_re, v_ssm_a_im, v_ssm_log_dt, v_ssm_b_re, v_ssm_b_im, v_ssm_c_re, v_ssm_c_im, v_ssm_d, v_w_attn_proj, v_w_glu_v, v_w_glu_g, v_w_out, v_ffn2_norm, v_ffn2_w_gate, v_ffn2_w_up, v_ffn2_w_down, v_final_norm):
    names = ("meta_tokens", "ffn1_norm", "ffn1_w_gate", "ffn1_w_up", "ffn1_w_down", "mix_norm", "w_in", "attn_sinks",
             "ssm_a_re", "ssm_a_im", "ssm_log_dt", "ssm_b_re", "ssm_b_im", "ssm_c_re", "ssm_c_im", "ssm_d",
             "w_attn_proj", "w_glu_v", "w_glu_g", "w_out", "ffn2_norm", "ffn2_w_gate", "ffn2_w_up", "ffn2_w_down",
             "final_norm")
    weights = dict(zip(names, (meta_tokens, ffn1_norm, ffn1_w_gate, ffn1_w_up, ffn1_w_down, mix_norm, w_in, attn_sinks, ssm_a_re, ssm_a_im, ssm_log_dt, ssm_b_re, ssm_b_im, ssm_c_re, ssm_c_im, ssm_d, w_attn_proj, w_glu_v, w_glu_g, w_out, ffn2_norm, ffn2_w_gate, ffn2_w_up, ffn2_w_down, final_norm)))
    moments_m = dict(zip(names, (m_meta_tokens, m_ffn1_norm, m_ffn1_w_gate, m_ffn1_w_up, m_ffn1_w_down, m_mix_norm, m_w_in, m_attn_sinks, m_ssm_a_re, m_ssm_a_im, m_ssm_log_dt, m_ssm_b_re, m_ssm_b_im, m_ssm_c_re, m_ssm_c_im, m_ssm_d, m_w_attn_proj, m_w_glu_v, m_w_glu_g, m_w_out, m_ffn2_norm, m_ffn2_w_gate, m_ffn2_w_up, m_ffn2_w_down, m_final_norm)))
    moments_v = dict(zip(names, (v_meta_tokens, v_ffn1_norm, v_ffn1_w_gate, v_ffn1_w_up, v_ffn1_w_down, v_mix_norm, v_w_in, v_attn_sinks, v_ssm_a_re, v_ssm_a_im, v_ssm_log_dt, v_ssm_b_re, v_ssm_b_im, v_ssm_c_re, v_ssm_c_im, v_ssm_d, v_w_attn_proj, v_w_glu_v, v_w_glu_g, v_w_out, v_ffn2_norm, v_ffn2_w_gate, v_ffn2_w_up, v_ffn2_w_down, v_final_norm)))
    me = _my_index()

    gathers = {}
    token = jnp.zeros((8, 128), F32)
    for i in range(DEPTH):
        for part, ks in PARTS.items():
            shards = [_to_rows(k, weights[k][i]).astype(BF16) for k in ks]
            if (i, part) == (0, "ffn1"):
                shards.append(meta_tokens)
            ex = Exchange(shards, False, f"gather_{part}_{i}")
            state, token = ex.start(shards, token)
            gathers[i, part] = (ex, state, shards)
    all_started = token

    def get_weights(i, part, after):
        ex, state, shards = gathers[i, part]
        shards, lands = ex.wait(state, all_started if after is None else after)
        fulls = ex.place(lands, shards)
        got = dict(zip(PARTS[part], fulls))
        if (i, part) == (0, "ffn1"):
            got["meta_tokens"] = jnp.swapaxes(fulls[-1].reshape(N_DEV, N_META, 128), 0, 1).reshape(N_META, D_MODEL)
        return got

    scatters = []

    def put_grads(i, part, gdict):
        ks = PARTS[part]
        srcs = [gdict[k] for k in ks]
        ex = Exchange(srcs, True, f"scatter_{part}_{i}")
        state, tok = ex.start(srcs, all_started)
        scatters.append((i, ks, ex, state))
        return tok

    small = {k: weights[k] for k in SMALL}
    loss, dx, dmeta, gs = local_step(x[0], loss_target[0], get_weights, put_grads, small)

    grads, deltas, new_m, new_v = {}, {}, {}, {}
    small_list = [loss.reshape(1), dmeta] + [gs[k] for k in SMALL]
    packed = _pack_rows(small_list, D_MODEL)
    (packed_all,) = all_gather([packed], "gather_small")
    total = sum_slots(packed_all, "sum_small")
    pieces = _unpack_rows(total, [a.shape for a in small_list])
    loss_out = pieces[0][0]
    grads["meta_tokens"] = lax.dynamic_slice_in_dim(pieces[1], me * 128, 128, axis=1)
    for k, p in zip(SMALL, pieces[2:]):
        grads[k] = p
    for k in ("meta_tokens",) + SMALL:
        deltas[k], new_m[k], new_v[k] = adamw(weights[k], grads[k], moments_m[k], moments_v[k], f"adamw_{k}")

    summed = {}
    after = deltas["final_norm"]
    for i, ks, ex, state in scatters:
        partials, lands = ex.wait(state, after)
        for k, partial, slots in zip(ks, partials, lands):
            summed[k, i] = sum_blocks(slots, partial, f"sum_{k}_{i}")
        after = summed[ks[-1], i]
    for part in PARTS.values():
        for k in part:
            grads[k] = _to_rows(k, jnp.stack([summed[k, i] for i in range(DEPTH)]))
            deltas[k], new_m[k], new_v[k] = adamw(weights[k], grads[k], moments_m[k], moments_v[k], f"adamw_{k}")
    return (loss_out, dx[None], *[grads[k] for k in names], *[deltas[k] for k in names],
            *[new_m[k] for k in names], *[new_v[k] for k in names])
```

```python
import functools
import math

import jax
import jax.numpy as jnp
from jax import lax
from jax.experimental import pallas as pl
from jax.experimental.pallas import tpu as pltpu

F32 = jnp.float32
BF16 = jnp.bfloat16

D_MODEL = 1024
DEPTH = 2
N_META = 16
HEAD_DIM = 64
N_Q_HEADS = 8
ATTN_WIDTH = 512
KV_WIDTH = 128
QKV_WIDTH = ATTN_WIDTH + 2 * KV_WIDTH
WINDOW = 128
BLK = 128
ROPE_THETA = 500000.0
ROT_DIM = 16
SSM_WIDTH = 512
SSM_GROUP = 16
SSM_GROUPS = 32
SSM_STATE = 64
STATE_WIDTH = SSM_GROUPS * SSM_STATE
D_FF = 2816
IN_WIDTH = 3328
EPS = 1e-6
NEG_INF = -1e30
PAD_FRONT = (-N_META) % BLK
N_DEV = 8

ADAM_LR = 0.001
ADAM_B1 = 0.9
ADAM_B2 = 0.999
ADAM_EPS = 1e-08
ADAM_WD = 0.01
ADAM_STEP = 10

VMEM_LIMIT = 56 * 1024 * 1024
TOKEN_TILE = 384
_VMEM = pl.BlockSpec(memory_space=pltpu.VMEM)
_SMEM = pl.BlockSpec(memory_space=pltpu.SMEM)
_ANY = pl.BlockSpec(memory_space=pl.ANY)
MESH = pl.DeviceIdType.MESH


def _params(sem=None):
    return pltpu.CompilerParams(dimension_semantics=sem, vmem_limit_bytes=VMEM_LIMIT)


def _nt(a, b):
    return lax.dot_general(a, b, (((1,), (1,)), ((), ())), preferred_element_type=F32)


def _nn(a, b):
    return jnp.dot(a, b, preferred_element_type=F32)


def _tn(a, b):
    return lax.dot_general(a, b, (((0,), (0,)), ((), ())), preferred_element_type=F32)


def _row_spec(tm, width):
    return pl.BlockSpec((tm, width), lambda i: (i, 0))


def _acc_spec(shape):
    return pl.BlockSpec(shape, lambda i: (0,) * len(shape))


def _rms_stats(x):
    r = lax.rsqrt(jnp.mean(x * x, axis=-1, keepdims=True) + EPS)
    return x * r, r


def _rms_bwd(dn, xh, r, g):
    dg = jnp.sum(dn * xh, axis=0, keepdims=True)
    dxh = dn * g
    dx = r * (dxh - xh * jnp.mean(dxh * xh, axis=-1, keepdims=True))
    return dx, dg


def ffn_fwd(h, g, wg_t, wu_t, wd, name):
    t, d = h.shape
    f = wd.shape[0]
    tm = TOKEN_TILE

    def body(h_ref, g_ref, wg_ref, wu_ref, wd_ref, ho_ref, n_ref, a_ref, b_ref):
        x = h_ref[...]
        xh, _ = _rms_stats(x)
        n = (xh * g_ref[...]).astype(BF16)
        n_ref[...] = n
        a = _nt(n, wg_ref[...])
        b = _nt(n, wu_ref[...])
        a_ref[...] = a.astype(BF16)
        b_ref[...] = b.astype(BF16)
        s = (a * jax.nn.sigmoid(a) * b).astype(BF16)
        ho_ref[...] = x + 0.5 * _nn(s, wd_ref[...])

    return pl.pallas_call(
        body, name=name, grid=(t // tm,),
        in_specs=[_row_spec(tm, d), _acc_spec((1, d)), _VMEM, _VMEM, _VMEM],
        out_specs=[_row_spec(tm, d), _row_spec(tm, d), _row_spec(tm, f), _row_spec(tm, f)],
        out_shape=[jax.ShapeDtypeStruct((t, d), F32), jax.ShapeDtypeStruct((t, d), BF16),
                   jax.ShapeDtypeStruct((t, f), BF16), jax.ShapeDtypeStruct((t, f), BF16)],
        compiler_params=_params(("arbitrary",)),
    )(h, g, wg_t, wu_t, wd)


def ffn_bwd(dh, h, g, a, b, wg_t, wu_t, wd, dep, name):
    t, d = h.shape
    f = wd.shape[0]
    tm = TOKEN_TILE // 2

    def body(dh_ref, h_ref, g_ref, a_ref, b_ref, wg_ref, wu_ref, wd_ref, dep_ref,
             dhi_ref, da_ref, db_ref, s_ref, dhb_ref, dg_ref):
        dh_t = dh_ref[...]
        dhb = (0.5 * dh_t).astype(BF16)
        dhb_ref[...] = dhb
        ds = _nt(dhb, wd_ref[...])
        av = a_ref[...].astype(F32)
        bv = b_ref[...].astype(F32)
        sig = jax.nn.sigmoid(av)
        sl = av * sig
        s_ref[...] = (sl * bv).astype(BF16)
        da = (ds * bv * (sig * (1.0 + av * (1.0 - sig)))).astype(BF16)
        db = (ds * sl).astype(BF16)
        da_ref[...] = da
        db_ref[...] = db
        dn = _nn(da, wg_ref[...]) + _nn(db, wu_ref[...])
        xh, r = _rms_stats(h_ref[...])
        dx, dg = _rms_bwd(dn, xh, r, g_ref[...])
        dhi_ref[...] = dh_t + dx

        @pl.when(pl.program_id(0) == 0)
        def _():
            dg_ref[...] = jnp.zeros_like(dg_ref)

        dg_ref[...] += dg

    return pl.pallas_call(
        body, name=name, grid=(t // tm,),
        in_specs=[_row_spec(tm, d), _row_spec(tm, d), _acc_spec((1, d)), _row_spec(tm, f), _row_spec(tm, f),
                  _VMEM, _VMEM, _VMEM, _ANY],
        out_specs=[_row_spec(tm, d), _row_spec(tm, f), _row_spec(tm, f), _row_spec(tm, f), _row_spec(tm, d),
                   _acc_spec((1, d))],
        out_shape=[jax.ShapeDtypeStruct((t, d), F32), jax.ShapeDtypeStruct((t, f), BF16),
                   jax.ShapeDtypeStruct((t, f), BF16), jax.ShapeDtypeStruct((t, f), BF16),
                   jax.ShapeDtypeStruct((t, d), BF16), jax.ShapeDtypeStruct((1, d), F32)],
        compiler_params=_params(("arbitrary",)),
    )(dh, h, g, a, b, wg_t, wu_t, wd, dep)


DW_TILE = 256


def tn_matmul(x, y, name, dep=None):
    t, m = x.shape
    n = y.shape[1]
    bm = DW_TILE
    deps = [] if dep is None else [dep]

    def body(x_ref, y_ref, *rest):
        rest[-1][...] = _tn(x_ref[...], y_ref[...]).astype(BF16)

    return pl.pallas_call(
        body, name=name, grid=(m // bm,),
        in_specs=[pl.BlockSpec((t, bm), lambda i: (0, i)), _VMEM] + [_ANY] * len(deps),
        out_specs=pl.BlockSpec((bm, n), lambda i: (i, 0)),
        out_shape=jax.ShapeDtypeStruct((m, n), BF16),
        compiler_params=_params(("arbitrary",)),
    )(x, y, *deps)


def head_fwd_bwd(h, g, tgt):
    t, d = h.shape

    def body(h_ref, g_ref, t_ref, loss_ref, dh_ref, dg_ref):
        i = pl.program_id(0)
        xh, r = _rms_stats(h_ref[...])
        gv = g_ref[...]
        valid = (i > 0).astype(F32)
        e = (xh * gv - t_ref[...]) * valid
        dx, dg = _rms_bwd(e * (1.0 / d), xh, r, gv)
        dh_ref[...] = dx

        @pl.when(i == 0)
        def _():
            dg_ref[...] = jnp.zeros_like(dg_ref)
            loss_ref[...] = jnp.zeros_like(loss_ref)

        dg_ref[...] += dg
        loss_ref[...] += jnp.sum(e * e) * (0.5 / d)

    return pl.pallas_call(
        body, name="head", grid=(t // BLK,),
        in_specs=[_row_spec(BLK, d), _acc_spec((1, d)),
                  pl.BlockSpec((BLK, d), lambda i: (jnp.maximum(i - 1, 0), 0))],
        out_specs=[_acc_spec((1, 128)), _row_spec(BLK, d), _acc_spec((1, d))],
        out_shape=[jax.ShapeDtypeStruct((1, 128), F32), jax.ShapeDtypeStruct((t, d), F32),
                   jax.ShapeDtypeStruct((1, d), F32)],
        compiler_params=_params(("arbitrary",)),
    )(h, g, tgt)


def rope_tables(t):
    pos = jnp.arange(t, dtype=F32) - PAD_FRONT
    inv_freq = ROPE_THETA ** (-jnp.arange(0, ROT_DIM, 2, dtype=F32) / ROT_DIM)
    ang = pos[:, None] * inv_freq[None, :]
    cos, sin = jnp.cos(ang), jnp.sin(ang)
    ones = jnp.ones((t, HEAD_DIM - ROT_DIM), F32)
    cos_h = jnp.concatenate([cos, cos, ones], axis=1)
    sin_h = jnp.concatenate([-sin, sin, 0.0 * ones], axis=1)
    return jnp.concatenate([cos_h, cos_h], axis=1), jnp.concatenate([sin_h, sin_h], axis=1)


def _swap_halves(x):
    n = x.shape[1]
    lane = lax.broadcasted_iota(jnp.int32, x.shape, 1)
    return jnp.where(lane % HEAD_DIM < ROT_DIM // 2, pltpu.roll(x, n - ROT_DIM // 2, 1), pltpu.roll(x, ROT_DIM // 2, 1))


def _rope(x, cos_t, sin_t, sign):
    return x * cos_t + sign * (_swap_halves(x) * sin_t)


def win_fwd(h, g, win_t, cos_t, sin_t, name):
    t, d = h.shape
    tm = TOKEN_TILE

    def body(h_ref, g_ref, w_ref, c_ref, s_ref, n_ref, qkv_ref, u_ref, gates_ref):
        xh, _ = _rms_stats(h_ref[...])
        n = (xh * g_ref[...]).astype(BF16)
        n_ref[...] = n
        z = _nt(n, w_ref[...])
        c, s = c_ref[...], s_ref[...]
        for j in range((ATTN_WIDTH + KV_WIDTH) // 128):
            qkv_ref[:, j * 128:(j + 1) * 128] = _rope(z[:, j * 128:(j + 1) * 128], c, s, 1.0).astype(BF16)
        qkv_ref[:, ATTN_WIDTH + KV_WIDTH:QKV_WIDTH] = z[:, ATTN_WIDTH + KV_WIDTH:QKV_WIDTH].astype(BF16)
        u_ref[...] = z[:, QKV_WIDTH:QKV_WIDTH + SSM_WIDTH]
        gates_ref[...] = z[:, QKV_WIDTH + SSM_WIDTH:]

    return pl.pallas_call(
        body, name=name, grid=(t // tm,),
        in_specs=[_row_spec(tm, d), _acc_spec((1, d)), _VMEM, _row_spec(tm, 128), _row_spec(tm, 128)],
        out_specs=[_row_spec(tm, d), _row_spec(tm, QKV_WIDTH), _row_spec(tm, SSM_WIDTH), _row_spec(tm, 2 * d)],
        out_shape=[jax.ShapeDtypeStruct((t, d), BF16), jax.ShapeDtypeStruct((t, QKV_WIDTH), BF16),
                   jax.ShapeDtypeStruct((t, SSM_WIDTH), F32), jax.ShapeDtypeStruct((t, 2 * d), F32)],
        compiler_params=_params(("arbitrary",)),
    )(h, g, win_t, cos_t, sin_t)


def win_bwd(dh, h, g, dz, win_t, dep, name):
    t, d = h.shape
    tm = TOKEN_TILE

    def body(dh_ref, h_ref, g_ref, dz_ref, w_ref, dep_ref, dhi_ref, dg_ref):
        dn = _nn(dz_ref[...], w_ref[...])
        xh, r = _rms_stats(h_ref[...])
        dx, dg = _rms_bwd(dn, xh, r, g_ref[...])
        dhi_ref[...] = dh_ref[...] + dx

        @pl.when(pl.program_id(0) == 0)
        def _():
            dg_ref[...] = jnp.zeros_like(dg_ref)

        dg_ref[...] += dg

    return pl.pallas_call(
        body, name=name, grid=(t // tm,),
        in_specs=[_row_spec(tm, d), _row_spec(tm, d), _acc_spec((1, d)), _row_spec(tm, IN_WIDTH), _VMEM, _ANY],
        out_specs=[_row_spec(tm, d), _acc_spec((1, d))],
        out_shape=[jax.ShapeDtypeStruct((t, d), F32), jax.ShapeDtypeStruct((1, d), F32)],
        compiler_params=_params(("arbitrary",)),
    )(dh, h, g, dz, win_t, dep)


def _attn_mask(blk):
    q_pos = blk * BLK + lax.broadcasted_iota(jnp.int32, (BLK, 3 * BLK), 0) - PAD_FRONT
    col = lax.broadcasted_iota(jnp.int32, (BLK, 3 * BLK), 1)
    part = col // BLK
    k_pos = jnp.where(part == 0, col, (blk + part - 2) * BLK + (col - part * BLK)) - PAD_FRONT
    dist = q_pos - k_pos
    meta_ok = (part == 0) & (k_pos >= 0) & (dist >= 0)
    band_ok = (part > 0) & (k_pos >= N_META) & (dist >= 0) & (dist < WINDOW)
    return meta_ok | band_ok


def _head_halves(x128, kv):
    x = x128.astype(F32)
    lane = lax.broadcasted_iota(jnp.int32, x.shape, 1)
    swapped = pltpu.roll(x, HEAD_DIM, 1)
    lo, hi = (x, swapped) if kv == 0 else (swapped, x)
    return jnp.where(lane < HEAD_DIM, lo, 0.0).astype(BF16), jnp.where(lane >= HEAD_DIM, hi, 0.0).astype(BF16)


def _gather_keys(meta_ref, prev_ref, cur_ref, lo):
    return jnp.concatenate([meta_ref[:, lo:lo + 128], prev_ref[:, lo:lo + 128], cur_ref[:, lo:lo + 128]], axis=0)


def _softmax_with_sink(s, mask, sink):
    s = jnp.where(mask, s * (HEAD_DIM ** -0.5), NEG_INF)
    m = jnp.maximum(jnp.max(s, axis=-1, keepdims=True), sink)
    p = jnp.exp(s - m)
    p_sink = jnp.exp(sink - m)
    inv = 1.0 / (jnp.sum(p, axis=-1, keepdims=True) + p_sink)
    return p * inv, p_sink * inv


def attn_fwd(qkv, sinks, name):
    t = qkv.shape[0]
    nb = t // BLK

    def body(sink_ref, meta_ref, prev_ref, cur_ref, o_ref):
        blk = pl.program_id(0)
        mask = _attn_mask(blk)
        k128 = _gather_keys(meta_ref, prev_ref, cur_ref, ATTN_WIDTH)
        v128 = _gather_keys(meta_ref, prev_ref, cur_ref, ATTN_WIDTH + KV_WIDTH)
        for kv in range(2):
            k_lo, k_hi = _head_halves(k128, kv)
            v_lo, v_hi = _head_halves(v128, kv)
            for pair in range(2):
                lanes = slice((2 * kv + pair) * 128, (2 * kv + pair + 1) * 128)
                q128 = cur_ref[:, lanes]
                head = 4 * kv + 2 * pair
                p_a, _ = _softmax_with_sink(_nt(q128, k_lo), mask, sink_ref[0, head])
                p_b, _ = _softmax_with_sink(_nt(q128, k_hi), mask, sink_ref[0, head + 1])
                o_ref[:, lanes] = (_nn(p_a.astype(BF16), v_lo) + _nn(p_b.astype(BF16), v_hi)).astype(BF16)

    blk_spec = lambda f: pl.BlockSpec((BLK, QKV_WIDTH), f)
    return pl.pallas_call(
        body, name=name, grid=(nb,),
        in_specs=[_SMEM, blk_spec(lambda i: (0, 0)), blk_spec(lambda i: (jnp.maximum(i - 1, 0), 0)),
                  blk_spec(lambda i: (i, 0))],
        out_specs=_row_spec(BLK, ATTN_WIDTH),
        out_shape=jax.ShapeDtypeStruct((t, ATTN_WIDTH), BF16),
        compiler_params=_params(("arbitrary",)),
    )(sinks, qkv, qkv, qkv)


def attn_bwd(qkv, do, sinks, cos_t, sin_t, name):
    t = qkv.shape[0]
    nb = t // BLK

    def body(sink_ref, meta_ref, prev_ref, cur_ref, do_ref, c_ref, s_ref, dqkv_ref, dsink_ref, carry_ref, macc_ref):
        step = pl.program_id(0)
        blk = nb - 1 - step

        @pl.when(step == 0)
        def _():
            dsink_ref[...] = jnp.zeros_like(dsink_ref)
            carry_ref[...] = jnp.zeros_like(carry_ref)
            macc_ref[...] = jnp.zeros_like(macc_ref)

        mask = _attn_mask(blk)
        lane = lax.broadcasted_iota(jnp.int32, (3 * BLK, 128), 1)
        k128 = _gather_keys(meta_ref, prev_ref, cur_ref, ATTN_WIDTH)
        v128 = _gather_keys(meta_ref, prev_ref, cur_ref, ATTN_WIDTH + KV_WIDTH)
        cos_b, sin_b = c_ref[...], s_ref[...]
        dk_heads, dv_heads = [], []
        for kv in range(2):
            k_lo, k_hi = _head_halves(k128, kv)
            v_lo, v_hi = _head_halves(v128, kv)
            dk_acc = jnp.zeros((3 * BLK, 128), F32)
            dv_acc = jnp.zeros((3 * BLK, 128), F32)
            for pair in range(2):
                lanes = slice((2 * kv + pair) * 128, (2 * kv + pair + 1) * 128)
                q128 = cur_ref[:, lanes]
                do128 = do_ref[:, lanes]
                head = 4 * kv + 2 * pair
                ds_pair, p_pair = [], []
                for half, (k_h, v_h) in enumerate(((k_lo, v_lo), (k_hi, v_hi))):
                    p, p_sink = _softmax_with_sink(_nt(q128, k_h), mask, sink_ref[0, head + half])
                    dp = _nt(do128, v_h)
                    dsum = jnp.sum(p * dp, axis=-1, keepdims=True)
                    ds_pair.append((p * (dp - dsum) * (HEAD_DIM ** -0.5)).astype(BF16))
                    p_pair.append(p.astype(BF16))
                    dsink = -jnp.sum(p_sink * dsum, axis=0, keepdims=True)
                    dsink_ref[head + half:head + half + 1, :] += jnp.broadcast_to(dsink, (1, 128))
                dq = _nn(ds_pair[0], k_lo) + _nn(ds_pair[1], k_hi)
                dqkv_ref[:, lanes] = _rope(dq, cos_b, sin_b, -1.0).astype(BF16)
                dk_acc += jnp.where(lane < HEAD_DIM, _tn(ds_pair[0], q128), _tn(ds_pair[1], q128))
                dv_acc += jnp.where(lane < HEAD_DIM, _tn(p_pair[0], do128), _tn(p_pair[1], do128))
            dk_heads.append(dk_acc + pltpu.roll(dk_acc, HEAD_DIM, 1))
            dv_heads.append(dv_acc + pltpu.roll(dv_acc, HEAD_DIM, 1))
        dkv = jnp.concatenate([jnp.where(lane < HEAD_DIM, dk_heads[0], dk_heads[1]),
                               jnp.where(lane < HEAD_DIM, dv_heads[0], dv_heads[1])], axis=1)
        macc_ref[...] += dkv[0:BLK]
        is_last = (blk == 0).astype(F32)
        mine = dkv[2 * BLK:3 * BLK] + carry_ref[...] + is_last * macc_ref[...]
        carry_ref[...] = dkv[BLK:2 * BLK]
        dqkv_ref[:, ATTN_WIDTH:ATTN_WIDTH + KV_WIDTH] = _rope(mine[:, 0:128], cos_b, sin_b, -1.0).astype(BF16)
        dqkv_ref[:, ATTN_WIDTH + KV_WIDTH:QKV_WIDTH] = mine[:, 128:256].astype(BF16)

    rev = lambda i: nb - 1 - i
    blk_spec = lambda f: pl.BlockSpec((BLK, QKV_WIDTH), f)
    return pl.pallas_call(
        body, name=name, grid=(nb,),
        in_specs=[_SMEM, blk_spec(lambda i: (0, 0)), blk_spec(lambda i: (jnp.maximum(rev(i) - 1, 0), 0)),
                  blk_spec(lambda i: (rev(i), 0)), pl.BlockSpec((BLK, ATTN_WIDTH), lambda i: (rev(i), 0)),
                  pl.BlockSpec((BLK, 128), lambda i: (rev(i), 0)), pl.BlockSpec((BLK, 128), lambda i: (rev(i), 0))],
        out_specs=[pl.BlockSpec((BLK, QKV_WIDTH), lambda i: (rev(i), 0)), _acc_spec((N_Q_HEADS, 128))],
        out_shape=[jax.ShapeDtypeStruct((t, QKV_WIDTH), BF16), jax.ShapeDtypeStruct((N_Q_HEADS, 128), F32)],
        scratch_shapes=[pltpu.VMEM((BLK, 256), F32), pltpu.VMEM((BLK, 256), F32)],
        compiler_params=_params(("arbitrary",)),
    )(sinks, qkv, qkv, qkv, do, cos_t, sin_t)


def _cmul(ar, ai, br, bi):
    return ar * br - ai * bi, ar * bi + ai * br


def ssm_prep(a_re, a_im, log_dt, b_re_t, b_im_t, name):
    def body(ar_ref, ai_ref, ldt_ref, br_ref, bi_ref, lr_ref, li_ref, bbr_ref, bbi_ref):
        ar, ai = ar_ref[...], ai_ref[...]
        dt = jnp.exp(ldt_ref[...])
        mag = jnp.exp(ar * dt)
        lr = mag * jnp.cos(ai * dt)
        li = mag * jnp.sin(ai * dt)
        den = ar * ar + ai * ai
        nr = lr - 1.0
        cr = ((nr * ar + li * ai) / den)[:, None, :]
        ci = ((li * ar - nr * ai) / den)[:, None, :]
        br, bi = br_ref[...], bi_ref[...]
        lr_ref[...] = lr
        li_ref[...] = li
        bbr_ref[...] = cr * br - ci * bi
        bbi_ref[...] = cr * bi + ci * br

    gp = jax.ShapeDtypeStruct(a_re.shape, F32)
    gcp = jax.ShapeDtypeStruct(b_re_t.shape, F32)
    return pl.pallas_call(body, name=name, out_shape=[gp, gp, gcp, gcp],
                          in_specs=[_VMEM] * 5, out_specs=[_VMEM] * 4)(a_re, a_im, log_dt, b_re_t, b_im_t)


def ssm_prep_bwd(a_re, a_im, log_dt, b_re_t, b_im_t, dl_re, dl_im, dbb_re, dbb_im, name):
    def body(ar_ref, ai_ref, ldt_ref, br_ref, bi_ref, dlr_ref, dli_ref, dbbr_ref, dbbi_ref,
             dar_ref, dai_ref, dldt_ref, dbr_ref, dbi_ref):
        ar, ai = ar_ref[...], ai_ref[...]
        dt = jnp.exp(ldt_ref[...])
        mag = jnp.exp(ar * dt)
        lr = mag * jnp.cos(ai * dt)
        li = mag * jnp.sin(ai * dt)
        den = ar * ar + ai * ai
        nr = lr - 1.0
        cr = (nr * ar + li * ai) / den
        ci = (li * ar - nr * ai) / den
        br, bi = br_ref[...], bi_ref[...]
        dbbr, dbbi = dbbr_ref[...], dbbi_ref[...]
        dbr_ref[...] = cr[:, None, :] * dbbr + ci[:, None, :] * dbbi
        dbi_ref[...] = cr[:, None, :] * dbbi - ci[:, None, :] * dbbr
        dcr = jnp.sum(br * dbbr + bi * dbbi, axis=1)
        dci = jnp.sum(br * dbbi - bi * dbbr, axis=1)
        d_num_r = dcr / den
        d_num_i = dci / den
        d_den = -(dcr * cr + dci * ci) / den
        d_lr = dlr_ref[...] + d_num_r * ar - d_num_i * ai
        d_li = dli_ref[...] + d_num_r * ai + d_num_i * ar
        d_ar = d_num_r * nr + d_num_i * li + d_den * 2.0 * ar
        d_ai = d_num_r * li - d_num_i * nr + d_den * 2.0 * ai
        d_mag = (d_lr * lr + d_li * li) / mag
        d_theta = d_li * lr - d_lr * li
        d_ardt = d_mag * mag
        dar_ref[...] = d_ar + d_ardt * dt
        dai_ref[...] = d_ai + d_theta * dt
        d_dt = jnp.sum(d_ardt * ar + d_theta * ai, axis=1, keepdims=True)
        dldt_ref[...] = d_dt * dt

    gp = jax.ShapeDtypeStruct(a_re.shape, F32)
    gcp = jax.ShapeDtypeStruct(b_re_t.shape, F32)
    return pl.pallas_call(body, name=name, out_shape=[gp, gp, jax.ShapeDtypeStruct(log_dt.shape, F32), gcp, gcp],
                          in_specs=[_VMEM] * 9, out_specs=[_VMEM] * 5,
                          )(a_re, a_im, log_dt, b_re_t, b_im_t, dl_re, dl_im, dbb_re, dbb_im)


N_CHUNK = 4
U_CHUNK = SSM_WIDTH // N_CHUNK
H_CHUNK = STATE_WIDTH // N_CHUNK
SUB = 8


def _block_diag_b(bb):
    x = bb.reshape(N_CHUNK, 8, SSM_GROUP, 1, SSM_STATE)
    same = (jnp.arange(8)[:, None] == jnp.arange(8)[None, :])[None, :, None, :, None]
    return jnp.where(same, x, 0.0).reshape(N_CHUNK, U_CHUNK, H_CHUNK)


def _block_diag_c(c):
    x = jnp.swapaxes(c.reshape(N_CHUNK, 8, SSM_GROUP, SSM_STATE), 2, 3)[:, :, :, None, :]
    same = (jnp.arange(8)[:, None] == jnp.arange(8)[None, :])[None, :, None, :, None]
    return jnp.where(same, x, 0.0).reshape(N_CHUNK, H_CHUNK, U_CHUNK)


def _diag_of_b(m):
    x = m.reshape(N_CHUNK, 8, SSM_GROUP, 8, SSM_STATE)
    return jnp.stack([x[:, g, :, g, :] for g in range(8)], axis=1).reshape(SSM_GROUPS, SSM_GROUP, SSM_STATE)


def _diag_of_c(m):
    x = m.reshape(N_CHUNK, 8, SSM_STATE, 8, SSM_GROUP)
    d = jnp.stack([x[:, g, :, g, :] for g in range(8)], axis=1)
    return jnp.swapaxes(d, 2, 3).reshape(SSM_GROUPS, SSM_GROUP, SSM_STATE)


def _lambda_tables(lr, li, reverse):
    p1 = (lr, li)
    p2 = _cmul(*p1, *p1)
    p4 = _cmul(*p2, *p2)
    rows = [p1]
    for _ in range(SUB - 1):
        rows.append(_cmul(*rows[-1], *p1))
    if reverse:
        rows = rows[::-1]
    return p1, p2, p4, (jnp.concatenate([r[0] for r in rows], axis=0), jnp.concatenate([r[1] for r in rows], axis=0))


def _scan8(xr, xi, pows, table, cr, ci, reverse):
    row = lax.broadcasted_iota(jnp.int32, xr.shape, 0)
    for d, (pr, pi) in zip((1, 2, 4), pows):
        if reverse:
            sr, si = pltpu.roll(xr, SUB - d, 0), pltpu.roll(xi, SUB - d, 0)
            keep = row < SUB - d
        else:
            sr, si = pltpu.roll(xr, d, 0), pltpu.roll(xi, d, 0)
            keep = row >= d
        sr = jnp.where(keep, sr, 0.0)
        si = jnp.where(keep, si, 0.0)
        xr, xi = xr + pr * sr - pi * si, xi + pr * si + pi * sr
    tr, ti = table
    return xr + tr * cr - ti * ci, xi + tr * ci + ti * cr


def _gelu_and_grad(y):
    k0 = math.sqrt(2.0 / math.pi)
    inner = k0 * (y + 0.044715 * y * y * y)
    th = jnp.tanh(inner)
    g = 0.5 * y * (1.0 + th)
    dg = 0.5 * (1.0 + th) + 0.5 * y * (1.0 - th * th) * k0 * (1.0 + 3.0 * 0.044715 * y * y)
    return g, dg


def ssm_fwd(u, lam_re, lam_im, bb_re, bb_im, cc_re, cc_im, d_skip, name):
    t = u.shape[0]
    tt = BLK

    def body(u_ref, lr_ref, li_ref, bbr_ref, bbi_ref, ccr_ref, cci_ref, d_ref, yg_ref, hr_ref, hi_ref, cr_ref, ci_ref):
        @pl.when(pl.program_id(0) == 0)
        def _():
            cr_ref[...] = jnp.zeros_like(cr_ref)
            ci_ref[...] = jnp.zeros_like(ci_ref)

        uv = u_ref[...]
        ub = uv.astype(BF16)
        for j in range(N_CHUNK):
            hs = slice(j * H_CHUNK, (j + 1) * H_CHUNK)
            us = slice(j * U_CHUNK, (j + 1) * U_CHUNK)
            hr_ref[:, hs] = _nn(ub[:, us], bbr_ref[j])
            hi_ref[:, hs] = _nn(ub[:, us], bbi_ref[j])
        p1, p2, p4, table = _lambda_tables(lr_ref[...], li_ref[...], False)

        def group(i, carry):
            cr, ci = carry
            rows = pl.ds(pl.multiple_of(i * SUB, SUB), SUB)
            xr, xi = _scan8(hr_ref[rows, :], hi_ref[rows, :], (p1, p2, p4), table, cr, ci, False)
            hr_ref[rows, :] = xr
            hi_ref[rows, :] = xi
            return xr[SUB - 1:SUB, :], xi[SUB - 1:SUB, :]

        cr, ci = lax.fori_loop(0, tt // SUB, group, (cr_ref[...], ci_ref[...]))
        cr_ref[...] = cr
        ci_ref[...] = ci
        for j in range(N_CHUNK):
            hs = slice(j * H_CHUNK, (j + 1) * H_CHUNK)
            us = slice(j * U_CHUNK, (j + 1) * U_CHUNK)
            y = (_nn(hr_ref[:, hs].astype(BF16), ccr_ref[j]) - _nn(hi_ref[:, hs].astype(BF16), cci_ref[j])
                 + d_ref[:, us] * uv[:, us])
            yg_ref[:, us] = _gelu_and_grad(y)[0].astype(BF16)

    return pl.pallas_call(
        body, name=name, grid=(t // tt,),
        in_specs=[_row_spec(tt, SSM_WIDTH), _VMEM, _VMEM, _VMEM, _VMEM, _VMEM, _VMEM, _VMEM],
        out_specs=[_row_spec(tt, SSM_WIDTH), _row_spec(tt, STATE_WIDTH), _row_spec(tt, STATE_WIDTH)],
        out_shape=[jax.ShapeDtypeStruct((t, SSM_WIDTH), BF16), jax.ShapeDtypeStruct((t, STATE_WIDTH), F32),
                   jax.ShapeDtypeStruct((t, STATE_WIDTH), F32)],
        scratch_shapes=[pltpu.VMEM((1, STATE_WIDTH), F32), pltpu.VMEM((1, STATE_WIDTH), F32)],
        compiler_params=_params(("arbitrary",)),
    )(u, lam_re, lam_im, bb_re, bb_im, cc_re, cc_im, d_skip)


def ssm_bwd(dyg, u, h_re, h_im, lam_re, lam_im, bb_re, bb_im, cc_re, cc_im, d_skip, name):
    t = u.shape[0]
    tt = BLK
    nt = t // tt

    def body(dyg_ref, u_ref, hr_ref, hi_ref, lr_ref, li_ref, bbr_ref, bbi_ref, ccr_ref, cci_ref, d_ref,
             du_ref, dlr_ref, dli_ref, dbbr_ref, dbbi_ref, dccr_ref, dcci_ref, dd_ref,
             ar_ref, ai_ref, cr_ref, ci_ref):
        step = pl.program_id(0)
        tile = nt - 1 - step

        @pl.when(step == 0)
        def _():
            for ref in (cr_ref, ci_ref, dlr_ref, dli_ref, dbbr_ref, dbbi_ref, dccr_ref, dcci_ref, dd_ref):
                ref[...] = jnp.zeros_like(ref)

        uv = u_ref[...]
        ub = uv.astype(BF16)
        dskip = d_ref[...]
        dy_chunks = []
        for j in range(N_CHUNK):
            hs = slice(j * H_CHUNK, (j + 1) * H_CHUNK)
            us = slice(j * U_CHUNK, (j + 1) * U_CHUNK)
            hrb = hr_ref[:, hs].astype(BF16)
            hib = hi_ref[:, hs].astype(BF16)
            y = _nn(hrb, ccr_ref[j]) - _nn(hib, cci_ref[j]) + dskip[:, us] * uv[:, us]
            dy = dyg_ref[:, us] * _gelu_and_grad(y)[1]
            dy_chunks.append(dy)
            dyb = dy.astype(BF16)
            dccr_ref[j] += _tn(hrb, dyb)
            dcci_ref[j] -= _tn(hib, dyb)
            ar_ref[:, hs] = _nt(dyb, ccr_ref[j])
            ai_ref[:, hs] = -_nt(dyb, cci_ref[j])
        dy_all = jnp.concatenate(dy_chunks, axis=1)
        dd_ref[...] += jnp.sum(dy_all * uv, axis=0, keepdims=True)

        lr, li = lr_ref[...], li_ref[...]
        p1, p2, p4, table = _lambda_tables(lr, -li, True)
        last_row = lax.broadcasted_iota(jnp.int32, (SUB, STATE_WIDTH), 0) == SUB - 1

        def group(k, carry):
            cr, ci, accr, acci = carry
            i = tt // SUB - 1 - k
            rows = pl.ds(pl.multiple_of(i * SUB, SUB), SUB)
            xr, xi = _scan8(ar_ref[rows, :], ai_ref[rows, :], (p1, p2, p4), table, cr, ci, True)
            ar_ref[rows, :] = xr
            ai_ref[rows, :] = xi
            nr = jnp.where(last_row, cr, pltpu.roll(xr, SUB - 1, 0))
            ni = jnp.where(last_row, ci, pltpu.roll(xi, SUB - 1, 0))
            hr, hi = hr_ref[rows, :], hi_ref[rows, :]
            return xr[0:1, :], xi[0:1, :], accr + nr * hr + ni * hi, acci + ni * hr - nr * hi

        zero = jnp.zeros((SUB, STATE_WIDTH), F32)
        cr, ci, accr, acci = lax.fori_loop(0, tt // SUB, group, (cr_ref[...], ci_ref[...], zero, zero))
        cr_ref[...] = cr
        ci_ref[...] = ci
        dlr_ref[...] += accr
        dli_ref[...] += acci

        row = tile * tt + lax.broadcasted_iota(jnp.int32, (tt, U_CHUNK), 0)
        for j in range(N_CHUNK):
            hs = slice(j * H_CHUNK, (j + 1) * H_CHUNK)
            us = slice(j * U_CHUNK, (j + 1) * U_CHUNK)
            arb = ar_ref[:, hs].astype(BF16)
            aib = ai_ref[:, hs].astype(BF16)
            dbbr_ref[j] += _tn(ub[:, us], arb)
            dbbi_ref[j] += _tn(ub[:, us], aib)
            du = _nt(arb, bbr_ref[j]) + _nt(aib, bbi_ref[j]) + dy_chunks[j] * dskip[:, us]
            du_ref[:, us] = jnp.where(row >= PAD_FRONT, du, 0.0).astype(BF16)

    rev = lambda i: (nt - 1 - i, 0)
    full = lambda shape: pl.BlockSpec(shape, lambda i: (0,) * len(shape))
    return pl.pallas_call(
        body, name=name, grid=(nt,),
        in_specs=[pl.BlockSpec((tt, SSM_WIDTH), rev), pl.BlockSpec((tt, SSM_WIDTH), rev),
                  pl.BlockSpec((tt, STATE_WIDTH), rev), pl.BlockSpec((tt, STATE_WIDTH), rev),
                  _VMEM, _VMEM, _VMEM, _VMEM, _VMEM, _VMEM, _VMEM],
        out_specs=[pl.BlockSpec((tt, SSM_WIDTH), rev), full((SUB, STATE_WIDTH)), full((SUB, STATE_WIDTH)),
                   full((N_CHUNK, U_CHUNK, H_CHUNK)), full((N_CHUNK, U_CHUNK, H_CHUNK)),
                   full((N_CHUNK, H_CHUNK, U_CHUNK)), full((N_CHUNK, H_CHUNK, U_CHUNK)), full((1, SSM_WIDTH))],
        out_shape=[jax.ShapeDtypeStruct((t, SSM_WIDTH), BF16),
                   jax.ShapeDtypeStruct((SUB, STATE_WIDTH), F32), jax.ShapeDtypeStruct((SUB, STATE_WIDTH), F32),
                   jax.ShapeDtypeStruct((N_CHUNK, U_CHUNK, H_CHUNK), F32),
                   jax.ShapeDtypeStruct((N_CHUNK, U_CHUNK, H_CHUNK), F32),
                   jax.ShapeDtypeStruct((N_CHUNK, H_CHUNK, U_CHUNK), F32),
                   jax.ShapeDtypeStruct((N_CHUNK, H_CHUNK, U_CHUNK), F32),
                   jax.ShapeDtypeStruct((1, SSM_WIDTH), F32)],
        scratch_shapes=[pltpu.VMEM((tt, STATE_WIDTH), F32), pltpu.VMEM((tt, STATE_WIDTH), F32),
                        pltpu.VMEM((1, STATE_WIDTH), F32), pltpu.VMEM((1, STATE_WIDTH), F32)],
        compiler_params=_params(("arbitrary",)),
    )(dyg, u, h_re, h_im, lam_re, lam_im, bb_re, bb_im, cc_re, cc_im, d_skip)


def merge_fwd(h, o, yg, gates, wap_t, wv_t, wgg_t, wout, name):
    t, d = h.shape
    tm = TOKEN_TILE

    def body(h_ref, o_ref, yg_ref, gt_ref, wap_ref, wv_ref, wgg_ref, wout_ref, ho_ref, mg_ref, a_ref, sv_ref, sg_ref):
        att = _nt(o_ref[...], wap_ref[...])
        ygv = yg_ref[...]
        sv = _nt(ygv, wv_ref[...])
        sg = _nt(ygv, wgg_ref[...])
        a_ref[...] = att
        sv_ref[...] = sv
        sg_ref[...] = sg
        merged = (jax.nn.sigmoid(gt_ref[:, 0:d]) * att
                  + jax.nn.sigmoid(gt_ref[:, d:2 * d]) * (sv * jax.nn.sigmoid(sg))).astype(BF16)
        mg_ref[...] = merged
        ho_ref[...] = h_ref[...] + _nn(merged, wout_ref[...])

    return pl.pallas_call(
        body, name=name, grid=(t // tm,),
        in_specs=[_row_spec(tm, d), _row_spec(tm, ATTN_WIDTH), _row_spec(tm, SSM_WIDTH), _row_spec(tm, 2 * d),
                  _VMEM, _VMEM, _VMEM, _VMEM],
        out_specs=[_row_spec(tm, d), _row_spec(tm, d), _row_spec(tm, d), _row_spec(tm, d), _row_spec(tm, d)],
        out_shape=[jax.ShapeDtypeStruct((t, d), F32), jax.ShapeDtypeStruct((t, d), BF16),
                   jax.ShapeDtypeStruct((t, d), F32), jax.ShapeDtypeStruct((t, d), F32),
                   jax.ShapeDtypeStruct((t, d), F32)],
        compiler_params=_params(("arbitrary",)),
    )(h, o, yg, gates, wap_t, wv_t, wgg_t, wout)


def merge_bwd(dh, gates, att, sv, sg, wap_t, wv_t, wgg_t, wout, dep, name):
    t, d = dh.shape
    tm = TOKEN_TILE

    def body(dh_ref, gt_ref, a_ref, sv_ref, sg_ref, wap_ref, wv_ref, wgg_ref, wout_ref, dep_ref,
             dgt_ref, da_ref, dsv_ref, dsg_ref, do_ref, dyg_ref, dhb_ref):
        dhb = dh_ref[...].astype(BF16)
        dhb_ref[...] = dhb
        dm = _nt(dhb, wout_ref[...])
        sig_a = jax.nn.sigmoid(gt_ref[:, 0:d])
        sig_s = jax.nn.sigmoid(gt_ref[:, d:2 * d])
        sig_g = jax.nn.sigmoid(sg_ref[...])
        svv = sv_ref[...]
        dgt_ref[:, 0:d] = (dm * a_ref[...] * sig_a * (1.0 - sig_a)).astype(BF16)
        dgt_ref[:, d:2 * d] = (dm * (svv * sig_g) * sig_s * (1.0 - sig_s)).astype(BF16)
        da = (dm * sig_a).astype(BF16)
        d_s = dm * sig_s
        dsv = (d_s * sig_g).astype(BF16)
        dsg = (d_s * svv * sig_g * (1.0 - sig_g)).astype(BF16)
        da_ref[...] = da
        dsv_ref[...] = dsv
        dsg_ref[...] = dsg
        do_ref[...] = _nn(da, wap_ref[...]).astype(BF16)
        dyg_ref[...] = _nn(dsv, wv_ref[...]) + _nn(dsg, wgg_ref[...])

    return pl.pallas_call(
        body, name=name, grid=(t // tm,),
        in_specs=[_row_spec(tm, d), _row_spec(tm, 2 * d), _row_spec(tm, d), _row_spec(tm, d), _row_spec(tm, d),
                  _VMEM, _VMEM, _VMEM, _VMEM, _ANY],
        out_specs=[_row_spec(tm, 2 * d), _row_spec(tm, d), _row_spec(tm, d), _row_spec(tm, d),
                   _row_spec(tm, ATTN_WIDTH), _row_spec(tm, SSM_WIDTH), _row_spec(tm, d)],
        out_shape=[jax.ShapeDtypeStruct((t, 2 * d), BF16), jax.ShapeDtypeStruct((t, d), BF16),
                   jax.ShapeDtypeStruct((t, d), BF16), jax.ShapeDtypeStruct((t, d), BF16),
                   jax.ShapeDtypeStruct((t, ATTN_WIDTH), BF16), jax.ShapeDtypeStruct((t, SSM_WIDTH), F32),
                   jax.ShapeDtypeStruct((t, d), BF16)],
        compiler_params=_params(("arbitrary",)),
    )(dh, gates, att, sv, sg, wap_t, wv_t, wgg_t, wout, dep)


def _adamw_math(w, g, m, v):
    mn = ADAM_B1 * m + (1.0 - ADAM_B1) * g
    vn = ADAM_B2 * v + (1.0 - ADAM_B2) * (g * g)
    m_hat = mn / (1.0 - ADAM_B1 ** ADAM_STEP)
    v_hat = vn / (1.0 - ADAM_B2 ** ADAM_STEP)
    return -ADAM_LR * (m_hat / (jnp.sqrt(v_hat) + ADAM_EPS) + ADAM_WD * w), mn, vn


def adamw_layer(w, g, m, v, layer, prev, name):
    _, rows, cols = w.shape
    tr = rows
    for cand in (512, 256):
        if rows > cand and rows % cand == 0:
            tr = cand
            break

    def body(w_ref, g_ref, m_ref, v_ref, *rest):
        go_ref, d_ref, mo_ref, vo_ref = rest[-4:]
        gv = g_ref[...]
        go_ref[0] = gv
        d_ref[0], mo_ref[0], vo_ref[0] = _adamw_math(w_ref[0], gv, m_ref[0], v_ref[0])

    spec3 = pl.BlockSpec((1, tr, cols), lambda r: (layer, r, 0))
    out = jax.ShapeDtypeStruct(w.shape, F32)
    extra = [] if prev is None else list(prev)
    return pl.pallas_call(
        body, name=name, grid=(rows // tr,),
        in_specs=[spec3, _row_spec(tr, cols), spec3, spec3] + [_ANY] * len(extra),
        out_specs=[spec3] * 4, out_shape=[out] * 4,
        input_output_aliases={4 + j: j for j in range(len(extra))},
        compiler_params=_params(("arbitrary",)),
    )(w, g, m, v, *extra)


def adamw(w, g, m, v, name):
    shape = w.shape
    as2d = lambda a: a.reshape(-1, shape[-1]) if a.ndim >= 2 else a.reshape(1, -1)
    w2, g2, m2, v2 = as2d(w), as2d(g), as2d(m), as2d(v)
    rows, cols = w2.shape
    tr = rows
    for cand in (1024, 704, 512, 256):
        if rows > cand and rows % cand == 0:
            tr = cand
            break

    def body(w_ref, g_ref, m_ref, v_ref, d_ref, mo_ref, vo_ref):
        d_ref[...], mo_ref[...], vo_ref[...] = _adamw_math(w_ref[...], g_ref[...], m_ref[...], v_ref[...])

    spec = _row_spec(tr, cols)
    out = jax.ShapeDtypeStruct((rows, cols), F32)
    d, mn, vn = pl.pallas_call(
        body, name=name, grid=(rows // tr,), in_specs=[spec] * 4, out_specs=[spec] * 3, out_shape=[out] * 3,
        compiler_params=_params(("arbitrary",)),
    )(w2, g2, m2, v2)
    return d.reshape(shape), mn.reshape(shape), vn.reshape(shape)


def _my_index():
    return 4 * lax.axis_index("x") + 2 * lax.axis_index("y") + lax.axis_index("c")


def _peer(p):
    return (lax.axis_index("x") ^ ((p >> 2) & 1), lax.axis_index("y") ^ ((p >> 1) & 1), lax.axis_index("c") ^ (p & 1))


_HBM = pl.BlockSpec(memory_space=pltpu.HBM)
_SEM = pl.BlockSpec(memory_space=pltpu.SEMAPHORE)
_EFFECT = pltpu.SideEffectType.DATAFLOW_SIDE_EFFECTING


class Exchange:
    def __init__(self, srcs, scatter, name):
        self.n = n = len(srcs)
        self.scatter = scatter
        self.name = name
        widths = sorted({s.shape[1] for s in srcs}, reverse=True)
        self.ncls = len(widths)
        self.cls = [widths.index(s.shape[1]) for s in srcs]
        self.cnts = [s.shape[0] // N_DEV if scatter else s.shape[0] for s in srcs]
        self.totals = [sum(c for c, k in zip(self.cnts, self.cls) if k == w) for w in range(self.ncls)]
        self.sizer = [max((k for k in range(n) if self.cls[k] == w), key=lambda k: self.cnts[k])
                      for w in range(self.ncls)]
        assert all(N_DEV * self.cnts[self.sizer[w]] >= self.totals[w] for w in range(self.ncls))
        if scatter:
            self.land_shapes = [(N_DEV, c, s.shape[1]) for s, c in zip(srcs, self.cnts)]
        else:
            self.land_shapes = [(N_DEV * c, s.shape[1]) for s, c in zip(srcs, self.cnts)]
        self.dtypes = [s.dtype for s in srcs]

    def _block(self, k, who):
        return pl.ds(pl.multiple_of(who * self.cnts[k], 16), self.cnts[k])

    def _sem(self, p, w):
        return (p - 1) * self.ncls + w

    def start(self, srcs, after):
        n = self.n

        def body(*refs):
            src, land = refs[:n], refs[n:2 * n]
            send_sems, recv_sems = refs[2 * n + 1], refs[2 * n + 2]
            token = refs[-1]
            me = _my_index()
            for p in range(1, N_DEV):
                for k in range(n):
                    if self.scatter:
                        s_ref, d_ref = src[k].at[self._block(k, me ^ p), :], land[k].at[me]
                    else:
                        s_ref, d_ref = src[k], land[k].at[self._block(k, me), :]
                    pltpu.make_async_remote_copy(
                        src_ref=s_ref, dst_ref=d_ref, send_sem=send_sems.at[self._sem(p, self.cls[k])],
                        recv_sem=recv_sems.at[self._sem(p, self.cls[k])], device_id=_peer(p),
                        device_id_type=MESH).start()
            token[...] = jnp.zeros_like(token)

        sems = pltpu.SemaphoreType.DMA(((N_DEV - 1) * self.ncls,))
        thru = [pltpu.HBM(s.shape, s.dtype) for s in srcs] + [pltpu.HBM(shp, dt) for shp, dt in
                                                               zip(self.land_shapes, self.dtypes)]
        lands = [pltpu.with_memory_space_constraint(lax.empty(shp, dt), pltpu.HBM)
                 for shp, dt in zip(self.land_shapes, self.dtypes)]
        out = pl.pallas_call(
            body, name=self.name + "_start",
            in_specs=[_HBM] * (2 * n) + [_ANY],
            out_shape=[sems, sems] + thru + [jax.ShapeDtypeStruct((8, 128), F32)],
            out_specs=[_SEM, _SEM] + [_HBM] * (2 * n) + [_VMEM],
            input_output_aliases={j: 2 + j for j in range(2 * n)},
            compiler_params=pltpu.CompilerParams(has_side_effects=_EFFECT),
        )(*[pltpu.with_memory_space_constraint(s, pltpu.HBM) for s in srcs], *lands, after)
        return out[:-1], out[-1]

    def wait(self, state, after):
        n = self.n
        send_sems, recv_sems = state[0], state[1]
        thru = state[2:]

        def body(*refs):
            src, land = refs[:n], refs[n:2 * n]
            send_sems, recv_sems = refs[2 * n], refs[2 * n + 1]
            for p in range(1, N_DEV):
                for w in range(self.ncls):
                    big = src[self.sizer[w]] if self.scatter else land[self.sizer[w]]
                    span = big.at[pl.ds(0, self.totals[w]), :]
                    copy = pltpu.make_async_remote_copy(
                        src_ref=span, dst_ref=span, send_sem=send_sems.at[self._sem(p, w)],
                        recv_sem=recv_sems.at[self._sem(p, w)],
                        device_id=_peer(p), device_id_type=MESH)
                    copy.wait_send()
                    copy.wait_recv()

        out = pl.pallas_call(
            body, name=self.name + "_wait",
            in_specs=[_HBM] * (2 * n) + [_SEM, _SEM, _ANY],
            out_shape=[pltpu.HBM(a.shape, a.dtype) for a in thru], out_specs=[_HBM] * (2 * n),
            input_output_aliases={j: j for j in range(2 * n)},
            compiler_params=pltpu.CompilerParams(has_side_effects=_EFFECT),
        )(*thru, send_sems, recv_sems, after)
        return out[:n], out[n:]

    def place(self, lands, srcs):
        n = self.n
        assert not self.scatter

        def body(*refs):
            src, land = refs[n:2 * n], refs[2 * n:3 * n]
            bufs, sems = refs[3 * n:4 * n], refs[-1]
            me = _my_index()
            loads = [pltpu.make_async_copy(src[k], bufs[k], sems.at[k]) for k in range(n)]
            stores = [pltpu.make_async_copy(bufs[k], land[k].at[self._block(k, me), :], sems.at[k]) for k in range(n)]
            for cp in loads:
                cp.start()
            for k in range(n):
                loads[k].wait()
                stores[k].start()
            for cp in stores:
                cp.wait()

        return pl.pallas_call(
            body, name=self.name + "_place", in_specs=[_ANY] * (2 * n), out_specs=[_ANY] * n,
            out_shape=[jax.ShapeDtypeStruct(a.shape, a.dtype) for a in lands],
            input_output_aliases={j: j for j in range(n)},
            scratch_shapes=[pltpu.VMEM(s.shape, s.dtype) for s in srcs] + [pltpu.SemaphoreType.DMA((n,))],
        )(*lands, *srcs)


def sum_blocks(landed, full, name):
    _, cnt, cols = landed.shape

    def body(land_ref, full_ref, o_ref, own_ref, sem):
        me = _my_index()
        own = pltpu.make_async_copy(full_ref.at[pl.ds(pl.multiple_of(me * cnt, 16), cnt), :], own_ref, sem)
        own.start()
        acc = land_ref[me ^ 1].astype(F32)
        for p in range(2, N_DEV):
            acc = acc + land_ref[me ^ p].astype(F32)
        own.wait()
        o_ref[...] = acc + own_ref[...].astype(F32)

    return pl.pallas_call(
        body, name=name, in_specs=[_VMEM, _ANY], out_specs=_VMEM,
        out_shape=jax.ShapeDtypeStruct((cnt, cols), F32),
        scratch_shapes=[pltpu.VMEM((cnt, cols), landed.dtype), pltpu.SemaphoreType.DMA],
        compiler_params=_params(),
    )(landed, full)


def sum_slots(slots, name):
    _, rows, cols = slots.shape
    tr = rows
    if rows > 512:
        for cand in (256, 128, 64, 32, 16, 8):
            if rows % cand == 0:
                tr = cand
                break

    def body(s_ref, o_ref):
        acc = s_ref[0].astype(F32)
        for j in range(1, N_DEV):
            acc = acc + s_ref[j].astype(F32)
        o_ref[...] = acc

    return pl.pallas_call(
        body, name=name, grid=(rows // tr,),
        in_specs=[pl.BlockSpec((N_DEV, tr, cols), lambda i: (0, i, 0))], out_specs=_row_spec(tr, cols),
        out_shape=jax.ShapeDtypeStruct((rows, cols), F32), compiler_params=_params(("arbitrary",)),
    )(slots)


BIG_T = ("ffn1_w_gate", "ffn1_w_up", "w_in", "ffn2_w_gate", "ffn2_w_up")
BIG_N = ("ffn1_w_down", "w_out", "ffn2_w_down")
HALF_T = ("w_attn_proj", "w_glu_v", "w_glu_g")
SMALL = ("ffn1_norm", "mix_norm", "attn_sinks", "ssm_a_re", "ssm_a_im", "ssm_log_dt", "ssm_b_re", "ssm_b_im",
         "ssm_c_re", "ssm_c_im", "ssm_d", "ffn2_norm", "final_norm")
PARTS = {"ffn1": ("ffn1_w_gate", "ffn1_w_up", "ffn1_w_down"),
         "mix": ("w_in", "w_out", "w_attn_proj", "w_glu_v", "w_glu_g"),
         "ffn2": ("ffn2_w_gate", "ffn2_w_up", "ffn2_w_down")}


def _to_rows(name, a):
    return a if name in BIG_N else jnp.swapaxes(a, -1, -2)


def local_step(x, tgt, get_weights, put_grads, small):
    seq, d = x.shape
    t = PAD_FRONT + N_META + seq
    cos_t, sin_t = rope_tables(t)
    row = lambda a: a.reshape(1, -1)
    saved = []
    h = None
    for i in range(DEPTH):
        s = {}
        w = dict(get_weights(i, "ffn1", h))
        if i == 0:
            h = jnp.concatenate([jnp.zeros((PAD_FRONT, d), F32), w["meta_tokens"], x], axis=0)
        s["h0"] = h
        h, s["n1"], s["a1"], s["b1"] = ffn_fwd(h, row(small["ffn1_norm"][i]), w["ffn1_w_gate"], w["ffn1_w_up"],
                                               w["ffn1_w_down"], f"ffn1_fwd_{i}")
        s["h1"] = h
        w.update(get_weights(i, "mix", h))
        s["n2"], s["qkv"], s["u"], s["gates"] = win_fwd(h, row(small["mix_norm"][i]), w["w_in"], cos_t, sin_t,
                                                        f"win_fwd_{i}")
        b_re_t = jnp.swapaxes(small["ssm_b_re"][i], 1, 2)
        b_im_t = jnp.swapaxes(small["ssm_b_im"][i], 1, 2)
        s["b_t"] = (b_re_t, b_im_t)
        lam_re, lam_im, bbar_re, bbar_im = ssm_prep(small["ssm_a_re"][i], small["ssm_a_im"][i],
                                                    small["ssm_log_dt"][i].reshape(-1, 1), b_re_t, b_im_t, f"ssm_prep_{i}")
        s["ssm"] = (row(lam_re), row(lam_im), _block_diag_b(bbar_re).astype(BF16), _block_diag_b(bbar_im).astype(BF16),
                    _block_diag_c(small["ssm_c_re"][i]).astype(BF16), _block_diag_c(small["ssm_c_im"][i]).astype(BF16),
                    row(small["ssm_d"][i]))
        s["yg"], s["h_re"], s["h_im"] = ssm_fwd(s["u"], *s["ssm"], f"ssm_fwd_{i}")
        s["o"] = attn_fwd(s["qkv"], row(small["attn_sinks"][i]), f"attn_fwd_{i}")
        h, s["merged"], s["att"], s["sv"], s["sg"] = merge_fwd(
            h, s["o"], s["yg"], s["gates"], w["w_attn_proj"], w["w_glu_v"], w["w_glu_g"], w["w_out"],
            f"merge_fwd_{i}")
        s["h2"] = h
        w.update(get_weights(i, "ffn2", h))
        h, s["n3"], s["a3"], s["b3"] = ffn_fwd(h, row(small["ffn2_norm"][i]), w["ffn2_w_gate"], w["ffn2_w_up"],
                                               w["ffn2_w_down"], f"ffn2_fwd_{i}")
        s["w"] = w
        saved.append(s)

    loss, dh, d_final = head_fwd_bwd(h, row(small["final_norm"]), tgt)
    gs = {k: [None] * DEPTH for k in SMALL if k != "final_norm"}
    dep = loss
    for i in reversed(range(DEPTH)):
        s = saved[i]
        w = s["w"]
        dh, da, db, sact, dhb, dg = ffn_bwd(dh, s["h2"], row(small["ffn2_norm"][i]), s["a3"], s["b3"], w["ffn2_w_gate"],
                                            w["ffn2_w_up"], w["ffn2_w_down"], dep, f"ffn2_bwd_{i}")
        gs["ffn2_norm"][i] = dg[0]
        dep = put_grads(i, "ffn2", {"ffn2_w_gate": tn_matmul(da, s["n3"], f"ffn2_dwg_{i}"),
                                    "ffn2_w_up": tn_matmul(db, s["n3"], f"ffn2_dwu_{i}"),
                                    "ffn2_w_down": tn_matmul(sact, dhb, f"ffn2_dwd_{i}")})

        dgates, datt, dsv, dsg, do, dyg, dhb = merge_bwd(dh, s["gates"], s["att"], s["sv"], s["sg"], w["w_attn_proj"],
                                                         w["w_glu_v"], w["w_glu_g"], w["w_out"], dep, f"merge_bwd_{i}")
        gmix = {"w_out": tn_matmul(s["merged"], dhb, f"dwout_{i}"),
                "w_attn_proj": tn_matmul(datt, s["o"], f"dwap_{i}"),
                "w_glu_v": tn_matmul(dsv, s["yg"], f"dwv_{i}"),
                "w_glu_g": tn_matmul(dsg, s["yg"], f"dwgg_{i}")}
        dqkv, dsink = attn_bwd(s["qkv"], do, row(small["attn_sinks"][i]), cos_t, sin_t, f"attn_bwd_{i}")
        gs["attn_sinks"][i] = dsink[:, 0]
        du, dl_re, dl_im, dbb_re, dbb_im, dcc_re, dcc_im, dd = ssm_bwd(dyg, s["u"], s["h_re"], s["h_im"], *s["ssm"],
                                                                      f"ssm_bwd_{i}")
        fold = lambda a: jnp.sum(a, axis=0).reshape(SSM_GROUPS, SSM_STATE)
        da_re, da_im, dldt, db_re_t, db_im_t = ssm_prep_bwd(
            small["ssm_a_re"][i], small["ssm_a_im"][i], small["ssm_log_dt"][i].reshape(-1, 1), *s["b_t"],
            fold(dl_re), fold(dl_im), _diag_of_b(dbb_re), _diag_of_b(dbb_im), f"ssm_prep_bwd_{i}")
        gs["ssm_a_re"][i], gs["ssm_a_im"][i], gs["ssm_log_dt"][i] = da_re, da_im, dldt[:, 0]
        gs["ssm_b_re"][i], gs["ssm_b_im"][i] = jnp.swapaxes(db_re_t, 1, 2), jnp.swapaxes(db_im_t, 1, 2)
        gs["ssm_c_re"][i], gs["ssm_c_im"][i] = _diag_of_c(dcc_re), _diag_of_c(dcc_im)
        gs["ssm_d"][i] = dd[0]
        dz = jnp.concatenate([dqkv, du, dgates], axis=1)
        gmix["w_in"] = tn_matmul(dz, s["n2"], f"dwin_{i}")
        dep = put_grads(i, "mix", gmix)
        dh, dg = win_bwd(dh, s["h1"], row(small["mix_norm"][i]), dz, w["w_in"], dep, f"win_bwd_{i}")
        gs["mix_norm"][i] = dg[0]

        dh, da, db, sact, dhb, dg = ffn_bwd(dh, s["h0"], row(small["ffn1_norm"][i]), s["a1"], s["b1"], w["ffn1_w_gate"],
                                            w["ffn1_w_up"], w["ffn1_w_down"], dep, f"ffn1_bwd_{i}")
        gs["ffn1_norm"][i] = dg[0]
        if i > 0:
            dep = put_grads(i, "ffn1", {"ffn1_w_gate": tn_matmul(da, s["n1"], f"ffn1_dwg_{i}"),
                                        "ffn1_w_up": tn_matmul(db, s["n1"], f"ffn1_dwu_{i}"),
                                        "ffn1_w_down": tn_matmul(sact, dhb, f"ffn1_dwd_{i}")})
        else:
            for k, xa, ya in (("ffn1_w_down", sact, dhb), ("ffn1_w_gate", da, s["n1"]), ("ffn1_w_up", db, s["n1"])):
                dep = put_grads(i, "ffn1", {k: tn_matmul(xa, ya, f"d_{k}_{i}", dep)})

    gs = {k: jnp.stack(v) for k, v in gs.items()}
    gs["final_norm"] = d_final[0]
    return loss[0, 0], dh[PAD_FRONT + N_META:], dh[PAD_FRONT:PAD_FRONT + N_META], gs


def _pack_rows(arrays, cols):
    flat = jnp.concatenate([a.reshape(-1) for a in arrays])
    rows = -(-flat.shape[0] // cols)
    rows = -(-rows // 16) * 16
    return jnp.pad(flat, (0, rows * cols - flat.shape[0])).reshape(rows, cols)


def _unpack_rows(packed, shapes):
    flat = packed.reshape(-1)
    out, off = [], 0
    for shp in shapes:
        n = math.prod(shp)
        out.append(flat[off:off + n].reshape(shp))
        off += n
    return out


def kernel(x, meta_tokens, ffn1_norm, ffn1_w_gate, ffn1_w_up, ffn1_w_down, mix_norm, w_in, attn_sinks, ssm_a_re, ssm_a_im, ssm_log_dt, ssm_b_re, ssm_b_im, ssm_c_re, ssm_c_im, ssm_d, w_attn_proj, w_glu_v, w_glu_g, w_out, ffn2_norm, ffn2_w_gate, ffn2_w_up, ffn2_w_down, final_norm, loss_target, m_meta_tokens, m_ffn1_norm, m_ffn1_w_gate, m_ffn1_w_up, m_ffn1_w_down, m_mix_norm, m_w_in, m_attn_sinks, m_ssm_a_re, m_ssm_a_im, m_ssm_log_dt, m_ssm_b_re, m_ssm_b_im, m_ssm_c_re, m_ssm_c_im, m_ssm_d, m_w_attn_proj, m_w_glu_v, m_w_glu_g, m_w_out, m_ffn2_norm, m_ffn2_w_gate, m_ffn2_w_up, m_ffn2_w_down, m_final_norm, v_meta_tokens, v_ffn1_norm, v_ffn1_w_gate, v_ffn1_w_up, v_ffn1_w_down, v_mix_norm, v_w_in, v_attn_sinks, v_ssm_a_re, v_ssm_a_im, v_ssm_log_dt, v_ssm_b_re, v_ssm_b_im, v_ssm_c_re, v_ssm_c_im, v_ssm_d, v_w_attn_proj, v_w_glu_v, v_w_glu_g, v_w_out, v_ffn2_norm, v_ffn2_w_gate, v_ffn2_w_up, v_ffn2_w_down, v_final_norm):
    names = ("meta_tokens", "ffn1_norm", "ffn1_w_gate", "ffn1_w_up", "ffn1_w_down", "mix_norm", "w_in", "attn_sinks",
             "ssm_a_re", "ssm_a_im", "ssm_log_dt", "ssm_b_re", "ssm_b_im", "ssm_c_re", "ssm_c_im", "ssm_d",
             "w_attn_proj", "w_glu_v", "w_glu_g", "w_out", "ffn2_norm", "ffn2_w_gate", "ffn2_w_up", "ffn2_w_down",
             "final_norm")
    weights = dict(zip(names, (meta_tokens, ffn1_norm, ffn1_w_gate, ffn1_w_up, ffn1_w_down, mix_norm, w_in, attn_sinks, ssm_a_re, ssm_a_im, ssm_log_dt, ssm_b_re, ssm_b_im, ssm_c_re, ssm_c_im, ssm_d, w_attn_proj, w_glu_v, w_glu_g, w_out, ffn2_norm, ffn2_w_gate, ffn2_w_up, ffn2_w_down, final_norm)))
    moments_m = dict(zip(names, (m_meta_tokens, m_ffn1_norm, m_ffn1_w_gate, m_ffn1_w_up, m_ffn1_w_down, m_mix_norm, m_w_in, m_attn_sinks, m_ssm_a_re, m_ssm_a_im, m_ssm_log_dt, m_ssm_b_re, m_ssm_b_im, m_ssm_c_re, m_ssm_c_im, m_ssm_d, m_w_attn_proj, m_w_glu_v, m_w_glu_g, m_w_out, m_ffn2_norm, m_ffn2_w_gate, m_ffn2_w_up, m_ffn2_w_down, m_final_norm)))
    moments_v = dict(zip(names, (v_meta_tokens, v_ffn1_norm, v_ffn1_w_gate, v_ffn1_w_up, v_ffn1_w_down, v_mix_norm, v_w_in, v_attn_sinks, v_ssm_a_re, v_ssm_a_im, v_ssm_log_dt, v_ssm_b_re, v_ssm_b_im, v_ssm_c_re, v_ssm_c_im, v_ssm_d, v_w_attn_proj, v_w_glu_v, v_w_glu_g, v_w_out, v_ffn2_norm, v_ffn2_w_gate, v_ffn2_w_up, v_ffn2_w_down, v_final_norm)))
    me = _my_index()

    gathers = {}
    token = jnp.zeros((8, 128), F32)
    for i in range(DEPTH):
        for part, ks in PARTS.items():
            shards = [_to_rows(k, weights[k][i]).astype(BF16) for k in ks]
            if (i, part) == (0, "ffn1"):
                shards.append(meta_tokens)
            ex = Exchange(shards, False, f"gather_{part}_{i}")
            state, token = ex.start(shards, token)
            gathers[i, part] = (ex, state, shards)
    all_started = token

    def get_weights(i, part, after):
        ex, state, shards = gathers[i, part]
        shards, lands = ex.wait(state, all_started if after is None else after)
        fulls = ex.place(lands, shards)
        got = dict(zip(PARTS[part], fulls))
        if (i, part) == (0, "ffn1"):
            got["meta_tokens"] = jnp.swapaxes(fulls[-1].reshape(N_DEV, N_META, 128), 0, 1).reshape(N_META, D_MODEL)
        return got

    scatters = []

    def put_grads(i, part, gdict):
        ks = list(gdict)
        srcs = [gdict[k] for k in ks]
        ex = Exchange(srcs, True, f"scatter_{part if len(ks) > 1 else ks[0]}_{i}")
        state, tok = ex.start(srcs, all_started)
        scatters.append((i, ks, ex, state))
        return tok

    small = {k: weights[k] for k in SMALL}
    loss, dx, dmeta, gs = local_step(x[0], loss_target[0], get_weights, put_grads, small)

    grads, deltas, new_m, new_v = {}, {}, {}, {}
    small_list = [loss.reshape(1), dmeta] + [gs[k] for k in SMALL]
    packed = _pack_rows(small_list, D_MODEL)
    small_ex = Exchange([packed], False, "gather_small")
    small_state, after = small_ex.start([packed], dx)

    updated = {}
    for i, ks, ex, state in scatters:
        partials, lands = ex.wait(state, after)
        for k, partial, slots in zip(ks, partials, lands):
            g = _to_rows(k, sum_blocks(slots, partial, f"sum_{k}_{i}"))
            updated[k] = adamw_layer(weights[k], g, moments_m[k], moments_v[k], i, updated.get(k), f"adamw_{k}_{i}")
            after = updated[k][0]
    for k, (g, d, mn, vn) in updated.items():
        grads[k], deltas[k], new_m[k], new_v[k] = g, d, mn, vn

    packed_own, packed_all = small_ex.wait(small_state, after)
    (packed_all,) = small_ex.place(packed_all, packed_own)
    total = sum_slots(packed_all.reshape(N_DEV, packed.shape[0], D_MODEL), "sum_small")
    pieces = _unpack_rows(total, [a.shape for a in small_list])
    loss_out = pieces[0][0]
    grads["meta_tokens"] = lax.dynamic_slice_in_dim(pieces[1], me * 128, 128, axis=1)
    for k, p in zip(SMALL, pieces[2:]):
        grads[k] = p
    for k in ("meta_tokens",) + SMALL:
        deltas[k], new_m[k], new_v[k] = adamw(weights[k], grads[k], moments_m[k], moments_v[k], f"adamw_{k}")
    return (loss_out, dx[None], *[grads[k] for k in names], *[deltas[k] for k in names],
            *[new_m[k] for k in names], *[new_v[k] for k in names])
```

```python
import functools
import math

import jax
import jax.numpy as jnp
from jax import lax
from jax.experimental import pallas as pl
from jax.experimental.pallas import tpu as pltpu

F32 = jnp.float32
BF16 = jnp.bfloat16

D_MODEL = 1024
DEPTH = 2
N_META = 16
HEAD_DIM = 64
N_Q_HEADS = 8
ATTN_WIDTH = 512
KV_WIDTH = 128
QKV_WIDTH = ATTN_WIDTH + 2 * KV_WIDTH
WINDOW = 128
BLK = 128
ROPE_THETA = 500000.0
ROT_DIM = 16
SSM_WIDTH = 512
SSM_GROUP = 16
SSM_GROUPS = 32
SSM_STATE = 64
STATE_WIDTH = SSM_GROUPS * SSM_STATE
D_FF = 2816
IN_WIDTH = 3328
EPS = 1e-6
NEG_INF = -1e30
PAD_FRONT = (-N_META) % BLK
N_DEV = 8

ADAM_LR = 0.001
ADAM_B1 = 0.9
ADAM_B2 = 0.999
ADAM_EPS = 1e-08
ADAM_WD = 0.01
ADAM_STEP = 10

VMEM_LIMIT = 56 * 1024 * 1024
TOKEN_TILE = 384
_VMEM = pl.BlockSpec(memory_space=pltpu.VMEM)
_SMEM = pl.BlockSpec(memory_space=pltpu.SMEM)
_ANY = pl.BlockSpec(memory_space=pl.ANY)
MESH = pl.DeviceIdType.MESH


def _params(sem=None):
    return pltpu.CompilerParams(dimension_semantics=sem, vmem_limit_bytes=VMEM_LIMIT)


def _nt(a, b):
    return lax.dot_general(a, b, (((1,), (1,)), ((), ())), preferred_element_type=F32)


def _nn(a, b):
    return jnp.dot(a, b, preferred_element_type=F32)


def _tn(a, b):
    return lax.dot_general(a, b, (((0,), (0,)), ((), ())), preferred_element_type=F32)


def _row_spec(tm, width):
    return pl.BlockSpec((tm, width), lambda i: (i, 0))


def _acc_spec(shape):
    return pl.BlockSpec(shape, lambda i: (0,) * len(shape))


def _rms_stats(x):
    r = lax.rsqrt(jnp.mean(x * x, axis=-1, keepdims=True) + EPS)
    return x * r, r


def _rms_bwd(dn, xh, r, g):
    dg = jnp.sum(dn * xh, axis=0, keepdims=True)
    dxh = dn * g
    dx = r * (dxh - xh * jnp.mean(dxh * xh, axis=-1, keepdims=True))
    return dx, dg


def ffn_fwd(h, g, wg_t, wu_t, wd, name):
    t, d = h.shape
    f = wd.shape[0]
    tm = TOKEN_TILE

    def body(h_ref, g_ref, wg_ref, wu_ref, wd_ref, ho_ref, n_ref, a_ref, b_ref):
        x = h_ref[...]
        xh, _ = _rms_stats(x)
        n = (xh * g_ref[...]).astype(BF16)
        n_ref[...] = n
        a = _nt(n, wg_ref[...])
        b = _nt(n, wu_ref[...])
        a_ref[...] = a.astype(BF16)
        b_ref[...] = b.astype(BF16)
        s = (a * jax.nn.sigmoid(a) * b).astype(BF16)
        ho_ref[...] = x + 0.5 * _nn(s, wd_ref[...])

    return pl.pallas_call(
        body, name=name, grid=(t // tm,),
        in_specs=[_row_spec(tm, d), _acc_spec((1, d)), _VMEM, _VMEM, _VMEM],
        out_specs=[_row_spec(tm, d), _row_spec(tm, d), _row_spec(tm, f), _row_spec(tm, f)],
        out_shape=[jax.ShapeDtypeStruct((t, d), F32), jax.ShapeDtypeStruct((t, d), BF16),
                   jax.ShapeDtypeStruct((t, f), BF16), jax.ShapeDtypeStruct((t, f), BF16)],
        compiler_params=_params(("arbitrary",)),
    )(h, g, wg_t, wu_t, wd)


def ffn_bwd(dh, h, g, a, b, wg_t, wu_t, wd, dep, name):
    t, d = h.shape
    f = wd.shape[0]
    tm = TOKEN_TILE // 2

    def body(dh_ref, h_ref, g_ref, a_ref, b_ref, wg_ref, wu_ref, wd_ref, dep_ref,
             dhi_ref, da_ref, db_ref, s_ref, dhb_ref, dg_ref):
        dh_t = dh_ref[...]
        dhb = (0.5 * dh_t).astype(BF16)
        dhb_ref[...] = dhb
        ds = _nt(dhb, wd_ref[...])
        av = a_ref[...].astype(F32)
        bv = b_ref[...].astype(F32)
        sig = jax.nn.sigmoid(av)
        sl = av * sig
        s_ref[...] = (sl * bv).astype(BF16)
        da = (ds * bv * (sig * (1.0 + av * (1.0 - sig)))).astype(BF16)
        db = (ds * sl).astype(BF16)
        da_ref[...] = da
        db_ref[...] = db
        dn = _nn(da, wg_ref[...]) + _nn(db, wu_ref[...])
        xh, r = _rms_stats(h_ref[...])
        dx, dg = _rms_bwd(dn, xh, r, g_ref[...])
        dhi_ref[...] = dh_t + dx

        @pl.when(pl.program_id(0) == 0)
        def _():
            dg_ref[...] = jnp.zeros_like(dg_ref)

        dg_ref[...] += dg

    return pl.pallas_call(
        body, name=name, grid=(t // tm,),
        in_specs=[_row_spec(tm, d), _row_spec(tm, d), _acc_spec((1, d)), _row_spec(tm, f), _row_spec(tm, f),
                  _VMEM, _VMEM, _VMEM, _ANY],
        out_specs=[_row_spec(tm, d), _row_spec(tm, f), _row_spec(tm, f), _row_spec(tm, f), _row_spec(tm, d),
                   _acc_spec((1, d))],
        out_shape=[jax.ShapeDtypeStruct((t, d), F32), jax.ShapeDtypeStruct((t, f), BF16),
                   jax.ShapeDtypeStruct((t, f), BF16), jax.ShapeDtypeStruct((t, f), BF16),
                   jax.ShapeDtypeStruct((t, d), BF16), jax.ShapeDtypeStruct((1, d), F32)],
        compiler_params=_params(("arbitrary",)),
    )(dh, h, g, a, b, wg_t, wu_t, wd, dep)


DW_TILE = 256


def tn_matmul(x, y, name, dep=None):
    xs = list(x) if isinstance(x, (list, tuple)) else [x]
    t = xs[0].shape[0]
    n = y.shape[1]
    bm = DW_TILE
    tiles = [a.shape[1] // bm for a in xs]
    offs = [sum(tiles[:k]) for k in range(len(xs))]
    deps = [] if dep is None else [dep]

    def body(*refs):
        y_ref, o_ref = refs[len(xs)], refs[-1]
        i = pl.program_id(0)
        for k in range(len(xs)):
            @pl.when((i >= offs[k]) & (i < offs[k] + tiles[k]))
            def _(k=k):
                o_ref[...] = _tn(refs[k][...], y_ref[...]).astype(BF16)

    def x_spec(k):
        return pl.BlockSpec((t, bm), lambda i: (0, jnp.clip(i - offs[k], 0, tiles[k] - 1)))

    return pl.pallas_call(
        body, name=name, grid=(sum(tiles),),
        in_specs=[x_spec(k) for k in range(len(xs))] + [_VMEM] + [_ANY] * len(deps),
        out_specs=pl.BlockSpec((bm, n), lambda i: (i, 0)),
        out_shape=jax.ShapeDtypeStruct((sum(tiles) * bm, n), BF16),
        compiler_params=_params(("arbitrary",)),
    )(*xs, y, *deps)


def head_fwd_bwd(h, g, tgt):
    t, d = h.shape

    def body(h_ref, g_ref, t_ref, loss_ref, dh_ref, dg_ref):
        i = pl.program_id(0)
        xh, r = _rms_stats(h_ref[...])
        gv = g_ref[...]
        valid = (i > 0).astype(F32)
        e = (xh * gv - t_ref[...]) * valid
        dx, dg = _rms_bwd(e * (1.0 / d), xh, r, gv)
        dh_ref[...] = dx

        @pl.when(i == 0)
        def _():
            dg_ref[...] = jnp.zeros_like(dg_ref)
            loss_ref[...] = jnp.zeros_like(loss_ref)

        dg_ref[...] += dg
        loss_ref[...] += jnp.sum(e * e) * (0.5 / d)

    return pl.pallas_call(
        body, name="head", grid=(t // BLK,),
        in_specs=[_row_spec(BLK, d), _acc_spec((1, d)),
                  pl.BlockSpec((BLK, d), lambda i: (jnp.maximum(i - 1, 0), 0))],
        out_specs=[_acc_spec((1, 128)), _row_spec(BLK, d), _acc_spec((1, d))],
        out_shape=[jax.ShapeDtypeStruct((1, 128), F32), jax.ShapeDtypeStruct((t, d), F32),
                   jax.ShapeDtypeStruct((1, d), F32)],
        compiler_params=_params(("arbitrary",)),
    )(h, g, tgt)


def rope_tables(t):
    pos = jnp.arange(t, dtype=F32) - PAD_FRONT
    inv_freq = ROPE_THETA ** (-jnp.arange(0, ROT_DIM, 2, dtype=F32) / ROT_DIM)
    ang = pos[:, None] * inv_freq[None, :]
    cos, sin = jnp.cos(ang), jnp.sin(ang)
    ones = jnp.ones((t, HEAD_DIM - ROT_DIM), F32)
    cos_h = jnp.concatenate([cos, cos, ones], axis=1)
    sin_h = jnp.concatenate([-sin, sin, 0.0 * ones], axis=1)
    return jnp.concatenate([cos_h, cos_h], axis=1), jnp.concatenate([sin_h, sin_h], axis=1)


def _swap_halves(x):
    n = x.shape[1]
    lane = lax.broadcasted_iota(jnp.int32, x.shape, 1)
    return jnp.where(lane % HEAD_DIM < ROT_DIM // 2, pltpu.roll(x, n - ROT_DIM // 2, 1), pltpu.roll(x, ROT_DIM // 2, 1))


def _rope(x, cos_t, sin_t, sign):
    return x * cos_t + sign * (_swap_halves(x) * sin_t)


def win_fwd(h, g, win_t, cos_t, sin_t, name):
    t, d = h.shape
    tm = TOKEN_TILE

    def body(h_ref, g_ref, w_ref, c_ref, s_ref, n_ref, qkv_ref, u_ref, gates_ref):
        xh, _ = _rms_stats(h_ref[...])
        n = (xh * g_ref[...]).astype(BF16)
        n_ref[...] = n
        z = _nt(n, w_ref[...])
        c, s = c_ref[...], s_ref[...]
        for j in range((ATTN_WIDTH + KV_WIDTH) // 128):
            qkv_ref[:, j * 128:(j + 1) * 128] = _rope(z[:, j * 128:(j + 1) * 128], c, s, 1.0).astype(BF16)
        qkv_ref[:, ATTN_WIDTH + KV_WIDTH:QKV_WIDTH] = z[:, ATTN_WIDTH + KV_WIDTH:QKV_WIDTH].astype(BF16)
        u_ref[...] = z[:, QKV_WIDTH:QKV_WIDTH + SSM_WIDTH]
        gates_ref[...] = z[:, QKV_WIDTH + SSM_WIDTH:]

    return pl.pallas_call(
        body, name=name, grid=(t // tm,),
        in_specs=[_row_spec(tm, d), _acc_spec((1, d)), _VMEM, _row_spec(tm, 128), _row_spec(tm, 128)],
        out_specs=[_row_spec(tm, d), _row_spec(tm, QKV_WIDTH), _row_spec(tm, SSM_WIDTH), _row_spec(tm, 2 * d)],
        out_shape=[jax.ShapeDtypeStruct((t, d), BF16), jax.ShapeDtypeStruct((t, QKV_WIDTH), BF16),
                   jax.ShapeDtypeStruct((t, SSM_WIDTH), F32), jax.ShapeDtypeStruct((t, 2 * d), F32)],
        compiler_params=_params(("arbitrary",)),
    )(h, g, win_t, cos_t, sin_t)


def win_bwd(dh, h, g, dqkv, du, dgates, win_t, dep, name):
    t, d = h.shape
    tm = TOKEN_TILE

    def body(dh_ref, h_ref, g_ref, dqkv_ref, du_ref, dgt_ref, w_ref, dep_ref, dhi_ref, dg_ref):
        dn = (_nn(dqkv_ref[...], w_ref[0:QKV_WIDTH, :])
              + _nn(du_ref[...], w_ref[QKV_WIDTH:QKV_WIDTH + SSM_WIDTH, :])
              + _nn(dgt_ref[...], w_ref[QKV_WIDTH + SSM_WIDTH:, :]))
        xh, r = _rms_stats(h_ref[...])
        dx, dg = _rms_bwd(dn, xh, r, g_ref[...])
        dhi_ref[...] = dh_ref[...] + dx

        @pl.when(pl.program_id(0) == 0)
        def _():
            dg_ref[...] = jnp.zeros_like(dg_ref)

        dg_ref[...] += dg

    return pl.pallas_call(
        body, name=name, grid=(t // tm,),
        in_specs=[_row_spec(tm, d), _row_spec(tm, d), _acc_spec((1, d)), _row_spec(tm, QKV_WIDTH),
                  _row_spec(tm, SSM_WIDTH), _row_spec(tm, 2 * d), _VMEM, _ANY],
        out_specs=[_row_spec(tm, d), _acc_spec((1, d))],
        out_shape=[jax.ShapeDtypeStruct((t, d), F32), jax.ShapeDtypeStruct((1, d), F32)],
        compiler_params=_params(("arbitrary",)),
    )(dh, h, g, dqkv, du, dgates, win_t, dep)


def _attn_mask(blk):
    q_pos = blk * BLK + lax.broadcasted_iota(jnp.int32, (BLK, 3 * BLK), 0) - PAD_FRONT
    col = lax.broadcasted_iota(jnp.int32, (BLK, 3 * BLK), 1)
    part = col // BLK
    k_pos = jnp.where(part == 0, col, (blk + part - 2) * BLK + (col - part * BLK)) - PAD_FRONT
    dist = q_pos - k_pos
    meta_ok = (part == 0) & (k_pos >= 0) & (dist >= 0)
    band_ok = (part > 0) & (k_pos >= N_META) & (dist >= 0) & (dist < WINDOW)
    return meta_ok | band_ok


def _head_halves(x128, kv):
    x = x128.astype(F32)
    lane = lax.broadcasted_iota(jnp.int32, x.shape, 1)
    swapped = pltpu.roll(x, HEAD_DIM, 1)
    lo, hi = (x, swapped) if kv == 0 else (swapped, x)
    return jnp.where(lane < HEAD_DIM, lo, 0.0).astype(BF16), jnp.where(lane >= HEAD_DIM, hi, 0.0).astype(BF16)


def _gather_keys(meta_ref, prev_ref, cur_ref, lo):
    return jnp.concatenate([meta_ref[:, lo:lo + 128], prev_ref[:, lo:lo + 128], cur_ref[:, lo:lo + 128]], axis=0)


def _softmax_with_sink(s, mask, sink):
    s = jnp.where(mask, s * (HEAD_DIM ** -0.5), NEG_INF)
    m = jnp.maximum(jnp.max(s, axis=-1, keepdims=True), sink)
    p = jnp.exp(s - m)
    p_sink = jnp.exp(sink - m)
    inv = 1.0 / (jnp.sum(p, axis=-1, keepdims=True) + p_sink)
    return p * inv, p_sink * inv


def attn_fwd(qkv, sinks, name):
    t = qkv.shape[0]
    nb = t // BLK

    def body(sink_ref, meta_ref, prev_ref, cur_ref, o_ref):
        blk = pl.program_id(0)
        mask = _attn_mask(blk)
        k128 = _gather_keys(meta_ref, prev_ref, cur_ref, ATTN_WIDTH)
        v128 = _gather_keys(meta_ref, prev_ref, cur_ref, ATTN_WIDTH + KV_WIDTH)
        for kv in range(2):
            k_lo, k_hi = _head_halves(k128, kv)
            v_lo, v_hi = _head_halves(v128, kv)
            for pair in range(2):
                lanes = slice((2 * kv + pair) * 128, (2 * kv + pair + 1) * 128)
                q128 = cur_ref[:, lanes]
                head = 4 * kv + 2 * pair
                p_a, _ = _softmax_with_sink(_nt(q128, k_lo), mask, sink_ref[0, head])
                p_b, _ = _softmax_with_sink(_nt(q128, k_hi), mask, sink_ref[0, head + 1])
                o_ref[:, lanes] = (_nn(p_a.astype(BF16), v_lo) + _nn(p_b.astype(BF16), v_hi)).astype(BF16)

    blk_spec = lambda f: pl.BlockSpec((BLK, QKV_WIDTH), f)
    return pl.pallas_call(
        body, name=name, grid=(nb,),
        in_specs=[_SMEM, blk_spec(lambda i: (0, 0)), blk_spec(lambda i: (jnp.maximum(i - 1, 0), 0)),
                  blk_spec(lambda i: (i, 0))],
        out_specs=_row_spec(BLK, ATTN_WIDTH),
        out_shape=jax.ShapeDtypeStruct((t, ATTN_WIDTH), BF16),
        compiler_params=_params(("arbitrary",)),
    )(sinks, qkv, qkv, qkv)


def attn_bwd(qkv, do, sinks, cos_t, sin_t, name):
    t = qkv.shape[0]
    nb = t // BLK

    def body(sink_ref, meta_ref, prev_ref, cur_ref, do_ref, c_ref, s_ref, dqkv_ref, dsink_ref, carry_ref, macc_ref):
        step = pl.program_id(0)
        blk = nb - 1 - step

        @pl.when(step == 0)
        def _():
            dsink_ref[...] = jnp.zeros_like(dsink_ref)
            carry_ref[...] = jnp.zeros_like(carry_ref)
            macc_ref[...] = jnp.zeros_like(macc_ref)

        mask = _attn_mask(blk)
        lane = lax.broadcasted_iota(jnp.int32, (3 * BLK, 128), 1)
        k128 = _gather_keys(meta_ref, prev_ref, cur_ref, ATTN_WIDTH)
        v128 = _gather_keys(meta_ref, prev_ref, cur_ref, ATTN_WIDTH + KV_WIDTH)
        cos_b, sin_b = c_ref[...], s_ref[...]
        dk_heads, dv_heads = [], []
        for kv in range(2):
            k_lo, k_hi = _head_halves(k128, kv)
            v_lo, v_hi = _head_halves(v128, kv)
            dk_acc = jnp.zeros((3 * BLK, 128), F32)
            dv_acc = jnp.zeros((3 * BLK, 128), F32)
            for pair in range(2):
                lanes = slice((2 * kv + pair) * 128, (2 * kv + pair + 1) * 128)
                q128 = cur_ref[:, lanes]
                do128 = do_ref[:, lanes]
                head = 4 * kv + 2 * pair
                ds_pair, p_pair = [], []
                for half, (k_h, v_h) in enumerate(((k_lo, v_lo), (k_hi, v_hi))):
                    p, p_sink = _softmax_with_sink(_nt(q128, k_h), mask, sink_ref[0, head + half])
                    dp = _nt(do128, v_h)
                    dsum = jnp.sum(p * dp, axis=-1, keepdims=True)
                    ds_pair.append((p * (dp - dsum) * (HEAD_DIM ** -0.5)).astype(BF16))
                    p_pair.append(p.astype(BF16))
                    dsink = -jnp.sum(p_sink * dsum, axis=0, keepdims=True)
                    dsink_ref[head + half:head + half + 1, :] += jnp.broadcast_to(dsink, (1, 128))
                dq = _nn(ds_pair[0], k_lo) + _nn(ds_pair[1], k_hi)
                dqkv_ref[:, lanes] = _rope(dq, cos_b, sin_b, -1.0).astype(BF16)
                dk_acc += jnp.where(lane < HEAD_DIM, _tn(ds_pair[0], q128), _tn(ds_pair[1], q128))
                dv_acc += jnp.where(lane < HEAD_DIM, _tn(p_pair[0], do128), _tn(p_pair[1], do128))
            dk_heads.append(dk_acc + pltpu.roll(dk_acc, HEAD_DIM, 1))
            dv_heads.append(dv_acc + pltpu.roll(dv_acc, HEAD_DIM, 1))
        dkv = jnp.concatenate([jnp.where(lane < HEAD_DIM, dk_heads[0], dk_heads[1]),
                               jnp.where(lane < HEAD_DIM, dv_heads[0], dv_heads[1])], axis=1)
        macc_ref[...] += dkv[0:BLK]
        is_last = (blk == 0).astype(F32)
        mine = dkv[2 * BLK:3 * BLK] + carry_ref[...] + is_last * macc_ref[...]
        carry_ref[...] = dkv[BLK:2 * BLK]
        dqkv_ref[:, ATTN_WIDTH:ATTN_WIDTH + KV_WIDTH] = _rope(mine[:, 0:128], cos_b, sin_b, -1.0).astype(BF16)
        dqkv_ref[:, ATTN_WIDTH + KV_WIDTH:QKV_WIDTH] = mine[:, 128:256].astype(BF16)

    rev = lambda i: nb - 1 - i
    blk_spec = lambda f: pl.BlockSpec((BLK, QKV_WIDTH), f)
    return pl.pallas_call(
        body, name=name, grid=(nb,),
        in_specs=[_SMEM, blk_spec(lambda i: (0, 0)), blk_spec(lambda i: (jnp.maximum(rev(i) - 1, 0), 0)),
                  blk_spec(lambda i: (rev(i), 0)), pl.BlockSpec((BLK, ATTN_WIDTH), lambda i: (rev(i), 0)),
                  pl.BlockSpec((BLK, 128), lambda i: (rev(i), 0)), pl.BlockSpec((BLK, 128), lambda i: (rev(i), 0))],
        out_specs=[pl.BlockSpec((BLK, QKV_WIDTH), lambda i: (rev(i), 0)), _acc_spec((N_Q_HEADS, 128))],
        out_shape=[jax.ShapeDtypeStruct((t, QKV_WIDTH), BF16), jax.ShapeDtypeStruct((N_Q_HEADS, 128), F32)],
        scratch_shapes=[pltpu.VMEM((BLK, 256), F32), pltpu.VMEM((BLK, 256), F32)],
        compiler_params=_params(("arbitrary",)),
    )(sinks, qkv, qkv, qkv, do, cos_t, sin_t)


def _cmul(ar, ai, br, bi):
    return ar * br - ai * bi, ar * bi + ai * br


def ssm_prep(a_re, a_im, log_dt, b_re_t, b_im_t, name):
    def body(ar_ref, ai_ref, ldt_ref, br_ref, bi_ref, lr_ref, li_ref, bbr_ref, bbi_ref):
        ar, ai = ar_ref[...], ai_ref[...]
        dt = jnp.exp(ldt_ref[...])
        mag = jnp.exp(ar * dt)
        lr = mag * jnp.cos(ai * dt)
        li = mag * jnp.sin(ai * dt)
        den = ar * ar + ai * ai
        nr = lr - 1.0
        cr = ((nr * ar + li * ai) / den)[:, None, :]
        ci = ((li * ar - nr * ai) / den)[:, None, :]
        br, bi = br_ref[...], bi_ref[...]
        lr_ref[...] = lr
        li_ref[...] = li
        bbr_ref[...] = cr * br - ci * bi
        bbi_ref[...] = cr * bi + ci * br

    gp = jax.ShapeDtypeStruct(a_re.shape, F32)
    gcp = jax.ShapeDtypeStruct(b_re_t.shape, F32)
    return pl.pallas_call(body, name=name, out_shape=[gp, gp, gcp, gcp],
                          in_specs=[_VMEM] * 5, out_specs=[_VMEM] * 4)(a_re, a_im, log_dt, b_re_t, b_im_t)


def ssm_prep_bwd(a_re, a_im, log_dt, b_re_t, b_im_t, dl_re, dl_im, dbb_re, dbb_im, name):
    def body(ar_ref, ai_ref, ldt_ref, br_ref, bi_ref, dlr_ref, dli_ref, dbbr_ref, dbbi_ref,
             dar_ref, dai_ref, dldt_ref, dbr_ref, dbi_ref):
        ar, ai = ar_ref[...], ai_ref[...]
        dt = jnp.exp(ldt_ref[...])
        mag = jnp.exp(ar * dt)
        lr = mag * jnp.cos(ai * dt)
        li = mag * jnp.sin(ai * dt)
        den = ar * ar + ai * ai
        nr = lr - 1.0
        cr = (nr * ar + li * ai) / den
        ci = (li * ar - nr * ai) / den
        br, bi = br_ref[...], bi_ref[...]
        dbbr, dbbi = dbbr_ref[...], dbbi_ref[...]
        dbr_ref[...] = cr[:, None, :] * dbbr + ci[:, None, :] * dbbi
        dbi_ref[...] = cr[:, None, :] * dbbi - ci[:, None, :] * dbbr
        dcr = jnp.sum(br * dbbr + bi * dbbi, axis=1)
        dci = jnp.sum(br * dbbi - bi * dbbr, axis=1)
        d_num_r = dcr / den
        d_num_i = dci / den
        d_den = -(dcr * cr + dci * ci) / den
        d_lr = dlr_ref[...] + d_num_r * ar - d_num_i * ai
        d_li = dli_ref[...] + d_num_r * ai + d_num_i * ar
        d_ar = d_num_r * nr + d_num_i * li + d_den * 2.0 * ar
        d_ai = d_num_r * li - d_num_i * nr + d_den * 2.0 * ai
        d_mag = (d_lr * lr + d_li * li) / mag
        d_theta = d_li * lr - d_lr * li
        d_ardt = d_mag * mag
        dar_ref[...] = d_ar + d_ardt * dt
        dai_ref[...] = d_ai + d_theta * dt
        d_dt = jnp.sum(d_ardt * ar + d_theta * ai, axis=1, keepdims=True)
        dldt_ref[...] = d_dt * dt

    gp = jax.ShapeDtypeStruct(a_re.shape, F32)
    gcp = jax.ShapeDtypeStruct(b_re_t.shape, F32)
    return pl.pallas_call(body, name=name, out_shape=[gp, gp, jax.ShapeDtypeStruct(log_dt.shape, F32), gcp, gcp],
                          in_specs=[_VMEM] * 9, out_specs=[_VMEM] * 5,
                          )(a_re, a_im, log_dt, b_re_t, b_im_t, dl_re, dl_im, dbb_re, dbb_im)


N_CHUNK = 4
U_CHUNK = SSM_WIDTH // N_CHUNK
H_CHUNK = STATE_WIDTH // N_CHUNK
SUB = 8


def _block_diag_b(bb):
    x = bb.reshape(N_CHUNK, 8, SSM_GROUP, 1, SSM_STATE)
    same = (jnp.arange(8)[:, None] == jnp.arange(8)[None, :])[None, :, None, :, None]
    return jnp.where(same, x, 0.0).reshape(N_CHUNK, U_CHUNK, H_CHUNK)


def _block_diag_c(c):
    x = jnp.swapaxes(c.reshape(N_CHUNK, 8, SSM_GROUP, SSM_STATE), 2, 3)[:, :, :, None, :]
    same = (jnp.arange(8)[:, None] == jnp.arange(8)[None, :])[None, :, None, :, None]
    return jnp.where(same, x, 0.0).reshape(N_CHUNK, H_CHUNK, U_CHUNK)


def _diag_of_b(m):
    x = m.reshape(N_CHUNK, 8, SSM_GROUP, 8, SSM_STATE)
    return jnp.stack([x[:, g, :, g, :] for g in range(8)], axis=1).reshape(SSM_GROUPS, SSM_GROUP, SSM_STATE)


def _diag_of_c(m):
    x = m.reshape(N_CHUNK, 8, SSM_STATE, 8, SSM_GROUP)
    d = jnp.stack([x[:, g, :, g, :] for g in range(8)], axis=1)
    return jnp.swapaxes(d, 2, 3).reshape(SSM_GROUPS, SSM_GROUP, SSM_STATE)


def _lambda_tables(lr, li, reverse):
    p1 = (lr, li)
    p2 = _cmul(*p1, *p1)
    p4 = _cmul(*p2, *p2)
    rows = [p1]
    for _ in range(SUB - 1):
        rows.append(_cmul(*rows[-1], *p1))
    if reverse:
        rows = rows[::-1]
    return p1, p2, p4, (jnp.concatenate([r[0] for r in rows], axis=0), jnp.concatenate([r[1] for r in rows], axis=0))


def _scan8(xr, xi, pows, table, cr, ci, reverse):
    row = lax.broadcasted_iota(jnp.int32, xr.shape, 0)
    for d, (pr, pi) in zip((1, 2, 4), pows):
        if reverse:
            sr, si = pltpu.roll(xr, SUB - d, 0), pltpu.roll(xi, SUB - d, 0)
            keep = row < SUB - d
        else:
            sr, si = pltpu.roll(xr, d, 0), pltpu.roll(xi, d, 0)
            keep = row >= d
        sr = jnp.where(keep, sr, 0.0)
        si = jnp.where(keep, si, 0.0)
        xr, xi = xr + pr * sr - pi * si, xi + pr * si + pi * sr
    tr, ti = table
    return xr + tr * cr - ti * ci, xi + tr * ci + ti * cr


def _gelu_and_grad(y):
    k0 = math.sqrt(2.0 / math.pi)
    inner = k0 * (y + 0.044715 * y * y * y)
    th = jnp.tanh(inner)
    g = 0.5 * y * (1.0 + th)
    dg = 0.5 * (1.0 + th) + 0.5 * y * (1.0 - th * th) * k0 * (1.0 + 3.0 * 0.044715 * y * y)
    return g, dg


def ssm_fwd(u, lam_re, lam_im, bb_re, bb_im, cc_re, cc_im, d_skip, name):
    t = u.shape[0]
    tt = BLK

    def body(u_ref, lr_ref, li_ref, bbr_ref, bbi_ref, ccr_ref, cci_ref, d_ref, yg_ref, hr_ref, hi_ref, cr_ref, ci_ref):
        @pl.when(pl.program_id(0) == 0)
        def _():
            cr_ref[...] = jnp.zeros_like(cr_ref)
            ci_ref[...] = jnp.zeros_like(ci_ref)

        uv = u_ref[...]
        ub = uv.astype(BF16)
        for j in range(N_CHUNK):
            hs = slice(j * H_CHUNK, (j + 1) * H_CHUNK)
            us = slice(j * U_CHUNK, (j + 1) * U_CHUNK)
            hr_ref[:, hs] = _nn(ub[:, us], bbr_ref[j])
            hi_ref[:, hs] = _nn(ub[:, us], bbi_ref[j])
        p1, p2, p4, table = _lambda_tables(lr_ref[...], li_ref[...], False)

        def group(i, carry):
            cr, ci = carry
            rows = pl.ds(pl.multiple_of(i * SUB, SUB), SUB)
            xr, xi = _scan8(hr_ref[rows, :], hi_ref[rows, :], (p1, p2, p4), table, cr, ci, False)
            hr_ref[rows, :] = xr
            hi_ref[rows, :] = xi
            return xr[SUB - 1:SUB, :], xi[SUB - 1:SUB, :]

        cr, ci = lax.fori_loop(0, tt // SUB, group, (cr_ref[...], ci_ref[...]))
        cr_ref[...] = cr
        ci_ref[...] = ci
        for j in range(N_CHUNK):
            hs = slice(j * H_CHUNK, (j + 1) * H_CHUNK)
            us = slice(j * U_CHUNK, (j + 1) * U_CHUNK)
            y = (_nn(hr_ref[:, hs].astype(BF16), ccr_ref[j]) - _nn(hi_ref[:, hs].astype(BF16), cci_ref[j])
                 + d_ref[:, us] * uv[:, us])
            yg_ref[:, us] = _gelu_and_grad(y)[0].astype(BF16)

    return pl.pallas_call(
        body, name=name, grid=(t // tt,),
        in_specs=[_row_spec(tt, SSM_WIDTH), _VMEM, _VMEM, _VMEM, _VMEM, _VMEM, _VMEM, _VMEM],
        out_specs=[_row_spec(tt, SSM_WIDTH), _row_spec(tt, STATE_WIDTH), _row_spec(tt, STATE_WIDTH)],
        out_shape=[jax.ShapeDtypeStruct((t, SSM_WIDTH), BF16), jax.ShapeDtypeStruct((t, STATE_WIDTH), F32),
                   jax.ShapeDtypeStruct((t, STATE_WIDTH), F32)],
        scratch_shapes=[pltpu.VMEM((1, STATE_WIDTH), F32), pltpu.VMEM((1, STATE_WIDTH), F32)],
        compiler_params=_params(("arbitrary",)),
    )(u, lam_re, lam_im, bb_re, bb_im, cc_re, cc_im, d_skip)


def ssm_bwd(dyg, u, h_re, h_im, lam_re, lam_im, bb_re, bb_im, cc_re, cc_im, d_skip, name):
    t = u.shape[0]
    tt = BLK
    nt = t // tt

    def body(dyg_ref, u_ref, hr_ref, hi_ref, lr_ref, li_ref, bbr_ref, bbi_ref, ccr_ref, cci_ref, d_ref,
             du_ref, dlr_ref, dli_ref, dbbr_ref, dbbi_ref, dccr_ref, dcci_ref, dd_ref,
             ar_ref, ai_ref, cr_ref, ci_ref):
        step = pl.program_id(0)
        tile = nt - 1 - step

        @pl.when(step == 0)
        def _():
            for ref in (cr_ref, ci_ref, dlr_ref, dli_ref, dbbr_ref, dbbi_ref, dccr_ref, dcci_ref, dd_ref):
                ref[...] = jnp.zeros_like(ref)

        uv = u_ref[...]
        ub = uv.astype(BF16)
        dskip = d_ref[...]
        dy_chunks = []
        for j in range(N_CHUNK):
            hs = slice(j * H_CHUNK, (j + 1) * H_CHUNK)
            us = slice(j * U_CHUNK, (j + 1) * U_CHUNK)
            hrb = hr_ref[:, hs].astype(BF16)
            hib = hi_ref[:, hs].astype(BF16)
            y = _nn(hrb, ccr_ref[j]) - _nn(hib, cci_ref[j]) + dskip[:, us] * uv[:, us]
            dy = dyg_ref[:, us] * _gelu_and_grad(y)[1]
            dy_chunks.append(dy)
            dyb = dy.astype(BF16)
            dccr_ref[j] += _tn(hrb, dyb)
            dcci_ref[j] -= _tn(hib, dyb)
            ar_ref[:, hs] = _nt(dyb, ccr_ref[j])
            ai_ref[:, hs] = -_nt(dyb, cci_ref[j])
        dy_all = jnp.concatenate(dy_chunks, axis=1)
        dd_ref[...] += jnp.sum(dy_all * uv, axis=0, keepdims=True)

        lr, li = lr_ref[...], li_ref[...]
        p1, p2, p4, table = _lambda_tables(lr, -li, True)
        last_row = lax.broadcasted_iota(jnp.int32, (SUB, STATE_WIDTH), 0) == SUB - 1

        def group(k, carry):
            cr, ci, accr, acci = carry
            i = tt // SUB - 1 - k
            rows = pl.ds(pl.multiple_of(i * SUB, SUB), SUB)
            xr, xi = _scan8(ar_ref[rows, :], ai_ref[rows, :], (p1, p2, p4), table, cr, ci, True)
            ar_ref[rows, :] = xr
            ai_ref[rows, :] = xi
            nr = jnp.where(last_row, cr, pltpu.roll(xr, SUB - 1, 0))
            ni = jnp.where(last_row, ci, pltpu.roll(xi, SUB - 1, 0))
            hr, hi = hr_ref[rows, :], hi_ref[rows, :]
            return xr[0:1, :], xi[0:1, :], accr + nr * hr + ni * hi, acci + ni * hr - nr * hi

        zero = jnp.zeros((SUB, STATE_WIDTH), F32)
        cr, ci, accr, acci = lax.fori_loop(0, tt // SUB, group, (cr_ref[...], ci_ref[...], zero, zero))
        cr_ref[...] = cr
        ci_ref[...] = ci
        dlr_ref[...] += accr
        dli_ref[...] += acci

        row = tile * tt + lax.broadcasted_iota(jnp.int32, (tt, U_CHUNK), 0)
        for j in range(N_CHUNK):
            hs = slice(j * H_CHUNK, (j + 1) * H_CHUNK)
            us = slice(j * U_CHUNK, (j + 1) * U_CHUNK)
            arb = ar_ref[:, hs].astype(BF16)
            aib = ai_ref[:, hs].astype(BF16)
            dbbr_ref[j] += _tn(ub[:, us], arb)
            dbbi_ref[j] += _tn(ub[:, us], aib)
            du = _nt(arb, bbr_ref[j]) + _nt(aib, bbi_ref[j]) + dy_chunks[j] * dskip[:, us]
            du_ref[:, us] = jnp.where(row >= PAD_FRONT, du, 0.0).astype(BF16)

    rev = lambda i: (nt - 1 - i, 0)
    full = lambda shape: pl.BlockSpec(shape, lambda i: (0,) * len(shape))
    return pl.pallas_call(
        body, name=name, grid=(nt,),
        in_specs=[pl.BlockSpec((tt, SSM_WIDTH), rev), pl.BlockSpec((tt, SSM_WIDTH), rev),
                  pl.BlockSpec((tt, STATE_WIDTH), rev), pl.BlockSpec((tt, STATE_WIDTH), rev),
                  _VMEM, _VMEM, _VMEM, _VMEM, _VMEM, _VMEM, _VMEM],
        out_specs=[pl.BlockSpec((tt, SSM_WIDTH), rev), full((SUB, STATE_WIDTH)), full((SUB, STATE_WIDTH)),
                   full((N_CHUNK, U_CHUNK, H_CHUNK)), full((N_CHUNK, U_CHUNK, H_CHUNK)),
                   full((N_CHUNK, H_CHUNK, U_CHUNK)), full((N_CHUNK, H_CHUNK, U_CHUNK)), full((1, SSM_WIDTH))],
        out_shape=[jax.ShapeDtypeStruct((t, SSM_WIDTH), BF16),
                   jax.ShapeDtypeStruct((SUB, STATE_WIDTH), F32), jax.ShapeDtypeStruct((SUB, STATE_WIDTH), F32),
                   jax.ShapeDtypeStruct((N_CHUNK, U_CHUNK, H_CHUNK), F32),
                   jax.ShapeDtypeStruct((N_CHUNK, U_CHUNK, H_CHUNK), F32),
                   jax.ShapeDtypeStruct((N_CHUNK, H_CHUNK, U_CHUNK), F32),
                   jax.ShapeDtypeStruct((N_CHUNK, H_CHUNK, U_CHUNK), F32),
                   jax.ShapeDtypeStruct((1, SSM_WIDTH), F32)],
        scratch_shapes=[pltpu.VMEM((tt, STATE_WIDTH), F32), pltpu.VMEM((tt, STATE_WIDTH), F32),
                        pltpu.VMEM((1, STATE_WIDTH), F32), pltpu.VMEM((1, STATE_WIDTH), F32)],
        compiler_params=_params(("arbitrary",)),
    )(dyg, u, h_re, h_im, lam_re, lam_im, bb_re, bb_im, cc_re, cc_im, d_skip)


def merge_fwd(h, o, yg, gates, wap_t, wv_t, wgg_t, wout, name):
    t, d = h.shape
    tm = TOKEN_TILE

    def body(h_ref, o_ref, yg_ref, gt_ref, wap_ref, wv_ref, wgg_ref, wout_ref, ho_ref, mg_ref, a_ref, sv_ref, sg_ref):
        att = _nt(o_ref[...], wap_ref[...])
        ygv = yg_ref[...]
        sv = _nt(ygv, wv_ref[...])
        sg = _nt(ygv, wgg_ref[...])
        a_ref[...] = att
        sv_ref[...] = sv
        sg_ref[...] = sg
        merged = (jax.nn.sigmoid(gt_ref[:, 0:d]) * att
                  + jax.nn.sigmoid(gt_ref[:, d:2 * d]) * (sv * jax.nn.sigmoid(sg))).astype(BF16)
        mg_ref[...] = merged
        ho_ref[...] = h_ref[...] + _nn(merged, wout_ref[...])

    return pl.pallas_call(
        body, name=name, grid=(t // tm,),
        in_specs=[_row_spec(tm, d), _row_spec(tm, ATTN_WIDTH), _row_spec(tm, SSM_WIDTH), _row_spec(tm, 2 * d),
                  _VMEM, _VMEM, _VMEM, _VMEM],
        out_specs=[_row_spec(tm, d), _row_spec(tm, d), _row_spec(tm, d), _row_spec(tm, d), _row_spec(tm, d)],
        out_shape=[jax.ShapeDtypeStruct((t, d), F32), jax.ShapeDtypeStruct((t, d), BF16),
                   jax.ShapeDtypeStruct((t, d), F32), jax.ShapeDtypeStruct((t, d), F32),
                   jax.ShapeDtypeStruct((t, d), F32)],
        compiler_params=_params(("arbitrary",)),
    )(h, o, yg, gates, wap_t, wv_t, wgg_t, wout)


def merge_bwd(dh, gates, att, sv, sg, wap_t, wv_t, wgg_t, wout, dep, name):
    t, d = dh.shape
    tm = TOKEN_TILE

    def body(dh_ref, gt_ref, a_ref, sv_ref, sg_ref, wap_ref, wv_ref, wgg_ref, wout_ref, dep_ref,
             dgt_ref, da_ref, dsv_ref, dsg_ref, do_ref, dyg_ref, dhb_ref):
        dhb = dh_ref[...].astype(BF16)
        dhb_ref[...] = dhb
        dm = _nt(dhb, wout_ref[...])
        sig_a = jax.nn.sigmoid(gt_ref[:, 0:d])
        sig_s = jax.nn.sigmoid(gt_ref[:, d:2 * d])
        sig_g = jax.nn.sigmoid(sg_ref[...])
        svv = sv_ref[...]
        dgt_ref[:, 0:d] = (dm * a_ref[...] * sig_a * (1.0 - sig_a)).astype(BF16)
        dgt_ref[:, d:2 * d] = (dm * (svv * sig_g) * sig_s * (1.0 - sig_s)).astype(BF16)
        da = (dm * sig_a).astype(BF16)
        d_s = dm * sig_s
        dsv = (d_s * sig_g).astype(BF16)
        dsg = (d_s * svv * sig_g * (1.0 - sig_g)).astype(BF16)
        da_ref[...] = da
        dsv_ref[...] = dsv
        dsg_ref[...] = dsg
        do_ref[...] = _nn(da, wap_ref[...]).astype(BF16)
        dyg_ref[...] = _nn(dsv, wv_ref[...]) + _nn(dsg, wgg_ref[...])

    return pl.pallas_call(
        body, name=name, grid=(t // tm,),
        in_specs=[_row_spec(tm, d), _row_spec(tm, 2 * d), _row_spec(tm, d), _row_spec(tm, d), _row_spec(tm, d),
                  _VMEM, _VMEM, _VMEM, _VMEM, _ANY],
        out_specs=[_row_spec(tm, 2 * d), _row_spec(tm, d), _row_spec(tm, d), _row_spec(tm, d),
                   _row_spec(tm, ATTN_WIDTH), _row_spec(tm, SSM_WIDTH), _row_spec(tm, d)],
        out_shape=[jax.ShapeDtypeStruct((t, 2 * d), BF16), jax.ShapeDtypeStruct((t, d), BF16),
                   jax.ShapeDtypeStruct((t, d), BF16), jax.ShapeDtypeStruct((t, d), BF16),
                   jax.ShapeDtypeStruct((t, ATTN_WIDTH), BF16), jax.ShapeDtypeStruct((t, SSM_WIDTH), F32),
                   jax.ShapeDtypeStruct((t, d), BF16)],
        compiler_params=_params(("arbitrary",)),
    )(dh, gates, att, sv, sg, wap_t, wv_t, wgg_t, wout, dep)


def _adamw_math(w, g, m, v):
    mn = ADAM_B1 * m + (1.0 - ADAM_B1) * g
    vn = ADAM_B2 * v + (1.0 - ADAM_B2) * (g * g)
    m_hat = mn / (1.0 - ADAM_B1 ** ADAM_STEP)
    v_hat = vn / (1.0 - ADAM_B2 ** ADAM_STEP)
    return -ADAM_LR * (m_hat / (jnp.sqrt(v_hat) + ADAM_EPS) + ADAM_WD * w), mn, vn


def adamw_layer(w, g, m, v, layer, prev, name):
    _, rows, cols = w.shape
    tr = rows
    for cand in (512, 256):
        if rows > cand and rows % cand == 0:
            tr = cand
            break

    def body(w_ref, g_ref, m_ref, v_ref, *rest):
        go_ref, d_ref, mo_ref, vo_ref = rest[-4:]
        gv = g_ref[...]
        go_ref[0] = gv
        d_ref[0], mo_ref[0], vo_ref[0] = _adamw_math(w_ref[0], gv, m_ref[0], v_ref[0])

    spec3 = pl.BlockSpec((1, tr, cols), lambda r: (layer, r, 0))
    out = jax.ShapeDtypeStruct(w.shape, F32)
    extra = [] if prev is None else list(prev)
    return pl.pallas_call(
        body, name=name, grid=(rows // tr,),
        in_specs=[spec3, _row_spec(tr, cols), spec3, spec3] + [_ANY] * len(extra),
        out_specs=[spec3] * 4, out_shape=[out] * 4,
        input_output_aliases={4 + j: j for j in range(len(extra))},
        compiler_params=_params(("arbitrary",)),
    )(w, g, m, v, *extra)


def adamw(w, g, m, v, name):
    shape = w.shape
    as2d = lambda a: a.reshape(-1, shape[-1]) if a.ndim >= 2 else a.reshape(1, -1)
    w2, g2, m2, v2 = as2d(w), as2d(g), as2d(m), as2d(v)
    rows, cols = w2.shape
    tr = rows
    for cand in (1024, 704, 512, 256):
        if rows > cand and rows % cand == 0:
            tr = cand
            break

    def body(w_ref, g_ref, m_ref, v_ref, d_ref, mo_ref, vo_ref):
        d_ref[...], mo_ref[...], vo_ref[...] = _adamw_math(w_ref[...], g_ref[...], m_ref[...], v_ref[...])

    spec = _row_spec(tr, cols)
    out = jax.ShapeDtypeStruct((rows, cols), F32)
    d, mn, vn = pl.pallas_call(
        body, name=name, grid=(rows // tr,), in_specs=[spec] * 4, out_specs=[spec] * 3, out_shape=[out] * 3,
        compiler_params=_params(("arbitrary",)),
    )(w2, g2, m2, v2)
    return d.reshape(shape), mn.reshape(shape), vn.reshape(shape)


def _my_index():
    return 4 * lax.axis_index("x") + 2 * lax.axis_index("y") + lax.axis_index("c")


def _peer(p):
    return (lax.axis_index("x") ^ ((p >> 2) & 1), lax.axis_index("y") ^ ((p >> 1) & 1), lax.axis_index("c") ^ (p & 1))


_HBM = pl.BlockSpec(memory_space=pltpu.HBM)
_SEM = pl.BlockSpec(memory_space=pltpu.SEMAPHORE)
_EFFECT = pltpu.SideEffectType.DATAFLOW_SIDE_EFFECTING


class Exchange:
    def __init__(self, srcs, scatter, name):
        self.n = n = len(srcs)
        self.scatter = scatter
        self.name = name
        widths = sorted({s.shape[1] for s in srcs}, reverse=True)
        self.ncls = len(widths)
        self.cls = [widths.index(s.shape[1]) for s in srcs]
        self.cnts = [s.shape[0] // N_DEV if scatter else s.shape[0] for s in srcs]
        self.totals = [sum(c for c, k in zip(self.cnts, self.cls) if k == w) for w in range(self.ncls)]
        self.sizer = [max((k for k in range(n) if self.cls[k] == w), key=lambda k: self.cnts[k])
                      for w in range(self.ncls)]
        assert all(N_DEV * self.cnts[self.sizer[w]] >= self.totals[w] for w in range(self.ncls))
        if scatter:
            self.land_shapes = [(N_DEV, c, s.shape[1]) for s, c in zip(srcs, self.cnts)]
        else:
            self.land_shapes = [(N_DEV * c, s.shape[1]) for s, c in zip(srcs, self.cnts)]
        self.dtypes = [s.dtype for s in srcs]

    def _block(self, k, who):
        return pl.ds(pl.multiple_of(who * self.cnts[k], 16), self.cnts[k])

    def _sem(self, p, w):
        return (p - 1) * self.ncls + w

    def start(self, srcs, after):
        n = self.n

        def body(*refs):
            src, land = refs[:n], refs[n:2 * n]
            send_sems, recv_sems = refs[2 * n + 1], refs[2 * n + 2]
            token = refs[-1]
            me = _my_index()
            for p in range(1, N_DEV):
                for k in range(n):
                    if self.scatter:
                        s_ref, d_ref = src[k].at[self._block(k, me ^ p), :], land[k].at[me]
                    else:
                        s_ref, d_ref = src[k], land[k].at[self._block(k, me), :]
                    pltpu.make_async_remote_copy(
                        src_ref=s_ref, dst_ref=d_ref, send_sem=send_sems.at[self._sem(p, self.cls[k])],
                        recv_sem=recv_sems.at[self._sem(p, self.cls[k])], device_id=_peer(p),
                        device_id_type=MESH).start()
            token[...] = jnp.zeros_like(token)

        sems = pltpu.SemaphoreType.DMA(((N_DEV - 1) * self.ncls,))
        thru = [pltpu.HBM(s.shape, s.dtype) for s in srcs] + [pltpu.HBM(shp, dt) for shp, dt in
                                                               zip(self.land_shapes, self.dtypes)]
        lands = [pltpu.with_memory_space_constraint(lax.empty(shp, dt), pltpu.HBM)
                 for shp, dt in zip(self.land_shapes, self.dtypes)]
        out = pl.pallas_call(
            body, name=self.name + "_start",
            in_specs=[_HBM] * (2 * n) + [_ANY],
            out_shape=[sems, sems] + thru + [jax.ShapeDtypeStruct((8, 128), F32)],
            out_specs=[_SEM, _SEM] + [_HBM] * (2 * n) + [_VMEM],
            input_output_aliases={j: 2 + j for j in range(2 * n)},
            compiler_params=pltpu.CompilerParams(has_side_effects=_EFFECT),
        )(*[pltpu.with_memory_space_constraint(s, pltpu.HBM) for s in srcs], *lands, after)
        return out[:-1], out[-1]

    def wait(self, state, after):
        n = self.n
        send_sems, recv_sems = state[0], state[1]
        thru = state[2:]

        def body(*refs):
            src, land = refs[:n], refs[n:2 * n]
            send_sems, recv_sems = refs[2 * n], refs[2 * n + 1]
            for p in range(1, N_DEV):
                for w in range(self.ncls):
                    big = src[self.sizer[w]] if self.scatter else land[self.sizer[w]]
                    span = big.at[pl.ds(0, self.totals[w]), :]
                    copy = pltpu.make_async_remote_copy(
                        src_ref=span, dst_ref=span, send_sem=send_sems.at[self._sem(p, w)],
                        recv_sem=recv_sems.at[self._sem(p, w)],
                        device_id=_peer(p), device_id_type=MESH)
                    copy.wait_send()
                    copy.wait_recv()

        out = pl.pallas_call(
            body, name=self.name + "_wait",
            in_specs=[_HBM] * (2 * n) + [_SEM, _SEM, _ANY],
            out_shape=[pltpu.HBM(a.shape, a.dtype) for a in thru], out_specs=[_HBM] * (2 * n),
            input_output_aliases={j: j for j in range(2 * n)},
            compiler_params=pltpu.CompilerParams(has_side_effects=_EFFECT),
        )(*thru, send_sems, recv_sems, after)
        return out[:n], out[n:]

    def place(self, lands, srcs):
        n = self.n
        assert not self.scatter

        def body(*refs):
            src, land = refs[n:2 * n], refs[2 * n:3 * n]
            bufs, sems = refs[3 * n:4 * n], refs[-1]
            me = _my_index()
            loads = [pltpu.make_async_copy(src[k], bufs[k], sems.at[k]) for k in range(n)]
            stores = [pltpu.make_async_copy(bufs[k], land[k].at[self._block(k, me), :], sems.at[k]) for k in range(n)]
            for cp in loads:
                cp.start()
            for k in range(n):
                loads[k].wait()
                stores[k].start()
            for cp in stores:
                cp.wait()

        return pl.pallas_call(
            body, name=self.name + "_place", in_specs=[_ANY] * (2 * n), out_specs=[_ANY] * n,
            out_shape=[jax.ShapeDtypeStruct(a.shape, a.dtype) for a in lands],
            input_output_aliases={j: j for j in range(n)},
            scratch_shapes=[pltpu.VMEM(s.shape, s.dtype) for s in srcs] + [pltpu.SemaphoreType.DMA((n,))],
        )(*lands, *srcs)


def sum_blocks(landed, full, name):
    _, cnt, cols = landed.shape

    def body(land_ref, full_ref, o_ref, own_ref, sem):
        me = _my_index()
        own = pltpu.make_async_copy(full_ref.at[pl.ds(pl.multiple_of(me * cnt, 16), cnt), :], own_ref, sem)
        own.start()
        acc = land_ref[me ^ 1].astype(F32)
        for p in range(2, N_DEV):
            acc = acc + land_ref[me ^ p].astype(F32)
        own.wait()
        o_ref[...] = acc + own_ref[...].astype(F32)

    return pl.pallas_call(
        body, name=name, in_specs=[_VMEM, _ANY], out_specs=_VMEM,
        out_shape=jax.ShapeDtypeStruct((cnt, cols), F32),
        scratch_shapes=[pltpu.VMEM((cnt, cols), landed.dtype), pltpu.SemaphoreType.DMA],
        compiler_params=_params(),
    )(landed, full)


def sum_slots(slots, name):
    _, rows, cols = slots.shape
    tr = rows
    if rows > 512:
        for cand in (256, 128, 64, 32, 16, 8):
            if rows % cand == 0:
                tr = cand
                break

    def body(s_ref, o_ref):
        acc = s_ref[0].astype(F32)
        for j in range(1, N_DEV):
            acc = acc + s_ref[j].astype(F32)
        o_ref[...] = acc

    return pl.pallas_call(
        body, name=name, grid=(rows // tr,),
        in_specs=[pl.BlockSpec((N_DEV, tr, cols), lambda i: (0, i, 0))], out_specs=_row_spec(tr, cols),
        out_shape=jax.ShapeDtypeStruct((rows, cols), F32), compiler_params=_params(("arbitrary",)),
    )(slots)


BIG_T = ("ffn1_w_gate", "ffn1_w_up", "w_in", "ffn2_w_gate", "ffn2_w_up")
BIG_N = ("ffn1_w_down", "w_out", "ffn2_w_down")
HALF_T = ("w_attn_proj", "w_glu_v", "w_glu_g")
SMALL = ("ffn1_norm", "mix_norm", "attn_sinks", "ssm_a_re", "ssm_a_im", "ssm_log_dt", "ssm_b_re", "ssm_b_im",
         "ssm_c_re", "ssm_c_im", "ssm_d", "ffn2_norm", "final_norm")
PARTS = {"ffn1": ("ffn1_w_gate", "ffn1_w_up", "ffn1_w_down"),
         "mix": ("w_in", "w_out", "w_attn_proj", "w_glu_v", "w_glu_g"),
         "ffn2": ("ffn2_w_gate", "ffn2_w_up", "ffn2_w_down")}


def _to_rows(name, a):
    return a if name in BIG_N else jnp.swapaxes(a, -1, -2)


def local_step(x, tgt, get_weights, put_grads, small):
    seq, d = x.shape
    t = PAD_FRONT + N_META + seq
    cos_t, sin_t = rope_tables(t)
    row = lambda a: a.reshape(1, -1)
    saved = []
    h = None
    for i in range(DEPTH):
        s = {}
        w = dict(get_weights(i, "ffn1", h))
        if i == 0:
            h = jnp.concatenate([jnp.zeros((PAD_FRONT, d), F32), w["meta_tokens"], x], axis=0)
        s["h0"] = h
        h, s["n1"], s["a1"], s["b1"] = ffn_fwd(h, row(small["ffn1_norm"][i]), w["ffn1_w_gate"], w["ffn1_w_up"],
                                               w["ffn1_w_down"], f"ffn1_fwd_{i}")
        s["h1"] = h
        w.update(get_weights(i, "mix", h))
        s["n2"], s["qkv"], s["u"], s["gates"] = win_fwd(h, row(small["mix_norm"][i]), w["w_in"], cos_t, sin_t,
                                                        f"win_fwd_{i}")
        b_re_t = jnp.swapaxes(small["ssm_b_re"][i], 1, 2)
        b_im_t = jnp.swapaxes(small["ssm_b_im"][i], 1, 2)
        s["b_t"] = (b_re_t, b_im_t)
        lam_re, lam_im, bbar_re, bbar_im = ssm_prep(small["ssm_a_re"][i], small["ssm_a_im"][i],
                                                    small["ssm_log_dt"][i].reshape(-1, 1), b_re_t, b_im_t, f"ssm_prep_{i}")
        s["ssm"] = (row(lam_re), row(lam_im), _block_diag_b(bbar_re).astype(BF16), _block_diag_b(bbar_im).astype(BF16),
                    _block_diag_c(small["ssm_c_re"][i]).astype(BF16), _block_diag_c(small["ssm_c_im"][i]).astype(BF16),
                    row(small["ssm_d"][i]))
        s["yg"], s["h_re"], s["h_im"] = ssm_fwd(s["u"], *s["ssm"], f"ssm_fwd_{i}")
        s["o"] = attn_fwd(s["qkv"], row(small["attn_sinks"][i]), f"attn_fwd_{i}")
        h, s["merged"], s["att"], s["sv"], s["sg"] = merge_fwd(
            h, s["o"], s["yg"], s["gates"], w["w_attn_proj"], w["w_glu_v"], w["w_glu_g"], w["w_out"],
            f"merge_fwd_{i}")
        s["h2"] = h
        w.update(get_weights(i, "ffn2", h))
        h, s["n3"], s["a3"], s["b3"] = ffn_fwd(h, row(small["ffn2_norm"][i]), w["ffn2_w_gate"], w["ffn2_w_up"],
                                               w["ffn2_w_down"], f"ffn2_fwd_{i}")
        s["w"] = w
        saved.append(s)

    loss, dh, d_final = head_fwd_bwd(h, row(small["final_norm"]), tgt)
    gs = {k: [None] * DEPTH for k in SMALL if k != "final_norm"}
    dep = loss
    for i in reversed(range(DEPTH)):
        s = saved[i]
        w = s["w"]
        dh, da, db, sact, dhb, dg = ffn_bwd(dh, s["h2"], row(small["ffn2_norm"][i]), s["a3"], s["b3"], w["ffn2_w_gate"],
                                            w["ffn2_w_up"], w["ffn2_w_down"], dep, f"ffn2_bwd_{i}")
        gs["ffn2_norm"][i] = dg[0]
        dep = put_grads(i, "ffn2", {"ffn2_w_gate": tn_matmul(da, s["n3"], f"ffn2_dwg_{i}"),
                                    "ffn2_w_up": tn_matmul(db, s["n3"], f"ffn2_dwu_{i}"),
                                    "ffn2_w_down": tn_matmul(sact, dhb, f"ffn2_dwd_{i}")})

        dgates, datt, dsv, dsg, do, dyg, dhb = merge_bwd(dh, s["gates"], s["att"], s["sv"], s["sg"], w["w_attn_proj"],
                                                         w["w_glu_v"], w["w_glu_g"], w["w_out"], dep, f"merge_bwd_{i}")
        gmix = {"w_out": tn_matmul(s["merged"], dhb, f"dwout_{i}"),
                "w_attn_proj": tn_matmul(datt, s["o"], f"dwap_{i}"),
                "w_glu_v": tn_matmul(dsv, s["yg"], f"dwv_{i}"),
                "w_glu_g": tn_matmul(dsg, s["yg"], f"dwgg_{i}")}
        dqkv, dsink = attn_bwd(s["qkv"], do, row(small["attn_sinks"][i]), cos_t, sin_t, f"attn_bwd_{i}")
        gs["attn_sinks"][i] = dsink[:, 0]
        du, dl_re, dl_im, dbb_re, dbb_im, dcc_re, dcc_im, dd = ssm_bwd(dyg, s["u"], s["h_re"], s["h_im"], *s["ssm"],
                                                                      f"ssm_bwd_{i}")
        fold = lambda a: jnp.sum(a, axis=0).reshape(SSM_GROUPS, SSM_STATE)
        da_re, da_im, dldt, db_re_t, db_im_t = ssm_prep_bwd(
            small["ssm_a_re"][i], small["ssm_a_im"][i], small["ssm_log_dt"][i].reshape(-1, 1), *s["b_t"],
            fold(dl_re), fold(dl_im), _diag_of_b(dbb_re), _diag_of_b(dbb_im), f"ssm_prep_bwd_{i}")
        gs["ssm_a_re"][i], gs["ssm_a_im"][i], gs["ssm_log_dt"][i] = da_re, da_im, dldt[:, 0]
        gs["ssm_b_re"][i], gs["ssm_b_im"][i] = jnp.swapaxes(db_re_t, 1, 2), jnp.swapaxes(db_im_t, 1, 2)
        gs["ssm_c_re"][i], gs["ssm_c_im"][i] = _diag_of_c(dcc_re), _diag_of_c(dcc_im)
        gs["ssm_d"][i] = dd[0]
        gmix["w_in"] = tn_matmul([dqkv, du, dgates], s["n2"], f"dwin_{i}")
        dep = put_grads(i, "mix", gmix)
        dh, dg = win_bwd(dh, s["h1"], row(small["mix_norm"][i]), dqkv, du, dgates, w["w_in"], dep, f"win_bwd_{i}")
        gs["mix_norm"][i] = dg[0]

        dh, da, db, sact, dhb, dg = ffn_bwd(dh, s["h0"], row(small["ffn1_norm"][i]), s["a1"], s["b1"], w["ffn1_w_gate"],
                                            w["ffn1_w_up"], w["ffn1_w_down"], dep, f"ffn1_bwd_{i}")
        gs["ffn1_norm"][i] = dg[0]
        if i > 0:
            dep = put_grads(i, "ffn1", {"ffn1_w_gate": tn_matmul(da, s["n1"], f"ffn1_dwg_{i}"),
                                        "ffn1_w_up": tn_matmul(db, s["n1"], f"ffn1_dwu_{i}"),
                                        "ffn1_w_down": tn_matmul(sact, dhb, f"ffn1_dwd_{i}")})
        else:
            for k, xa, ya in (("ffn1_w_down", sact, dhb), ("ffn1_w_gate", da, s["n1"]), ("ffn1_w_up", db, s["n1"])):
                dep = put_grads(i, "ffn1", {k: tn_matmul(xa, ya, f"d_{k}_{i}", dep)})

    gs = {k: jnp.stack(v) for k, v in gs.items()}
    gs["final_norm"] = d_final[0]
    return loss[0, 0], dh[PAD_FRONT + N_META:], dh[PAD_FRONT:PAD_FRONT + N_META], gs, dep


def _pack_rows(arrays, cols):
    flat = jnp.concatenate([a.reshape(-1) for a in arrays])
    rows = -(-flat.shape[0] // cols)
    rows = -(-rows // 16) * 16
    return jnp.pad(flat, (0, rows * cols - flat.shape[0])).reshape(rows, cols)


def _unpack_rows(packed, shapes):
    flat = packed.reshape(-1)
    out, off = [], 0
    for shp in shapes:
        n = math.prod(shp)
        out.append(flat[off:off + n].reshape(shp))
        off += n
    return out


def kernel(x, meta_tokens, ffn1_norm, ffn1_w_gate, ffn1_w_up, ffn1_w_down, mix_norm, w_in, attn_sinks, ssm_a_re, ssm_a_im, ssm_log_dt, ssm_b_re, ssm_b_im, ssm_c_re, ssm_c_im, ssm_d, w_attn_proj, w_glu_v, w_glu_g, w_out, ffn2_norm, ffn2_w_gate, ffn2_w_up, ffn2_w_down, final_norm, loss_target, m_meta_tokens, m_ffn1_norm, m_ffn1_w_gate, m_ffn1_w_up, m_ffn1_w_down, m_mix_norm, m_w_in, m_attn_sinks, m_ssm_a_re, m_ssm_a_im, m_ssm_log_dt, m_ssm_b_re, m_ssm_b_im, m_ssm_c_re, m_ssm_c_im, m_ssm_d, m_w_attn_proj, m_w_glu_v, m_w_glu_g, m_w_out, m_ffn2_norm, m_ffn2_w_gate, m_ffn2_w_up, m_ffn2_w_down, m_final_norm, v_meta_tokens, v_ffn1_norm, v_ffn1_w_gate, v_ffn1_w_up, v_ffn1_w_down, v_mix_norm, v_w_in, v_attn_sinks, v_ssm_a_re, v_ssm_a_im, v_ssm_log_dt, v_ssm_b_re, v_ssm_b_im, v_ssm_c_re, v_ssm_c_im, v_ssm_d, v_w_attn_proj, v_w_glu_v, v_w_glu_g, v_w_out, v_ffn2_norm, v_ffn2_w_gate, v_ffn2_w_up, v_ffn2_w_down, v_final_norm):
    names = ("meta_tokens", "ffn1_norm", "ffn1_w_gate", "ffn1_w_up", "ffn1_w_down", "mix_norm", "w_in", "attn_sinks",
             "ssm_a_re", "ssm_a_im", "ssm_log_dt", "ssm_b_re", "ssm_b_im", "ssm_c_re", "ssm_c_im", "ssm_d",
             "w_attn_proj", "w_glu_v", "w_glu_g", "w_out", "ffn2_norm", "ffn2_w_gate", "ffn2_w_up", "ffn2_w_down",
             "final_norm")
    weights = dict(zip(names, (meta_tokens, ffn1_norm, ffn1_w_gate, ffn1_w_up, ffn1_w_down, mix_norm, w_in, attn_sinks, ssm_a_re, ssm_a_im, ssm_log_dt, ssm_b_re, ssm_b_im, ssm_c_re, ssm_c_im, ssm_d, w_attn_proj, w_glu_v, w_glu_g, w_out, ffn2_norm, ffn2_w_gate, ffn2_w_up, ffn2_w_down, final_norm)))
    moments_m = dict(zip(names, (m_meta_tokens, m_ffn1_norm, m_ffn1_w_gate, m_ffn1_w_up, m_ffn1_w_down, m_mix_norm, m_w_in, m_attn_sinks, m_ssm_a_re, m_ssm_a_im, m_ssm_log_dt, m_ssm_b_re, m_ssm_b_im, m_ssm_c_re, m_ssm_c_im, m_ssm_d, m_w_attn_proj, m_w_glu_v, m_w_glu_g, m_w_out, m_ffn2_norm, m_ffn2_w_gate, m_ffn2_w_up, m_ffn2_w_down, m_final_norm)))
    moments_v = dict(zip(names, (v_meta_tokens, v_ffn1_norm, v_ffn1_w_gate, v_ffn1_w_up, v_ffn1_w_down, v_mix_norm, v_w_in, v_attn_sinks, v_ssm_a_re, v_ssm_a_im, v_ssm_log_dt, v_ssm_b_re, v_ssm_b_im, v_ssm_c_re, v_ssm_c_im, v_ssm_d, v_w_attn_proj, v_w_glu_v, v_w_glu_g, v_w_out, v_ffn2_norm, v_ffn2_w_gate, v_ffn2_w_up, v_ffn2_w_down, v_final_norm)))
    me = _my_index()

    gathers = {}
    token = jnp.zeros((8, 128), F32)
    for i in range(DEPTH):
        for part, ks in PARTS.items():
            shards = [_to_rows(k, weights[k][i]).astype(BF16) for k in ks]
            if (i, part) == (0, "ffn1"):
                shards.append(meta_tokens)
            ex = Exchange(shards, False, f"gather_{part}_{i}")
            state, token = ex.start(shards, token)
            gathers[i, part] = (ex, state, shards)
    all_started = token

    def get_weights(i, part, after):
        ex, state, shards = gathers[i, part]
        shards, lands = ex.wait(state, all_started if after is None else after)
        fulls = ex.place(lands, shards)
        got = dict(zip(PARTS[part], fulls))
        if (i, part) == (0, "ffn1"):
            got["meta_tokens"] = jnp.swapaxes(fulls[-1].reshape(N_DEV, N_META, 128), 0, 1).reshape(N_META, D_MODEL)
        return got

    scatters = []

    def put_grads(i, part, gdict):
        ks = list(gdict)
        srcs = [gdict[k] for k in ks]
        ex = Exchange(srcs, True, f"scatter_{part if len(ks) > 1 else ks[0]}_{i}")
        state, tok = ex.start(srcs, all_started)
        scatters.append((i, ks, ex, state))
        return tok

    small = {k: weights[k] for k in SMALL}
    loss, dx, dmeta, gs, last_started = local_step(x[0], loss_target[0], get_weights, put_grads, small)

    grads, deltas, new_m, new_v = {}, {}, {}, {}
    small_list = [loss.reshape(1), dmeta] + [gs[k] for k in SMALL]
    packed = _pack_rows(small_list, D_MODEL)
    small_ex = Exchange([packed], False, "gather_small")
    small_state, after = small_ex.start([packed], last_started)

    updated = {}
    for i, ks, ex, state in scatters:
        partials, lands = ex.wait(state, after)
        for k, partial, slots in zip(ks, partials, lands):
            g = _to_rows(k, sum_blocks(slots, partial, f"sum_{k}_{i}"))
            updated[k] = adamw_layer(weights[k], g, moments_m[k], moments_v[k], i, updated.get(k), f"adamw_{k}_{i}")
            after = updated[k][0]
    for k, (g, d, mn, vn) in updated.items():
        grads[k], deltas[k], new_m[k], new_v[k] = g, d, mn, vn

    packed_own, packed_all = small_ex.wait(small_state, after)
    (packed_all,) = small_ex.place(packed_all, packed_own)
    total = sum_slots(packed_all.reshape(N_DEV, packed.shape[0], D_MODEL), "sum_small")
    pieces = _unpack_rows(total, [a.shape for a in small_list])
    loss_out = pieces[0][0]
    grads["meta_tokens"] = lax.dynamic_slice_in_dim(pieces[1], me * 128, 128, axis=1)
    for k, p in zip(SMALL, pieces[2:]):
        grads[k] = p
    for k in ("meta_tokens",) + SMALL:
        deltas[k], new_m[k], new_v[k] = adamw(weights[k], grads[k], moments_m[k], moments_v[k], f"adamw_{k}")
    return (loss_out, dx[None], *[grads[k] for k in names], *[deltas[k] for k in names],
            *[new_m[k] for k in names], *[new_v[k] for k in names])
```

```python
import functools
import math

import jax
import jax.numpy as jnp
from jax import lax
from jax.experimental import pallas as pl
from jax.experimental.pallas import tpu as pltpu

F32 = jnp.float32
BF16 = jnp.bfloat16

D_MODEL = 1024
DEPTH = 2
N_META = 16
HEAD_DIM = 64
N_Q_HEADS = 8
ATTN_WIDTH = 512
KV_WIDTH = 128
QKV_WIDTH = ATTN_WIDTH + 2 * KV_WIDTH
WINDOW = 128
BLK = 128
ROPE_THETA = 500000.0
ROT_DIM = 16
SSM_WIDTH = 512
SSM_GROUP = 16
SSM_GROUPS = 32
SSM_STATE = 64
STATE_WIDTH = SSM_GROUPS * SSM_STATE
D_FF = 2816
IN_WIDTH = 3328
EPS = 1e-6
NEG_INF = -1e30
PAD_FRONT = (-N_META) % BLK
N_DEV = 8

ADAM_LR = 0.001
ADAM_B1 = 0.9
ADAM_B2 = 0.999
ADAM_EPS = 1e-08
ADAM_WD = 0.01
ADAM_STEP = 10

VMEM_LIMIT = 56 * 1024 * 1024
TOKEN_TILE = 384
_VMEM = pl.BlockSpec(memory_space=pltpu.VMEM)
_SMEM = pl.BlockSpec(memory_space=pltpu.SMEM)
_ANY = pl.BlockSpec(memory_space=pl.ANY)
MESH = pl.DeviceIdType.MESH


def _params(sem=None):
    return pltpu.CompilerParams(dimension_semantics=sem, vmem_limit_bytes=VMEM_LIMIT)


def _nt(a, b):
    return lax.dot_general(a, b, (((1,), (1,)), ((), ())), preferred_element_type=F32)


def _nn(a, b):
    return jnp.dot(a, b, preferred_element_type=F32)


def _tn(a, b):
    return lax.dot_general(a, b, (((0,), (0,)), ((), ())), preferred_element_type=F32)


def _row_spec(tm, width):
    return pl.BlockSpec((tm, width), lambda i: (i, 0))


def _acc_spec(shape):
    return pl.BlockSpec(shape, lambda i: (0,) * len(shape))


def _rms_stats(x):
    r = lax.rsqrt(jnp.mean(x * x, axis=-1, keepdims=True) + EPS)
    return x * r, r


def _rms_bwd(dn, xh, r, g):
    dg = jnp.sum(dn * xh, axis=0, keepdims=True)
    dxh = dn * g
    dx = r * (dxh - xh * jnp.mean(dxh * xh, axis=-1, keepdims=True))
    return dx, dg


def ffn_fwd(h, g, wg_t, wu_t, wd, name):
    t, d = h.shape
    f = wd.shape[0]
    tm = TOKEN_TILE

    def body(h_ref, g_ref, wg_ref, wu_ref, wd_ref, ho_ref, n_ref, a_ref, b_ref):
        x = h_ref[...]
        xh, _ = _rms_stats(x)
        n = (xh * g_ref[...]).astype(BF16)
        n_ref[...] = n
        a = _nt(n, wg_ref[...])
        b = _nt(n, wu_ref[...])
        a_ref[...] = a.astype(BF16)
        b_ref[...] = b.astype(BF16)
        s = (a * jax.nn.sigmoid(a) * b).astype(BF16)
        ho_ref[...] = x + 0.5 * _nn(s, wd_ref[...])

    return pl.pallas_call(
        body, name=name, grid=(t // tm,),
        in_specs=[_row_spec(tm, d), _acc_spec((1, d)), _VMEM, _VMEM, _VMEM],
        out_specs=[_row_spec(tm, d), _row_spec(tm, d), _row_spec(tm, f), _row_spec(tm, f)],
        out_shape=[jax.ShapeDtypeStruct((t, d), F32), jax.ShapeDtypeStruct((t, d), BF16),
                   jax.ShapeDtypeStruct((t, f), BF16), jax.ShapeDtypeStruct((t, f), BF16)],
        compiler_params=_params(("arbitrary",)),
    )(h, g, wg_t, wu_t, wd)


def ffn_bwd(dh, h, g, a, b, wg_t, wu_t, wd, dep, name):
    t, d = h.shape
    f = wd.shape[0]
    tm = TOKEN_TILE // 2

    def body(dh_ref, h_ref, g_ref, a_ref, b_ref, wg_ref, wu_ref, wd_ref, dep_ref,
             dhi_ref, da_ref, db_ref, s_ref, dhb_ref, dg_ref):
        dh_t = dh_ref[...]
        dhb = (0.5 * dh_t).astype(BF16)
        dhb_ref[...] = dhb
        ds = _nt(dhb, wd_ref[...])
        av = a_ref[...].astype(F32)
        bv = b_ref[...].astype(F32)
        sig = jax.nn.sigmoid(av)
        sl = av * sig
        s_ref[...] = (sl * bv).astype(BF16)
        da = (ds * bv * (sig * (1.0 + av * (1.0 - sig)))).astype(BF16)
        db = (ds * sl).astype(BF16)
        da_ref[...] = da
        db_ref[...] = db
        dn = _nn(da, wg_ref[...]) + _nn(db, wu_ref[...])
        xh, r = _rms_stats(h_ref[...])
        dx, dg = _rms_bwd(dn, xh, r, g_ref[...])
        dhi_ref[...] = dh_t + dx

        @pl.when(pl.program_id(0) == 0)
        def _():
            dg_ref[...] = jnp.zeros_like(dg_ref)

        dg_ref[...] += dg

    return pl.pallas_call(
        body, name=name, grid=(t // tm,),
        in_specs=[_row_spec(tm, d), _row_spec(tm, d), _acc_spec((1, d)), _row_spec(tm, f), _row_spec(tm, f),
                  _VMEM, _VMEM, _VMEM, _ANY],
        out_specs=[_row_spec(tm, d), _row_spec(tm, f), _row_spec(tm, f), _row_spec(tm, f), _row_spec(tm, d),
                   _acc_spec((1, d))],
        out_shape=[jax.ShapeDtypeStruct((t, d), F32), jax.ShapeDtypeStruct((t, f), BF16),
                   jax.ShapeDtypeStruct((t, f), BF16), jax.ShapeDtypeStruct((t, f), BF16),
                   jax.ShapeDtypeStruct((t, d), BF16), jax.ShapeDtypeStruct((1, d), F32)],
        compiler_params=_params(("arbitrary",)),
    )(dh, h, g, a, b, wg_t, wu_t, wd, dep)


DW_TILE = 256


def tn_matmul(x, y, name, dep=None):
    xs = list(x) if isinstance(x, (list, tuple)) else [x]
    t = xs[0].shape[0]
    n = y.shape[1]
    bm = DW_TILE
    tiles = [a.shape[1] // bm for a in xs]
    offs = [sum(tiles[:k]) for k in range(len(xs))]
    deps = [] if dep is None else [dep]

    def body(*refs):
        y_ref, o_ref = refs[len(xs)], refs[-1]
        i = pl.program_id(0)
        for k in range(len(xs)):
            @pl.when((i >= offs[k]) & (i < offs[k] + tiles[k]))
            def _(k=k):
                o_ref[...] = _tn(refs[k][...], y_ref[...]).astype(BF16)

    def x_spec(k):
        return pl.BlockSpec((t, bm), lambda i: (0, jnp.clip(i - offs[k], 0, tiles[k] - 1)))

    return pl.pallas_call(
        body, name=name, grid=(sum(tiles),),
        in_specs=[x_spec(k) for k in range(len(xs))] + [_VMEM] + [_ANY] * len(deps),
        out_specs=pl.BlockSpec((bm, n), lambda i: (i, 0)),
        out_shape=jax.ShapeDtypeStruct((sum(tiles) * bm, n), BF16),
        compiler_params=_params(("arbitrary",)),
    )(*xs, y, *deps)


def head_fwd_bwd(h, g, tgt):
    t, d = h.shape

    def body(h_ref, g_ref, t_ref, loss_ref, dh_ref, dg_ref):
        i = pl.program_id(0)
        xh, r = _rms_stats(h_ref[...])
        gv = g_ref[...]
        valid = (i > 0).astype(F32)
        e = (xh * gv - t_ref[...]) * valid
        dx, dg = _rms_bwd(e * (1.0 / d), xh, r, gv)
        dh_ref[...] = dx

        @pl.when(i == 0)
        def _():
            dg_ref[...] = jnp.zeros_like(dg_ref)
            loss_ref[...] = jnp.zeros_like(loss_ref)

        dg_ref[...] += dg
        loss_ref[...] += jnp.sum(e * e) * (0.5 / d)

    return pl.pallas_call(
        body, name="head", grid=(t // BLK,),
        in_specs=[_row_spec(BLK, d), _acc_spec((1, d)),
                  pl.BlockSpec((BLK, d), lambda i: (jnp.maximum(i - 1, 0), 0))],
        out_specs=[_acc_spec((1, 128)), _row_spec(BLK, d), _acc_spec((1, d))],
        out_shape=[jax.ShapeDtypeStruct((1, 128), F32), jax.ShapeDtypeStruct((t, d), F32),
                   jax.ShapeDtypeStruct((1, d), F32)],
        compiler_params=_params(("arbitrary",)),
    )(h, g, tgt)


def rope_tables(t):
    pos = jnp.arange(t, dtype=F32) - PAD_FRONT
    inv_freq = ROPE_THETA ** (-jnp.arange(0, ROT_DIM, 2, dtype=F32) / ROT_DIM)
    ang = pos[:, None] * inv_freq[None, :]
    cos, sin = jnp.cos(ang), jnp.sin(ang)
    ones = jnp.ones((t, HEAD_DIM - ROT_DIM), F32)
    cos_h = jnp.concatenate([cos, cos, ones], axis=1)
    sin_h = jnp.concatenate([-sin, sin, 0.0 * ones], axis=1)
    return jnp.concatenate([cos_h, cos_h], axis=1), jnp.concatenate([sin_h, sin_h], axis=1)


def _swap_halves(x):
    n = x.shape[1]
    lane = lax.broadcasted_iota(jnp.int32, x.shape, 1)
    return jnp.where(lane % HEAD_DIM < ROT_DIM // 2, pltpu.roll(x, n - ROT_DIM // 2, 1), pltpu.roll(x, ROT_DIM // 2, 1))


def _rope(x, cos_t, sin_t, sign):
    return x * cos_t + sign * (_swap_halves(x) * sin_t)


def win_fwd(h, g, win_t, cos_t, sin_t, name):
    t, d = h.shape
    tm = TOKEN_TILE

    def body(h_ref, g_ref, w_ref, c_ref, s_ref, n_ref, qkv_ref, u_ref, gates_ref):
        xh, _ = _rms_stats(h_ref[...])
        n = (xh * g_ref[...]).astype(BF16)
        n_ref[...] = n
        z = _nt(n, w_ref[...])
        c, s = c_ref[...], s_ref[...]
        for j in range((ATTN_WIDTH + KV_WIDTH) // 128):
            qkv_ref[:, j * 128:(j + 1) * 128] = _rope(z[:, j * 128:(j + 1) * 128], c, s, 1.0).astype(BF16)
        qkv_ref[:, ATTN_WIDTH + KV_WIDTH:QKV_WIDTH] = z[:, ATTN_WIDTH + KV_WIDTH:QKV_WIDTH].astype(BF16)
        u_ref[...] = z[:, QKV_WIDTH:QKV_WIDTH + SSM_WIDTH]
        gates_ref[...] = z[:, QKV_WIDTH + SSM_WIDTH:]

    return pl.pallas_call(
        body, name=name, grid=(t // tm,),
        in_specs=[_row_spec(tm, d), _acc_spec((1, d)), _VMEM, _row_spec(tm, 128), _row_spec(tm, 128)],
        out_specs=[_row_spec(tm, d), _row_spec(tm, QKV_WIDTH), _row_spec(tm, SSM_WIDTH), _row_spec(tm, 2 * d)],
        out_shape=[jax.ShapeDtypeStruct((t, d), BF16), jax.ShapeDtypeStruct((t, QKV_WIDTH), BF16),
                   jax.ShapeDtypeStruct((t, SSM_WIDTH), F32), jax.ShapeDtypeStruct((t, 2 * d), F32)],
        compiler_params=_params(("arbitrary",)),
    )(h, g, win_t, cos_t, sin_t)


def win_bwd(dh, h, g, dqkv, du, dgates, win_t, dep, name):
    t, d = h.shape
    tm = TOKEN_TILE

    def body(dh_ref, h_ref, g_ref, dqkv_ref, du_ref, dgt_ref, w_ref, dep_ref, dhi_ref, dg_ref):
        dn = (_nn(dqkv_ref[...], w_ref[0:QKV_WIDTH, :])
              + _nn(du_ref[...], w_ref[QKV_WIDTH:QKV_WIDTH + SSM_WIDTH, :])
              + _nn(dgt_ref[...], w_ref[QKV_WIDTH + SSM_WIDTH:, :]))
        xh, r = _rms_stats(h_ref[...])
        dx, dg = _rms_bwd(dn, xh, r, g_ref[...])
        dhi_ref[...] = dh_ref[...] + dx

        @pl.when(pl.program_id(0) == 0)
        def _():
            dg_ref[...] = jnp.zeros_like(dg_ref)

        dg_ref[...] += dg

    return pl.pallas_call(
        body, name=name, grid=(t // tm,),
        in_specs=[_row_spec(tm, d), _row_spec(tm, d), _acc_spec((1, d)), _row_spec(tm, QKV_WIDTH),
                  _row_spec(tm, SSM_WIDTH), _row_spec(tm, 2 * d), _VMEM, _ANY],
        out_specs=[_row_spec(tm, d), _acc_spec((1, d))],
        out_shape=[jax.ShapeDtypeStruct((t, d), F32), jax.ShapeDtypeStruct((1, d), F32)],
        compiler_params=_params(("arbitrary",)),
    )(dh, h, g, dqkv, du, dgates, win_t, dep)


def _attn_mask(blk):
    q_pos = blk * BLK + lax.broadcasted_iota(jnp.int32, (BLK, 3 * BLK), 0) - PAD_FRONT
    col = lax.broadcasted_iota(jnp.int32, (BLK, 3 * BLK), 1)
    part = col // BLK
    k_pos = jnp.where(part == 0, col, (blk + part - 2) * BLK + (col - part * BLK)) - PAD_FRONT
    dist = q_pos - k_pos
    meta_ok = (part == 0) & (k_pos >= 0) & (dist >= 0)
    band_ok = (part > 0) & (k_pos >= N_META) & (dist >= 0) & (dist < WINDOW)
    return meta_ok | band_ok


def _head_halves(x128, kv):
    x = x128.astype(F32)
    lane = lax.broadcasted_iota(jnp.int32, x.shape, 1)
    swapped = pltpu.roll(x, HEAD_DIM, 1)
    lo, hi = (x, swapped) if kv == 0 else (swapped, x)
    return jnp.where(lane < HEAD_DIM, lo, 0.0).astype(BF16), jnp.where(lane >= HEAD_DIM, hi, 0.0).astype(BF16)


def _gather_keys(meta_ref, prev_ref, cur_ref, lo):
    return jnp.concatenate([meta_ref[:, lo:lo + 128], prev_ref[:, lo:lo + 128], cur_ref[:, lo:lo + 128]], axis=0)


def _softmax_with_sink(s, mask, sink):
    s = jnp.where(mask, s * (HEAD_DIM ** -0.5), NEG_INF)
    m = jnp.maximum(jnp.max(s, axis=-1, keepdims=True), sink)
    p = jnp.exp(s - m)
    p_sink = jnp.exp(sink - m)
    inv = 1.0 / (jnp.sum(p, axis=-1, keepdims=True) + p_sink)
    return p * inv, p_sink * inv


def attn_fwd(qkv, sinks, name):
    t = qkv.shape[0]
    nb = t // BLK

    def body(sink_ref, meta_ref, prev_ref, cur_ref, o_ref):
        blk = pl.program_id(0)
        mask = _attn_mask(blk)
        k128 = _gather_keys(meta_ref, prev_ref, cur_ref, ATTN_WIDTH)
        v128 = _gather_keys(meta_ref, prev_ref, cur_ref, ATTN_WIDTH + KV_WIDTH)
        for kv in range(2):
            k_lo, k_hi = _head_halves(k128, kv)
            v_lo, v_hi = _head_halves(v128, kv)
            for pair in range(2):
                lanes = slice((2 * kv + pair) * 128, (2 * kv + pair + 1) * 128)
                q128 = cur_ref[:, lanes]
                head = 4 * kv + 2 * pair
                p_a, _ = _softmax_with_sink(_nt(q128, k_lo), mask, sink_ref[0, head])
                p_b, _ = _softmax_with_sink(_nt(q128, k_hi), mask, sink_ref[0, head + 1])
                o_ref[:, lanes] = (_nn(p_a.astype(BF16), v_lo) + _nn(p_b.astype(BF16), v_hi)).astype(BF16)

    blk_spec = lambda f: pl.BlockSpec((BLK, QKV_WIDTH), f)
    return pl.pallas_call(
        body, name=name, grid=(nb,),
        in_specs=[_SMEM, blk_spec(lambda i: (0, 0)), blk_spec(lambda i: (jnp.maximum(i - 1, 0), 0)),
                  blk_spec(lambda i: (i, 0))],
        out_specs=_row_spec(BLK, ATTN_WIDTH),
        out_shape=jax.ShapeDtypeStruct((t, ATTN_WIDTH), BF16),
        compiler_params=_params(("arbitrary",)),
    )(sinks, qkv, qkv, qkv)


def attn_bwd(qkv, do, sinks, cos_t, sin_t, name):
    t = qkv.shape[0]
    nb = t // BLK

    def body(sink_ref, meta_ref, prev_ref, cur_ref, do_ref, c_ref, s_ref, dqkv_ref, dsink_ref, carry_ref, macc_ref):
        step = pl.program_id(0)
        blk = nb - 1 - step

        @pl.when(step == 0)
        def _():
            dsink_ref[...] = jnp.zeros_like(dsink_ref)
            carry_ref[...] = jnp.zeros_like(carry_ref)
            macc_ref[...] = jnp.zeros_like(macc_ref)

        mask = _attn_mask(blk)
        lane = lax.broadcasted_iota(jnp.int32, (3 * BLK, 128), 1)
        k128 = _gather_keys(meta_ref, prev_ref, cur_ref, ATTN_WIDTH)
        v128 = _gather_keys(meta_ref, prev_ref, cur_ref, ATTN_WIDTH + KV_WIDTH)
        cos_b, sin_b = c_ref[...], s_ref[...]
        dk_heads, dv_heads = [], []
        for kv in range(2):
            k_lo, k_hi = _head_halves(k128, kv)
            v_lo, v_hi = _head_halves(v128, kv)
            dk_acc = jnp.zeros((3 * BLK, 128), F32)
            dv_acc = jnp.zeros((3 * BLK, 128), F32)
            for pair in range(2):
                lanes = slice((2 * kv + pair) * 128, (2 * kv + pair + 1) * 128)
                q128 = cur_ref[:, lanes]
                do128 = do_ref[:, lanes]
                head = 4 * kv + 2 * pair
                ds_pair, p_pair = [], []
                for half, (k_h, v_h) in enumerate(((k_lo, v_lo), (k_hi, v_hi))):
                    p, p_sink = _softmax_with_sink(_nt(q128, k_h), mask, sink_ref[0, head + half])
                    dp = _nt(do128, v_h)
                    dsum = jnp.sum(p * dp, axis=-1, keepdims=True)
                    ds_pair.append((p * (dp - dsum) * (HEAD_DIM ** -0.5)).astype(BF16))
                    p_pair.append(p.astype(BF16))
                    dsink = -jnp.sum(p_sink * dsum, axis=0, keepdims=True)
                    dsink_ref[head + half:head + half + 1, :] += jnp.broadcast_to(dsink, (1, 128))
                dq = _nn(ds_pair[0], k_lo) + _nn(ds_pair[1], k_hi)
                dqkv_ref[:, lanes] = _rope(dq, cos_b, sin_b, -1.0).astype(BF16)
                dk_acc += jnp.where(lane < HEAD_DIM, _tn(ds_pair[0], q128), _tn(ds_pair[1], q128))
                dv_acc += jnp.where(lane < HEAD_DIM, _tn(p_pair[0], do128), _tn(p_pair[1], do128))
            dk_heads.append(dk_acc + pltpu.roll(dk_acc, HEAD_DIM, 1))
            dv_heads.append(dv_acc + pltpu.roll(dv_acc, HEAD_DIM, 1))
        dkv = jnp.concatenate([jnp.where(lane < HEAD_DIM, dk_heads[0], dk_heads[1]),
                               jnp.where(lane < HEAD_DIM, dv_heads[0], dv_heads[1])], axis=1)
        macc_ref[...] += dkv[0:BLK]
        is_last = (blk == 0).astype(F32)
        mine = dkv[2 * BLK:3 * BLK] + carry_ref[...] + is_last * macc_ref[...]
        carry_ref[...] = dkv[BLK:2 * BLK]
        dqkv_ref[:, ATTN_WIDTH:ATTN_WIDTH + KV_WIDTH] = _rope(mine[:, 0:128], cos_b, sin_b, -1.0).astype(BF16)
        dqkv_ref[:, ATTN_WIDTH + KV_WIDTH:QKV_WIDTH] = mine[:, 128:256].astype(BF16)

    rev = lambda i: nb - 1 - i
    blk_spec = lambda f: pl.BlockSpec((BLK, QKV_WIDTH), f)
    return pl.pallas_call(
        body, name=name, grid=(nb,),
        in_specs=[_SMEM, blk_spec(lambda i: (0, 0)), blk_spec(lambda i: (jnp.maximum(rev(i) - 1, 0), 0)),
                  blk_spec(lambda i: (rev(i), 0)), pl.BlockSpec((BLK, ATTN_WIDTH), lambda i: (rev(i), 0)),
                  pl.BlockSpec((BLK, 128), lambda i: (rev(i), 0)), pl.BlockSpec((BLK, 128), lambda i: (rev(i), 0))],
        out_specs=[pl.BlockSpec((BLK, QKV_WIDTH), lambda i: (rev(i), 0)), _acc_spec((N_Q_HEADS, 128))],
        out_shape=[jax.ShapeDtypeStruct((t, QKV_WIDTH), BF16), jax.ShapeDtypeStruct((N_Q_HEADS, 128), F32)],
        scratch_shapes=[pltpu.VMEM((BLK, 256), F32), pltpu.VMEM((BLK, 256), F32)],
        compiler_params=_params(("arbitrary",)),
    )(sinks, qkv, qkv, qkv, do, cos_t, sin_t)


def _cmul(ar, ai, br, bi):
    return ar * br - ai * bi, ar * bi + ai * br


def ssm_prep(a_re, a_im, log_dt, b_re_t, b_im_t, name):
    def body(ar_ref, ai_ref, ldt_ref, br_ref, bi_ref, lr_ref, li_ref, bbr_ref, bbi_ref):
        ar, ai = ar_ref[...], ai_ref[...]
        dt = jnp.exp(ldt_ref[...])
        mag = jnp.exp(ar * dt)
        lr = mag * jnp.cos(ai * dt)
        li = mag * jnp.sin(ai * dt)
        den = ar * ar + ai * ai
        nr = lr - 1.0
        cr = ((nr * ar + li * ai) / den)[:, None, :]
        ci = ((li * ar - nr * ai) / den)[:, None, :]
        br, bi = br_ref[...], bi_ref[...]
        lr_ref[...] = lr
        li_ref[...] = li
        bbr_ref[...] = cr * br - ci * bi
        bbi_ref[...] = cr * bi + ci * br

    gp = jax.ShapeDtypeStruct(a_re.shape, F32)
    gcp = jax.ShapeDtypeStruct(b_re_t.shape, F32)
    return pl.pallas_call(body, name=name, out_shape=[gp, gp, gcp, gcp],
                          in_specs=[_VMEM] * 5, out_specs=[_VMEM] * 4)(a_re, a_im, log_dt, b_re_t, b_im_t)


def ssm_prep_bwd(a_re, a_im, log_dt, b_re_t, b_im_t, dl_re, dl_im, dbb_re, dbb_im, name):
    def body(ar_ref, ai_ref, ldt_ref, br_ref, bi_ref, dlr_ref, dli_ref, dbbr_ref, dbbi_ref,
             dar_ref, dai_ref, dldt_ref, dbr_ref, dbi_ref):
        ar, ai = ar_ref[...], ai_ref[...]
        dt = jnp.exp(ldt_ref[...])
        mag = jnp.exp(ar * dt)
        lr = mag * jnp.cos(ai * dt)
        li = mag * jnp.sin(ai * dt)
        den = ar * ar + ai * ai
        nr = lr - 1.0
        cr = (nr * ar + li * ai) / den
        ci = (li * ar - nr * ai) / den
        br, bi = br_ref[...], bi_ref[...]
        dbbr, dbbi = dbbr_ref[...], dbbi_ref[...]
        dbr_ref[...] = cr[:, None, :] * dbbr + ci[:, None, :] * dbbi
        dbi_ref[...] = cr[:, None, :] * dbbi - ci[:, None, :] * dbbr
        dcr = jnp.sum(br * dbbr + bi * dbbi, axis=1)
        dci = jnp.sum(br * dbbi - bi * dbbr, axis=1)
        d_num_r = dcr / den
        d_num_i = dci / den
        d_den = -(dcr * cr + dci * ci) / den
        d_lr = dlr_ref[...] + d_num_r * ar - d_num_i * ai
        d_li = dli_ref[...] + d_num_r * ai + d_num_i * ar
        d_ar = d_num_r * nr + d_num_i * li + d_den * 2.0 * ar
        d_ai = d_num_r * li - d_num_i * nr + d_den * 2.0 * ai
        d_mag = (d_lr * lr + d_li * li) / mag
        d_theta = d_li * lr - d_lr * li
        d_ardt = d_mag * mag
        dar_ref[...] = d_ar + d_ardt * dt
        dai_ref[...] = d_ai + d_theta * dt
        d_dt = jnp.sum(d_ardt * ar + d_theta * ai, axis=1, keepdims=True)
        dldt_ref[...] = d_dt * dt

    gp = jax.ShapeDtypeStruct(a_re.shape, F32)
    gcp = jax.ShapeDtypeStruct(b_re_t.shape, F32)
    return pl.pallas_call(body, name=name, out_shape=[gp, gp, jax.ShapeDtypeStruct(log_dt.shape, F32), gcp, gcp],
                          in_specs=[_VMEM] * 9, out_specs=[_VMEM] * 5,
                          )(a_re, a_im, log_dt, b_re_t, b_im_t, dl_re, dl_im, dbb_re, dbb_im)


N_CHUNK = 4
U_CHUNK = SSM_WIDTH // N_CHUNK
H_CHUNK = STATE_WIDTH // N_CHUNK
SUB = 8


def _block_diag_b(bb):
    x = bb.reshape(N_CHUNK, 8, SSM_GROUP, 1, SSM_STATE)
    same = (jnp.arange(8)[:, None] == jnp.arange(8)[None, :])[None, :, None, :, None]
    return jnp.where(same, x, 0.0).reshape(N_CHUNK, U_CHUNK, H_CHUNK)


def _block_diag_c(c):
    x = jnp.swapaxes(c.reshape(N_CHUNK, 8, SSM_GROUP, SSM_STATE), 2, 3)[:, :, :, None, :]
    same = (jnp.arange(8)[:, None] == jnp.arange(8)[None, :])[None, :, None, :, None]
    return jnp.where(same, x, 0.0).reshape(N_CHUNK, H_CHUNK, U_CHUNK)


def _diag_of_b(m):
    x = m.reshape(N_CHUNK, 8, SSM_GROUP, 8, SSM_STATE)
    return jnp.stack([x[:, g, :, g, :] for g in range(8)], axis=1).reshape(SSM_GROUPS, SSM_GROUP, SSM_STATE)


def _diag_of_c(m):
    x = m.reshape(N_CHUNK, 8, SSM_STATE, 8, SSM_GROUP)
    d = jnp.stack([x[:, g, :, g, :] for g in range(8)], axis=1)
    return jnp.swapaxes(d, 2, 3).reshape(SSM_GROUPS, SSM_GROUP, SSM_STATE)


def _lambda_tables(lr, li, reverse):
    p1 = (lr, li)
    p2 = _cmul(*p1, *p1)
    p4 = _cmul(*p2, *p2)
    rows = [p1]
    for _ in range(SUB - 1):
        rows.append(_cmul(*rows[-1], *p1))
    if reverse:
        rows = rows[::-1]
    return p1, p2, p4, (jnp.concatenate([r[0] for r in rows], axis=0), jnp.concatenate([r[1] for r in rows], axis=0))


def _scan8(xr, xi, pows, table, cr, ci, reverse):
    row = lax.broadcasted_iota(jnp.int32, xr.shape, 0)
    for d, (pr, pi) in zip((1, 2, 4), pows):
        if reverse:
            sr, si = pltpu.roll(xr, SUB - d, 0), pltpu.roll(xi, SUB - d, 0)
            keep = row < SUB - d
        else:
            sr, si = pltpu.roll(xr, d, 0), pltpu.roll(xi, d, 0)
            keep = row >= d
        sr = jnp.where(keep, sr, 0.0)
        si = jnp.where(keep, si, 0.0)
        xr, xi = xr + pr * sr - pi * si, xi + pr * si + pi * sr
    tr, ti = table
    return xr + tr * cr - ti * ci, xi + tr * ci + ti * cr


def _gelu_and_grad(y):
    k0 = math.sqrt(2.0 / math.pi)
    inner = k0 * (y + 0.044715 * y * y * y)
    th = jnp.tanh(inner)
    g = 0.5 * y * (1.0 + th)
    dg = 0.5 * (1.0 + th) + 0.5 * y * (1.0 - th * th) * k0 * (1.0 + 3.0 * 0.044715 * y * y)
    return g, dg


SEG = BLK // SUB
SCAN_LANES = 512


def _perm_matrix(to_segments):
    a = lax.broadcasted_iota(jnp.int32, (BLK, BLK), 0)
    b = lax.broadcasted_iota(jnp.int32, (BLK, BLK), 1)
    rho, time = (a, b) if to_segments else (b, a)
    return (time == (rho % SUB) * SEG + rho // SUB).astype(BF16)


def _permute_f32(p, x):
    hi = x.astype(BF16)
    r1 = x - hi.astype(F32)
    mid = r1.astype(BF16)
    lo = (r1 - mid.astype(F32)).astype(BF16)
    return _nn(p, hi) + _nn(p, mid) + _nn(p, lo)


def _power_table(lr, li, pr_ref, pi_ref):
    cur = (lr, li)
    for r in range(SEG):
        pr_ref[r:r + 1, :] = cur[0]
        pi_ref[r:r + 1, :] = cur[1]
        cur = _cmul(*cur, lr, li)


def _segment_scan(xr_ref, xi_ref, lanes, lam, table_row, cr_ref, ci_ref, reverse, extra=None):
    lr, li = lam
    row = lax.broadcasted_iota(jnp.int32, (SUB, SCAN_LANES), 0)

    def rows_of(k):
        r = SEG - 1 - k if reverse else k
        return pl.ds(pl.multiple_of(r * SUB, SUB), SUB)

    def first(k, st):
        sr, si = st
        rows = rows_of(k)
        nr = lr * sr - li * si + xr_ref[rows, lanes]
        ni = lr * si + li * sr + xi_ref[rows, lanes]
        xr_ref[rows, lanes] = nr
        xi_ref[rows, lanes] = ni
        return nr, ni

    zero = jnp.zeros((SUB, SCAN_LANES), F32)
    er, ei = lax.fori_loop(0, SEG, first, (zero, zero))
    l16 = table_row(SEG - 1)
    q1, q2, q4, tab = _lambda_tables(l16[0], l16[1], reverse)
    c_r, c_i = cr_ref[:, lanes], ci_ref[:, lanes]
    gr, gi = _scan8(er, ei, (q1, q2, q4), tab, c_r, c_i, reverse)
    if reverse:
        cin_r = jnp.where(row == SUB - 1, c_r, pltpu.roll(gr, SUB - 1, 0))
        cin_i = jnp.where(row == SUB - 1, c_i, pltpu.roll(gi, SUB - 1, 0))
        cr_ref[:, lanes] = gr[0:1]
        ci_ref[:, lanes] = gi[0:1]
    else:
        cin_r = jnp.where(row == 0, c_r, pltpu.roll(gr, 1, 0))
        cin_i = jnp.where(row == 0, c_i, pltpu.roll(gi, 1, 0))
        cr_ref[:, lanes] = gr[SUB - 1:SUB]
        ci_ref[:, lanes] = gi[SUB - 1:SUB]

    def second(k, carry):
        rows = rows_of(k)
        tr, ti = table_row(k)
        ar = xr_ref[rows, lanes] + tr * cin_r - ti * cin_i
        ai = xi_ref[rows, lanes] + tr * cin_i + ti * cin_r
        xr_ref[rows, lanes] = ar
        xi_ref[rows, lanes] = ai
        if extra is None:
            return carry
        return extra(rows, carry, ar, ai)

    init = 0 if extra is None else (cin_r, cin_i, zero, zero)
    return lax.fori_loop(0, SEG, second, init)


def ssm_fwd(u, lam_re, lam_im, bb_re, bb_im, cc_re, cc_im, d_skip, name):
    t = u.shape[0]
    tt = BLK

    def body(u_ref, lr_ref, li_ref, bbr_ref, bbi_ref, ccr_ref, cci_ref, d_ref, yg_ref, hr_ref, hi_ref,
             cr_ref, ci_ref, pr_ref, pi_ref, up_ref, y_ref):
        @pl.when(pl.program_id(0) == 0)
        def _():
            cr_ref[...] = jnp.zeros_like(cr_ref)
            ci_ref[...] = jnp.zeros_like(ci_ref)
            _power_table(lr_ref[...], li_ref[...], pr_ref, pi_ref)

        up_ref[...] = _permute_f32(_perm_matrix(True), u_ref[...])
        ub = up_ref[...].astype(BF16)
        for j in range(N_CHUNK):
            hs = slice(j * H_CHUNK, (j + 1) * H_CHUNK)
            us = slice(j * U_CHUNK, (j + 1) * U_CHUNK)
            hr_ref[:, hs] = _nn(ub[:, us], bbr_ref[j])
            hi_ref[:, hs] = _nn(ub[:, us], bbi_ref[j])
        for c in range(STATE_WIDTH // SCAN_LANES):
            lanes = slice(c * SCAN_LANES, (c + 1) * SCAN_LANES)
            _segment_scan(hr_ref, hi_ref, lanes, (lr_ref[:, lanes], li_ref[:, lanes]),
                          lambda k, lanes=lanes: (pr_ref[pl.ds(k, 1), lanes], pi_ref[pl.ds(k, 1), lanes]),
                          cr_ref, ci_ref, False)
        for j in range(N_CHUNK):
            hs = slice(j * H_CHUNK, (j + 1) * H_CHUNK)
            us = slice(j * U_CHUNK, (j + 1) * U_CHUNK)
            y = (_nn(hr_ref[:, hs].astype(BF16), ccr_ref[j]) - _nn(hi_ref[:, hs].astype(BF16), cci_ref[j])
                 + d_ref[:, us] * up_ref[:, us])
            y_ref[:, us] = _gelu_and_grad(y)[0]
        yg_ref[...] = _nn(_perm_matrix(False), y_ref[...].astype(BF16)).astype(BF16)

    return pl.pallas_call(
        body, name=name, grid=(t // tt,),
        in_specs=[_row_spec(tt, SSM_WIDTH), _VMEM, _VMEM, _VMEM, _VMEM, _VMEM, _VMEM, _VMEM],
        out_specs=[_row_spec(tt, SSM_WIDTH), _row_spec(tt, STATE_WIDTH), _row_spec(tt, STATE_WIDTH)],
        out_shape=[jax.ShapeDtypeStruct((t, SSM_WIDTH), BF16), jax.ShapeDtypeStruct((t, STATE_WIDTH), F32),
                   jax.ShapeDtypeStruct((t, STATE_WIDTH), F32)],
        scratch_shapes=[pltpu.VMEM((1, STATE_WIDTH), F32), pltpu.VMEM((1, STATE_WIDTH), F32),
                        pltpu.VMEM((SEG, STATE_WIDTH), F32), pltpu.VMEM((SEG, STATE_WIDTH), F32),
                        pltpu.VMEM((tt, SSM_WIDTH), F32), pltpu.VMEM((tt, SSM_WIDTH), F32)],
        compiler_params=_params(("arbitrary",)),
    )(u, lam_re, lam_im, bb_re, bb_im, cc_re, cc_im, d_skip)


def ssm_bwd(dyg, u, h_re, h_im, lam_re, lam_im, bb_re, bb_im, cc_re, cc_im, d_skip, name):
    t = u.shape[0]
    tt = BLK
    nt = t // tt

    def body(dyg_ref, u_ref, hr_ref, hi_ref, lr_ref, li_ref, bbr_ref, bbi_ref, ccr_ref, cci_ref, d_ref,
             du_ref, dlr_ref, dli_ref, dbbr_ref, dbbi_ref, dccr_ref, dcci_ref, dd_ref,
             ar_ref, ai_ref, cr_ref, ci_ref, pr_ref, pi_ref, up_ref, dy_ref, dup_ref):
        step = pl.program_id(0)
        tile = nt - 1 - step

        @pl.when(step == 0)
        def _():
            for ref in (cr_ref, ci_ref, dlr_ref, dli_ref, dbbr_ref, dbbi_ref, dccr_ref, dcci_ref, dd_ref):
                ref[...] = jnp.zeros_like(ref)
            _power_table(lr_ref[...], li_ref[...], pr_ref, pi_ref)

        to_segments = _perm_matrix(True)
        up_ref[...] = _permute_f32(to_segments, u_ref[...])
        dy_ref[...] = _permute_f32(to_segments, dyg_ref[...])
        uv = up_ref[...]
        ub = uv.astype(BF16)
        dskip = d_ref[...]
        for j in range(N_CHUNK):
            hs = slice(j * H_CHUNK, (j + 1) * H_CHUNK)
            us = slice(j * U_CHUNK, (j + 1) * U_CHUNK)
            hrb = hr_ref[:, hs].astype(BF16)
            hib = hi_ref[:, hs].astype(BF16)
            y = _nn(hrb, ccr_ref[j]) - _nn(hib, cci_ref[j]) + dskip[:, us] * uv[:, us]
            dy = dy_ref[:, us] * _gelu_and_grad(y)[1]
            dy_ref[:, us] = dy
            dyb = dy.astype(BF16)
            dccr_ref[j] += _tn(hrb, dyb)
            dcci_ref[j] -= _tn(hib, dyb)
            ar_ref[:, hs] = _nt(dyb, ccr_ref[j])
            ai_ref[:, hs] = -_nt(dyb, cci_ref[j])
        dd_ref[...] += jnp.sum(dy_ref[...] * uv, axis=0, keepdims=True)

        for c in range(STATE_WIDTH // SCAN_LANES):
            lanes = slice(c * SCAN_LANES, (c + 1) * SCAN_LANES)

            def dlambda(rows, carry, ar, ai, lanes=lanes):
                nr, ni, accr, acci = carry
                hr, hi = hr_ref[rows, lanes], hi_ref[rows, lanes]
                return ar, ai, accr + nr * hr + ni * hi, acci + ni * hr - nr * hi

            _, _, accr, acci = _segment_scan(
                ar_ref, ai_ref, lanes, (lr_ref[:, lanes], -li_ref[:, lanes]),
                lambda k, lanes=lanes: (pr_ref[pl.ds(k, 1), lanes], -pi_ref[pl.ds(k, 1), lanes]),
                cr_ref, ci_ref, True, dlambda)
            dlr_ref[:, lanes] += accr
            dli_ref[:, lanes] += acci

        rho = lax.broadcasted_iota(jnp.int32, (tt, U_CHUNK), 0)
        time = tile * tt + (rho % SUB) * SEG + rho // SUB
        for j in range(N_CHUNK):
            hs = slice(j * H_CHUNK, (j + 1) * H_CHUNK)
            us = slice(j * U_CHUNK, (j + 1) * U_CHUNK)
            arb = ar_ref[:, hs].astype(BF16)
            aib = ai_ref[:, hs].astype(BF16)
            dbbr_ref[j] += _tn(ub[:, us], arb)
            dbbi_ref[j] += _tn(ub[:, us], aib)
            du = _nt(arb, bbr_ref[j]) + _nt(aib, bbi_ref[j]) + dy_ref[:, us] * dskip[:, us]
            dup_ref[:, us] = jnp.where(time >= PAD_FRONT, du, 0.0)
        du_ref[...] = _nn(_perm_matrix(False), dup_ref[...].astype(BF16)).astype(BF16)

    rev = lambda i: (nt - 1 - i, 0)
    full = lambda shape: pl.BlockSpec(shape, lambda i: (0,) * len(shape))
    return pl.pallas_call(
        body, name=name, grid=(nt,),
        in_specs=[pl.BlockSpec((tt, SSM_WIDTH), rev), pl.BlockSpec((tt, SSM_WIDTH), rev),
                  pl.BlockSpec((tt, STATE_WIDTH), rev), pl.BlockSpec((tt, STATE_WIDTH), rev),
                  _VMEM, _VMEM, _VMEM, _VMEM, _VMEM, _VMEM, _VMEM],
        out_specs=[pl.BlockSpec((tt, SSM_WIDTH), rev), full((SUB, STATE_WIDTH)), full((SUB, STATE_WIDTH)),
                   full((N_CHUNK, U_CHUNK, H_CHUNK)), full((N_CHUNK, U_CHUNK, H_CHUNK)),
                   full((N_CHUNK, H_CHUNK, U_CHUNK)), full((N_CHUNK, H_CHUNK, U_CHUNK)), full((1, SSM_WIDTH))],
        out_shape=[jax.ShapeDtypeStruct((t, SSM_WIDTH), BF16),
                   jax.ShapeDtypeStruct((SUB, STATE_WIDTH), F32), jax.ShapeDtypeStruct((SUB, STATE_WIDTH), F32),
                   jax.ShapeDtypeStruct((N_CHUNK, U_CHUNK, H_CHUNK), F32),
                   jax.ShapeDtypeStruct((N_CHUNK, U_CHUNK, H_CHUNK), F32),
                   jax.ShapeDtypeStruct((N_CHUNK, H_CHUNK, U_CHUNK), F32),
                   jax.ShapeDtypeStruct((N_CHUNK, H_CHUNK, U_CHUNK), F32),
                   jax.ShapeDtypeStruct((1, SSM_WIDTH), F32)],
        scratch_shapes=[pltpu.VMEM((tt, STATE_WIDTH), F32), pltpu.VMEM((tt, STATE_WIDTH), F32),
                        pltpu.VMEM((1, STATE_WIDTH), F32), pltpu.VMEM((1, STATE_WIDTH), F32),
                        pltpu.VMEM((SEG, STATE_WIDTH), F32), pltpu.VMEM((SEG, STATE_WIDTH), F32),
                        pltpu.VMEM((tt, SSM_WIDTH), F32), pltpu.VMEM((tt, SSM_WIDTH), F32),
                        pltpu.VMEM((tt, SSM_WIDTH), F32)],
        compiler_params=_params(("arbitrary",)),
    )(dyg, u, h_re, h_im, lam_re, lam_im, bb_re, bb_im, cc_re, cc_im, d_skip)


def merge_fwd(h, o, yg, gates, wap_t, wv_t, wgg_t, wout, name):
    t, d = h.shape
    tm = TOKEN_TILE

    def body(h_ref, o_ref, yg_ref, gt_ref, wap_ref, wv_ref, wgg_ref, wout_ref, ho_ref, mg_ref, a_ref, sv_ref, sg_ref):
        att = _nt(o_ref[...], wap_ref[...])
        ygv = yg_ref[...]
        sv = _nt(ygv, wv_ref[...])
        sg = _nt(ygv, wgg_ref[...])
        a_ref[...] = att
        sv_ref[...] = sv
        sg_ref[...] = sg
        merged = (jax.nn.sigmoid(gt_ref[:, 0:d]) * att
                  + jax.nn.sigmoid(gt_ref[:, d:2 * d]) * (sv * jax.nn.sigmoid(sg))).astype(BF16)
        mg_ref[...] = merged
        ho_ref[...] = h_ref[...] + _nn(merged, wout_ref[...])

    return pl.pallas_call(
        body, name=name, grid=(t // tm,),
        in_specs=[_row_spec(tm, d), _row_spec(tm, ATTN_WIDTH), _row_spec(tm, SSM_WIDTH), _row_spec(tm, 2 * d),
                  _VMEM, _VMEM, _VMEM, _VMEM],
        out_specs=[_row_spec(tm, d), _row_spec(tm, d), _row_spec(tm, d), _row_spec(tm, d), _row_spec(tm, d)],
        out_shape=[jax.ShapeDtypeStruct((t, d), F32), jax.ShapeDtypeStruct((t, d), BF16),
                   jax.ShapeDtypeStruct((t, d), F32), jax.ShapeDtypeStruct((t, d), F32),
                   jax.ShapeDtypeStruct((t, d), F32)],
        compiler_params=_params(("arbitrary",)),
    )(h, o, yg, gates, wap_t, wv_t, wgg_t, wout)


def merge_bwd(dh, gates, att, sv, sg, wap_t, wv_t, wgg_t, wout, dep, name):
    t, d = dh.shape
    tm = TOKEN_TILE

    def body(dh_ref, gt_ref, a_ref, sv_ref, sg_ref, wap_ref, wv_ref, wgg_ref, wout_ref, dep_ref,
             dgt_ref, da_ref, dsv_ref, dsg_ref, do_ref, dyg_ref, dhb_ref):
        dhb = dh_ref[...].astype(BF16)
        dhb_ref[...] = dhb
        dm = _nt(dhb, wout_ref[...])
        sig_a = jax.nn.sigmoid(gt_ref[:, 0:d])
        sig_s = jax.nn.sigmoid(gt_ref[:, d:2 * d])
        sig_g = jax.nn.sigmoid(sg_ref[...])
        svv = sv_ref[...]
        dgt_ref[:, 0:d] = (dm * a_ref[...] * sig_a * (1.0 - sig_a)).astype(BF16)
        dgt_ref[:, d:2 * d] = (dm * (svv * sig_g) * sig_s * (1.0 - sig_s)).astype(BF16)
        da = (dm * sig_a).astype(BF16)
        d_s = dm * sig_s
        dsv = (d_s * sig_g).astype(BF16)
        dsg = (d_s * svv * sig_g * (1.0 - sig_g)).astype(BF16)
        da_ref[...] = da
        dsv_ref[...] = dsv
        dsg_ref[...] = dsg
        do_ref[...] = _nn(da, wap_ref[...]).astype(BF16)
        dyg_ref[...] = _nn(dsv, wv_ref[...]) + _nn(dsg, wgg_ref[...])

    return pl.pallas_call(
        body, name=name, grid=(t // tm,),
        in_specs=[_row_spec(tm, d), _row_spec(tm, 2 * d), _row_spec(tm, d), _row_spec(tm, d), _row_spec(tm, d),
                  _VMEM, _VMEM, _VMEM, _VMEM, _ANY],
        out_specs=[_row_spec(tm, 2 * d), _row_spec(tm, d), _row_spec(tm, d), _row_spec(tm, d),
                   _row_spec(tm, ATTN_WIDTH), _row_spec(tm, SSM_WIDTH), _row_spec(tm, d)],
        out_shape=[jax.ShapeDtypeStruct((t, 2 * d), BF16), jax.ShapeDtypeStruct((t, d), BF16),
                   jax.ShapeDtypeStruct((t, d), BF16), jax.ShapeDtypeStruct((t, d), BF16),
                   jax.ShapeDtypeStruct((t, ATTN_WIDTH), BF16), jax.ShapeDtypeStruct((t, SSM_WIDTH), F32),
                   jax.ShapeDtypeStruct((t, d), BF16)],
        compiler_params=_params(("arbitrary",)),
    )(dh, gates, att, sv, sg, wap_t, wv_t, wgg_t, wout, dep)


def _adamw_math(w, g, m, v):
    mn = ADAM_B1 * m + (1.0 - ADAM_B1) * g
    vn = ADAM_B2 * v + (1.0 - ADAM_B2) * (g * g)
    m_hat = mn / (1.0 - ADAM_B1 ** ADAM_STEP)
    v_hat = vn / (1.0 - ADAM_B2 ** ADAM_STEP)
    return -ADAM_LR * (m_hat / (jnp.sqrt(v_hat) + ADAM_EPS) + ADAM_WD * w), mn, vn


def adamw_layer(w, g, m, v, layer, prev, name):
    _, rows, cols = w.shape
    tr = rows
    for cand in (512, 256):
        if rows > cand and rows % cand == 0:
            tr = cand
            break

    def body(w_ref, g_ref, m_ref, v_ref, *rest):
        go_ref, d_ref, mo_ref, vo_ref = rest[-4:]
        gv = g_ref[...]
        go_ref[0] = gv
        d_ref[0], mo_ref[0], vo_ref[0] = _adamw_math(w_ref[0], gv, m_ref[0], v_ref[0])

    spec3 = pl.BlockSpec((1, tr, cols), lambda r: (layer, r, 0))
    out = jax.ShapeDtypeStruct(w.shape, F32)
    extra = [] if prev is None else list(prev)
    return pl.pallas_call(
        body, name=name, grid=(rows // tr,),
        in_specs=[spec3, _row_spec(tr, cols), spec3, spec3] + [_ANY] * len(extra),
        out_specs=[spec3] * 4, out_shape=[out] * 4,
        input_output_aliases={4 + j: j for j in range(len(extra))},
        compiler_params=_params(("arbitrary",)),
    )(w, g, m, v, *extra)


def adamw(w, g, m, v, name, minor_swap=False):
    if minor_swap:
        d, mn, vn = adamw(*[jnp.swapaxes(a, -1, -2) for a in (w, g, m, v)], name)
        return jnp.swapaxes(d, -1, -2), jnp.swapaxes(mn, -1, -2), jnp.swapaxes(vn, -1, -2)
    shape = w.shape
    as2d = lambda a: a.reshape(-1, shape[-1]) if a.ndim >= 2 else a.reshape(1, -1)
    w2, g2, m2, v2 = as2d(w), as2d(g), as2d(m), as2d(v)
    rows, cols = w2.shape
    tr = rows
    for cand in (1024, 704, 512, 256):
        if rows > cand and rows % cand == 0:
            tr = cand
            break

    def body(w_ref, g_ref, m_ref, v_ref, d_ref, mo_ref, vo_ref):
        d_ref[...], mo_ref[...], vo_ref[...] = _adamw_math(w_ref[...], g_ref[...], m_ref[...], v_ref[...])

    spec = _row_spec(tr, cols)
    out = jax.ShapeDtypeStruct((rows, cols), F32)
    d, mn, vn = pl.pallas_call(
        body, name=name, grid=(rows // tr,), in_specs=[spec] * 4, out_specs=[spec] * 3, out_shape=[out] * 3,
        compiler_params=_params(("arbitrary",)),
    )(w2, g2, m2, v2)
    return d.reshape(shape), mn.reshape(shape), vn.reshape(shape)


def _my_index():
    return 4 * lax.axis_index("x") + 2 * lax.axis_index("y") + lax.axis_index("c")


def _peer(p):
    return (lax.axis_index("x") ^ ((p >> 2) & 1), lax.axis_index("y") ^ ((p >> 1) & 1), lax.axis_index("c") ^ (p & 1))


_HBM = pl.BlockSpec(memory_space=pltpu.HBM)
_SEM = pl.BlockSpec(memory_space=pltpu.SEMAPHORE)
_EFFECT = pltpu.SideEffectType.DATAFLOW_SIDE_EFFECTING


class Exchange:
    def __init__(self, srcs, scatter, name):
        self.n = n = len(srcs)
        self.scatter = scatter
        self.name = name
        widths = sorted({s.shape[1] for s in srcs}, reverse=True)
        self.ncls = len(widths)
        self.cls = [widths.index(s.shape[1]) for s in srcs]
        self.cnts = [s.shape[0] // N_DEV if scatter else s.shape[0] for s in srcs]
        self.totals = [sum(c for c, k in zip(self.cnts, self.cls) if k == w) for w in range(self.ncls)]
        self.sizer = [max((k for k in range(n) if self.cls[k] == w), key=lambda k: self.cnts[k])
                      for w in range(self.ncls)]
        assert all(N_DEV * self.cnts[self.sizer[w]] >= self.totals[w] for w in range(self.ncls))
        if scatter:
            self.land_shapes = [(N_DEV, c, s.shape[1]) for s, c in zip(srcs, self.cnts)]
        else:
            self.land_shapes = [(N_DEV * c, s.shape[1]) for s, c in zip(srcs, self.cnts)]
        self.dtypes = [s.dtype for s in srcs]

    def _block(self, k, who):
        return pl.ds(pl.multiple_of(who * self.cnts[k], 16), self.cnts[k])

    def _sem(self, p, w):
        return (p - 1) * self.ncls + w

    def start(self, srcs, after):
        n = self.n

        def body(*refs):
            src, land = refs[:n], refs[n:2 * n]
            send_sems, recv_sems = refs[2 * n + 1], refs[2 * n + 2]
            token = refs[-1]
            me = _my_index()
            for p in range(1, N_DEV):
                for k in range(n):
                    if self.scatter:
                        s_ref, d_ref = src[k].at[self._block(k, me ^ p), :], land[k].at[me]
                    else:
                        s_ref, d_ref = src[k], land[k].at[self._block(k, me), :]
                    pltpu.make_async_remote_copy(
                        src_ref=s_ref, dst_ref=d_ref, send_sem=send_sems.at[self._sem(p, self.cls[k])],
                        recv_sem=recv_sems.at[self._sem(p, self.cls[k])], device_id=_peer(p),
                        device_id_type=MESH).start()
            token[...] = jnp.zeros_like(token)

        sems = pltpu.SemaphoreType.DMA(((N_DEV - 1) * self.ncls,))
        thru = [pltpu.HBM(s.shape, s.dtype) for s in srcs] + [pltpu.HBM(shp, dt) for shp, dt in
                                                               zip(self.land_shapes, self.dtypes)]
        lands = [pltpu.with_memory_space_constraint(lax.empty(shp, dt), pltpu.HBM)
                 for shp, dt in zip(self.land_shapes, self.dtypes)]
        out = pl.pallas_call(
            body, name=self.name + "_start",
            in_specs=[_HBM] * (2 * n) + [_ANY],
            out_shape=[sems, sems] + thru + [jax.ShapeDtypeStruct((8, 128), F32)],
            out_specs=[_SEM, _SEM] + [_HBM] * (2 * n) + [_VMEM],
            input_output_aliases={j: 2 + j for j in range(2 * n)},
            compiler_params=pltpu.CompilerParams(has_side_effects=_EFFECT),
        )(*[pltpu.with_memory_space_constraint(s, pltpu.HBM) for s in srcs], *lands, after)
        return out[:-1], out[-1]

    def wait(self, state, after):
        n = self.n
        send_sems, recv_sems = state[0], state[1]
        thru = state[2:]

        def body(*refs):
            src, land = refs[:n], refs[n:2 * n]
            send_sems, recv_sems = refs[2 * n], refs[2 * n + 1]
            for p in range(1, N_DEV):
                for w in range(self.ncls):
                    big = src[self.sizer[w]] if self.scatter else land[self.sizer[w]]
                    span = big.at[pl.ds(0, self.totals[w]), :]
                    copy = pltpu.make_async_remote_copy(
                        src_ref=span, dst_ref=span, send_sem=send_sems.at[self._sem(p, w)],
                        recv_sem=recv_sems.at[self._sem(p, w)],
                        device_id=_peer(p), device_id_type=MESH)
                    copy.wait_send()
                    copy.wait_recv()

        out = pl.pallas_call(
            body, name=self.name + "_wait",
            in_specs=[_HBM] * (2 * n) + [_SEM, _SEM, _ANY],
            out_shape=[pltpu.HBM(a.shape, a.dtype) for a in thru], out_specs=[_HBM] * (2 * n),
            input_output_aliases={j: j for j in range(2 * n)},
            compiler_params=pltpu.CompilerParams(has_side_effects=_EFFECT),
        )(*thru, send_sems, recv_sems, after)
        return out[:n], out[n:]

    def place(self, lands, srcs):
        n = self.n
        assert not self.scatter

        def body(*refs):
            src, land = refs[n:2 * n], refs[2 * n:3 * n]
            bufs, sems = refs[3 * n:4 * n], refs[-1]
            me = _my_index()
            loads = [pltpu.make_async_copy(src[k], bufs[k], sems.at[k]) for k in range(n)]
            stores = [pltpu.make_async_copy(bufs[k], land[k].at[self._block(k, me), :], sems.at[k]) for k in range(n)]
            for cp in loads:
                cp.start()
            for k in range(n):
                loads[k].wait()
                stores[k].start()
            for cp in stores:
                cp.wait()

        return pl.pallas_call(
            body, name=self.name + "_place", in_specs=[_ANY] * (2 * n), out_specs=[_ANY] * n,
            out_shape=[jax.ShapeDtypeStruct(a.shape, a.dtype) for a in lands],
            input_output_aliases={j: j for j in range(n)},
            scratch_shapes=[pltpu.VMEM(s.shape, s.dtype) for s in srcs] + [pltpu.SemaphoreType.DMA((n,))],
        )(*lands, *srcs)


def sum_blocks(landed, full, name):
    _, cnt, cols = landed.shape

    def body(land_ref, full_ref, o_ref, own_ref, sem):
        me = _my_index()
        own = pltpu.make_async_copy(full_ref.at[pl.ds(pl.multiple_of(me * cnt, 16), cnt), :], own_ref, sem)
        own.start()
        acc = land_ref[me ^ 1].astype(F32)
        for p in range(2, N_DEV):
            acc = acc + land_ref[me ^ p].astype(F32)
        own.wait()
        o_ref[...] = acc + own_ref[...].astype(F32)

    return pl.pallas_call(
        body, name=name, in_specs=[_VMEM, _ANY], out_specs=_VMEM,
        out_shape=jax.ShapeDtypeStruct((cnt, cols), F32),
        scratch_shapes=[pltpu.VMEM((cnt, cols), landed.dtype), pltpu.SemaphoreType.DMA],
        compiler_params=_params(),
    )(landed, full)


def sum_slots(slots, name):
    _, rows, cols = slots.shape
    tr = rows
    if rows > 512:
        for cand in (256, 128, 64, 32, 16, 8):
            if rows % cand == 0:
                tr = cand
                break

    def body(s_ref, o_ref):
        acc = s_ref[0].astype(F32)
        for j in range(1, N_DEV):
            acc = acc + s_ref[j].astype(F32)
        o_ref[...] = acc

    return pl.pallas_call(
        body, name=name, grid=(rows // tr,),
        in_specs=[pl.BlockSpec((N_DEV, tr, cols), lambda i: (0, i, 0))], out_specs=_row_spec(tr, cols),
        out_shape=jax.ShapeDtypeStruct((rows, cols), F32), compiler_params=_params(("arbitrary",)),
    )(slots)


BIG_T = ("ffn1_w_gate", "ffn1_w_up", "w_in", "ffn2_w_gate", "ffn2_w_up")
BIG_N = ("ffn1_w_down", "w_out", "ffn2_w_down")
HALF_T = ("w_attn_proj", "w_glu_v", "w_glu_g")
SMALL = ("ffn1_norm", "mix_norm", "attn_sinks", "ssm_a_re", "ssm_a_im", "ssm_log_dt", "ssm_b_re", "ssm_b_im",
         "ssm_c_re", "ssm_c_im", "ssm_d", "ffn2_norm", "final_norm")
PARTS = {"ffn1": ("ffn1_w_gate", "ffn1_w_up", "ffn1_w_down"),
         "mix": ("w_in", "w_out", "w_attn_proj", "w_glu_v", "w_glu_g"),
         "ffn2": ("ffn2_w_gate", "ffn2_w_up", "ffn2_w_down")}


def _to_rows(name, a):
    return a if name in BIG_N else jnp.swapaxes(a, -1, -2)


def local_step(x, tgt, get_weights, put_grads, small):
    seq, d = x.shape
    t = PAD_FRONT + N_META + seq
    cos_t, sin_t = rope_tables(t)
    row = lambda a: a.reshape(1, -1)
    saved = []
    h = None
    for i in range(DEPTH):
        s = {}
        w = dict(get_weights(i, "ffn1", h))
        if i == 0:
            h = jnp.concatenate([jnp.zeros((PAD_FRONT, d), F32), w["meta_tokens"], x], axis=0)
        s["h0"] = h
        h, s["n1"], s["a1"], s["b1"] = ffn_fwd(h, row(small["ffn1_norm"][i]), w["ffn1_w_gate"], w["ffn1_w_up"],
                                               w["ffn1_w_down"], f"ffn1_fwd_{i}")
        s["h1"] = h
        w.update(get_weights(i, "mix", h))
        s["n2"], s["qkv"], s["u"], s["gates"] = win_fwd(h, row(small["mix_norm"][i]), w["w_in"], cos_t, sin_t,
                                                        f"win_fwd_{i}")
        b_re_t = jnp.swapaxes(small["ssm_b_re"][i], 1, 2)
        b_im_t = jnp.swapaxes(small["ssm_b_im"][i], 1, 2)
        s["b_t"] = (b_re_t, b_im_t)
        lam_re, lam_im, bbar_re, bbar_im = ssm_prep(small["ssm_a_re"][i], small["ssm_a_im"][i],
                                                    small["ssm_log_dt"][i].reshape(-1, 1), b_re_t, b_im_t, f"ssm_prep_{i}")
        s["ssm"] = (row(lam_re), row(lam_im), _block_diag_b(bbar_re).astype(BF16), _block_diag_b(bbar_im).astype(BF16),
                    _block_diag_c(small["ssm_c_re"][i]).astype(BF16), _block_diag_c(small["ssm_c_im"][i]).astype(BF16),
                    row(small["ssm_d"][i]))
        s["yg"], s["h_re"], s["h_im"] = ssm_fwd(s["u"], *s["ssm"], f"ssm_fwd_{i}")
        s["o"] = attn_fwd(s["qkv"], row(small["attn_sinks"][i]), f"attn_fwd_{i}")
        h, s["merged"], s["att"], s["sv"], s["sg"] = merge_fwd(
            h, s["o"], s["yg"], s["gates"], w["w_attn_proj"], w["w_glu_v"], w["w_glu_g"], w["w_out"],
            f"merge_fwd_{i}")
        s["h2"] = h
        w.update(get_weights(i, "ffn2", h))
        h, s["n3"], s["a3"], s["b3"] = ffn_fwd(h, row(small["ffn2_norm"][i]), w["ffn2_w_gate"], w["ffn2_w_up"],
                                               w["ffn2_w_down"], f"ffn2_fwd_{i}")
        s["w"] = w
        saved.append(s)

    loss, dh, d_final = head_fwd_bwd(h, row(small["final_norm"]), tgt)
    gs = {k: [None] * DEPTH for k in SMALL if k != "final_norm"}
    dep = loss
    for i in reversed(range(DEPTH)):
        s = saved[i]
        w = s["w"]
        dh, da, db, sact, dhb, dg = ffn_bwd(dh, s["h2"], row(small["ffn2_norm"][i]), s["a3"], s["b3"], w["ffn2_w_gate"],
                                            w["ffn2_w_up"], w["ffn2_w_down"], dep, f"ffn2_bwd_{i}")
        gs["ffn2_norm"][i] = dg[0]
        dep = put_grads(i, "ffn2", {"ffn2_w_gate": tn_matmul(da, s["n3"], f"ffn2_dwg_{i}"),
                                    "ffn2_w_up": tn_matmul(db, s["n3"], f"ffn2_dwu_{i}"),
                                    "ffn2_w_down": tn_matmul(sact, dhb, f"ffn2_dwd_{i}")})

        dgates, datt, dsv, dsg, do, dyg, dhb = merge_bwd(dh, s["gates"], s["att"], s["sv"], s["sg"], w["w_attn_proj"],
                                                         w["w_glu_v"], w["w_glu_g"], w["w_out"], dep, f"merge_bwd_{i}")
        gmix = {"w_out": tn_matmul(s["merged"], dhb, f"dwout_{i}"),
                "w_attn_proj": tn_matmul(datt, s["o"], f"dwap_{i}"),
                "w_glu_v": tn_matmul(dsv, s["yg"], f"dwv_{i}"),
                "w_glu_g": tn_matmul(dsg, s["yg"], f"dwgg_{i}")}
        dqkv, dsink = attn_bwd(s["qkv"], do, row(small["attn_sinks"][i]), cos_t, sin_t, f"attn_bwd_{i}")
        gs["attn_sinks"][i] = dsink[:, 0]
        du, dl_re, dl_im, dbb_re, dbb_im, dcc_re, dcc_im, dd = ssm_bwd(dyg, s["u"], s["h_re"], s["h_im"], *s["ssm"],
                                                                      f"ssm_bwd_{i}")
        fold = lambda a: jnp.sum(a, axis=0).reshape(SSM_GROUPS, SSM_STATE)
        da_re, da_im, dldt, db_re_t, db_im_t = ssm_prep_bwd(
            small["ssm_a_re"][i], small["ssm_a_im"][i], small["ssm_log_dt"][i].reshape(-1, 1), *s["b_t"],
            fold(dl_re), fold(dl_im), _diag_of_b(dbb_re), _diag_of_b(dbb_im), f"ssm_prep_bwd_{i}")
        gs["ssm_a_re"][i], gs["ssm_a_im"][i], gs["ssm_log_dt"][i] = da_re, da_im, dldt[:, 0]
        gs["ssm_b_re"][i], gs["ssm_b_im"][i] = jnp.swapaxes(db_re_t, 1, 2), jnp.swapaxes(db_im_t, 1, 2)
        gs["ssm_c_re"][i], gs["ssm_c_im"][i] = _diag_of_c(dcc_re), _diag_of_c(dcc_im)
        gs["ssm_d"][i] = dd[0]
        gmix["w_in"] = tn_matmul([dqkv, du, dgates], s["n2"], f"dwin_{i}")
        dep = put_grads(i, "mix", gmix)
        dh, dg = win_bwd(dh, s["h1"], row(small["mix_norm"][i]), dqkv, du, dgates, w["w_in"], dep, f"win_bwd_{i}")
        gs["mix_norm"][i] = dg[0]

        dh, da, db, sact, dhb, dg = ffn_bwd(dh, s["h0"], row(small["ffn1_norm"][i]), s["a1"], s["b1"], w["ffn1_w_gate"],
                                            w["ffn1_w_up"], w["ffn1_w_down"], dep, f"ffn1_bwd_{i}")
        gs["ffn1_norm"][i] = dg[0]
        if i > 0:
            dep = put_grads(i, "ffn1", {"ffn1_w_gate": tn_matmul(da, s["n1"], f"ffn1_dwg_{i}"),
                                        "ffn1_w_up": tn_matmul(db, s["n1"], f"ffn1_dwu_{i}"),
                                        "ffn1_w_down": tn_matmul(sact, dhb, f"ffn1_dwd_{i}")})
        else:
            for k, xa, ya in (("ffn1_w_down", sact, dhb), ("ffn1_w_gate", da, s["n1"]), ("ffn1_w_up", db, s["n1"])):
                dep = put_grads(i, "ffn1", {k: tn_matmul(xa, ya, f"d_{k}_{i}", dep)})

    gs = {k: jnp.stack(v) for k, v in gs.items()}
    gs["final_norm"] = d_final[0]
    return loss[0, 0], dh[PAD_FRONT + N_META:], dh[PAD_FRONT:PAD_FRONT + N_META], gs, dep


def _pack_rows(arrays, cols):
    flat = jnp.concatenate([a.reshape(-1) for a in arrays])
    rows = -(-flat.shape[0] // cols)
    rows = -(-rows // 16) * 16
    return jnp.pad(flat, (0, rows * cols - flat.shape[0])).reshape(rows, cols)


def _unpack_rows(packed, shapes):
    flat = packed.reshape(-1)
    out, off = [], 0
    for shp in shapes:
        n = math.prod(shp)
        out.append(flat[off:off + n].reshape(shp))
        off += n
    return out


def kernel(x, meta_tokens, ffn1_norm, ffn1_w_gate, ffn1_w_up, ffn1_w_down, mix_norm, w_in, attn_sinks, ssm_a_re, ssm_a_im, ssm_log_dt, ssm_b_re, ssm_b_im, ssm_c_re, ssm_c_im, ssm_d, w_attn_proj, w_glu_v, w_glu_g, w_out, ffn2_norm, ffn2_w_gate, ffn2_w_up, ffn2_w_down, final_norm, loss_target, m_meta_tokens, m_ffn1_norm, m_ffn1_w_gate, m_ffn1_w_up, m_ffn1_w_down, m_mix_norm, m_w_in, m_attn_sinks, m_ssm_a_re, m_ssm_a_im, m_ssm_log_dt, m_ssm_b_re, m_ssm_b_im, m_ssm_c_re, m_ssm_c_im, m_ssm_d, m_w_attn_proj, m_w_glu_v, m_w_glu_g, m_w_out, m_ffn2_norm, m_ffn2_w_gate, m_ffn2_w_up, m_ffn2_w_down, m_final_norm, v_meta_tokens, v_ffn1_norm, v_ffn1_w_gate, v_ffn1_w_up, v_ffn1_w_down, v_mix_norm, v_w_in, v_attn_sinks, v_ssm_a_re, v_ssm_a_im, v_ssm_log_dt, v_ssm_b_re, v_ssm_b_im, v_ssm_c_re, v_ssm_c_im, v_ssm_d, v_w_attn_proj, v_w_glu_v, v_w_glu_g, v_w_out, v_ffn2_norm, v_ffn2_w_gate, v_ffn2_w_up, v_ffn2_w_down, v_final_norm):
    names = ("meta_tokens", "ffn1_norm", "ffn1_w_gate", "ffn1_w_up", "ffn1_w_down", "mix_norm", "w_in", "attn_sinks",
             "ssm_a_re", "ssm_a_im", "ssm_log_dt", "ssm_b_re", "ssm_b_im", "ssm_c_re", "ssm_c_im", "ssm_d",
             "w_attn_proj", "w_glu_v", "w_glu_g", "w_out", "ffn2_norm", "ffn2_w_gate", "ffn2_w_up", "ffn2_w_down",
             "final_norm")
    weights = dict(zip(names, (meta_tokens, ffn1_norm, ffn1_w_gate, ffn1_w_up, ffn1_w_down, mix_norm, w_in, attn_sinks, ssm_a_re, ssm_a_im, ssm_log_dt, ssm_b_re, ssm_b_im, ssm_c_re, ssm_c_im, ssm_d, w_attn_proj, w_glu_v, w_glu_g, w_out, ffn2_norm, ffn2_w_gate, ffn2_w_up, ffn2_w_down, final_norm)))
    moments_m = dict(zip(names, (m_meta_tokens, m_ffn1_norm, m_ffn1_w_gate, m_ffn1_w_up, m_ffn1_w_down, m_mix_norm, m_w_in, m_attn_sinks, m_ssm_a_re, m_ssm_a_im, m_ssm_log_dt, m_ssm_b_re, m_ssm_b_im, m_ssm_c_re, m_ssm_c_im, m_ssm_d, m_w_attn_proj, m_w_glu_v, m_w_glu_g, m_w_out, m_ffn2_norm, m_ffn2_w_gate, m_ffn2_w_up, m_ffn2_w_down, m_final_norm)))
    moments_v = dict(zip(names, (v_meta_tokens, v_ffn1_norm, v_ffn1_w_gate, v_ffn1_w_up, v_ffn1_w_down, v_mix_norm, v_w_in, v_attn_sinks, v_ssm_a_re, v_ssm_a_im, v_ssm_log_dt, v_ssm_b_re, v_ssm_b_im, v_ssm_c_re, v_ssm_c_im, v_ssm_d, v_w_attn_proj, v_w_glu_v, v_w_glu_g, v_w_out, v_ffn2_norm, v_ffn2_w_gate, v_ffn2_w_up, v_ffn2_w_down, v_final_norm)))
    me = _my_index()

    gathers = {}
    token = jnp.zeros((8, 128), F32)
    for i in range(DEPTH):
        for part, ks in PARTS.items():
            shards = [_to_rows(k, weights[k][i]).astype(BF16) for k in ks]
            if (i, part) == (0, "ffn1"):
                shards.append(meta_tokens)
            ex = Exchange(shards, False, f"gather_{part}_{i}")
            state, token = ex.start(shards, token)
            gathers[i, part] = (ex, state, shards)
    all_started = token

    def get_weights(i, part, after):
        ex, state, shards = gathers[i, part]
        shards, lands = ex.wait(state, all_started if after is None else after)
        fulls = ex.place(lands, shards)
        got = dict(zip(PARTS[part], fulls))
        if (i, part) == (0, "ffn1"):
            got["meta_tokens"] = jnp.swapaxes(fulls[-1].reshape(N_DEV, N_META, 128), 0, 1).reshape(N_META, D_MODEL)
        return got

    scatters = []

    def put_grads(i, part, gdict):
        ks = list(gdict)
        srcs = [gdict[k] for k in ks]
        ex = Exchange(srcs, True, f"scatter_{part if len(ks) > 1 else ks[0]}_{i}")
        state, tok = ex.start(srcs, all_started)
        scatters.append((i, ks, ex, state))
        return tok

    small = {k: weights[k] for k in SMALL}
    loss, dx, dmeta, gs, last_started = local_step(x[0], loss_target[0], get_weights, put_grads, small)

    grads, deltas, new_m, new_v = {}, {}, {}, {}
    small_list = [loss.reshape(1), dmeta] + [gs[k] for k in SMALL]
    packed = _pack_rows(small_list, D_MODEL)
    small_ex = Exchange([packed], False, "gather_small")
    small_state, after = small_ex.start([packed], last_started)

    updated = {}
    for i, ks, ex, state in scatters:
        partials, lands = ex.wait(state, after)
        for k, partial, slots in zip(ks, partials, lands):
            g = sum_blocks(slots, partial, f"sum_{k}_{i}")
            updated[k] = adamw_layer(_to_rows(k, weights[k]), g, _to_rows(k, moments_m[k]), _to_rows(k, moments_v[k]),
                                     i, updated.get(k), f"adamw_{k}_{i}")
            after = updated[k][0]
    for k, outs in updated.items():
        grads[k], deltas[k], new_m[k], new_v[k] = [_to_rows(k, a) for a in outs]

    packed_own, packed_all = small_ex.wait(small_state, after)
    (packed_all,) = small_ex.place(packed_all, packed_own)
    total = sum_slots(packed_all.reshape(N_DEV, packed.shape[0], D_MODEL), "sum_small")
    pieces = _unpack_rows(total, [a.shape for a in small_list])
    loss_out = pieces[0][0]
    grads["meta_tokens"] = lax.dynamic_slice_in_dim(pieces[1], me * 128, 128, axis=1)
    for k, p in zip(SMALL, pieces[2:]):
        grads[k] = p
    for k in ("meta_tokens",) + SMALL:
        deltas[k], new_m[k], new_v[k] = adamw(weights[k], grads[k], moments_m[k], moments_v[k], f"adamw_{k}",
                                              minor_swap=k in ("ssm_b_re", "ssm_b_im"))
    return (loss_out, dx[None], *[grads[k] for k in names], *[deltas[k] for k in names],
            *[new_m[k] for k in names], *[new_v[k] for k in names])
```

```python
import functools
import math

import jax
import jax.numpy as jnp
from jax import lax
from jax.experimental import pallas as pl
from jax.experimental.pallas import tpu as pltpu

F32 = jnp.float32
BF16 = jnp.bfloat16

D_MODEL = 1024
DEPTH = 2
N_META = 16
HEAD_DIM = 64
N_Q_HEADS = 8
ATTN_WIDTH = 512
KV_WIDTH = 128
QKV_WIDTH = ATTN_WIDTH + 2 * KV_WIDTH
WINDOW = 128
BLK = 128
ROPE_THETA = 500000.0
ROT_DIM = 16
SSM_WIDTH = 512
SSM_GROUP = 16
SSM_GROUPS = 32
SSM_STATE = 64
STATE_WIDTH = SSM_GROUPS * SSM_STATE
D_FF = 2816
IN_WIDTH = 3328
EPS = 1e-6
NEG_INF = -1e30
PAD_FRONT = (-N_META) % BLK
N_DEV = 8

ADAM_LR = 0.001
ADAM_B1 = 0.9
ADAM_B2 = 0.999
ADAM_EPS = 1e-08
ADAM_WD = 0.01
ADAM_STEP = 10

VMEM_LIMIT = 56 * 1024 * 1024
TOKEN_TILE = 384
_VMEM = pl.BlockSpec(memory_space=pltpu.VMEM)
_SMEM = pl.BlockSpec(memory_space=pltpu.SMEM)
_ANY = pl.BlockSpec(memory_space=pl.ANY)
MESH = pl.DeviceIdType.MESH


def _params(sem=None):
    return pltpu.CompilerParams(dimension_semantics=sem, vmem_limit_bytes=VMEM_LIMIT)


def _nt(a, b):
    return lax.dot_general(a, b, (((1,), (1,)), ((), ())), preferred_element_type=F32)


def _nn(a, b):
    return jnp.dot(a, b, preferred_element_type=F32)


def _tn(a, b):
    return lax.dot_general(a, b, (((0,), (0,)), ((), ())), preferred_element_type=F32)


def _row_spec(tm, width):
    return pl.BlockSpec((tm, width), lambda i: (i, 0))


def _acc_spec(shape):
    return pl.BlockSpec(shape, lambda i: (0,) * len(shape))


def _rms_stats(x):
    r = lax.rsqrt(jnp.mean(x * x, axis=-1, keepdims=True) + EPS)
    return x * r, r


def _rms_bwd(dn, xh, r, g):
    dg = jnp.sum(dn * xh, axis=0, keepdims=True)
    dxh = dn * g
    dx = r * (dxh - xh * jnp.mean(dxh * xh, axis=-1, keepdims=True))
    return dx, dg


def ffn_fwd(h, g, wg_t, wu_t, wd, name):
    t, d = h.shape
    f = wd.shape[0]
    tm = TOKEN_TILE

    def body(h_ref, g_ref, wg_ref, wu_ref, wd_ref, ho_ref, n_ref, a_ref, b_ref):
        x = h_ref[...]
        xh, _ = _rms_stats(x)
        n = (xh * g_ref[...]).astype(BF16)
        n_ref[...] = n
        a = _nt(n, wg_ref[...])
        b = _nt(n, wu_ref[...])
        a_ref[...] = a.astype(BF16)
        b_ref[...] = b.astype(BF16)
        s = (a * jax.nn.sigmoid(a) * b).astype(BF16)
        ho_ref[...] = x + 0.5 * _nn(s, wd_ref[...])

    return pl.pallas_call(
        body, name=name, grid=(t // tm,),
        in_specs=[_row_spec(tm, d), _acc_spec((1, d)), _VMEM, _VMEM, _VMEM],
        out_specs=[_row_spec(tm, d), _row_spec(tm, d), _row_spec(tm, f), _row_spec(tm, f)],
        out_shape=[jax.ShapeDtypeStruct((t, d), F32), jax.ShapeDtypeStruct((t, d), BF16),
                   jax.ShapeDtypeStruct((t, f), BF16), jax.ShapeDtypeStruct((t, f), BF16)],
        compiler_params=_params(("arbitrary",)),
    )(h, g, wg_t, wu_t, wd)


def ffn_bwd(dh, h, g, a, b, wg_t, wu_t, wd, dep, name):
    t, d = h.shape
    f = wd.shape[0]
    tm = TOKEN_TILE // 2

    def body(dh_ref, h_ref, g_ref, a_ref, b_ref, wg_ref, wu_ref, wd_ref, dep_ref,
             dhi_ref, da_ref, db_ref, s_ref, dhb_ref, dg_ref):
        dh_t = dh_ref[...]
        dhb = (0.5 * dh_t).astype(BF16)
        dhb_ref[...] = dhb
        ds = _nt(dhb, wd_ref[...])
        av = a_ref[...].astype(F32)
        bv = b_ref[...].astype(F32)
        sig = jax.nn.sigmoid(av)
        sl = av * sig
        s_ref[...] = (sl * bv).astype(BF16)
        da = (ds * bv * (sig * (1.0 + av * (1.0 - sig)))).astype(BF16)
        db = (ds * sl).astype(BF16)
        da_ref[...] = da
        db_ref[...] = db
        dn = _nn(da, wg_ref[...]) + _nn(db, wu_ref[...])
        xh, r = _rms_stats(h_ref[...])
        dx, dg = _rms_bwd(dn, xh, r, g_ref[...])
        dhi_ref[...] = dh_t + dx

        @pl.when(pl.program_id(0) == 0)
        def _():
            dg_ref[...] = jnp.zeros_like(dg_ref)

        dg_ref[...] += dg

    return pl.pallas_call(
        body, name=name, grid=(t // tm,),
        in_specs=[_row_spec(tm, d), _row_spec(tm, d), _acc_spec((1, d)), _row_spec(tm, f), _row_spec(tm, f),
                  _VMEM, _VMEM, _VMEM, _ANY],
        out_specs=[_row_spec(tm, d), _row_spec(tm, f), _row_spec(tm, f), _row_spec(tm, f), _row_spec(tm, d),
                   _acc_spec((1, d))],
        out_shape=[jax.ShapeDtypeStruct((t, d), F32), jax.ShapeDtypeStruct((t, f), BF16),
                   jax.ShapeDtypeStruct((t, f), BF16), jax.ShapeDtypeStruct((t, f), BF16),
                   jax.ShapeDtypeStruct((t, d), BF16), jax.ShapeDtypeStruct((1, d), F32)],
        compiler_params=_params(("arbitrary",)),
    )(dh, h, g, a, b, wg_t, wu_t, wd, dep)


DW_TILE = 256


def tn_matmul(x, y, name, dep=None):
    xs = list(x) if isinstance(x, (list, tuple)) else [x]
    t = xs[0].shape[0]
    n = y.shape[1]
    bm = DW_TILE
    tiles = [a.shape[1] // bm for a in xs]
    offs = [sum(tiles[:k]) for k in range(len(xs))]
    deps = [] if dep is None else [dep]

    def body(*refs):
        y_ref, o_ref = refs[len(xs)], refs[-1]
        i = pl.program_id(0)
        for k in range(len(xs)):
            @pl.when((i >= offs[k]) & (i < offs[k] + tiles[k]))
            def _(k=k):
                o_ref[...] = _tn(refs[k][...], y_ref[...]).astype(BF16)

    def x_spec(k):
        return pl.BlockSpec((t, bm), lambda i: (0, jnp.clip(i - offs[k], 0, tiles[k] - 1)))

    return pl.pallas_call(
        body, name=name, grid=(sum(tiles),),
        in_specs=[x_spec(k) for k in range(len(xs))] + [_VMEM] + [_ANY] * len(deps),
        out_specs=pl.BlockSpec((bm, n), lambda i: (i, 0)),
        out_shape=jax.ShapeDtypeStruct((sum(tiles) * bm, n), BF16),
        compiler_params=_params(("arbitrary",)),
    )(*xs, y, *deps)


def head_fwd_bwd(h, g, tgt):
    t, d = h.shape

    def body(h_ref, g_ref, t_ref, loss_ref, dh_ref, dg_ref):
        i = pl.program_id(0)
        xh, r = _rms_stats(h_ref[...])
        gv = g_ref[...]
        valid = (i > 0).astype(F32)
        e = (xh * gv - t_ref[...]) * valid
        dx, dg = _rms_bwd(e * (1.0 / d), xh, r, gv)
        dh_ref[...] = dx

        @pl.when(i == 0)
        def _():
            dg_ref[...] = jnp.zeros_like(dg_ref)
            loss_ref[...] = jnp.zeros_like(loss_ref)

        dg_ref[...] += dg
        loss_ref[...] += jnp.sum(e * e) * (0.5 / d)

    return pl.pallas_call(
        body, name="head", grid=(t // BLK,),
        in_specs=[_row_spec(BLK, d), _acc_spec((1, d)),
                  pl.BlockSpec((BLK, d), lambda i: (jnp.maximum(i - 1, 0), 0))],
        out_specs=[_acc_spec((1, 128)), _row_spec(BLK, d), _acc_spec((1, d))],
        out_shape=[jax.ShapeDtypeStruct((1, 128), F32), jax.ShapeDtypeStruct((t, d), F32),
                   jax.ShapeDtypeStruct((1, d), F32)],
        compiler_params=_params(("arbitrary",)),
    )(h, g, tgt)


def rope_tables(t):
    pos = jnp.arange(t, dtype=F32) - PAD_FRONT
    inv_freq = ROPE_THETA ** (-jnp.arange(0, ROT_DIM, 2, dtype=F32) / ROT_DIM)
    ang = pos[:, None] * inv_freq[None, :]
    cos, sin = jnp.cos(ang), jnp.sin(ang)
    ones = jnp.ones((t, HEAD_DIM - ROT_DIM), F32)
    cos_h = jnp.concatenate([cos, cos, ones], axis=1)
    sin_h = jnp.concatenate([-sin, sin, 0.0 * ones], axis=1)
    return jnp.concatenate([cos_h, cos_h], axis=1), jnp.concatenate([sin_h, sin_h], axis=1)


def _swap_halves(x):
    n = x.shape[1]
    lane = lax.broadcasted_iota(jnp.int32, x.shape, 1)
    return jnp.where(lane % HEAD_DIM < ROT_DIM // 2, pltpu.roll(x, n - ROT_DIM // 2, 1), pltpu.roll(x, ROT_DIM // 2, 1))


def _rope(x, cos_t, sin_t, sign):
    return x * cos_t + sign * (_swap_halves(x) * sin_t)


def win_fwd(h, g, win_t, cos_t, sin_t, name):
    t, d = h.shape
    tm = TOKEN_TILE

    def body(h_ref, g_ref, w_ref, c_ref, s_ref, n_ref, qkv_ref, u_ref, gates_ref):
        xh, _ = _rms_stats(h_ref[...])
        n = (xh * g_ref[...]).astype(BF16)
        n_ref[...] = n
        z = _nt(n, w_ref[...])
        c, s = c_ref[...], s_ref[...]
        for j in range((ATTN_WIDTH + KV_WIDTH) // 128):
            qkv_ref[:, j * 128:(j + 1) * 128] = _rope(z[:, j * 128:(j + 1) * 128], c, s, 1.0).astype(BF16)
        qkv_ref[:, ATTN_WIDTH + KV_WIDTH:QKV_WIDTH] = z[:, ATTN_WIDTH + KV_WIDTH:QKV_WIDTH].astype(BF16)
        u_ref[...] = z[:, QKV_WIDTH:QKV_WIDTH + SSM_WIDTH]
        gates_ref[...] = z[:, QKV_WIDTH + SSM_WIDTH:]

    return pl.pallas_call(
        body, name=name, grid=(t // tm,),
        in_specs=[_row_spec(tm, d), _acc_spec((1, d)), _VMEM, _row_spec(tm, 128), _row_spec(tm, 128)],
        out_specs=[_row_spec(tm, d), _row_spec(tm, QKV_WIDTH), _row_spec(tm, SSM_WIDTH), _row_spec(tm, 2 * d)],
        out_shape=[jax.ShapeDtypeStruct((t, d), BF16), jax.ShapeDtypeStruct((t, QKV_WIDTH), BF16),
                   jax.ShapeDtypeStruct((t, SSM_WIDTH), F32), jax.ShapeDtypeStruct((t, 2 * d), F32)],
        compiler_params=_params(("arbitrary",)),
    )(h, g, win_t, cos_t, sin_t)


def win_bwd(dh, h, g, dqkv, du, dgates, win_t, dep, name):
    t, d = h.shape
    tm = TOKEN_TILE

    def body(dh_ref, h_ref, g_ref, dqkv_ref, du_ref, dgt_ref, w_ref, dep_ref, dhi_ref, dg_ref):
        dn = (_nn(dqkv_ref[...], w_ref[0:QKV_WIDTH, :])
              + _nn(du_ref[...], w_ref[QKV_WIDTH:QKV_WIDTH + SSM_WIDTH, :])
              + _nn(dgt_ref[...], w_ref[QKV_WIDTH + SSM_WIDTH:, :]))
        xh, r = _rms_stats(h_ref[...])
        dx, dg = _rms_bwd(dn, xh, r, g_ref[...])
        dhi_ref[...] = dh_ref[...] + dx

        @pl.when(pl.program_id(0) == 0)
        def _():
            dg_ref[...] = jnp.zeros_like(dg_ref)

        dg_ref[...] += dg

    return pl.pallas_call(
        body, name=name, grid=(t // tm,),
        in_specs=[_row_spec(tm, d), _row_spec(tm, d), _acc_spec((1, d)), _row_spec(tm, QKV_WIDTH),
                  _row_spec(tm, SSM_WIDTH), _row_spec(tm, 2 * d), _VMEM, _ANY],
        out_specs=[_row_spec(tm, d), _acc_spec((1, d))],
        out_shape=[jax.ShapeDtypeStruct((t, d), F32), jax.ShapeDtypeStruct((1, d), F32)],
        compiler_params=_params(("arbitrary",)),
    )(dh, h, g, dqkv, du, dgates, win_t, dep)


def _attn_mask(blk):
    q_pos = blk * BLK + lax.broadcasted_iota(jnp.int32, (BLK, 3 * BLK), 0) - PAD_FRONT
    col = lax.broadcasted_iota(jnp.int32, (BLK, 3 * BLK), 1)
    part = col // BLK
    k_pos = jnp.where(part == 0, col, (blk + part - 2) * BLK + (col - part * BLK)) - PAD_FRONT
    dist = q_pos - k_pos
    meta_ok = (part == 0) & (k_pos >= 0) & (dist >= 0)
    band_ok = (part > 0) & (k_pos >= N_META) & (dist >= 0) & (dist < WINDOW)
    return meta_ok | band_ok


def _head_halves(x128, kv):
    x = x128.astype(F32)
    lane = lax.broadcasted_iota(jnp.int32, x.shape, 1)
    swapped = pltpu.roll(x, HEAD_DIM, 1)
    lo, hi = (x, swapped) if kv == 0 else (swapped, x)
    return jnp.where(lane < HEAD_DIM, lo, 0.0).astype(BF16), jnp.where(lane >= HEAD_DIM, hi, 0.0).astype(BF16)


def _gather_keys(meta_ref, prev_ref, cur_ref, lo):
    return jnp.concatenate([meta_ref[:, lo:lo + 128], prev_ref[:, lo:lo + 128], cur_ref[:, lo:lo + 128]], axis=0)


def _pair_lanes(kv):
    return slice(2 * kv * 128, (2 * kv + 1) * 128), slice((2 * kv + 1) * 128, (2 * kv + 2) * 128)


def _stacked_sinks(sink_ref, head):
    row = lax.broadcasted_iota(jnp.int32, (2 * BLK, 1), 0)
    return jnp.where(row < BLK, sink_ref[0, head], sink_ref[0, head + 2])


def _softmax_with_sink(s, mask, sink):
    s = jnp.where(mask, s * (HEAD_DIM ** -0.5), NEG_INF)
    m = jnp.maximum(jnp.max(s, axis=-1, keepdims=True), sink)
    p = jnp.exp(s - m)
    p_sink = jnp.exp(sink - m)
    inv = 1.0 / (jnp.sum(p, axis=-1, keepdims=True) + p_sink)
    return p * inv, p_sink * inv


def attn_fwd(qkv, sinks, name):
    t = qkv.shape[0]
    nb = t // BLK

    def body(sink_ref, meta_ref, prev_ref, cur_ref, o_ref):
        blk = pl.program_id(0)
        mask = _attn_mask(blk)
        mask2 = jnp.concatenate([mask, mask], axis=0)
        k128 = _gather_keys(meta_ref, prev_ref, cur_ref, ATTN_WIDTH)
        v128 = _gather_keys(meta_ref, prev_ref, cur_ref, ATTN_WIDTH + KV_WIDTH)
        for kv in range(2):
            k_lo, k_hi = _head_halves(k128, kv)
            v_lo, v_hi = _head_halves(v128, kv)
            lanes0, lanes1 = _pair_lanes(kv)
            q2 = jnp.concatenate([cur_ref[:, lanes0], cur_ref[:, lanes1]], axis=0)
            p_a, _ = _softmax_with_sink(_nt(q2, k_lo), mask2, _stacked_sinks(sink_ref, 4 * kv))
            p_b, _ = _softmax_with_sink(_nt(q2, k_hi), mask2, _stacked_sinks(sink_ref, 4 * kv + 1))
            o2 = (_nn(p_a.astype(BF16), v_lo) + _nn(p_b.astype(BF16), v_hi)).astype(BF16)
            o_ref[:, lanes0] = o2[0:BLK]
            o_ref[:, lanes1] = o2[BLK:2 * BLK]

    blk_spec = lambda f: pl.BlockSpec((BLK, QKV_WIDTH), f)
    return pl.pallas_call(
        body, name=name, grid=(nb,),
        in_specs=[_SMEM, blk_spec(lambda i: (0, 0)), blk_spec(lambda i: (jnp.maximum(i - 1, 0), 0)),
                  blk_spec(lambda i: (i, 0))],
        out_specs=_row_spec(BLK, ATTN_WIDTH),
        out_shape=jax.ShapeDtypeStruct((t, ATTN_WIDTH), BF16),
        compiler_params=_params(("arbitrary",)),
    )(sinks, qkv, qkv, qkv)


def attn_bwd(qkv, do, sinks, cos_t, sin_t, name):
    t = qkv.shape[0]
    nb = t // BLK

    def body(sink_ref, meta_ref, prev_ref, cur_ref, do_ref, c_ref, s_ref, dqkv_ref, dsink_ref, carry_ref, macc_ref):
        step = pl.program_id(0)
        blk = nb - 1 - step

        @pl.when(step == 0)
        def _():
            dsink_ref[...] = jnp.zeros_like(dsink_ref)
            carry_ref[...] = jnp.zeros_like(carry_ref)
            macc_ref[...] = jnp.zeros_like(macc_ref)

        mask = _attn_mask(blk)
        mask2 = jnp.concatenate([mask, mask], axis=0)
        lane = lax.broadcasted_iota(jnp.int32, (3 * BLK, 128), 1)
        k128 = _gather_keys(meta_ref, prev_ref, cur_ref, ATTN_WIDTH)
        v128 = _gather_keys(meta_ref, prev_ref, cur_ref, ATTN_WIDTH + KV_WIDTH)
        cos_b, sin_b = c_ref[...], s_ref[...]
        dk_heads, dv_heads = [], []
        for kv in range(2):
            k_lo, k_hi = _head_halves(k128, kv)
            v_lo, v_hi = _head_halves(v128, kv)
            lanes0, lanes1 = _pair_lanes(kv)
            q2 = jnp.concatenate([cur_ref[:, lanes0], cur_ref[:, lanes1]], axis=0)
            do2 = jnp.concatenate([do_ref[:, lanes0], do_ref[:, lanes1]], axis=0)
            ds_half, p_half = [], []
            for half, (k_h, v_h) in enumerate(((k_lo, v_lo), (k_hi, v_hi))):
                head = 4 * kv + half
                p, p_sink = _softmax_with_sink(_nt(q2, k_h), mask2, _stacked_sinks(sink_ref, head))
                dp = _nt(do2, v_h)
                dsum = jnp.sum(p * dp, axis=-1, keepdims=True)
                ds_half.append((p * (dp - dsum) * (HEAD_DIM ** -0.5)).astype(BF16))
                p_half.append(p.astype(BF16))
                dsink = p_sink * dsum
                for part, h in ((0, head), (1, head + 2)):
                    total = -jnp.sum(dsink[part * BLK:(part + 1) * BLK], axis=0, keepdims=True)
                    dsink_ref[h:h + 1, :] += jnp.broadcast_to(total, (1, 128))
            dq2 = _nn(ds_half[0], k_lo) + _nn(ds_half[1], k_hi)
            dqkv_ref[:, lanes0] = _rope(dq2[0:BLK], cos_b, sin_b, -1.0).astype(BF16)
            dqkv_ref[:, lanes1] = _rope(dq2[BLK:2 * BLK], cos_b, sin_b, -1.0).astype(BF16)
            dk_acc = jnp.where(lane < HEAD_DIM, _tn(ds_half[0], q2), _tn(ds_half[1], q2))
            dv_acc = jnp.where(lane < HEAD_DIM, _tn(p_half[0], do2), _tn(p_half[1], do2))
            dk_heads.append(dk_acc + pltpu.roll(dk_acc, HEAD_DIM, 1))
            dv_heads.append(dv_acc + pltpu.roll(dv_acc, HEAD_DIM, 1))
        dkv = jnp.concatenate([jnp.where(lane < HEAD_DIM, dk_heads[0], dk_heads[1]),
                               jnp.where(lane < HEAD_DIM, dv_heads[0], dv_heads[1])], axis=1)
        macc_ref[...] += dkv[0:BLK]
        is_last = (blk == 0).astype(F32)
        mine = dkv[2 * BLK:3 * BLK] + carry_ref[...] + is_last * macc_ref[...]
        carry_ref[...] = dkv[BLK:2 * BLK]
        dqkv_ref[:, ATTN_WIDTH:ATTN_WIDTH + KV_WIDTH] = _rope(mine[:, 0:128], cos_b, sin_b, -1.0).astype(BF16)
        dqkv_ref[:, ATTN_WIDTH + KV_WIDTH:QKV_WIDTH] = mine[:, 128:256].astype(BF16)

    rev = lambda i: nb - 1 - i
    blk_spec = lambda f: pl.BlockSpec((BLK, QKV_WIDTH), f)
    return pl.pallas_call(
        body, name=name, grid=(nb,),
        in_specs=[_SMEM, blk_spec(lambda i: (0, 0)), blk_spec(lambda i: (jnp.maximum(rev(i) - 1, 0), 0)),
                  blk_spec(lambda i: (rev(i), 0)), pl.BlockSpec((BLK, ATTN_WIDTH), lambda i: (rev(i), 0)),
                  pl.BlockSpec((BLK, 128), lambda i: (rev(i), 0)), pl.BlockSpec((BLK, 128), lambda i: (rev(i), 0))],
        out_specs=[pl.BlockSpec((BLK, QKV_WIDTH), lambda i: (rev(i), 0)), _acc_spec((N_Q_HEADS, 128))],
        out_shape=[jax.ShapeDtypeStruct((t, QKV_WIDTH), BF16), jax.ShapeDtypeStruct((N_Q_HEADS, 128), F32)],
        scratch_shapes=[pltpu.VMEM((BLK, 256), F32), pltpu.VMEM((BLK, 256), F32)],
        compiler_params=_params(("arbitrary",)),
    )(sinks, qkv, qkv, qkv, do, cos_t, sin_t)


def _cmul(ar, ai, br, bi):
    return ar * br - ai * bi, ar * bi + ai * br


def ssm_prep(a_re, a_im, log_dt, b_re_t, b_im_t, name):
    def body(ar_ref, ai_ref, ldt_ref, br_ref, bi_ref, lr_ref, li_ref, bbr_ref, bbi_ref):
        ar, ai = ar_ref[...], ai_ref[...]
        dt = jnp.exp(ldt_ref[...])
        mag = jnp.exp(ar * dt)
        lr = mag * jnp.cos(ai * dt)
        li = mag * jnp.sin(ai * dt)
        den = ar * ar + ai * ai
        nr = lr - 1.0
        cr = ((nr * ar + li * ai) / den)[:, None, :]
        ci = ((li * ar - nr * ai) / den)[:, None, :]
        br, bi = br_ref[...], bi_ref[...]
        lr_ref[...] = lr
        li_ref[...] = li
        bbr_ref[...] = cr * br - ci * bi
        bbi_ref[...] = cr * bi + ci * br

    gp = jax.ShapeDtypeStruct(a_re.shape, F32)
    gcp = jax.ShapeDtypeStruct(b_re_t.shape, F32)
    return pl.pallas_call(body, name=name, out_shape=[gp, gp, gcp, gcp],
                          in_specs=[_VMEM] * 5, out_specs=[_VMEM] * 4)(a_re, a_im, log_dt, b_re_t, b_im_t)


def ssm_prep_bwd(a_re, a_im, log_dt, b_re_t, b_im_t, dl_re, dl_im, dbb_re, dbb_im, name):
    def body(ar_ref, ai_ref, ldt_ref, br_ref, bi_ref, dlr_ref, dli_ref, dbbr_ref, dbbi_ref,
             dar_ref, dai_ref, dldt_ref, dbr_ref, dbi_ref):
        ar, ai = ar_ref[...], ai_ref[...]
        dt = jnp.exp(ldt_ref[...])
        mag = jnp.exp(ar * dt)
        lr = mag * jnp.cos(ai * dt)
        li = mag * jnp.sin(ai * dt)
        den = ar * ar + ai * ai
        nr = lr - 1.0
        cr = (nr * ar + li * ai) / den
        ci = (li * ar - nr * ai) / den
        br, bi = br_ref[...], bi_ref[...]
        dbbr, dbbi = dbbr_ref[...], dbbi_ref[...]
        dbr_ref[...] = cr[:, None, :] * dbbr + ci[:, None, :] * dbbi
        dbi_ref[...] = cr[:, None, :] * dbbi - ci[:, None, :] * dbbr
        dcr = jnp.sum(br * dbbr + bi * dbbi, axis=1)
        dci = jnp.sum(br * dbbi - bi * dbbr, axis=1)
        d_num_r = dcr / den
        d_num_i = dci / den
        d_den = -(dcr * cr + dci * ci) / den
        d_lr = dlr_ref[...] + d_num_r * ar - d_num_i * ai
        d_li = dli_ref[...] + d_num_r * ai + d_num_i * ar
        d_ar = d_num_r * nr + d_num_i * li + d_den * 2.0 * ar
        d_ai = d_num_r * li - d_num_i * nr + d_den * 2.0 * ai
        d_mag = (d_lr * lr + d_li * li) / mag
        d_theta = d_li * lr - d_lr * li
        d_ardt = d_mag * mag
        dar_ref[...] = d_ar + d_ardt * dt
        dai_ref[...] = d_ai + d_theta * dt
        d_dt = jnp.sum(d_ardt * ar + d_theta * ai, axis=1, keepdims=True)
        dldt_ref[...] = d_dt * dt

    gp = jax.ShapeDtypeStruct(a_re.shape, F32)
    gcp = jax.ShapeDtypeStruct(b_re_t.shape, F32)
    return pl.pallas_call(body, name=name, out_shape=[gp, gp, jax.ShapeDtypeStruct(log_dt.shape, F32), gcp, gcp],
                          in_specs=[_VMEM] * 9, out_specs=[_VMEM] * 5,
                          )(a_re, a_im, log_dt, b_re_t, b_im_t, dl_re, dl_im, dbb_re, dbb_im)


N_CHUNK = 4
U_CHUNK = SSM_WIDTH // N_CHUNK
H_CHUNK = STATE_WIDTH // N_CHUNK
SUB = 8


def _block_diag_b(bb):
    x = bb.reshape(N_CHUNK, 8, SSM_GROUP, 1, SSM_STATE)
    same = (jnp.arange(8)[:, None] == jnp.arange(8)[None, :])[None, :, None, :, None]
    return jnp.where(same, x, 0.0).reshape(N_CHUNK, U_CHUNK, H_CHUNK)


def _block_diag_c(c):
    x = jnp.swapaxes(c.reshape(N_CHUNK, 8, SSM_GROUP, SSM_STATE), 2, 3)[:, :, :, None, :]
    same = (jnp.arange(8)[:, None] == jnp.arange(8)[None, :])[None, :, None, :, None]
    return jnp.where(same, x, 0.0).reshape(N_CHUNK, H_CHUNK, U_CHUNK)


def _diag_of_b(m):
    x = m.reshape(N_CHUNK, 8, SSM_GROUP, 8, SSM_STATE)
    return jnp.stack([x[:, g, :, g, :] for g in range(8)], axis=1).reshape(SSM_GROUPS, SSM_GROUP, SSM_STATE)


def _diag_of_c(m):
    x = m.reshape(N_CHUNK, 8, SSM_STATE, 8, SSM_GROUP)
    d = jnp.stack([x[:, g, :, g, :] for g in range(8)], axis=1)
    return jnp.swapaxes(d, 2, 3).reshape(SSM_GROUPS, SSM_GROUP, SSM_STATE)


def _lambda_tables(lr, li, reverse):
    p1 = (lr, li)
    p2 = _cmul(*p1, *p1)
    p4 = _cmul(*p2, *p2)
    rows = [p1]
    for _ in range(SUB - 1):
        rows.append(_cmul(*rows[-1], *p1))
    if reverse:
        rows = rows[::-1]
    return p1, p2, p4, (jnp.concatenate([r[0] for r in rows], axis=0), jnp.concatenate([r[1] for r in rows], axis=0))


def _scan8(xr, xi, pows, table, cr, ci, reverse):
    row = lax.broadcasted_iota(jnp.int32, xr.shape, 0)
    for d, (pr, pi) in zip((1, 2, 4), pows):
        if reverse:
            sr, si = pltpu.roll(xr, SUB - d, 0), pltpu.roll(xi, SUB - d, 0)
            keep = row < SUB - d
        else:
            sr, si = pltpu.roll(xr, d, 0), pltpu.roll(xi, d, 0)
            keep = row >= d
        sr = jnp.where(keep, sr, 0.0)
        si = jnp.where(keep, si, 0.0)
        xr, xi = xr + pr * sr - pi * si, xi + pr * si + pi * sr
    tr, ti = table
    return xr + tr * cr - ti * ci, xi + tr * ci + ti * cr


def _gelu_and_grad(y):
    k0 = math.sqrt(2.0 / math.pi)
    inner = k0 * (y + 0.044715 * y * y * y)
    th = jnp.tanh(inner)
    g = 0.5 * y * (1.0 + th)
    dg = 0.5 * (1.0 + th) + 0.5 * y * (1.0 - th * th) * k0 * (1.0 + 3.0 * 0.044715 * y * y)
    return g, dg


SCAN_TILE = TOKEN_TILE
SEG = SCAN_TILE // SUB
SCAN_LANES = 512


def _perm_matrix(to_segments):
    a = lax.broadcasted_iota(jnp.int32, (SCAN_TILE, SCAN_TILE), 0)
    b = lax.broadcasted_iota(jnp.int32, (SCAN_TILE, SCAN_TILE), 1)
    rho, time = (a, b) if to_segments else (b, a)
    return (time == (rho % SUB) * SEG + rho // SUB).astype(BF16)


def _permute_f32(p, x):
    hi = x.astype(BF16)
    r1 = x - hi.astype(F32)
    mid = r1.astype(BF16)
    lo = (r1 - mid.astype(F32)).astype(BF16)
    return _nn(p, hi) + _nn(p, mid) + _nn(p, lo)


def _power_table(lr, li, pr_ref, pi_ref):
    cur = (lr, li)
    for r in range(SEG):
        pr_ref[r:r + 1, :] = cur[0]
        pi_ref[r:r + 1, :] = cur[1]
        cur = _cmul(*cur, lr, li)


def _segment_scan(xr_ref, xi_ref, lanes, lam, table_row, cr_ref, ci_ref, reverse, extra=None):
    lr, li = lam
    row = lax.broadcasted_iota(jnp.int32, (SUB, SCAN_LANES), 0)

    def rows_of(k):
        r = SEG - 1 - k if reverse else k
        return pl.ds(pl.multiple_of(r * SUB, SUB), SUB)

    def first(k, st):
        sr, si = st
        rows = rows_of(k)
        nr = lr * sr - li * si + xr_ref[rows, lanes]
        ni = lr * si + li * sr + xi_ref[rows, lanes]
        xr_ref[rows, lanes] = nr
        xi_ref[rows, lanes] = ni
        return nr, ni

    zero = jnp.zeros((SUB, SCAN_LANES), F32)
    er, ei = lax.fori_loop(0, SEG, first, (zero, zero))
    l16 = table_row(SEG - 1)
    q1, q2, q4, tab = _lambda_tables(l16[0], l16[1], reverse)
    c_r, c_i = cr_ref[:, lanes], ci_ref[:, lanes]
    gr, gi = _scan8(er, ei, (q1, q2, q4), tab, c_r, c_i, reverse)
    if reverse:
        cin_r = jnp.where(row == SUB - 1, c_r, pltpu.roll(gr, SUB - 1, 0))
        cin_i = jnp.where(row == SUB - 1, c_i, pltpu.roll(gi, SUB - 1, 0))
        cr_ref[:, lanes] = gr[0:1]
        ci_ref[:, lanes] = gi[0:1]
    else:
        cin_r = jnp.where(row == 0, c_r, pltpu.roll(gr, 1, 0))
        cin_i = jnp.where(row == 0, c_i, pltpu.roll(gi, 1, 0))
        cr_ref[:, lanes] = gr[SUB - 1:SUB]
        ci_ref[:, lanes] = gi[SUB - 1:SUB]

    def second(k, carry):
        rows = rows_of(k)
        tr, ti = table_row(k)
        ar = xr_ref[rows, lanes] + tr * cin_r - ti * cin_i
        ai = xi_ref[rows, lanes] + tr * cin_i + ti * cin_r
        xr_ref[rows, lanes] = ar
        xi_ref[rows, lanes] = ai
        if extra is None:
            return carry
        return extra(rows, carry, ar, ai)

    init = 0 if extra is None else (cin_r, cin_i, zero, zero)
    return lax.fori_loop(0, SEG, second, init)


def ssm_fwd(u, lam_re, lam_im, bb_re, bb_im, cc_re, cc_im, d_skip, name):
    t = u.shape[0]
    tt = SCAN_TILE

    def body(u_ref, lr_ref, li_ref, bbr_ref, bbi_ref, ccr_ref, cci_ref, d_ref, yg_ref, hr_ref, hi_ref,
             cr_ref, ci_ref, pr_ref, pi_ref, up_ref, y_ref):
        @pl.when(pl.program_id(0) == 0)
        def _():
            cr_ref[...] = jnp.zeros_like(cr_ref)
            ci_ref[...] = jnp.zeros_like(ci_ref)
            _power_table(lr_ref[...], li_ref[...], pr_ref, pi_ref)

        up_ref[...] = _permute_f32(_perm_matrix(True), u_ref[...])
        ub = up_ref[...].astype(BF16)
        for j in range(N_CHUNK):
            hs = slice(j * H_CHUNK, (j + 1) * H_CHUNK)
            us = slice(j * U_CHUNK, (j + 1) * U_CHUNK)
            hr_ref[:, hs] = _nn(ub[:, us], bbr_ref[j])
            hi_ref[:, hs] = _nn(ub[:, us], bbi_ref[j])
        for c in range(STATE_WIDTH // SCAN_LANES):
            lanes = slice(c * SCAN_LANES, (c + 1) * SCAN_LANES)
            _segment_scan(hr_ref, hi_ref, lanes, (lr_ref[:, lanes], li_ref[:, lanes]),
                          lambda k, lanes=lanes: (pr_ref[pl.ds(k, 1), lanes], pi_ref[pl.ds(k, 1), lanes]),
                          cr_ref, ci_ref, False)
        for j in range(N_CHUNK):
            hs = slice(j * H_CHUNK, (j + 1) * H_CHUNK)
            us = slice(j * U_CHUNK, (j + 1) * U_CHUNK)
            y = (_nn(hr_ref[:, hs].astype(BF16), ccr_ref[j]) - _nn(hi_ref[:, hs].astype(BF16), cci_ref[j])
                 + d_ref[:, us] * up_ref[:, us])
            y_ref[:, us] = _gelu_and_grad(y)[0]
        yg_ref[...] = _nn(_perm_matrix(False), y_ref[...].astype(BF16)).astype(BF16)

    return pl.pallas_call(
        body, name=name, grid=(t // tt,),
        in_specs=[_row_spec(tt, SSM_WIDTH), _VMEM, _VMEM, _VMEM, _VMEM, _VMEM, _VMEM, _VMEM],
        out_specs=[_row_spec(tt, SSM_WIDTH), _row_spec(tt, STATE_WIDTH), _row_spec(tt, STATE_WIDTH)],
        out_shape=[jax.ShapeDtypeStruct((t, SSM_WIDTH), BF16), jax.ShapeDtypeStruct((t, STATE_WIDTH), F32),
                   jax.ShapeDtypeStruct((t, STATE_WIDTH), F32)],
        scratch_shapes=[pltpu.VMEM((1, STATE_WIDTH), F32), pltpu.VMEM((1, STATE_WIDTH), F32),
                        pltpu.VMEM((SEG, STATE_WIDTH), F32), pltpu.VMEM((SEG, STATE_WIDTH), F32),
                        pltpu.VMEM((tt, SSM_WIDTH), F32), pltpu.VMEM((tt, SSM_WIDTH), F32)],
        compiler_params=_params(("arbitrary",)),
    )(u, lam_re, lam_im, bb_re, bb_im, cc_re, cc_im, d_skip)


def ssm_bwd(dyg, u, h_re, h_im, lam_re, lam_im, bb_re, bb_im, cc_re, cc_im, d_skip, name):
    t = u.shape[0]
    tt = SCAN_TILE
    nt = t // tt

    def body(dyg_ref, u_ref, hr_ref, hi_ref, lr_ref, li_ref, bbr_ref, bbi_ref, ccr_ref, cci_ref, d_ref,
             du_ref, dlr_ref, dli_ref, dbbr_ref, dbbi_ref, dccr_ref, dcci_ref, dd_ref,
             ar_ref, ai_ref, cr_ref, ci_ref, pr_ref, pi_ref, up_ref, dy_ref, dup_ref):
        step = pl.program_id(0)
        tile = nt - 1 - step

        @pl.when(step == 0)
        def _():
            for ref in (cr_ref, ci_ref, dlr_ref, dli_ref, dbbr_ref, dbbi_ref, dccr_ref, dcci_ref, dd_ref):
                ref[...] = jnp.zeros_like(ref)
            _power_table(lr_ref[...], li_ref[...], pr_ref, pi_ref)

        to_segments = _perm_matrix(True)
        up_ref[...] = _permute_f32(to_segments, u_ref[...])
        dy_ref[...] = _permute_f32(to_segments, dyg_ref[...])
        uv = up_ref[...]
        ub = uv.astype(BF16)
        dskip = d_ref[...]
        for j in range(N_CHUNK):
            hs = slice(j * H_CHUNK, (j + 1) * H_CHUNK)
            us = slice(j * U_CHUNK, (j + 1) * U_CHUNK)
            hrb = hr_ref[:, hs].astype(BF16)
            hib = hi_ref[:, hs].astype(BF16)
            y = _nn(hrb, ccr_ref[j]) - _nn(hib, cci_ref[j]) + dskip[:, us] * uv[:, us]
            dy = dy_ref[:, us] * _gelu_and_grad(y)[1]
            dy_ref[:, us] = dy
            dyb = dy.astype(BF16)
            dccr_ref[j] += _tn(hrb, dyb)
            dcci_ref[j] -= _tn(hib, dyb)
            ar_ref[:, hs] = _nt(dyb, ccr_ref[j])
            ai_ref[:, hs] = -_nt(dyb, cci_ref[j])
        dd_ref[...] += jnp.sum(dy_ref[...] * uv, axis=0, keepdims=True)

        for c in range(STATE_WIDTH // SCAN_LANES):
            lanes = slice(c * SCAN_LANES, (c + 1) * SCAN_LANES)

            def dlambda(rows, carry, ar, ai, lanes=lanes):
                nr, ni, accr, acci = carry
                hr, hi = hr_ref[rows, lanes], hi_ref[rows, lanes]
                return ar, ai, accr + nr * hr + ni * hi, acci + ni * hr - nr * hi

            _, _, accr, acci = _segment_scan(
                ar_ref, ai_ref, lanes, (lr_ref[:, lanes], -li_ref[:, lanes]),
                lambda k, lanes=lanes: (pr_ref[pl.ds(k, 1), lanes], -pi_ref[pl.ds(k, 1), lanes]),
                cr_ref, ci_ref, True, dlambda)
            dlr_ref[:, lanes] += accr
            dli_ref[:, lanes] += acci

        rho = lax.broadcasted_iota(jnp.int32, (tt, U_CHUNK), 0)
        time = tile * tt + (rho % SUB) * SEG + rho // SUB
        for j in range(N_CHUNK):
            hs = slice(j * H_CHUNK, (j + 1) * H_CHUNK)
            us = slice(j * U_CHUNK, (j + 1) * U_CHUNK)
            arb = ar_ref[:, hs].astype(BF16)
            aib = ai_ref[:, hs].astype(BF16)
            dbbr_ref[j] += _tn(ub[:, us], arb)
            dbbi_ref[j] += _tn(ub[:, us], aib)
            du = _nt(arb, bbr_ref[j]) + _nt(aib, bbi_ref[j]) + dy_ref[:, us] * dskip[:, us]
            dup_ref[:, us] = jnp.where(time >= PAD_FRONT, du, 0.0)
        du_ref[...] = _nn(_perm_matrix(False), dup_ref[...].astype(BF16)).astype(BF16)

    rev = lambda i: (nt - 1 - i, 0)
    full = lambda shape: pl.BlockSpec(shape, lambda i: (0,) * len(shape))
    return pl.pallas_call(
        body, name=name, grid=(nt,),
        in_specs=[pl.BlockSpec((tt, SSM_WIDTH), rev), pl.BlockSpec((tt, SSM_WIDTH), rev),
                  pl.BlockSpec((tt, STATE_WIDTH), rev), pl.BlockSpec((tt, STATE_WIDTH), rev),
                  _VMEM, _VMEM, _VMEM, _VMEM, _VMEM, _VMEM, _VMEM],
        out_specs=[pl.BlockSpec((tt, SSM_WIDTH), rev), full((SUB, STATE_WIDTH)), full((SUB, STATE_WIDTH)),
                   full((N_CHUNK, U_CHUNK, H_CHUNK)), full((N_CHUNK, U_CHUNK, H_CHUNK)),
                   full((N_CHUNK, H_CHUNK, U_CHUNK)), full((N_CHUNK, H_CHUNK, U_CHUNK)), full((1, SSM_WIDTH))],
        out_shape=[jax.ShapeDtypeStruct((t, SSM_WIDTH), BF16),
                   jax.ShapeDtypeStruct((SUB, STATE_WIDTH), F32), jax.ShapeDtypeStruct((SUB, STATE_WIDTH), F32),
                   jax.ShapeDtypeStruct((N_CHUNK, U_CHUNK, H_CHUNK), F32),
                   jax.ShapeDtypeStruct((N_CHUNK, U_CHUNK, H_CHUNK), F32),
                   jax.ShapeDtypeStruct((N_CHUNK, H_CHUNK, U_CHUNK), F32),
                   jax.ShapeDtypeStruct((N_CHUNK, H_CHUNK, U_CHUNK), F32),
                   jax.ShapeDtypeStruct((1, SSM_WIDTH), F32)],
        scratch_shapes=[pltpu.VMEM((tt, STATE_WIDTH), F32), pltpu.VMEM((tt, STATE_WIDTH), F32),
                        pltpu.VMEM((1, STATE_WIDTH), F32), pltpu.VMEM((1, STATE_WIDTH), F32),
                        pltpu.VMEM((SEG, STATE_WIDTH), F32), pltpu.VMEM((SEG, STATE_WIDTH), F32),
                        pltpu.VMEM((tt, SSM_WIDTH), F32), pltpu.VMEM((tt, SSM_WIDTH), F32),
                        pltpu.VMEM((tt, SSM_WIDTH), F32)],
        compiler_params=_params(("arbitrary",)),
    )(dyg, u, h_re, h_im, lam_re, lam_im, bb_re, bb_im, cc_re, cc_im, d_skip)


def merge_fwd(h, o, yg, gates, wap_t, wv_t, wgg_t, wout, name):
    t, d = h.shape
    tm = TOKEN_TILE

    def body(h_ref, o_ref, yg_ref, gt_ref, wap_ref, wv_ref, wgg_ref, wout_ref, ho_ref, mg_ref, a_ref, sv_ref, sg_ref):
        att = _nt(o_ref[...], wap_ref[...])
        ygv = yg_ref[...]
        sv = _nt(ygv, wv_ref[...])
        sg = _nt(ygv, wgg_ref[...])
        a_ref[...] = att
        sv_ref[...] = sv
        sg_ref[...] = sg
        merged = (jax.nn.sigmoid(gt_ref[:, 0:d]) * att
                  + jax.nn.sigmoid(gt_ref[:, d:2 * d]) * (sv * jax.nn.sigmoid(sg))).astype(BF16)
        mg_ref[...] = merged
        ho_ref[...] = h_ref[...] + _nn(merged, wout_ref[...])

    return pl.pallas_call(
        body, name=name, grid=(t // tm,),
        in_specs=[_row_spec(tm, d), _row_spec(tm, ATTN_WIDTH), _row_spec(tm, SSM_WIDTH), _row_spec(tm, 2 * d),
                  _VMEM, _VMEM, _VMEM, _VMEM],
        out_specs=[_row_spec(tm, d), _row_spec(tm, d), _row_spec(tm, d), _row_spec(tm, d), _row_spec(tm, d)],
        out_shape=[jax.ShapeDtypeStruct((t, d), F32), jax.ShapeDtypeStruct((t, d), BF16),
                   jax.ShapeDtypeStruct((t, d), F32), jax.ShapeDtypeStruct((t, d), F32),
                   jax.ShapeDtypeStruct((t, d), F32)],
        compiler_params=_params(("arbitrary",)),
    )(h, o, yg, gates, wap_t, wv_t, wgg_t, wout)


def merge_bwd(dh, gates, att, sv, sg, wap_t, wv_t, wgg_t, wout, dep, name):
    t, d = dh.shape
    tm = TOKEN_TILE

    def body(dh_ref, gt_ref, a_ref, sv_ref, sg_ref, wap_ref, wv_ref, wgg_ref, wout_ref, dep_ref,
             dgt_ref, da_ref, dsv_ref, dsg_ref, do_ref, dyg_ref, dhb_ref):
        dhb = dh_ref[...].astype(BF16)
        dhb_ref[...] = dhb
        dm = _nt(dhb, wout_ref[...])
        sig_a = jax.nn.sigmoid(gt_ref[:, 0:d])
        sig_s = jax.nn.sigmoid(gt_ref[:, d:2 * d])
        sig_g = jax.nn.sigmoid(sg_ref[...])
        svv = sv_ref[...]
        dgt_ref[:, 0:d] = (dm * a_ref[...] * sig_a * (1.0 - sig_a)).astype(BF16)
        dgt_ref[:, d:2 * d] = (dm * (svv * sig_g) * sig_s * (1.0 - sig_s)).astype(BF16)
        da = (dm * sig_a).astype(BF16)
        d_s = dm * sig_s
        dsv = (d_s * sig_g).astype(BF16)
        dsg = (d_s * svv * sig_g * (1.0 - sig_g)).astype(BF16)
        da_ref[...] = da
        dsv_ref[...] = dsv
        dsg_ref[...] = dsg
        do_ref[...] = _nn(da, wap_ref[...]).astype(BF16)
        dyg_ref[...] = _nn(dsv, wv_ref[...]) + _nn(dsg, wgg_ref[...])

    return pl.pallas_call(
        body, name=name, grid=(t // tm,),
        in_specs=[_row_spec(tm, d), _row_spec(tm, 2 * d), _row_spec(tm, d), _row_spec(tm, d), _row_spec(tm, d),
                  _VMEM, _VMEM, _VMEM, _VMEM, _ANY],
        out_specs=[_row_spec(tm, 2 * d), _row_spec(tm, d), _row_spec(tm, d), _row_spec(tm, d),
                   _row_spec(tm, ATTN_WIDTH), _row_spec(tm, SSM_WIDTH), _row_spec(tm, d)],
        out_shape=[jax.ShapeDtypeStruct((t, 2 * d), BF16), jax.ShapeDtypeStruct((t, d), BF16),
                   jax.ShapeDtypeStruct((t, d), BF16), jax.ShapeDtypeStruct((t, d), BF16),
                   jax.ShapeDtypeStruct((t, ATTN_WIDTH), BF16), jax.ShapeDtypeStruct((t, SSM_WIDTH), F32),
                   jax.ShapeDtypeStruct((t, d), BF16)],
        compiler_params=_params(("arbitrary",)),
    )(dh, gates, att, sv, sg, wap_t, wv_t, wgg_t, wout, dep)


def _adamw_math(w, g, m, v):
    mn = ADAM_B1 * m + (1.0 - ADAM_B1) * g
    vn = ADAM_B2 * v + (1.0 - ADAM_B2) * (g * g)
    m_hat = mn / (1.0 - ADAM_B1 ** ADAM_STEP)
    v_hat = vn / (1.0 - ADAM_B2 ** ADAM_STEP)
    return -ADAM_LR * (m_hat / (jnp.sqrt(v_hat) + ADAM_EPS) + ADAM_WD * w), mn, vn


def adamw_layer(w, g, m, v, layer, prev, name):
    _, rows, cols = w.shape
    tr = rows
    for cand in (512, 256):
        if rows > cand and rows % cand == 0:
            tr = cand
            break

    def body(w_ref, g_ref, m_ref, v_ref, *rest):
        go_ref, d_ref, mo_ref, vo_ref = rest[-4:]
        gv = g_ref[...]
        go_ref[0] = gv
        d_ref[0], mo_ref[0], vo_ref[0] = _adamw_math(w_ref[0], gv, m_ref[0], v_ref[0])

    spec3 = pl.BlockSpec((1, tr, cols), lambda r: (layer, r, 0))
    out = jax.ShapeDtypeStruct(w.shape, F32)
    extra = [] if prev is None else list(prev)
    return pl.pallas_call(
        body, name=name, grid=(rows // tr,),
        in_specs=[spec3, _row_spec(tr, cols), spec3, spec3] + [_ANY] * len(extra),
        out_specs=[spec3] * 4, out_shape=[out] * 4,
        input_output_aliases={4 + j: j for j in range(len(extra))},
        compiler_params=_params(("arbitrary",)),
    )(w, g, m, v, *extra)


def adamw(w, g, m, v, name, minor_swap=False):
    if minor_swap:
        d, mn, vn = adamw(*[jnp.swapaxes(a, -1, -2) for a in (w, g, m, v)], name)
        return jnp.swapaxes(d, -1, -2), jnp.swapaxes(mn, -1, -2), jnp.swapaxes(vn, -1, -2)
    shape = w.shape
    as2d = lambda a: a.reshape(-1, shape[-1]) if a.ndim >= 2 else a.reshape(1, -1)
    w2, g2, m2, v2 = as2d(w), as2d(g), as2d(m), as2d(v)
    rows, cols = w2.shape
    tr = rows
    for cand in (1024, 704, 512, 256):
        if rows > cand and rows % cand == 0:
            tr = cand
            break

    def body(w_ref, g_ref, m_ref, v_ref, d_ref, mo_ref, vo_ref):
        d_ref[...], mo_ref[...], vo_ref[...] = _adamw_math(w_ref[...], g_ref[...], m_ref[...], v_ref[...])

    spec = _row_spec(tr, cols)
    out = jax.ShapeDtypeStruct((rows, cols), F32)
    d, mn, vn = pl.pallas_call(
        body, name=name, grid=(rows // tr,), in_specs=[spec] * 4, out_specs=[spec] * 3, out_shape=[out] * 3,
        compiler_params=_params(("arbitrary",)),
    )(w2, g2, m2, v2)
    return d.reshape(shape), mn.reshape(shape), vn.reshape(shape)


def _my_index():
    return 4 * lax.axis_index("x") + 2 * lax.axis_index("y") + lax.axis_index("c")


def _peer(p):
    return (lax.axis_index("x") ^ ((p >> 2) & 1), lax.axis_index("y") ^ ((p >> 1) & 1), lax.axis_index("c") ^ (p & 1))


_HBM = pl.BlockSpec(memory_space=pltpu.HBM)
_SEM = pl.BlockSpec(memory_space=pltpu.SEMAPHORE)
_EFFECT = pltpu.SideEffectType.DATAFLOW_SIDE_EFFECTING


class Exchange:
    def __init__(self, srcs, scatter, name):
        self.n = n = len(srcs)
        self.scatter = scatter
        self.name = name
        widths = sorted({s.shape[1] for s in srcs}, reverse=True)
        self.ncls = len(widths)
        self.cls = [widths.index(s.shape[1]) for s in srcs]
        self.cnts = [s.shape[0] // N_DEV if scatter else s.shape[0] for s in srcs]
        self.totals = [sum(c for c, k in zip(self.cnts, self.cls) if k == w) for w in range(self.ncls)]
        self.sizer = [max((k for k in range(n) if self.cls[k] == w), key=lambda k: self.cnts[k])
                      for w in range(self.ncls)]
        assert all(N_DEV * self.cnts[self.sizer[w]] >= self.totals[w] for w in range(self.ncls))
        if scatter:
            self.land_shapes = [(N_DEV, c, s.shape[1]) for s, c in zip(srcs, self.cnts)]
        else:
            self.land_shapes = [(N_DEV * c, s.shape[1]) for s, c in zip(srcs, self.cnts)]
        self.dtypes = [s.dtype for s in srcs]

    def _block(self, k, who):
        return pl.ds(pl.multiple_of(who * self.cnts[k], 16), self.cnts[k])

    def _sem(self, p, w):
        return (p - 1) * self.ncls + w

    def start(self, srcs, after):
        n = self.n

        def body(*refs):
            src, land = refs[:n], refs[n:2 * n]
            send_sems, recv_sems = refs[2 * n + 1], refs[2 * n + 2]
            token = refs[-1]
            me = _my_index()
            for p in range(1, N_DEV):
                for k in range(n):
                    if self.scatter:
                        s_ref, d_ref = src[k].at[self._block(k, me ^ p), :], land[k].at[me]
                    else:
                        s_ref, d_ref = src[k], land[k].at[self._block(k, me), :]
                    pltpu.make_async_remote_copy(
                        src_ref=s_ref, dst_ref=d_ref, send_sem=send_sems.at[self._sem(p, self.cls[k])],
                        recv_sem=recv_sems.at[self._sem(p, self.cls[k])], device_id=_peer(p),
                        device_id_type=MESH).start()
            token[...] = jnp.zeros_like(token)

        sems = pltpu.SemaphoreType.DMA(((N_DEV - 1) * self.ncls,))
        thru = [pltpu.HBM(s.shape, s.dtype) for s in srcs] + [pltpu.HBM(shp, dt) for shp, dt in
                                                               zip(self.land_shapes, self.dtypes)]
        lands = [pltpu.with_memory_space_constraint(lax.empty(shp, dt), pltpu.HBM)
                 for shp, dt in zip(self.land_shapes, self.dtypes)]
        out = pl.pallas_call(
            body, name=self.name + "_start",
            in_specs=[_HBM] * (2 * n) + [_ANY],
            out_shape=[sems, sems] + thru + [jax.ShapeDtypeStruct((8, 128), F32)],
            out_specs=[_SEM, _SEM] + [_HBM] * (2 * n) + [_VMEM],
            input_output_aliases={j: 2 + j for j in range(2 * n)},
            compiler_params=pltpu.CompilerParams(has_side_effects=_EFFECT),
        )(*[pltpu.with_memory_space_constraint(s, pltpu.HBM) for s in srcs], *lands, after)
        return out[:-1], out[-1]

    def wait(self, state, after):
        n = self.n
        send_sems, recv_sems = state[0], state[1]
        thru = state[2:]

        def body(*refs):
            src, land = refs[:n], refs[n:2 * n]
            send_sems, recv_sems = refs[2 * n], refs[2 * n + 1]
            for p in range(1, N_DEV):
                for w in range(self.ncls):
                    big = src[self.sizer[w]] if self.scatter else land[self.sizer[w]]
                    span = big.at[pl.ds(0, self.totals[w]), :]
                    copy = pltpu.make_async_remote_copy(
                        src_ref=span, dst_ref=span, send_sem=send_sems.at[self._sem(p, w)],
                        recv_sem=recv_sems.at[self._sem(p, w)],
                        device_id=_peer(p), device_id_type=MESH)
                    copy.wait_send()
                    copy.wait_recv()

        out = pl.pallas_call(
            body, name=self.name + "_wait",
            in_specs=[_HBM] * (2 * n) + [_SEM, _SEM, _ANY],
            out_shape=[pltpu.HBM(a.shape, a.dtype) for a in thru], out_specs=[_HBM] * (2 * n),
            input_output_aliases={j: j for j in range(2 * n)},
            compiler_params=pltpu.CompilerParams(has_side_effects=_EFFECT),
        )(*thru, send_sems, recv_sems, after)
        return out[:n], out[n:]

    def place(self, lands, srcs):
        n = self.n
        assert not self.scatter

        def body(*refs):
            src, land = refs[n:2 * n], refs[2 * n:3 * n]
            bufs, sems = refs[3 * n:4 * n], refs[-1]
            me = _my_index()
            loads = [pltpu.make_async_copy(src[k], bufs[k], sems.at[k]) for k in range(n)]
            stores = [pltpu.make_async_copy(bufs[k], land[k].at[self._block(k, me), :], sems.at[k]) for k in range(n)]
            for cp in loads:
                cp.start()
            for k in range(n):
                loads[k].wait()
                stores[k].start()
            for cp in stores:
                cp.wait()

        return pl.pallas_call(
            body, name=self.name + "_place", in_specs=[_ANY] * (2 * n), out_specs=[_ANY] * n,
            out_shape=[jax.ShapeDtypeStruct(a.shape, a.dtype) for a in lands],
            input_output_aliases={j: j for j in range(n)},
            scratch_shapes=[pltpu.VMEM(s.shape, s.dtype) for s in srcs] + [pltpu.SemaphoreType.DMA((n,))],
        )(*lands, *srcs)


def sum_blocks(landed, full, name):
    _, cnt, cols = landed.shape

    def body(land_ref, full_ref, o_ref, own_ref, sem):
        me = _my_index()
        own = pltpu.make_async_copy(full_ref.at[pl.ds(pl.multiple_of(me * cnt, 16), cnt), :], own_ref, sem)
        own.start()
        acc = land_ref[me ^ 1].astype(F32)
        for p in range(2, N_DEV):
            acc = acc + land_ref[me ^ p].astype(F32)
        own.wait()
        o_ref[...] = acc + own_ref[...].astype(F32)

    return pl.pallas_call(
        body, name=name, in_specs=[_VMEM, _ANY], out_specs=_VMEM,
        out_shape=jax.ShapeDtypeStruct((cnt, cols), F32),
        scratch_shapes=[pltpu.VMEM((cnt, cols), landed.dtype), pltpu.SemaphoreType.DMA],
        compiler_params=_params(),
    )(landed, full)


def sum_slots(slots, name):
    _, rows, cols = slots.shape
    tr = rows
    if rows > 512:
        for cand in (256, 128, 64, 32, 16, 8):
            if rows % cand == 0:
                tr = cand
                break

    def body(s_ref, o_ref):
        acc = s_ref[0].astype(F32)
        for j in range(1, N_DEV):
            acc = acc + s_ref[j].astype(F32)
        o_ref[...] = acc

    return pl.pallas_call(
        body, name=name, grid=(rows // tr,),
        in_specs=[pl.BlockSpec((N_DEV, tr, cols), lambda i: (0, i, 0))], out_specs=_row_spec(tr, cols),
        out_shape=jax.ShapeDtypeStruct((rows, cols), F32), compiler_params=_params(("arbitrary",)),
    )(slots)


BIG_T = ("ffn1_w_gate", "ffn1_w_up", "w_in", "ffn2_w_gate", "ffn2_w_up")
BIG_N = ("ffn1_w_down", "w_out", "ffn2_w_down")
HALF_T = ("w_attn_proj", "w_glu_v", "w_glu_g")
SMALL = ("ffn1_norm", "mix_norm", "attn_sinks", "ssm_a_re", "ssm_a_im", "ssm_log_dt", "ssm_b_re", "ssm_b_im",
         "ssm_c_re", "ssm_c_im", "ssm_d", "ffn2_norm", "final_norm")
PARTS = {"ffn1": ("ffn1_w_gate", "ffn1_w_up", "ffn1_w_down"),
         "mix": ("w_in", "w_out", "w_attn_proj", "w_glu_v", "w_glu_g"),
         "ffn2": ("ffn2_w_gate", "ffn2_w_up", "ffn2_w_down")}


def _to_rows(name, a):
    return a if name in BIG_N else jnp.swapaxes(a, -1, -2)


def local_step(x, tgt, get_weights, put_grads, small):
    seq, d = x.shape
    t = PAD_FRONT + N_META + seq
    cos_t, sin_t = rope_tables(t)
    row = lambda a: a.reshape(1, -1)
    saved = []
    h = None
    for i in range(DEPTH):
        s = {}
        w = dict(get_weights(i, "ffn1", h))
        if i == 0:
            h = jnp.concatenate([jnp.zeros((PAD_FRONT, d), F32), w["meta_tokens"], x], axis=0)
        s["h0"] = h
        h, s["n1"], s["a1"], s["b1"] = ffn_fwd(h, row(small["ffn1_norm"][i]), w["ffn1_w_gate"], w["ffn1_w_up"],
                                               w["ffn1_w_down"], f"ffn1_fwd_{i}")
        s["h1"] = h
        w.update(get_weights(i, "mix", h))
        s["n2"], s["qkv"], s["u"], s["gates"] = win_fwd(h, row(small["mix_norm"][i]), w["w_in"], cos_t, sin_t,
                                                        f"win_fwd_{i}")
        b_re_t = jnp.swapaxes(small["ssm_b_re"][i], 1, 2)
        b_im_t = jnp.swapaxes(small["ssm_b_im"][i], 1, 2)
        s["b_t"] = (b_re_t, b_im_t)
        lam_re, lam_im, bbar_re, bbar_im = ssm_prep(small["ssm_a_re"][i], small["ssm_a_im"][i],
                                                    small["ssm_log_dt"][i].reshape(-1, 1), b_re_t, b_im_t, f"ssm_prep_{i}")
        s["ssm"] = (row(lam_re), row(lam_im), _block_diag_b(bbar_re).astype(BF16), _block_diag_b(bbar_im).astype(BF16),
                    _block_diag_c(small["ssm_c_re"][i]).astype(BF16), _block_diag_c(small["ssm_c_im"][i]).astype(BF16),
                    row(small["ssm_d"][i]))
        s["yg"], s["h_re"], s["h_im"] = ssm_fwd(s["u"], *s["ssm"], f"ssm_fwd_{i}")
        s["o"] = attn_fwd(s["qkv"], row(small["attn_sinks"][i]), f"attn_fwd_{i}")
        h, s["merged"], s["att"], s["sv"], s["sg"] = merge_fwd(
            h, s["o"], s["yg"], s["gates"], w["w_attn_proj"], w["w_glu_v"], w["w_glu_g"], w["w_out"],
            f"merge_fwd_{i}")
        s["h2"] = h
        w.update(get_weights(i, "ffn2", h))
        h, s["n3"], s["a3"], s["b3"] = ffn_fwd(h, row(small["ffn2_norm"][i]), w["ffn2_w_gate"], w["ffn2_w_up"],
                                               w["ffn2_w_down"], f"ffn2_fwd_{i}")
        s["w"] = w
        saved.append(s)

    loss, dh, d_final = head_fwd_bwd(h, row(small["final_norm"]), tgt)
    gs = {k: [None] * DEPTH for k in SMALL if k != "final_norm"}
    dep = loss
    for i in reversed(range(DEPTH)):
        s = saved[i]
        w = s["w"]
        dh, da, db, sact, dhb, dg = ffn_bwd(dh, s["h2"], row(small["ffn2_norm"][i]), s["a3"], s["b3"], w["ffn2_w_gate"],
                                            w["ffn2_w_up"], w["ffn2_w_down"], dep, f"ffn2_bwd_{i}")
        gs["ffn2_norm"][i] = dg[0]
        dep = put_grads(i, "ffn2", {"ffn2_w_gate": tn_matmul(da, s["n3"], f"ffn2_dwg_{i}"),
                                    "ffn2_w_up": tn_matmul(db, s["n3"], f"ffn2_dwu_{i}"),
                                    "ffn2_w_down": tn_matmul(sact, dhb, f"ffn2_dwd_{i}")})

        dgates, datt, dsv, dsg, do, dyg, dhb = merge_bwd(dh, s["gates"], s["att"], s["sv"], s["sg"], w["w_attn_proj"],
                                                         w["w_glu_v"], w["w_glu_g"], w["w_out"], dep, f"merge_bwd_{i}")
        gmix = {"w_out": tn_matmul(s["merged"], dhb, f"dwout_{i}"),
                "w_attn_proj": tn_matmul(datt, s["o"], f"dwap_{i}"),
                "w_glu_v": tn_matmul(dsv, s["yg"], f"dwv_{i}"),
                "w_glu_g": tn_matmul(dsg, s["yg"], f"dwgg_{i}")}
        dqkv, dsink = attn_bwd(s["qkv"], do, row(small["attn_sinks"][i]), cos_t, sin_t, f"attn_bwd_{i}")
        gs["attn_sinks"][i] = dsink[:, 0]
        du, dl_re, dl_im, dbb_re, dbb_im, dcc_re, dcc_im, dd = ssm_bwd(dyg, s["u"], s["h_re"], s["h_im"], *s["ssm"],
                                                                      f"ssm_bwd_{i}")
        fold = lambda a: jnp.sum(a, axis=0).reshape(SSM_GROUPS, SSM_STATE)
        da_re, da_im, dldt, db_re_t, db_im_t = ssm_prep_bwd(
            small["ssm_a_re"][i], small["ssm_a_im"][i], small["ssm_log_dt"][i].reshape(-1, 1), *s["b_t"],
            fold(dl_re), fold(dl_im), _diag_of_b(dbb_re), _diag_of_b(dbb_im), f"ssm_prep_bwd_{i}")
        gs["ssm_a_re"][i], gs["ssm_a_im"][i], gs["ssm_log_dt"][i] = da_re, da_im, dldt[:, 0]
        gs["ssm_b_re"][i], gs["ssm_b_im"][i] = jnp.swapaxes(db_re_t, 1, 2), jnp.swapaxes(db_im_t, 1, 2)
        gs["ssm_c_re"][i], gs["ssm_c_im"][i] = _diag_of_c(dcc_re), _diag_of_c(dcc_im)
        gs["ssm_d"][i] = dd[0]
        gmix["w_in"] = tn_matmul([dqkv, du, dgates], s["n2"], f"dwin_{i}")
        dep = put_grads(i, "mix", gmix)
        dh, dg = win_bwd(dh, s["h1"], row(small["mix_norm"][i]), dqkv, du, dgates, w["w_in"], dep, f"win_bwd_{i}")
        gs["mix_norm"][i] = dg[0]

        dh, da, db, sact, dhb, dg = ffn_bwd(dh, s["h0"], row(small["ffn1_norm"][i]), s["a1"], s["b1"], w["ffn1_w_gate"],
                                            w["ffn1_w_up"], w["ffn1_w_down"], dep, f"ffn1_bwd_{i}")
        gs["ffn1_norm"][i] = dg[0]
        if i > 0:
            dep = put_grads(i, "ffn1", {"ffn1_w_gate": tn_matmul(da, s["n1"], f"ffn1_dwg_{i}"),
                                        "ffn1_w_up": tn_matmul(db, s["n1"], f"ffn1_dwu_{i}"),
                                        "ffn1_w_down": tn_matmul(sact, dhb, f"ffn1_dwd_{i}")})
        else:
            for k, xa, ya in (("ffn1_w_down", sact, dhb), ("ffn1_w_gate", da, s["n1"]), ("ffn1_w_up", db, s["n1"])):
                dep = put_grads(i, "ffn1", {k: tn_matmul(xa, ya, f"d_{k}_{i}", dep)})

    gs = {k: jnp.stack(v) for k, v in gs.items()}
    gs["final_norm"] = d_final[0]
    return loss[0, 0], dh[PAD_FRONT + N_META:], dh[PAD_FRONT:PAD_FRONT + N_META], gs, dep


def _pack_rows(arrays, cols):
    flat = jnp.concatenate([a.reshape(-1) for a in arrays])
    rows = -(-flat.shape[0] // cols)
    rows = -(-rows // 16) * 16
    return jnp.pad(flat, (0, rows * cols - flat.shape[0])).reshape(rows, cols)


def _unpack_rows(packed, shapes):
    flat = packed.reshape(-1)
    out, off = [], 0
    for shp in shapes:
        n = math.prod(shp)
        out.append(flat[off:off + n].reshape(shp))
        off += n
    return out


def kernel(x, meta_tokens, ffn1_norm, ffn1_w_gate, ffn1_w_up, ffn1_w_down, mix_norm, w_in, attn_sinks, ssm_a_re, ssm_a_im, ssm_log_dt, ssm_b_re, ssm_b_im, ssm_c_re, ssm_c_im, ssm_d, w_attn_proj, w_glu_v, w_glu_g, w_out, ffn2_norm, ffn2_w_gate, ffn2_w_up, ffn2_w_down, final_norm, loss_target, m_meta_tokens, m_ffn1_norm, m_ffn1_w_gate, m_ffn1_w_up, m_ffn1_w_down, m_mix_norm, m_w_in, m_attn_sinks, m_ssm_a_re, m_ssm_a_im, m_ssm_log_dt, m_ssm_b_re, m_ssm_b_im, m_ssm_c_re, m_ssm_c_im, m_ssm_d, m_w_attn_proj, m_w_glu_v, m_w_glu_g, m_w_out, m_ffn2_norm, m_ffn2_w_gate, m_ffn2_w_up, m_ffn2_w_down, m_final_norm, v_meta_tokens, v_ffn1_norm, v_ffn1_w_gate, v_ffn1_w_up, v_ffn1_w_down, v_mix_norm, v_w_in, v_attn_sinks, v_ssm_a_re, v_ssm_a_im, v_ssm_log_dt, v_ssm_b_re, v_ssm_b_im, v_ssm_c_re, v_ssm_c_im, v_ssm_d, v_w_attn_proj, v_w_glu_v, v_w_glu_g, v_w_out, v_ffn2_norm, v_ffn2_w_gate, v_ffn2_w_up, v_ffn2_w_down, v_final_norm):
    names = ("meta_tokens", "ffn1_norm", "ffn1_w_gate", "ffn1_w_up", "ffn1_w_down", "mix_norm", "w_in", "attn_sinks",
             "ssm_a_re", "ssm_a_im", "ssm_log_dt", "ssm_b_re", "ssm_b_im", "ssm_c_re", "ssm_c_im", "ssm_d",
             "w_attn_proj", "w_glu_v", "w_glu_g", "w_out", "ffn2_norm", "ffn2_w_gate", "ffn2_w_up", "ffn2_w_down",
             "final_norm")
    weights = dict(zip(names, (meta_tokens, ffn1_norm, ffn1_w_gate, ffn1_w_up, ffn1_w_down, mix_norm, w_in, attn_sinks, ssm_a_re, ssm_a_im, ssm_log_dt, ssm_b_re, ssm_b_im, ssm_c_re, ssm_c_im, ssm_d, w_attn_proj, w_glu_v, w_glu_g, w_out, ffn2_norm, ffn2_w_gate, ffn2_w_up, ffn2_w_down, final_norm)))
    moments_m = dict(zip(names, (m_meta_tokens, m_ffn1_norm, m_ffn1_w_gate, m_ffn1_w_up, m_ffn1_w_down, m_mix_norm, m_w_in, m_attn_sinks, m_ssm_a_re, m_ssm_a_im, m_ssm_log_dt, m_ssm_b_re, m_ssm_b_im, m_ssm_c_re, m_ssm_c_im, m_ssm_d, m_w_attn_proj, m_w_glu_v, m_w_glu_g, m_w_out, m_ffn2_norm, m_ffn2_w_gate, m_ffn2_w_up, m_ffn2_w_down, m_final_norm)))
    moments_v = dict(zip(names, (v_meta_tokens, v_ffn1_norm, v_ffn1_w_gate, v_ffn1_w_up, v_ffn1_w_down, v_mix_norm, v_w_in, v_attn_sinks, v_ssm_a_re, v_ssm_a_im, v_ssm_log_dt, v_ssm_b_re, v_ssm_b_im, v_ssm_c_re, v_ssm_c_im, v_ssm_d, v_w_attn_proj, v_w_glu_v, v_w_glu_g, v_w_out, v_ffn2_norm, v_ffn2_w_gate, v_ffn2_w_up, v_ffn2_w_down, v_final_norm)))
    me = _my_index()

    gathers = {}
    token = jnp.zeros((8, 128), F32)
    for i in range(DEPTH):
        for part, ks in PARTS.items():
            shards = [_to_rows(k, weights[k][i]).astype(BF16) for k in ks]
            if (i, part) == (0, "ffn1"):
                shards.append(meta_tokens)
            ex = Exchange(shards, False, f"gather_{part}_{i}")
            state, token = ex.start(shards, token)
            gathers[i, part] = (ex, state, shards)
    all_started = token

    def get_weights(i, part, after):
        ex, state, shards = gathers[i, part]
        shards, lands = ex.wait(state, all_started if after is None else after)
        fulls = ex.place(lands, shards)
        got = dict(zip(PARTS[part], fulls))
        if (i, part) == (0, "ffn1"):
            got["meta_tokens"] = jnp.swapaxes(fulls[-1].reshape(N_DEV, N_META, 128), 0, 1).reshape(N_META, D_MODEL)
        return got

    scatters = []

    def put_grads(i, part, gdict):
        ks = list(gdict)
        srcs = [gdict[k] for k in ks]
        ex = Exchange(srcs, True, f"scatter_{part if len(ks) > 1 else ks[0]}_{i}")
        state, tok = ex.start(srcs, all_started)
        scatters.append((i, ks, ex, state))
        return tok

    small = {k: weights[k] for k in SMALL}
    loss, dx, dmeta, gs, last_started = local_step(x[0], loss_target[0], get_weights, put_grads, small)

    grads, deltas, new_m, new_v = {}, {}, {}, {}
    small_list = [loss.reshape(1), dmeta] + [gs[k] for k in SMALL]
    packed = _pack_rows(small_list, D_MODEL)
    small_ex = Exchange([packed], False, "gather_small")
    small_state, after = small_ex.start([packed], last_started)

    updated = {}
    for i, ks, ex, state in scatters:
        partials, lands = ex.wait(state, after)
        for k, partial, slots in zip(ks, partials, lands):
            g = sum_blocks(slots, partial, f"sum_{k}_{i}")
            updated[k] = adamw_layer(_to_rows(k, weights[k]), g, _to_rows(k, moments_m[k]), _to_rows(k, moments_v[k]),
                                     i, updated.get(k), f"adamw_{k}_{i}")
            after = updated[k][0]
    for k, outs in updated.items():
        grads[k], deltas[k], new_m[k], new_v[k] = [_to_rows(k, a) for a in outs]

    packed_own, packed_all = small_ex.wait(small_state, after)
    (packed_all,) = small_ex.place(packed_all, packed_own)
    total = sum_slots(packed_all.reshape(N_DEV, packed.shape[0], D_MODEL), "sum_small")
    pieces = _unpack_rows(total, [a.shape for a in small_list])
    loss_out = pieces[0][0]
    grads["meta_tokens"] = lax.dynamic_slice_in_dim(pieces[1], me * 128, 128, axis=1)
    for k, p in zip(SMALL, pieces[2:]):
        grads[k] = p
    for k in ("meta_tokens",) + SMALL:
        deltas[k], new_m[k], new_v[k] = adamw(weights[k], grads[k], moments_m[k], moments_v[k], f"adamw_{k}",
                                              minor_swap=k in ("ssm_b_re", "ssm_b_im"))
    return (loss_out, dx[None], *[grads[k] for k in names], *[deltas[k] for k in names],
            *[new_m[k] for k in names], *[new_v[k] for k in names])
```

```python
import functools
import math

import jax
import jax.numpy as jnp
from jax import lax
from jax.experimental import pallas as pl
from jax.experimental.pallas import tpu as pltpu

F32 = jnp.float32
BF16 = jnp.bfloat16

D_MODEL = 1024
DEPTH = 2
N_META = 16
HEAD_DIM = 64
N_Q_HEADS = 8
ATTN_WIDTH = 512
KV_WIDTH = 128
QKV_WIDTH = ATTN_WIDTH + 2 * KV_WIDTH
WINDOW = 128
BLK = 128
ROPE_THETA = 500000.0
ROT_DIM = 16
SSM_WIDTH = 512
SSM_GROUP = 16
SSM_GROUPS = 32
SSM_STATE = 64
STATE_WIDTH = SSM_GROUPS * SSM_STATE
D_FF = 2816
IN_WIDTH = 3328
EPS = 1e-6
NEG_INF = -1e30
PAD_FRONT = (-N_META) % BLK
N_DEV = 8

ADAM_LR = 0.001
ADAM_B1 = 0.9
ADAM_B2 = 0.999
ADAM_EPS = 1e-08
ADAM_WD = 0.01
ADAM_STEP = 10

VMEM_LIMIT = 56 * 1024 * 1024
TOKEN_TILE = 384
_VMEM = pl.BlockSpec(memory_space=pltpu.VMEM)
_SMEM = pl.BlockSpec(memory_space=pltpu.SMEM)
_ANY = pl.BlockSpec(memory_space=pl.ANY)
MESH = pl.DeviceIdType.MESH


def _params(sem=None):
    return pltpu.CompilerParams(dimension_semantics=sem, vmem_limit_bytes=VMEM_LIMIT)


def _nt(a, b):
    return lax.dot_general(a, b, (((1,), (1,)), ((), ())), preferred_element_type=F32)


def _nn(a, b):
    return jnp.dot(a, b, preferred_element_type=F32)


def _tn(a, b):
    return lax.dot_general(a, b, (((0,), (0,)), ((), ())), preferred_element_type=F32)


def _row_spec(tm, width):
    return pl.BlockSpec((tm, width), lambda i: (i, 0))


def _acc_spec(shape):
    return pl.BlockSpec(shape, lambda i: (0,) * len(shape))


def _rms_stats(x):
    r = lax.rsqrt(jnp.mean(x * x, axis=-1, keepdims=True) + EPS)
    return x * r, r


def _rms_bwd(dn, xh, r, g):
    dg = jnp.sum(dn * xh, axis=0, keepdims=True)
    dxh = dn * g
    dx = r * (dxh - xh * jnp.mean(dxh * xh, axis=-1, keepdims=True))
    return dx, dg


def ffn_fwd(h, g, wg_t, wu_t, wd, name):
    t, d = h.shape
    f = wd.shape[0]
    tm = TOKEN_TILE

    def body(h_ref, g_ref, wg_ref, wu_ref, wd_ref, ho_ref, n_ref, a_ref, b_ref):
        x = h_ref[...]
        xh, _ = _rms_stats(x)
        n = (xh * g_ref[...]).astype(BF16)
        n_ref[...] = n
        a = _nt(n, wg_ref[...])
        b = _nt(n, wu_ref[...])
        a_ref[...] = a.astype(BF16)
        b_ref[...] = b.astype(BF16)
        s = (a * jax.nn.sigmoid(a) * b).astype(BF16)
        ho_ref[...] = x + 0.5 * _nn(s, wd_ref[...])

    return pl.pallas_call(
        body, name=name, grid=(t // tm,),
        in_specs=[_row_spec(tm, d), _acc_spec((1, d)), _VMEM, _VMEM, _VMEM],
        out_specs=[_row_spec(tm, d), _row_spec(tm, d), _row_spec(tm, f), _row_spec(tm, f)],
        out_shape=[jax.ShapeDtypeStruct((t, d), F32), jax.ShapeDtypeStruct((t, d), BF16),
                   jax.ShapeDtypeStruct((t, f), BF16), jax.ShapeDtypeStruct((t, f), BF16)],
        compiler_params=_params(("arbitrary",)),
    )(h, g, wg_t, wu_t, wd)


def ffn_bwd(dh, h, g, a, b, wg_t, wu_t, wd, dep, name):
    t, d = h.shape
    f = wd.shape[0]
    tm = TOKEN_TILE // 2

    def body(dh_ref, h_ref, g_ref, a_ref, b_ref, wg_ref, wu_ref, wd_ref, dep_ref,
             dhi_ref, da_ref, db_ref, s_ref, dhb_ref, dg_ref):
        dh_t = dh_ref[...]
        dhb = (0.5 * dh_t).astype(BF16)
        dhb_ref[...] = dhb
        ds = _nt(dhb, wd_ref[...])
        av = a_ref[...].astype(F32)
        bv = b_ref[...].astype(F32)
        sig = jax.nn.sigmoid(av)
        sl = av * sig
        s_ref[...] = (sl * bv).astype(BF16)
        da = (ds * bv * (sig * (1.0 + av * (1.0 - sig)))).astype(BF16)
        db = (ds * sl).astype(BF16)
        da_ref[...] = da
        db_ref[...] = db
        dn = _nn(da, wg_ref[...]) + _nn(db, wu_ref[...])
        xh, r = _rms_stats(h_ref[...])
        dx, dg = _rms_bwd(dn, xh, r, g_ref[...])
        dhi_ref[...] = dh_t + dx

        @pl.when(pl.program_id(0) == 0)
        def _():
            dg_ref[...] = jnp.zeros_like(dg_ref)

        dg_ref[...] += dg

    return pl.pallas_call(
        body, name=name, grid=(t // tm,),
        in_specs=[_row_spec(tm, d), _row_spec(tm, d), _acc_spec((1, d)), _row_spec(tm, f), _row_spec(tm, f),
                  _VMEM, _VMEM, _VMEM, _ANY],
        out_specs=[_row_spec(tm, d), _row_spec(tm, f), _row_spec(tm, f), _row_spec(tm, f), _row_spec(tm, d),
                   _acc_spec((1, d))],
        out_shape=[jax.ShapeDtypeStruct((t, d), F32), jax.ShapeDtypeStruct((t, f), BF16),
                   jax.ShapeDtypeStruct((t, f), BF16), jax.ShapeDtypeStruct((t, f), BF16),
                   jax.ShapeDtypeStruct((t, d), BF16), jax.ShapeDtypeStruct((1, d), F32)],
        compiler_params=_params(("arbitrary",)),
    )(dh, h, g, a, b, wg_t, wu_t, wd, dep)


DW_TILE = 256


def tn_matmul(x, y, name, dep=None):
    xs = list(x) if isinstance(x, (list, tuple)) else [x]
    t = xs[0].shape[0]
    n = y.shape[1]
    bm = DW_TILE
    tiles = [a.shape[1] // bm for a in xs]
    offs = [sum(tiles[:k]) for k in range(len(xs))]
    deps = [] if dep is None else [dep]

    def body(*refs):
        y_ref, o_ref = refs[len(xs)], refs[-1]
        i = pl.program_id(0)
        for k in range(len(xs)):
            @pl.when((i >= offs[k]) & (i < offs[k] + tiles[k]))
            def _(k=k):
                o_ref[...] = _tn(refs[k][...], y_ref[...]).astype(BF16)

    def x_spec(k):
        return pl.BlockSpec((t, bm), lambda i: (0, jnp.clip(i - offs[k], 0, tiles[k] - 1)))

    return pl.pallas_call(
        body, name=name, grid=(sum(tiles),),
        in_specs=[x_spec(k) for k in range(len(xs))] + [_VMEM] + [_ANY] * len(deps),
        out_specs=pl.BlockSpec((bm, n), lambda i: (i, 0)),
        out_shape=jax.ShapeDtypeStruct((sum(tiles) * bm, n), BF16),
        compiler_params=_params(("arbitrary",)),
    )(*xs, y, *deps)


def head_fwd_bwd(h, g, tgt):
    t, d = h.shape

    def body(h_ref, g_ref, t_ref, loss_ref, dh_ref, dg_ref):
        i = pl.program_id(0)
        xh, r = _rms_stats(h_ref[...])
        gv = g_ref[...]
        valid = (i > 0).astype(F32)
        e = (xh * gv - t_ref[...]) * valid
        dx, dg = _rms_bwd(e * (1.0 / d), xh, r, gv)
        dh_ref[...] = dx

        @pl.when(i == 0)
        def _():
            dg_ref[...] = jnp.zeros_like(dg_ref)
            loss_ref[...] = jnp.zeros_like(loss_ref)

        dg_ref[...] += dg
        loss_ref[...] += jnp.sum(e * e) * (0.5 / d)

    return pl.pallas_call(
        body, name="head", grid=(t // BLK,),
        in_specs=[_row_spec(BLK, d), _acc_spec((1, d)),
                  pl.BlockSpec((BLK, d), lambda i: (jnp.maximum(i - 1, 0), 0))],
        out_specs=[_acc_spec((1, 128)), _row_spec(BLK, d), _acc_spec((1, d))],
        out_shape=[jax.ShapeDtypeStruct((1, 128), F32), jax.ShapeDtypeStruct((t, d), F32),
                   jax.ShapeDtypeStruct((1, d), F32)],
        compiler_params=_params(("arbitrary",)),
    )(h, g, tgt)


def rope_tables(t):
    pos = jnp.arange(t, dtype=F32) - PAD_FRONT
    inv_freq = ROPE_THETA ** (-jnp.arange(0, ROT_DIM, 2, dtype=F32) / ROT_DIM)
    ang = pos[:, None] * inv_freq[None, :]
    cos, sin = jnp.cos(ang), jnp.sin(ang)
    ones = jnp.ones((t, HEAD_DIM - ROT_DIM), F32)
    cos_h = jnp.concatenate([cos, cos, ones], axis=1)
    sin_h = jnp.concatenate([-sin, sin, 0.0 * ones], axis=1)
    return jnp.concatenate([cos_h, cos_h], axis=1), jnp.concatenate([sin_h, sin_h], axis=1)


def _swap_halves(x):
    n = x.shape[1]
    lane = lax.broadcasted_iota(jnp.int32, x.shape, 1)
    return jnp.where(lane % HEAD_DIM < ROT_DIM // 2, pltpu.roll(x, n - ROT_DIM // 2, 1), pltpu.roll(x, ROT_DIM // 2, 1))


def _rope(x, cos_t, sin_t, sign):
    return x * cos_t + sign * (_swap_halves(x) * sin_t)


def win_fwd(h, g, win_t, cos_t, sin_t, name):
    t, d = h.shape
    tm = TOKEN_TILE

    def body(h_ref, g_ref, w_ref, c_ref, s_ref, n_ref, qkv_ref, u_ref, gates_ref):
        xh, _ = _rms_stats(h_ref[...])
        n = (xh * g_ref[...]).astype(BF16)
        n_ref[...] = n
        z = _nt(n, w_ref[...])
        c, s = c_ref[...], s_ref[...]
        for j in range((ATTN_WIDTH + KV_WIDTH) // 128):
            qkv_ref[:, j * 128:(j + 1) * 128] = _rope(z[:, j * 128:(j + 1) * 128], c, s, 1.0).astype(BF16)
        qkv_ref[:, ATTN_WIDTH + KV_WIDTH:QKV_WIDTH] = z[:, ATTN_WIDTH + KV_WIDTH:QKV_WIDTH].astype(BF16)
        u_ref[...] = z[:, QKV_WIDTH:QKV_WIDTH + SSM_WIDTH]
        gates_ref[...] = z[:, QKV_WIDTH + SSM_WIDTH:].astype(BF16)

    return pl.pallas_call(
        body, name=name, grid=(t // tm,),
        in_specs=[_row_spec(tm, d), _acc_spec((1, d)), _VMEM, _row_spec(tm, 128), _row_spec(tm, 128)],
        out_specs=[_row_spec(tm, d), _row_spec(tm, QKV_WIDTH), _row_spec(tm, SSM_WIDTH), _row_spec(tm, 2 * d)],
        out_shape=[jax.ShapeDtypeStruct((t, d), BF16), jax.ShapeDtypeStruct((t, QKV_WIDTH), BF16),
                   jax.ShapeDtypeStruct((t, SSM_WIDTH), F32), jax.ShapeDtypeStruct((t, 2 * d), BF16)],
        compiler_params=_params(("arbitrary",)),
    )(h, g, win_t, cos_t, sin_t)


def win_bwd(dh, h, g, dqkv, du, dgates, win_t, dep, name):
    t, d = h.shape
    tm = TOKEN_TILE

    def body(dh_ref, h_ref, g_ref, dqkv_ref, du_ref, dgt_ref, w_ref, dep_ref, dhi_ref, dg_ref):
        dn = (_nn(dqkv_ref[...], w_ref[0:QKV_WIDTH, :])
              + _nn(du_ref[...], w_ref[QKV_WIDTH:QKV_WIDTH + SSM_WIDTH, :])
              + _nn(dgt_ref[...], w_ref[QKV_WIDTH + SSM_WIDTH:, :]))
        xh, r = _rms_stats(h_ref[...])
        dx, dg = _rms_bwd(dn, xh, r, g_ref[...])
        dhi_ref[...] = dh_ref[...] + dx

        @pl.when(pl.program_id(0) == 0)
        def _():
            dg_ref[...] = jnp.zeros_like(dg_ref)

        dg_ref[...] += dg

    return pl.pallas_call(
        body, name=name, grid=(t // tm,),
        in_specs=[_row_spec(tm, d), _row_spec(tm, d), _acc_spec((1, d)), _row_spec(tm, QKV_WIDTH),
                  _row_spec(tm, SSM_WIDTH), _row_spec(tm, 2 * d), _VMEM, _ANY],
        out_specs=[_row_spec(tm, d), _acc_spec((1, d))],
        out_shape=[jax.ShapeDtypeStruct((t, d), F32), jax.ShapeDtypeStruct((1, d), F32)],
        compiler_params=_params(("arbitrary",)),
    )(dh, h, g, dqkv, du, dgates, win_t, dep)


def _attn_mask(blk):
    q_pos = blk * BLK + lax.broadcasted_iota(jnp.int32, (BLK, 3 * BLK), 0) - PAD_FRONT
    col = lax.broadcasted_iota(jnp.int32, (BLK, 3 * BLK), 1)
    part = col // BLK
    k_pos = jnp.where(part == 0, col, (blk + part - 2) * BLK + (col - part * BLK)) - PAD_FRONT
    dist = q_pos - k_pos
    meta_ok = (part == 0) & (k_pos >= 0) & (dist >= 0)
    band_ok = (part > 0) & (k_pos >= N_META) & (dist >= 0) & (dist < WINDOW)
    return meta_ok | band_ok


def _head_halves(x128, kv):
    x = x128.astype(F32)
    lane = lax.broadcasted_iota(jnp.int32, x.shape, 1)
    swapped = pltpu.roll(x, HEAD_DIM, 1)
    lo, hi = (x, swapped) if kv == 0 else (swapped, x)
    return jnp.where(lane < HEAD_DIM, lo, 0.0).astype(BF16), jnp.where(lane >= HEAD_DIM, hi, 0.0).astype(BF16)


def _gather_keys(meta_ref, prev_ref, cur_ref, lo):
    return jnp.concatenate([meta_ref[:, lo:lo + 128], prev_ref[:, lo:lo + 128], cur_ref[:, lo:lo + 128]], axis=0)


def _pair_lanes(kv):
    return slice(2 * kv * 128, (2 * kv + 1) * 128), slice((2 * kv + 1) * 128, (2 * kv + 2) * 128)


def _stacked_sinks(sink_ref, head):
    row = lax.broadcasted_iota(jnp.int32, (2 * BLK, 1), 0)
    return jnp.where(row < BLK, sink_ref[0, head], sink_ref[0, head + 2])


def _softmax_with_sink(s, mask, sink):
    s = jnp.where(mask, s * (HEAD_DIM ** -0.5), NEG_INF)
    m = jnp.maximum(jnp.max(s, axis=-1, keepdims=True), sink)
    p = jnp.exp(s - m)
    p_sink = jnp.exp(sink - m)
    inv = 1.0 / (jnp.sum(p, axis=-1, keepdims=True) + p_sink)
    return p * inv, p_sink * inv


def attn_fwd(qkv, sinks, name):
    t = qkv.shape[0]
    nb = t // BLK

    def body(sink_ref, meta_ref, prev_ref, cur_ref, o_ref):
        blk = pl.program_id(0)
        mask = _attn_mask(blk)
        mask2 = jnp.concatenate([mask, mask], axis=0)
        k128 = _gather_keys(meta_ref, prev_ref, cur_ref, ATTN_WIDTH)
        v128 = _gather_keys(meta_ref, prev_ref, cur_ref, ATTN_WIDTH + KV_WIDTH)
        for kv in range(2):
            k_lo, k_hi = _head_halves(k128, kv)
            v_lo, v_hi = _head_halves(v128, kv)
            lanes0, lanes1 = _pair_lanes(kv)
            q2 = jnp.concatenate([cur_ref[:, lanes0], cur_ref[:, lanes1]], axis=0)
            p_a, _ = _softmax_with_sink(_nt(q2, k_lo), mask2, _stacked_sinks(sink_ref, 4 * kv))
            p_b, _ = _softmax_with_sink(_nt(q2, k_hi), mask2, _stacked_sinks(sink_ref, 4 * kv + 1))
            o2 = (_nn(p_a.astype(BF16), v_lo) + _nn(p_b.astype(BF16), v_hi)).astype(BF16)
            o_ref[:, lanes0] = o2[0:BLK]
            o_ref[:, lanes1] = o2[BLK:2 * BLK]

    blk_spec = lambda f: pl.BlockSpec((BLK, QKV_WIDTH), f)
    return pl.pallas_call(
        body, name=name, grid=(nb,),
        in_specs=[_SMEM, blk_spec(lambda i: (0, 0)), blk_spec(lambda i: (jnp.maximum(i - 1, 0), 0)),
                  blk_spec(lambda i: (i, 0))],
        out_specs=_row_spec(BLK, ATTN_WIDTH),
        out_shape=jax.ShapeDtypeStruct((t, ATTN_WIDTH), BF16),
        compiler_params=_params(("arbitrary",)),
    )(sinks, qkv, qkv, qkv)


def attn_bwd(qkv, do, sinks, cos_t, sin_t, name):
    t = qkv.shape[0]
    nb = t // BLK

    def body(sink_ref, meta_ref, prev_ref, cur_ref, do_ref, c_ref, s_ref, dqkv_ref, dsink_ref, carry_ref, macc_ref):
        step = pl.program_id(0)
        blk = nb - 1 - step

        @pl.when(step == 0)
        def _():
            dsink_ref[...] = jnp.zeros_like(dsink_ref)
            carry_ref[...] = jnp.zeros_like(carry_ref)
            macc_ref[...] = jnp.zeros_like(macc_ref)

        mask = _attn_mask(blk)
        mask2 = jnp.concatenate([mask, mask], axis=0)
        lane = lax.broadcasted_iota(jnp.int32, (3 * BLK, 128), 1)
        k128 = _gather_keys(meta_ref, prev_ref, cur_ref, ATTN_WIDTH)
        v128 = _gather_keys(meta_ref, prev_ref, cur_ref, ATTN_WIDTH + KV_WIDTH)
        cos_b, sin_b = c_ref[...], s_ref[...]
        dk_heads, dv_heads = [], []
        for kv in range(2):
            k_lo, k_hi = _head_halves(k128, kv)
            v_lo, v_hi = _head_halves(v128, kv)
            lanes0, lanes1 = _pair_lanes(kv)
            q2 = jnp.concatenate([cur_ref[:, lanes0], cur_ref[:, lanes1]], axis=0)
            do2 = jnp.concatenate([do_ref[:, lanes0], do_ref[:, lanes1]], axis=0)
            ds_half, p_half = [], []
            for half, (k_h, v_h) in enumerate(((k_lo, v_lo), (k_hi, v_hi))):
                head = 4 * kv + half
                p, p_sink = _softmax_with_sink(_nt(q2, k_h), mask2, _stacked_sinks(sink_ref, head))
                dp = _nt(do2, v_h)
                dsum = jnp.sum(p * dp, axis=-1, keepdims=True)
                ds_half.append((p * (dp - dsum) * (HEAD_DIM ** -0.5)).astype(BF16))
                p_half.append(p.astype(BF16))
                dsink = p_sink * dsum
                for part, h in ((0, head), (1, head + 2)):
                    total = -jnp.sum(dsink[part * BLK:(part + 1) * BLK], axis=0, keepdims=True)
                    dsink_ref[h:h + 1, :] += jnp.broadcast_to(total, (1, 128))
            dq2 = _nn(ds_half[0], k_lo) + _nn(ds_half[1], k_hi)
            dqkv_ref[:, lanes0] = _rope(dq2[0:BLK], cos_b, sin_b, -1.0).astype(BF16)
            dqkv_ref[:, lanes1] = _rope(dq2[BLK:2 * BLK], cos_b, sin_b, -1.0).astype(BF16)
            dk_acc = jnp.where(lane < HEAD_DIM, _tn(ds_half[0], q2), _tn(ds_half[1], q2))
            dv_acc = jnp.where(lane < HEAD_DIM, _tn(p_half[0], do2), _tn(p_half[1], do2))
            dk_heads.append(dk_acc + pltpu.roll(dk_acc, HEAD_DIM, 1))
            dv_heads.append(dv_acc + pltpu.roll(dv_acc, HEAD_DIM, 1))
        dkv = jnp.concatenate([jnp.where(lane < HEAD_DIM, dk_heads[0], dk_heads[1]),
                               jnp.where(lane < HEAD_DIM, dv_heads[0], dv_heads[1])], axis=1)
        macc_ref[...] += dkv[0:BLK]
        is_last = (blk == 0).astype(F32)
        mine = dkv[2 * BLK:3 * BLK] + carry_ref[...] + is_last * macc_ref[...]
        carry_ref[...] = dkv[BLK:2 * BLK]
        dqkv_ref[:, ATTN_WIDTH:ATTN_WIDTH + KV_WIDTH] = _rope(mine[:, 0:128], cos_b, sin_b, -1.0).astype(BF16)
        dqkv_ref[:, ATTN_WIDTH + KV_WIDTH:QKV_WIDTH] = mine[:, 128:256].astype(BF16)

    rev = lambda i: nb - 1 - i
    blk_spec = lambda f: pl.BlockSpec((BLK, QKV_WIDTH), f)
    return pl.pallas_call(
        body, name=name, grid=(nb,),
        in_specs=[_SMEM, blk_spec(lambda i: (0, 0)), blk_spec(lambda i: (jnp.maximum(rev(i) - 1, 0), 0)),
                  blk_spec(lambda i: (rev(i), 0)), pl.BlockSpec((BLK, ATTN_WIDTH), lambda i: (rev(i), 0)),
                  pl.BlockSpec((BLK, 128), lambda i: (rev(i), 0)), pl.BlockSpec((BLK, 128), lambda i: (rev(i), 0))],
        out_specs=[pl.BlockSpec((BLK, QKV_WIDTH), lambda i: (rev(i), 0)), _acc_spec((N_Q_HEADS, 128))],
        out_shape=[jax.ShapeDtypeStruct((t, QKV_WIDTH), BF16), jax.ShapeDtypeStruct((N_Q_HEADS, 128), F32)],
        scratch_shapes=[pltpu.VMEM((BLK, 256), F32), pltpu.VMEM((BLK, 256), F32)],
        compiler_params=_params(("arbitrary",)),
    )(sinks, qkv, qkv, qkv, do, cos_t, sin_t)


def _cmul(ar, ai, br, bi):
    return ar * br - ai * bi, ar * bi + ai * br


def ssm_prep(a_re, a_im, log_dt, b_re_t, b_im_t, name):
    def body(ar_ref, ai_ref, ldt_ref, br_ref, bi_ref, lr_ref, li_ref, bbr_ref, bbi_ref):
        ar, ai = ar_ref[...], ai_ref[...]
        dt = jnp.exp(ldt_ref[...])
        mag = jnp.exp(ar * dt)
        lr = mag * jnp.cos(ai * dt)
        li = mag * jnp.sin(ai * dt)
        den = ar * ar + ai * ai
        nr = lr - 1.0
        cr = ((nr * ar + li * ai) / den)[:, None, :]
        ci = ((li * ar - nr * ai) / den)[:, None, :]
        br, bi = br_ref[...], bi_ref[...]
        lr_ref[...] = lr
        li_ref[...] = li
        bbr_ref[...] = cr * br - ci * bi
        bbi_ref[...] = cr * bi + ci * br

    gp = jax.ShapeDtypeStruct(a_re.shape, F32)
    gcp = jax.ShapeDtypeStruct(b_re_t.shape, F32)
    return pl.pallas_call(body, name=name, out_shape=[gp, gp, gcp, gcp],
                          in_specs=[_VMEM] * 5, out_specs=[_VMEM] * 4)(a_re, a_im, log_dt, b_re_t, b_im_t)


def ssm_prep_bwd(a_re, a_im, log_dt, b_re_t, b_im_t, dl_re, dl_im, dbb_re, dbb_im, name):
    def body(ar_ref, ai_ref, ldt_ref, br_ref, bi_ref, dlr_ref, dli_ref, dbbr_ref, dbbi_ref,
             dar_ref, dai_ref, dldt_ref, dbr_ref, dbi_ref):
        ar, ai = ar_ref[...], ai_ref[...]
        dt = jnp.exp(ldt_ref[...])
        mag = jnp.exp(ar * dt)
        lr = mag * jnp.cos(ai * dt)
        li = mag * jnp.sin(ai * dt)
        den = ar * ar + ai * ai
        nr = lr - 1.0
        cr = (nr * ar + li * ai) / den
        ci = (li * ar - nr * ai) / den
        br, bi = br_ref[...], bi_ref[...]
        dbbr, dbbi = dbbr_ref[...], dbbi_ref[...]
        dbr_ref[...] = cr[:, None, :] * dbbr + ci[:, None, :] * dbbi
        dbi_ref[...] = cr[:, None, :] * dbbi - ci[:, None, :] * dbbr
        dcr = jnp.sum(br * dbbr + bi * dbbi, axis=1)
        dci = jnp.sum(br * dbbi - bi * dbbr, axis=1)
        d_num_r = dcr / den
        d_num_i = dci / den
        d_den = -(dcr * cr + dci * ci) / den
        d_lr = dlr_ref[...] + d_num_r * ar - d_num_i * ai
        d_li = dli_ref[...] + d_num_r * ai + d_num_i * ar
        d_ar = d_num_r * nr + d_num_i * li + d_den * 2.0 * ar
        d_ai = d_num_r * li - d_num_i * nr + d_den * 2.0 * ai
        d_mag = (d_lr * lr + d_li * li) / mag
        d_theta = d_li * lr - d_lr * li
        d_ardt = d_mag * mag
        dar_ref[...] = d_ar + d_ardt * dt
        dai_ref[...] = d_ai + d_theta * dt
        d_dt = jnp.sum(d_ardt * ar + d_theta * ai, axis=1, keepdims=True)
        dldt_ref[...] = d_dt * dt

    gp = jax.ShapeDtypeStruct(a_re.shape, F32)
    gcp = jax.ShapeDtypeStruct(b_re_t.shape, F32)
    return pl.pallas_call(body, name=name, out_shape=[gp, gp, jax.ShapeDtypeStruct(log_dt.shape, F32), gcp, gcp],
                          in_specs=[_VMEM] * 9, out_specs=[_VMEM] * 5,
                          )(a_re, a_im, log_dt, b_re_t, b_im_t, dl_re, dl_im, dbb_re, dbb_im)


N_CHUNK = 4
U_CHUNK = SSM_WIDTH // N_CHUNK
H_CHUNK = STATE_WIDTH // N_CHUNK
SUB = 8


def _block_diag_b(bb):
    x = bb.reshape(N_CHUNK, 8, SSM_GROUP, 1, SSM_STATE)
    same = (jnp.arange(8)[:, None] == jnp.arange(8)[None, :])[None, :, None, :, None]
    return jnp.where(same, x, 0.0).reshape(N_CHUNK, U_CHUNK, H_CHUNK)


def _block_diag_c(c):
    x = jnp.swapaxes(c.reshape(N_CHUNK, 8, SSM_GROUP, SSM_STATE), 2, 3)[:, :, :, None, :]
    same = (jnp.arange(8)[:, None] == jnp.arange(8)[None, :])[None, :, None, :, None]
    return jnp.where(same, x, 0.0).reshape(N_CHUNK, H_CHUNK, U_CHUNK)


def _diag_of_b(m):
    x = m.reshape(N_CHUNK, 8, SSM_GROUP, 8, SSM_STATE)
    return jnp.stack([x[:, g, :, g, :] for g in range(8)], axis=1).reshape(SSM_GROUPS, SSM_GROUP, SSM_STATE)


def _diag_of_c(m):
    x = m.reshape(N_CHUNK, 8, SSM_STATE, 8, SSM_GROUP)
    d = jnp.stack([x[:, g, :, g, :] for g in range(8)], axis=1)
    return jnp.swapaxes(d, 2, 3).reshape(SSM_GROUPS, SSM_GROUP, SSM_STATE)


def _lambda_tables(lr, li, reverse):
    p1 = (lr, li)
    p2 = _cmul(*p1, *p1)
    p4 = _cmul(*p2, *p2)
    rows = [p1]
    for _ in range(SUB - 1):
        rows.append(_cmul(*rows[-1], *p1))
    if reverse:
        rows = rows[::-1]
    return p1, p2, p4, (jnp.concatenate([r[0] for r in rows], axis=0), jnp.concatenate([r[1] for r in rows], axis=0))


def _scan8(xr, xi, pows, table, cr, ci, reverse):
    row = lax.broadcasted_iota(jnp.int32, xr.shape, 0)
    for d, (pr, pi) in zip((1, 2, 4), pows):
        if reverse:
            sr, si = pltpu.roll(xr, SUB - d, 0), pltpu.roll(xi, SUB - d, 0)
            keep = row < SUB - d
        else:
            sr, si = pltpu.roll(xr, d, 0), pltpu.roll(xi, d, 0)
            keep = row >= d
        sr = jnp.where(keep, sr, 0.0)
        si = jnp.where(keep, si, 0.0)
        xr, xi = xr + pr * sr - pi * si, xi + pr * si + pi * sr
    tr, ti = table
    return xr + tr * cr - ti * ci, xi + tr * ci + ti * cr


def _gelu_and_grad(y):
    k0 = math.sqrt(2.0 / math.pi)
    inner = k0 * (y + 0.044715 * y * y * y)
    th = jnp.tanh(inner)
    g = 0.5 * y * (1.0 + th)
    dg = 0.5 * (1.0 + th) + 0.5 * y * (1.0 - th * th) * k0 * (1.0 + 3.0 * 0.044715 * y * y)
    return g, dg


SCAN_TILE = TOKEN_TILE
SEG = SCAN_TILE // SUB
SCAN_LANES = 512


def _perm_matrix(to_segments):
    a = lax.broadcasted_iota(jnp.int32, (SCAN_TILE, SCAN_TILE), 0)
    b = lax.broadcasted_iota(jnp.int32, (SCAN_TILE, SCAN_TILE), 1)
    rho, time = (a, b) if to_segments else (b, a)
    return (time == (rho % SUB) * SEG + rho // SUB).astype(BF16)


def _permute_f32(p, x):
    hi = x.astype(BF16)
    r1 = x - hi.astype(F32)
    mid = r1.astype(BF16)
    lo = (r1 - mid.astype(F32)).astype(BF16)
    return _nn(p, hi) + _nn(p, mid) + _nn(p, lo)


def _power_table(lr, li, pr_ref, pi_ref):
    cur = (lr, li)
    for r in range(SEG):
        pr_ref[r:r + 1, :] = cur[0]
        pi_ref[r:r + 1, :] = cur[1]
        cur = _cmul(*cur, lr, li)


def _segment_scan(xr_ref, xi_ref, lanes, lam, table_row, cr_ref, ci_ref, reverse, extra=None):
    lr, li = lam
    row = lax.broadcasted_iota(jnp.int32, (SUB, SCAN_LANES), 0)

    def rows_of(k):
        r = SEG - 1 - k if reverse else k
        return pl.ds(pl.multiple_of(r * SUB, SUB), SUB)

    def first(k, st):
        sr, si = st
        rows = rows_of(k)
        nr = lr * sr - li * si + xr_ref[rows, lanes]
        ni = lr * si + li * sr + xi_ref[rows, lanes]
        xr_ref[rows, lanes] = nr
        xi_ref[rows, lanes] = ni
        return nr, ni

    zero = jnp.zeros((SUB, SCAN_LANES), F32)
    er, ei = lax.fori_loop(0, SEG, first, (zero, zero))
    l16 = table_row(SEG - 1)
    q1, q2, q4, tab = _lambda_tables(l16[0], l16[1], reverse)
    c_r, c_i = cr_ref[:, lanes], ci_ref[:, lanes]
    gr, gi = _scan8(er, ei, (q1, q2, q4), tab, c_r, c_i, reverse)
    if reverse:
        cin_r = jnp.where(row == SUB - 1, c_r, pltpu.roll(gr, SUB - 1, 0))
        cin_i = jnp.where(row == SUB - 1, c_i, pltpu.roll(gi, SUB - 1, 0))
        cr_ref[:, lanes] = gr[0:1]
        ci_ref[:, lanes] = gi[0:1]
    else:
        cin_r = jnp.where(row == 0, c_r, pltpu.roll(gr, 1, 0))
        cin_i = jnp.where(row == 0, c_i, pltpu.roll(gi, 1, 0))
        cr_ref[:, lanes] = gr[SUB - 1:SUB]
        ci_ref[:, lanes] = gi[SUB - 1:SUB]

    def second(k, carry):
        rows = rows_of(k)
        tr, ti = table_row(k)
        ar = xr_ref[rows, lanes] + tr * cin_r - ti * cin_i
        ai = xi_ref[rows, lanes] + tr * cin_i + ti * cin_r
        xr_ref[rows, lanes] = ar
        xi_ref[rows, lanes] = ai
        if extra is None:
            return carry
        return extra(rows, carry, ar, ai)

    init = 0 if extra is None else (cin_r, cin_i, zero, zero)
    return lax.fori_loop(0, SEG, second, init)


def ssm_fwd(u, lam_re, lam_im, bb_re, bb_im, cc_re, cc_im, d_skip, name):
    t = u.shape[0]
    tt = SCAN_TILE

    def body(u_ref, lr_ref, li_ref, bbr_ref, bbi_ref, ccr_ref, cci_ref, d_ref, yg_ref, hr_ref, hi_ref,
             cr_ref, ci_ref, pr_ref, pi_ref, up_ref, y_ref):
        @pl.when(pl.program_id(0) == 0)
        def _():
            cr_ref[...] = jnp.zeros_like(cr_ref)
            ci_ref[...] = jnp.zeros_like(ci_ref)
            _power_table(lr_ref[...], li_ref[...], pr_ref, pi_ref)

        up_ref[...] = _permute_f32(_perm_matrix(True), u_ref[...])
        ub = up_ref[...].astype(BF16)
        for j in range(N_CHUNK):
            hs = slice(j * H_CHUNK, (j + 1) * H_CHUNK)
            us = slice(j * U_CHUNK, (j + 1) * U_CHUNK)
            hr_ref[:, hs] = _nn(ub[:, us], bbr_ref[j])
            hi_ref[:, hs] = _nn(ub[:, us], bbi_ref[j])
        for c in range(STATE_WIDTH // SCAN_LANES):
            lanes = slice(c * SCAN_LANES, (c + 1) * SCAN_LANES)
            _segment_scan(hr_ref, hi_ref, lanes, (lr_ref[:, lanes], li_ref[:, lanes]),
                          lambda k, lanes=lanes: (pr_ref[pl.ds(k, 1), lanes], pi_ref[pl.ds(k, 1), lanes]),
                          cr_ref, ci_ref, False)
        for j in range(N_CHUNK):
            hs = slice(j * H_CHUNK, (j + 1) * H_CHUNK)
            us = slice(j * U_CHUNK, (j + 1) * U_CHUNK)
            y = (_nn(hr_ref[:, hs].astype(BF16), ccr_ref[j]) - _nn(hi_ref[:, hs].astype(BF16), cci_ref[j])
                 + d_ref[:, us] * up_ref[:, us])
            y_ref[:, us] = _gelu_and_grad(y)[0]
        yg_ref[...] = _nn(_perm_matrix(False), y_ref[...].astype(BF16)).astype(BF16)

    return pl.pallas_call(
        body, name=name, grid=(t // tt,),
        in_specs=[_row_spec(tt, SSM_WIDTH), _VMEM, _VMEM, _VMEM, _VMEM, _VMEM, _VMEM, _VMEM],
        out_specs=[_row_spec(tt, SSM_WIDTH), _row_spec(tt, STATE_WIDTH), _row_spec(tt, STATE_WIDTH)],
        out_shape=[jax.ShapeDtypeStruct((t, SSM_WIDTH), BF16), jax.ShapeDtypeStruct((t, STATE_WIDTH), F32),
                   jax.ShapeDtypeStruct((t, STATE_WIDTH), F32)],
        scratch_shapes=[pltpu.VMEM((1, STATE_WIDTH), F32), pltpu.VMEM((1, STATE_WIDTH), F32),
                        pltpu.VMEM((SEG, STATE_WIDTH), F32), pltpu.VMEM((SEG, STATE_WIDTH), F32),
                        pltpu.VMEM((tt, SSM_WIDTH), F32), pltpu.VMEM((tt, SSM_WIDTH), F32)],
        compiler_params=_params(("arbitrary",)),
    )(u, lam_re, lam_im, bb_re, bb_im, cc_re, cc_im, d_skip)


def ssm_bwd(dyg, u, h_re, h_im, lam_re, lam_im, bb_re, bb_im, cc_re, cc_im, d_skip, name):
    t = u.shape[0]
    tt = SCAN_TILE
    nt = t // tt

    def body(dyg_ref, u_ref, hr_ref, hi_ref, lr_ref, li_ref, bbr_ref, bbi_ref, ccr_ref, cci_ref, d_ref,
             du_ref, dlr_ref, dli_ref, dbbr_ref, dbbi_ref, dccr_ref, dcci_ref, dd_ref,
             ar_ref, ai_ref, cr_ref, ci_ref, pr_ref, pi_ref, up_ref, dy_ref, dup_ref):
        step = pl.program_id(0)
        tile = nt - 1 - step

        @pl.when(step == 0)
        def _():
            for ref in (cr_ref, ci_ref, dlr_ref, dli_ref, dbbr_ref, dbbi_ref, dccr_ref, dcci_ref, dd_ref):
                ref[...] = jnp.zeros_like(ref)
            _power_table(lr_ref[...], li_ref[...], pr_ref, pi_ref)

        to_segments = _perm_matrix(True)
        up_ref[...] = _permute_f32(to_segments, u_ref[...])
        dy_ref[...] = _permute_f32(to_segments, dyg_ref[...])
        uv = up_ref[...]
        ub = uv.astype(BF16)
        dskip = d_ref[...]
        for j in range(N_CHUNK):
            hs = slice(j * H_CHUNK, (j + 1) * H_CHUNK)
            us = slice(j * U_CHUNK, (j + 1) * U_CHUNK)
            hrb = hr_ref[:, hs].astype(BF16)
            hib = hi_ref[:, hs].astype(BF16)
            y = _nn(hrb, ccr_ref[j]) - _nn(hib, cci_ref[j]) + dskip[:, us] * uv[:, us]
            dy = dy_ref[:, us] * _gelu_and_grad(y)[1]
            dy_ref[:, us] = dy
            dyb = dy.astype(BF16)
            dccr_ref[j] += _tn(hrb, dyb)
            dcci_ref[j] -= _tn(hib, dyb)
            ar_ref[:, hs] = _nt(dyb, ccr_ref[j])
            ai_ref[:, hs] = -_nt(dyb, cci_ref[j])
        dd_ref[...] += jnp.sum(dy_ref[...] * uv, axis=0, keepdims=True)

        for c in range(STATE_WIDTH // SCAN_LANES):
            lanes = slice(c * SCAN_LANES, (c + 1) * SCAN_LANES)

            def dlambda(rows, carry, ar, ai, lanes=lanes):
                nr, ni, accr, acci = carry
                hr, hi = hr_ref[rows, lanes], hi_ref[rows, lanes]
                return ar, ai, accr + nr * hr + ni * hi, acci + ni * hr - nr * hi

            _, _, accr, acci = _segment_scan(
                ar_ref, ai_ref, lanes, (lr_ref[:, lanes], -li_ref[:, lanes]),
                lambda k, lanes=lanes: (pr_ref[pl.ds(k, 1), lanes], -pi_ref[pl.ds(k, 1), lanes]),
                cr_ref, ci_ref, True, dlambda)
            dlr_ref[:, lanes] += accr
            dli_ref[:, lanes] += acci

        rho = lax.broadcasted_iota(jnp.int32, (tt, U_CHUNK), 0)
        time = tile * tt + (rho % SUB) * SEG + rho // SUB
        for j in range(N_CHUNK):
            hs = slice(j * H_CHUNK, (j + 1) * H_CHUNK)
            us = slice(j * U_CHUNK, (j + 1) * U_CHUNK)
            arb = ar_ref[:, hs].astype(BF16)
            aib = ai_ref[:, hs].astype(BF16)
            dbbr_ref[j] += _tn(ub[:, us], arb)
            dbbi_ref[j] += _tn(ub[:, us], aib)
            du = _nt(arb, bbr_ref[j]) + _nt(aib, bbi_ref[j]) + dy_ref[:, us] * dskip[:, us]
            dup_ref[:, us] = jnp.where(time >= PAD_FRONT, du, 0.0)
        du_ref[...] = _nn(_perm_matrix(False), dup_ref[...].astype(BF16)).astype(BF16)

    rev = lambda i: (nt - 1 - i, 0)
    full = lambda shape: pl.BlockSpec(shape, lambda i: (0,) * len(shape))
    return pl.pallas_call(
        body, name=name, grid=(nt,),
        in_specs=[pl.BlockSpec((tt, SSM_WIDTH), rev), pl.BlockSpec((tt, SSM_WIDTH), rev),
                  pl.BlockSpec((tt, STATE_WIDTH), rev), pl.BlockSpec((tt, STATE_WIDTH), rev),
                  _VMEM, _VMEM, _VMEM, _VMEM, _VMEM, _VMEM, _VMEM],
        out_specs=[pl.BlockSpec((tt, SSM_WIDTH), rev), full((SUB, STATE_WIDTH)), full((SUB, STATE_WIDTH)),
                   full((N_CHUNK, U_CHUNK, H_CHUNK)), full((N_CHUNK, U_CHUNK, H_CHUNK)),
                   full((N_CHUNK, H_CHUNK, U_CHUNK)), full((N_CHUNK, H_CHUNK, U_CHUNK)), full((1, SSM_WIDTH))],
        out_shape=[jax.ShapeDtypeStruct((t, SSM_WIDTH), BF16),
                   jax.ShapeDtypeStruct((SUB, STATE_WIDTH), F32), jax.ShapeDtypeStruct((SUB, STATE_WIDTH), F32),
                   jax.ShapeDtypeStruct((N_CHUNK, U_CHUNK, H_CHUNK), F32),
                   jax.ShapeDtypeStruct((N_CHUNK, U_CHUNK, H_CHUNK), F32),
                   jax.ShapeDtypeStruct((N_CHUNK, H_CHUNK, U_CHUNK), F32),
                   jax.ShapeDtypeStruct((N_CHUNK, H_CHUNK, U_CHUNK), F32),
                   jax.ShapeDtypeStruct((1, SSM_WIDTH), F32)],
        scratch_shapes=[pltpu.VMEM((tt, STATE_WIDTH), F32), pltpu.VMEM((tt, STATE_WIDTH), F32),
                        pltpu.VMEM((1, STATE_WIDTH), F32), pltpu.VMEM((1, STATE_WIDTH), F32),
                        pltpu.VMEM((SEG, STATE_WIDTH), F32), pltpu.VMEM((SEG, STATE_WIDTH), F32),
                        pltpu.VMEM((tt, SSM_WIDTH), F32), pltpu.VMEM((tt, SSM_WIDTH), F32),
                        pltpu.VMEM((tt, SSM_WIDTH), F32)],
        compiler_params=_params(("arbitrary",)),
    )(dyg, u, h_re, h_im, lam_re, lam_im, bb_re, bb_im, cc_re, cc_im, d_skip)


def merge_fwd(h, o, yg, gates, wap_t, wv_t, wgg_t, wout, name):
    t, d = h.shape
    tm = TOKEN_TILE

    def body(h_ref, o_ref, yg_ref, gt_ref, wap_ref, wv_ref, wgg_ref, wout_ref, ho_ref, mg_ref, a_ref, sv_ref, sg_ref):
        att = _nt(o_ref[...], wap_ref[...])
        ygv = yg_ref[...]
        sv = _nt(ygv, wv_ref[...])
        sg = _nt(ygv, wgg_ref[...])
        a_ref[...] = att.astype(BF16)
        sv_ref[...] = sv.astype(BF16)
        sg_ref[...] = sg.astype(BF16)
        merged = (jax.nn.sigmoid(gt_ref[:, 0:d].astype(F32)) * att
                  + jax.nn.sigmoid(gt_ref[:, d:2 * d].astype(F32)) * (sv * jax.nn.sigmoid(sg))).astype(BF16)
        mg_ref[...] = merged
        ho_ref[...] = h_ref[...] + _nn(merged, wout_ref[...])

    return pl.pallas_call(
        body, name=name, grid=(t // tm,),
        in_specs=[_row_spec(tm, d), _row_spec(tm, ATTN_WIDTH), _row_spec(tm, SSM_WIDTH), _row_spec(tm, 2 * d),
                  _VMEM, _VMEM, _VMEM, _VMEM],
        out_specs=[_row_spec(tm, d), _row_spec(tm, d), _row_spec(tm, d), _row_spec(tm, d), _row_spec(tm, d)],
        out_shape=[jax.ShapeDtypeStruct((t, d), F32), jax.ShapeDtypeStruct((t, d), BF16),
                   jax.ShapeDtypeStruct((t, d), BF16), jax.ShapeDtypeStruct((t, d), BF16),
                   jax.ShapeDtypeStruct((t, d), BF16)],
        compiler_params=_params(("arbitrary",)),
    )(h, o, yg, gates, wap_t, wv_t, wgg_t, wout)


def merge_bwd(dh, gates, att, sv, sg, wap_t, wv_t, wgg_t, wout, dep, name):
    t, d = dh.shape
    tm = TOKEN_TILE

    def body(dh_ref, gt_ref, a_ref, sv_ref, sg_ref, wap_ref, wv_ref, wgg_ref, wout_ref, dep_ref,
             dgt_ref, da_ref, dsv_ref, dsg_ref, do_ref, dyg_ref, dhb_ref):
        dhb = dh_ref[...].astype(BF16)
        dhb_ref[...] = dhb
        dm = _nt(dhb, wout_ref[...])
        sig_a = jax.nn.sigmoid(gt_ref[:, 0:d].astype(F32))
        sig_s = jax.nn.sigmoid(gt_ref[:, d:2 * d].astype(F32))
        sig_g = jax.nn.sigmoid(sg_ref[...].astype(F32))
        svv = sv_ref[...].astype(F32)
        dgt_ref[:, 0:d] = (dm * a_ref[...].astype(F32) * sig_a * (1.0 - sig_a)).astype(BF16)
        dgt_ref[:, d:2 * d] = (dm * (svv * sig_g) * sig_s * (1.0 - sig_s)).astype(BF16)
        da = (dm * sig_a).astype(BF16)
        d_s = dm * sig_s
        dsv = (d_s * sig_g).astype(BF16)
        dsg = (d_s * svv * sig_g * (1.0 - sig_g)).astype(BF16)
        da_ref[...] = da
        dsv_ref[...] = dsv
        dsg_ref[...] = dsg
        do_ref[...] = _nn(da, wap_ref[...]).astype(BF16)
        dyg_ref[...] = _nn(dsv, wv_ref[...]) + _nn(dsg, wgg_ref[...])

    return pl.pallas_call(
        body, name=name, grid=(t // tm,),
        in_specs=[_row_spec(tm, d), _row_spec(tm, 2 * d), _row_spec(tm, d), _row_spec(tm, d), _row_spec(tm, d),
                  _VMEM, _VMEM, _VMEM, _VMEM, _ANY],
        out_specs=[_row_spec(tm, 2 * d), _row_spec(tm, d), _row_spec(tm, d), _row_spec(tm, d),
                   _row_spec(tm, ATTN_WIDTH), _row_spec(tm, SSM_WIDTH), _row_spec(tm, d)],
        out_shape=[jax.ShapeDtypeStruct((t, 2 * d), BF16), jax.ShapeDtypeStruct((t, d), BF16),
                   jax.ShapeDtypeStruct((t, d), BF16), jax.ShapeDtypeStruct((t, d), BF16),
                   jax.ShapeDtypeStruct((t, ATTN_WIDTH), BF16), jax.ShapeDtypeStruct((t, SSM_WIDTH), F32),
                   jax.ShapeDtypeStruct((t, d), BF16)],
        compiler_params=_params(("arbitrary",)),
    )(dh, gates, att, sv, sg, wap_t, wv_t, wgg_t, wout, dep)


def _adamw_math(w, g, m, v):
    mn = ADAM_B1 * m + (1.0 - ADAM_B1) * g
    vn = ADAM_B2 * v + (1.0 - ADAM_B2) * (g * g)
    m_hat = mn / (1.0 - ADAM_B1 ** ADAM_STEP)
    v_hat = vn / (1.0 - ADAM_B2 ** ADAM_STEP)
    return -ADAM_LR * (m_hat / (jnp.sqrt(v_hat) + ADAM_EPS) + ADAM_WD * w), mn, vn


def adamw_layer(w, g, m, v, layer, prev, name):
    _, rows, cols = w.shape
    tr = rows
    for cand in (512, 256):
        if rows > cand and rows % cand == 0:
            tr = cand
            break

    def body(w_ref, g_ref, m_ref, v_ref, *rest):
        go_ref, d_ref, mo_ref, vo_ref = rest[-4:]
        gv = g_ref[...]
        go_ref[0] = gv
        d_ref[0], mo_ref[0], vo_ref[0] = _adamw_math(w_ref[0], gv, m_ref[0], v_ref[0])

    spec3 = pl.BlockSpec((1, tr, cols), lambda r: (layer, r, 0))
    out = jax.ShapeDtypeStruct(w.shape, F32)
    extra = [] if prev is None else list(prev)
    return pl.pallas_call(
        body, name=name, grid=(rows // tr,),
        in_specs=[spec3, _row_spec(tr, cols), spec3, spec3] + [_ANY] * len(extra),
        out_specs=[spec3] * 4, out_shape=[out] * 4,
        input_output_aliases={4 + j: j for j in range(len(extra))},
        compiler_params=_params(("arbitrary",)),
    )(w, g, m, v, *extra)


def adamw(w, g, m, v, name, minor_swap=False):
    if minor_swap:
        d, mn, vn = adamw(*[jnp.swapaxes(a, -1, -2) for a in (w, g, m, v)], name)
        return jnp.swapaxes(d, -1, -2), jnp.swapaxes(mn, -1, -2), jnp.swapaxes(vn, -1, -2)
    shape = w.shape
    as2d = lambda a: a.reshape(-1, shape[-1]) if a.ndim >= 2 else a.reshape(1, -1)
    w2, g2, m2, v2 = as2d(w), as2d(g), as2d(m), as2d(v)
    rows, cols = w2.shape
    tr = rows
    for cand in (1024, 704, 512, 256):
        if rows > cand and rows % cand == 0:
            tr = cand
            break

    def body(w_ref, g_ref, m_ref, v_ref, d_ref, mo_ref, vo_ref):
        d_ref[...], mo_ref[...], vo_ref[...] = _adamw_math(w_ref[...], g_ref[...], m_ref[...], v_ref[...])

    spec = _row_spec(tr, cols)
    out = jax.ShapeDtypeStruct((rows, cols), F32)
    d, mn, vn = pl.pallas_call(
        body, name=name, grid=(rows // tr,), in_specs=[spec] * 4, out_specs=[spec] * 3, out_shape=[out] * 3,
        compiler_params=_params(("arbitrary",)),
    )(w2, g2, m2, v2)
    return d.reshape(shape), mn.reshape(shape), vn.reshape(shape)


def _my_index():
    return 4 * lax.axis_index("x") + 2 * lax.axis_index("y") + lax.axis_index("c")


def _peer(p):
    return (lax.axis_index("x") ^ ((p >> 2) & 1), lax.axis_index("y") ^ ((p >> 1) & 1), lax.axis_index("c") ^ (p & 1))


_HBM = pl.BlockSpec(memory_space=pltpu.HBM)
_SEM = pl.BlockSpec(memory_space=pltpu.SEMAPHORE)
_EFFECT = pltpu.SideEffectType.DATAFLOW_SIDE_EFFECTING


class Exchange:
    def __init__(self, srcs, scatter, name):
        self.n = n = len(srcs)
        self.scatter = scatter
        self.name = name
        widths = sorted({s.shape[1] for s in srcs}, reverse=True)
        self.ncls = len(widths)
        self.cls = [widths.index(s.shape[1]) for s in srcs]
        self.cnts = [s.shape[0] // N_DEV if scatter else s.shape[0] for s in srcs]
        self.totals = [sum(c for c, k in zip(self.cnts, self.cls) if k == w) for w in range(self.ncls)]
        self.sizer = [max((k for k in range(n) if self.cls[k] == w), key=lambda k: self.cnts[k])
                      for w in range(self.ncls)]
        assert all(N_DEV * self.cnts[self.sizer[w]] >= self.totals[w] for w in range(self.ncls))
        if scatter:
            self.land_shapes = [(N_DEV, c, s.shape[1]) for s, c in zip(srcs, self.cnts)]
        else:
            self.land_shapes = [(N_DEV * c, s.shape[1]) for s, c in zip(srcs, self.cnts)]
        self.dtypes = [s.dtype for s in srcs]

    def _block(self, k, who):
        return pl.ds(pl.multiple_of(who * self.cnts[k], 16), self.cnts[k])

    def _sem(self, p, w):
        return (p - 1) * self.ncls + w

    def start(self, srcs, after):
        n = self.n

        def body(*refs):
            src, land = refs[:n], refs[n:2 * n]
            send_sems, recv_sems = refs[2 * n + 1], refs[2 * n + 2]
            token = refs[-1]
            me = _my_index()
            for p in range(1, N_DEV):
                for k in range(n):
                    if self.scatter:
                        s_ref, d_ref = src[k].at[self._block(k, me ^ p), :], land[k].at[me]
                    else:
                        s_ref, d_ref = src[k], land[k].at[self._block(k, me), :]
                    pltpu.make_async_remote_copy(
                        src_ref=s_ref, dst_ref=d_ref, send_sem=send_sems.at[self._sem(p, self.cls[k])],
                        recv_sem=recv_sems.at[self._sem(p, self.cls[k])], device_id=_peer(p),
                        device_id_type=MESH).start()
            token[...] = jnp.zeros_like(token)

        sems = pltpu.SemaphoreType.DMA(((N_DEV - 1) * self.ncls,))
        thru = [pltpu.HBM(s.shape, s.dtype) for s in srcs] + [pltpu.HBM(shp, dt) for shp, dt in
                                                               zip(self.land_shapes, self.dtypes)]
        lands = [pltpu.with_memory_space_constraint(lax.empty(shp, dt), pltpu.HBM)
                 for shp, dt in zip(self.land_shapes, self.dtypes)]
        out = pl.pallas_call(
            body, name=self.name + "_start",
            in_specs=[_HBM] * (2 * n) + [_ANY],
            out_shape=[sems, sems] + thru + [jax.ShapeDtypeStruct((8, 128), F32)],
            out_specs=[_SEM, _SEM] + [_HBM] * (2 * n) + [_VMEM],
            input_output_aliases={j: 2 + j for j in range(2 * n)},
            compiler_params=pltpu.CompilerParams(has_side_effects=_EFFECT),
        )(*[pltpu.with_memory_space_constraint(s, pltpu.HBM) for s in srcs], *lands, after)
        return out[:-1], out[-1]

    def wait(self, state, after):
        n = self.n
        send_sems, recv_sems = state[0], state[1]
        thru = state[2:]
        after = list(after) if isinstance(after, (list, tuple)) else [after]

        def body(*refs):
            src, land = refs[:n], refs[n:2 * n]
            send_sems, recv_sems = refs[2 * n], refs[2 * n + 1]
            for p in range(1, N_DEV):
                for w in range(self.ncls):
                    big = src[self.sizer[w]] if self.scatter else land[self.sizer[w]]
                    span = big.at[pl.ds(0, self.totals[w]), :]
                    copy = pltpu.make_async_remote_copy(
                        src_ref=span, dst_ref=span, send_sem=send_sems.at[self._sem(p, w)],
                        recv_sem=recv_sems.at[self._sem(p, w)],
                        device_id=_peer(p), device_id_type=MESH)
                    copy.wait_send()
                    copy.wait_recv()

        out = pl.pallas_call(
            body, name=self.name + "_wait",
            in_specs=[_HBM] * (2 * n) + [_SEM, _SEM] + [_ANY] * len(after),
            out_shape=[pltpu.HBM(a.shape, a.dtype) for a in thru], out_specs=[_HBM] * (2 * n),
            input_output_aliases={j: j for j in range(2 * n)},
            compiler_params=pltpu.CompilerParams(has_side_effects=_EFFECT),
        )(*thru, send_sems, recv_sems, *after)
        return out[:n], out[n:]

    def place(self, lands, srcs):
        n = self.n
        assert not self.scatter

        def body(*refs):
            src, land = refs[n:2 * n], refs[2 * n:3 * n]
            bufs, sems = refs[3 * n:4 * n], refs[-1]
            me = _my_index()
            loads = [pltpu.make_async_copy(src[k], bufs[k], sems.at[k]) for k in range(n)]
            stores = [pltpu.make_async_copy(bufs[k], land[k].at[self._block(k, me), :], sems.at[k]) for k in range(n)]
            for cp in loads:
                cp.start()
            for k in range(n):
                loads[k].wait()
                stores[k].start()
            for cp in stores:
                cp.wait()

        return pl.pallas_call(
            body, name=self.name + "_place", in_specs=[_ANY] * (2 * n), out_specs=[_ANY] * n,
            out_shape=[jax.ShapeDtypeStruct(a.shape, a.dtype) for a in lands],
            input_output_aliases={j: j for j in range(n)},
            scratch_shapes=[pltpu.VMEM(s.shape, s.dtype) for s in srcs] + [pltpu.SemaphoreType.DMA((n,))],
        )(*lands, *srcs)


def sum_blocks(landed, full, name):
    _, cnt, cols = landed.shape

    def body(land_ref, full_ref, o_ref, own_ref, sem):
        me = _my_index()
        own = pltpu.make_async_copy(full_ref.at[pl.ds(pl.multiple_of(me * cnt, 16), cnt), :], own_ref, sem)
        own.start()
        acc = land_ref[me ^ 1].astype(F32)
        for p in range(2, N_DEV):
            acc = acc + land_ref[me ^ p].astype(F32)
        own.wait()
        o_ref[...] = acc + own_ref[...].astype(F32)

    return pl.pallas_call(
        body, name=name, in_specs=[_VMEM, _ANY], out_specs=_VMEM,
        out_shape=jax.ShapeDtypeStruct((cnt, cols), F32),
        scratch_shapes=[pltpu.VMEM((cnt, cols), landed.dtype), pltpu.SemaphoreType.DMA],
        compiler_params=_params(),
    )(landed, full)


def sum_slots(slots, name):
    _, rows, cols = slots.shape
    tr = rows
    if rows > 512:
        for cand in (256, 128, 64, 32, 16, 8):
            if rows % cand == 0:
                tr = cand
                break

    def body(s_ref, o_ref):
        acc = s_ref[0].astype(F32)
        for j in range(1, N_DEV):
            acc = acc + s_ref[j].astype(F32)
        o_ref[...] = acc

    return pl.pallas_call(
        body, name=name, grid=(rows // tr,),
        in_specs=[pl.BlockSpec((N_DEV, tr, cols), lambda i: (0, i, 0))], out_specs=_row_spec(tr, cols),
        out_shape=jax.ShapeDtypeStruct((rows, cols), F32), compiler_params=_params(("arbitrary",)),
    )(slots)


BIG_T = ("ffn1_w_gate", "ffn1_w_up", "w_in", "ffn2_w_gate", "ffn2_w_up")
BIG_N = ("ffn1_w_down", "w_out", "ffn2_w_down")
HALF_T = ("w_attn_proj", "w_glu_v", "w_glu_g")
SMALL = ("ffn1_norm", "mix_norm", "attn_sinks", "ssm_a_re", "ssm_a_im", "ssm_log_dt", "ssm_b_re", "ssm_b_im",
         "ssm_c_re", "ssm_c_im", "ssm_d", "ffn2_norm", "final_norm")
PARTS = {"ffn1": ("ffn1_w_gate", "ffn1_w_up", "ffn1_w_down"),
         "mix": ("w_in", "w_out", "w_attn_proj", "w_glu_v", "w_glu_g"),
         "ffn2": ("ffn2_w_gate", "ffn2_w_up", "ffn2_w_down")}


def _to_rows(name, a):
    return a if name in BIG_N else jnp.swapaxes(a, -1, -2)


def local_step(x, tgt, get_weights, put_grads, small):
    seq, d = x.shape
    t = PAD_FRONT + N_META + seq
    cos_t, sin_t = rope_tables(t)
    row = lambda a: a.reshape(1, -1)
    tables = []
    for i in range(DEPTH):
        b_re_t = jnp.swapaxes(small["ssm_b_re"][i], 1, 2)
        b_im_t = jnp.swapaxes(small["ssm_b_im"][i], 1, 2)
        lam_re, lam_im, bbar_re, bbar_im = ssm_prep(small["ssm_a_re"][i], small["ssm_a_im"][i],
                                                    small["ssm_log_dt"][i].reshape(-1, 1), b_re_t, b_im_t, f"ssm_prep_{i}")
        tables.append(((b_re_t, b_im_t),
                       (row(lam_re), row(lam_im), _block_diag_b(bbar_re).astype(BF16), _block_diag_b(bbar_im).astype(BF16),
                        _block_diag_c(small["ssm_c_re"][i]).astype(BF16), _block_diag_c(small["ssm_c_im"][i]).astype(BF16),
                        row(small["ssm_d"][i]))))
    early = [cos_t, sin_t] + [a for _, tab in tables for a in tab[2:6]]
    saved = []
    h = None
    for i in range(DEPTH):
        s = {}
        w = dict(get_weights(i, "ffn1", early if i == 0 else h))
        if i == 0:
            h = jnp.concatenate([jnp.zeros((PAD_FRONT, d), F32), w["meta_tokens"], x], axis=0)
        s["h0"] = h
        h, s["n1"], s["a1"], s["b1"] = ffn_fwd(h, row(small["ffn1_norm"][i]), w["ffn1_w_gate"], w["ffn1_w_up"],
                                               w["ffn1_w_down"], f"ffn1_fwd_{i}")
        s["h1"] = h
        w.update(get_weights(i, "mix", h))
        s["n2"], s["qkv"], s["u"], s["gates"] = win_fwd(h, row(small["mix_norm"][i]), w["w_in"], cos_t, sin_t,
                                                        f"win_fwd_{i}")
        s["b_t"], s["ssm"] = tables[i]
        s["yg"], s["h_re"], s["h_im"] = ssm_fwd(s["u"], *s["ssm"], f"ssm_fwd_{i}")
        s["o"] = attn_fwd(s["qkv"], row(small["attn_sinks"][i]), f"attn_fwd_{i}")
        h, s["merged"], s["att"], s["sv"], s["sg"] = merge_fwd(
            h, s["o"], s["yg"], s["gates"], w["w_attn_proj"], w["w_glu_v"], w["w_glu_g"], w["w_out"],
            f"merge_fwd_{i}")
        s["h2"] = h
        w.update(get_weights(i, "ffn2", h))
        h, s["n3"], s["a3"], s["b3"] = ffn_fwd(h, row(small["ffn2_norm"][i]), w["ffn2_w_gate"], w["ffn2_w_up"],
                                               w["ffn2_w_down"], f"ffn2_fwd_{i}")
        s["w"] = w
        saved.append(s)

    loss, dh, d_final = head_fwd_bwd(h, row(small["final_norm"]), tgt)
    gs = {k: [None] * DEPTH for k in SMALL if k != "final_norm"}
    dep = loss
    for i in reversed(range(DEPTH)):
        s = saved[i]
        w = s["w"]
        dh, da, db, sact, dhb, dg = ffn_bwd(dh, s["h2"], row(small["ffn2_norm"][i]), s["a3"], s["b3"], w["ffn2_w_gate"],
                                            w["ffn2_w_up"], w["ffn2_w_down"], dep, f"ffn2_bwd_{i}")
        gs["ffn2_norm"][i] = dg[0]
        dep = put_grads(i, "ffn2", {"ffn2_w_gate": tn_matmul(da, s["n3"], f"ffn2_dwg_{i}"),
                                    "ffn2_w_up": tn_matmul(db, s["n3"], f"ffn2_dwu_{i}"),
                                    "ffn2_w_down": tn_matmul(sact, dhb, f"ffn2_dwd_{i}")})

        dgates, datt, dsv, dsg, do, dyg, dhb = merge_bwd(dh, s["gates"], s["att"], s["sv"], s["sg"], w["w_attn_proj"],
                                                         w["w_glu_v"], w["w_glu_g"], w["w_out"], dep, f"merge_bwd_{i}")
        gmix = {"w_out": tn_matmul(s["merged"], dhb, f"dwout_{i}"),
                "w_attn_proj": tn_matmul(datt, s["o"], f"dwap_{i}"),
                "w_glu_v": tn_matmul(dsv, s["yg"], f"dwv_{i}"),
                "w_glu_g": tn_matmul(dsg, s["yg"], f"dwgg_{i}")}
        dqkv, dsink = attn_bwd(s["qkv"], do, row(small["attn_sinks"][i]), cos_t, sin_t, f"attn_bwd_{i}")
        gs["attn_sinks"][i] = dsink[:, 0]
        du, dl_re, dl_im, dbb_re, dbb_im, dcc_re, dcc_im, dd = ssm_bwd(dyg, s["u"], s["h_re"], s["h_im"], *s["ssm"],
                                                                      f"ssm_bwd_{i}")
        fold = lambda a: jnp.sum(a, axis=0).reshape(SSM_GROUPS, SSM_STATE)
        da_re, da_im, dldt, db_re_t, db_im_t = ssm_prep_bwd(
            small["ssm_a_re"][i], small["ssm_a_im"][i], small["ssm_log_dt"][i].reshape(-1, 1), *s["b_t"],
            fold(dl_re), fold(dl_im), _diag_of_b(dbb_re), _diag_of_b(dbb_im), f"ssm_prep_bwd_{i}")
        gs["ssm_a_re"][i], gs["ssm_a_im"][i], gs["ssm_log_dt"][i] = da_re, da_im, dldt[:, 0]
        gs["ssm_b_re"][i], gs["ssm_b_im"][i] = jnp.swapaxes(db_re_t, 1, 2), jnp.swapaxes(db_im_t, 1, 2)
        gs["ssm_c_re"][i], gs["ssm_c_im"][i] = _diag_of_c(dcc_re), _diag_of_c(dcc_im)
        gs["ssm_d"][i] = dd[0]
        gmix["w_in"] = tn_matmul([dqkv, du, dgates], s["n2"], f"dwin_{i}")
        dep = put_grads(i, "mix", gmix)
        dh, dg = win_bwd(dh, s["h1"], row(small["mix_norm"][i]), dqkv, du, dgates, w["w_in"], dep, f"win_bwd_{i}")
        gs["mix_norm"][i] = dg[0]

        dh, da, db, sact, dhb, dg = ffn_bwd(dh, s["h0"], row(small["ffn1_norm"][i]), s["a1"], s["b1"], w["ffn1_w_gate"],
                                            w["ffn1_w_up"], w["ffn1_w_down"], dep, f"ffn1_bwd_{i}")
        gs["ffn1_norm"][i] = dg[0]
        if i > 0:
            dep = put_grads(i, "ffn1", {"ffn1_w_gate": tn_matmul(da, s["n1"], f"ffn1_dwg_{i}"),
                                        "ffn1_w_up": tn_matmul(db, s["n1"], f"ffn1_dwu_{i}"),
                                        "ffn1_w_down": tn_matmul(sact, dhb, f"ffn1_dwd_{i}")})
        else:
            for k, xa, ya in (("ffn1_w_down", sact, dhb), ("ffn1_w_gate", da, s["n1"]), ("ffn1_w_up", db, s["n1"])):
                dep = put_grads(i, "ffn1", {k: tn_matmul(xa, ya, f"d_{k}_{i}", dep)})

    gs = {k: jnp.stack(v) for k, v in gs.items()}
    gs["final_norm"] = d_final[0]
    return loss[0, 0], dh[PAD_FRONT + N_META:], dh[PAD_FRONT:PAD_FRONT + N_META], gs, dep


def _pack_rows(arrays, cols):
    flat = jnp.concatenate([a.reshape(-1) for a in arrays])
    rows = -(-flat.shape[0] // cols)
    rows = -(-rows // 16) * 16
    return jnp.pad(flat, (0, rows * cols - flat.shape[0])).reshape(rows, cols)


def _unpack_rows(packed, shapes):
    flat = packed.reshape(-1)
    out, off = [], 0
    for shp in shapes:
        n = math.prod(shp)
        out.append(flat[off:off + n].reshape(shp))
        off += n
    return out


def kernel(x, meta_tokens, ffn1_norm, ffn1_w_gate, ffn1_w_up, ffn1_w_down, mix_norm, w_in, attn_sinks, ssm_a_re, ssm_a_im, ssm_log_dt, ssm_b_re, ssm_b_im, ssm_c_re, ssm_c_im, ssm_d, w_attn_proj, w_glu_v, w_glu_g, w_out, ffn2_norm, ffn2_w_gate, ffn2_w_up, ffn2_w_down, final_norm, loss_target, m_meta_tokens, m_ffn1_norm, m_ffn1_w_gate, m_ffn1_w_up, m_ffn1_w_down, m_mix_norm, m_w_in, m_attn_sinks, m_ssm_a_re, m_ssm_a_im, m_ssm_log_dt, m_ssm_b_re, m_ssm_b_im, m_ssm_c_re, m_ssm_c_im, m_ssm_d, m_w_attn_proj, m_w_glu_v, m_w_glu_g, m_w_out, m_ffn2_norm, m_ffn2_w_gate, m_ffn2_w_up, m_ffn2_w_down, m_final_norm, v_meta_tokens, v_ffn1_norm, v_ffn1_w_gate, v_ffn1_w_up, v_ffn1_w_down, v_mix_norm, v_w_in, v_attn_sinks, v_ssm_a_re, v_ssm_a_im, v_ssm_log_dt, v_ssm_b_re, v_ssm_b_im, v_ssm_c_re, v_ssm_c_im, v_ssm_d, v_w_attn_proj, v_w_glu_v, v_w_glu_g, v_w_out, v_ffn2_norm, v_ffn2_w_gate, v_ffn2_w_up, v_ffn2_w_down, v_final_norm):
    names = ("meta_tokens", "ffn1_norm", "ffn1_w_gate", "ffn1_w_up", "ffn1_w_down", "mix_norm", "w_in", "attn_sinks",
             "ssm_a_re", "ssm_a_im", "ssm_log_dt", "ssm_b_re", "ssm_b_im", "ssm_c_re", "ssm_c_im", "ssm_d",
             "w_attn_proj", "w_glu_v", "w_glu_g", "w_out", "ffn2_norm", "ffn2_w_gate", "ffn2_w_up", "ffn2_w_down",
             "final_norm")
    weights = dict(zip(names, (meta_tokens, ffn1_norm, ffn1_w_gate, ffn1_w_up, ffn1_w_down, mix_norm, w_in, attn_sinks, ssm_a_re, ssm_a_im, ssm_log_dt, ssm_b_re, ssm_b_im, ssm_c_re, ssm_c_im, ssm_d, w_attn_proj, w_glu_v, w_glu_g, w_out, ffn2_norm, ffn2_w_gate, ffn2_w_up, ffn2_w_down, final_norm)))
    moments_m = dict(zip(names, (m_meta_tokens, m_ffn1_norm, m_ffn1_w_gate, m_ffn1_w_up, m_ffn1_w_down, m_mix_norm, m_w_in, m_attn_sinks, m_ssm_a_re, m_ssm_a_im, m_ssm_log_dt, m_ssm_b_re, m_ssm_b_im, m_ssm_c_re, m_ssm_c_im, m_ssm_d, m_w_attn_proj, m_w_glu_v, m_w_glu_g, m_w_out, m_ffn2_norm, m_ffn2_w_gate, m_ffn2_w_up, m_ffn2_w_down, m_final_norm)))
    moments_v = dict(zip(names, (v_meta_tokens, v_ffn1_norm, v_ffn1_w_gate, v_ffn1_w_up, v_ffn1_w_down, v_mix_norm, v_w_in, v_attn_sinks, v_ssm_a_re, v_ssm_a_im, v_ssm_log_dt, v_ssm_b_re, v_ssm_b_im, v_ssm_c_re, v_ssm_c_im, v_ssm_d, v_w_attn_proj, v_w_glu_v, v_w_glu_g, v_w_out, v_ffn2_norm, v_ffn2_w_gate, v_ffn2_w_up, v_ffn2_w_down, v_final_norm)))
    me = _my_index()

    gathers = {}
    token = jnp.zeros((8, 128), F32)
    for i in range(DEPTH):
        for part, ks in PARTS.items():
            shards = [_to_rows(k, weights[k][i]).astype(BF16) for k in ks]
            if (i, part) == (0, "ffn1"):
                shards.append(meta_tokens)
            ex = Exchange(shards, False, f"gather_{part}_{i}")
            state, token = ex.start(shards, token)
            gathers[i, part] = (ex, state, shards)
    all_started = token

    def get_weights(i, part, after):
        ex, state, shards = gathers[i, part]
        shards, lands = ex.wait(state, [all_started] + list(after) if (i, part) == (0, "ffn1") else after)
        fulls = ex.place(lands, shards)
        got = dict(zip(PARTS[part], fulls))
        if (i, part) == (0, "ffn1"):
            got["meta_tokens"] = jnp.swapaxes(fulls[-1].reshape(N_DEV, N_META, 128), 0, 1).reshape(N_META, D_MODEL)
        return got

    scatters = []

    def put_grads(i, part, gdict):
        ks = list(gdict)
        srcs = [gdict[k] for k in ks]
        ex = Exchange(srcs, True, f"scatter_{part if len(ks) > 1 else ks[0]}_{i}")
        state, tok = ex.start(srcs, all_started)
        scatters.append((i, ks, ex, state))
        return tok

    small = {k: weights[k] for k in SMALL}
    loss, dx, dmeta, gs, last_started = local_step(x[0], loss_target[0], get_weights, put_grads, small)

    grads, deltas, new_m, new_v = {}, {}, {}, {}
    small_list = [loss.reshape(1), dmeta] + [gs[k] for k in SMALL]
    packed = _pack_rows(small_list, D_MODEL)
    small_ex = Exchange([packed], False, "gather_small")
    small_state, after = small_ex.start([packed], last_started)

    updated = {}
    for i, ks, ex, state in scatters:
        partials, lands = ex.wait(state, after)
        for k, partial, slots in zip(ks, partials, lands):
            g = sum_blocks(slots, partial, f"sum_{k}_{i}")
            updated[k] = adamw_layer(_to_rows(k, weights[k]), g, _to_rows(k, moments_m[k]), _to_rows(k, moments_v[k]),
                                     i, updated.get(k), f"adamw_{k}_{i}")
            after = updated[k][0]
    for k, outs in updated.items():
        grads[k], deltas[k], new_m[k], new_v[k] = [_to_rows(k, a) for a in outs]

    packed_own, packed_all = small_ex.wait(small_state, after)
    (packed_all,) = small_ex.place(packed_all, packed_own)
    total = sum_slots(packed_all.reshape(N_DEV, packed.shape[0], D_MODEL), "sum_small")
    pieces = _unpack_rows(total, [a.shape for a in small_list])
    loss_out = pieces[0][0]
    grads["meta_tokens"] = lax.dynamic_slice_in_dim(pieces[1], me * 128, 128, axis=1)
    for k, p in zip(SMALL, pieces[2:]):
        grads[k] = p
    for k in ("meta_tokens",) + SMALL:
        deltas[k], new_m[k], new_v[k] = adamw(weights[k], grads[k], moments_m[k], moments_v[k], f"adamw_{k}",
                                              minor_swap=k in ("ssm_b_re", "ssm_b_im"))
    return (loss_out, dx[None], *[grads[k] for k in names], *[deltas[k] for k in names],
            *[new_m[k] for k in names], *[new_v[k] for k in names])
```

```python
import functools
import math

import jax
import jax.numpy as jnp
from jax import lax
from jax.experimental import pallas as pl
from jax.experimental.pallas import tpu as pltpu

F32 = jnp.float32
BF16 = jnp.bfloat16

D_MODEL = 1024
DEPTH = 2
N_META = 16
HEAD_DIM = 64
N_Q_HEADS = 8
ATTN_WIDTH = 512
KV_WIDTH = 128
QKV_WIDTH = ATTN_WIDTH + 2 * KV_WIDTH
WINDOW = 128
BLK = 128
ROPE_THETA = 500000.0
ROT_DIM = 16
SSM_WIDTH = 512
SSM_GROUP = 16
SSM_GROUPS = 32
SSM_STATE = 64
STATE_WIDTH = SSM_GROUPS * SSM_STATE
D_FF = 2816
IN_WIDTH = 3328
EPS = 1e-6
NEG_INF = -1e30
PAD_FRONT = (-N_META) % BLK
N_DEV = 8

ADAM_LR = 0.001
ADAM_B1 = 0.9
ADAM_B2 = 0.999
ADAM_EPS = 1e-08
ADAM_WD = 0.01
ADAM_STEP = 10

VMEM_LIMIT = 56 * 1024 * 1024
TOKEN_TILE = 384
_VMEM = pl.BlockSpec(memory_space=pltpu.VMEM)
_SMEM = pl.BlockSpec(memory_space=pltpu.SMEM)
_ANY = pl.BlockSpec(memory_space=pl.ANY)
MESH = pl.DeviceIdType.MESH


def _params(sem=None):
    return pltpu.CompilerParams(dimension_semantics=sem, vmem_limit_bytes=VMEM_LIMIT)


def _nt(a, b):
    return lax.dot_general(a, b, (((1,), (1,)), ((), ())), preferred_element_type=F32)


def _nn(a, b):
    return jnp.dot(a, b, preferred_element_type=F32)


def _tn(a, b):
    return lax.dot_general(a, b, (((0,), (0,)), ((), ())), preferred_element_type=F32)


def _row_spec(tm, width):
    return pl.BlockSpec((tm, width), lambda i: (i, 0))


def _acc_spec(shape):
    return pl.BlockSpec(shape, lambda i: (0,) * len(shape))


def _rms_stats(x):
    r = lax.rsqrt(jnp.mean(x * x, axis=-1, keepdims=True) + EPS)
    return x * r, r


def _rms_bwd(dn, xh, r, g):
    dg = jnp.sum(dn * xh, axis=0, keepdims=True)
    dxh = dn * g
    dx = r * (dxh - xh * jnp.mean(dxh * xh, axis=-1, keepdims=True))
    return dx, dg


def ffn_fwd(h, g, wg_t, wu_t, wd, name):
    t, d = h.shape
    f = wd.shape[0]
    tm = TOKEN_TILE

    def body(h_ref, g_ref, wg_ref, wu_ref, wd_ref, ho_ref, n_ref, a_ref, b_ref):
        x = h_ref[...]
        xh, _ = _rms_stats(x)
        n = (xh * g_ref[...]).astype(BF16)
        n_ref[...] = n
        a = _nt(n, wg_ref[...])
        b = _nt(n, wu_ref[...])
        a_ref[...] = a.astype(BF16)
        b_ref[...] = b.astype(BF16)
        s = (a * jax.nn.sigmoid(a) * b).astype(BF16)
        ho_ref[...] = x + 0.5 * _nn(s, wd_ref[...])

    return pl.pallas_call(
        body, name=name, grid=(t // tm,),
        in_specs=[_row_spec(tm, d), _acc_spec((1, d)), _VMEM, _VMEM, _VMEM],
        out_specs=[_row_spec(tm, d), _row_spec(tm, d), _row_spec(tm, f), _row_spec(tm, f)],
        out_shape=[jax.ShapeDtypeStruct((t, d), F32), jax.ShapeDtypeStruct((t, d), BF16),
                   jax.ShapeDtypeStruct((t, f), BF16), jax.ShapeDtypeStruct((t, f), BF16)],
        compiler_params=_params(("arbitrary",)),
    )(h, g, wg_t, wu_t, wd)


def ffn_bwd(dh, h, g, a, b, wg_t, wu_t, wd, dep, name):
    t, d = h.shape
    f = wd.shape[0]
    tm = TOKEN_TILE // 2

    def body(dh_ref, h_ref, g_ref, a_ref, b_ref, wg_ref, wu_ref, wd_ref, dep_ref,
             dhi_ref, da_ref, db_ref, s_ref, dhb_ref, dg_ref):
        dh_t = dh_ref[...]
        dhb = (0.5 * dh_t).astype(BF16)
        dhb_ref[...] = dhb
        ds = _nt(dhb, wd_ref[...])
        av = a_ref[...].astype(F32)
        bv = b_ref[...].astype(F32)
        sig = jax.nn.sigmoid(av)
        sl = av * sig
        s_ref[...] = (sl * bv).astype(BF16)
        da = (ds * bv * (sig * (1.0 + av * (1.0 - sig)))).astype(BF16)
        db = (ds * sl).astype(BF16)
        da_ref[...] = da
        db_ref[...] = db
        dn = _nn(da, wg_ref[...]) + _nn(db, wu_ref[...])
        xh, r = _rms_stats(h_ref[...])
        dx, dg = _rms_bwd(dn, xh, r, g_ref[...])
        dhi_ref[...] = dh_t + dx

        @pl.when(pl.program_id(0) == 0)
        def _():
            dg_ref[...] = jnp.zeros_like(dg_ref)

        dg_ref[...] += dg

    return pl.pallas_call(
        body, name=name, grid=(t // tm,),
        in_specs=[_row_spec(tm, d), _row_spec(tm, d), _acc_spec((1, d)), _row_spec(tm, f), _row_spec(tm, f),
                  _VMEM, _VMEM, _VMEM, _ANY],
        out_specs=[_row_spec(tm, d), _row_spec(tm, f), _row_spec(tm, f), _row_spec(tm, f), _row_spec(tm, d),
                   _acc_spec((1, d))],
        out_shape=[jax.ShapeDtypeStruct((t, d), F32), jax.ShapeDtypeStruct((t, f), BF16),
                   jax.ShapeDtypeStruct((t, f), BF16), jax.ShapeDtypeStruct((t, f), BF16),
                   jax.ShapeDtypeStruct((t, d), BF16), jax.ShapeDtypeStruct((1, d), F32)],
        compiler_params=_params(("arbitrary",)),
    )(dh, h, g, a, b, wg_t, wu_t, wd, dep)


DW_TILE = 256


def tn_matmul(x, y, name, dep=None):
    xs = list(x) if isinstance(x, (list, tuple)) else [x]
    t = xs[0].shape[0]
    n = y.shape[1]
    bm = DW_TILE
    tiles = [a.shape[1] // bm for a in xs]
    offs = [sum(tiles[:k]) for k in range(len(xs))]
    deps = [] if dep is None else [dep]

    def body(*refs):
        y_ref, o_ref = refs[len(xs)], refs[-1]
        i = pl.program_id(0)
        for k in range(len(xs)):
            @pl.when((i >= offs[k]) & (i < offs[k] + tiles[k]))
            def _(k=k):
                o_ref[...] = _tn(refs[k][...], y_ref[...]).astype(BF16)

    def x_spec(k):
        return pl.BlockSpec((t, bm), lambda i: (0, jnp.clip(i - offs[k], 0, tiles[k] - 1)))

    return pl.pallas_call(
        body, name=name, grid=(sum(tiles),),
        in_specs=[x_spec(k) for k in range(len(xs))] + [_VMEM] + [_ANY] * len(deps),
        out_specs=pl.BlockSpec((bm, n), lambda i: (i, 0)),
        out_shape=jax.ShapeDtypeStruct((sum(tiles) * bm, n), BF16),
        compiler_params=_params(("arbitrary",)),
    )(*xs, y, *deps)


def head_fwd_bwd(h, g, tgt):
    t, d = h.shape

    def body(h_ref, g_ref, t_ref, loss_ref, dh_ref, dg_ref):
        i = pl.program_id(0)
        xh, r = _rms_stats(h_ref[...])
        gv = g_ref[...]
        valid = (i > 0).astype(F32)
        e = (xh * gv - t_ref[...]) * valid
        dx, dg = _rms_bwd(e * (1.0 / d), xh, r, gv)
        dh_ref[...] = dx

        @pl.when(i == 0)
        def _():
            dg_ref[...] = jnp.zeros_like(dg_ref)
            loss_ref[...] = jnp.zeros_like(loss_ref)

        dg_ref[...] += dg
        loss_ref[...] += jnp.sum(e * e) * (0.5 / d)

    return pl.pallas_call(
        body, name="head", grid=(t // BLK,),
        in_specs=[_row_spec(BLK, d), _acc_spec((1, d)),
                  pl.BlockSpec((BLK, d), lambda i: (jnp.maximum(i - 1, 0), 0))],
        out_specs=[_acc_spec((1, 128)), _row_spec(BLK, d), _acc_spec((1, d))],
        out_shape=[jax.ShapeDtypeStruct((1, 128), F32), jax.ShapeDtypeStruct((t, d), F32),
                   jax.ShapeDtypeStruct((1, d), F32)],
        compiler_params=_params(("arbitrary",)),
    )(h, g, tgt)


def rope_tables(t):
    pos = jnp.arange(t, dtype=F32) - PAD_FRONT
    inv_freq = ROPE_THETA ** (-jnp.arange(0, ROT_DIM, 2, dtype=F32) / ROT_DIM)
    ang = pos[:, None] * inv_freq[None, :]
    cos, sin = jnp.cos(ang), jnp.sin(ang)
    ones = jnp.ones((t, HEAD_DIM - ROT_DIM), F32)
    cos_h = jnp.concatenate([cos, cos, ones], axis=1)
    sin_h = jnp.concatenate([-sin, sin, 0.0 * ones], axis=1)
    return jnp.concatenate([cos_h, cos_h], axis=1), jnp.concatenate([sin_h, sin_h], axis=1)


def _swap_halves(x):
    n = x.shape[1]
    lane = lax.broadcasted_iota(jnp.int32, x.shape, 1)
    return jnp.where(lane % HEAD_DIM < ROT_DIM // 2, pltpu.roll(x, n - ROT_DIM // 2, 1), pltpu.roll(x, ROT_DIM // 2, 1))


def _rope(x, cos_t, sin_t, sign):
    return x * cos_t + sign * (_swap_halves(x) * sin_t)


def win_fwd(h, g, win_t, cos_t, sin_t, name):
    t, d = h.shape
    tm = TOKEN_TILE

    def body(h_ref, g_ref, w_ref, c_ref, s_ref, n_ref, qkv_ref, u_ref, gates_ref):
        xh, _ = _rms_stats(h_ref[...])
        n = (xh * g_ref[...]).astype(BF16)
        n_ref[...] = n
        z = _nt(n, w_ref[...])
        c, s = c_ref[...], s_ref[...]
        for j in range((ATTN_WIDTH + KV_WIDTH) // 128):
            qkv_ref[:, j * 128:(j + 1) * 128] = _rope(z[:, j * 128:(j + 1) * 128], c, s, 1.0).astype(BF16)
        qkv_ref[:, ATTN_WIDTH + KV_WIDTH:QKV_WIDTH] = z[:, ATTN_WIDTH + KV_WIDTH:QKV_WIDTH].astype(BF16)
        u_ref[...] = z[:, QKV_WIDTH:QKV_WIDTH + SSM_WIDTH]
        gates_ref[...] = z[:, QKV_WIDTH + SSM_WIDTH:].astype(BF16)

    return pl.pallas_call(
        body, name=name, grid=(t // tm,),
        in_specs=[_row_spec(tm, d), _acc_spec((1, d)), _VMEM, _row_spec(tm, 128), _row_spec(tm, 128)],
        out_specs=[_row_spec(tm, d), _row_spec(tm, QKV_WIDTH), _row_spec(tm, SSM_WIDTH), _row_spec(tm, 2 * d)],
        out_shape=[jax.ShapeDtypeStruct((t, d), BF16), jax.ShapeDtypeStruct((t, QKV_WIDTH), BF16),
                   jax.ShapeDtypeStruct((t, SSM_WIDTH), F32), jax.ShapeDtypeStruct((t, 2 * d), BF16)],
        compiler_params=_params(("arbitrary",)),
    )(h, g, win_t, cos_t, sin_t)


def win_bwd(dh, h, g, dqkv, du, dgates, win_t, dep, name):
    t, d = h.shape
    tm = TOKEN_TILE

    def body(dh_ref, h_ref, g_ref, dqkv_ref, du_ref, dgt_ref, w_ref, dep_ref, dhi_ref, dg_ref):
        dn = (_nn(dqkv_ref[...], w_ref[0:QKV_WIDTH, :])
              + _nn(du_ref[...], w_ref[QKV_WIDTH:QKV_WIDTH + SSM_WIDTH, :])
              + _nn(dgt_ref[...], w_ref[QKV_WIDTH + SSM_WIDTH:, :]))
        xh, r = _rms_stats(h_ref[...])
        dx, dg = _rms_bwd(dn, xh, r, g_ref[...])
        dhi_ref[...] = dh_ref[...] + dx

        @pl.when(pl.program_id(0) == 0)
        def _():
            dg_ref[...] = jnp.zeros_like(dg_ref)

        dg_ref[...] += dg

    return pl.pallas_call(
        body, name=name, grid=(t // tm,),
        in_specs=[_row_spec(tm, d), _row_spec(tm, d), _acc_spec((1, d)), _row_spec(tm, QKV_WIDTH),
                  _row_spec(tm, SSM_WIDTH), _row_spec(tm, 2 * d), _VMEM, _ANY],
        out_specs=[_row_spec(tm, d), _acc_spec((1, d))],
        out_shape=[jax.ShapeDtypeStruct((t, d), F32), jax.ShapeDtypeStruct((1, d), F32)],
        compiler_params=_params(("arbitrary",)),
    )(dh, h, g, dqkv, du, dgates, win_t, dep)


def _attn_mask(blk):
    q_pos = blk * BLK + lax.broadcasted_iota(jnp.int32, (BLK, 3 * BLK), 0) - PAD_FRONT
    col = lax.broadcasted_iota(jnp.int32, (BLK, 3 * BLK), 1)
    part = col // BLK
    k_pos = jnp.where(part == 0, col, (blk + part - 2) * BLK + (col - part * BLK)) - PAD_FRONT
    dist = q_pos - k_pos
    meta_ok = (part == 0) & (k_pos >= 0) & (dist >= 0)
    band_ok = (part > 0) & (k_pos >= N_META) & (dist >= 0) & (dist < WINDOW)
    return meta_ok | band_ok


def _head_halves(x128, kv):
    x = x128.astype(F32)
    lane = lax.broadcasted_iota(jnp.int32, x.shape, 1)
    swapped = pltpu.roll(x, HEAD_DIM, 1)
    lo, hi = (x, swapped) if kv == 0 else (swapped, x)
    return jnp.where(lane < HEAD_DIM, lo, 0.0).astype(BF16), jnp.where(lane >= HEAD_DIM, hi, 0.0).astype(BF16)


def _gather_keys(meta_ref, prev_ref, cur_ref, lo):
    return jnp.concatenate([meta_ref[:, lo:lo + 128], prev_ref[:, lo:lo + 128], cur_ref[:, lo:lo + 128]], axis=0)


def _pair_lanes(kv):
    return slice(2 * kv * 128, (2 * kv + 1) * 128), slice((2 * kv + 1) * 128, (2 * kv + 2) * 128)


def _stacked_sinks(sink_ref, head):
    row = lax.broadcasted_iota(jnp.int32, (2 * BLK, 1), 0)
    return jnp.where(row < BLK, sink_ref[0, head], sink_ref[0, head + 2])


def _softmax_with_sink(s, mask, sink):
    s = jnp.where(mask, s * (HEAD_DIM ** -0.5), NEG_INF)
    m = jnp.maximum(jnp.max(s, axis=-1, keepdims=True), sink)
    p = jnp.exp(s - m)
    p_sink = jnp.exp(sink - m)
    inv = 1.0 / (jnp.sum(p, axis=-1, keepdims=True) + p_sink)
    return p * inv, p_sink * inv


def attn_fwd(qkv, sinks, name):
    t = qkv.shape[0]
    nb = t // BLK

    def body(sink_ref, meta_ref, prev_ref, cur_ref, o_ref):
        blk = pl.program_id(0)
        mask = _attn_mask(blk)
        mask2 = jnp.concatenate([mask, mask], axis=0)
        k128 = _gather_keys(meta_ref, prev_ref, cur_ref, ATTN_WIDTH)
        v128 = _gather_keys(meta_ref, prev_ref, cur_ref, ATTN_WIDTH + KV_WIDTH)
        for kv in range(2):
            k_lo, k_hi = _head_halves(k128, kv)
            v_lo, v_hi = _head_halves(v128, kv)
            lanes0, lanes1 = _pair_lanes(kv)
            q2 = jnp.concatenate([cur_ref[:, lanes0], cur_ref[:, lanes1]], axis=0)
            p_a, _ = _softmax_with_sink(_nt(q2, k_lo), mask2, _stacked_sinks(sink_ref, 4 * kv))
            p_b, _ = _softmax_with_sink(_nt(q2, k_hi), mask2, _stacked_sinks(sink_ref, 4 * kv + 1))
            o2 = (_nn(p_a.astype(BF16), v_lo) + _nn(p_b.astype(BF16), v_hi)).astype(BF16)
            o_ref[:, lanes0] = o2[0:BLK]
            o_ref[:, lanes1] = o2[BLK:2 * BLK]

    blk_spec = lambda f: pl.BlockSpec((BLK, QKV_WIDTH), f)
    return pl.pallas_call(
        body, name=name, grid=(nb,),
        in_specs=[_SMEM, blk_spec(lambda i: (0, 0)), blk_spec(lambda i: (jnp.maximum(i - 1, 0), 0)),
                  blk_spec(lambda i: (i, 0))],
        out_specs=_row_spec(BLK, ATTN_WIDTH),
        out_shape=jax.ShapeDtypeStruct((t, ATTN_WIDTH), BF16),
        compiler_params=_params(("arbitrary",)),
    )(sinks, qkv, qkv, qkv)


def attn_bwd(qkv, do, sinks, cos_t, sin_t, name):
    t = qkv.shape[0]
    nb = t // BLK

    def body(sink_ref, meta_ref, prev_ref, cur_ref, do_ref, c_ref, s_ref, dqkv_ref, dsink_ref, carry_ref, macc_ref):
        step = pl.program_id(0)
        blk = nb - 1 - step

        @pl.when(step == 0)
        def _():
            dsink_ref[...] = jnp.zeros_like(dsink_ref)
            carry_ref[...] = jnp.zeros_like(carry_ref)
            macc_ref[...] = jnp.zeros_like(macc_ref)

        mask = _attn_mask(blk)
        mask2 = jnp.concatenate([mask, mask], axis=0)
        lane = lax.broadcasted_iota(jnp.int32, (3 * BLK, 128), 1)
        k128 = _gather_keys(meta_ref, prev_ref, cur_ref, ATTN_WIDTH)
        v128 = _gather_keys(meta_ref, prev_ref, cur_ref, ATTN_WIDTH + KV_WIDTH)
        cos_b, sin_b = c_ref[...], s_ref[...]
        dk_heads, dv_heads = [], []
        for kv in range(2):
            k_lo, k_hi = _head_halves(k128, kv)
            v_lo, v_hi = _head_halves(v128, kv)
            lanes0, lanes1 = _pair_lanes(kv)
            q2 = jnp.concatenate([cur_ref[:, lanes0], cur_ref[:, lanes1]], axis=0)
            do2 = jnp.concatenate([do_ref[:, lanes0], do_ref[:, lanes1]], axis=0)
            ds_half, p_half = [], []
            for half, (k_h, v_h) in enumerate(((k_lo, v_lo), (k_hi, v_hi))):
                head = 4 * kv + half
                p, p_sink = _softmax_with_sink(_nt(q2, k_h), mask2, _stacked_sinks(sink_ref, head))
                dp = _nt(do2, v_h)
                dsum = jnp.sum(p * dp, axis=-1, keepdims=True)
                ds_half.append((p * (dp - dsum) * (HEAD_DIM ** -0.5)).astype(BF16))
                p_half.append(p.astype(BF16))
                dsink = p_sink * dsum
                for part, h in ((0, head), (1, head + 2)):
                    total = -jnp.sum(dsink[part * BLK:(part + 1) * BLK], axis=0, keepdims=True)
                    dsink_ref[h:h + 1, :] += jnp.broadcast_to(total, (1, 128))
            dq2 = _nn(ds_half[0], k_lo) + _nn(ds_half[1], k_hi)
            dqkv_ref[:, lanes0] = _rope(dq2[0:BLK], cos_b, sin_b, -1.0).astype(BF16)
            dqkv_ref[:, lanes1] = _rope(dq2[BLK:2 * BLK], cos_b, sin_b, -1.0).astype(BF16)
            dk_acc = jnp.where(lane < HEAD_DIM, _tn(ds_half[0], q2), _tn(ds_half[1], q2))
            dv_acc = jnp.where(lane < HEAD_DIM, _tn(p_half[0], do2), _tn(p_half[1], do2))
            dk_heads.append(dk_acc + pltpu.roll(dk_acc, HEAD_DIM, 1))
            dv_heads.append(dv_acc + pltpu.roll(dv_acc, HEAD_DIM, 1))
        dkv = jnp.concatenate([jnp.where(lane < HEAD_DIM, dk_heads[0], dk_heads[1]),
                               jnp.where(lane < HEAD_DIM, dv_heads[0], dv_heads[1])], axis=1)
        macc_ref[...] += dkv[0:BLK]
        is_last = (blk == 0).astype(F32)
        mine = dkv[2 * BLK:3 * BLK] + carry_ref[...] + is_last * macc_ref[...]
        carry_ref[...] = dkv[BLK:2 * BLK]
        dqkv_ref[:, ATTN_WIDTH:ATTN_WIDTH + KV_WIDTH] = _rope(mine[:, 0:128], cos_b, sin_b, -1.0).astype(BF16)
        dqkv_ref[:, ATTN_WIDTH + KV_WIDTH:QKV_WIDTH] = mine[:, 128:256].astype(BF16)

    rev = lambda i: nb - 1 - i
    blk_spec = lambda f: pl.BlockSpec((BLK, QKV_WIDTH), f)
    return pl.pallas_call(
        body, name=name, grid=(nb,),
        in_specs=[_SMEM, blk_spec(lambda i: (0, 0)), blk_spec(lambda i: (jnp.maximum(rev(i) - 1, 0), 0)),
                  blk_spec(lambda i: (rev(i), 0)), pl.BlockSpec((BLK, ATTN_WIDTH), lambda i: (rev(i), 0)),
                  pl.BlockSpec((BLK, 128), lambda i: (rev(i), 0)), pl.BlockSpec((BLK, 128), lambda i: (rev(i), 0))],
        out_specs=[pl.BlockSpec((BLK, QKV_WIDTH), lambda i: (rev(i), 0)), _acc_spec((N_Q_HEADS, 128))],
        out_shape=[jax.ShapeDtypeStruct((t, QKV_WIDTH), BF16), jax.ShapeDtypeStruct((N_Q_HEADS, 128), F32)],
        scratch_shapes=[pltpu.VMEM((BLK, 256), F32), pltpu.VMEM((BLK, 256), F32)],
        compiler_params=_params(("arbitrary",)),
    )(sinks, qkv, qkv, qkv, do, cos_t, sin_t)


def _cmul(ar, ai, br, bi):
    return ar * br - ai * bi, ar * bi + ai * br


def ssm_prep(a_re, a_im, log_dt, b_re_t, b_im_t, name):
    def body(ar_ref, ai_ref, ldt_ref, br_ref, bi_ref, lr_ref, li_ref, bbr_ref, bbi_ref):
        ar, ai = ar_ref[...], ai_ref[...]
        dt = jnp.exp(ldt_ref[...])
        mag = jnp.exp(ar * dt)
        lr = mag * jnp.cos(ai * dt)
        li = mag * jnp.sin(ai * dt)
        den = ar * ar + ai * ai
        nr = lr - 1.0
        cr = ((nr * ar + li * ai) / den)[:, None, :]
        ci = ((li * ar - nr * ai) / den)[:, None, :]
        br, bi = br_ref[...], bi_ref[...]
        lr_ref[...] = lr
        li_ref[...] = li
        bbr_ref[...] = cr * br - ci * bi
        bbi_ref[...] = cr * bi + ci * br

    gp = jax.ShapeDtypeStruct(a_re.shape, F32)
    gcp = jax.ShapeDtypeStruct(b_re_t.shape, F32)
    return pl.pallas_call(body, name=name, out_shape=[gp, gp, gcp, gcp],
                          in_specs=[_VMEM] * 5, out_specs=[_VMEM] * 4)(a_re, a_im, log_dt, b_re_t, b_im_t)


def ssm_prep_bwd(a_re, a_im, log_dt, b_re_t, b_im_t, dl_re, dl_im, dbb_re, dbb_im, name):
    def body(ar_ref, ai_ref, ldt_ref, br_ref, bi_ref, dlr_ref, dli_ref, dbbr_ref, dbbi_ref,
             dar_ref, dai_ref, dldt_ref, dbr_ref, dbi_ref):
        ar, ai = ar_ref[...], ai_ref[...]
        dt = jnp.exp(ldt_ref[...])
        mag = jnp.exp(ar * dt)
        lr = mag * jnp.cos(ai * dt)
        li = mag * jnp.sin(ai * dt)
        den = ar * ar + ai * ai
        nr = lr - 1.0
        cr = (nr * ar + li * ai) / den
        ci = (li * ar - nr * ai) / den
        br, bi = br_ref[...], bi_ref[...]
        dbbr, dbbi = dbbr_ref[...], dbbi_ref[...]
        dbr_ref[...] = cr[:, None, :] * dbbr + ci[:, None, :] * dbbi
        dbi_ref[...] = cr[:, None, :] * dbbi - ci[:, None, :] * dbbr
        dcr = jnp.sum(br * dbbr + bi * dbbi, axis=1)
        dci = jnp.sum(br * dbbi - bi * dbbr, axis=1)
        d_num_r = dcr / den
        d_num_i = dci / den
        d_den = -(dcr * cr + dci * ci) / den
        d_lr = dlr_ref[...] + d_num_r * ar - d_num_i * ai
        d_li = dli_ref[...] + d_num_r * ai + d_num_i * ar
        d_ar = d_num_r * nr + d_num_i * li + d_den * 2.0 * ar
        d_ai = d_num_r * li - d_num_i * nr + d_den * 2.0 * ai
        d_mag = (d_lr * lr + d_li * li) / mag
        d_theta = d_li * lr - d_lr * li
        d_ardt = d_mag * mag
        dar_ref[...] = d_ar + d_ardt * dt
        dai_ref[...] = d_ai + d_theta * dt
        d_dt = jnp.sum(d_ardt * ar + d_theta * ai, axis=1, keepdims=True)
        dldt_ref[...] = d_dt * dt

    gp = jax.ShapeDtypeStruct(a_re.shape, F32)
    gcp = jax.ShapeDtypeStruct(b_re_t.shape, F32)
    return pl.pallas_call(body, name=name, out_shape=[gp, gp, jax.ShapeDtypeStruct(log_dt.shape, F32), gcp, gcp],
                          in_specs=[_VMEM] * 9, out_specs=[_VMEM] * 5,
                          )(a_re, a_im, log_dt, b_re_t, b_im_t, dl_re, dl_im, dbb_re, dbb_im)


N_CHUNK = 4
U_CHUNK = SSM_WIDTH // N_CHUNK
H_CHUNK = STATE_WIDTH // N_CHUNK
SUB = 8


def _block_diag_b(bb):
    x = bb.reshape(N_CHUNK, 8, SSM_GROUP, 1, SSM_STATE)
    same = (jnp.arange(8)[:, None] == jnp.arange(8)[None, :])[None, :, None, :, None]
    return jnp.where(same, x, 0.0).reshape(N_CHUNK, U_CHUNK, H_CHUNK)


def _block_diag_c(c):
    x = jnp.swapaxes(c.reshape(N_CHUNK, 8, SSM_GROUP, SSM_STATE), 2, 3)[:, :, :, None, :]
    same = (jnp.arange(8)[:, None] == jnp.arange(8)[None, :])[None, :, None, :, None]
    return jnp.where(same, x, 0.0).reshape(N_CHUNK, H_CHUNK, U_CHUNK)


def _diag_of_b(m):
    x = m.reshape(N_CHUNK, 8, SSM_GROUP, 8, SSM_STATE)
    return jnp.stack([x[:, g, :, g, :] for g in range(8)], axis=1).reshape(SSM_GROUPS, SSM_GROUP, SSM_STATE)


def _diag_of_c(m):
    x = m.reshape(N_CHUNK, 8, SSM_STATE, 8, SSM_GROUP)
    d = jnp.stack([x[:, g, :, g, :] for g in range(8)], axis=1)
    return jnp.swapaxes(d, 2, 3).reshape(SSM_GROUPS, SSM_GROUP, SSM_STATE)


def _lambda_tables(lr, li, reverse):
    p1 = (lr, li)
    p2 = _cmul(*p1, *p1)
    p4 = _cmul(*p2, *p2)
    rows = [p1]
    for _ in range(SUB - 1):
        rows.append(_cmul(*rows[-1], *p1))
    if reverse:
        rows = rows[::-1]
    return p1, p2, p4, (jnp.concatenate([r[0] for r in rows], axis=0), jnp.concatenate([r[1] for r in rows], axis=0))


def _scan8(xr, xi, pows, table, cr, ci, reverse):
    row = lax.broadcasted_iota(jnp.int32, xr.shape, 0)
    for d, (pr, pi) in zip((1, 2, 4), pows):
        if reverse:
            sr, si = pltpu.roll(xr, SUB - d, 0), pltpu.roll(xi, SUB - d, 0)
            keep = row < SUB - d
        else:
            sr, si = pltpu.roll(xr, d, 0), pltpu.roll(xi, d, 0)
            keep = row >= d
        sr = jnp.where(keep, sr, 0.0)
        si = jnp.where(keep, si, 0.0)
        xr, xi = xr + pr * sr - pi * si, xi + pr * si + pi * sr
    tr, ti = table
    return xr + tr * cr - ti * ci, xi + tr * ci + ti * cr


def _gelu_and_grad(y):
    k0 = math.sqrt(2.0 / math.pi)
    inner = k0 * (y + 0.044715 * y * y * y)
    th = jnp.tanh(inner)
    g = 0.5 * y * (1.0 + th)
    dg = 0.5 * (1.0 + th) + 0.5 * y * (1.0 - th * th) * k0 * (1.0 + 3.0 * 0.044715 * y * y)
    return g, dg


SCAN_TILE = TOKEN_TILE
SEG = SCAN_TILE // SUB
SCAN_LANES = 512


def _perm_matrix(to_segments):
    a = lax.broadcasted_iota(jnp.int32, (SCAN_TILE, SCAN_TILE), 0)
    b = lax.broadcasted_iota(jnp.int32, (SCAN_TILE, SCAN_TILE), 1)
    rho, time = (a, b) if to_segments else (b, a)
    return (time == (rho % SUB) * SEG + rho // SUB).astype(BF16)


def _permute_f32(p, x):
    hi = x.astype(BF16)
    r1 = x - hi.astype(F32)
    mid = r1.astype(BF16)
    lo = (r1 - mid.astype(F32)).astype(BF16)
    return _nn(p, hi) + _nn(p, mid) + _nn(p, lo)


def _power_table(lr, li, pr_ref, pi_ref):
    cur = (lr, li)
    for r in range(SEG):
        pr_ref[r:r + 1, :] = cur[0]
        pi_ref[r:r + 1, :] = cur[1]
        cur = _cmul(*cur, lr, li)


def _segment_scan(xr_ref, xi_ref, lanes, lam, table_row, cr_ref, ci_ref, reverse, extra=None):
    lr, li = lam
    row = lax.broadcasted_iota(jnp.int32, (SUB, SCAN_LANES), 0)

    def rows_of(k):
        r = SEG - 1 - k if reverse else k
        return pl.ds(pl.multiple_of(r * SUB, SUB), SUB)

    def first(k, st):
        sr, si = st
        rows = rows_of(k)
        nr = lr * sr - li * si + xr_ref[rows, lanes]
        ni = lr * si + li * sr + xi_ref[rows, lanes]
        xr_ref[rows, lanes] = nr
        xi_ref[rows, lanes] = ni
        return nr, ni

    zero = jnp.zeros((SUB, SCAN_LANES), F32)
    er, ei = lax.fori_loop(0, SEG, first, (zero, zero))
    l16 = table_row(SEG - 1)
    q1, q2, q4, tab = _lambda_tables(l16[0], l16[1], reverse)
    c_r, c_i = cr_ref[:, lanes], ci_ref[:, lanes]
    gr, gi = _scan8(er, ei, (q1, q2, q4), tab, c_r, c_i, reverse)
    if reverse:
        cin_r = jnp.where(row == SUB - 1, c_r, pltpu.roll(gr, SUB - 1, 0))
        cin_i = jnp.where(row == SUB - 1, c_i, pltpu.roll(gi, SUB - 1, 0))
        cr_ref[:, lanes] = gr[0:1]
        ci_ref[:, lanes] = gi[0:1]
    else:
        cin_r = jnp.where(row == 0, c_r, pltpu.roll(gr, 1, 0))
        cin_i = jnp.where(row == 0, c_i, pltpu.roll(gi, 1, 0))
        cr_ref[:, lanes] = gr[SUB - 1:SUB]
        ci_ref[:, lanes] = gi[SUB - 1:SUB]

    def second(k, carry):
        rows = rows_of(k)
        tr, ti = table_row(k)
        ar = xr_ref[rows, lanes] + tr * cin_r - ti * cin_i
        ai = xi_ref[rows, lanes] + tr * cin_i + ti * cin_r
        xr_ref[rows, lanes] = ar
        xi_ref[rows, lanes] = ai
        if extra is None:
            return carry
        return extra(rows, carry, ar, ai)

    init = 0 if extra is None else (cin_r, cin_i, zero, zero)
    return lax.fori_loop(0, SEG, second, init)


def ssm_fwd(u, lam_re, lam_im, bb_re, bb_im, cc_re, cc_im, d_skip, name):
    t = u.shape[0]
    tt = SCAN_TILE

    def body(u_ref, lr_ref, li_ref, bbr_ref, bbi_ref, ccr_ref, cci_ref, d_ref, yg_ref, hr_ref, hi_ref,
             cr_ref, ci_ref, pr_ref, pi_ref, up_ref, y_ref):
        @pl.when(pl.program_id(0) == 0)
        def _():
            cr_ref[...] = jnp.zeros_like(cr_ref)
            ci_ref[...] = jnp.zeros_like(ci_ref)
            _power_table(lr_ref[...], li_ref[...], pr_ref, pi_ref)

        up_ref[...] = _permute_f32(_perm_matrix(True), u_ref[...])
        ub = up_ref[...].astype(BF16)
        for j in range(N_CHUNK):
            hs = slice(j * H_CHUNK, (j + 1) * H_CHUNK)
            us = slice(j * U_CHUNK, (j + 1) * U_CHUNK)
            hr_ref[:, hs] = _nn(ub[:, us], bbr_ref[j])
            hi_ref[:, hs] = _nn(ub[:, us], bbi_ref[j])
        for c in range(STATE_WIDTH // SCAN_LANES):
            lanes = slice(c * SCAN_LANES, (c + 1) * SCAN_LANES)
            _segment_scan(hr_ref, hi_ref, lanes, (lr_ref[:, lanes], li_ref[:, lanes]),
                          lambda k, lanes=lanes: (pr_ref[pl.ds(k, 1), lanes], pi_ref[pl.ds(k, 1), lanes]),
                          cr_ref, ci_ref, False)
        for j in range(N_CHUNK):
            hs = slice(j * H_CHUNK, (j + 1) * H_CHUNK)
            us = slice(j * U_CHUNK, (j + 1) * U_CHUNK)
            y = (_nn(hr_ref[:, hs].astype(BF16), ccr_ref[j]) - _nn(hi_ref[:, hs].astype(BF16), cci_ref[j])
                 + d_ref[:, us] * up_ref[:, us])
            y_ref[:, us] = _gelu_and_grad(y)[0]
        yg_ref[...] = _nn(_perm_matrix(False), y_ref[...].astype(BF16)).astype(BF16)

    return pl.pallas_call(
        body, name=name, grid=(t // tt,),
        in_specs=[_row_spec(tt, SSM_WIDTH), _VMEM, _VMEM, _VMEM, _VMEM, _VMEM, _VMEM, _VMEM],
        out_specs=[_row_spec(tt, SSM_WIDTH), _row_spec(tt, STATE_WIDTH), _row_spec(tt, STATE_WIDTH)],
        out_shape=[jax.ShapeDtypeStruct((t, SSM_WIDTH), BF16), jax.ShapeDtypeStruct((t, STATE_WIDTH), F32),
                   jax.ShapeDtypeStruct((t, STATE_WIDTH), F32)],
        scratch_shapes=[pltpu.VMEM((1, STATE_WIDTH), F32), pltpu.VMEM((1, STATE_WIDTH), F32),
                        pltpu.VMEM((SEG, STATE_WIDTH), F32), pltpu.VMEM((SEG, STATE_WIDTH), F32),
                        pltpu.VMEM((tt, SSM_WIDTH), F32), pltpu.VMEM((tt, SSM_WIDTH), F32)],
        compiler_params=_params(("arbitrary",)),
    )(u, lam_re, lam_im, bb_re, bb_im, cc_re, cc_im, d_skip)


def ssm_bwd(dyg, u, h_re, h_im, lam_re, lam_im, bb_re, bb_im, cc_re, cc_im, d_skip, name):
    t = u.shape[0]
    tt = SCAN_TILE
    nt = t // tt

    def body(dyg_ref, u_ref, hr_ref, hi_ref, lr_ref, li_ref, bbr_ref, bbi_ref, ccr_ref, cci_ref, d_ref,
             du_ref, dlr_ref, dli_ref, dbbr_ref, dbbi_ref, dccr_ref, dcci_ref, dd_ref,
             ar_ref, ai_ref, cr_ref, ci_ref, pr_ref, pi_ref, up_ref, dy_ref, dup_ref):
        step = pl.program_id(0)
        tile = nt - 1 - step

        @pl.when(step == 0)
        def _():
            for ref in (cr_ref, ci_ref, dlr_ref, dli_ref, dbbr_ref, dbbi_ref, dccr_ref, dcci_ref, dd_ref):
                ref[...] = jnp.zeros_like(ref)
            _power_table(lr_ref[...], li_ref[...], pr_ref, pi_ref)

        to_segments = _perm_matrix(True)
        up_ref[...] = _permute_f32(to_segments, u_ref[...])
        dy_ref[...] = _permute_f32(to_segments, dyg_ref[...])
        uv = up_ref[...]
        ub = uv.astype(BF16)
        dskip = d_ref[...]
        for j in range(N_CHUNK):
            hs = slice(j * H_CHUNK, (j + 1) * H_CHUNK)
            us = slice(j * U_CHUNK, (j + 1) * U_CHUNK)
            hrb = hr_ref[:, hs].astype(BF16)
            hib = hi_ref[:, hs].astype(BF16)
            y = _nn(hrb, ccr_ref[j]) - _nn(hib, cci_ref[j]) + dskip[:, us] * uv[:, us]
            dy = dy_ref[:, us] * _gelu_and_grad(y)[1]
            dy_ref[:, us] = dy
            dyb = dy.astype(BF16)
            dccr_ref[j] += _tn(hrb, dyb)
            dcci_ref[j] -= _tn(hib, dyb)
            ar_ref[:, hs] = _nt(dyb, ccr_ref[j])
            ai_ref[:, hs] = -_nt(dyb, cci_ref[j])
        dd_ref[...] += jnp.sum(dy_ref[...] * uv, axis=0, keepdims=True)

        for c in range(STATE_WIDTH // SCAN_LANES):
            lanes = slice(c * SCAN_LANES, (c + 1) * SCAN_LANES)

            def dlambda(rows, carry, ar, ai, lanes=lanes):
                nr, ni, accr, acci = carry
                hr, hi = hr_ref[rows, lanes], hi_ref[rows, lanes]
                return ar, ai, accr + nr * hr + ni * hi, acci + ni * hr - nr * hi

            _, _, accr, acci = _segment_scan(
                ar_ref, ai_ref, lanes, (lr_ref[:, lanes], -li_ref[:, lanes]),
                lambda k, lanes=lanes: (pr_ref[pl.ds(k, 1), lanes], -pi_ref[pl.ds(k, 1), lanes]),
                cr_ref, ci_ref, True, dlambda)
            dlr_ref[:, lanes] += accr
            dli_ref[:, lanes] += acci

        rho = lax.broadcasted_iota(jnp.int32, (tt, U_CHUNK), 0)
        time = tile * tt + (rho % SUB) * SEG + rho // SUB
        for j in range(N_CHUNK):
            hs = slice(j * H_CHUNK, (j + 1) * H_CHUNK)
            us = slice(j * U_CHUNK, (j + 1) * U_CHUNK)
            arb = ar_ref[:, hs].astype(BF16)
            aib = ai_ref[:, hs].astype(BF16)
            dbbr_ref[j] += _tn(ub[:, us], arb)
            dbbi_ref[j] += _tn(ub[:, us], aib)
            du = _nt(arb, bbr_ref[j]) + _nt(aib, bbi_ref[j]) + dy_ref[:, us] * dskip[:, us]
            dup_ref[:, us] = jnp.where(time >= PAD_FRONT, du, 0.0)
        du_ref[...] = _nn(_perm_matrix(False), dup_ref[...].astype(BF16)).astype(BF16)

    rev = lambda i: (nt - 1 - i, 0)
    full = lambda shape: pl.BlockSpec(shape, lambda i: (0,) * len(shape))
    return pl.pallas_call(
        body, name=name, grid=(nt,),
        in_specs=[pl.BlockSpec((tt, SSM_WIDTH), rev), pl.BlockSpec((tt, SSM_WIDTH), rev),
                  pl.BlockSpec((tt, STATE_WIDTH), rev), pl.BlockSpec((tt, STATE_WIDTH), rev),
                  _VMEM, _VMEM, _VMEM, _VMEM, _VMEM, _VMEM, _VMEM],
        out_specs=[pl.BlockSpec((tt, SSM_WIDTH), rev), full((SUB, STATE_WIDTH)), full((SUB, STATE_WIDTH)),
                   full((N_CHUNK, U_CHUNK, H_CHUNK)), full((N_CHUNK, U_CHUNK, H_CHUNK)),
                   full((N_CHUNK, H_CHUNK, U_CHUNK)), full((N_CHUNK, H_CHUNK, U_CHUNK)), full((1, SSM_WIDTH))],
        out_shape=[jax.ShapeDtypeStruct((t, SSM_WIDTH), BF16),
                   jax.ShapeDtypeStruct((SUB, STATE_WIDTH), F32), jax.ShapeDtypeStruct((SUB, STATE_WIDTH), F32),
                   jax.ShapeDtypeStruct((N_CHUNK, U_CHUNK, H_CHUNK), F32),
                   jax.ShapeDtypeStruct((N_CHUNK, U_CHUNK, H_CHUNK), F32),
                   jax.ShapeDtypeStruct((N_CHUNK, H_CHUNK, U_CHUNK), F32),
                   jax.ShapeDtypeStruct((N_CHUNK, H_CHUNK, U_CHUNK), F32),
                   jax.ShapeDtypeStruct((1, SSM_WIDTH), F32)],
        scratch_shapes=[pltpu.VMEM((tt, STATE_WIDTH), F32), pltpu.VMEM((tt, STATE_WIDTH), F32),
                        pltpu.VMEM((1, STATE_WIDTH), F32), pltpu.VMEM((1, STATE_WIDTH), F32),
                        pltpu.VMEM((SEG, STATE_WIDTH), F32), pltpu.VMEM((SEG, STATE_WIDTH), F32),
                        pltpu.VMEM((tt, SSM_WIDTH), F32), pltpu.VMEM((tt, SSM_WIDTH), F32),
                        pltpu.VMEM((tt, SSM_WIDTH), F32)],
        compiler_params=_params(("arbitrary",)),
    )(dyg, u, h_re, h_im, lam_re, lam_im, bb_re, bb_im, cc_re, cc_im, d_skip)


def merge_fwd(h, o, yg, gates, wap_t, wv_t, wgg_t, wout, name):
    t, d = h.shape
    tm = TOKEN_TILE

    def body(h_ref, o_ref, yg_ref, gt_ref, wap_ref, wv_ref, wgg_ref, wout_ref, ho_ref, mg_ref, a_ref, sv_ref, sg_ref):
        att = _nt(o_ref[...], wap_ref[...])
        ygv = yg_ref[...]
        sv = _nt(ygv, wv_ref[...])
        sg = _nt(ygv, wgg_ref[...])
        a_ref[...] = att.astype(BF16)
        sv_ref[...] = sv.astype(BF16)
        sg_ref[...] = sg.astype(BF16)
        merged = (jax.nn.sigmoid(gt_ref[:, 0:d].astype(F32)) * att
                  + jax.nn.sigmoid(gt_ref[:, d:2 * d].astype(F32)) * (sv * jax.nn.sigmoid(sg))).astype(BF16)
        mg_ref[...] = merged
        ho_ref[...] = h_ref[...] + _nn(merged, wout_ref[...])

    return pl.pallas_call(
        body, name=name, grid=(t // tm,),
        in_specs=[_row_spec(tm, d), _row_spec(tm, ATTN_WIDTH), _row_spec(tm, SSM_WIDTH), _row_spec(tm, 2 * d),
                  _VMEM, _VMEM, _VMEM, _VMEM],
        out_specs=[_row_spec(tm, d), _row_spec(tm, d), _row_spec(tm, d), _row_spec(tm, d), _row_spec(tm, d)],
        out_shape=[jax.ShapeDtypeStruct((t, d), F32), jax.ShapeDtypeStruct((t, d), BF16),
                   jax.ShapeDtypeStruct((t, d), BF16), jax.ShapeDtypeStruct((t, d), BF16),
                   jax.ShapeDtypeStruct((t, d), BF16)],
        compiler_params=_params(("arbitrary",)),
    )(h, o, yg, gates, wap_t, wv_t, wgg_t, wout)


def merge_bwd(dh, gates, att, sv, sg, wap_t, wv_t, wgg_t, wout, dep, name):
    t, d = dh.shape
    tm = TOKEN_TILE

    def body(dh_ref, gt_ref, a_ref, sv_ref, sg_ref, wap_ref, wv_ref, wgg_ref, wout_ref, dep_ref,
             dgt_ref, da_ref, dsv_ref, dsg_ref, do_ref, dyg_ref, dhb_ref):
        dhb = dh_ref[...].astype(BF16)
        dhb_ref[...] = dhb
        dm = _nt(dhb, wout_ref[...])
        sig_a = jax.nn.sigmoid(gt_ref[:, 0:d].astype(F32))
        sig_s = jax.nn.sigmoid(gt_ref[:, d:2 * d].astype(F32))
        sig_g = jax.nn.sigmoid(sg_ref[...].astype(F32))
        svv = sv_ref[...].astype(F32)
        dgt_ref[:, 0:d] = (dm * a_ref[...].astype(F32) * sig_a * (1.0 - sig_a)).astype(BF16)
        dgt_ref[:, d:2 * d] = (dm * (svv * sig_g) * sig_s * (1.0 - sig_s)).astype(BF16)
        da = (dm * sig_a).astype(BF16)
        d_s = dm * sig_s
        dsv = (d_s * sig_g).astype(BF16)
        dsg = (d_s * svv * sig_g * (1.0 - sig_g)).astype(BF16)
        da_ref[...] = da
        dsv_ref[...] = dsv
        dsg_ref[...] = dsg
        do_ref[...] = _nn(da, wap_ref[...]).astype(BF16)
        dyg_ref[...] = _nn(dsv, wv_ref[...]) + _nn(dsg, wgg_ref[...])

    return pl.pallas_call(
        body, name=name, grid=(t // tm,),
        in_specs=[_row_spec(tm, d), _row_spec(tm, 2 * d), _row_spec(tm, d), _row_spec(tm, d), _row_spec(tm, d),
                  _VMEM, _VMEM, _VMEM, _VMEM, _ANY],
        out_specs=[_row_spec(tm, 2 * d), _row_spec(tm, d), _row_spec(tm, d), _row_spec(tm, d),
                   _row_spec(tm, ATTN_WIDTH), _row_spec(tm, SSM_WIDTH), _row_spec(tm, d)],
        out_shape=[jax.ShapeDtypeStruct((t, 2 * d), BF16), jax.ShapeDtypeStruct((t, d), BF16),
                   jax.ShapeDtypeStruct((t, d), BF16), jax.ShapeDtypeStruct((t, d), BF16),
                   jax.ShapeDtypeStruct((t, ATTN_WIDTH), BF16), jax.ShapeDtypeStruct((t, SSM_WIDTH), F32),
                   jax.ShapeDtypeStruct((t, d), BF16)],
        compiler_params=_params(("arbitrary",)),
    )(dh, gates, att, sv, sg, wap_t, wv_t, wgg_t, wout, dep)


def _adamw_math(w, g, m, v):
    mn = ADAM_B1 * m + (1.0 - ADAM_B1) * g
    vn = ADAM_B2 * v + (1.0 - ADAM_B2) * (g * g)
    m_hat = mn / (1.0 - ADAM_B1 ** ADAM_STEP)
    v_hat = vn / (1.0 - ADAM_B2 ** ADAM_STEP)
    return -ADAM_LR * (m_hat / (jnp.sqrt(v_hat) + ADAM_EPS) + ADAM_WD * w), mn, vn


def adamw_layer(w, g, m, v, layer, prev, name):
    _, rows, cols = w.shape
    tr = rows
    for cand in (512, 256):
        if rows > cand and rows % cand == 0:
            tr = cand
            break

    def body(w_ref, g_ref, m_ref, v_ref, *rest):
        go_ref, d_ref, mo_ref, vo_ref = rest[-4:]
        gv = g_ref[...]
        go_ref[0] = gv
        d_ref[0], mo_ref[0], vo_ref[0] = _adamw_math(w_ref[0], gv, m_ref[0], v_ref[0])

    spec3 = pl.BlockSpec((1, tr, cols), lambda r: (layer, r, 0))
    out = jax.ShapeDtypeStruct(w.shape, F32)
    extra = [] if prev is None else list(prev)
    return pl.pallas_call(
        body, name=name, grid=(rows // tr,),
        in_specs=[spec3, _row_spec(tr, cols), spec3, spec3] + [_ANY] * len(extra),
        out_specs=[spec3] * 4, out_shape=[out] * 4,
        input_output_aliases={4 + j: j for j in range(len(extra))},
        compiler_params=_params(("arbitrary",)),
    )(w, g, m, v, *extra)


def adamw(w, g, m, v, name, minor_swap=False):
    if minor_swap:
        d, mn, vn = adamw(*[jnp.swapaxes(a, -1, -2) for a in (w, g, m, v)], name)
        return jnp.swapaxes(d, -1, -2), jnp.swapaxes(mn, -1, -2), jnp.swapaxes(vn, -1, -2)
    shape = w.shape
    as2d = lambda a: a.reshape(-1, shape[-1]) if a.ndim >= 2 else a.reshape(1, -1)
    w2, g2, m2, v2 = as2d(w), as2d(g), as2d(m), as2d(v)
    rows, cols = w2.shape
    tr = rows
    for cand in (1024, 704, 512, 256):
        if rows > cand and rows % cand == 0:
            tr = cand
            break

    def body(w_ref, g_ref, m_ref, v_ref, d_ref, mo_ref, vo_ref):
        d_ref[...], mo_ref[...], vo_ref[...] = _adamw_math(w_ref[...], g_ref[...], m_ref[...], v_ref[...])

    spec = _row_spec(tr, cols)
    out = jax.ShapeDtypeStruct((rows, cols), F32)
    d, mn, vn = pl.pallas_call(
        body, name=name, grid=(rows // tr,), in_specs=[spec] * 4, out_specs=[spec] * 3, out_shape=[out] * 3,
        compiler_params=_params(("arbitrary",)),
    )(w2, g2, m2, v2)
    return d.reshape(shape), mn.reshape(shape), vn.reshape(shape)


def _my_index():
    return 4 * lax.axis_index("x") + 2 * lax.axis_index("y") + lax.axis_index("c")


def _peer(p):
    return (lax.axis_index("x") ^ ((p >> 2) & 1), lax.axis_index("y") ^ ((p >> 1) & 1), lax.axis_index("c") ^ (p & 1))


_HBM = pl.BlockSpec(memory_space=pltpu.HBM)
_SEM = pl.BlockSpec(memory_space=pltpu.SEMAPHORE)
_EFFECT = pltpu.SideEffectType.DATAFLOW_SIDE_EFFECTING


class Exchange:
    RELAYED = (2, 4, 6)

    def __init__(self, srcs, scatter, name, relay=False):
        self.n = n = len(srcs)
        self.scatter = scatter
        self.name = name
        self.relayed = relay
        assert not (relay and scatter)
        self.direct = (1,) + self.RELAYED if relay else tuple(range(1, N_DEV))
        widths = sorted({s.shape[1] for s in srcs}, reverse=True)
        self.ncls = len(widths)
        self.cls = [widths.index(s.shape[1]) for s in srcs]
        self.cnts = [s.shape[0] // N_DEV if scatter else s.shape[0] for s in srcs]
        self.totals = [sum(c for c, k in zip(self.cnts, self.cls) if k == w) for w in range(self.ncls)]
        self.sizer = [max((k for k in range(n) if self.cls[k] == w), key=lambda k: self.cnts[k])
                      for w in range(self.ncls)]
        assert all(N_DEV * self.cnts[self.sizer[w]] >= self.totals[w] for w in range(self.ncls))
        if scatter:
            self.land_shapes = [(N_DEV, c, s.shape[1]) for s, c in zip(srcs, self.cnts)]
        else:
            self.land_shapes = [(N_DEV * c, s.shape[1]) for s, c in zip(srcs, self.cnts)]
        self.dtypes = [s.dtype for s in srcs]

    def _block(self, k, who):
        return pl.ds(pl.multiple_of(who * self.cnts[k], 16), self.cnts[k])

    def _sem(self, p, w):
        return (p - 1) * self.ncls + w

    def start(self, srcs, after):
        n = self.n

        def body(*refs):
            src, land = refs[:n], refs[n:2 * n]
            send_sems, recv_sems = refs[2 * n + 1], refs[2 * n + 2]
            token = refs[-1]
            me = _my_index()
            for p in self.direct:
                for k in range(n):
                    if self.scatter:
                        s_ref, d_ref = src[k].at[self._block(k, me ^ p), :], land[k].at[me]
                    else:
                        s_ref, d_ref = src[k], land[k].at[self._block(k, me), :]
                    pltpu.make_async_remote_copy(
                        src_ref=s_ref, dst_ref=d_ref, send_sem=send_sems.at[self._sem(p, self.cls[k])],
                        recv_sem=recv_sems.at[self._sem(p, self.cls[k])], device_id=_peer(p),
                        device_id_type=MESH).start()
            token[...] = jnp.zeros_like(token)

        sems = pltpu.SemaphoreType.DMA(((N_DEV - 1) * self.ncls,))
        thru = [pltpu.HBM(s.shape, s.dtype) for s in srcs] + [pltpu.HBM(shp, dt) for shp, dt in
                                                               zip(self.land_shapes, self.dtypes)]
        lands = [pltpu.with_memory_space_constraint(lax.empty(shp, dt), pltpu.HBM)
                 for shp, dt in zip(self.land_shapes, self.dtypes)]
        out = pl.pallas_call(
            body, name=self.name + "_start",
            in_specs=[_HBM] * (2 * n) + [_ANY],
            out_shape=[sems, sems] + thru + [jax.ShapeDtypeStruct((8, 128), F32)],
            out_specs=[_SEM, _SEM] + [_HBM] * (2 * n) + [_VMEM],
            input_output_aliases={j: 2 + j for j in range(2 * n)},
            compiler_params=pltpu.CompilerParams(has_side_effects=_EFFECT),
        )(*[pltpu.with_memory_space_constraint(s, pltpu.HBM) for s in srcs], *lands, after)
        return out[:-1], out[-1]

    def _span_copy(self, src, land, w, send_sem, recv_sem, p):
        big = src[self.sizer[w]] if self.scatter else land[self.sizer[w]]
        span = big.at[pl.ds(0, self.totals[w]), :]
        return pltpu.make_async_remote_copy(src_ref=span, dst_ref=span, send_sem=send_sem, recv_sem=recv_sem,
                                            device_id=_peer(p), device_id_type=MESH)

    def relay(self, state, after):
        n = self.n
        send_sems, recv_sems = state[0], state[1]
        thru = state[2:]

        def body(*refs):
            land = refs[n:2 * n]
            send_a, recv_a = refs[2 * n], refs[2 * n + 1]
            send_b, recv_b = refs[2 * n + 3], refs[2 * n + 4]
            me = _my_index()
            for p in self.RELAYED:
                for w in range(self.ncls):
                    self._span_copy(None, land, w, send_a.at[self._sem(p, w)], recv_a.at[self._sem(p, w)], p).wait_recv()
            for j, p in enumerate(self.RELAYED):
                for k in range(n):
                    rows = land[k].at[self._block(k, me ^ p), :]
                    pltpu.make_async_remote_copy(
                        src_ref=rows, dst_ref=rows, send_sem=send_b.at[j * self.ncls + self.cls[k]],
                        recv_sem=recv_b.at[j * self.ncls + self.cls[k]], device_id=_peer(1),
                        device_id_type=MESH).start()

        sems = pltpu.SemaphoreType.DMA((len(self.RELAYED) * self.ncls,))
        out = pl.pallas_call(
            body, name=self.name + "_relay",
            in_specs=[_HBM] * (2 * n) + [_SEM, _SEM, _ANY],
            out_shape=[sems, sems] + [pltpu.HBM(a.shape, a.dtype) for a in thru], out_specs=[_SEM, _SEM] + [_HBM] * (2 * n),
            input_output_aliases={j: 2 + j for j in range(2 * n)},
            compiler_params=pltpu.CompilerParams(has_side_effects=_EFFECT),
        )(*thru, send_sems, recv_sems, after)
        return [send_sems, recv_sems] + list(out[2:]) + [out[0], out[1]]

    def wait(self, state, after):
        n = self.n
        send_sems, recv_sems = state[0], state[1]
        thru = state[2:2 + 2 * n]
        relay_sems = list(state[2 + 2 * n:])
        assert len(relay_sems) == (2 if self.relayed else 0)
        after = list(after) if isinstance(after, (list, tuple)) else [after]

        def body(*refs):
            src, land = refs[:n], refs[n:2 * n]
            send_a, recv_a = refs[2 * n], refs[2 * n + 1]
            for p in self.direct:
                for w in range(self.ncls):
                    copy = self._span_copy(src, land, w, send_a.at[self._sem(p, w)], recv_a.at[self._sem(p, w)], p)
                    copy.wait_send()
                    if not (self.relayed and p in self.RELAYED):
                        copy.wait_recv()
            if self.relayed:
                send_b, recv_b = refs[2 * n + 2], refs[2 * n + 3]
                for j in range(len(self.RELAYED)):
                    for w in range(self.ncls):
                        copy = self._span_copy(src, land, w, send_b.at[j * self.ncls + w],
                                               recv_b.at[j * self.ncls + w], 1)
                        copy.wait_send()
                        copy.wait_recv()

        out = pl.pallas_call(
            body, name=self.name + "_wait",
            in_specs=[_HBM] * (2 * n) + [_SEM] * (2 + len(relay_sems)) + [_ANY] * len(after),
            out_shape=[pltpu.HBM(a.shape, a.dtype) for a in thru], out_specs=[_HBM] * (2 * n),
            input_output_aliases={j: j for j in range(2 * n)},
            compiler_params=pltpu.CompilerParams(has_side_effects=_EFFECT),
        )(*thru, send_sems, recv_sems, *relay_sems, *after)
        return out[:n], out[n:]

    def place(self, lands, srcs):
        n = self.n
        assert not self.scatter

        def body(*refs):
            src, land = refs[n:2 * n], refs[2 * n:3 * n]
            bufs, sems = refs[3 * n:4 * n], refs[-1]
            me = _my_index()
            loads = [pltpu.make_async_copy(src[k], bufs[k], sems.at[k]) for k in range(n)]
            stores = [pltpu.make_async_copy(bufs[k], land[k].at[self._block(k, me), :], sems.at[k]) for k in range(n)]
            for cp in loads:
                cp.start()
            for k in range(n):
                loads[k].wait()
                stores[k].start()
            for cp in stores:
                cp.wait()

        return pl.pallas_call(
            body, name=self.name + "_place", in_specs=[_ANY] * (2 * n), out_specs=[_ANY] * n,
            out_shape=[jax.ShapeDtypeStruct(a.shape, a.dtype) for a in lands],
            input_output_aliases={j: j for j in range(n)},
            scratch_shapes=[pltpu.VMEM(s.shape, s.dtype) for s in srcs] + [pltpu.SemaphoreType.DMA((n,))],
        )(*lands, *srcs)


def sum_blocks(landed, full, name):
    _, cnt, cols = landed.shape

    def body(land_ref, full_ref, o_ref, own_ref, sem):
        me = _my_index()
        own = pltpu.make_async_copy(full_ref.at[pl.ds(pl.multiple_of(me * cnt, 16), cnt), :], own_ref, sem)
        own.start()
        acc = land_ref[me ^ 1].astype(F32)
        for p in range(2, N_DEV):
            acc = acc + land_ref[me ^ p].astype(F32)
        own.wait()
        o_ref[...] = acc + own_ref[...].astype(F32)

    return pl.pallas_call(
        body, name=name, in_specs=[_VMEM, _ANY], out_specs=_VMEM,
        out_shape=jax.ShapeDtypeStruct((cnt, cols), F32),
        scratch_shapes=[pltpu.VMEM((cnt, cols), landed.dtype), pltpu.SemaphoreType.DMA],
        compiler_params=_params(),
    )(landed, full)


def sum_slots(slots, name):
    _, rows, cols = slots.shape
    tr = rows
    if rows > 512:
        for cand in (256, 128, 64, 32, 16, 8):
            if rows % cand == 0:
                tr = cand
                break

    def body(s_ref, o_ref):
        acc = s_ref[0].astype(F32)
        for j in range(1, N_DEV):
            acc = acc + s_ref[j].astype(F32)
        o_ref[...] = acc

    return pl.pallas_call(
        body, name=name, grid=(rows // tr,),
        in_specs=[pl.BlockSpec((N_DEV, tr, cols), lambda i: (0, i, 0))], out_specs=_row_spec(tr, cols),
        out_shape=jax.ShapeDtypeStruct((rows, cols), F32), compiler_params=_params(("arbitrary",)),
    )(slots)


BIG_T = ("ffn1_w_gate", "ffn1_w_up", "w_in", "ffn2_w_gate", "ffn2_w_up")
BIG_N = ("ffn1_w_down", "w_out", "ffn2_w_down")
HALF_T = ("w_attn_proj", "w_glu_v", "w_glu_g")
SMALL = ("ffn1_norm", "mix_norm", "attn_sinks", "ssm_a_re", "ssm_a_im", "ssm_log_dt", "ssm_b_re", "ssm_b_im",
         "ssm_c_re", "ssm_c_im", "ssm_d", "ffn2_norm", "final_norm")
PARTS = {"ffn1": ("ffn1_w_gate", "ffn1_w_up", "ffn1_w_down"),
         "mix": ("w_in", "w_out", "w_attn_proj", "w_glu_v", "w_glu_g"),
         "ffn2": ("ffn2_w_gate", "ffn2_w_up", "ffn2_w_down")}


def _to_rows(name, a):
    return a if name in BIG_N else jnp.swapaxes(a, -1, -2)


def local_step(x, tgt, get_weights, put_grads, small):
    seq, d = x.shape
    t = PAD_FRONT + N_META + seq
    cos_t, sin_t = rope_tables(t)
    row = lambda a: a.reshape(1, -1)
    tables = []
    for i in range(DEPTH):
        b_re_t = jnp.swapaxes(small["ssm_b_re"][i], 1, 2)
        b_im_t = jnp.swapaxes(small["ssm_b_im"][i], 1, 2)
        lam_re, lam_im, bbar_re, bbar_im = ssm_prep(small["ssm_a_re"][i], small["ssm_a_im"][i],
                                                    small["ssm_log_dt"][i].reshape(-1, 1), b_re_t, b_im_t, f"ssm_prep_{i}")
        tables.append(((b_re_t, b_im_t),
                       (row(lam_re), row(lam_im), _block_diag_b(bbar_re).astype(BF16), _block_diag_b(bbar_im).astype(BF16),
                        _block_diag_c(small["ssm_c_re"][i]).astype(BF16), _block_diag_c(small["ssm_c_im"][i]).astype(BF16),
                        row(small["ssm_d"][i]))))
    early = [cos_t, sin_t] + [a for _, tab in tables for a in tab[2:6]]
    saved = []
    h = None
    for i in range(DEPTH):
        s = {}
        w = dict(get_weights(i, "ffn1", early if i == 0 else h))
        if i == 0:
            h = jnp.concatenate([jnp.zeros((PAD_FRONT, d), F32), w["meta_tokens"], x], axis=0)
        s["h0"] = h
        h, s["n1"], s["a1"], s["b1"] = ffn_fwd(h, row(small["ffn1_norm"][i]), w["ffn1_w_gate"], w["ffn1_w_up"],
                                               w["ffn1_w_down"], f"ffn1_fwd_{i}")
        s["h1"] = h
        w.update(get_weights(i, "mix", h))
        s["n2"], s["qkv"], s["u"], s["gates"] = win_fwd(h, row(small["mix_norm"][i]), w["w_in"], cos_t, sin_t,
                                                        f"win_fwd_{i}")
        s["b_t"], s["ssm"] = tables[i]
        s["yg"], s["h_re"], s["h_im"] = ssm_fwd(s["u"], *s["ssm"], f"ssm_fwd_{i}")
        s["o"] = attn_fwd(s["qkv"], row(small["attn_sinks"][i]), f"attn_fwd_{i}")
        h, s["merged"], s["att"], s["sv"], s["sg"] = merge_fwd(
            h, s["o"], s["yg"], s["gates"], w["w_attn_proj"], w["w_glu_v"], w["w_glu_g"], w["w_out"],
            f"merge_fwd_{i}")
        s["h2"] = h
        w.update(get_weights(i, "ffn2", h))
        h, s["n3"], s["a3"], s["b3"] = ffn_fwd(h, row(small["ffn2_norm"][i]), w["ffn2_w_gate"], w["ffn2_w_up"],
                                               w["ffn2_w_down"], f"ffn2_fwd_{i}")
        s["w"] = w
        saved.append(s)

    loss, dh, d_final = head_fwd_bwd(h, row(small["final_norm"]), tgt)
    gs = {k: [None] * DEPTH for k in SMALL if k != "final_norm"}
    dep = loss
    for i in reversed(range(DEPTH)):
        s = saved[i]
        w = s["w"]
        dh, da, db, sact, dhb, dg = ffn_bwd(dh, s["h2"], row(small["ffn2_norm"][i]), s["a3"], s["b3"], w["ffn2_w_gate"],
                                            w["ffn2_w_up"], w["ffn2_w_down"], dep, f"ffn2_bwd_{i}")
        gs["ffn2_norm"][i] = dg[0]
        dep = put_grads(i, "ffn2", {"ffn2_w_gate": tn_matmul(da, s["n3"], f"ffn2_dwg_{i}"),
                                    "ffn2_w_up": tn_matmul(db, s["n3"], f"ffn2_dwu_{i}"),
                                    "ffn2_w_down": tn_matmul(sact, dhb, f"ffn2_dwd_{i}")})

        dgates, datt, dsv, dsg, do, dyg, dhb = merge_bwd(dh, s["gates"], s["att"], s["sv"], s["sg"], w["w_attn_proj"],
                                                         w["w_glu_v"], w["w_glu_g"], w["w_out"], dep, f"merge_bwd_{i}")
        gmix = {"w_out": tn_matmul(s["merged"], dhb, f"dwout_{i}"),
                "w_attn_proj": tn_matmul(datt, s["o"], f"dwap_{i}"),
                "w_glu_v": tn_matmul(dsv, s["yg"], f"dwv_{i}"),
                "w_glu_g": tn_matmul(dsg, s["yg"], f"dwgg_{i}")}
        dqkv, dsink = attn_bwd(s["qkv"], do, row(small["attn_sinks"][i]), cos_t, sin_t, f"attn_bwd_{i}")
        gs["attn_sinks"][i] = dsink[:, 0]
        du, dl_re, dl_im, dbb_re, dbb_im, dcc_re, dcc_im, dd = ssm_bwd(dyg, s["u"], s["h_re"], s["h_im"], *s["ssm"],
                                                                      f"ssm_bwd_{i}")
        fold = lambda a: jnp.sum(a, axis=0).reshape(SSM_GROUPS, SSM_STATE)
        da_re, da_im, dldt, db_re_t, db_im_t = ssm_prep_bwd(
            small["ssm_a_re"][i], small["ssm_a_im"][i], small["ssm_log_dt"][i].reshape(-1, 1), *s["b_t"],
            fold(dl_re), fold(dl_im), _diag_of_b(dbb_re), _diag_of_b(dbb_im), f"ssm_prep_bwd_{i}")
        gs["ssm_a_re"][i], gs["ssm_a_im"][i], gs["ssm_log_dt"][i] = da_re, da_im, dldt[:, 0]
        gs["ssm_b_re"][i], gs["ssm_b_im"][i] = jnp.swapaxes(db_re_t, 1, 2), jnp.swapaxes(db_im_t, 1, 2)
        gs["ssm_c_re"][i], gs["ssm_c_im"][i] = _diag_of_c(dcc_re), _diag_of_c(dcc_im)
        gs["ssm_d"][i] = dd[0]
        gmix["w_in"] = tn_matmul([dqkv, du, dgates], s["n2"], f"dwin_{i}")
        dep = put_grads(i, "mix", gmix)
        dh, dg = win_bwd(dh, s["h1"], row(small["mix_norm"][i]), dqkv, du, dgates, w["w_in"], dep, f"win_bwd_{i}")
        gs["mix_norm"][i] = dg[0]

        dh, da, db, sact, dhb, dg = ffn_bwd(dh, s["h0"], row(small["ffn1_norm"][i]), s["a1"], s["b1"], w["ffn1_w_gate"],
                                            w["ffn1_w_up"], w["ffn1_w_down"], dep, f"ffn1_bwd_{i}")
        gs["ffn1_norm"][i] = dg[0]
        if i > 0:
            dep = put_grads(i, "ffn1", {"ffn1_w_gate": tn_matmul(da, s["n1"], f"ffn1_dwg_{i}"),
                                        "ffn1_w_up": tn_matmul(db, s["n1"], f"ffn1_dwu_{i}"),
                                        "ffn1_w_down": tn_matmul(sact, dhb, f"ffn1_dwd_{i}")})
        else:
            for k, xa, ya in (("ffn1_w_down", sact, dhb), ("ffn1_w_gate", da, s["n1"]), ("ffn1_w_up", db, s["n1"])):
                dep = put_grads(i, "ffn1", {k: tn_matmul(xa, ya, f"d_{k}_{i}", dep)})

    gs = {k: jnp.stack(v) for k, v in gs.items()}
    gs["final_norm"] = d_final[0]
    return loss[0, 0], dh[PAD_FRONT + N_META:], dh[PAD_FRONT:PAD_FRONT + N_META], gs, dep


def _pack_rows(arrays, cols):
    flat = jnp.concatenate([a.reshape(-1) for a in arrays])
    rows = -(-flat.shape[0] // cols)
    rows = -(-rows // 16) * 16
    return jnp.pad(flat, (0, rows * cols - flat.shape[0])).reshape(rows, cols)


def _unpack_rows(packed, shapes):
    flat = packed.reshape(-1)
    out, off = [], 0
    for shp in shapes:
        n = math.prod(shp)
        out.append(flat[off:off + n].reshape(shp))
        off += n
    return out


def kernel(x, meta_tokens, ffn1_norm, ffn1_w_gate, ffn1_w_up, ffn1_w_down, mix_norm, w_in, attn_sinks, ssm_a_re, ssm_a_im, ssm_log_dt, ssm_b_re, ssm_b_im, ssm_c_re, ssm_c_im, ssm_d, w_attn_proj, w_glu_v, w_glu_g, w_out, ffn2_norm, ffn2_w_gate, ffn2_w_up, ffn2_w_down, final_norm, loss_target, m_meta_tokens, m_ffn1_norm, m_ffn1_w_gate, m_ffn1_w_up, m_ffn1_w_down, m_mix_norm, m_w_in, m_attn_sinks, m_ssm_a_re, m_ssm_a_im, m_ssm_log_dt, m_ssm_b_re, m_ssm_b_im, m_ssm_c_re, m_ssm_c_im, m_ssm_d, m_w_attn_proj, m_w_glu_v, m_w_glu_g, m_w_out, m_ffn2_norm, m_ffn2_w_gate, m_ffn2_w_up, m_ffn2_w_down, m_final_norm, v_meta_tokens, v_ffn1_norm, v_ffn1_w_gate, v_ffn1_w_up, v_ffn1_w_down, v_mix_norm, v_w_in, v_attn_sinks, v_ssm_a_re, v_ssm_a_im, v_ssm_log_dt, v_ssm_b_re, v_ssm_b_im, v_ssm_c_re, v_ssm_c_im, v_ssm_d, v_w_attn_proj, v_w_glu_v, v_w_glu_g, v_w_out, v_ffn2_norm, v_ffn2_w_gate, v_ffn2_w_up, v_ffn2_w_down, v_final_norm):
    names = ("meta_tokens", "ffn1_norm", "ffn1_w_gate", "ffn1_w_up", "ffn1_w_down", "mix_norm", "w_in", "attn_sinks",
             "ssm_a_re", "ssm_a_im", "ssm_log_dt", "ssm_b_re", "ssm_b_im", "ssm_c_re", "ssm_c_im", "ssm_d",
             "w_attn_proj", "w_glu_v", "w_glu_g", "w_out", "ffn2_norm", "ffn2_w_gate", "ffn2_w_up", "ffn2_w_down",
             "final_norm")
    weights = dict(zip(names, (meta_tokens, ffn1_norm, ffn1_w_gate, ffn1_w_up, ffn1_w_down, mix_norm, w_in, attn_sinks, ssm_a_re, ssm_a_im, ssm_log_dt, ssm_b_re, ssm_b_im, ssm_c_re, ssm_c_im, ssm_d, w_attn_proj, w_glu_v, w_glu_g, w_out, ffn2_norm, ffn2_w_gate, ffn2_w_up, ffn2_w_down, final_norm)))
    moments_m = dict(zip(names, (m_meta_tokens, m_ffn1_norm, m_ffn1_w_gate, m_ffn1_w_up, m_ffn1_w_down, m_mix_norm, m_w_in, m_attn_sinks, m_ssm_a_re, m_ssm_a_im, m_ssm_log_dt, m_ssm_b_re, m_ssm_b_im, m_ssm_c_re, m_ssm_c_im, m_ssm_d, m_w_attn_proj, m_w_glu_v, m_w_glu_g, m_w_out, m_ffn2_norm, m_ffn2_w_gate, m_ffn2_w_up, m_ffn2_w_down, m_final_norm)))
    moments_v = dict(zip(names, (v_meta_tokens, v_ffn1_norm, v_ffn1_w_gate, v_ffn1_w_up, v_ffn1_w_down, v_mix_norm, v_w_in, v_attn_sinks, v_ssm_a_re, v_ssm_a_im, v_ssm_log_dt, v_ssm_b_re, v_ssm_b_im, v_ssm_c_re, v_ssm_c_im, v_ssm_d, v_w_attn_proj, v_w_glu_v, v_w_glu_g, v_w_out, v_ffn2_norm, v_ffn2_w_gate, v_ffn2_w_up, v_ffn2_w_down, v_final_norm)))
    me = _my_index()

    order = [(i, part) for i in range(DEPTH) for part in PARTS]
    gathers = {}
    token = jnp.zeros((8, 128), F32)
    for i, part in order:
        shards = [_to_rows(k, weights[k][i]).astype(BF16) for k in PARTS[part]]
        if (i, part) == order[0]:
            shards.append(meta_tokens)
        ex = Exchange(shards, False, f"gather_{part}_{i}", relay=True)
        state, token = ex.start(shards, token)
        gathers[i, part] = [ex, state, False]
    all_started = token

    def relay(group, after):
        ex, state, relayed = gathers[group]
        if not relayed:
            gathers[group][1:] = [ex.relay(state, after), True]

    relay(order[0], all_started)

    def get_weights(i, part, after):
        g = order.index((i, part))
        first = g == 0
        relay(order[g], all_started if first else after)
        if g >= 2 and g + 1 < len(order):
            relay(order[g + 1], after)
        ex, state, _ = gathers[i, part]
        shards, lands = ex.wait(state, [all_started] + list(after) if first else after)
        fulls = ex.place(lands, shards)
        got = dict(zip(PARTS[part], fulls))
        if (i, part) == (0, "ffn1"):
            got["meta_tokens"] = jnp.swapaxes(fulls[-1].reshape(N_DEV, N_META, 128), 0, 1).reshape(N_META, D_MODEL)
        return got

    scatters = []

    def put_grads(i, part, gdict):
        ks = list(gdict)
        srcs = [gdict[k] for k in ks]
        ex = Exchange(srcs, True, f"scatter_{part if len(ks) > 1 else ks[0]}_{i}")
        state, tok = ex.start(srcs, all_started)
        scatters.append((i, ks, ex, state))
        return tok

    small = {k: weights[k] for k in SMALL}
    loss, dx, dmeta, gs, last_started = local_step(x[0], loss_target[0], get_weights, put_grads, small)

    grads, deltas, new_m, new_v = {}, {}, {}, {}
    small_list = [loss.reshape(1), dmeta] + [gs[k] for k in SMALL]
    packed = _pack_rows(small_list, D_MODEL)
    small_ex = Exchange([packed], False, "gather_small")
    small_state, after = small_ex.start([packed], last_started)

    updated = {}
    for i, ks, ex, state in scatters:
        partials, lands = ex.wait(state, after)
        for k, partial, slots in zip(ks, partials, lands):
            g = sum_blocks(slots, partial, f"sum_{k}_{i}")
            updated[k] = adamw_layer(_to_rows(k, weights[k]), g, _to_rows(k, moments_m[k]), _to_rows(k, moments_v[k]),
                                     i, updated.get(k), f"adamw_{k}_{i}")
            after = updated[k][0]
    for k, outs in updated.items():
        grads[k], deltas[k], new_m[k], new_v[k] = [_to_rows(k, a) for a in outs]

    packed_own, packed_all = small_ex.wait(small_state, after)
    (packed_all,) = small_ex.place(packed_all, packed_own)
    total = sum_slots(packed_all.reshape(N_DEV, packed.shape[0], D_MODEL), "sum_small")
    pieces = _unpack_rows(total, [a.shape for a in small_list])
    loss_out = pieces[0][0]
    grads["meta_tokens"] = lax.dynamic_slice_in_dim(pieces[1], me * 128, 128, axis=1)
    for k, p in zip(SMALL, pieces[2:]):
        grads[k] = p
    for k in ("meta_tokens",) + SMALL:
        deltas[k], new_m[k], new_v[k] = adamw(weights[k], grads[k], moments_m[k], moments_v[k], f"adamw_{k}",
                                              minor_swap=k in ("ssm_b_re", "ssm_b_im"))
    return (loss_out, dx[None], *[grads[k] for k in names], *[deltas[k] for k in names],
            *[new_m[k] for k in names], *[new_v[k] for k in names])
```

```python
import functools
import math

import jax
import jax.numpy as jnp
from jax import lax
from jax.experimental import pallas as pl
from jax.experimental.pallas import tpu as pltpu

F32 = jnp.float32
BF16 = jnp.bfloat16

D_MODEL = 1024
DEPTH = 2
N_META = 16
HEAD_DIM = 64
N_Q_HEADS = 8
ATTN_WIDTH = 512
KV_WIDTH = 128
QKV_WIDTH = ATTN_WIDTH + 2 * KV_WIDTH
WINDOW = 128
BLK = 128
ROPE_THETA = 500000.0
ROT_DIM = 16
SSM_WIDTH = 512
SSM_GROUP = 16
SSM_GROUPS = 32
SSM_STATE = 64
STATE_WIDTH = SSM_GROUPS * SSM_STATE
D_FF = 2816
IN_WIDTH = 3328
EPS = 1e-6
NEG_INF = -1e30
PAD_FRONT = (-N_META) % BLK
N_DEV = 8

ADAM_LR = 0.001
ADAM_B1 = 0.9
ADAM_B2 = 0.999
ADAM_EPS = 1e-08
ADAM_WD = 0.01
ADAM_STEP = 10

VMEM_LIMIT = 56 * 1024 * 1024
TOKEN_TILE = 384
_VMEM = pl.BlockSpec(memory_space=pltpu.VMEM)
_SMEM = pl.BlockSpec(memory_space=pltpu.SMEM)
_ANY = pl.BlockSpec(memory_space=pl.ANY)
MESH = pl.DeviceIdType.MESH


def _params(sem=None):
    return pltpu.CompilerParams(dimension_semantics=sem, vmem_limit_bytes=VMEM_LIMIT)


def _nt(a, b):
    return lax.dot_general(a, b, (((1,), (1,)), ((), ())), preferred_element_type=F32)


def _nn(a, b):
    return jnp.dot(a, b, preferred_element_type=F32)


def _tn(a, b):
    return lax.dot_general(a, b, (((0,), (0,)), ((), ())), preferred_element_type=F32)


def _row_spec(tm, width):
    return pl.BlockSpec((tm, width), lambda i: (i, 0))


def _acc_spec(shape):
    return pl.BlockSpec(shape, lambda i: (0,) * len(shape))


def _rms_stats(x):
    r = lax.rsqrt(jnp.mean(x * x, axis=-1, keepdims=True) + EPS)
    return x * r, r


def _rms_bwd(dn, xh, r, g):
    dg = jnp.sum(dn * xh, axis=0, keepdims=True)
    dxh = dn * g
    dx = r * (dxh - xh * jnp.mean(dxh * xh, axis=-1, keepdims=True))
    return dx, dg


def ffn_fwd(h, g, wg_t, wu_t, wd, name):
    t, d = h.shape
    f = wd.shape[0]
    tm = TOKEN_TILE

    def body(h_ref, g_ref, wg_ref, wu_ref, wd_ref, ho_ref, n_ref, a_ref, b_ref):
        x = h_ref[...]
        xh, _ = _rms_stats(x)
        n = (xh * g_ref[...]).astype(BF16)
        n_ref[...] = n
        a = _nt(n, wg_ref[...])
        b = _nt(n, wu_ref[...])
        a_ref[...] = a.astype(BF16)
        b_ref[...] = b.astype(BF16)
        s = (a * jax.nn.sigmoid(a) * b).astype(BF16)
        ho_ref[...] = x + 0.5 * _nn(s, wd_ref[...])

    return pl.pallas_call(
        body, name=name, grid=(t // tm,),
        in_specs=[_row_spec(tm, d), _acc_spec((1, d)), _VMEM, _VMEM, _VMEM],
        out_specs=[_row_spec(tm, d), _row_spec(tm, d), _row_spec(tm, f), _row_spec(tm, f)],
        out_shape=[jax.ShapeDtypeStruct((t, d), F32), jax.ShapeDtypeStruct((t, d), BF16),
                   jax.ShapeDtypeStruct((t, f), BF16), jax.ShapeDtypeStruct((t, f), BF16)],
        compiler_params=_params(("arbitrary",)),
    )(h, g, wg_t, wu_t, wd)


def ffn_bwd(dh, h, g, a, b, wg_t, wu_t, wd, dep, name):
    t, d = h.shape
    f = wd.shape[0]
    tm = TOKEN_TILE // 2

    def body(dh_ref, h_ref, g_ref, a_ref, b_ref, wg_ref, wu_ref, wd_ref, dep_ref,
             dhi_ref, da_ref, db_ref, s_ref, dhb_ref, dg_ref):
        dh_t = dh_ref[...]
        dhb = (0.5 * dh_t).astype(BF16)
        dhb_ref[...] = dhb
        ds = _nt(dhb, wd_ref[...])
        av = a_ref[...].astype(F32)
        bv = b_ref[...].astype(F32)
        sig = jax.nn.sigmoid(av)
        sl = av * sig
        s_ref[...] = (sl * bv).astype(BF16)
        da = (ds * bv * (sig * (1.0 + av * (1.0 - sig)))).astype(BF16)
        db = (ds * sl).astype(BF16)
        da_ref[...] = da
        db_ref[...] = db
        dn = _nn(da, wg_ref[...]) + _nn(db, wu_ref[...])
        xh, r = _rms_stats(h_ref[...])
        dx, dg = _rms_bwd(dn, xh, r, g_ref[...])
        dhi_ref[...] = dh_t + dx

        @pl.when(pl.program_id(0) == 0)
        def _():
            dg_ref[...] = jnp.zeros_like(dg_ref)

        dg_ref[...] += dg

    return pl.pallas_call(
        body, name=name, grid=(t // tm,),
        in_specs=[_row_spec(tm, d), _row_spec(tm, d), _acc_spec((1, d)), _row_spec(tm, f), _row_spec(tm, f),
                  _VMEM, _VMEM, _VMEM, _ANY],
        out_specs=[_row_spec(tm, d), _row_spec(tm, f), _row_spec(tm, f), _row_spec(tm, f), _row_spec(tm, d),
                   _acc_spec((1, d))],
        out_shape=[jax.ShapeDtypeStruct((t, d), F32), jax.ShapeDtypeStruct((t, f), BF16),
                   jax.ShapeDtypeStruct((t, f), BF16), jax.ShapeDtypeStruct((t, f), BF16),
                   jax.ShapeDtypeStruct((t, d), BF16), jax.ShapeDtypeStruct((1, d), F32)],
        compiler_params=_params(("arbitrary",)),
    )(dh, h, g, a, b, wg_t, wu_t, wd, dep)


DW_TILE = 256


def tn_matmul(x, y, name, dep=None):
    xs = list(x) if isinstance(x, (list, tuple)) else [x]
    t = xs[0].shape[0]
    n = y.shape[1]
    bm = DW_TILE
    tiles = [a.shape[1] // bm for a in xs]
    offs = [sum(tiles[:k]) for k in range(len(xs))]
    deps = [] if dep is None else [dep]

    def body(*refs):
        y_ref, o_ref = refs[len(xs)], refs[-1]
        i = pl.program_id(0)
        for k in range(len(xs)):
            @pl.when((i >= offs[k]) & (i < offs[k] + tiles[k]))
            def _(k=k):
                o_ref[...] = _tn(refs[k][...], y_ref[...]).astype(BF16)

    def x_spec(k):
        return pl.BlockSpec((t, bm), lambda i: (0, jnp.clip(i - offs[k], 0, tiles[k] - 1)))

    return pl.pallas_call(
        body, name=name, grid=(sum(tiles),),
        in_specs=[x_spec(k) for k in range(len(xs))] + [_VMEM] + [_ANY] * len(deps),
        out_specs=pl.BlockSpec((bm, n), lambda i: (i, 0)),
        out_shape=jax.ShapeDtypeStruct((sum(tiles) * bm, n), BF16),
        compiler_params=_params(("arbitrary",)),
    )(*xs, y, *deps)


def head_fwd_bwd(h, g, tgt):
    t, d = h.shape

    def body(h_ref, g_ref, t_ref, loss_ref, dh_ref, dg_ref):
        i = pl.program_id(0)
        xh, r = _rms_stats(h_ref[...])
        gv = g_ref[...]
        valid = (i > 0).astype(F32)
        e = (xh * gv - t_ref[...]) * valid
        dx, dg = _rms_bwd(e * (1.0 / d), xh, r, gv)
        dh_ref[...] = dx

        @pl.when(i == 0)
        def _():
            dg_ref[...] = jnp.zeros_like(dg_ref)
            loss_ref[...] = jnp.zeros_like(loss_ref)

        dg_ref[...] += dg
        loss_ref[...] += jnp.sum(e * e) * (0.5 / d)

    return pl.pallas_call(
        body, name="head", grid=(t // BLK,),
        in_specs=[_row_spec(BLK, d), _acc_spec((1, d)),
                  pl.BlockSpec((BLK, d), lambda i: (jnp.maximum(i - 1, 0), 0))],
        out_specs=[_acc_spec((1, 128)), _row_spec(BLK, d), _acc_spec((1, d))],
        out_shape=[jax.ShapeDtypeStruct((1, 128), F32), jax.ShapeDtypeStruct((t, d), F32),
                   jax.ShapeDtypeStruct((1, d), F32)],
        compiler_params=_params(("arbitrary",)),
    )(h, g, tgt)


def rope_tables(t):
    pos = jnp.arange(t, dtype=F32) - PAD_FRONT
    inv_freq = ROPE_THETA ** (-jnp.arange(0, ROT_DIM, 2, dtype=F32) / ROT_DIM)
    ang = pos[:, None] * inv_freq[None, :]
    cos, sin = jnp.cos(ang), jnp.sin(ang)
    ones = jnp.ones((t, HEAD_DIM - ROT_DIM), F32)
    cos_h = jnp.concatenate([cos, cos, ones], axis=1)
    sin_h = jnp.concatenate([-sin, sin, 0.0 * ones], axis=1)
    return jnp.concatenate([cos_h, cos_h], axis=1), jnp.concatenate([sin_h, sin_h], axis=1)


def _swap_halves(x):
    n = x.shape[1]
    lane = lax.broadcasted_iota(jnp.int32, x.shape, 1)
    return jnp.where(lane % HEAD_DIM < ROT_DIM // 2, pltpu.roll(x, n - ROT_DIM // 2, 1), pltpu.roll(x, ROT_DIM // 2, 1))


def _rope(x, cos_t, sin_t, sign):
    return x * cos_t + sign * (_swap_halves(x) * sin_t)


def win_fwd(h, g, win_t, cos_t, sin_t, name):
    t, d = h.shape
    tm = TOKEN_TILE

    def body(h_ref, g_ref, w_ref, c_ref, s_ref, n_ref, qkv_ref, u_ref, gates_ref):
        xh, _ = _rms_stats(h_ref[...])
        n = (xh * g_ref[...]).astype(BF16)
        n_ref[...] = n
        z = _nt(n, w_ref[...])
        c, s = c_ref[...], s_ref[...]
        for j in range((ATTN_WIDTH + KV_WIDTH) // 128):
            qkv_ref[:, j * 128:(j + 1) * 128] = _rope(z[:, j * 128:(j + 1) * 128], c, s, 1.0).astype(BF16)
        qkv_ref[:, ATTN_WIDTH + KV_WIDTH:QKV_WIDTH] = z[:, ATTN_WIDTH + KV_WIDTH:QKV_WIDTH].astype(BF16)
        u_ref[...] = z[:, QKV_WIDTH:QKV_WIDTH + SSM_WIDTH]
        gates_ref[...] = z[:, QKV_WIDTH + SSM_WIDTH:].astype(BF16)

    return pl.pallas_call(
        body, name=name, grid=(t // tm,),
        in_specs=[_row_spec(tm, d), _acc_spec((1, d)), _VMEM, _row_spec(tm, 128), _row_spec(tm, 128)],
        out_specs=[_row_spec(tm, d), _row_spec(tm, QKV_WIDTH), _row_spec(tm, SSM_WIDTH), _row_spec(tm, 2 * d)],
        out_shape=[jax.ShapeDtypeStruct((t, d), BF16), jax.ShapeDtypeStruct((t, QKV_WIDTH), BF16),
                   jax.ShapeDtypeStruct((t, SSM_WIDTH), F32), jax.ShapeDtypeStruct((t, 2 * d), BF16)],
        compiler_params=_params(("arbitrary",)),
    )(h, g, win_t, cos_t, sin_t)


def win_bwd(dh, h, g, dqkv, du, dgates, win_t, dep, name):
    t, d = h.shape
    tm = TOKEN_TILE

    def body(dh_ref, h_ref, g_ref, dqkv_ref, du_ref, dgt_ref, w_ref, dep_ref, dhi_ref, dg_ref):
        dn = (_nn(dqkv_ref[...], w_ref[0:QKV_WIDTH, :])
              + _nn(du_ref[...], w_ref[QKV_WIDTH:QKV_WIDTH + SSM_WIDTH, :])
              + _nn(dgt_ref[...], w_ref[QKV_WIDTH + SSM_WIDTH:, :]))
        xh, r = _rms_stats(h_ref[...])
        dx, dg = _rms_bwd(dn, xh, r, g_ref[...])
        dhi_ref[...] = dh_ref[...] + dx

        @pl.when(pl.program_id(0) == 0)
        def _():
            dg_ref[...] = jnp.zeros_like(dg_ref)

        dg_ref[...] += dg

    return pl.pallas_call(
        body, name=name, grid=(t // tm,),
        in_specs=[_row_spec(tm, d), _row_spec(tm, d), _acc_spec((1, d)), _row_spec(tm, QKV_WIDTH),
                  _row_spec(tm, SSM_WIDTH), _row_spec(tm, 2 * d), _VMEM, _ANY],
        out_specs=[_row_spec(tm, d), _acc_spec((1, d))],
        out_shape=[jax.ShapeDtypeStruct((t, d), F32), jax.ShapeDtypeStruct((1, d), F32)],
        compiler_params=_params(("arbitrary",)),
    )(dh, h, g, dqkv, du, dgates, win_t, dep)


def _attn_mask(blk):
    q_pos = blk * BLK + lax.broadcasted_iota(jnp.int32, (BLK, 3 * BLK), 0) - PAD_FRONT
    col = lax.broadcasted_iota(jnp.int32, (BLK, 3 * BLK), 1)
    part = col // BLK
    k_pos = jnp.where(part == 0, col, (blk + part - 2) * BLK + (col - part * BLK)) - PAD_FRONT
    dist = q_pos - k_pos
    meta_ok = (part == 0) & (k_pos >= 0) & (dist >= 0)
    band_ok = (part > 0) & (k_pos >= N_META) & (dist >= 0) & (dist < WINDOW)
    return meta_ok | band_ok


def _head_halves(x128, kv):
    x = x128.astype(F32)
    lane = lax.broadcasted_iota(jnp.int32, x.shape, 1)
    swapped = pltpu.roll(x, HEAD_DIM, 1)
    lo, hi = (x, swapped) if kv == 0 else (swapped, x)
    return jnp.where(lane < HEAD_DIM, lo, 0.0).astype(BF16), jnp.where(lane >= HEAD_DIM, hi, 0.0).astype(BF16)


def _gather_keys(meta_ref, prev_ref, cur_ref, lo):
    return jnp.concatenate([meta_ref[:, lo:lo + 128], prev_ref[:, lo:lo + 128], cur_ref[:, lo:lo + 128]], axis=0)


def _pair_lanes(kv):
    return slice(2 * kv * 128, (2 * kv + 1) * 128), slice((2 * kv + 1) * 128, (2 * kv + 2) * 128)


def _stacked_sinks(sink_ref, head):
    row = lax.broadcasted_iota(jnp.int32, (2 * BLK, 1), 0)
    return jnp.where(row < BLK, sink_ref[0, head], sink_ref[0, head + 2])


def _softmax_with_sink(s, mask, sink):
    s = jnp.where(mask, s * (HEAD_DIM ** -0.5), NEG_INF)
    m = jnp.maximum(jnp.max(s, axis=-1, keepdims=True), sink)
    p = jnp.exp(s - m)
    p_sink = jnp.exp(sink - m)
    inv = 1.0 / (jnp.sum(p, axis=-1, keepdims=True) + p_sink)
    return p * inv, p_sink * inv


def attn_fwd(qkv, sinks, name):
    t = qkv.shape[0]
    nb = t // BLK

    def body(sink_ref, meta_ref, prev_ref, cur_ref, o_ref):
        blk = pl.program_id(0)
        mask = _attn_mask(blk)
        mask2 = jnp.concatenate([mask, mask], axis=0)
        k128 = _gather_keys(meta_ref, prev_ref, cur_ref, ATTN_WIDTH)
        v128 = _gather_keys(meta_ref, prev_ref, cur_ref, ATTN_WIDTH + KV_WIDTH)
        for kv in range(2):
            k_lo, k_hi = _head_halves(k128, kv)
            v_lo, v_hi = _head_halves(v128, kv)
            lanes0, lanes1 = _pair_lanes(kv)
            q2 = jnp.concatenate([cur_ref[:, lanes0], cur_ref[:, lanes1]], axis=0)
            p_a, _ = _softmax_with_sink(_nt(q2, k_lo), mask2, _stacked_sinks(sink_ref, 4 * kv))
            p_b, _ = _softmax_with_sink(_nt(q2, k_hi), mask2, _stacked_sinks(sink_ref, 4 * kv + 1))
            o2 = (_nn(p_a.astype(BF16), v_lo) + _nn(p_b.astype(BF16), v_hi)).astype(BF16)
            o_ref[:, lanes0] = o2[0:BLK]
            o_ref[:, lanes1] = o2[BLK:2 * BLK]

    blk_spec = lambda f: pl.BlockSpec((BLK, QKV_WIDTH), f)
    return pl.pallas_call(
        body, name=name, grid=(nb,),
        in_specs=[_SMEM, blk_spec(lambda i: (0, 0)), blk_spec(lambda i: (jnp.maximum(i - 1, 0), 0)),
                  blk_spec(lambda i: (i, 0))],
        out_specs=_row_spec(BLK, ATTN_WIDTH),
        out_shape=jax.ShapeDtypeStruct((t, ATTN_WIDTH), BF16),
        compiler_params=_params(("arbitrary",)),
    )(sinks, qkv, qkv, qkv)


def attn_bwd(qkv, do, sinks, cos_t, sin_t, name):
    t = qkv.shape[0]
    nb = t // BLK

    def body(sink_ref, meta_ref, prev_ref, cur_ref, do_ref, c_ref, s_ref, dqkv_ref, dsink_ref, carry_ref, macc_ref):
        step = pl.program_id(0)
        blk = nb - 1 - step

        @pl.when(step == 0)
        def _():
            dsink_ref[...] = jnp.zeros_like(dsink_ref)
            carry_ref[...] = jnp.zeros_like(carry_ref)
            macc_ref[...] = jnp.zeros_like(macc_ref)

        mask = _attn_mask(blk)
        mask2 = jnp.concatenate([mask, mask], axis=0)
        lane = lax.broadcasted_iota(jnp.int32, (3 * BLK, 128), 1)
        k128 = _gather_keys(meta_ref, prev_ref, cur_ref, ATTN_WIDTH)
        v128 = _gather_keys(meta_ref, prev_ref, cur_ref, ATTN_WIDTH + KV_WIDTH)
        cos_b, sin_b = c_ref[...], s_ref[...]
        dk_heads, dv_heads = [], []
        for kv in range(2):
            k_lo, k_hi = _head_halves(k128, kv)
            v_lo, v_hi = _head_halves(v128, kv)
            lanes0, lanes1 = _pair_lanes(kv)
            q2 = jnp.concatenate([cur_ref[:, lanes0], cur_ref[:, lanes1]], axis=0)
            do2 = jnp.concatenate([do_ref[:, lanes0], do_ref[:, lanes1]], axis=0)
            ds_half, p_half = [], []
            for half, (k_h, v_h) in enumerate(((k_lo, v_lo), (k_hi, v_hi))):
                head = 4 * kv + half
                p, p_sink = _softmax_with_sink(_nt(q2, k_h), mask2, _stacked_sinks(sink_ref, head))
                dp = _nt(do2, v_h)
                dsum = jnp.sum(p * dp, axis=-1, keepdims=True)
                ds_half.append((p * (dp - dsum) * (HEAD_DIM ** -0.5)).astype(BF16))
                p_half.append(p.astype(BF16))
                dsink = p_sink * dsum
                for part, h in ((0, head), (1, head + 2)):
                    total = -jnp.sum(dsink[part * BLK:(part + 1) * BLK], axis=0, keepdims=True)
                    dsink_ref[h:h + 1, :] += jnp.broadcast_to(total, (1, 128))
            dq2 = _nn(ds_half[0], k_lo) + _nn(ds_half[1], k_hi)
            dqkv_ref[:, lanes0] = _rope(dq2[0:BLK], cos_b, sin_b, -1.0).astype(BF16)
            dqkv_ref[:, lanes1] = _rope(dq2[BLK:2 * BLK], cos_b, sin_b, -1.0).astype(BF16)
            dk_acc = jnp.where(lane < HEAD_DIM, _tn(ds_half[0], q2), _tn(ds_half[1], q2))
            dv_acc = jnp.where(lane < HEAD_DIM, _tn(p_half[0], do2), _tn(p_half[1], do2))
            dk_heads.append(dk_acc + pltpu.roll(dk_acc, HEAD_DIM, 1))
            dv_heads.append(dv_acc + pltpu.roll(dv_acc, HEAD_DIM, 1))
        dkv = jnp.concatenate([jnp.where(lane < HEAD_DIM, dk_heads[0], dk_heads[1]),
                               jnp.where(lane < HEAD_DIM, dv_heads[0], dv_heads[1])], axis=1)
        macc_ref[...] += dkv[0:BLK]
        is_last = (blk == 0).astype(F32)
        mine = dkv[2 * BLK:3 * BLK] + carry_ref[...] + is_last * macc_ref[...]
        carry_ref[...] = dkv[BLK:2 * BLK]
        dqkv_ref[:, ATTN_WIDTH:ATTN_WIDTH + KV_WIDTH] = _rope(mine[:, 0:128], cos_b, sin_b, -1.0).astype(BF16)
        dqkv_ref[:, ATTN_WIDTH + KV_WIDTH:QKV_WIDTH] = mine[:, 128:256].astype(BF16)

    rev = lambda i: nb - 1 - i
    blk_spec = lambda f: pl.BlockSpec((BLK, QKV_WIDTH), f)
    return pl.pallas_call(
        body, name=name, grid=(nb,),
        in_specs=[_SMEM, blk_spec(lambda i: (0, 0)), blk_spec(lambda i: (jnp.maximum(rev(i) - 1, 0), 0)),
                  blk_spec(lambda i: (rev(i), 0)), pl.BlockSpec((BLK, ATTN_WIDTH), lambda i: (rev(i), 0)),
                  pl.BlockSpec((BLK, 128), lambda i: (rev(i), 0)), pl.BlockSpec((BLK, 128), lambda i: (rev(i), 0))],
        out_specs=[pl.BlockSpec((BLK, QKV_WIDTH), lambda i: (rev(i), 0)), _acc_spec((N_Q_HEADS, 128))],
        out_shape=[jax.ShapeDtypeStruct((t, QKV_WIDTH), BF16), jax.ShapeDtypeStruct((N_Q_HEADS, 128), F32)],
        scratch_shapes=[pltpu.VMEM((BLK, 256), F32), pltpu.VMEM((BLK, 256), F32)],
        compiler_params=_params(("arbitrary",)),
    )(sinks, qkv, qkv, qkv, do, cos_t, sin_t)


def _cmul(ar, ai, br, bi):
    return ar * br - ai * bi, ar * bi + ai * br


def ssm_prep(a_re, a_im, log_dt, b_re_t, b_im_t, name):
    def body(ar_ref, ai_ref, ldt_ref, br_ref, bi_ref, lr_ref, li_ref, bbr_ref, bbi_ref):
        ar, ai = ar_ref[...], ai_ref[...]
        dt = jnp.exp(ldt_ref[...])
        mag = jnp.exp(ar * dt)
        lr = mag * jnp.cos(ai * dt)
        li = mag * jnp.sin(ai * dt)
        den = ar * ar + ai * ai
        nr = lr - 1.0
        cr = ((nr * ar + li * ai) / den)[:, None, :]
        ci = ((li * ar - nr * ai) / den)[:, None, :]
        br, bi = br_ref[...], bi_ref[...]
        lr_ref[...] = lr
        li_ref[...] = li
        bbr_ref[...] = cr * br - ci * bi
        bbi_ref[...] = cr * bi + ci * br

    gp = jax.ShapeDtypeStruct(a_re.shape, F32)
    gcp = jax.ShapeDtypeStruct(b_re_t.shape, F32)
    return pl.pallas_call(body, name=name, out_shape=[gp, gp, gcp, gcp],
                          in_specs=[_VMEM] * 5, out_specs=[_VMEM] * 4)(a_re, a_im, log_dt, b_re_t, b_im_t)


def ssm_prep_bwd(a_re, a_im, log_dt, b_re_t, b_im_t, dl_re, dl_im, dbb_re, dbb_im, name):
    def body(ar_ref, ai_ref, ldt_ref, br_ref, bi_ref, dlr_ref, dli_ref, dbbr_ref, dbbi_ref,
             dar_ref, dai_ref, dldt_ref, dbr_ref, dbi_ref):
        ar, ai = ar_ref[...], ai_ref[...]
        dt = jnp.exp(ldt_ref[...])
        mag = jnp.exp(ar * dt)
        lr = mag * jnp.cos(ai * dt)
        li = mag * jnp.sin(ai * dt)
        den = ar * ar + ai * ai
        nr = lr - 1.0
        cr = (nr * ar + li * ai) / den
        ci = (li * ar - nr * ai) / den
        br, bi = br_ref[...], bi_ref[...]
        dbbr, dbbi = dbbr_ref[...], dbbi_ref[...]
        dbr_ref[...] = cr[:, None, :] * dbbr + ci[:, None, :] * dbbi
        dbi_ref[...] = cr[:, None, :] * dbbi - ci[:, None, :] * dbbr
        dcr = jnp.sum(br * dbbr + bi * dbbi, axis=1)
        dci = jnp.sum(br * dbbi - bi * dbbr, axis=1)
        d_num_r = dcr / den
        d_num_i = dci / den
        d_den = -(dcr * cr + dci * ci) / den
        d_lr = dlr_ref[...] + d_num_r * ar - d_num_i * ai
        d_li = dli_ref[...] + d_num_r * ai + d_num_i * ar
        d_ar = d_num_r * nr + d_num_i * li + d_den * 2.0 * ar
        d_ai = d_num_r * li - d_num_i * nr + d_den * 2.0 * ai
        d_mag = (d_lr * lr + d_li * li) / mag
        d_theta = d_li * lr - d_lr * li
        d_ardt = d_mag * mag
        dar_ref[...] = d_ar + d_ardt * dt
        dai_ref[...] = d_ai + d_theta * dt
        d_dt = jnp.sum(d_ardt * ar + d_theta * ai, axis=1, keepdims=True)
        dldt_ref[...] = d_dt * dt

    gp = jax.ShapeDtypeStruct(a_re.shape, F32)
    gcp = jax.ShapeDtypeStruct(b_re_t.shape, F32)
    return pl.pallas_call(body, name=name, out_shape=[gp, gp, jax.ShapeDtypeStruct(log_dt.shape, F32), gcp, gcp],
                          in_specs=[_VMEM] * 9, out_specs=[_VMEM] * 5,
                          )(a_re, a_im, log_dt, b_re_t, b_im_t, dl_re, dl_im, dbb_re, dbb_im)


N_CHUNK = 4
U_CHUNK = SSM_WIDTH // N_CHUNK
H_CHUNK = STATE_WIDTH // N_CHUNK
SUB = 8


def _block_diag_b(bb):
    x = bb.reshape(N_CHUNK, 8, SSM_GROUP, 1, SSM_STATE)
    same = (jnp.arange(8)[:, None] == jnp.arange(8)[None, :])[None, :, None, :, None]
    return jnp.where(same, x, 0.0).reshape(N_CHUNK, U_CHUNK, H_CHUNK)


def _block_diag_c(c):
    x = jnp.swapaxes(c.reshape(N_CHUNK, 8, SSM_GROUP, SSM_STATE), 2, 3)[:, :, :, None, :]
    same = (jnp.arange(8)[:, None] == jnp.arange(8)[None, :])[None, :, None, :, None]
    return jnp.where(same, x, 0.0).reshape(N_CHUNK, H_CHUNK, U_CHUNK)


def _diag_of_b(m):
    x = m.reshape(N_CHUNK, 8, SSM_GROUP, 8, SSM_STATE)
    return jnp.stack([x[:, g, :, g, :] for g in range(8)], axis=1).reshape(SSM_GROUPS, SSM_GROUP, SSM_STATE)


def _diag_of_c(m):
    x = m.reshape(N_CHUNK, 8, SSM_STATE, 8, SSM_GROUP)
    d = jnp.stack([x[:, g, :, g, :] for g in range(8)], axis=1)
    return jnp.swapaxes(d, 2, 3).reshape(SSM_GROUPS, SSM_GROUP, SSM_STATE)


def _lambda_tables(lr, li, reverse):
    p1 = (lr, li)
    p2 = _cmul(*p1, *p1)
    p4 = _cmul(*p2, *p2)
    rows = [p1]
    for _ in range(SUB - 1):
        rows.append(_cmul(*rows[-1], *p1))
    if reverse:
        rows = rows[::-1]
    return p1, p2, p4, (jnp.concatenate([r[0] for r in rows], axis=0), jnp.concatenate([r[1] for r in rows], axis=0))


def _scan8(xr, xi, pows, table, cr, ci, reverse):
    row = lax.broadcasted_iota(jnp.int32, xr.shape, 0)
    for d, (pr, pi) in zip((1, 2, 4), pows):
        if reverse:
            sr, si = pltpu.roll(xr, SUB - d, 0), pltpu.roll(xi, SUB - d, 0)
            keep = row < SUB - d
        else:
            sr, si = pltpu.roll(xr, d, 0), pltpu.roll(xi, d, 0)
            keep = row >= d
        sr = jnp.where(keep, sr, 0.0)
        si = jnp.where(keep, si, 0.0)
        xr, xi = xr + pr * sr - pi * si, xi + pr * si + pi * sr
    tr, ti = table
    return xr + tr * cr - ti * ci, xi + tr * ci + ti * cr


def _gelu_and_grad(y):
    k0 = math.sqrt(2.0 / math.pi)
    inner = k0 * (y + 0.044715 * y * y * y)
    th = jnp.tanh(inner)
    g = 0.5 * y * (1.0 + th)
    dg = 0.5 * (1.0 + th) + 0.5 * y * (1.0 - th * th) * k0 * (1.0 + 3.0 * 0.044715 * y * y)
    return g, dg


SCAN_TILE = TOKEN_TILE
SEG = SCAN_TILE // SUB
SCAN_LANES = 512


def _perm_matrix(to_segments):
    a = lax.broadcasted_iota(jnp.int32, (SCAN_TILE, SCAN_TILE), 0)
    b = lax.broadcasted_iota(jnp.int32, (SCAN_TILE, SCAN_TILE), 1)
    rho, time = (a, b) if to_segments else (b, a)
    return (time == (rho % SUB) * SEG + rho // SUB).astype(BF16)


def _permute_f32(p, x):
    hi = x.astype(BF16)
    r1 = x - hi.astype(F32)
    mid = r1.astype(BF16)
    lo = (r1 - mid.astype(F32)).astype(BF16)
    return _nn(p, hi) + _nn(p, mid) + _nn(p, lo)


def _power_table(lr, li, pr_ref, pi_ref):
    cur = (lr, li)
    for r in range(SEG):
        pr_ref[r:r + 1, :] = cur[0]
        pi_ref[r:r + 1, :] = cur[1]
        cur = _cmul(*cur, lr, li)


def _segment_scan(xr_ref, xi_ref, lanes, lam, table_row, cr_ref, ci_ref, reverse, extra=None):
    lr, li = lam
    row = lax.broadcasted_iota(jnp.int32, (SUB, SCAN_LANES), 0)

    def rows_of(k):
        r = SEG - 1 - k if reverse else k
        return pl.ds(pl.multiple_of(r * SUB, SUB), SUB)

    def first(k, st):
        sr, si = st
        rows = rows_of(k)
        nr = lr * sr - li * si + xr_ref[rows, lanes]
        ni = lr * si + li * sr + xi_ref[rows, lanes]
        xr_ref[rows, lanes] = nr
        xi_ref[rows, lanes] = ni
        return nr, ni

    zero = jnp.zeros((SUB, SCAN_LANES), F32)
    er, ei = lax.fori_loop(0, SEG, first, (zero, zero))
    l16 = table_row(SEG - 1)
    q1, q2, q4, tab = _lambda_tables(l16[0], l16[1], reverse)
    c_r, c_i = cr_ref[:, lanes], ci_ref[:, lanes]
    gr, gi = _scan8(er, ei, (q1, q2, q4), tab, c_r, c_i, reverse)
    if reverse:
        cin_r = jnp.where(row == SUB - 1, c_r, pltpu.roll(gr, SUB - 1, 0))
        cin_i = jnp.where(row == SUB - 1, c_i, pltpu.roll(gi, SUB - 1, 0))
        cr_ref[:, lanes] = gr[0:1]
        ci_ref[:, lanes] = gi[0:1]
    else:
        cin_r = jnp.where(row == 0, c_r, pltpu.roll(gr, 1, 0))
        cin_i = jnp.where(row == 0, c_i, pltpu.roll(gi, 1, 0))
        cr_ref[:, lanes] = gr[SUB - 1:SUB]
        ci_ref[:, lanes] = gi[SUB - 1:SUB]

    def second(k, carry):
        rows = rows_of(k)
        tr, ti = table_row(k)
        ar = xr_ref[rows, lanes] + tr * cin_r - ti * cin_i
        ai = xi_ref[rows, lanes] + tr * cin_i + ti * cin_r
        xr_ref[rows, lanes] = ar
        xi_ref[rows, lanes] = ai
        if extra is None:
            return carry
        return extra(rows, carry, ar, ai)

    init = 0 if extra is None else (cin_r, cin_i, zero, zero)
    return lax.fori_loop(0, SEG, second, init)


def ssm_fwd(u, lam_re, lam_im, bb_re, bb_im, cc_re, cc_im, d_skip, name):
    t = u.shape[0]
    tt = SCAN_TILE

    def body(u_ref, lr_ref, li_ref, bbr_ref, bbi_ref, ccr_ref, cci_ref, d_ref, yg_ref, hr_ref, hi_ref,
             cr_ref, ci_ref, pr_ref, pi_ref, up_ref, y_ref):
        @pl.when(pl.program_id(0) == 0)
        def _():
            cr_ref[...] = jnp.zeros_like(cr_ref)
            ci_ref[...] = jnp.zeros_like(ci_ref)
            _power_table(lr_ref[...], li_ref[...], pr_ref, pi_ref)

        up_ref[...] = _permute_f32(_perm_matrix(True), u_ref[...])
        ub = up_ref[...].astype(BF16)
        for j in range(N_CHUNK):
            hs = slice(j * H_CHUNK, (j + 1) * H_CHUNK)
            us = slice(j * U_CHUNK, (j + 1) * U_CHUNK)
            hr_ref[:, hs] = _nn(ub[:, us], bbr_ref[j])
            hi_ref[:, hs] = _nn(ub[:, us], bbi_ref[j])
        for c in range(STATE_WIDTH // SCAN_LANES):
            lanes = slice(c * SCAN_LANES, (c + 1) * SCAN_LANES)
            _segment_scan(hr_ref, hi_ref, lanes, (lr_ref[:, lanes], li_ref[:, lanes]),
                          lambda k, lanes=lanes: (pr_ref[pl.ds(k, 1), lanes], pi_ref[pl.ds(k, 1), lanes]),
                          cr_ref, ci_ref, False)
        for j in range(N_CHUNK):
            hs = slice(j * H_CHUNK, (j + 1) * H_CHUNK)
            us = slice(j * U_CHUNK, (j + 1) * U_CHUNK)
            y = (_nn(hr_ref[:, hs].astype(BF16), ccr_ref[j]) - _nn(hi_ref[:, hs].astype(BF16), cci_ref[j])
                 + d_ref[:, us] * up_ref[:, us])
            y_ref[:, us] = _gelu_and_grad(y)[0]
        yg_ref[...] = _nn(_perm_matrix(False), y_ref[...].astype(BF16)).astype(BF16)

    return pl.pallas_call(
        body, name=name, grid=(t // tt,),
        in_specs=[_row_spec(tt, SSM_WIDTH), _VMEM, _VMEM, _VMEM, _VMEM, _VMEM, _VMEM, _VMEM],
        out_specs=[_row_spec(tt, SSM_WIDTH), _row_spec(tt, STATE_WIDTH), _row_spec(tt, STATE_WIDTH)],
        out_shape=[jax.ShapeDtypeStruct((t, SSM_WIDTH), BF16), jax.ShapeDtypeStruct((t, STATE_WIDTH), F32),
                   jax.ShapeDtypeStruct((t, STATE_WIDTH), F32)],
        scratch_shapes=[pltpu.VMEM((1, STATE_WIDTH), F32), pltpu.VMEM((1, STATE_WIDTH), F32),
                        pltpu.VMEM((SEG, STATE_WIDTH), F32), pltpu.VMEM((SEG, STATE_WIDTH), F32),
                        pltpu.VMEM((tt, SSM_WIDTH), F32), pltpu.VMEM((tt, SSM_WIDTH), F32)],
        compiler_params=_params(("arbitrary",)),
    )(u, lam_re, lam_im, bb_re, bb_im, cc_re, cc_im, d_skip)


def ssm_bwd(dyg, u, h_re, h_im, lam_re, lam_im, bb_re, bb_im, cc_re, cc_im, d_skip, name):
    t = u.shape[0]
    tt = SCAN_TILE
    nt = t // tt

    def body(dyg_ref, u_ref, hr_ref, hi_ref, lr_ref, li_ref, bbr_ref, bbi_ref, ccr_ref, cci_ref, d_ref,
             du_ref, dlr_ref, dli_ref, dbbr_ref, dbbi_ref, dccr_ref, dcci_ref, dd_ref,
             ar_ref, ai_ref, cr_ref, ci_ref, pr_ref, pi_ref, up_ref, dy_ref, dup_ref):
        step = pl.program_id(0)
        tile = nt - 1 - step

        @pl.when(step == 0)
        def _():
            for ref in (cr_ref, ci_ref, dlr_ref, dli_ref, dbbr_ref, dbbi_ref, dccr_ref, dcci_ref, dd_ref):
                ref[...] = jnp.zeros_like(ref)
            _power_table(lr_ref[...], li_ref[...], pr_ref, pi_ref)

        to_segments = _perm_matrix(True)
        up_ref[...] = _permute_f32(to_segments, u_ref[...])
        dy_ref[...] = _permute_f32(to_segments, dyg_ref[...])
        uv = up_ref[...]
        ub = uv.astype(BF16)
        dskip = d_ref[...]
        for j in range(N_CHUNK):
            hs = slice(j * H_CHUNK, (j + 1) * H_CHUNK)
            us = slice(j * U_CHUNK, (j + 1) * U_CHUNK)
            hrb = hr_ref[:, hs].astype(BF16)
            hib = hi_ref[:, hs].astype(BF16)
            y = _nn(hrb, ccr_ref[j]) - _nn(hib, cci_ref[j]) + dskip[:, us] * uv[:, us]
            dy = dy_ref[:, us] * _gelu_and_grad(y)[1]
            dy_ref[:, us] = dy
            dyb = dy.astype(BF16)
            dccr_ref[j] += _tn(hrb, dyb)
            dcci_ref[j] -= _tn(hib, dyb)
            ar_ref[:, hs] = _nt(dyb, ccr_ref[j])
            ai_ref[:, hs] = -_nt(dyb, cci_ref[j])
        dd_ref[...] += jnp.sum(dy_ref[...] * uv, axis=0, keepdims=True)

        for c in range(STATE_WIDTH // SCAN_LANES):
            lanes = slice(c * SCAN_LANES, (c + 1) * SCAN_LANES)

            def dlambda(rows, carry, ar, ai, lanes=lanes):
                nr, ni, accr, acci = carry
                hr, hi = hr_ref[rows, lanes], hi_ref[rows, lanes]
                return ar, ai, accr + nr * hr + ni * hi, acci + ni * hr - nr * hi

            _, _, accr, acci = _segment_scan(
                ar_ref, ai_ref, lanes, (lr_ref[:, lanes], -li_ref[:, lanes]),
                lambda k, lanes=lanes: (pr_ref[pl.ds(k, 1), lanes], -pi_ref[pl.ds(k, 1), lanes]),
                cr_ref, ci_ref, True, dlambda)
            dlr_ref[:, lanes] += accr
            dli_ref[:, lanes] += acci

        rho = lax.broadcasted_iota(jnp.int32, (tt, U_CHUNK), 0)
        time = tile * tt + (rho % SUB) * SEG + rho // SUB
        for j in range(N_CHUNK):
            hs = slice(j * H_CHUNK, (j + 1) * H_CHUNK)
            us = slice(j * U_CHUNK, (j + 1) * U_CHUNK)
            arb = ar_ref[:, hs].astype(BF16)
            aib = ai_ref[:, hs].astype(BF16)
            dbbr_ref[j] += _tn(ub[:, us], arb)
            dbbi_ref[j] += _tn(ub[:, us], aib)
            du = _nt(arb, bbr_ref[j]) + _nt(aib, bbi_ref[j]) + dy_ref[:, us] * dskip[:, us]
            dup_ref[:, us] = jnp.where(time >= PAD_FRONT, du, 0.0)
        du_ref[...] = _nn(_perm_matrix(False), dup_ref[...].astype(BF16)).astype(BF16)

    rev = lambda i: (nt - 1 - i, 0)
    full = lambda shape: pl.BlockSpec(shape, lambda i: (0,) * len(shape))
    return pl.pallas_call(
        body, name=name, grid=(nt,),
        in_specs=[pl.BlockSpec((tt, SSM_WIDTH), rev), pl.BlockSpec((tt, SSM_WIDTH), rev),
                  pl.BlockSpec((tt, STATE_WIDTH), rev), pl.BlockSpec((tt, STATE_WIDTH), rev),
                  _VMEM, _VMEM, _VMEM, _VMEM, _VMEM, _VMEM, _VMEM],
        out_specs=[pl.BlockSpec((tt, SSM_WIDTH), rev), full((SUB, STATE_WIDTH)), full((SUB, STATE_WIDTH)),
                   full((N_CHUNK, U_CHUNK, H_CHUNK)), full((N_CHUNK, U_CHUNK, H_CHUNK)),
                   full((N_CHUNK, H_CHUNK, U_CHUNK)), full((N_CHUNK, H_CHUNK, U_CHUNK)), full((1, SSM_WIDTH))],
        out_shape=[jax.ShapeDtypeStruct((t, SSM_WIDTH), BF16),
                   jax.ShapeDtypeStruct((SUB, STATE_WIDTH), F32), jax.ShapeDtypeStruct((SUB, STATE_WIDTH), F32),
                   jax.ShapeDtypeStruct((N_CHUNK, U_CHUNK, H_CHUNK), F32),
                   jax.ShapeDtypeStruct((N_CHUNK, U_CHUNK, H_CHUNK), F32),
                   jax.ShapeDtypeStruct((N_CHUNK, H_CHUNK, U_CHUNK), F32),
                   jax.ShapeDtypeStruct((N_CHUNK, H_CHUNK, U_CHUNK), F32),
                   jax.ShapeDtypeStruct((1, SSM_WIDTH), F32)],
        scratch_shapes=[pltpu.VMEM((tt, STATE_WIDTH), F32), pltpu.VMEM((tt, STATE_WIDTH), F32),
                        pltpu.VMEM((1, STATE_WIDTH), F32), pltpu.VMEM((1, STATE_WIDTH), F32),
                        pltpu.VMEM((SEG, STATE_WIDTH), F32), pltpu.VMEM((SEG, STATE_WIDTH), F32),
                        pltpu.VMEM((tt, SSM_WIDTH), F32), pltpu.VMEM((tt, SSM_WIDTH), F32),
                        pltpu.VMEM((tt, SSM_WIDTH), F32)],
        compiler_params=_params(("arbitrary",)),
    )(dyg, u, h_re, h_im, lam_re, lam_im, bb_re, bb_im, cc_re, cc_im, d_skip)


def merge_fwd(h, o, yg, gates, wap_t, wv_t, wgg_t, wout, name):
    t, d = h.shape
    tm = TOKEN_TILE

    def body(h_ref, o_ref, yg_ref, gt_ref, wap_ref, wv_ref, wgg_ref, wout_ref, ho_ref, mg_ref, a_ref, sv_ref, sg_ref):
        att = _nt(o_ref[...], wap_ref[...])
        ygv = yg_ref[...]
        sv = _nt(ygv, wv_ref[...])
        sg = _nt(ygv, wgg_ref[...])
        a_ref[...] = att.astype(BF16)
        sv_ref[...] = sv.astype(BF16)
        sg_ref[...] = sg.astype(BF16)
        merged = (jax.nn.sigmoid(gt_ref[:, 0:d].astype(F32)) * att
                  + jax.nn.sigmoid(gt_ref[:, d:2 * d].astype(F32)) * (sv * jax.nn.sigmoid(sg))).astype(BF16)
        mg_ref[...] = merged
        ho_ref[...] = h_ref[...] + _nn(merged, wout_ref[...])

    return pl.pallas_call(
        body, name=name, grid=(t // tm,),
        in_specs=[_row_spec(tm, d), _row_spec(tm, ATTN_WIDTH), _row_spec(tm, SSM_WIDTH), _row_spec(tm, 2 * d),
                  _VMEM, _VMEM, _VMEM, _VMEM],
        out_specs=[_row_spec(tm, d), _row_spec(tm, d), _row_spec(tm, d), _row_spec(tm, d), _row_spec(tm, d)],
        out_shape=[jax.ShapeDtypeStruct((t, d), F32), jax.ShapeDtypeStruct((t, d), BF16),
                   jax.ShapeDtypeStruct((t, d), BF16), jax.ShapeDtypeStruct((t, d), BF16),
                   jax.ShapeDtypeStruct((t, d), BF16)],
        compiler_params=_params(("arbitrary",)),
    )(h, o, yg, gates, wap_t, wv_t, wgg_t, wout)


def merge_bwd(dh, gates, att, sv, sg, wap_t, wv_t, wgg_t, wout, dep, name):
    t, d = dh.shape
    tm = TOKEN_TILE

    def body(dh_ref, gt_ref, a_ref, sv_ref, sg_ref, wap_ref, wv_ref, wgg_ref, wout_ref, dep_ref,
             dgt_ref, da_ref, dsv_ref, dsg_ref, do_ref, dyg_ref, dhb_ref):
        dhb = dh_ref[...].astype(BF16)
        dhb_ref[...] = dhb
        dm = _nt(dhb, wout_ref[...])
        sig_a = jax.nn.sigmoid(gt_ref[:, 0:d].astype(F32))
        sig_s = jax.nn.sigmoid(gt_ref[:, d:2 * d].astype(F32))
        sig_g = jax.nn.sigmoid(sg_ref[...].astype(F32))
        svv = sv_ref[...].astype(F32)
        dgt_ref[:, 0:d] = (dm * a_ref[...].astype(F32) * sig_a * (1.0 - sig_a)).astype(BF16)
        dgt_ref[:, d:2 * d] = (dm * (svv * sig_g) * sig_s * (1.0 - sig_s)).astype(BF16)
        da = (dm * sig_a).astype(BF16)
        d_s = dm * sig_s
        dsv = (d_s * sig_g).astype(BF16)
        dsg = (d_s * svv * sig_g * (1.0 - sig_g)).astype(BF16)
        da_ref[...] = da
        dsv_ref[...] = dsv
        dsg_ref[...] = dsg
        do_ref[...] = _nn(da, wap_ref[...]).astype(BF16)
        dyg_ref[...] = _nn(dsv, wv_ref[...]) + _nn(dsg, wgg_ref[...])

    return pl.pallas_call(
        body, name=name, grid=(t // tm,),
        in_specs=[_row_spec(tm, d), _row_spec(tm, 2 * d), _row_spec(tm, d), _row_spec(tm, d), _row_spec(tm, d),
                  _VMEM, _VMEM, _VMEM, _VMEM, _ANY],
        out_specs=[_row_spec(tm, 2 * d), _row_spec(tm, d), _row_spec(tm, d), _row_spec(tm, d),
                   _row_spec(tm, ATTN_WIDTH), _row_spec(tm, SSM_WIDTH), _row_spec(tm, d)],
        out_shape=[jax.ShapeDtypeStruct((t, 2 * d), BF16), jax.ShapeDtypeStruct((t, d), BF16),
                   jax.ShapeDtypeStruct((t, d), BF16), jax.ShapeDtypeStruct((t, d), BF16),
                   jax.ShapeDtypeStruct((t, ATTN_WIDTH), BF16), jax.ShapeDtypeStruct((t, SSM_WIDTH), F32),
                   jax.ShapeDtypeStruct((t, d), BF16)],
        compiler_params=_params(("arbitrary",)),
    )(dh, gates, att, sv, sg, wap_t, wv_t, wgg_t, wout, dep)


def _adamw_math(w, g, m, v):
    mn = ADAM_B1 * m + (1.0 - ADAM_B1) * g
    vn = ADAM_B2 * v + (1.0 - ADAM_B2) * (g * g)
    m_hat = mn / (1.0 - ADAM_B1 ** ADAM_STEP)
    v_hat = vn / (1.0 - ADAM_B2 ** ADAM_STEP)
    return -ADAM_LR * (m_hat / (jnp.sqrt(v_hat) + ADAM_EPS) + ADAM_WD * w), mn, vn


def sum_adamw_layer(me, landed, partial, w, m, v, layer, prev, name):
    _, rows, cols = w.shape
    tr = rows // 2 if rows % 32 == 0 else rows
    steps = rows // tr

    def body(me_ref, land_ref, own_ref, w_ref, m_ref, v_ref, *rest):
        go_ref, d_ref, mo_ref, vo_ref = rest[-4:]
        who = me_ref[0]
        gv = land_ref[who ^ 1].astype(F32)
        for p in range(2, N_DEV):
            gv = gv + land_ref[who ^ p].astype(F32)
        gv = gv + own_ref[...].astype(F32)
        go_ref[0] = gv
        d_ref[0], mo_ref[0], vo_ref[0] = _adamw_math(w_ref[0], gv, m_ref[0], v_ref[0])

    spec3 = pl.BlockSpec((1, tr, cols), lambda r, me_ref: (layer, r, 0))
    out = jax.ShapeDtypeStruct(w.shape, F32)
    extra = [] if prev is None else list(prev)
    grid_spec = pltpu.PrefetchScalarGridSpec(
        num_scalar_prefetch=1, grid=(steps,),
        in_specs=[pl.BlockSpec((N_DEV, tr, cols), lambda r, me_ref: (0, r, 0)),
                  pl.BlockSpec((tr, cols), lambda r, me_ref: (me_ref[0] * steps + r, 0)),
                  spec3, spec3, spec3] + [_ANY] * len(extra),
        out_specs=[spec3] * 4)
    return pl.pallas_call(
        body, name=name, grid_spec=grid_spec, out_shape=[out] * 4,
        input_output_aliases={6 + j: j for j in range(len(extra))},
        compiler_params=_params(("arbitrary",)),
    )(me, landed, partial, w, m, v, *extra)


def adamw(w, g, m, v, name, minor_swap=False):
    if minor_swap:
        d, mn, vn = adamw(*[jnp.swapaxes(a, -1, -2) for a in (w, g, m, v)], name)
        return jnp.swapaxes(d, -1, -2), jnp.swapaxes(mn, -1, -2), jnp.swapaxes(vn, -1, -2)
    shape = w.shape
    as2d = lambda a: a.reshape(-1, shape[-1]) if a.ndim >= 2 else a.reshape(1, -1)
    w2, g2, m2, v2 = as2d(w), as2d(g), as2d(m), as2d(v)
    rows, cols = w2.shape
    tr = rows
    for cand in (1024, 704, 512, 256):
        if rows > cand and rows % cand == 0:
            tr = cand
            break

    def body(w_ref, g_ref, m_ref, v_ref, d_ref, mo_ref, vo_ref):
        d_ref[...], mo_ref[...], vo_ref[...] = _adamw_math(w_ref[...], g_ref[...], m_ref[...], v_ref[...])

    spec = _row_spec(tr, cols)
    out = jax.ShapeDtypeStruct((rows, cols), F32)
    d, mn, vn = pl.pallas_call(
        body, name=name, grid=(rows // tr,), in_specs=[spec] * 4, out_specs=[spec] * 3, out_shape=[out] * 3,
        compiler_params=_params(("arbitrary",)),
    )(w2, g2, m2, v2)
    return d.reshape(shape), mn.reshape(shape), vn.reshape(shape)


def _my_index():
    return 4 * lax.axis_index("x") + 2 * lax.axis_index("y") + lax.axis_index("c")


def _peer(p):
    return (lax.axis_index("x") ^ ((p >> 2) & 1), lax.axis_index("y") ^ ((p >> 1) & 1), lax.axis_index("c") ^ (p & 1))


_HBM = pl.BlockSpec(memory_space=pltpu.HBM)
_SEM = pl.BlockSpec(memory_space=pltpu.SEMAPHORE)
_EFFECT = pltpu.SideEffectType.DATAFLOW_SIDE_EFFECTING


class Exchange:
    RELAYED = (2, 4, 6)

    def __init__(self, srcs, scatter, name, relay=False):
        self.n = n = len(srcs)
        self.scatter = scatter
        self.name = name
        self.relayed = relay
        assert not (relay and scatter)
        self.direct = (1,) + self.RELAYED if relay else tuple(range(1, N_DEV))
        widths = sorted({s.shape[1] for s in srcs}, reverse=True)
        self.ncls = len(widths)
        self.cls = [widths.index(s.shape[1]) for s in srcs]
        self.cnts = [s.shape[0] // N_DEV if scatter else s.shape[0] for s in srcs]
        self.totals = [sum(c for c, k in zip(self.cnts, self.cls) if k == w) for w in range(self.ncls)]
        self.sizer = [max((k for k in range(n) if self.cls[k] == w), key=lambda k: self.cnts[k])
                      for w in range(self.ncls)]
        assert all(N_DEV * self.cnts[self.sizer[w]] >= self.totals[w] for w in range(self.ncls))
        if scatter:
            self.land_shapes = [(N_DEV, c, s.shape[1]) for s, c in zip(srcs, self.cnts)]
        else:
            self.land_shapes = [(N_DEV * c, s.shape[1]) for s, c in zip(srcs, self.cnts)]
        self.dtypes = [s.dtype for s in srcs]

    def _block(self, k, who):
        return pl.ds(pl.multiple_of(who * self.cnts[k], 16), self.cnts[k])

    def _sem(self, p, w):
        return (p - 1) * self.ncls + w

    def start(self, srcs, after):
        n = self.n

        def body(*refs):
            src, land = refs[:n], refs[n:2 * n]
            send_sems, recv_sems = refs[2 * n + 1], refs[2 * n + 2]
            token = refs[-1]
            me = _my_index()
            for p in self.direct:
                for k in range(n):
                    if self.scatter:
                        s_ref, d_ref = src[k].at[self._block(k, me ^ p), :], land[k].at[me]
                    else:
                        s_ref, d_ref = src[k], land[k].at[self._block(k, me), :]
                    pltpu.make_async_remote_copy(
                        src_ref=s_ref, dst_ref=d_ref, send_sem=send_sems.at[self._sem(p, self.cls[k])],
                        recv_sem=recv_sems.at[self._sem(p, self.cls[k])], device_id=_peer(p),
                        device_id_type=MESH).start()
            token[...] = jnp.zeros_like(token)

        sems = pltpu.SemaphoreType.DMA(((N_DEV - 1) * self.ncls,))
        thru = [pltpu.HBM(s.shape, s.dtype) for s in srcs] + [pltpu.HBM(shp, dt) for shp, dt in
                                                               zip(self.land_shapes, self.dtypes)]
        lands = [pltpu.with_memory_space_constraint(lax.empty(shp, dt), pltpu.HBM)
                 for shp, dt in zip(self.land_shapes, self.dtypes)]
        out = pl.pallas_call(
            body, name=self.name + "_start",
            in_specs=[_HBM] * (2 * n) + [_ANY],
            out_shape=[sems, sems] + thru + [jax.ShapeDtypeStruct((8, 128), F32)],
            out_specs=[_SEM, _SEM] + [_HBM] * (2 * n) + [_VMEM],
            input_output_aliases={j: 2 + j for j in range(2 * n)},
            compiler_params=pltpu.CompilerParams(has_side_effects=_EFFECT),
        )(*[pltpu.with_memory_space_constraint(s, pltpu.HBM) for s in srcs], *lands, after)
        return out[:-1], out[-1]

    def _span_copy(self, src, land, w, send_sem, recv_sem, p):
        big = src[self.sizer[w]] if self.scatter else land[self.sizer[w]]
        span = big.at[pl.ds(0, self.totals[w]), :]
        return pltpu.make_async_remote_copy(src_ref=span, dst_ref=span, send_sem=send_sem, recv_sem=recv_sem,
                                            device_id=_peer(p), device_id_type=MESH)

    def relay(self, state, after):
        n = self.n
        send_sems, recv_sems = state[0], state[1]
        thru = state[2:]
        after = list(after) if isinstance(after, (list, tuple)) else [after]
        first_out = 2 * n + 2 + len(after)

        def body(*refs):
            land = refs[n:2 * n]
            send_a, recv_a = refs[2 * n], refs[2 * n + 1]
            send_b, recv_b = refs[first_out], refs[first_out + 1]
            refs[-1][...] = jnp.zeros_like(refs[-1])
            me = _my_index()
            for p in self.RELAYED:
                for w in range(self.ncls):
                    self._span_copy(None, land, w, send_a.at[self._sem(p, w)], recv_a.at[self._sem(p, w)], p).wait_recv()
            for j, p in enumerate(self.RELAYED):
                for k in range(n):
                    rows = land[k].at[self._block(k, me ^ p), :]
                    pltpu.make_async_remote_copy(
                        src_ref=rows, dst_ref=rows, send_sem=send_b.at[j * self.ncls + self.cls[k]],
                        recv_sem=recv_b.at[j * self.ncls + self.cls[k]], device_id=_peer(1),
                        device_id_type=MESH).start()

        sems = pltpu.SemaphoreType.DMA((len(self.RELAYED) * self.ncls,))
        out = pl.pallas_call(
            body, name=self.name + "_relay",
            in_specs=[_HBM] * (2 * n) + [_SEM, _SEM] + [_ANY] * len(after),
            out_shape=[sems, sems] + [pltpu.HBM(a.shape, a.dtype) for a in thru] + [jax.ShapeDtypeStruct((8, 128), F32)],
            out_specs=[_SEM, _SEM] + [_HBM] * (2 * n) + [_VMEM],
            input_output_aliases={j: 2 + j for j in range(2 * n)},
            compiler_params=pltpu.CompilerParams(has_side_effects=_EFFECT),
        )(*thru, send_sems, recv_sems, *after)
        return [send_sems, recv_sems] + list(out[2:-1]) + [out[0], out[1]], out[-1]

    def wait(self, state, after):
        n = self.n
        send_sems, recv_sems = state[0], state[1]
        thru = state[2:2 + 2 * n]
        relay_sems = list(state[2 + 2 * n:])
        assert len(relay_sems) == (2 if self.relayed else 0)
        after = list(after) if isinstance(after, (list, tuple)) else [after]

        def body(*refs):
            src, land = refs[:n], refs[n:2 * n]
            send_a, recv_a = refs[2 * n], refs[2 * n + 1]
            for p in self.direct:
                for w in range(self.ncls):
                    copy = self._span_copy(src, land, w, send_a.at[self._sem(p, w)], recv_a.at[self._sem(p, w)], p)
                    copy.wait_send()
                    if not (self.relayed and p in self.RELAYED):
                        copy.wait_recv()
            if self.relayed:
                send_b, recv_b = refs[2 * n + 2], refs[2 * n + 3]
                for j in range(len(self.RELAYED)):
                    for w in range(self.ncls):
                        copy = self._span_copy(src, land, w, send_b.at[j * self.ncls + w],
                                               recv_b.at[j * self.ncls + w], 1)
                        copy.wait_send()
                        copy.wait_recv()

        out = pl.pallas_call(
            body, name=self.name + "_wait",
            in_specs=[_HBM] * (2 * n) + [_SEM] * (2 + len(relay_sems)) + [_ANY] * len(after),
            out_shape=[pltpu.HBM(a.shape, a.dtype) for a in thru], out_specs=[_HBM] * (2 * n),
            input_output_aliases={j: j for j in range(2 * n)},
            compiler_params=pltpu.CompilerParams(has_side_effects=_EFFECT),
        )(*thru, send_sems, recv_sems, *relay_sems, *after)
        return out[:n], out[n:]

    def place(self, lands, srcs):
        n = self.n
        assert not self.scatter

        def body(*refs):
            src, land = refs[n:2 * n], refs[2 * n:3 * n]
            bufs, sems = refs[3 * n:4 * n], refs[-1]
            me = _my_index()
            loads = [pltpu.make_async_copy(src[k], bufs[k], sems.at[k]) for k in range(n)]
            stores = [pltpu.make_async_copy(bufs[k], land[k].at[self._block(k, me), :], sems.at[k]) for k in range(n)]
            for cp in loads:
                cp.start()
            for k in range(n):
                loads[k].wait()
                stores[k].start()
            for cp in stores:
                cp.wait()

        return pl.pallas_call(
            body, name=self.name + "_place", in_specs=[_ANY] * (2 * n), out_specs=[_ANY] * n,
            out_shape=[jax.ShapeDtypeStruct(a.shape, a.dtype) for a in lands],
            input_output_aliases={j: j for j in range(n)},
            scratch_shapes=[pltpu.VMEM(s.shape, s.dtype) for s in srcs] + [pltpu.SemaphoreType.DMA((n,))],
        )(*lands, *srcs)


def sum_blocks(landed, full, name):
    _, cnt, cols = landed.shape

    def body(land_ref, full_ref, o_ref, own_ref, sem):
        me = _my_index()
        own = pltpu.make_async_copy(full_ref.at[pl.ds(pl.multiple_of(me * cnt, 16), cnt), :], own_ref, sem)
        own.start()
        acc = land_ref[me ^ 1].astype(F32)
        for p in range(2, N_DEV):
            acc = acc + land_ref[me ^ p].astype(F32)
        own.wait()
        o_ref[...] = acc + own_ref[...].astype(F32)

    return pl.pallas_call(
        body, name=name, in_specs=[_VMEM, _ANY], out_specs=_VMEM,
        out_shape=jax.ShapeDtypeStruct((cnt, cols), F32),
        scratch_shapes=[pltpu.VMEM((cnt, cols), landed.dtype), pltpu.SemaphoreType.DMA],
        compiler_params=_params(),
    )(landed, full)


def sum_slots(slots, name):
    _, rows, cols = slots.shape
    tr = rows
    if rows > 512:
        for cand in (256, 128, 64, 32, 16, 8):
            if rows % cand == 0:
                tr = cand
                break

    def body(s_ref, o_ref):
        acc = s_ref[0].astype(F32)
        for j in range(1, N_DEV):
            acc = acc + s_ref[j].astype(F32)
        o_ref[...] = acc

    return pl.pallas_call(
        body, name=name, grid=(rows // tr,),
        in_specs=[pl.BlockSpec((N_DEV, tr, cols), lambda i: (0, i, 0))], out_specs=_row_spec(tr, cols),
        out_shape=jax.ShapeDtypeStruct((rows, cols), F32), compiler_params=_params(("arbitrary",)),
    )(slots)


BIG_T = ("ffn1_w_gate", "ffn1_w_up", "w_in", "ffn2_w_gate", "ffn2_w_up")
BIG_N = ("ffn1_w_down", "w_out", "ffn2_w_down")
HALF_T = ("w_attn_proj", "w_glu_v", "w_glu_g")
SMALL = ("ffn1_norm", "mix_norm", "attn_sinks", "ssm_a_re", "ssm_a_im", "ssm_log_dt", "ssm_b_re", "ssm_b_im",
         "ssm_c_re", "ssm_c_im", "ssm_d", "ffn2_norm", "final_norm")
PARTS = {"ffn1": ("ffn1_w_gate", "ffn1_w_up", "ffn1_w_down"),
         "mix": ("w_in", "w_out", "w_attn_proj", "w_glu_v", "w_glu_g"),
         "ffn2": ("ffn2_w_gate", "ffn2_w_up", "ffn2_w_down")}


def _to_rows(name, a):
    return a if name in BIG_N else jnp.swapaxes(a, -1, -2)


def local_step(x, tgt, get_weights, put_grads, small):
    seq, d = x.shape
    t = PAD_FRONT + N_META + seq
    cos_t, sin_t = rope_tables(t)
    row = lambda a: a.reshape(1, -1)
    tables = []
    for i in range(DEPTH):
        b_re_t = jnp.swapaxes(small["ssm_b_re"][i], 1, 2)
        b_im_t = jnp.swapaxes(small["ssm_b_im"][i], 1, 2)
        lam_re, lam_im, bbar_re, bbar_im = ssm_prep(small["ssm_a_re"][i], small["ssm_a_im"][i],
                                                    small["ssm_log_dt"][i].reshape(-1, 1), b_re_t, b_im_t, f"ssm_prep_{i}")
        tables.append(((b_re_t, b_im_t),
                       (row(lam_re), row(lam_im), _block_diag_b(bbar_re).astype(BF16), _block_diag_b(bbar_im).astype(BF16),
                        _block_diag_c(small["ssm_c_re"][i]).astype(BF16), _block_diag_c(small["ssm_c_im"][i]).astype(BF16),
                        row(small["ssm_d"][i]))))
    early = [cos_t, sin_t] + [a for _, tab in tables for a in tab[2:6]]
    saved = []
    h = None
    for i in range(DEPTH):
        s = {}
        w = dict(get_weights(i, "ffn1", early if i == 0 else h))
        if i == 0:
            h = jnp.concatenate([jnp.zeros((PAD_FRONT, d), F32), w["meta_tokens"], x], axis=0)
        s["h0"] = h
        h, s["n1"], s["a1"], s["b1"] = ffn_fwd(h, row(small["ffn1_norm"][i]), w["ffn1_w_gate"], w["ffn1_w_up"],
                                               w["ffn1_w_down"], f"ffn1_fwd_{i}")
        s["h1"] = h
        w.update(get_weights(i, "mix", h))
        s["n2"], s["qkv"], s["u"], s["gates"] = win_fwd(h, row(small["mix_norm"][i]), w["w_in"], cos_t, sin_t,
                                                        f"win_fwd_{i}")
        s["b_t"], s["ssm"] = tables[i]
        s["yg"], s["h_re"], s["h_im"] = ssm_fwd(s["u"], *s["ssm"], f"ssm_fwd_{i}")
        s["o"] = attn_fwd(s["qkv"], row(small["attn_sinks"][i]), f"attn_fwd_{i}")
        h, s["merged"], s["att"], s["sv"], s["sg"] = merge_fwd(
            h, s["o"], s["yg"], s["gates"], w["w_attn_proj"], w["w_glu_v"], w["w_glu_g"], w["w_out"],
            f"merge_fwd_{i}")
        s["h2"] = h
        w.update(get_weights(i, "ffn2", h))
        h, s["n3"], s["a3"], s["b3"] = ffn_fwd(h, row(small["ffn2_norm"][i]), w["ffn2_w_gate"], w["ffn2_w_up"],
                                               w["ffn2_w_down"], f"ffn2_fwd_{i}")
        s["w"] = w
        saved.append(s)

    loss, dh, d_final = head_fwd_bwd(h, row(small["final_norm"]), tgt)
    gs = {k: [None] * DEPTH for k in SMALL if k != "final_norm"}
    dep = loss
    for i in reversed(range(DEPTH)):
        s = saved[i]
        w = s["w"]
        dh, da, db, sact, dhb, dg = ffn_bwd(dh, s["h2"], row(small["ffn2_norm"][i]), s["a3"], s["b3"], w["ffn2_w_gate"],
                                            w["ffn2_w_up"], w["ffn2_w_down"], dep, f"ffn2_bwd_{i}")
        gs["ffn2_norm"][i] = dg[0]
        dep = put_grads(i, "ffn2", {"ffn2_w_gate": tn_matmul(da, s["n3"], f"ffn2_dwg_{i}"),
                                    "ffn2_w_up": tn_matmul(db, s["n3"], f"ffn2_dwu_{i}"),
                                    "ffn2_w_down": tn_matmul(sact, dhb, f"ffn2_dwd_{i}")})

        dgates, datt, dsv, dsg, do, dyg, dhb = merge_bwd(dh, s["gates"], s["att"], s["sv"], s["sg"], w["w_attn_proj"],
                                                         w["w_glu_v"], w["w_glu_g"], w["w_out"], dep, f"merge_bwd_{i}")
        gmix = {"w_out": tn_matmul(s["merged"], dhb, f"dwout_{i}"),
                "w_attn_proj": tn_matmul(datt, s["o"], f"dwap_{i}"),
                "w_glu_v": tn_matmul(dsv, s["yg"], f"dwv_{i}"),
                "w_glu_g": tn_matmul(dsg, s["yg"], f"dwgg_{i}")}
        dqkv, dsink = attn_bwd(s["qkv"], do, row(small["attn_sinks"][i]), cos_t, sin_t, f"attn_bwd_{i}")
        gs["attn_sinks"][i] = dsink[:, 0]
        du, dl_re, dl_im, dbb_re, dbb_im, dcc_re, dcc_im, dd = ssm_bwd(dyg, s["u"], s["h_re"], s["h_im"], *s["ssm"],
                                                                      f"ssm_bwd_{i}")
        fold = lambda a: jnp.sum(a, axis=0).reshape(SSM_GROUPS, SSM_STATE)
        da_re, da_im, dldt, db_re_t, db_im_t = ssm_prep_bwd(
            small["ssm_a_re"][i], small["ssm_a_im"][i], small["ssm_log_dt"][i].reshape(-1, 1), *s["b_t"],
            fold(dl_re), fold(dl_im), _diag_of_b(dbb_re), _diag_of_b(dbb_im), f"ssm_prep_bwd_{i}")
        gs["ssm_a_re"][i], gs["ssm_a_im"][i], gs["ssm_log_dt"][i] = da_re, da_im, dldt[:, 0]
        gs["ssm_b_re"][i], gs["ssm_b_im"][i] = jnp.swapaxes(db_re_t, 1, 2), jnp.swapaxes(db_im_t, 1, 2)
        gs["ssm_c_re"][i], gs["ssm_c_im"][i] = _diag_of_c(dcc_re), _diag_of_c(dcc_im)
        gs["ssm_d"][i] = dd[0]
        gmix["w_in"] = tn_matmul([dqkv, du, dgates], s["n2"], f"dwin_{i}")
        dep = put_grads(i, "mix", gmix)
        dh, dg = win_bwd(dh, s["h1"], row(small["mix_norm"][i]), dqkv, du, dgates, w["w_in"], dep, f"win_bwd_{i}")
        gs["mix_norm"][i] = dg[0]

        dh, da, db, sact, dhb, dg = ffn_bwd(dh, s["h0"], row(small["ffn1_norm"][i]), s["a1"], s["b1"], w["ffn1_w_gate"],
                                            w["ffn1_w_up"], w["ffn1_w_down"], dep, f"ffn1_bwd_{i}")
        gs["ffn1_norm"][i] = dg[0]
        if i > 0:
            dep = put_grads(i, "ffn1", {"ffn1_w_gate": tn_matmul(da, s["n1"], f"ffn1_dwg_{i}"),
                                        "ffn1_w_up": tn_matmul(db, s["n1"], f"ffn1_dwu_{i}"),
                                        "ffn1_w_down": tn_matmul(sact, dhb, f"ffn1_dwd_{i}")})
        else:
            for k, xa, ya in (("ffn1_w_down", sact, dhb), ("ffn1_w_gate", da, s["n1"]), ("ffn1_w_up", db, s["n1"])):
                dep = put_grads(i, "ffn1", {k: tn_matmul(xa, ya, f"d_{k}_{i}", dep)})

    gs = {k: jnp.stack(v) for k, v in gs.items()}
    gs["final_norm"] = d_final[0]
    return loss[0, 0], dh[PAD_FRONT + N_META:], dh[PAD_FRONT:PAD_FRONT + N_META], gs, dep


def _pack_rows(arrays, cols):
    flat = jnp.concatenate([a.reshape(-1) for a in arrays])
    rows = -(-flat.shape[0] // cols)
    rows = -(-rows // 16) * 16
    return jnp.pad(flat, (0, rows * cols - flat.shape[0])).reshape(rows, cols)


def _unpack_rows(packed, shapes):
    flat = packed.reshape(-1)
    out, off = [], 0
    for shp in shapes:
        n = math.prod(shp)
        out.append(flat[off:off + n].reshape(shp))
        off += n
    return out


def kernel(x, meta_tokens, ffn1_norm, ffn1_w_gate, ffn1_w_up, ffn1_w_down, mix_norm, w_in, attn_sinks, ssm_a_re, ssm_a_im, ssm_log_dt, ssm_b_re, ssm_b_im, ssm_c_re, ssm_c_im, ssm_d, w_attn_proj, w_glu_v, w_glu_g, w_out, ffn2_norm, ffn2_w_gate, ffn2_w_up, ffn2_w_down, final_norm, loss_target, m_meta_tokens, m_ffn1_norm, m_ffn1_w_gate, m_ffn1_w_up, m_ffn1_w_down, m_mix_norm, m_w_in, m_attn_sinks, m_ssm_a_re, m_ssm_a_im, m_ssm_log_dt, m_ssm_b_re, m_ssm_b_im, m_ssm_c_re, m_ssm_c_im, m_ssm_d, m_w_attn_proj, m_w_glu_v, m_w_glu_g, m_w_out, m_ffn2_norm, m_ffn2_w_gate, m_ffn2_w_up, m_ffn2_w_down, m_final_norm, v_meta_tokens, v_ffn1_norm, v_ffn1_w_gate, v_ffn1_w_up, v_ffn1_w_down, v_mix_norm, v_w_in, v_attn_sinks, v_ssm_a_re, v_ssm_a_im, v_ssm_log_dt, v_ssm_b_re, v_ssm_b_im, v_ssm_c_re, v_ssm_c_im, v_ssm_d, v_w_attn_proj, v_w_glu_v, v_w_glu_g, v_w_out, v_ffn2_norm, v_ffn2_w_gate, v_ffn2_w_up, v_ffn2_w_down, v_final_norm):
    names = ("meta_tokens", "ffn1_norm", "ffn1_w_gate", "ffn1_w_up", "ffn1_w_down", "mix_norm", "w_in", "attn_sinks",
             "ssm_a_re", "ssm_a_im", "ssm_log_dt", "ssm_b_re", "ssm_b_im", "ssm_c_re", "ssm_c_im", "ssm_d",
             "w_attn_proj", "w_glu_v", "w_glu_g", "w_out", "ffn2_norm", "ffn2_w_gate", "ffn2_w_up", "ffn2_w_down",
             "final_norm")
    weights = dict(zip(names, (meta_tokens, ffn1_norm, ffn1_w_gate, ffn1_w_up, ffn1_w_down, mix_norm, w_in, attn_sinks, ssm_a_re, ssm_a_im, ssm_log_dt, ssm_b_re, ssm_b_im, ssm_c_re, ssm_c_im, ssm_d, w_attn_proj, w_glu_v, w_glu_g, w_out, ffn2_norm, ffn2_w_gate, ffn2_w_up, ffn2_w_down, final_norm)))
    moments_m = dict(zip(names, (m_meta_tokens, m_ffn1_norm, m_ffn1_w_gate, m_ffn1_w_up, m_ffn1_w_down, m_mix_norm, m_w_in, m_attn_sinks, m_ssm_a_re, m_ssm_a_im, m_ssm_log_dt, m_ssm_b_re, m_ssm_b_im, m_ssm_c_re, m_ssm_c_im, m_ssm_d, m_w_attn_proj, m_w_glu_v, m_w_glu_g, m_w_out, m_ffn2_norm, m_ffn2_w_gate, m_ffn2_w_up, m_ffn2_w_down, m_final_norm)))
    moments_v = dict(zip(names, (v_meta_tokens, v_ffn1_norm, v_ffn1_w_gate, v_ffn1_w_up, v_ffn1_w_down, v_mix_norm, v_w_in, v_attn_sinks, v_ssm_a_re, v_ssm_a_im, v_ssm_log_dt, v_ssm_b_re, v_ssm_b_im, v_ssm_c_re, v_ssm_c_im, v_ssm_d, v_w_attn_proj, v_w_glu_v, v_w_glu_g, v_w_out, v_ffn2_norm, v_ffn2_w_gate, v_ffn2_w_up, v_ffn2_w_down, v_final_norm)))
    me = _my_index()

    order = [(i, part) for i in range(DEPTH) for part in PARTS]
    gathers = {}
    token = jnp.zeros((8, 128), F32)
    for i, part in order:
        shards = [_to_rows(k, weights[k][i]).astype(BF16) for k in PARTS[part]]
        if (i, part) == order[0]:
            shards.append(meta_tokens)
        ex = Exchange(shards, False, f"gather_{part}_{i}", relay=True)
        state, token = ex.start(shards, token)
        gathers[i, part] = [ex, state, False]
    all_started = token

    def relay(group, after):
        ex, state, relayed = gathers[group]
        if relayed:
            return []
        new_state, relay_token = ex.relay(state, after)
        gathers[group][1:] = [new_state, True]
        return [relay_token]

    def get_weights(i, part, after):
        g = order.index((i, part))
        after = [all_started] + list(after) if g == 0 else [after]
        tokens = relay(order[g], after)
        if g >= 2 and g + 1 < len(order):
            tokens += relay(order[g + 1], after)
        ex, state, _ = gathers[i, part]
        shards, lands = ex.wait(state, after + tokens)
        fulls = ex.place(lands, shards)
        got = dict(zip(PARTS[part], fulls))
        if (i, part) == (0, "ffn1"):
            got["meta_tokens"] = jnp.swapaxes(fulls[-1].reshape(N_DEV, N_META, 128), 0, 1).reshape(N_META, D_MODEL)
        return got

    scatters = []

    def put_grads(i, part, gdict):
        ks = list(gdict)
        srcs = [gdict[k] for k in ks]
        ex = Exchange(srcs, True, f"scatter_{part if len(ks) > 1 else ks[0]}_{i}")
        state, tok = ex.start(srcs, all_started)
        scatters.append((i, ks, ex, state))
        return tok

    small = {k: weights[k] for k in SMALL}
    loss, dx, dmeta, gs, last_started = local_step(x[0], loss_target[0], get_weights, put_grads, small)

    grads, deltas, new_m, new_v = {}, {}, {}, {}
    small_list = [loss.reshape(1), dmeta] + [gs[k] for k in SMALL]
    packed = _pack_rows(small_list, D_MODEL)
    small_ex = Exchange([packed], False, "gather_small")
    small_state, after = small_ex.start([packed], last_started)

    updated = {}
    me_index = jnp.reshape(me, (1,)).astype(jnp.int32)
    for i, ks, ex, state in scatters:
        partials, lands = ex.wait(state, after)
        for k, partial, slots in zip(ks, partials, lands):
            updated[k] = sum_adamw_layer(me_index, slots, partial, _to_rows(k, weights[k]), _to_rows(k, moments_m[k]),
                                         _to_rows(k, moments_v[k]), i, updated.get(k), f"adamw_{k}_{i}")
            after = updated[k][0]
    for k, outs in updated.items():
        grads[k], deltas[k], new_m[k], new_v[k] = [_to_rows(k, a) for a in outs]

    packed_own, packed_all = small_ex.wait(small_state, after)
    (packed_all,) = small_ex.place(packed_all, packed_own)
    total = sum_slots(packed_all.reshape(N_DEV, packed.shape[0], D_MODEL), "sum_small")
    pieces = _unpack_rows(total, [a.shape for a in small_list])
    loss_out = pieces[0][0]
    grads["meta_tokens"] = lax.dynamic_slice_in_dim(pieces[1], me * 128, 128, axis=1)
    for k, p in zip(SMALL, pieces[2:]):
        grads[k] = p
    for k in ("meta_tokens",) + SMALL:
        deltas[k], new_m[k], new_v[k] = adamw(weights[k], grads[k], moments_m[k], moments_v[k], f"adamw_{k}",
                                              minor_swap=k in ("ssm_b_re", "ssm_b_im"))
    return (loss_out, dx[None], *[grads[k] for k in names], *[deltas[k] for k in names],
            *[new_m[k] for k in names], *[new_v[k] for k in names])
```

```python
import functools
import math

import jax
import jax.numpy as jnp
from jax import lax
from jax.experimental import pallas as pl
from jax.experimental.pallas import tpu as pltpu

F32 = jnp.float32
BF16 = jnp.bfloat16

D_MODEL = 1024
DEPTH = 2
N_META = 16
HEAD_DIM = 64
N_Q_HEADS = 8
ATTN_WIDTH = 512
KV_WIDTH = 128
QKV_WIDTH = ATTN_WIDTH + 2 * KV_WIDTH
WINDOW = 128
BLK = 128
ROPE_THETA = 500000.0
ROT_DIM = 16
SSM_WIDTH = 512
SSM_GROUP = 16
SSM_GROUPS = 32
SSM_STATE = 64
STATE_WIDTH = SSM_GROUPS * SSM_STATE
D_FF = 2816
IN_WIDTH = 3328
EPS = 1e-6
NEG_INF = -1e30
PAD_FRONT = (-N_META) % BLK
N_DEV = 8

ADAM_LR = 0.001
ADAM_B1 = 0.9
ADAM_B2 = 0.999
ADAM_EPS = 1e-08
ADAM_WD = 0.01
ADAM_STEP = 10

VMEM_LIMIT = 56 * 1024 * 1024
TOKEN_TILE = 384
_VMEM = pl.BlockSpec(memory_space=pltpu.VMEM)
_SMEM = pl.BlockSpec(memory_space=pltpu.SMEM)
_ANY = pl.BlockSpec(memory_space=pl.ANY)
MESH = pl.DeviceIdType.MESH


def _params(sem=None):
    return pltpu.CompilerParams(dimension_semantics=sem, vmem_limit_bytes=VMEM_LIMIT)


def _nt(a, b):
    return lax.dot_general(a, b, (((1,), (1,)), ((), ())), preferred_element_type=F32)


def _nn(a, b):
    return jnp.dot(a, b, preferred_element_type=F32)


def _tn(a, b):
    return lax.dot_general(a, b, (((0,), (0,)), ((), ())), preferred_element_type=F32)


def _row_spec(tm, width):
    return pl.BlockSpec((tm, width), lambda i: (i, 0))


def _acc_spec(shape):
    return pl.BlockSpec(shape, lambda i: (0,) * len(shape))


def _sigmoid(x):
    return 0.5 * jnp.tanh(0.5 * x) + 0.5


def _rms_stats(x):
    r = lax.rsqrt(jnp.mean(x * x, axis=-1, keepdims=True) + EPS)
    return x * r, r


def _rms_bwd(dn, xh, r, g):
    dg = jnp.sum(dn * xh, axis=0, keepdims=True)
    dxh = dn * g
    dx = r * (dxh - xh * jnp.mean(dxh * xh, axis=-1, keepdims=True))
    return dx, dg


def ffn_fwd(h, g, wg_t, wu_t, wd, name):
    t, d = h.shape
    f = wd.shape[0]
    tm = TOKEN_TILE

    def body(h_ref, g_ref, wg_ref, wu_ref, wd_ref, ho_ref, n_ref, sl_ref, p_ref, s_ref):
        x = h_ref[...]
        xh, _ = _rms_stats(x)
        n = (xh * g_ref[...]).astype(BF16)
        n_ref[...] = n
        a = _nt(n, wg_ref[...])
        b = _nt(n, wu_ref[...])
        sig = jax.nn.sigmoid(a)
        sl = a * sig
        sl_ref[...] = sl.astype(BF16)
        p_ref[...] = (b * (sig + sl * (1.0 - sig))).astype(BF16)
        s = (sl * b).astype(BF16)
        s_ref[...] = s
        ho_ref[...] = x + 0.5 * _nn(s, wd_ref[...])

    ho, n, sl, p, s = pl.pallas_call(
        body, name=name, grid=(t // tm,),
        in_specs=[_row_spec(tm, d), _acc_spec((1, d)), _VMEM, _VMEM, _VMEM],
        out_specs=[_row_spec(tm, d), _row_spec(tm, d), _row_spec(tm, f), _row_spec(tm, f), _row_spec(tm, f)],
        out_shape=[jax.ShapeDtypeStruct((t, d), F32), jax.ShapeDtypeStruct((t, d), BF16),
                   jax.ShapeDtypeStruct((t, f), BF16), jax.ShapeDtypeStruct((t, f), BF16),
                   jax.ShapeDtypeStruct((t, f), BF16)],
        compiler_params=_params(("arbitrary",)),
    )(h, g, wg_t, wu_t, wd)
    return ho, n, (sl, p, s)


def ffn_bwd(dh, h, g, acts, wg_t, wu_t, wd, dep, name):
    t, d = h.shape
    f = wd.shape[0]
    tm = TOKEN_TILE
    sl, p, s = acts

    def hidden_body(dh_ref, sl_ref, p_ref, wd_ref, dep_ref, da_ref, db_ref, dhb_ref):
        dhb = (0.5 * dh_ref[...]).astype(BF16)
        dhb_ref[...] = dhb
        ds = _nt(dhb, wd_ref[...])
        da_ref[...] = (ds * p_ref[...].astype(F32)).astype(BF16)
        db_ref[...] = (ds * sl_ref[...].astype(F32)).astype(BF16)

    da, db, dhb = pl.pallas_call(
        hidden_body, name=name + "_h", grid=(t // tm,),
        in_specs=[_row_spec(tm, d), _row_spec(tm, f), _row_spec(tm, f), _VMEM, _ANY],
        out_specs=[_row_spec(tm, f), _row_spec(tm, f), _row_spec(tm, d)],
        out_shape=[jax.ShapeDtypeStruct((t, f), BF16), jax.ShapeDtypeStruct((t, f), BF16),
                   jax.ShapeDtypeStruct((t, d), BF16)],
        compiler_params=_params(("arbitrary",)),
    )(dh, sl, p, wd, dep)

    def input_body(dh_ref, h_ref, g_ref, da_ref, db_ref, wg_ref, wu_ref, dhi_ref, dg_ref):
        dn = _nn(da_ref[...], wg_ref[...]) + _nn(db_ref[...], wu_ref[...])
        xh, r = _rms_stats(h_ref[...])
        dx, dg = _rms_bwd(dn, xh, r, g_ref[...])
        dhi_ref[...] = dh_ref[...] + dx

        @pl.when(pl.program_id(0) == 0)
        def _():
            dg_ref[...] = jnp.zeros_like(dg_ref)

        dg_ref[...] += dg

    dhi, dg = pl.pallas_call(
        input_body, name=name + "_x", grid=(t // tm,),
        in_specs=[_row_spec(tm, d), _row_spec(tm, d), _acc_spec((1, d)), _row_spec(tm, f), _row_spec(tm, f),
                  _VMEM, _VMEM],
        out_specs=[_row_spec(tm, d), _acc_spec((1, d))],
        out_shape=[jax.ShapeDtypeStruct((t, d), F32), jax.ShapeDtypeStruct((1, d), F32)],
        compiler_params=_params(("arbitrary",)),
    )(dh, h, g, da, db, wg_t, wu_t)
    return dhi, da, db, s, dhb, dg


DW_TILE = 256


def tn_matmul(x, y, name, dep=None):
    xs = list(x) if isinstance(x, (list, tuple)) else [x]
    t = xs[0].shape[0]
    n = y.shape[1]
    bm = DW_TILE
    tiles = [a.shape[1] // bm for a in xs]
    offs = [sum(tiles[:k]) for k in range(len(xs))]
    deps = [] if dep is None else [dep]

    def body(*refs):
        y_ref, o_ref = refs[len(xs)], refs[-1]
        i = pl.program_id(0)
        for k in range(len(xs)):
            @pl.when((i >= offs[k]) & (i < offs[k] + tiles[k]))
            def _(k=k):
                o_ref[...] = _tn(refs[k][...], y_ref[...]).astype(BF16)

    def x_spec(k):
        return pl.BlockSpec((t, bm), lambda i: (0, jnp.clip(i - offs[k], 0, tiles[k] - 1)))

    return pl.pallas_call(
        body, name=name, grid=(sum(tiles),),
        in_specs=[x_spec(k) for k in range(len(xs))] + [_VMEM] + [_ANY] * len(deps),
        out_specs=pl.BlockSpec((bm, n), lambda i: (i, 0)),
        out_shape=jax.ShapeDtypeStruct((sum(tiles) * bm, n), BF16),
        compiler_params=_params(("arbitrary",)),
    )(*xs, y, *deps)


def head_fwd_bwd(h, g, tgt):
    t, d = h.shape

    def body(h_ref, g_ref, t_ref, loss_ref, dh_ref, dg_ref):
        i = pl.program_id(0)
        xh, r = _rms_stats(h_ref[...])
        gv = g_ref[...]
        valid = (i > 0).astype(F32)
        e = (xh * gv - t_ref[...]) * valid
        dx, dg = _rms_bwd(e * (1.0 / d), xh, r, gv)
        dh_ref[...] = dx

        @pl.when(i == 0)
        def _():
            dg_ref[...] = jnp.zeros_like(dg_ref)
            loss_ref[...] = jnp.zeros_like(loss_ref)

        dg_ref[...] += dg
        loss_ref[...] += jnp.sum(e * e) * (0.5 / d)

    return pl.pallas_call(
        body, name="head", grid=(t // BLK,),
        in_specs=[_row_spec(BLK, d), _acc_spec((1, d)),
                  pl.BlockSpec((BLK, d), lambda i: (jnp.maximum(i - 1, 0), 0))],
        out_specs=[_acc_spec((1, 128)), _row_spec(BLK, d), _acc_spec((1, d))],
        out_shape=[jax.ShapeDtypeStruct((1, 128), F32), jax.ShapeDtypeStruct((t, d), F32),
                   jax.ShapeDtypeStruct((1, d), F32)],
        compiler_params=_params(("arbitrary",)),
    )(h, g, tgt)


def rope_tables(t):
    pos = jnp.arange(t, dtype=F32) - PAD_FRONT
    inv_freq = ROPE_THETA ** (-jnp.arange(0, ROT_DIM, 2, dtype=F32) / ROT_DIM)
    ang = pos[:, None] * inv_freq[None, :]
    cos, sin = jnp.cos(ang), jnp.sin(ang)
    ones = jnp.ones((t, HEAD_DIM - ROT_DIM), F32)
    cos_h = jnp.concatenate([cos, cos, ones], axis=1)
    sin_h = jnp.concatenate([-sin, sin, 0.0 * ones], axis=1)
    return jnp.concatenate([cos_h, cos_h], axis=1), jnp.concatenate([sin_h, sin_h], axis=1)


def _swap_halves(x):
    n = x.shape[1]
    lane = lax.broadcasted_iota(jnp.int32, x.shape, 1)
    return jnp.where(lane % HEAD_DIM < ROT_DIM // 2, pltpu.roll(x, n - ROT_DIM // 2, 1), pltpu.roll(x, ROT_DIM // 2, 1))


def _rope(x, cos_t, sin_t, sign):
    return x * cos_t + sign * (_swap_halves(x) * sin_t)


def win_fwd(h, g, win_t, cos_t, sin_t, name):
    t, d = h.shape
    tm = TOKEN_TILE

    def body(h_ref, g_ref, w_ref, c_ref, s_ref, n_ref, qkv_ref, u_ref, gates_ref):
        xh, _ = _rms_stats(h_ref[...])
        n = (xh * g_ref[...]).astype(BF16)
        n_ref[...] = n
        z = _nt(n, w_ref[...])
        c, s = c_ref[...], s_ref[...]
        for j in range((ATTN_WIDTH + KV_WIDTH) // 128):
            qkv_ref[:, j * 128:(j + 1) * 128] = _rope(z[:, j * 128:(j + 1) * 128], c, s, 1.0).astype(BF16)
        qkv_ref[:, ATTN_WIDTH + KV_WIDTH:QKV_WIDTH] = z[:, ATTN_WIDTH + KV_WIDTH:QKV_WIDTH].astype(BF16)
        u_ref[...] = z[:, QKV_WIDTH:QKV_WIDTH + SSM_WIDTH]
        gates_ref[...] = z[:, QKV_WIDTH + SSM_WIDTH:].astype(BF16)

    return pl.pallas_call(
        body, name=name, grid=(t // tm,),
        in_specs=[_row_spec(tm, d), _acc_spec((1, d)), _VMEM, _row_spec(tm, 128), _row_spec(tm, 128)],
        out_specs=[_row_spec(tm, d), _row_spec(tm, QKV_WIDTH), _row_spec(tm, SSM_WIDTH), _row_spec(tm, 2 * d)],
        out_shape=[jax.ShapeDtypeStruct((t, d), BF16), jax.ShapeDtypeStruct((t, QKV_WIDTH), BF16),
                   jax.ShapeDtypeStruct((t, SSM_WIDTH), F32), jax.ShapeDtypeStruct((t, 2 * d), BF16)],
        compiler_params=_params(("arbitrary",)),
    )(h, g, win_t, cos_t, sin_t)


def win_bwd(dh, h, g, dqkv, du, dgates, win_t, dep, name):
    t, d = h.shape
    tm = TOKEN_TILE

    def body(dh_ref, h_ref, g_ref, dqkv_ref, du_ref, dgt_ref, w_ref, dep_ref, dhi_ref, dg_ref):
        dn = (_nn(dqkv_ref[...], w_ref[0:QKV_WIDTH, :])
              + _nn(du_ref[...], w_ref[QKV_WIDTH:QKV_WIDTH + SSM_WIDTH, :])
              + _nn(dgt_ref[...], w_ref[QKV_WIDTH + SSM_WIDTH:, :]))
        xh, r = _rms_stats(h_ref[...])
        dx, dg = _rms_bwd(dn, xh, r, g_ref[...])
        dhi_ref[...] = dh_ref[...] + dx

        @pl.when(pl.program_id(0) == 0)
        def _():
            dg_ref[...] = jnp.zeros_like(dg_ref)

        dg_ref[...] += dg

    return pl.pallas_call(
        body, name=name, grid=(t // tm,),
        in_specs=[_row_spec(tm, d), _row_spec(tm, d), _acc_spec((1, d)), _row_spec(tm, QKV_WIDTH),
                  _row_spec(tm, SSM_WIDTH), _row_spec(tm, 2 * d), _VMEM, _ANY],
        out_specs=[_row_spec(tm, d), _acc_spec((1, d))],
        out_shape=[jax.ShapeDtypeStruct((t, d), F32), jax.ShapeDtypeStruct((1, d), F32)],
        compiler_params=_params(("arbitrary",)),
    )(dh, h, g, dqkv, du, dgates, win_t, dep)


def _attn_mask(blk):
    q_pos = blk * BLK + lax.broadcasted_iota(jnp.int32, (BLK, 3 * BLK), 0) - PAD_FRONT
    col = lax.broadcasted_iota(jnp.int32, (BLK, 3 * BLK), 1)
    part = col // BLK
    k_pos = jnp.where(part == 0, col, (blk + part - 2) * BLK + (col - part * BLK)) - PAD_FRONT
    dist = q_pos - k_pos
    meta_ok = (part == 0) & (k_pos >= 0) & (dist >= 0)
    band_ok = (part > 0) & (k_pos >= N_META) & (dist >= 0) & (dist < WINDOW)
    return meta_ok | band_ok


def _head_halves(x128, kv):
    x = x128.astype(F32)
    lane = lax.broadcasted_iota(jnp.int32, x.shape, 1)
    swapped = pltpu.roll(x, HEAD_DIM, 1)
    lo, hi = (x, swapped) if kv == 0 else (swapped, x)
    return jnp.where(lane < HEAD_DIM, lo, 0.0).astype(BF16), jnp.where(lane >= HEAD_DIM, hi, 0.0).astype(BF16)


def _gather_keys(meta_ref, prev_ref, cur_ref, lo):
    return jnp.concatenate([meta_ref[:, lo:lo + 128], prev_ref[:, lo:lo + 128], cur_ref[:, lo:lo + 128]], axis=0)


def _pair_lanes(kv):
    return slice(2 * kv * 128, (2 * kv + 1) * 128), slice((2 * kv + 1) * 128, (2 * kv + 2) * 128)


def _stacked_sinks(sink_ref, head):
    row = lax.broadcasted_iota(jnp.int32, (2 * BLK, 1), 0)
    return jnp.where(row < BLK, sink_ref[0, head], sink_ref[0, head + 2])


def _softmax_with_sink(s, mask, sink):
    s = jnp.where(mask, s * (HEAD_DIM ** -0.5), NEG_INF)
    m = jnp.maximum(jnp.max(s, axis=-1, keepdims=True), sink)
    p = jnp.exp(s - m)
    p_sink = jnp.exp(sink - m)
    inv = 1.0 / (jnp.sum(p, axis=-1, keepdims=True) + p_sink)
    return p * inv, p_sink * inv


def attn_fwd(qkv, sinks, name):
    t = qkv.shape[0]
    nb = t // BLK

    def body(sink_ref, meta_ref, prev_ref, cur_ref, o_ref):
        blk = pl.program_id(0)
        mask = _attn_mask(blk)
        mask2 = jnp.concatenate([mask, mask], axis=0)
        k128 = _gather_keys(meta_ref, prev_ref, cur_ref, ATTN_WIDTH)
        v128 = _gather_keys(meta_ref, prev_ref, cur_ref, ATTN_WIDTH + KV_WIDTH)
        for kv in range(2):
            k_lo, k_hi = _head_halves(k128, kv)
            v_lo, v_hi = _head_halves(v128, kv)
            lanes0, lanes1 = _pair_lanes(kv)
            q2 = jnp.concatenate([cur_ref[:, lanes0], cur_ref[:, lanes1]], axis=0)
            p_a, _ = _softmax_with_sink(_nt(q2, k_lo), mask2, _stacked_sinks(sink_ref, 4 * kv))
            p_b, _ = _softmax_with_sink(_nt(q2, k_hi), mask2, _stacked_sinks(sink_ref, 4 * kv + 1))
            o2 = (_nn(p_a.astype(BF16), v_lo) + _nn(p_b.astype(BF16), v_hi)).astype(BF16)
            o_ref[:, lanes0] = o2[0:BLK]
            o_ref[:, lanes1] = o2[BLK:2 * BLK]

    blk_spec = lambda f: pl.BlockSpec((BLK, QKV_WIDTH), f)
    return pl.pallas_call(
        body, name=name, grid=(nb,),
        in_specs=[_SMEM, blk_spec(lambda i: (0, 0)), blk_spec(lambda i: (jnp.maximum(i - 1, 0), 0)),
                  blk_spec(lambda i: (i, 0))],
        out_specs=_row_spec(BLK, ATTN_WIDTH),
        out_shape=jax.ShapeDtypeStruct((t, ATTN_WIDTH), BF16),
        compiler_params=_params(("arbitrary",)),
    )(sinks, qkv, qkv, qkv)


def attn_bwd(qkv, do, sinks, cos_t, sin_t, name):
    t = qkv.shape[0]
    nb = t // BLK

    def body(sink_ref, meta_ref, prev_ref, cur_ref, do_ref, c_ref, s_ref, dqkv_ref, dsink_ref, carry_ref, macc_ref):
        step = pl.program_id(0)
        blk = nb - 1 - step

        @pl.when(step == 0)
        def _():
            dsink_ref[...] = jnp.zeros_like(dsink_ref)
            carry_ref[...] = jnp.zeros_like(carry_ref)
            macc_ref[...] = jnp.zeros_like(macc_ref)

        mask = _attn_mask(blk)
        mask2 = jnp.concatenate([mask, mask], axis=0)
        lane = lax.broadcasted_iota(jnp.int32, (3 * BLK, 128), 1)
        k128 = _gather_keys(meta_ref, prev_ref, cur_ref, ATTN_WIDTH)
        v128 = _gather_keys(meta_ref, prev_ref, cur_ref, ATTN_WIDTH + KV_WIDTH)
        cos_b, sin_b = c_ref[...], s_ref[...]
        dk_heads, dv_heads = [], []
        for kv in range(2):
            k_lo, k_hi = _head_halves(k128, kv)
            v_lo, v_hi = _head_halves(v128, kv)
            lanes0, lanes1 = _pair_lanes(kv)
            q2 = jnp.concatenate([cur_ref[:, lanes0], cur_ref[:, lanes1]], axis=0)
            do2 = jnp.concatenate([do_ref[:, lanes0], do_ref[:, lanes1]], axis=0)
            ds_half, p_half = [], []
            for half, (k_h, v_h) in enumerate(((k_lo, v_lo), (k_hi, v_hi))):
                head = 4 * kv + half
                p, p_sink = _softmax_with_sink(_nt(q2, k_h), mask2, _stacked_sinks(sink_ref, head))
                dp = _nt(do2, v_h)
                dsum = jnp.sum(p * dp, axis=-1, keepdims=True)
                ds_half.append((p * (dp - dsum) * (HEAD_DIM ** -0.5)).astype(BF16))
                p_half.append(p.astype(BF16))
                dsink = p_sink * dsum
                for part, h in ((0, head), (1, head + 2)):
                    total = -jnp.sum(dsink[part * BLK:(part + 1) * BLK], axis=0, keepdims=True)
                    dsink_ref[h:h + 1, :] += jnp.broadcast_to(total, (1, 128))
            dq2 = _nn(ds_half[0], k_lo) + _nn(ds_half[1], k_hi)
            dqkv_ref[:, lanes0] = _rope(dq2[0:BLK], cos_b, sin_b, -1.0).astype(BF16)
            dqkv_ref[:, lanes1] = _rope(dq2[BLK:2 * BLK], cos_b, sin_b, -1.0).astype(BF16)
            dk_acc = jnp.where(lane < HEAD_DIM, _tn(ds_half[0], q2), _tn(ds_half[1], q2))
            dv_acc = jnp.where(lane < HEAD_DIM, _tn(p_half[0], do2), _tn(p_half[1], do2))
            dk_heads.append(dk_acc + pltpu.roll(dk_acc, HEAD_DIM, 1))
            dv_heads.append(dv_acc + pltpu.roll(dv_acc, HEAD_DIM, 1))
        dkv = jnp.concatenate([jnp.where(lane < HEAD_DIM, dk_heads[0], dk_heads[1]),
                               jnp.where(lane < HEAD_DIM, dv_heads[0], dv_heads[1])], axis=1)
        macc_ref[...] += dkv[0:BLK]
        is_last = (blk == 0).astype(F32)
        mine = dkv[2 * BLK:3 * BLK] + carry_ref[...] + is_last * macc_ref[...]
        carry_ref[...] = dkv[BLK:2 * BLK]
        dqkv_ref[:, ATTN_WIDTH:ATTN_WIDTH + KV_WIDTH] = _rope(mine[:, 0:128], cos_b, sin_b, -1.0).astype(BF16)
        dqkv_ref[:, ATTN_WIDTH + KV_WIDTH:QKV_WIDTH] = mine[:, 128:256].astype(BF16)

    rev = lambda i: nb - 1 - i
    blk_spec = lambda f: pl.BlockSpec((BLK, QKV_WIDTH), f)
    return pl.pallas_call(
        body, name=name, grid=(nb,),
        in_specs=[_SMEM, blk_spec(lambda i: (0, 0)), blk_spec(lambda i: (jnp.maximum(rev(i) - 1, 0), 0)),
                  blk_spec(lambda i: (rev(i), 0)), pl.BlockSpec((BLK, ATTN_WIDTH), lambda i: (rev(i), 0)),
                  pl.BlockSpec((BLK, 128), lambda i: (rev(i), 0)), pl.BlockSpec((BLK, 128), lambda i: (rev(i), 0))],
        out_specs=[pl.BlockSpec((BLK, QKV_WIDTH), lambda i: (rev(i), 0)), _acc_spec((N_Q_HEADS, 128))],
        out_shape=[jax.ShapeDtypeStruct((t, QKV_WIDTH), BF16), jax.ShapeDtypeStruct((N_Q_HEADS, 128), F32)],
        scratch_shapes=[pltpu.VMEM((BLK, 256), F32), pltpu.VMEM((BLK, 256), F32)],
        compiler_params=_params(("arbitrary",)),
    )(sinks, qkv, qkv, qkv, do, cos_t, sin_t)


def _cmul(ar, ai, br, bi):
    return ar * br - ai * bi, ar * bi + ai * br


def ssm_prep(a_re, a_im, log_dt, b_re_t, b_im_t, name):
    def body(ar_ref, ai_ref, ldt_ref, br_ref, bi_ref, lr_ref, li_ref, bbr_ref, bbi_ref):
        ar, ai = ar_ref[...], ai_ref[...]
        dt = jnp.exp(ldt_ref[...])
        mag = jnp.exp(ar * dt)
        lr = mag * jnp.cos(ai * dt)
        li = mag * jnp.sin(ai * dt)
        den = ar * ar + ai * ai
        nr = lr - 1.0
        cr = ((nr * ar + li * ai) / den)[:, None, :]
        ci = ((li * ar - nr * ai) / den)[:, None, :]
        br, bi = br_ref[...], bi_ref[...]
        lr_ref[...] = lr
        li_ref[...] = li
        bbr_ref[...] = cr * br - ci * bi
        bbi_ref[...] = cr * bi + ci * br

    gp = jax.ShapeDtypeStruct(a_re.shape, F32)
    gcp = jax.ShapeDtypeStruct(b_re_t.shape, F32)
    return pl.pallas_call(body, name=name, out_shape=[gp, gp, gcp, gcp],
                          in_specs=[_VMEM] * 5, out_specs=[_VMEM] * 4)(a_re, a_im, log_dt, b_re_t, b_im_t)


def ssm_prep_bwd(a_re, a_im, log_dt, b_re_t, b_im_t, dl_re, dl_im, dbb_re, dbb_im, name):
    def body(ar_ref, ai_ref, ldt_ref, br_ref, bi_ref, dlr_ref, dli_ref, dbbr_ref, dbbi_ref,
             dar_ref, dai_ref, dldt_ref, dbr_ref, dbi_ref):
        ar, ai = ar_ref[...], ai_ref[...]
        dt = jnp.exp(ldt_ref[...])
        mag = jnp.exp(ar * dt)
        lr = mag * jnp.cos(ai * dt)
        li = mag * jnp.sin(ai * dt)
        den = ar * ar + ai * ai
        nr = lr - 1.0
        cr = (nr * ar + li * ai) / den
        ci = (li * ar - nr * ai) / den
        br, bi = br_ref[...], bi_ref[...]
        dbbr, dbbi = dbbr_ref[...], dbbi_ref[...]
        dbr_ref[...] = cr[:, None, :] * dbbr + ci[:, None, :] * dbbi
        dbi_ref[...] = cr[:, None, :] * dbbi - ci[:, None, :] * dbbr
        dcr = jnp.sum(br * dbbr + bi * dbbi, axis=1)
        dci = jnp.sum(br * dbbi - bi * dbbr, axis=1)
        d_num_r = dcr / den
        d_num_i = dci / den
        d_den = -(dcr * cr + dci * ci) / den
        d_lr = dlr_ref[...] + d_num_r * ar - d_num_i * ai
        d_li = dli_ref[...] + d_num_r * ai + d_num_i * ar
        d_ar = d_num_r * nr + d_num_i * li + d_den * 2.0 * ar
        d_ai = d_num_r * li - d_num_i * nr + d_den * 2.0 * ai
        d_mag = (d_lr * lr + d_li * li) / mag
        d_theta = d_li * lr - d_lr * li
        d_ardt = d_mag * mag
        dar_ref[...] = d_ar + d_ardt * dt
        dai_ref[...] = d_ai + d_theta * dt
        d_dt = jnp.sum(d_ardt * ar + d_theta * ai, axis=1, keepdims=True)
        dldt_ref[...] = d_dt * dt

    gp = jax.ShapeDtypeStruct(a_re.shape, F32)
    gcp = jax.ShapeDtypeStruct(b_re_t.shape, F32)
    return pl.pallas_call(body, name=name, out_shape=[gp, gp, jax.ShapeDtypeStruct(log_dt.shape, F32), gcp, gcp],
                          in_specs=[_VMEM] * 9, out_specs=[_VMEM] * 5,
                          )(a_re, a_im, log_dt, b_re_t, b_im_t, dl_re, dl_im, dbb_re, dbb_im)


N_CHUNK = 4
U_CHUNK = SSM_WIDTH // N_CHUNK
H_CHUNK = STATE_WIDTH // N_CHUNK
SUB = 8


def _block_diag_b(bb):
    x = bb.reshape(N_CHUNK, 8, SSM_GROUP, 1, SSM_STATE)
    same = (jnp.arange(8)[:, None] == jnp.arange(8)[None, :])[None, :, None, :, None]
    return jnp.where(same, x, 0.0).reshape(N_CHUNK, U_CHUNK, H_CHUNK)


def _block_diag_c(c):
    x = jnp.swapaxes(c.reshape(N_CHUNK, 8, SSM_GROUP, SSM_STATE), 2, 3)[:, :, :, None, :]
    same = (jnp.arange(8)[:, None] == jnp.arange(8)[None, :])[None, :, None, :, None]
    return jnp.where(same, x, 0.0).reshape(N_CHUNK, H_CHUNK, U_CHUNK)


def _diag_of_b(m):
    x = m.reshape(N_CHUNK, 8, SSM_GROUP, 8, SSM_STATE)
    return jnp.stack([x[:, g, :, g, :] for g in range(8)], axis=1).reshape(SSM_GROUPS, SSM_GROUP, SSM_STATE)


def _diag_of_c(m):
    x = m.reshape(N_CHUNK, 8, SSM_STATE, 8, SSM_GROUP)
    d = jnp.stack([x[:, g, :, g, :] for g in range(8)], axis=1)
    return jnp.swapaxes(d, 2, 3).reshape(SSM_GROUPS, SSM_GROUP, SSM_STATE)


def _lambda_tables(lr, li, reverse):
    p1 = (lr, li)
    p2 = _cmul(*p1, *p1)
    p4 = _cmul(*p2, *p2)
    rows = [p1]
    for _ in range(SUB - 1):
        rows.append(_cmul(*rows[-1], *p1))
    if reverse:
        rows = rows[::-1]
    return p1, p2, p4, (jnp.concatenate([r[0] for r in rows], axis=0), jnp.concatenate([r[1] for r in rows], axis=0))


def _scan8(xr, xi, pows, table, cr, ci, reverse):
    row = lax.broadcasted_iota(jnp.int32, xr.shape, 0)
    for d, (pr, pi) in zip((1, 2, 4), pows):
        if reverse:
            sr, si = pltpu.roll(xr, SUB - d, 0), pltpu.roll(xi, SUB - d, 0)
            keep = row < SUB - d
        else:
            sr, si = pltpu.roll(xr, d, 0), pltpu.roll(xi, d, 0)
            keep = row >= d
        sr = jnp.where(keep, sr, 0.0)
        si = jnp.where(keep, si, 0.0)
        xr, xi = xr + pr * sr - pi * si, xi + pr * si + pi * sr
    tr, ti = table
    return xr + tr * cr - ti * ci, xi + tr * ci + ti * cr


def _gelu_and_grad(y):
    k0 = math.sqrt(2.0 / math.pi)
    inner = k0 * (y + 0.044715 * y * y * y)
    th = jnp.tanh(inner)
    g = 0.5 * y * (1.0 + th)
    dg = 0.5 * (1.0 + th) + 0.5 * y * (1.0 - th * th) * k0 * (1.0 + 3.0 * 0.044715 * y * y)
    return g, dg


SCAN_TILE = TOKEN_TILE
SEG = SCAN_TILE // SUB
SCAN_LANES = 512


def _perm_matrix(to_segments):
    a = lax.broadcasted_iota(jnp.int32, (SCAN_TILE, SCAN_TILE), 0)
    b = lax.broadcasted_iota(jnp.int32, (SCAN_TILE, SCAN_TILE), 1)
    rho, time = (a, b) if to_segments else (b, a)
    return (time == (rho % SUB) * SEG + rho // SUB).astype(BF16)


def _permute_f32(p, x):
    hi = x.astype(BF16)
    r1 = x - hi.astype(F32)
    mid = r1.astype(BF16)
    lo = (r1 - mid.astype(F32)).astype(BF16)
    return _nn(p, hi) + _nn(p, mid) + _nn(p, lo)


def _power_table(lr, li, pr_ref, pi_ref):
    cur = (lr, li)
    for r in range(SEG):
        pr_ref[r:r + 1, :] = cur[0]
        pi_ref[r:r + 1, :] = cur[1]
        cur = _cmul(*cur, lr, li)


def _segment_scan(xr_ref, xi_ref, lanes, lam, table_row, cr_ref, ci_ref, reverse, extra=None):
    lr, li = lam
    row = lax.broadcasted_iota(jnp.int32, (SUB, SCAN_LANES), 0)

    def rows_of(k):
        r = SEG - 1 - k if reverse else k
        return pl.ds(pl.multiple_of(r * SUB, SUB), SUB)

    def first(k, st):
        sr, si = st
        rows = rows_of(k)
        nr = lr * sr - li * si + xr_ref[rows, lanes]
        ni = lr * si + li * sr + xi_ref[rows, lanes]
        xr_ref[rows, lanes] = nr
        xi_ref[rows, lanes] = ni
        return nr, ni

    zero = jnp.zeros((SUB, SCAN_LANES), F32)
    er, ei = lax.fori_loop(0, SEG, first, (zero, zero))
    l16 = table_row(SEG - 1)
    q1, q2, q4, tab = _lambda_tables(l16[0], l16[1], reverse)
    c_r, c_i = cr_ref[:, lanes], ci_ref[:, lanes]
    gr, gi = _scan8(er, ei, (q1, q2, q4), tab, c_r, c_i, reverse)
    if reverse:
        cin_r = jnp.where(row == SUB - 1, c_r, pltpu.roll(gr, SUB - 1, 0))
        cin_i = jnp.where(row == SUB - 1, c_i, pltpu.roll(gi, SUB - 1, 0))
        cr_ref[:, lanes] = gr[0:1]
        ci_ref[:, lanes] = gi[0:1]
    else:
        cin_r = jnp.where(row == 0, c_r, pltpu.roll(gr, 1, 0))
        cin_i = jnp.where(row == 0, c_i, pltpu.roll(gi, 1, 0))
        cr_ref[:, lanes] = gr[SUB - 1:SUB]
        ci_ref[:, lanes] = gi[SUB - 1:SUB]

    def second(k, carry):
        rows = rows_of(k)
        tr, ti = table_row(k)
        ar = xr_ref[rows, lanes] + tr * cin_r - ti * cin_i
        ai = xi_ref[rows, lanes] + tr * cin_i + ti * cin_r
        xr_ref[rows, lanes] = ar
        xi_ref[rows, lanes] = ai
        if extra is None:
            return carry
        return extra(rows, carry, ar, ai)

    init = 0 if extra is None else (cin_r, cin_i, zero, zero)
    return lax.fori_loop(0, SEG, second, init)


def ssm_fwd(u, lam_re, lam_im, bb_re, bb_im, cc_re, cc_im, d_skip, name):
    t = u.shape[0]
    tt = SCAN_TILE

    def body(u_ref, lr_ref, li_ref, bbr_ref, bbi_ref, ccr_ref, cci_ref, d_ref, yg_ref, hr_ref, hi_ref,
             cr_ref, ci_ref, pr_ref, pi_ref, up_ref, y_ref):
        @pl.when(pl.program_id(0) == 0)
        def _():
            cr_ref[...] = jnp.zeros_like(cr_ref)
            ci_ref[...] = jnp.zeros_like(ci_ref)
            _power_table(lr_ref[...], li_ref[...], pr_ref, pi_ref)

        up_ref[...] = _permute_f32(_perm_matrix(True), u_ref[...])
        ub = up_ref[...].astype(BF16)
        for j in range(N_CHUNK):
            hs = slice(j * H_CHUNK, (j + 1) * H_CHUNK)
            us = slice(j * U_CHUNK, (j + 1) * U_CHUNK)
            hr_ref[:, hs] = _nn(ub[:, us], bbr_ref[j])
            hi_ref[:, hs] = _nn(ub[:, us], bbi_ref[j])
        for c in range(STATE_WIDTH // SCAN_LANES):
            lanes = slice(c * SCAN_LANES, (c + 1) * SCAN_LANES)
            _segment_scan(hr_ref, hi_ref, lanes, (lr_ref[:, lanes], li_ref[:, lanes]),
                          lambda k, lanes=lanes: (pr_ref[pl.ds(k, 1), lanes], pi_ref[pl.ds(k, 1), lanes]),
                          cr_ref, ci_ref, False)
        for j in range(N_CHUNK):
            hs = slice(j * H_CHUNK, (j + 1) * H_CHUNK)
            us = slice(j * U_CHUNK, (j + 1) * U_CHUNK)
            y = (_nn(hr_ref[:, hs].astype(BF16), ccr_ref[j]) - _nn(hi_ref[:, hs].astype(BF16), cci_ref[j])
                 + d_ref[:, us] * up_ref[:, us])
            y_ref[:, us] = _gelu_and_grad(y)[0]
        yg_ref[...] = _nn(_perm_matrix(False), y_ref[...].astype(BF16)).astype(BF16)

    return pl.pallas_call(
        body, name=name, grid=(t // tt,),
        in_specs=[_row_spec(tt, SSM_WIDTH), _VMEM, _VMEM, _VMEM, _VMEM, _VMEM, _VMEM, _VMEM],
        out_specs=[_row_spec(tt, SSM_WIDTH), _row_spec(tt, STATE_WIDTH), _row_spec(tt, STATE_WIDTH)],
        out_shape=[jax.ShapeDtypeStruct((t, SSM_WIDTH), BF16), jax.ShapeDtypeStruct((t, STATE_WIDTH), F32),
                   jax.ShapeDtypeStruct((t, STATE_WIDTH), F32)],
        scratch_shapes=[pltpu.VMEM((1, STATE_WIDTH), F32), pltpu.VMEM((1, STATE_WIDTH), F32),
                        pltpu.VMEM((SEG, STATE_WIDTH), F32), pltpu.VMEM((SEG, STATE_WIDTH), F32),
                        pltpu.VMEM((tt, SSM_WIDTH), F32), pltpu.VMEM((tt, SSM_WIDTH), F32)],
        compiler_params=_params(("arbitrary",)),
    )(u, lam_re, lam_im, bb_re, bb_im, cc_re, cc_im, d_skip)


def ssm_bwd(dyg, u, h_re, h_im, lam_re, lam_im, bb_re, bb_im, cc_re, cc_im, d_skip, name):
    t = u.shape[0]
    tt = SCAN_TILE
    nt = t // tt

    def body(dyg_ref, u_ref, hr_ref, hi_ref, lr_ref, li_ref, bbr_ref, bbi_ref, ccr_ref, cci_ref, d_ref,
             du_ref, dlr_ref, dli_ref, dbbr_ref, dbbi_ref, dccr_ref, dcci_ref, dd_ref,
             ar_ref, ai_ref, cr_ref, ci_ref, pr_ref, pi_ref, up_ref, dy_ref, dup_ref):
        step = pl.program_id(0)
        tile = nt - 1 - step

        @pl.when(step == 0)
        def _():
            for ref in (cr_ref, ci_ref, dlr_ref, dli_ref, dbbr_ref, dbbi_ref, dccr_ref, dcci_ref, dd_ref):
                ref[...] = jnp.zeros_like(ref)
            _power_table(lr_ref[...], li_ref[...], pr_ref, pi_ref)

        to_segments = _perm_matrix(True)
        up_ref[...] = _permute_f32(to_segments, u_ref[...])
        dy_ref[...] = _permute_f32(to_segments, dyg_ref[...])
        uv = up_ref[...]
        ub = uv.astype(BF16)
        dskip = d_ref[...]
        for j in range(N_CHUNK):
            hs = slice(j * H_CHUNK, (j + 1) * H_CHUNK)
            us = slice(j * U_CHUNK, (j + 1) * U_CHUNK)
            hrb = hr_ref[:, hs].astype(BF16)
            hib = hi_ref[:, hs].astype(BF16)
            y = _nn(hrb, ccr_ref[j]) - _nn(hib, cci_ref[j]) + dskip[:, us] * uv[:, us]
            dy = dy_ref[:, us] * _gelu_and_grad(y)[1]
            dy_ref[:, us] = dy
            dyb = dy.astype(BF16)
            dccr_ref[j] += _tn(hrb, dyb)
            dcci_ref[j] -= _tn(hib, dyb)
            ar_ref[:, hs] = _nt(dyb, ccr_ref[j])
            ai_ref[:, hs] = -_nt(dyb, cci_ref[j])
        dd_ref[...] += jnp.sum(dy_ref[...] * uv, axis=0, keepdims=True)

        for c in range(STATE_WIDTH // SCAN_LANES):
            lanes = slice(c * SCAN_LANES, (c + 1) * SCAN_LANES)

            def dlambda(rows, carry, ar, ai, lanes=lanes):
                nr, ni, accr, acci = carry
                hr, hi = hr_ref[rows, lanes], hi_ref[rows, lanes]
                return ar, ai, accr + nr * hr + ni * hi, acci + ni * hr - nr * hi

            _, _, accr, acci = _segment_scan(
                ar_ref, ai_ref, lanes, (lr_ref[:, lanes], -li_ref[:, lanes]),
                lambda k, lanes=lanes: (pr_ref[pl.ds(k, 1), lanes], -pi_ref[pl.ds(k, 1), lanes]),
                cr_ref, ci_ref, True, dlambda)
            dlr_ref[:, lanes] += accr
            dli_ref[:, lanes] += acci

        rho = lax.broadcasted_iota(jnp.int32, (tt, U_CHUNK), 0)
        time = tile * tt + (rho % SUB) * SEG + rho // SUB
        for j in range(N_CHUNK):
            hs = slice(j * H_CHUNK, (j + 1) * H_CHUNK)
            us = slice(j * U_CHUNK, (j + 1) * U_CHUNK)
            arb = ar_ref[:, hs].astype(BF16)
            aib = ai_ref[:, hs].astype(BF16)
            dbbr_ref[j] += _tn(ub[:, us], arb)
            dbbi_ref[j] += _tn(ub[:, us], aib)
            du = _nt(arb, bbr_ref[j]) + _nt(aib, bbi_ref[j]) + dy_ref[:, us] * dskip[:, us]
            dup_ref[:, us] = jnp.where(time >= PAD_FRONT, du, 0.0)
        du_ref[...] = _nn(_perm_matrix(False), dup_ref[...].astype(BF16)).astype(BF16)

    rev = lambda i: (nt - 1 - i, 0)
    full = lambda shape: pl.BlockSpec(shape, lambda i: (0,) * len(shape))
    return pl.pallas_call(
        body, name=name, grid=(nt,),
        in_specs=[pl.BlockSpec((tt, SSM_WIDTH), rev), pl.BlockSpec((tt, SSM_WIDTH), rev),
                  pl.BlockSpec((tt, STATE_WIDTH), rev), pl.BlockSpec((tt, STATE_WIDTH), rev),
                  _VMEM, _VMEM, _VMEM, _VMEM, _VMEM, _VMEM, _VMEM],
        out_specs=[pl.BlockSpec((tt, SSM_WIDTH), rev), full((SUB, STATE_WIDTH)), full((SUB, STATE_WIDTH)),
                   full((N_CHUNK, U_CHUNK, H_CHUNK)), full((N_CHUNK, U_CHUNK, H_CHUNK)),
                   full((N_CHUNK, H_CHUNK, U_CHUNK)), full((N_CHUNK, H_CHUNK, U_CHUNK)), full((1, SSM_WIDTH))],
        out_shape=[jax.ShapeDtypeStruct((t, SSM_WIDTH), BF16),
                   jax.ShapeDtypeStruct((SUB, STATE_WIDTH), F32), jax.ShapeDtypeStruct((SUB, STATE_WIDTH), F32),
                   jax.ShapeDtypeStruct((N_CHUNK, U_CHUNK, H_CHUNK), F32),
                   jax.ShapeDtypeStruct((N_CHUNK, U_CHUNK, H_CHUNK), F32),
                   jax.ShapeDtypeStruct((N_CHUNK, H_CHUNK, U_CHUNK), F32),
                   jax.ShapeDtypeStruct((N_CHUNK, H_CHUNK, U_CHUNK), F32),
                   jax.ShapeDtypeStruct((1, SSM_WIDTH), F32)],
        scratch_shapes=[pltpu.VMEM((tt, STATE_WIDTH), F32), pltpu.VMEM((tt, STATE_WIDTH), F32),
                        pltpu.VMEM((1, STATE_WIDTH), F32), pltpu.VMEM((1, STATE_WIDTH), F32),
                        pltpu.VMEM((SEG, STATE_WIDTH), F32), pltpu.VMEM((SEG, STATE_WIDTH), F32),
                        pltpu.VMEM((tt, SSM_WIDTH), F32), pltpu.VMEM((tt, SSM_WIDTH), F32),
                        pltpu.VMEM((tt, SSM_WIDTH), F32)],
        compiler_params=_params(("arbitrary",)),
    )(dyg, u, h_re, h_im, lam_re, lam_im, bb_re, bb_im, cc_re, cc_im, d_skip)


def merge_fwd(h, o, yg, gates, wap_t, wv_t, wgg_t, wout, name):
    t, d = h.shape
    tm = TOKEN_TILE

    def body(h_ref, o_ref, yg_ref, gt_ref, wap_ref, wv_ref, wgg_ref, wout_ref, ho_ref, mg_ref, a_ref, sv_ref, sg_ref):
        att = _nt(o_ref[...], wap_ref[...])
        ygv = yg_ref[...]
        sv = _nt(ygv, wv_ref[...])
        sg = _nt(ygv, wgg_ref[...])
        a_ref[...] = att.astype(BF16)
        sv_ref[...] = sv.astype(BF16)
        sg_ref[...] = sg.astype(BF16)
        merged = (jax.nn.sigmoid(gt_ref[:, 0:d].astype(F32)) * att
                  + jax.nn.sigmoid(gt_ref[:, d:2 * d].astype(F32)) * (sv * jax.nn.sigmoid(sg))).astype(BF16)
        mg_ref[...] = merged
        ho_ref[...] = h_ref[...] + _nn(merged, wout_ref[...])

    return pl.pallas_call(
        body, name=name, grid=(t // tm,),
        in_specs=[_row_spec(tm, d), _row_spec(tm, ATTN_WIDTH), _row_spec(tm, SSM_WIDTH), _row_spec(tm, 2 * d),
                  _VMEM, _VMEM, _VMEM, _VMEM],
        out_specs=[_row_spec(tm, d), _row_spec(tm, d), _row_spec(tm, d), _row_spec(tm, d), _row_spec(tm, d)],
        out_shape=[jax.ShapeDtypeStruct((t, d), F32), jax.ShapeDtypeStruct((t, d), BF16),
                   jax.ShapeDtypeStruct((t, d), BF16), jax.ShapeDtypeStruct((t, d), BF16),
                   jax.ShapeDtypeStruct((t, d), BF16)],
        compiler_params=_params(("arbitrary",)),
    )(h, o, yg, gates, wap_t, wv_t, wgg_t, wout)


def merge_bwd(dh, gates, att, sv, sg, wap_t, wv_t, wgg_t, wout, dep, name):
    t, d = dh.shape
    tm = TOKEN_TILE

    def body(dh_ref, gt_ref, a_ref, sv_ref, sg_ref, wap_ref, wv_ref, wgg_ref, wout_ref, dep_ref,
             dgt_ref, da_ref, dsv_ref, dsg_ref, do_ref, dyg_ref, dhb_ref):
        dhb = dh_ref[...].astype(BF16)
        dhb_ref[...] = dhb
        dm = _nt(dhb, wout_ref[...])
        sig_a = jax.nn.sigmoid(gt_ref[:, 0:d].astype(F32))
        sig_s = jax.nn.sigmoid(gt_ref[:, d:2 * d].astype(F32))
        sig_g = jax.nn.sigmoid(sg_ref[...].astype(F32))
        svv = sv_ref[...].astype(F32)
        dgt_ref[:, 0:d] = (dm * a_ref[...].astype(F32) * sig_a * (1.0 - sig_a)).astype(BF16)
        dgt_ref[:, d:2 * d] = (dm * (svv * sig_g) * sig_s * (1.0 - sig_s)).astype(BF16)
        da = (dm * sig_a).astype(BF16)
        d_s = dm * sig_s
        dsv = (d_s * sig_g).astype(BF16)
        dsg = (d_s * svv * sig_g * (1.0 - sig_g)).astype(BF16)
        da_ref[...] = da
        dsv_ref[...] = dsv
        dsg_ref[...] = dsg
        do_ref[...] = _nn(da, wap_ref[...]).astype(BF16)
        dyg_ref[...] = _nn(dsv, wv_ref[...]) + _nn(dsg, wgg_ref[...])

    return pl.pallas_call(
        body, name=name, grid=(t // tm,),
        in_specs=[_row_spec(tm, d), _row_spec(tm, 2 * d), _row_spec(tm, d), _row_spec(tm, d), _row_spec(tm, d),
                  _VMEM, _VMEM, _VMEM, _VMEM, _ANY],
        out_specs=[_row_spec(tm, 2 * d), _row_spec(tm, d), _row_spec(tm, d), _row_spec(tm, d),
                   _row_spec(tm, ATTN_WIDTH), _row_spec(tm, SSM_WIDTH), _row_spec(tm, d)],
        out_shape=[jax.ShapeDtypeStruct((t, 2 * d), BF16), jax.ShapeDtypeStruct((t, d), BF16),
                   jax.ShapeDtypeStruct((t, d), BF16), jax.ShapeDtypeStruct((t, d), BF16),
                   jax.ShapeDtypeStruct((t, ATTN_WIDTH), BF16), jax.ShapeDtypeStruct((t, SSM_WIDTH), F32),
                   jax.ShapeDtypeStruct((t, d), BF16)],
        compiler_params=_params(("arbitrary",)),
    )(dh, gates, att, sv, sg, wap_t, wv_t, wgg_t, wout, dep)


def _adamw_math(w, g, m, v):
    mn = ADAM_B1 * m + (1.0 - ADAM_B1) * g
    vn = ADAM_B2 * v + (1.0 - ADAM_B2) * (g * g)
    m_hat = mn / (1.0 - ADAM_B1 ** ADAM_STEP)
    v_hat = vn / (1.0 - ADAM_B2 ** ADAM_STEP)
    return -ADAM_LR * (m_hat / (jnp.sqrt(v_hat) + ADAM_EPS) + ADAM_WD * w), mn, vn


def sum_adamw_layer(me, landed, partial, w, m, v, layer, prev, name):
    _, rows, cols = w.shape
    tr = rows // 2 if rows % 32 == 0 else rows
    steps = rows // tr

    def body(me_ref, land_ref, own_ref, w_ref, m_ref, v_ref, *rest):
        go_ref, d_ref, mo_ref, vo_ref = rest[-4:]
        who = me_ref[0]
        gv = land_ref[who ^ 1].astype(F32)
        for p in range(2, N_DEV):
            gv = gv + land_ref[who ^ p].astype(F32)
        gv = gv + own_ref[...].astype(F32)
        go_ref[0] = gv
        d_ref[0], mo_ref[0], vo_ref[0] = _adamw_math(w_ref[0], gv, m_ref[0], v_ref[0])

    spec3 = pl.BlockSpec((1, tr, cols), lambda r, me_ref: (layer, r, 0))
    out = jax.ShapeDtypeStruct(w.shape, F32)
    extra = [] if prev is None else list(prev)
    grid_spec = pltpu.PrefetchScalarGridSpec(
        num_scalar_prefetch=1, grid=(steps,),
        in_specs=[pl.BlockSpec((N_DEV, tr, cols), lambda r, me_ref: (0, r, 0)),
                  pl.BlockSpec((tr, cols), lambda r, me_ref: (me_ref[0] * steps + r, 0)),
                  spec3, spec3, spec3] + [_ANY] * len(extra),
        out_specs=[spec3] * 4)
    return pl.pallas_call(
        body, name=name, grid_spec=grid_spec, out_shape=[out] * 4,
        input_output_aliases={6 + j: j for j in range(len(extra))},
        compiler_params=_params(("arbitrary",)),
    )(me, landed, partial, w, m, v, *extra)


def adamw(w, g, m, v, name, minor_swap=False):
    if minor_swap:
        d, mn, vn = adamw(*[jnp.swapaxes(a, -1, -2) for a in (w, g, m, v)], name)
        return jnp.swapaxes(d, -1, -2), jnp.swapaxes(mn, -1, -2), jnp.swapaxes(vn, -1, -2)
    shape = w.shape
    as2d = lambda a: a.reshape(-1, shape[-1]) if a.ndim >= 2 else a.reshape(1, -1)
    w2, g2, m2, v2 = as2d(w), as2d(g), as2d(m), as2d(v)
    rows, cols = w2.shape
    tr = rows
    for cand in (1024, 704, 512, 256):
        if rows > cand and rows % cand == 0:
            tr = cand
            break

    def body(w_ref, g_ref, m_ref, v_ref, d_ref, mo_ref, vo_ref):
        d_ref[...], mo_ref[...], vo_ref[...] = _adamw_math(w_ref[...], g_ref[...], m_ref[...], v_ref[...])

    spec = _row_spec(tr, cols)
    out = jax.ShapeDtypeStruct((rows, cols), F32)
    d, mn, vn = pl.pallas_call(
        body, name=name, grid=(rows // tr,), in_specs=[spec] * 4, out_specs=[spec] * 3, out_shape=[out] * 3,
        compiler_params=_params(("arbitrary",)),
    )(w2, g2, m2, v2)
    return d.reshape(shape), mn.reshape(shape), vn.reshape(shape)


def _my_index():
    return 4 * lax.axis_index("x") + 2 * lax.axis_index("y") + lax.axis_index("c")


def _peer(p):
    return (lax.axis_index("x") ^ ((p >> 2) & 1), lax.axis_index("y") ^ ((p >> 1) & 1), lax.axis_index("c") ^ (p & 1))


_HBM = pl.BlockSpec(memory_space=pltpu.HBM)
_SEM = pl.BlockSpec(memory_space=pltpu.SEMAPHORE)
_EFFECT = pltpu.SideEffectType.DATAFLOW_SIDE_EFFECTING


class Exchange:
    RELAYED = (2, 4, 6)

    def __init__(self, srcs, scatter, name, relay=False):
        self.n = n = len(srcs)
        self.scatter = scatter
        self.name = name
        self.relayed = relay
        assert not (relay and scatter)
        self.direct = (1,) + self.RELAYED if relay else tuple(range(1, N_DEV))
        widths = sorted({s.shape[1] for s in srcs}, reverse=True)
        self.ncls = len(widths)
        self.cls = [widths.index(s.shape[1]) for s in srcs]
        self.cnts = [s.shape[0] // N_DEV if scatter else s.shape[0] for s in srcs]
        self.totals = [sum(c for c, k in zip(self.cnts, self.cls) if k == w) for w in range(self.ncls)]
        self.sizer = [max((k for k in range(n) if self.cls[k] == w), key=lambda k: self.cnts[k])
                      for w in range(self.ncls)]
        assert all(N_DEV * self.cnts[self.sizer[w]] >= self.totals[w] for w in range(self.ncls))
        if scatter:
            self.land_shapes = [(N_DEV, c, s.shape[1]) for s, c in zip(srcs, self.cnts)]
        else:
            self.land_shapes = [(N_DEV * c, s.shape[1]) for s, c in zip(srcs, self.cnts)]
        self.dtypes = [s.dtype for s in srcs]

    def _block(self, k, who):
        return pl.ds(pl.multiple_of(who * self.cnts[k], 16), self.cnts[k])

    def _sem(self, p, w):
        return (p - 1) * self.ncls + w

    def start(self, srcs, after):
        n = self.n

        def body(*refs):
            src, land = refs[:n], refs[n:2 * n]
            send_sems, recv_sems = refs[2 * n + 1], refs[2 * n + 2]
            token = refs[-1]
            me = _my_index()
            for p in self.direct:
                for k in range(n):
                    if self.scatter:
                        s_ref, d_ref = src[k].at[self._block(k, me ^ p), :], land[k].at[me]
                    else:
                        s_ref, d_ref = src[k], land[k].at[self._block(k, me), :]
                    pltpu.make_async_remote_copy(
                        src_ref=s_ref, dst_ref=d_ref, send_sem=send_sems.at[self._sem(p, self.cls[k])],
                        recv_sem=recv_sems.at[self._sem(p, self.cls[k])], device_id=_peer(p),
                        device_id_type=MESH).start()
            token[...] = jnp.zeros_like(token)

        sems = pltpu.SemaphoreType.DMA(((N_DEV - 1) * self.ncls,))
        thru = [pltpu.HBM(s.shape, s.dtype) for s in srcs] + [pltpu.HBM(shp, dt) for shp, dt in
                                                               zip(self.land_shapes, self.dtypes)]
        lands = [pltpu.with_memory_space_constraint(lax.empty(shp, dt), pltpu.HBM)
                 for shp, dt in zip(self.land_shapes, self.dtypes)]
        out = pl.pallas_call(
            body, name=self.name + "_start",
            in_specs=[_HBM] * (2 * n) + [_ANY],
            out_shape=[sems, sems] + thru + [jax.ShapeDtypeStruct((8, 128), F32)],
            out_specs=[_SEM, _SEM] + [_HBM] * (2 * n) + [_VMEM],
            input_output_aliases={j: 2 + j for j in range(2 * n)},
            compiler_params=pltpu.CompilerParams(has_side_effects=_EFFECT),
        )(*[pltpu.with_memory_space_constraint(s, pltpu.HBM) for s in srcs], *lands, after)
        return out[:-1], out[-1]

    def _span_copy(self, src, land, w, send_sem, recv_sem, p):
        big = src[self.sizer[w]] if self.scatter else land[self.sizer[w]]
        span = big.at[pl.ds(0, self.totals[w]), :]
        return pltpu.make_async_remote_copy(src_ref=span, dst_ref=span, send_sem=send_sem, recv_sem=recv_sem,
                                            device_id=_peer(p), device_id_type=MESH)

    def relay(self, state, after):
        n = self.n
        send_sems, recv_sems = state[0], state[1]
        thru = state[2:]
        after = list(after) if isinstance(after, (list, tuple)) else [after]
        first_out = 2 * n + 2 + len(after)

        def body(*refs):
            land = refs[n:2 * n]
            send_a, recv_a = refs[2 * n], refs[2 * n + 1]
            send_b, recv_b = refs[first_out], refs[first_out + 1]
            refs[-1][...] = jnp.zeros_like(refs[-1])
            me = _my_index()
            for p in self.RELAYED:
                for w in range(self.ncls):
                    self._span_copy(None, land, w, send_a.at[self._sem(p, w)], recv_a.at[self._sem(p, w)], p).wait_recv()
            for j, p in enumerate(self.RELAYED):
                for k in range(n):
                    rows = land[k].at[self._block(k, me ^ p), :]
                    pltpu.make_async_remote_copy(
                        src_ref=rows, dst_ref=rows, send_sem=send_b.at[j * self.ncls + self.cls[k]],
                        recv_sem=recv_b.at[j * self.ncls + self.cls[k]], device_id=_peer(1),
                        device_id_type=MESH).start()

        sems = pltpu.SemaphoreType.DMA((len(self.RELAYED) * self.ncls,))
        out = pl.pallas_call(
            body, name=self.name + "_relay",
            in_specs=[_HBM] * (2 * n) + [_SEM, _SEM] + [_ANY] * len(after),
            out_shape=[sems, sems] + [pltpu.HBM(a.shape, a.dtype) for a in thru] + [jax.ShapeDtypeStruct((8, 128), F32)],
            out_specs=[_SEM, _SEM] + [_HBM] * (2 * n) + [_VMEM],
            input_output_aliases={j: 2 + j for j in range(2 * n)},
            compiler_params=pltpu.CompilerParams(has_side_effects=_EFFECT),
        )(*thru, send_sems, recv_sems, *after)
        return [send_sems, recv_sems] + list(out[2:-1]) + [out[0], out[1]], out[-1]

    def wait(self, state, after):
        n = self.n
        send_sems, recv_sems = state[0], state[1]
        thru = state[2:2 + 2 * n]
        relay_sems = list(state[2 + 2 * n:])
        assert len(relay_sems) == (2 if self.relayed else 0)
        after = list(after) if isinstance(after, (list, tuple)) else [after]

        def body(*refs):
            src, land = refs[:n], refs[n:2 * n]
            send_a, recv_a = refs[2 * n], refs[2 * n + 1]
            for p in self.direct:
                for w in range(self.ncls):
                    copy = self._span_copy(src, land, w, send_a.at[self._sem(p, w)], recv_a.at[self._sem(p, w)], p)
                    copy.wait_send()
                    if not (self.relayed and p in self.RELAYED):
                        copy.wait_recv()
            if self.relayed:
                send_b, recv_b = refs[2 * n + 2], refs[2 * n + 3]
                for j in range(len(self.RELAYED)):
                    for w in range(self.ncls):
                        copy = self._span_copy(src, land, w, send_b.at[j * self.ncls + w],
                                               recv_b.at[j * self.ncls + w], 1)
                        copy.wait_send()
                        copy.wait_recv()

        out = pl.pallas_call(
            body, name=self.name + "_wait",
            in_specs=[_HBM] * (2 * n) + [_SEM] * (2 + len(relay_sems)) + [_ANY] * len(after),
            out_shape=[pltpu.HBM(a.shape, a.dtype) for a in thru], out_specs=[_HBM] * (2 * n),
            input_output_aliases={j: j for j in range(2 * n)},
            compiler_params=pltpu.CompilerParams(has_side_effects=_EFFECT),
        )(*thru, send_sems, recv_sems, *relay_sems, *after)
        return out[:n], out[n:]

    def place(self, lands, srcs):
        n = self.n
        assert not self.scatter

        def body(*refs):
            src, land = refs[n:2 * n], refs[2 * n:3 * n]
            bufs, sems = refs[3 * n:4 * n], refs[-1]
            me = _my_index()
            loads = [pltpu.make_async_copy(src[k], bufs[k], sems.at[k]) for k in range(n)]
            stores = [pltpu.make_async_copy(bufs[k], land[k].at[self._block(k, me), :], sems.at[k]) for k in range(n)]
            for cp in loads:
                cp.start()
            for k in range(n):
                loads[k].wait()
                stores[k].start()
            for cp in stores:
                cp.wait()

        return pl.pallas_call(
            body, name=self.name + "_place", in_specs=[_ANY] * (2 * n), out_specs=[_ANY] * n,
            out_shape=[jax.ShapeDtypeStruct(a.shape, a.dtype) for a in lands],
            input_output_aliases={j: j for j in range(n)},
            scratch_shapes=[pltpu.VMEM(s.shape, s.dtype) for s in srcs] + [pltpu.SemaphoreType.DMA((n,))],
        )(*lands, *srcs)


def sum_blocks(landed, full, name):
    _, cnt, cols = landed.shape

    def body(land_ref, full_ref, o_ref, own_ref, sem):
        me = _my_index()
        own = pltpu.make_async_copy(full_ref.at[pl.ds(pl.multiple_of(me * cnt, 16), cnt), :], own_ref, sem)
        own.start()
        acc = land_ref[me ^ 1].astype(F32)
        for p in range(2, N_DEV):
            acc = acc + land_ref[me ^ p].astype(F32)
        own.wait()
        o_ref[...] = acc + own_ref[...].astype(F32)

    return pl.pallas_call(
        body, name=name, in_specs=[_VMEM, _ANY], out_specs=_VMEM,
        out_shape=jax.ShapeDtypeStruct((cnt, cols), F32),
        scratch_shapes=[pltpu.VMEM((cnt, cols), landed.dtype), pltpu.SemaphoreType.DMA],
        compiler_params=_params(),
    )(landed, full)


def sum_slots(slots, name):
    _, rows, cols = slots.shape
    tr = rows
    if rows > 512:
        for cand in (256, 128, 64, 32, 16, 8):
            if rows % cand == 0:
                tr = cand
                break

    def body(s_ref, o_ref):
        acc = s_ref[0].astype(F32)
        for j in range(1, N_DEV):
            acc = acc + s_ref[j].astype(F32)
        o_ref[...] = acc

    return pl.pallas_call(
        body, name=name, grid=(rows // tr,),
        in_specs=[pl.BlockSpec((N_DEV, tr, cols), lambda i: (0, i, 0))], out_specs=_row_spec(tr, cols),
        out_shape=jax.ShapeDtypeStruct((rows, cols), F32), compiler_params=_params(("arbitrary",)),
    )(slots)


BIG_T = ("ffn1_w_gate", "ffn1_w_up", "w_in", "ffn2_w_gate", "ffn2_w_up")
BIG_N = ("ffn1_w_down", "w_out", "ffn2_w_down")
HALF_T = ("w_attn_proj", "w_glu_v", "w_glu_g")
SMALL = ("ffn1_norm", "mix_norm", "attn_sinks", "ssm_a_re", "ssm_a_im", "ssm_log_dt", "ssm_b_re", "ssm_b_im",
         "ssm_c_re", "ssm_c_im", "ssm_d", "ffn2_norm", "final_norm")
PARTS = {"ffn1": ("ffn1_w_gate", "ffn1_w_up", "ffn1_w_down"),
         "mix": ("w_in", "w_out", "w_attn_proj", "w_glu_v", "w_glu_g"),
         "ffn2": ("ffn2_w_gate", "ffn2_w_up", "ffn2_w_down")}


def _to_rows(name, a):
    return a if name in BIG_N else jnp.swapaxes(a, -1, -2)


def local_step(x, tgt, get_weights, put_grads, small):
    seq, d = x.shape
    t = PAD_FRONT + N_META + seq
    cos_t, sin_t = rope_tables(t)
    row = lambda a: a.reshape(1, -1)
    tables = []
    for i in range(DEPTH):
        b_re_t = jnp.swapaxes(small["ssm_b_re"][i], 1, 2)
        b_im_t = jnp.swapaxes(small["ssm_b_im"][i], 1, 2)
        lam_re, lam_im, bbar_re, bbar_im = ssm_prep(small["ssm_a_re"][i], small["ssm_a_im"][i],
                                                    small["ssm_log_dt"][i].reshape(-1, 1), b_re_t, b_im_t, f"ssm_prep_{i}")
        tables.append(((b_re_t, b_im_t),
                       (row(lam_re), row(lam_im), _block_diag_b(bbar_re).astype(BF16), _block_diag_b(bbar_im).astype(BF16),
                        _block_diag_c(small["ssm_c_re"][i]).astype(BF16), _block_diag_c(small["ssm_c_im"][i]).astype(BF16),
                        row(small["ssm_d"][i]))))
    early = [cos_t, sin_t] + [a for _, tab in tables for a in tab[2:6]]
    saved = []
    h = None
    for i in range(DEPTH):
        s = {}
        w = dict(get_weights(i, "ffn1", early if i == 0 else h))
        if i == 0:
            h = jnp.concatenate([jnp.zeros((PAD_FRONT, d), F32), w["meta_tokens"], x], axis=0)
        s["h0"] = h
        h, s["n1"], s["acts1"] = ffn_fwd(h, row(small["ffn1_norm"][i]), w["ffn1_w_gate"], w["ffn1_w_up"],
                                               w["ffn1_w_down"], f"ffn1_fwd_{i}")
        s["h1"] = h
        w.update(get_weights(i, "mix", h))
        s["n2"], s["qkv"], s["u"], s["gates"] = win_fwd(h, row(small["mix_norm"][i]), w["w_in"], cos_t, sin_t,
                                                        f"win_fwd_{i}")
        s["b_t"], s["ssm"] = tables[i]
        s["yg"], s["h_re"], s["h_im"] = ssm_fwd(s["u"], *s["ssm"], f"ssm_fwd_{i}")
        s["o"] = attn_fwd(s["qkv"], row(small["attn_sinks"][i]), f"attn_fwd_{i}")
        h, s["merged"], s["att"], s["sv"], s["sg"] = merge_fwd(
            h, s["o"], s["yg"], s["gates"], w["w_attn_proj"], w["w_glu_v"], w["w_glu_g"], w["w_out"],
            f"merge_fwd_{i}")
        s["h2"] = h
        w.update(get_weights(i, "ffn2", h))
        h, s["n3"], s["acts3"] = ffn_fwd(h, row(small["ffn2_norm"][i]), w["ffn2_w_gate"], w["ffn2_w_up"],
                                               w["ffn2_w_down"], f"ffn2_fwd_{i}")
        s["w"] = w
        saved.append(s)

    loss, dh, d_final = head_fwd_bwd(h, row(small["final_norm"]), tgt)
    gs = {k: [None] * DEPTH for k in SMALL if k != "final_norm"}
    dep = loss
    for i in reversed(range(DEPTH)):
        s = saved[i]
        w = s["w"]
        dh, da, db, sact, dhb, dg = ffn_bwd(dh, s["h2"], row(small["ffn2_norm"][i]), s["acts3"], w["ffn2_w_gate"],
                                            w["ffn2_w_up"], w["ffn2_w_down"], dep, f"ffn2_bwd_{i}")
        gs["ffn2_norm"][i] = dg[0]
        dep = put_grads(i, "ffn2", {"ffn2_w_gate": tn_matmul(da, s["n3"], f"ffn2_dwg_{i}"),
                                    "ffn2_w_up": tn_matmul(db, s["n3"], f"ffn2_dwu_{i}"),
                                    "ffn2_w_down": tn_matmul(sact, dhb, f"ffn2_dwd_{i}")})

        dgates, datt, dsv, dsg, do, dyg, dhb = merge_bwd(dh, s["gates"], s["att"], s["sv"], s["sg"], w["w_attn_proj"],
                                                         w["w_glu_v"], w["w_glu_g"], w["w_out"], dep, f"merge_bwd_{i}")
        gmix = {"w_out": tn_matmul(s["merged"], dhb, f"dwout_{i}"),
                "w_attn_proj": tn_matmul(datt, s["o"], f"dwap_{i}"),
                "w_glu_v": tn_matmul(dsv, s["yg"], f"dwv_{i}"),
                "w_glu_g": tn_matmul(dsg, s["yg"], f"dwgg_{i}")}
        dqkv, dsink = attn_bwd(s["qkv"], do, row(small["attn_sinks"][i]), cos_t, sin_t, f"attn_bwd_{i}")
        gs["attn_sinks"][i] = dsink[:, 0]
        du, dl_re, dl_im, dbb_re, dbb_im, dcc_re, dcc_im, dd = ssm_bwd(dyg, s["u"], s["h_re"], s["h_im"], *s["ssm"],
                                                                      f"ssm_bwd_{i}")
        fold = lambda a: jnp.sum(a, axis=0).reshape(SSM_GROUPS, SSM_STATE)
        da_re, da_im, dldt, db_re_t, db_im_t = ssm_prep_bwd(
            small["ssm_a_re"][i], small["ssm_a_im"][i], small["ssm_log_dt"][i].reshape(-1, 1), *s["b_t"],
            fold(dl_re), fold(dl_im), _diag_of_b(dbb_re), _diag_of_b(dbb_im), f"ssm_prep_bwd_{i}")
        gs["ssm_a_re"][i], gs["ssm_a_im"][i], gs["ssm_log_dt"][i] = da_re, da_im, dldt[:, 0]
        gs["ssm_b_re"][i], gs["ssm_b_im"][i] = jnp.swapaxes(db_re_t, 1, 2), jnp.swapaxes(db_im_t, 1, 2)
        gs["ssm_c_re"][i], gs["ssm_c_im"][i] = _diag_of_c(dcc_re), _diag_of_c(dcc_im)
        gs["ssm_d"][i] = dd[0]
        gmix["w_in"] = tn_matmul([dqkv, du, dgates], s["n2"], f"dwin_{i}")
        dep = put_grads(i, "mix", gmix)
        dh, dg = win_bwd(dh, s["h1"], row(small["mix_norm"][i]), dqkv, du, dgates, w["w_in"], dep, f"win_bwd_{i}")
        gs["mix_norm"][i] = dg[0]

        dh, da, db, sact, dhb, dg = ffn_bwd(dh, s["h0"], row(small["ffn1_norm"][i]), s["acts1"], w["ffn1_w_gate"],
                                            w["ffn1_w_up"], w["ffn1_w_down"], dep, f"ffn1_bwd_{i}")
        gs["ffn1_norm"][i] = dg[0]
        if i > 0:
            dep = put_grads(i, "ffn1", {"ffn1_w_gate": tn_matmul(da, s["n1"], f"ffn1_dwg_{i}"),
                                        "ffn1_w_up": tn_matmul(db, s["n1"], f"ffn1_dwu_{i}"),
                                        "ffn1_w_down": tn_matmul(sact, dhb, f"ffn1_dwd_{i}")})
        else:
            for k, xa, ya in (("ffn1_w_down", sact, dhb), ("ffn1_w_gate", da, s["n1"]), ("ffn1_w_up", db, s["n1"])):
                dep = put_grads(i, "ffn1", {k: tn_matmul(xa, ya, f"d_{k}_{i}", dep)})

    gs = {k: jnp.stack(v) for k, v in gs.items()}
    gs["final_norm"] = d_final[0]
    return loss[0, 0], dh[PAD_FRONT + N_META:], dh[PAD_FRONT:PAD_FRONT + N_META], gs, dep


def _pack_rows(arrays, cols):
    flat = jnp.concatenate([a.reshape(-1) for a in arrays])
    rows = -(-flat.shape[0] // cols)
    rows = -(-rows // 16) * 16
    return jnp.pad(flat, (0, rows * cols - flat.shape[0])).reshape(rows, cols)


def _unpack_rows(packed, shapes):
    flat = packed.reshape(-1)
    out, off = [], 0
    for shp in shapes:
        n = math.prod(shp)
        out.append(flat[off:off + n].reshape(shp))
        off += n
    return out


def kernel(x, meta_tokens, ffn1_norm, ffn1_w_gate, ffn1_w_up, ffn1_w_down, mix_norm, w_in, attn_sinks, ssm_a_re, ssm_a_im, ssm_log_dt, ssm_b_re, ssm_b_im, ssm_c_re, ssm_c_im, ssm_d, w_attn_proj, w_glu_v, w_glu_g, w_out, ffn2_norm, ffn2_w_gate, ffn2_w_up, ffn2_w_down, final_norm, loss_target, m_meta_tokens, m_ffn1_norm, m_ffn1_w_gate, m_ffn1_w_up, m_ffn1_w_down, m_mix_norm, m_w_in, m_attn_sinks, m_ssm_a_re, m_ssm_a_im, m_ssm_log_dt, m_ssm_b_re, m_ssm_b_im, m_ssm_c_re, m_ssm_c_im, m_ssm_d, m_w_attn_proj, m_w_glu_v, m_w_glu_g, m_w_out, m_ffn2_norm, m_ffn2_w_gate, m_ffn2_w_up, m_ffn2_w_down, m_final_norm, v_meta_tokens, v_ffn1_norm, v_ffn1_w_gate, v_ffn1_w_up, v_ffn1_w_down, v_mix_norm, v_w_in, v_attn_sinks, v_ssm_a_re, v_ssm_a_im, v_ssm_log_dt, v_ssm_b_re, v_ssm_b_im, v_ssm_c_re, v_ssm_c_im, v_ssm_d, v_w_attn_proj, v_w_glu_v, v_w_glu_g, v_w_out, v_ffn2_norm, v_ffn2_w_gate, v_ffn2_w_up, v_ffn2_w_down, v_final_norm):
    names = ("meta_tokens", "ffn1_norm", "ffn1_w_gate", "ffn1_w_up", "ffn1_w_down", "mix_norm", "w_in", "attn_sinks",
             "ssm_a_re", "ssm_a_im", "ssm_log_dt", "ssm_b_re", "ssm_b_im", "ssm_c_re", "ssm_c_im", "ssm_d",
             "w_attn_proj", "w_glu_v", "w_glu_g", "w_out", "ffn2_norm", "ffn2_w_gate", "ffn2_w_up", "ffn2_w_down",
             "final_norm")
    weights = dict(zip(names, (meta_tokens, ffn1_norm, ffn1_w_gate, ffn1_w_up, ffn1_w_down, mix_norm, w_in, attn_sinks, ssm_a_re, ssm_a_im, ssm_log_dt, ssm_b_re, ssm_b_im, ssm_c_re, ssm_c_im, ssm_d, w_attn_proj, w_glu_v, w_glu_g, w_out, ffn2_norm, ffn2_w_gate, ffn2_w_up, ffn2_w_down, final_norm)))
    moments_m = dict(zip(names, (m_meta_tokens, m_ffn1_norm, m_ffn1_w_gate, m_ffn1_w_up, m_ffn1_w_down, m_mix_norm, m_w_in, m_attn_sinks, m_ssm_a_re, m_ssm_a_im, m_ssm_log_dt, m_ssm_b_re, m_ssm_b_im, m_ssm_c_re, m_ssm_c_im, m_ssm_d, m_w_attn_proj, m_w_glu_v, m_w_glu_g, m_w_out, m_ffn2_norm, m_ffn2_w_gate, m_ffn2_w_up, m_ffn2_w_down, m_final_norm)))
    moments_v = dict(zip(names, (v_meta_tokens, v_ffn1_norm, v_ffn1_w_gate, v_ffn1_w_up, v_ffn1_w_down, v_mix_norm, v_w_in, v_attn_sinks, v_ssm_a_re, v_ssm_a_im, v_ssm_log_dt, v_ssm_b_re, v_ssm_b_im, v_ssm_c_re, v_ssm_c_im, v_ssm_d, v_w_attn_proj, v_w_glu_v, v_w_glu_g, v_w_out, v_ffn2_norm, v_ffn2_w_gate, v_ffn2_w_up, v_ffn2_w_down, v_final_norm)))
    me = _my_index()

    order = [(i, part) for i in range(DEPTH) for part in PARTS]
    gathers = {}
    token = jnp.zeros((8, 128), F32)
    for i, part in order:
        shards = [_to_rows(k, weights[k][i]).astype(BF16) for k in PARTS[part]]
        if (i, part) == order[0]:
            shards.append(meta_tokens)
        ex = Exchange(shards, False, f"gather_{part}_{i}", relay=True)
        state, token = ex.start(shards, token)
        gathers[i, part] = [ex, state, False]
    all_started = token

    def relay(group, after):
        ex, state, relayed = gathers[group]
        if relayed:
            return []
        new_state, relay_token = ex.relay(state, after)
        gathers[group][1:] = [new_state, True]
        return [relay_token]

    def get_weights(i, part, after):
        g = order.index((i, part))
        after = [all_started] + list(after) if g == 0 else [after]
        tokens = relay(order[g], after)
        if g >= 2 and g + 1 < len(order):
            tokens += relay(order[g + 1], after)
        ex, state, _ = gathers[i, part]
        shards, lands = ex.wait(state, after + tokens)
        fulls = ex.place(lands, shards)
        got = dict(zip(PARTS[part], fulls))
        if (i, part) == (0, "ffn1"):
            got["meta_tokens"] = jnp.swapaxes(fulls[-1].reshape(N_DEV, N_META, 128), 0, 1).reshape(N_META, D_MODEL)
        return got

    scatters = []

    def put_grads(i, part, gdict):
        ks = list(gdict)
        srcs = [gdict[k] for k in ks]
        ex = Exchange(srcs, True, f"scatter_{part if len(ks) > 1 else ks[0]}_{i}")
        state, tok = ex.start(srcs, all_started)
        scatters.append((i, ks, ex, state))
        return tok

    small = {k: weights[k] for k in SMALL}
    loss, dx, dmeta, gs, last_started = local_step(x[0], loss_target[0], get_weights, put_grads, small)

    grads, deltas, new_m, new_v = {}, {}, {}, {}
    small_list = [loss.reshape(1), dmeta] + [gs[k] for k in SMALL]
    packed = _pack_rows(small_list, D_MODEL)
    small_ex = Exchange([packed], False, "gather_small")
    small_state, after = small_ex.start([packed], last_started)

    updated = {}
    me_index = jnp.reshape(me, (1,)).astype(jnp.int32)
    for i, ks, ex, state in scatters:
        partials, lands = ex.wait(state, after)
        for k, partial, slots in zip(ks, partials, lands):
            updated[k] = sum_adamw_layer(me_index, slots, partial, _to_rows(k, weights[k]), _to_rows(k, moments_m[k]),
                                         _to_rows(k, moments_v[k]), i, updated.get(k), f"adamw_{k}_{i}")
            after = updated[k][0]
    for k, outs in updated.items():
        grads[k], deltas[k], new_m[k], new_v[k] = [_to_rows(k, a) for a in outs]

    packed_own, packed_all = small_ex.wait(small_state, after)
    (packed_all,) = small_ex.place(packed_all, packed_own)
    total = sum_slots(packed_all.reshape(N_DEV, packed.shape[0], D_MODEL), "sum_small")
    pieces = _unpack_rows(total, [a.shape for a in small_list])
    loss_out = pieces[0][0]
    grads["meta_tokens"] = lax.dynamic_slice_in_dim(pieces[1], me * 128, 128, axis=1)
    for k, p in zip(SMALL, pieces[2:]):
        grads[k] = p
    for k in ("meta_tokens",) + SMALL:
        deltas[k], new_m[k], new_v[k] = adamw(weights[k], grads[k], moments_m[k], moments_v[k], f"adamw_{k}",
                                              minor_swap=k in ("ssm_b_re", "ssm_b_im"))
    return (loss_out, dx[None], *[grads[k] for k in names], *[deltas[k] for k in names],
            *[new_m[k] for k in names], *[new_v[k] for k in names])
```

```python
import functools
import math

import jax
import jax.numpy as jnp
from jax import lax
from jax.experimental import pallas as pl
from jax.experimental.pallas import tpu as pltpu

F32 = jnp.float32
BF16 = jnp.bfloat16

D_MODEL = 1024
DEPTH = 2
N_META = 16
HEAD_DIM = 64
N_Q_HEADS = 8
ATTN_WIDTH = 512
KV_WIDTH = 128
QKV_WIDTH = ATTN_WIDTH + 2 * KV_WIDTH
WINDOW = 128
BLK = 128
ROPE_THETA = 500000.0
ROT_DIM = 16
SSM_WIDTH = 512
SSM_GROUP = 16
SSM_GROUPS = 32
SSM_STATE = 64
STATE_WIDTH = SSM_GROUPS * SSM_STATE
D_FF = 2816
IN_WIDTH = 3328
EPS = 1e-6
NEG_INF = -1e30
PAD_FRONT = (-N_META) % BLK
N_DEV = 8

ADAM_LR = 0.001
ADAM_B1 = 0.9
ADAM_B2 = 0.999
ADAM_EPS = 1e-08
ADAM_WD = 0.01
ADAM_STEP = 10

VMEM_LIMIT = 56 * 1024 * 1024
TOKEN_TILE = 384
_VMEM = pl.BlockSpec(memory_space=pltpu.VMEM)
_SMEM = pl.BlockSpec(memory_space=pltpu.SMEM)
_ANY = pl.BlockSpec(memory_space=pl.ANY)
MESH = pl.DeviceIdType.MESH


def _params(sem=None):
    return pltpu.CompilerParams(dimension_semantics=sem, vmem_limit_bytes=VMEM_LIMIT)


def _nt(a, b):
    return lax.dot_general(a, b, (((1,), (1,)), ((), ())), preferred_element_type=F32)


def _nn(a, b):
    return jnp.dot(a, b, preferred_element_type=F32)


def _tn(a, b):
    return lax.dot_general(a, b, (((0,), (0,)), ((), ())), preferred_element_type=F32)


def _row_spec(tm, width):
    return pl.BlockSpec((tm, width), lambda i: (i, 0))


def _acc_spec(shape):
    return pl.BlockSpec(shape, lambda i: (0,) * len(shape))


def _sigmoid(x):
    return 0.5 * jnp.tanh(0.5 * x) + 0.5


def _rms_stats(x):
    r = lax.rsqrt(jnp.mean(x * x, axis=-1, keepdims=True) + EPS)
    return x * r, r


def _rms_bwd(dn, xh, r, g):
    dg = jnp.sum(dn * xh, axis=0, keepdims=True)
    dxh = dn * g
    dx = r * (dxh - xh * jnp.mean(dxh * xh, axis=-1, keepdims=True))
    return dx, dg


def ffn_fwd(h, g, wg_t, wu_t, wd, name):
    t, d = h.shape
    f = wd.shape[0]
    tm = TOKEN_TILE

    def body(h_ref, g_ref, wg_ref, wu_ref, wd_ref, ho_ref, n_ref, sl_ref, p_ref, s_ref):
        x = h_ref[...]
        xh, _ = _rms_stats(x)
        n = (xh * g_ref[...]).astype(BF16)
        n_ref[...] = n
        a = _nt(n, wg_ref[...])
        b = _nt(n, wu_ref[...])
        sig = jax.nn.sigmoid(a)
        sl = a * sig
        sl_ref[...] = sl.astype(BF16)
        p_ref[...] = (b * (sig + sl * (1.0 - sig))).astype(BF16)
        s = (sl * b).astype(BF16)
        s_ref[...] = s
        ho_ref[...] = x + 0.5 * _nn(s, wd_ref[...])

    ho, n, sl, p, s = pl.pallas_call(
        body, name=name, grid=(t // tm,),
        in_specs=[_row_spec(tm, d), _acc_spec((1, d)), _VMEM, _VMEM, _VMEM],
        out_specs=[_row_spec(tm, d), _row_spec(tm, d), _row_spec(tm, f), _row_spec(tm, f), _row_spec(tm, f)],
        out_shape=[jax.ShapeDtypeStruct((t, d), F32), jax.ShapeDtypeStruct((t, d), BF16),
                   jax.ShapeDtypeStruct((t, f), BF16), jax.ShapeDtypeStruct((t, f), BF16),
                   jax.ShapeDtypeStruct((t, f), BF16)],
        compiler_params=_params(("arbitrary",)),
    )(h, g, wg_t, wu_t, wd)
    return ho, n, (sl, p, s)


def ffn_bwd(dh, h, g, acts, wg_t, wu_t, wd, dep, name):
    t, d = h.shape
    f = wd.shape[0]
    tm = TOKEN_TILE
    sl, p, s = acts

    def hidden_body(dh_ref, sl_ref, p_ref, wd_ref, dep_ref, da_ref, db_ref, dhb_ref):
        dhb = (0.5 * dh_ref[...]).astype(BF16)
        dhb_ref[...] = dhb
        ds = _nt(dhb, wd_ref[...])
        da_ref[...] = (ds * p_ref[...].astype(F32)).astype(BF16)
        db_ref[...] = (ds * sl_ref[...].astype(F32)).astype(BF16)

    da, db, dhb = pl.pallas_call(
        hidden_body, name=name + "_h", grid=(t // tm,),
        in_specs=[_row_spec(tm, d), _row_spec(tm, f), _row_spec(tm, f), _VMEM, _ANY],
        out_specs=[_row_spec(tm, f), _row_spec(tm, f), _row_spec(tm, d)],
        out_shape=[jax.ShapeDtypeStruct((t, f), BF16), jax.ShapeDtypeStruct((t, f), BF16),
                   jax.ShapeDtypeStruct((t, d), BF16)],
        compiler_params=_params(("arbitrary",)),
    )(dh, sl, p, wd, dep)

    def input_body(dh_ref, h_ref, g_ref, da_ref, db_ref, wg_ref, wu_ref, dhi_ref, dg_ref):
        dn = _nn(da_ref[...], wg_ref[...]) + _nn(db_ref[...], wu_ref[...])
        xh, r = _rms_stats(h_ref[...])
        dx, dg = _rms_bwd(dn, xh, r, g_ref[...])
        dhi_ref[...] = dh_ref[...] + dx

        @pl.when(pl.program_id(0) == 0)
        def _():
            dg_ref[...] = jnp.zeros_like(dg_ref)

        dg_ref[...] += dg

    dhi, dg = pl.pallas_call(
        input_body, name=name + "_x", grid=(t // tm,),
        in_specs=[_row_spec(tm, d), _row_spec(tm, d), _acc_spec((1, d)), _row_spec(tm, f), _row_spec(tm, f),
                  _VMEM, _VMEM],
        out_specs=[_row_spec(tm, d), _acc_spec((1, d))],
        out_shape=[jax.ShapeDtypeStruct((t, d), F32), jax.ShapeDtypeStruct((1, d), F32)],
        compiler_params=_params(("arbitrary",)),
    )(dh, h, g, da, db, wg_t, wu_t)
    return dhi, da, db, s, dhb, dg


DW_TILE = 256


def tn_matmul(x, y, name, dep=None):
    xs = list(x) if isinstance(x, (list, tuple)) else [x]
    t = xs[0].shape[0]
    n = y.shape[1]
    bm = DW_TILE
    tiles = [a.shape[1] // bm for a in xs]
    offs = [sum(tiles[:k]) for k in range(len(xs))]
    deps = [] if dep is None else [dep]

    def body(*refs):
        y_ref, o_ref = refs[len(xs)], refs[-1]
        i = pl.program_id(0)
        for k in range(len(xs)):
            @pl.when((i >= offs[k]) & (i < offs[k] + tiles[k]))
            def _(k=k):
                o_ref[...] = _tn(refs[k][...], y_ref[...]).astype(BF16)

    def x_spec(k):
        return pl.BlockSpec((t, bm), lambda i: (0, jnp.clip(i - offs[k], 0, tiles[k] - 1)))

    return pl.pallas_call(
        body, name=name, grid=(sum(tiles),),
        in_specs=[x_spec(k) for k in range(len(xs))] + [_VMEM] + [_ANY] * len(deps),
        out_specs=pl.BlockSpec((bm, n), lambda i: (i, 0)),
        out_shape=jax.ShapeDtypeStruct((sum(tiles) * bm, n), BF16),
        compiler_params=_params(("arbitrary",)),
    )(*xs, y, *deps)


def head_fwd_bwd(h, g, tgt):
    t, d = h.shape

    def body(h_ref, g_ref, t_ref, loss_ref, dh_ref, dg_ref):
        i = pl.program_id(0)
        xh, r = _rms_stats(h_ref[...])
        gv = g_ref[...]
        valid = (i > 0).astype(F32)
        e = (xh * gv - t_ref[...]) * valid
        dx, dg = _rms_bwd(e * (1.0 / d), xh, r, gv)
        dh_ref[...] = dx

        @pl.when(i == 0)
        def _():
            dg_ref[...] = jnp.zeros_like(dg_ref)
            loss_ref[...] = jnp.zeros_like(loss_ref)

        dg_ref[...] += dg
        loss_ref[...] += jnp.sum(e * e) * (0.5 / d)

    return pl.pallas_call(
        body, name="head", grid=(t // BLK,),
        in_specs=[_row_spec(BLK, d), _acc_spec((1, d)),
                  pl.BlockSpec((BLK, d), lambda i: (jnp.maximum(i - 1, 0), 0))],
        out_specs=[_acc_spec((1, 128)), _row_spec(BLK, d), _acc_spec((1, d))],
        out_shape=[jax.ShapeDtypeStruct((1, 128), F32), jax.ShapeDtypeStruct((t, d), F32),
                   jax.ShapeDtypeStruct((1, d), F32)],
        compiler_params=_params(("arbitrary",)),
    )(h, g, tgt)


def rope_tables(t):
    pos = jnp.arange(t, dtype=F32) - PAD_FRONT
    inv_freq = ROPE_THETA ** (-jnp.arange(0, ROT_DIM, 2, dtype=F32) / ROT_DIM)
    ang = pos[:, None] * inv_freq[None, :]
    cos, sin = jnp.cos(ang), jnp.sin(ang)
    ones = jnp.ones((t, HEAD_DIM - ROT_DIM), F32)
    cos_h = jnp.concatenate([cos, cos, ones], axis=1)
    sin_h = jnp.concatenate([-sin, sin, 0.0 * ones], axis=1)
    return jnp.concatenate([cos_h, cos_h], axis=1), jnp.concatenate([sin_h, sin_h], axis=1)


def _swap_halves(x):
    n = x.shape[1]
    lane = lax.broadcasted_iota(jnp.int32, x.shape, 1)
    return jnp.where(lane % HEAD_DIM < ROT_DIM // 2, pltpu.roll(x, n - ROT_DIM // 2, 1), pltpu.roll(x, ROT_DIM // 2, 1))


def _rope(x, cos_t, sin_t, sign):
    return x * cos_t + sign * (_swap_halves(x) * sin_t)


def win_fwd(h, g, win_t, cos_t, sin_t, name):
    t, d = h.shape
    tm = TOKEN_TILE

    def body(h_ref, g_ref, w_ref, c_ref, s_ref, n_ref, qkv_ref, u_ref, gates_ref):
        xh, _ = _rms_stats(h_ref[...])
        n = (xh * g_ref[...]).astype(BF16)
        n_ref[...] = n
        z = _nt(n, w_ref[...])
        c, s = c_ref[...], s_ref[...]
        for j in range((ATTN_WIDTH + KV_WIDTH) // 128):
            qkv_ref[:, j * 128:(j + 1) * 128] = _rope(z[:, j * 128:(j + 1) * 128], c, s, 1.0).astype(BF16)
        qkv_ref[:, ATTN_WIDTH + KV_WIDTH:QKV_WIDTH] = z[:, ATTN_WIDTH + KV_WIDTH:QKV_WIDTH].astype(BF16)
        for j in range(N_CHUNK):
            u_ref[j] = z[:, QKV_WIDTH + j * U_CHUNK:QKV_WIDTH + (j + 1) * U_CHUNK]
        gates_ref[...] = z[:, QKV_WIDTH + SSM_WIDTH:].astype(BF16)

    return pl.pallas_call(
        body, name=name, grid=(t // tm,),
        in_specs=[_row_spec(tm, d), _acc_spec((1, d)), _VMEM, _row_spec(tm, 128), _row_spec(tm, 128)],
        out_specs=[_row_spec(tm, d), _row_spec(tm, QKV_WIDTH), _chunked_spec(tm, lambda i: i), _row_spec(tm, 2 * d)],
        out_shape=[jax.ShapeDtypeStruct((t, d), BF16), jax.ShapeDtypeStruct((t, QKV_WIDTH), BF16),
                   jax.ShapeDtypeStruct((N_CHUNK, t, U_CHUNK), F32), jax.ShapeDtypeStruct((t, 2 * d), BF16)],
        compiler_params=_params(("arbitrary",)),
    )(h, g, win_t, cos_t, sin_t)


def win_bwd(dh, h, g, dqkv, du, dgates, win_t, dep, name):
    t, d = h.shape
    tm = TOKEN_TILE

    def body(dh_ref, h_ref, g_ref, dqkv_ref, du_ref, dgt_ref, w_ref, dep_ref, dhi_ref, dg_ref):
        dn = (_nn(dqkv_ref[...], w_ref[0:QKV_WIDTH, :])
              + _nn(du_ref[...], w_ref[QKV_WIDTH:QKV_WIDTH + SSM_WIDTH, :])
              + _nn(dgt_ref[...], w_ref[QKV_WIDTH + SSM_WIDTH:, :]))
        xh, r = _rms_stats(h_ref[...])
        dx, dg = _rms_bwd(dn, xh, r, g_ref[...])
        dhi_ref[...] = dh_ref[...] + dx

        @pl.when(pl.program_id(0) == 0)
        def _():
            dg_ref[...] = jnp.zeros_like(dg_ref)

        dg_ref[...] += dg

    return pl.pallas_call(
        body, name=name, grid=(t // tm,),
        in_specs=[_row_spec(tm, d), _row_spec(tm, d), _acc_spec((1, d)), _row_spec(tm, QKV_WIDTH),
                  _row_spec(tm, SSM_WIDTH), _row_spec(tm, 2 * d), _VMEM, _ANY],
        out_specs=[_row_spec(tm, d), _acc_spec((1, d))],
        out_shape=[jax.ShapeDtypeStruct((t, d), F32), jax.ShapeDtypeStruct((1, d), F32)],
        compiler_params=_params(("arbitrary",)),
    )(dh, h, g, dqkv, du, dgates, win_t, dep)


def _attn_mask(blk):
    q_pos = blk * BLK + lax.broadcasted_iota(jnp.int32, (BLK, 3 * BLK), 0) - PAD_FRONT
    col = lax.broadcasted_iota(jnp.int32, (BLK, 3 * BLK), 1)
    part = col // BLK
    k_pos = jnp.where(part == 0, col, (blk + part - 2) * BLK + (col - part * BLK)) - PAD_FRONT
    dist = q_pos - k_pos
    meta_ok = (part == 0) & (k_pos >= 0) & (dist >= 0)
    band_ok = (part > 0) & (k_pos >= N_META) & (dist >= 0) & (dist < WINDOW)
    return meta_ok | band_ok


def _head_halves(x128, kv):
    x = x128.astype(F32)
    lane = lax.broadcasted_iota(jnp.int32, x.shape, 1)
    swapped = pltpu.roll(x, HEAD_DIM, 1)
    lo, hi = (x, swapped) if kv == 0 else (swapped, x)
    return jnp.where(lane < HEAD_DIM, lo, 0.0).astype(BF16), jnp.where(lane >= HEAD_DIM, hi, 0.0).astype(BF16)


def _gather_keys(meta_ref, prev_ref, cur_ref, lo):
    return jnp.concatenate([meta_ref[:, lo:lo + 128], prev_ref[:, lo:lo + 128], cur_ref[:, lo:lo + 128]], axis=0)


def _pair_lanes(kv):
    return slice(2 * kv * 128, (2 * kv + 1) * 128), slice((2 * kv + 1) * 128, (2 * kv + 2) * 128)


def _stacked_sinks(sink_ref, head):
    row = lax.broadcasted_iota(jnp.int32, (2 * BLK, 1), 0)
    return jnp.where(row < BLK, sink_ref[0, head], sink_ref[0, head + 2])


def _softmax_with_sink(s, mask, sink):
    s = jnp.where(mask, s * (HEAD_DIM ** -0.5), NEG_INF)
    m = jnp.maximum(jnp.max(s, axis=-1, keepdims=True), sink)
    p = jnp.exp(s - m)
    p_sink = jnp.exp(sink - m)
    inv = 1.0 / (jnp.sum(p, axis=-1, keepdims=True) + p_sink)
    return p * inv, p_sink * inv


def attn_fwd(qkv, sinks, name):
    t = qkv.shape[0]
    nb = t // BLK

    def body(sink_ref, meta_ref, prev_ref, cur_ref, o_ref):
        blk = pl.program_id(0)
        mask = _attn_mask(blk)
        mask2 = jnp.concatenate([mask, mask], axis=0)
        k128 = _gather_keys(meta_ref, prev_ref, cur_ref, ATTN_WIDTH)
        v128 = _gather_keys(meta_ref, prev_ref, cur_ref, ATTN_WIDTH + KV_WIDTH)
        for kv in range(2):
            k_lo, k_hi = _head_halves(k128, kv)
            v_lo, v_hi = _head_halves(v128, kv)
            lanes0, lanes1 = _pair_lanes(kv)
            q2 = jnp.concatenate([cur_ref[:, lanes0], cur_ref[:, lanes1]], axis=0)
            p_a, _ = _softmax_with_sink(_nt(q2, k_lo), mask2, _stacked_sinks(sink_ref, 4 * kv))
            p_b, _ = _softmax_with_sink(_nt(q2, k_hi), mask2, _stacked_sinks(sink_ref, 4 * kv + 1))
            o2 = (_nn(p_a.astype(BF16), v_lo) + _nn(p_b.astype(BF16), v_hi)).astype(BF16)
            o_ref[:, lanes0] = o2[0:BLK]
            o_ref[:, lanes1] = o2[BLK:2 * BLK]

    blk_spec = lambda f: pl.BlockSpec((BLK, QKV_WIDTH), f)
    return pl.pallas_call(
        body, name=name, grid=(nb,),
        in_specs=[_SMEM, blk_spec(lambda i: (0, 0)), blk_spec(lambda i: (jnp.maximum(i - 1, 0), 0)),
                  blk_spec(lambda i: (i, 0))],
        out_specs=_row_spec(BLK, ATTN_WIDTH),
        out_shape=jax.ShapeDtypeStruct((t, ATTN_WIDTH), BF16),
        compiler_params=_params(("arbitrary",)),
    )(sinks, qkv, qkv, qkv)


def attn_bwd(qkv, do, sinks, cos_t, sin_t, name):
    t = qkv.shape[0]
    nb = t // BLK

    def body(sink_ref, meta_ref, prev_ref, cur_ref, do_ref, c_ref, s_ref, dqkv_ref, dsink_ref, carry_ref, macc_ref):
        step = pl.program_id(0)
        blk = nb - 1 - step

        @pl.when(step == 0)
        def _():
            dsink_ref[...] = jnp.zeros_like(dsink_ref)
            carry_ref[...] = jnp.zeros_like(carry_ref)
            macc_ref[...] = jnp.zeros_like(macc_ref)

        mask = _attn_mask(blk)
        mask2 = jnp.concatenate([mask, mask], axis=0)
        lane = lax.broadcasted_iota(jnp.int32, (3 * BLK, 128), 1)
        k128 = _gather_keys(meta_ref, prev_ref, cur_ref, ATTN_WIDTH)
        v128 = _gather_keys(meta_ref, prev_ref, cur_ref, ATTN_WIDTH + KV_WIDTH)
        cos_b, sin_b = c_ref[...], s_ref[...]
        dk_heads, dv_heads = [], []
        for kv in range(2):
            k_lo, k_hi = _head_halves(k128, kv)
            v_lo, v_hi = _head_halves(v128, kv)
            lanes0, lanes1 = _pair_lanes(kv)
            q2 = jnp.concatenate([cur_ref[:, lanes0], cur_ref[:, lanes1]], axis=0)
            do2 = jnp.concatenate([do_ref[:, lanes0], do_ref[:, lanes1]], axis=0)
            ds_half, p_half = [], []
            for half, (k_h, v_h) in enumerate(((k_lo, v_lo), (k_hi, v_hi))):
                head = 4 * kv + half
                p, p_sink = _softmax_with_sink(_nt(q2, k_h), mask2, _stacked_sinks(sink_ref, head))
                dp = _nt(do2, v_h)
                dsum = jnp.sum(p * dp, axis=-1, keepdims=True)
                ds_half.append((p * (dp - dsum) * (HEAD_DIM ** -0.5)).astype(BF16))
                p_half.append(p.astype(BF16))
                dsink = p_sink * dsum
                for part, h in ((0, head), (1, head + 2)):
                    total = -jnp.sum(dsink[part * BLK:(part + 1) * BLK], axis=0, keepdims=True)
                    dsink_ref[h:h + 1, :] += jnp.broadcast_to(total, (1, 128))
            dq2 = _nn(ds_half[0], k_lo) + _nn(ds_half[1], k_hi)
            dqkv_ref[:, lanes0] = _rope(dq2[0:BLK], cos_b, sin_b, -1.0).astype(BF16)
            dqkv_ref[:, lanes1] = _rope(dq2[BLK:2 * BLK], cos_b, sin_b, -1.0).astype(BF16)
            dk_acc = jnp.where(lane < HEAD_DIM, _tn(ds_half[0], q2), _tn(ds_half[1], q2))
            dv_acc = jnp.where(lane < HEAD_DIM, _tn(p_half[0], do2), _tn(p_half[1], do2))
            dk_heads.append(dk_acc + pltpu.roll(dk_acc, HEAD_DIM, 1))
            dv_heads.append(dv_acc + pltpu.roll(dv_acc, HEAD_DIM, 1))
        dkv = jnp.concatenate([jnp.where(lane < HEAD_DIM, dk_heads[0], dk_heads[1]),
                               jnp.where(lane < HEAD_DIM, dv_heads[0], dv_heads[1])], axis=1)
        macc_ref[...] += dkv[0:BLK]
        is_last = (blk == 0).astype(F32)
        mine = dkv[2 * BLK:3 * BLK] + carry_ref[...] + is_last * macc_ref[...]
        carry_ref[...] = dkv[BLK:2 * BLK]
        dqkv_ref[:, ATTN_WIDTH:ATTN_WIDTH + KV_WIDTH] = _rope(mine[:, 0:128], cos_b, sin_b, -1.0).astype(BF16)
        dqkv_ref[:, ATTN_WIDTH + KV_WIDTH:QKV_WIDTH] = mine[:, 128:256].astype(BF16)

    rev = lambda i: nb - 1 - i
    blk_spec = lambda f: pl.BlockSpec((BLK, QKV_WIDTH), f)
    return pl.pallas_call(
        body, name=name, grid=(nb,),
        in_specs=[_SMEM, blk_spec(lambda i: (0, 0)), blk_spec(lambda i: (jnp.maximum(rev(i) - 1, 0), 0)),
                  blk_spec(lambda i: (rev(i), 0)), pl.BlockSpec((BLK, ATTN_WIDTH), lambda i: (rev(i), 0)),
                  pl.BlockSpec((BLK, 128), lambda i: (rev(i), 0)), pl.BlockSpec((BLK, 128), lambda i: (rev(i), 0))],
        out_specs=[pl.BlockSpec((BLK, QKV_WIDTH), lambda i: (rev(i), 0)), _acc_spec((N_Q_HEADS, 128))],
        out_shape=[jax.ShapeDtypeStruct((t, QKV_WIDTH), BF16), jax.ShapeDtypeStruct((N_Q_HEADS, 128), F32)],
        scratch_shapes=[pltpu.VMEM((BLK, 256), F32), pltpu.VMEM((BLK, 256), F32)],
        compiler_params=_params(("arbitrary",)),
    )(sinks, qkv, qkv, qkv, do, cos_t, sin_t)


def _cmul(ar, ai, br, bi):
    return ar * br - ai * bi, ar * bi + ai * br


def ssm_prep(a_re, a_im, log_dt, b_re_t, b_im_t, name):
    def body(ar_ref, ai_ref, ldt_ref, br_ref, bi_ref, lr_ref, li_ref, bbr_ref, bbi_ref):
        ar, ai = ar_ref[...], ai_ref[...]
        dt = jnp.exp(ldt_ref[...])
        mag = jnp.exp(ar * dt)
        lr = mag * jnp.cos(ai * dt)
        li = mag * jnp.sin(ai * dt)
        den = ar * ar + ai * ai
        nr = lr - 1.0
        cr = ((nr * ar + li * ai) / den)[:, None, :]
        ci = ((li * ar - nr * ai) / den)[:, None, :]
        br, bi = br_ref[...], bi_ref[...]
        lr_ref[...] = lr
        li_ref[...] = li
        bbr_ref[...] = cr * br - ci * bi
        bbi_ref[...] = cr * bi + ci * br

    gp = jax.ShapeDtypeStruct(a_re.shape, F32)
    gcp = jax.ShapeDtypeStruct(b_re_t.shape, F32)
    return pl.pallas_call(body, name=name, out_shape=[gp, gp, gcp, gcp],
                          in_specs=[_VMEM] * 5, out_specs=[_VMEM] * 4)(a_re, a_im, log_dt, b_re_t, b_im_t)


def ssm_prep_bwd(a_re, a_im, log_dt, b_re_t, b_im_t, dl_re, dl_im, dbb_re, dbb_im, name):
    def body(ar_ref, ai_ref, ldt_ref, br_ref, bi_ref, dlr_ref, dli_ref, dbbr_ref, dbbi_ref,
             dar_ref, dai_ref, dldt_ref, dbr_ref, dbi_ref):
        ar, ai = ar_ref[...], ai_ref[...]
        dt = jnp.exp(ldt_ref[...])
        mag = jnp.exp(ar * dt)
        lr = mag * jnp.cos(ai * dt)
        li = mag * jnp.sin(ai * dt)
        den = ar * ar + ai * ai
        nr = lr - 1.0
        cr = (nr * ar + li * ai) / den
        ci = (li * ar - nr * ai) / den
        br, bi = br_ref[...], bi_ref[...]
        dbbr, dbbi = dbbr_ref[...], dbbi_ref[...]
        dbr_ref[...] = cr[:, None, :] * dbbr + ci[:, None, :] * dbbi
        dbi_ref[...] = cr[:, None, :] * dbbi - ci[:, None, :] * dbbr
        dcr = jnp.sum(br * dbbr + bi * dbbi, axis=1)
        dci = jnp.sum(br * dbbi - bi * dbbr, axis=1)
        d_num_r = dcr / den
        d_num_i = dci / den
        d_den = -(dcr * cr + dci * ci) / den
        d_lr = dlr_ref[...] + d_num_r * ar - d_num_i * ai
        d_li = dli_ref[...] + d_num_r * ai + d_num_i * ar
        d_ar = d_num_r * nr + d_num_i * li + d_den * 2.0 * ar
        d_ai = d_num_r * li - d_num_i * nr + d_den * 2.0 * ai
        d_mag = (d_lr * lr + d_li * li) / mag
        d_theta = d_li * lr - d_lr * li
        d_ardt = d_mag * mag
        dar_ref[...] = d_ar + d_ardt * dt
        dai_ref[...] = d_ai + d_theta * dt
        d_dt = jnp.sum(d_ardt * ar + d_theta * ai, axis=1, keepdims=True)
        dldt_ref[...] = d_dt * dt

    gp = jax.ShapeDtypeStruct(a_re.shape, F32)
    gcp = jax.ShapeDtypeStruct(b_re_t.shape, F32)
    return pl.pallas_call(body, name=name, out_shape=[gp, gp, jax.ShapeDtypeStruct(log_dt.shape, F32), gcp, gcp],
                          in_specs=[_VMEM] * 9, out_specs=[_VMEM] * 5,
                          )(a_re, a_im, log_dt, b_re_t, b_im_t, dl_re, dl_im, dbb_re, dbb_im)


N_CHUNK = 4
U_CHUNK = SSM_WIDTH // N_CHUNK
H_CHUNK = STATE_WIDTH // N_CHUNK
SUB = 8


def _block_diag_b(bb):
    x = bb.reshape(N_CHUNK, 8, SSM_GROUP, 1, SSM_STATE)
    same = (jnp.arange(8)[:, None] == jnp.arange(8)[None, :])[None, :, None, :, None]
    return jnp.where(same, x, 0.0).reshape(N_CHUNK, U_CHUNK, H_CHUNK)


def _block_diag_c(c):
    x = jnp.swapaxes(c.reshape(N_CHUNK, 8, SSM_GROUP, SSM_STATE), 2, 3)[:, :, :, None, :]
    same = (jnp.arange(8)[:, None] == jnp.arange(8)[None, :])[None, :, None, :, None]
    return jnp.where(same, x, 0.0).reshape(N_CHUNK, H_CHUNK, U_CHUNK)


def _diag_of_b(m):
    x = m.reshape(N_CHUNK, 8, SSM_GROUP, 8, SSM_STATE)
    return jnp.stack([x[:, g, :, g, :] for g in range(8)], axis=1).reshape(SSM_GROUPS, SSM_GROUP, SSM_STATE)


def _diag_of_c(m):
    x = m.reshape(N_CHUNK, 8, SSM_STATE, 8, SSM_GROUP)
    d = jnp.stack([x[:, g, :, g, :] for g in range(8)], axis=1)
    return jnp.swapaxes(d, 2, 3).reshape(SSM_GROUPS, SSM_GROUP, SSM_STATE)


def _lambda_tables(lr, li, reverse):
    p1 = (lr, li)
    p2 = _cmul(*p1, *p1)
    p4 = _cmul(*p2, *p2)
    rows = [p1]
    for _ in range(SUB - 1):
        rows.append(_cmul(*rows[-1], *p1))
    if reverse:
        rows = rows[::-1]
    return p1, p2, p4, (jnp.concatenate([r[0] for r in rows], axis=0), jnp.concatenate([r[1] for r in rows], axis=0))


def _scan8(xr, xi, pows, table, cr, ci, reverse):
    row = lax.broadcasted_iota(jnp.int32, xr.shape, 0)
    for d, (pr, pi) in zip((1, 2, 4), pows):
        if reverse:
            sr, si = pltpu.roll(xr, SUB - d, 0), pltpu.roll(xi, SUB - d, 0)
            keep = row < SUB - d
        else:
            sr, si = pltpu.roll(xr, d, 0), pltpu.roll(xi, d, 0)
            keep = row >= d
        sr = jnp.where(keep, sr, 0.0)
        si = jnp.where(keep, si, 0.0)
        xr, xi = xr + pr * sr - pi * si, xi + pr * si + pi * sr
    tr, ti = table
    return xr + tr * cr - ti * ci, xi + tr * ci + ti * cr


def _gelu_and_grad(y):
    k0 = math.sqrt(2.0 / math.pi)
    inner = k0 * (y + 0.044715 * y * y * y)
    th = jnp.tanh(inner)
    g = 0.5 * y * (1.0 + th)
    dg = 0.5 * (1.0 + th) + 0.5 * y * (1.0 - th * th) * k0 * (1.0 + 3.0 * 0.044715 * y * y)
    return g, dg


SCAN_TILE = TOKEN_TILE
SEG = SCAN_TILE // SUB
SCAN_LANES = 512


def _perm_matrix(to_segments):
    a = lax.broadcasted_iota(jnp.int32, (SCAN_TILE, SCAN_TILE), 0)
    b = lax.broadcasted_iota(jnp.int32, (SCAN_TILE, SCAN_TILE), 1)
    rho, time = (a, b) if to_segments else (b, a)
    return (time == (rho % SUB) * SEG + rho // SUB).astype(BF16)


def _chunked_spec(rows, block_of):
    return pl.BlockSpec((N_CHUNK, rows, U_CHUNK), lambda i: (0, block_of(i), 0))


def _load_segments(src_ref, dst_ref):
    for j in range(N_CHUNK):
        for r in range(SEG):
            dst_ref[r * SUB:(r + 1) * SUB, j * U_CHUNK:(j + 1) * U_CHUNK] = src_ref.at[j][pl.ds(r, SUB, stride=SEG), :]


def _power_table(lr, li, pr_ref, pi_ref):
    cur = (lr, li)
    for r in range(SEG):
        pr_ref[r:r + 1, :] = cur[0]
        pi_ref[r:r + 1, :] = cur[1]
        cur = _cmul(*cur, lr, li)


def _segment_scan(xr_ref, xi_ref, lanes, lam, table_row, cr_ref, ci_ref, reverse, extra=None):
    lr, li = lam
    row = lax.broadcasted_iota(jnp.int32, (SUB, SCAN_LANES), 0)

    def rows_of(k):
        r = SEG - 1 - k if reverse else k
        return pl.ds(pl.multiple_of(r * SUB, SUB), SUB)

    def first(k, st):
        sr, si = st
        rows = rows_of(k)
        nr = lr * sr - li * si + xr_ref[rows, lanes]
        ni = lr * si + li * sr + xi_ref[rows, lanes]
        xr_ref[rows, lanes] = nr
        xi_ref[rows, lanes] = ni
        return nr, ni

    zero = jnp.zeros((SUB, SCAN_LANES), F32)
    er, ei = lax.fori_loop(0, SEG, first, (zero, zero))
    l16 = table_row(SEG - 1)
    q1, q2, q4, tab = _lambda_tables(l16[0], l16[1], reverse)
    c_r, c_i = cr_ref[:, lanes], ci_ref[:, lanes]
    gr, gi = _scan8(er, ei, (q1, q2, q4), tab, c_r, c_i, reverse)
    if reverse:
        cin_r = jnp.where(row == SUB - 1, c_r, pltpu.roll(gr, SUB - 1, 0))
        cin_i = jnp.where(row == SUB - 1, c_i, pltpu.roll(gi, SUB - 1, 0))
        cr_ref[:, lanes] = gr[0:1]
        ci_ref[:, lanes] = gi[0:1]
    else:
        cin_r = jnp.where(row == 0, c_r, pltpu.roll(gr, 1, 0))
        cin_i = jnp.where(row == 0, c_i, pltpu.roll(gi, 1, 0))
        cr_ref[:, lanes] = gr[SUB - 1:SUB]
        ci_ref[:, lanes] = gi[SUB - 1:SUB]

    def second(k, carry):
        rows = rows_of(k)
        tr, ti = table_row(k)
        ar = xr_ref[rows, lanes] + tr * cin_r - ti * cin_i
        ai = xi_ref[rows, lanes] + tr * cin_i + ti * cin_r
        xr_ref[rows, lanes] = ar
        xi_ref[rows, lanes] = ai
        if extra is None:
            return carry
        return extra(rows, carry, ar, ai)

    init = 0 if extra is None else (cin_r, cin_i, zero, zero)
    return lax.fori_loop(0, SEG, second, init)


def ssm_fwd(u, lam_re, lam_im, bb_re, bb_im, cc_re, cc_im, d_skip, name):
    t = u.shape[1]
    tt = SCAN_TILE

    def body(u_ref, lr_ref, li_ref, bbr_ref, bbi_ref, ccr_ref, cci_ref, d_ref, yg_ref, hr_ref, hi_ref,
             cr_ref, ci_ref, pr_ref, pi_ref, up_ref, y_ref):
        @pl.when(pl.program_id(0) == 0)
        def _():
            cr_ref[...] = jnp.zeros_like(cr_ref)
            ci_ref[...] = jnp.zeros_like(ci_ref)
            _power_table(lr_ref[...], li_ref[...], pr_ref, pi_ref)

        _load_segments(u_ref, up_ref)
        ub = up_ref[...].astype(BF16)
        for j in range(N_CHUNK):
            hs = slice(j * H_CHUNK, (j + 1) * H_CHUNK)
            us = slice(j * U_CHUNK, (j + 1) * U_CHUNK)
            hr_ref[:, hs] = _nn(ub[:, us], bbr_ref[j])
            hi_ref[:, hs] = _nn(ub[:, us], bbi_ref[j])
        for c in range(STATE_WIDTH // SCAN_LANES):
            lanes = slice(c * SCAN_LANES, (c + 1) * SCAN_LANES)
            _segment_scan(hr_ref, hi_ref, lanes, (lr_ref[:, lanes], li_ref[:, lanes]),
                          lambda k, lanes=lanes: (pr_ref[pl.ds(k, 1), lanes], pi_ref[pl.ds(k, 1), lanes]),
                          cr_ref, ci_ref, False)
        for j in range(N_CHUNK):
            hs = slice(j * H_CHUNK, (j + 1) * H_CHUNK)
            us = slice(j * U_CHUNK, (j + 1) * U_CHUNK)
            y = (_nn(hr_ref[:, hs].astype(BF16), ccr_ref[j]) - _nn(hi_ref[:, hs].astype(BF16), cci_ref[j])
                 + d_ref[:, us] * up_ref[:, us])
            y_ref[:, us] = _gelu_and_grad(y)[0]
        yg_ref[...] = _nn(_perm_matrix(False), y_ref[...].astype(BF16)).astype(BF16)

    return pl.pallas_call(
        body, name=name, grid=(t // tt,),
        in_specs=[_chunked_spec(tt, lambda i: i), _VMEM, _VMEM, _VMEM, _VMEM, _VMEM, _VMEM, _VMEM],
        out_specs=[_row_spec(tt, SSM_WIDTH), _row_spec(tt, STATE_WIDTH), _row_spec(tt, STATE_WIDTH)],
        out_shape=[jax.ShapeDtypeStruct((t, SSM_WIDTH), BF16), jax.ShapeDtypeStruct((t, STATE_WIDTH), F32),
                   jax.ShapeDtypeStruct((t, STATE_WIDTH), F32)],
        scratch_shapes=[pltpu.VMEM((1, STATE_WIDTH), F32), pltpu.VMEM((1, STATE_WIDTH), F32),
                        pltpu.VMEM((SEG, STATE_WIDTH), F32), pltpu.VMEM((SEG, STATE_WIDTH), F32),
                        pltpu.VMEM((tt, SSM_WIDTH), F32), pltpu.VMEM((tt, SSM_WIDTH), F32)],
        compiler_params=_params(("arbitrary",)),
    )(u, lam_re, lam_im, bb_re, bb_im, cc_re, cc_im, d_skip)


def ssm_bwd(dyg, u, h_re, h_im, lam_re, lam_im, bb_re, bb_im, cc_re, cc_im, d_skip, name):
    t = u.shape[1]
    tt = SCAN_TILE
    nt = t // tt

    def body(dyg_ref, u_ref, hr_ref, hi_ref, lr_ref, li_ref, bbr_ref, bbi_ref, ccr_ref, cci_ref, d_ref,
             du_ref, dlr_ref, dli_ref, dbbr_ref, dbbi_ref, dccr_ref, dcci_ref, dd_ref,
             ar_ref, ai_ref, cr_ref, ci_ref, pr_ref, pi_ref, up_ref, dy_ref, dup_ref):
        step = pl.program_id(0)
        tile = nt - 1 - step

        @pl.when(step == 0)
        def _():
            for ref in (cr_ref, ci_ref, dlr_ref, dli_ref, dbbr_ref, dbbi_ref, dccr_ref, dcci_ref, dd_ref):
                ref[...] = jnp.zeros_like(ref)
            _power_table(lr_ref[...], li_ref[...], pr_ref, pi_ref)

        _load_segments(u_ref, up_ref)
        _load_segments(dyg_ref, dy_ref)
        uv = up_ref[...]
        ub = uv.astype(BF16)
        dskip = d_ref[...]
        for j in range(N_CHUNK):
            hs = slice(j * H_CHUNK, (j + 1) * H_CHUNK)
            us = slice(j * U_CHUNK, (j + 1) * U_CHUNK)
            hrb = hr_ref[:, hs].astype(BF16)
            hib = hi_ref[:, hs].astype(BF16)
            y = _nn(hrb, ccr_ref[j]) - _nn(hib, cci_ref[j]) + dskip[:, us] * uv[:, us]
            dy = dy_ref[:, us] * _gelu_and_grad(y)[1]
            dy_ref[:, us] = dy
            dyb = dy.astype(BF16)
            dccr_ref[j] += _tn(hrb, dyb)
            dcci_ref[j] -= _tn(hib, dyb)
            ar_ref[:, hs] = _nt(dyb, ccr_ref[j])
            ai_ref[:, hs] = -_nt(dyb, cci_ref[j])
        dd_ref[...] += jnp.sum(dy_ref[...] * uv, axis=0, keepdims=True)

        for c in range(STATE_WIDTH // SCAN_LANES):
            lanes = slice(c * SCAN_LANES, (c + 1) * SCAN_LANES)

            def dlambda(rows, carry, ar, ai, lanes=lanes):
                nr, ni, accr, acci = carry
                hr, hi = hr_ref[rows, lanes], hi_ref[rows, lanes]
                return ar, ai, accr + nr * hr + ni * hi, acci + ni * hr - nr * hi

            _, _, accr, acci = _segment_scan(
                ar_ref, ai_ref, lanes, (lr_ref[:, lanes], -li_ref[:, lanes]),
                lambda k, lanes=lanes: (pr_ref[pl.ds(k, 1), lanes], -pi_ref[pl.ds(k, 1), lanes]),
                cr_ref, ci_ref, True, dlambda)
            dlr_ref[:, lanes] += accr
            dli_ref[:, lanes] += acci

        rho = lax.broadcasted_iota(jnp.int32, (tt, U_CHUNK), 0)
        time = tile * tt + (rho % SUB) * SEG + rho // SUB
        for j in range(N_CHUNK):
            hs = slice(j * H_CHUNK, (j + 1) * H_CHUNK)
            us = slice(j * U_CHUNK, (j + 1) * U_CHUNK)
            arb = ar_ref[:, hs].astype(BF16)
            aib = ai_ref[:, hs].astype(BF16)
            dbbr_ref[j] += _tn(ub[:, us], arb)
            dbbi_ref[j] += _tn(ub[:, us], aib)
            du = _nt(arb, bbr_ref[j]) + _nt(aib, bbi_ref[j]) + dy_ref[:, us] * dskip[:, us]
            dup_ref[:, us] = jnp.where(time >= PAD_FRONT, du, 0.0)
        du_ref[...] = _nn(_perm_matrix(False), dup_ref[...].astype(BF16)).astype(BF16)

    rev = lambda i: (nt - 1 - i, 0)
    full = lambda shape: pl.BlockSpec(shape, lambda i: (0,) * len(shape))
    return pl.pallas_call(
        body, name=name, grid=(nt,),
        in_specs=[_chunked_spec(tt, lambda i: nt - 1 - i), _chunked_spec(tt, lambda i: nt - 1 - i),
                  pl.BlockSpec((tt, STATE_WIDTH), rev), pl.BlockSpec((tt, STATE_WIDTH), rev),
                  _VMEM, _VMEM, _VMEM, _VMEM, _VMEM, _VMEM, _VMEM],
        out_specs=[pl.BlockSpec((tt, SSM_WIDTH), rev), full((SUB, STATE_WIDTH)), full((SUB, STATE_WIDTH)),
                   full((N_CHUNK, U_CHUNK, H_CHUNK)), full((N_CHUNK, U_CHUNK, H_CHUNK)),
                   full((N_CHUNK, H_CHUNK, U_CHUNK)), full((N_CHUNK, H_CHUNK, U_CHUNK)), full((1, SSM_WIDTH))],
        out_shape=[jax.ShapeDtypeStruct((t, SSM_WIDTH), BF16),
                   jax.ShapeDtypeStruct((SUB, STATE_WIDTH), F32), jax.ShapeDtypeStruct((SUB, STATE_WIDTH), F32),
                   jax.ShapeDtypeStruct((N_CHUNK, U_CHUNK, H_CHUNK), F32),
                   jax.ShapeDtypeStruct((N_CHUNK, U_CHUNK, H_CHUNK), F32),
                   jax.ShapeDtypeStruct((N_CHUNK, H_CHUNK, U_CHUNK), F32),
                   jax.ShapeDtypeStruct((N_CHUNK, H_CHUNK, U_CHUNK), F32),
                   jax.ShapeDtypeStruct((1, SSM_WIDTH), F32)],
        scratch_shapes=[pltpu.VMEM((tt, STATE_WIDTH), F32), pltpu.VMEM((tt, STATE_WIDTH), F32),
                        pltpu.VMEM((1, STATE_WIDTH), F32), pltpu.VMEM((1, STATE_WIDTH), F32),
                        pltpu.VMEM((SEG, STATE_WIDTH), F32), pltpu.VMEM((SEG, STATE_WIDTH), F32),
                        pltpu.VMEM((tt, SSM_WIDTH), F32), pltpu.VMEM((tt, SSM_WIDTH), F32),
                        pltpu.VMEM((tt, SSM_WIDTH), F32)],
        compiler_params=_params(("arbitrary",)),
    )(dyg, u, h_re, h_im, lam_re, lam_im, bb_re, bb_im, cc_re, cc_im, d_skip)


def merge_fwd(h, o, yg, gates, wap_t, wv_t, wgg_t, wout, name):
    t, d = h.shape
    tm = TOKEN_TILE

    def body(h_ref, o_ref, yg_ref, gt_ref, wap_ref, wv_ref, wgg_ref, wout_ref, ho_ref, mg_ref, a_ref, sv_ref, sg_ref):
        att = _nt(o_ref[...], wap_ref[...])
        ygv = yg_ref[...]
        sv = _nt(ygv, wv_ref[...])
        sg = _nt(ygv, wgg_ref[...])
        a_ref[...] = att.astype(BF16)
        sv_ref[...] = sv.astype(BF16)
        sg_ref[...] = sg.astype(BF16)
        merged = (jax.nn.sigmoid(gt_ref[:, 0:d].astype(F32)) * att
                  + jax.nn.sigmoid(gt_ref[:, d:2 * d].astype(F32)) * (sv * jax.nn.sigmoid(sg))).astype(BF16)
        mg_ref[...] = merged
        ho_ref[...] = h_ref[...] + _nn(merged, wout_ref[...])

    return pl.pallas_call(
        body, name=name, grid=(t // tm,),
        in_specs=[_row_spec(tm, d), _row_spec(tm, ATTN_WIDTH), _row_spec(tm, SSM_WIDTH), _row_spec(tm, 2 * d),
                  _VMEM, _VMEM, _VMEM, _VMEM],
        out_specs=[_row_spec(tm, d), _row_spec(tm, d), _row_spec(tm, d), _row_spec(tm, d), _row_spec(tm, d)],
        out_shape=[jax.ShapeDtypeStruct((t, d), F32), jax.ShapeDtypeStruct((t, d), BF16),
                   jax.ShapeDtypeStruct((t, d), BF16), jax.ShapeDtypeStruct((t, d), BF16),
                   jax.ShapeDtypeStruct((t, d), BF16)],
        compiler_params=_params(("arbitrary",)),
    )(h, o, yg, gates, wap_t, wv_t, wgg_t, wout)


def merge_bwd(dh, gates, att, sv, sg, wap_t, wv_t, wgg_t, wout, dep, name):
    t, d = dh.shape
    tm = TOKEN_TILE

    def body(dh_ref, gt_ref, a_ref, sv_ref, sg_ref, wap_ref, wv_ref, wgg_ref, wout_ref, dep_ref,
             dgt_ref, da_ref, dsv_ref, dsg_ref, do_ref, dyg_ref, dhb_ref):
        dhb = dh_ref[...].astype(BF16)
        dhb_ref[...] = dhb
        dm = _nt(dhb, wout_ref[...])
        sig_a = jax.nn.sigmoid(gt_ref[:, 0:d].astype(F32))
        sig_s = jax.nn.sigmoid(gt_ref[:, d:2 * d].astype(F32))
        sig_g = jax.nn.sigmoid(sg_ref[...].astype(F32))
        svv = sv_ref[...].astype(F32)
        dgt_ref[:, 0:d] = (dm * a_ref[...].astype(F32) * sig_a * (1.0 - sig_a)).astype(BF16)
        dgt_ref[:, d:2 * d] = (dm * (svv * sig_g) * sig_s * (1.0 - sig_s)).astype(BF16)
        da = (dm * sig_a).astype(BF16)
        d_s = dm * sig_s
        dsv = (d_s * sig_g).astype(BF16)
        dsg = (d_s * svv * sig_g * (1.0 - sig_g)).astype(BF16)
        da_ref[...] = da
        dsv_ref[...] = dsv
        dsg_ref[...] = dsg
        do_ref[...] = _nn(da, wap_ref[...]).astype(BF16)
        dyg = _nn(dsv, wv_ref[...]) + _nn(dsg, wgg_ref[...])
        for j in range(N_CHUNK):
            dyg_ref[j] = dyg[:, j * U_CHUNK:(j + 1) * U_CHUNK]

    return pl.pallas_call(
        body, name=name, grid=(t // tm,),
        in_specs=[_row_spec(tm, d), _row_spec(tm, 2 * d), _row_spec(tm, d), _row_spec(tm, d), _row_spec(tm, d),
                  _VMEM, _VMEM, _VMEM, _VMEM, _ANY],
        out_specs=[_row_spec(tm, 2 * d), _row_spec(tm, d), _row_spec(tm, d), _row_spec(tm, d),
                   _row_spec(tm, ATTN_WIDTH), _chunked_spec(tm, lambda i: i), _row_spec(tm, d)],
        out_shape=[jax.ShapeDtypeStruct((t, 2 * d), BF16), jax.ShapeDtypeStruct((t, d), BF16),
                   jax.ShapeDtypeStruct((t, d), BF16), jax.ShapeDtypeStruct((t, d), BF16),
                   jax.ShapeDtypeStruct((t, ATTN_WIDTH), BF16), jax.ShapeDtypeStruct((N_CHUNK, t, U_CHUNK), F32),
                   jax.ShapeDtypeStruct((t, d), BF16)],
        compiler_params=_params(("arbitrary",)),
    )(dh, gates, att, sv, sg, wap_t, wv_t, wgg_t, wout, dep)


def _adamw_math(w, g, m, v):
    mn = ADAM_B1 * m + (1.0 - ADAM_B1) * g
    vn = ADAM_B2 * v + (1.0 - ADAM_B2) * (g * g)
    m_hat = mn / (1.0 - ADAM_B1 ** ADAM_STEP)
    v_hat = vn / (1.0 - ADAM_B2 ** ADAM_STEP)
    return -ADAM_LR * (m_hat / (jnp.sqrt(v_hat) + ADAM_EPS) + ADAM_WD * w), mn, vn


def sum_adamw_layer(me, landed, partial, w, m, v, layer, prev, name):
    _, rows, cols = w.shape
    tr = rows // 2 if rows % 32 == 0 else rows
    steps = rows // tr

    def body(me_ref, land_ref, own_ref, w_ref, m_ref, v_ref, *rest):
        go_ref, d_ref, mo_ref, vo_ref = rest[-4:]
        who = me_ref[0]
        gv = land_ref[who ^ 1].astype(F32)
        for p in range(2, N_DEV):
            gv = gv + land_ref[who ^ p].astype(F32)
        gv = gv + own_ref[...].astype(F32)
        go_ref[0] = gv
        d_ref[0], mo_ref[0], vo_ref[0] = _adamw_math(w_ref[0], gv, m_ref[0], v_ref[0])

    spec3 = pl.BlockSpec((1, tr, cols), lambda r, me_ref: (layer, r, 0))
    out = jax.ShapeDtypeStruct(w.shape, F32)
    extra = [] if prev is None else list(prev)
    grid_spec = pltpu.PrefetchScalarGridSpec(
        num_scalar_prefetch=1, grid=(steps,),
        in_specs=[pl.BlockSpec((N_DEV, tr, cols), lambda r, me_ref: (0, r, 0)),
                  pl.BlockSpec((tr, cols), lambda r, me_ref: (me_ref[0] * steps + r, 0)),
                  spec3, spec3, spec3] + [_ANY] * len(extra),
        out_specs=[spec3] * 4)
    return pl.pallas_call(
        body, name=name, grid_spec=grid_spec, out_shape=[out] * 4,
        input_output_aliases={6 + j: j for j in range(len(extra))},
        compiler_params=_params(("arbitrary",)),
    )(me, landed, partial, w, m, v, *extra)


def adamw(w, g, m, v, name, minor_swap=False):
    if minor_swap:
        d, mn, vn = adamw(*[jnp.swapaxes(a, -1, -2) for a in (w, g, m, v)], name)
        return jnp.swapaxes(d, -1, -2), jnp.swapaxes(mn, -1, -2), jnp.swapaxes(vn, -1, -2)
    shape = w.shape
    as2d = lambda a: a.reshape(-1, shape[-1]) if a.ndim >= 2 else a.reshape(1, -1)
    w2, g2, m2, v2 = as2d(w), as2d(g), as2d(m), as2d(v)
    rows, cols = w2.shape
    tr = rows
    for cand in (1024, 704, 512, 256):
        if rows > cand and rows % cand == 0:
            tr = cand
            break

    def body(w_ref, g_ref, m_ref, v_ref, d_ref, mo_ref, vo_ref):
        d_ref[...], mo_ref[...], vo_ref[...] = _adamw_math(w_ref[...], g_ref[...], m_ref[...], v_ref[...])

    spec = _row_spec(tr, cols)
    out = jax.ShapeDtypeStruct((rows, cols), F32)
    d, mn, vn = pl.pallas_call(
        body, name=name, grid=(rows // tr,), in_specs=[spec] * 4, out_specs=[spec] * 3, out_shape=[out] * 3,
        compiler_params=_params(("arbitrary",)),
    )(w2, g2, m2, v2)
    return d.reshape(shape), mn.reshape(shape), vn.reshape(shape)


def _my_index():
    return 4 * lax.axis_index("x") + 2 * lax.axis_index("y") + lax.axis_index("c")


def _peer(p):
    return (lax.axis_index("x") ^ ((p >> 2) & 1), lax.axis_index("y") ^ ((p >> 1) & 1), lax.axis_index("c") ^ (p & 1))


_HBM = pl.BlockSpec(memory_space=pltpu.HBM)
_SEM = pl.BlockSpec(memory_space=pltpu.SEMAPHORE)
_EFFECT = pltpu.SideEffectType.DATAFLOW_SIDE_EFFECTING


class Exchange:
    RELAYED = (2, 4, 6)

    def __init__(self, srcs, scatter, name, relay=False):
        self.n = n = len(srcs)
        self.scatter = scatter
        self.name = name
        self.relayed = relay
        assert not (relay and scatter)
        self.direct = (1,) + self.RELAYED if relay else tuple(range(1, N_DEV))
        widths = sorted({s.shape[1] for s in srcs}, reverse=True)
        self.ncls = len(widths)
        self.cls = [widths.index(s.shape[1]) for s in srcs]
        self.cnts = [s.shape[0] // N_DEV if scatter else s.shape[0] for s in srcs]
        self.totals = [sum(c for c, k in zip(self.cnts, self.cls) if k == w) for w in range(self.ncls)]
        self.sizer = [max((k for k in range(n) if self.cls[k] == w), key=lambda k: self.cnts[k])
                      for w in range(self.ncls)]
        assert all(N_DEV * self.cnts[self.sizer[w]] >= self.totals[w] for w in range(self.ncls))
        if scatter:
            self.land_shapes = [(N_DEV, c, s.shape[1]) for s, c in zip(srcs, self.cnts)]
        else:
            self.land_shapes = [(N_DEV * c, s.shape[1]) for s, c in zip(srcs, self.cnts)]
        self.dtypes = [s.dtype for s in srcs]

    def _block(self, k, who):
        return pl.ds(pl.multiple_of(who * self.cnts[k], 16), self.cnts[k])

    def _sem(self, p, w):
        return (p - 1) * self.ncls + w

    def start(self, srcs, after):
        n = self.n

        def body(*refs):
            src, land = refs[:n], refs[n:2 * n]
            send_sems, recv_sems = refs[2 * n + 1], refs[2 * n + 2]
            token = refs[-1]
            me = _my_index()
            for p in self.direct:
                for k in range(n):
                    if self.scatter:
                        s_ref, d_ref = src[k].at[self._block(k, me ^ p), :], land[k].at[me]
                    else:
                        s_ref, d_ref = src[k], land[k].at[self._block(k, me), :]
                    pltpu.make_async_remote_copy(
                        src_ref=s_ref, dst_ref=d_ref, send_sem=send_sems.at[self._sem(p, self.cls[k])],
                        recv_sem=recv_sems.at[self._sem(p, self.cls[k])], device_id=_peer(p),
                        device_id_type=MESH).start()
            token[...] = jnp.zeros_like(token)

        sems = pltpu.SemaphoreType.DMA(((N_DEV - 1) * self.ncls,))
        thru = [pltpu.HBM(s.shape, s.dtype) for s in srcs] + [pltpu.HBM(shp, dt) for shp, dt in
                                                               zip(self.land_shapes, self.dtypes)]
        lands = [pltpu.with_memory_space_constraint(lax.empty(shp, dt), pltpu.HBM)
                 for shp, dt in zip(self.land_shapes, self.dtypes)]
        out = pl.pallas_call(
            body, name=self.name + "_start",
            in_specs=[_HBM] * (2 * n) + [_ANY],
            out_shape=[sems, sems] + thru + [jax.ShapeDtypeStruct((8, 128), F32)],
            out_specs=[_SEM, _SEM] + [_HBM] * (2 * n) + [_VMEM],
            input_output_aliases={j: 2 + j for j in range(2 * n)},
            compiler_params=pltpu.CompilerParams(has_side_effects=_EFFECT),
        )(*[pltpu.with_memory_space_constraint(s, pltpu.HBM) for s in srcs], *lands, after)
        return out[:-1], out[-1]

    def _span_copy(self, src, land, w, send_sem, recv_sem, p):
        big = src[self.sizer[w]] if self.scatter else land[self.sizer[w]]
        span = big.at[pl.ds(0, self.totals[w]), :]
        return pltpu.make_async_remote_copy(src_ref=span, dst_ref=span, send_sem=send_sem, recv_sem=recv_sem,
                                            device_id=_peer(p), device_id_type=MESH)

    def relay(self, state, after):
        n = self.n
        send_sems, recv_sems = state[0], state[1]
        thru = state[2:]
        after = list(after) if isinstance(after, (list, tuple)) else [after]
        first_out = 2 * n + 2 + len(after)

        def body(*refs):
            land = refs[n:2 * n]
            send_a, recv_a = refs[2 * n], refs[2 * n + 1]
            send_b, recv_b = refs[first_out], refs[first_out + 1]
            refs[-1][...] = jnp.zeros_like(refs[-1])
            me = _my_index()
            for p in self.RELAYED:
                for w in range(self.ncls):
                    self._span_copy(None, land, w, send_a.at[self._sem(p, w)], recv_a.at[self._sem(p, w)], p).wait_recv()
            for j, p in enumerate(self.RELAYED):
                for k in range(n):
                    rows = land[k].at[self._block(k, me ^ p), :]
                    pltpu.make_async_remote_copy(
                        src_ref=rows, dst_ref=rows, send_sem=send_b.at[j * self.ncls + self.cls[k]],
                        recv_sem=recv_b.at[j * self.ncls + self.cls[k]], device_id=_peer(1),
                        device_id_type=MESH).start()

        sems = pltpu.SemaphoreType.DMA((len(self.RELAYED) * self.ncls,))
        out = pl.pallas_call(
            body, name=self.name + "_relay",
            in_specs=[_HBM] * (2 * n) + [_SEM, _SEM] + [_ANY] * len(after),
            out_shape=[sems, sems] + [pltpu.HBM(a.shape, a.dtype) for a in thru] + [jax.ShapeDtypeStruct((8, 128), F32)],
            out_specs=[_SEM, _SEM] + [_HBM] * (2 * n) + [_VMEM],
            input_output_aliases={j: 2 + j for j in range(2 * n)},
            compiler_params=pltpu.CompilerParams(has_side_effects=_EFFECT),
        )(*thru, send_sems, recv_sems, *after)
        return [send_sems, recv_sems] + list(out[2:-1]) + [out[0], out[1]], out[-1]

    def wait(self, state, after):
        n = self.n
        send_sems, recv_sems = state[0], state[1]
        thru = state[2:2 + 2 * n]
        relay_sems = list(state[2 + 2 * n:])
        assert len(relay_sems) == (2 if self.relayed else 0)
        after = list(after) if isinstance(after, (list, tuple)) else [after]

        def body(*refs):
            src, land = refs[:n], refs[n:2 * n]
            send_a, recv_a = refs[2 * n], refs[2 * n + 1]
            for p in self.direct:
                for w in range(self.ncls):
                    copy = self._span_copy(src, land, w, send_a.at[self._sem(p, w)], recv_a.at[self._sem(p, w)], p)
                    copy.wait_send()
                    if not (self.relayed and p in self.RELAYED):
                        copy.wait_recv()
            if self.relayed:
                send_b, recv_b = refs[2 * n + 2], refs[2 * n + 3]
                for j in range(len(self.RELAYED)):
                    for w in range(self.ncls):
                        copy = self._span_copy(src, land, w, send_b.at[j * self.ncls + w],
                                               recv_b.at[j * self.ncls + w], 1)
                        copy.wait_send()
                        copy.wait_recv()

        out = pl.pallas_call(
            body, name=self.name + "_wait",
            in_specs=[_HBM] * (2 * n) + [_SEM] * (2 + len(relay_sems)) + [_ANY] * len(after),
            out_shape=[pltpu.HBM(a.shape, a.dtype) for a in thru], out_specs=[_HBM] * (2 * n),
            input_output_aliases={j: j for j in range(2 * n)},
            compiler_params=pltpu.CompilerParams(has_side_effects=_EFFECT),
        )(*thru, send_sems, recv_sems, *relay_sems, *after)
        return out[:n], out[n:]

    def place(self, lands, srcs):
        n = self.n
        assert not self.scatter

        def body(*refs):
            src, land = refs[n:2 * n], refs[2 * n:3 * n]
            bufs, sems = refs[3 * n:4 * n], refs[-1]
            me = _my_index()
            loads = [pltpu.make_async_copy(src[k], bufs[k], sems.at[k]) for k in range(n)]
            stores = [pltpu.make_async_copy(bufs[k], land[k].at[self._block(k, me), :], sems.at[k]) for k in range(n)]
            for cp in loads:
                cp.start()
            for k in range(n):
                loads[k].wait()
                stores[k].start()
            for cp in stores:
                cp.wait()

        return pl.pallas_call(
            body, name=self.name + "_place", in_specs=[_ANY] * (2 * n), out_specs=[_ANY] * n,
            out_shape=[jax.ShapeDtypeStruct(a.shape, a.dtype) for a in lands],
            input_output_aliases={j: j for j in range(n)},
            scratch_shapes=[pltpu.VMEM(s.shape, s.dtype) for s in srcs] + [pltpu.SemaphoreType.DMA((n,))],
        )(*lands, *srcs)


def sum_blocks(landed, full, name):
    _, cnt, cols = landed.shape

    def body(land_ref, full_ref, o_ref, own_ref, sem):
        me = _my_index()
        own = pltpu.make_async_copy(full_ref.at[pl.ds(pl.multiple_of(me * cnt, 16), cnt), :], own_ref, sem)
        own.start()
        acc = land_ref[me ^ 1].astype(F32)
        for p in range(2, N_DEV):
            acc = acc + land_ref[me ^ p].astype(F32)
        own.wait()
        o_ref[...] = acc + own_ref[...].astype(F32)

    return pl.pallas_call(
        body, name=name, in_specs=[_VMEM, _ANY], out_specs=_VMEM,
        out_shape=jax.ShapeDtypeStruct((cnt, cols), F32),
        scratch_shapes=[pltpu.VMEM((cnt, cols), landed.dtype), pltpu.SemaphoreType.DMA],
        compiler_params=_params(),
    )(landed, full)


def sum_slots(slots, name):
    _, rows, cols = slots.shape
    tr = rows
    if rows > 512:
        for cand in (256, 128, 64, 32, 16, 8):
            if rows % cand == 0:
                tr = cand
                break

    def body(s_ref, o_ref):
        acc = s_ref[0].astype(F32)
        for j in range(1, N_DEV):
            acc = acc + s_ref[j].astype(F32)
        o_ref[...] = acc

    return pl.pallas_call(
        body, name=name, grid=(rows // tr,),
        in_specs=[pl.BlockSpec((N_DEV, tr, cols), lambda i: (0, i, 0))], out_specs=_row_spec(tr, cols),
        out_shape=jax.ShapeDtypeStruct((rows, cols), F32), compiler_params=_params(("arbitrary",)),
    )(slots)


BIG_T = ("ffn1_w_gate", "ffn1_w_up", "w_in", "ffn2_w_gate", "ffn2_w_up")
BIG_N = ("ffn1_w_down", "w_out", "ffn2_w_down")
HALF_T = ("w_attn_proj", "w_glu_v", "w_glu_g")
SMALL = ("ffn1_norm", "mix_norm", "attn_sinks", "ssm_a_re", "ssm_a_im", "ssm_log_dt", "ssm_b_re", "ssm_b_im",
         "ssm_c_re", "ssm_c_im", "ssm_d", "ffn2_norm", "final_norm")
PARTS = {"ffn1": ("ffn1_w_gate", "ffn1_w_up", "ffn1_w_down"),
         "mix": ("w_in", "w_out", "w_attn_proj", "w_glu_v", "w_glu_g"),
         "ffn2": ("ffn2_w_gate", "ffn2_w_up", "ffn2_w_down")}


def _to_rows(name, a):
    return a if name in BIG_N else jnp.swapaxes(a, -1, -2)


def local_step(x, tgt, get_weights, put_grads, small):
    seq, d = x.shape
    t = PAD_FRONT + N_META + seq
    cos_t, sin_t = rope_tables(t)
    row = lambda a: a.reshape(1, -1)
    tables = []
    for i in range(DEPTH):
        b_re_t = jnp.swapaxes(small["ssm_b_re"][i], 1, 2)
        b_im_t = jnp.swapaxes(small["ssm_b_im"][i], 1, 2)
        lam_re, lam_im, bbar_re, bbar_im = ssm_prep(small["ssm_a_re"][i], small["ssm_a_im"][i],
                                                    small["ssm_log_dt"][i].reshape(-1, 1), b_re_t, b_im_t, f"ssm_prep_{i}")
        tables.append(((b_re_t, b_im_t),
                       (row(lam_re), row(lam_im), _block_diag_b(bbar_re).astype(BF16), _block_diag_b(bbar_im).astype(BF16),
                        _block_diag_c(small["ssm_c_re"][i]).astype(BF16), _block_diag_c(small["ssm_c_im"][i]).astype(BF16),
                        row(small["ssm_d"][i]))))
    early = [cos_t, sin_t] + [a for _, tab in tables for a in tab[2:6]]
    saved = []
    h = None
    for i in range(DEPTH):
        s = {}
        w = dict(get_weights(i, "ffn1", early if i == 0 else h))
        if i == 0:
            h = jnp.concatenate([jnp.zeros((PAD_FRONT, d), F32), w["meta_tokens"], x], axis=0)
        s["h0"] = h
        h, s["n1"], s["acts1"] = ffn_fwd(h, row(small["ffn1_norm"][i]), w["ffn1_w_gate"], w["ffn1_w_up"],
                                               w["ffn1_w_down"], f"ffn1_fwd_{i}")
        s["h1"] = h
        w.update(get_weights(i, "mix", h))
        s["n2"], s["qkv"], s["u"], s["gates"] = win_fwd(h, row(small["mix_norm"][i]), w["w_in"], cos_t, sin_t,
                                                        f"win_fwd_{i}")
        s["b_t"], s["ssm"] = tables[i]
        s["yg"], s["h_re"], s["h_im"] = ssm_fwd(s["u"], *s["ssm"], f"ssm_fwd_{i}")
        s["o"] = attn_fwd(s["qkv"], row(small["attn_sinks"][i]), f"attn_fwd_{i}")
        h, s["merged"], s["att"], s["sv"], s["sg"] = merge_fwd(
            h, s["o"], s["yg"], s["gates"], w["w_attn_proj"], w["w_glu_v"], w["w_glu_g"], w["w_out"],
            f"merge_fwd_{i}")
        s["h2"] = h
        w.update(get_weights(i, "ffn2", h))
        h, s["n3"], s["acts3"] = ffn_fwd(h, row(small["ffn2_norm"][i]), w["ffn2_w_gate"], w["ffn2_w_up"],
                                               w["ffn2_w_down"], f"ffn2_fwd_{i}")
        s["w"] = w
        saved.append(s)

    loss, dh, d_final = head_fwd_bwd(h, row(small["final_norm"]), tgt)
    gs = {k: [None] * DEPTH for k in SMALL if k != "final_norm"}
    dep = loss
    for i in reversed(range(DEPTH)):
        s = saved[i]
        w = s["w"]
        dh, da, db, sact, dhb, dg = ffn_bwd(dh, s["h2"], row(small["ffn2_norm"][i]), s["acts3"], w["ffn2_w_gate"],
                                            w["ffn2_w_up"], w["ffn2_w_down"], dep, f"ffn2_bwd_{i}")
        gs["ffn2_norm"][i] = dg[0]
        dep = put_grads(i, "ffn2", {"ffn2_w_gate": tn_matmul(da, s["n3"], f"ffn2_dwg_{i}"),
                                    "ffn2_w_up": tn_matmul(db, s["n3"], f"ffn2_dwu_{i}"),
                                    "ffn2_w_down": tn_matmul(sact, dhb, f"ffn2_dwd_{i}")})

        dgates, datt, dsv, dsg, do, dyg, dhb = merge_bwd(dh, s["gates"], s["att"], s["sv"], s["sg"], w["w_attn_proj"],
                                                         w["w_glu_v"], w["w_glu_g"], w["w_out"], dep, f"merge_bwd_{i}")
        gmix = {"w_out": tn_matmul(s["merged"], dhb, f"dwout_{i}"),
                "w_attn_proj": tn_matmul(datt, s["o"], f"dwap_{i}"),
                "w_glu_v": tn_matmul(dsv, s["yg"], f"dwv_{i}"),
                "w_glu_g": tn_matmul(dsg, s["yg"], f"dwgg_{i}")}
        dqkv, dsink = attn_bwd(s["qkv"], do, row(small["attn_sinks"][i]), cos_t, sin_t, f"attn_bwd_{i}")
        gs["attn_sinks"][i] = dsink[:, 0]
        du, dl_re, dl_im, dbb_re, dbb_im, dcc_re, dcc_im, dd = ssm_bwd(dyg, s["u"], s["h_re"], s["h_im"], *s["ssm"],
                                                                      f"ssm_bwd_{i}")
        fold = lambda a: jnp.sum(a, axis=0).reshape(SSM_GROUPS, SSM_STATE)
        da_re, da_im, dldt, db_re_t, db_im_t = ssm_prep_bwd(
            small["ssm_a_re"][i], small["ssm_a_im"][i], small["ssm_log_dt"][i].reshape(-1, 1), *s["b_t"],
            fold(dl_re), fold(dl_im), _diag_of_b(dbb_re), _diag_of_b(dbb_im), f"ssm_prep_bwd_{i}")
        gs["ssm_a_re"][i], gs["ssm_a_im"][i], gs["ssm_log_dt"][i] = da_re, da_im, dldt[:, 0]
        gs["ssm_b_re"][i], gs["ssm_b_im"][i] = jnp.swapaxes(db_re_t, 1, 2), jnp.swapaxes(db_im_t, 1, 2)
        gs["ssm_c_re"][i], gs["ssm_c_im"][i] = _diag_of_c(dcc_re), _diag_of_c(dcc_im)
        gs["ssm_d"][i] = dd[0]
        gmix["w_in"] = tn_matmul([dqkv, du, dgates], s["n2"], f"dwin_{i}")
        dep = put_grads(i, "mix", gmix)
        dh, dg = win_bwd(dh, s["h1"], row(small["mix_norm"][i]), dqkv, du, dgates, w["w_in"], dep, f"win_bwd_{i}")
        gs["mix_norm"][i] = dg[0]

        dh, da, db, sact, dhb, dg = ffn_bwd(dh, s["h0"], row(small["ffn1_norm"][i]), s["acts1"], w["ffn1_w_gate"],
                                            w["ffn1_w_up"], w["ffn1_w_down"], dep, f"ffn1_bwd_{i}")
        gs["ffn1_norm"][i] = dg[0]
        if i > 0:
            dep = put_grads(i, "ffn1", {"ffn1_w_gate": tn_matmul(da, s["n1"], f"ffn1_dwg_{i}"),
                                        "ffn1_w_up": tn_matmul(db, s["n1"], f"ffn1_dwu_{i}"),
                                        "ffn1_w_down": tn_matmul(sact, dhb, f"ffn1_dwd_{i}")})
        else:
            for k, xa, ya in (("ffn1_w_down", sact, dhb), ("ffn1_w_gate", da, s["n1"]), ("ffn1_w_up", db, s["n1"])):
                dep = put_grads(i, "ffn1", {k: tn_matmul(xa, ya, f"d_{k}_{i}", dep)})

    gs = {k: jnp.stack(v) for k, v in gs.items()}
    gs["final_norm"] = d_final[0]
    return loss[0, 0], dh[PAD_FRONT + N_META:], dh[PAD_FRONT:PAD_FRONT + N_META], gs, dep


def _pack_rows(arrays, cols):
    flat = jnp.concatenate([a.reshape(-1) for a in arrays])
    rows = -(-flat.shape[0] // cols)
    rows = -(-rows // 16) * 16
    return jnp.pad(flat, (0, rows * cols - flat.shape[0])).reshape(rows, cols)


def _unpack_rows(packed, shapes):
    flat = packed.reshape(-1)
    out, off = [], 0
    for shp in shapes:
        n = math.prod(shp)
        out.append(flat[off:off + n].reshape(shp))
        off += n
    return out


def kernel(x, meta_tokens, ffn1_norm, ffn1_w_gate, ffn1_w_up, ffn1_w_down, mix_norm, w_in, attn_sinks, ssm_a_re, ssm_a_im, ssm_log_dt, ssm_b_re, ssm_b_im, ssm_c_re, ssm_c_im, ssm_d, w_attn_proj, w_glu_v, w_glu_g, w_out, ffn2_norm, ffn2_w_gate, ffn2_w_up, ffn2_w_down, final_norm, loss_target, m_meta_tokens, m_ffn1_norm, m_ffn1_w_gate, m_ffn1_w_up, m_ffn1_w_down, m_mix_norm, m_w_in, m_attn_sinks, m_ssm_a_re, m_ssm_a_im, m_ssm_log_dt, m_ssm_b_re, m_ssm_b_im, m_ssm_c_re, m_ssm_c_im, m_ssm_d, m_w_attn_proj, m_w_glu_v, m_w_glu_g, m_w_out, m_ffn2_norm, m_ffn2_w_gate, m_ffn2_w_up, m_ffn2_w_down, m_final_norm, v_meta_tokens, v_ffn1_norm, v_ffn1_w_gate, v_ffn1_w_up, v_ffn1_w_down, v_mix_norm, v_w_in, v_attn_sinks, v_ssm_a_re, v_ssm_a_im, v_ssm_log_dt, v_ssm_b_re, v_ssm_b_im, v_ssm_c_re, v_ssm_c_im, v_ssm_d, v_w_attn_proj, v_w_glu_v, v_w_glu_g, v_w_out, v_ffn2_norm, v_ffn2_w_gate, v_ffn2_w_up, v_ffn2_w_down, v_final_norm):
    names = ("meta_tokens", "ffn1_norm", "ffn1_w_gate", "ffn1_w_up", "ffn1_w_down", "mix_norm", "w_in", "attn_sinks",
             "ssm_a_re", "ssm_a_im", "ssm_log_dt", "ssm_b_re", "ssm_b_im", "ssm_c_re", "ssm_c_im", "ssm_d",
             "w_attn_proj", "w_glu_v", "w_glu_g", "w_out", "ffn2_norm", "ffn2_w_gate", "ffn2_w_up", "ffn2_w_down",
             "final_norm")
    weights = dict(zip(names, (meta_tokens, ffn1_norm, ffn1_w_gate, ffn1_w_up, ffn1_w_down, mix_norm, w_in, attn_sinks, ssm_a_re, ssm_a_im, ssm_log_dt, ssm_b_re, ssm_b_im, ssm_c_re, ssm_c_im, ssm_d, w_attn_proj, w_glu_v, w_glu_g, w_out, ffn2_norm, ffn2_w_gate, ffn2_w_up, ffn2_w_down, final_norm)))
    moments_m = dict(zip(names, (m_meta_tokens, m_ffn1_norm, m_ffn1_w_gate, m_ffn1_w_up, m_ffn1_w_down, m_mix_norm, m_w_in, m_attn_sinks, m_ssm_a_re, m_ssm_a_im, m_ssm_log_dt, m_ssm_b_re, m_ssm_b_im, m_ssm_c_re, m_ssm_c_im, m_ssm_d, m_w_attn_proj, m_w_glu_v, m_w_glu_g, m_w_out, m_ffn2_norm, m_ffn2_w_gate, m_ffn2_w_up, m_ffn2_w_down, m_final_norm)))
    moments_v = dict(zip(names, (v_meta_tokens, v_ffn1_norm, v_ffn1_w_gate, v_ffn1_w_up, v_ffn1_w_down, v_mix_norm, v_w_in, v_attn_sinks, v_ssm_a_re, v_ssm_a_im, v_ssm_log_dt, v_ssm_b_re, v_ssm_b_im, v_ssm_c_re, v_ssm_c_im, v_ssm_d, v_w_attn_proj, v_w_glu_v, v_w_glu_g, v_w_out, v_ffn2_norm, v_ffn2_w_gate, v_ffn2_w_up, v_ffn2_w_down, v_final_norm)))
    me = _my_index()

    order = [(i, part) for i in range(DEPTH) for part in PARTS]
    gathers = {}
    token = jnp.zeros((8, 128), F32)
    for i, part in order:
        shards = [_to_rows(k, weights[k][i]).astype(BF16) for k in PARTS[part]]
        if (i, part) == order[0]:
            shards.append(meta_tokens)
        ex = Exchange(shards, False, f"gather_{part}_{i}", relay=True)
        state, token = ex.start(shards, token)
        gathers[i, part] = [ex, state, False]
    all_started = token

    def relay(group, after):
        ex, state, relayed = gathers[group]
        if relayed:
            return []
        new_state, relay_token = ex.relay(state, after)
        gathers[group][1:] = [new_state, True]
        return [relay_token]

    def get_weights(i, part, after):
        g = order.index((i, part))
        after = [all_started] + list(after) if g == 0 else [after]
        tokens = relay(order[g], after)
        if g >= 2 and g + 1 < len(order):
            tokens += relay(order[g + 1], after)
        ex, state, _ = gathers[i, part]
        shards, lands = ex.wait(state, after + tokens)
        fulls = ex.place(lands, shards)
        got = dict(zip(PARTS[part], fulls))
        if (i, part) == (0, "ffn1"):
            got["meta_tokens"] = jnp.swapaxes(fulls[-1].reshape(N_DEV, N_META, 128), 0, 1).reshape(N_META, D_MODEL)
        return got

    scatters = []

    def put_grads(i, part, gdict):
        ks = list(gdict)
        srcs = [gdict[k] for k in ks]
        ex = Exchange(srcs, True, f"scatter_{part if len(ks) > 1 else ks[0]}_{i}")
        state, tok = ex.start(srcs, all_started)
        scatters.append((i, ks, ex, state))
        return tok

    small = {k: weights[k] for k in SMALL}
    loss, dx, dmeta, gs, last_started = local_step(x[0], loss_target[0], get_weights, put_grads, small)

    grads, deltas, new_m, new_v = {}, {}, {}, {}
    small_list = [loss.reshape(1), dmeta] + [gs[k] for k in SMALL]
    packed = _pack_rows(small_list, D_MODEL)
    small_ex = Exchange([packed], False, "gather_small")
    small_state, after = small_ex.start([packed], last_started)

    updated = {}
    me_index = jnp.reshape(me, (1,)).astype(jnp.int32)
    for i, ks, ex, state in scatters:
        partials, lands = ex.wait(state, after)
        for k, partial, slots in zip(ks, partials, lands):
            updated[k] = sum_adamw_layer(me_index, slots, partial, _to_rows(k, weights[k]), _to_rows(k, moments_m[k]),
                                         _to_rows(k, moments_v[k]), i, updated.get(k), f"adamw_{k}_{i}")
            after = updated[k][0]
    for k, outs in updated.items():
        grads[k], deltas[k], new_m[k], new_v[k] = [_to_rows(k, a) for a in outs]

    packed_own, packed_all = small_ex.wait(small_state, after)
    (packed_all,) = small_ex.place(packed_all, packed_own)
    total = sum_slots(packed_all.reshape(N_DEV, packed.shape[0], D_MODEL), "sum_small")
    pieces = _unpack_rows(total, [a.shape for a in small_list])
    loss_out = pieces[0][0]
    grads["meta_tokens"] = lax.dynamic_slice_in_dim(pieces[1], me * 128, 128, axis=1)
    for k, p in zip(SMALL, pieces[2:]):
        grads[k] = p
    for k in ("meta_tokens",) + SMALL:
        deltas[k], new_m[k], new_v[k] = adamw(weights[k], grads[k], moments_m[k], moments_v[k], f"adamw_{k}",
                                              minor_swap=k in ("ssm_b_re", "ssm_b_im"))
    return (loss_out, dx[None], *[grads[k] for k in names], *[deltas[k] for k in names],
            *[new_m[k] for k in names], *[new_v[k] for k in names])
```

```python
import math

import jax
import jax.numpy as jnp
from jax import lax
from jax.experimental import pallas as pl
from jax.experimental.pallas import tpu as pltpu

F32 = jnp.float32
BF16 = jnp.bfloat16

D_MODEL = 1024
DEPTH = 2
N_META = 16
HEAD_DIM = 64
N_Q_HEADS = 8
ATTN_WIDTH = 512
KV_WIDTH = 128
QKV_WIDTH = ATTN_WIDTH + 2 * KV_WIDTH
WINDOW = 128
BLK = 128
ROPE_THETA = 500000.0
ROT_DIM = 16
SSM_WIDTH = 512
SSM_GROUP = 16
SSM_GROUPS = 32
SSM_STATE = 64
STATE_WIDTH = SSM_GROUPS * SSM_STATE
D_FF = 2816
IN_WIDTH = 3328
EPS = 1e-6
NEG_INF = -1e30
PAD_FRONT = (-N_META) % BLK
N_DEV = 8

ADAM_LR = 0.001
ADAM_B1 = 0.9
ADAM_B2 = 0.999
ADAM_EPS = 1e-08
ADAM_WD = 0.01
ADAM_STEP = 10

VMEM_LIMIT = 56 * 1024 * 1024
TOKEN_TILE = 384
_VMEM = pl.BlockSpec(memory_space=pltpu.VMEM)
_SMEM = pl.BlockSpec(memory_space=pltpu.SMEM)
_ANY = pl.BlockSpec(memory_space=pl.ANY)
MESH = pl.DeviceIdType.MESH


def _params(sem=None):
    return pltpu.CompilerParams(dimension_semantics=sem, vmem_limit_bytes=VMEM_LIMIT)


def _nt(a, b):
    return lax.dot_general(a, b, (((1,), (1,)), ((), ())), preferred_element_type=F32)


def _nn(a, b):
    return jnp.dot(a, b, preferred_element_type=F32)


def _tn(a, b):
    return lax.dot_general(a, b, (((0,), (0,)), ((), ())), preferred_element_type=F32)


def _row_spec(tm, width):
    return pl.BlockSpec((tm, width), lambda i: (i, 0))


def _acc_spec(shape):
    return pl.BlockSpec(shape, lambda i: (0,) * len(shape))


def _rms_stats(x):
    r = lax.rsqrt(jnp.mean(x * x, axis=-1, keepdims=True) + EPS)
    return x * r, r


def _rms_bwd(dn, xh, r, g):
    dg = jnp.sum(dn * xh, axis=0, keepdims=True)
    dxh = dn * g
    dx = r * (dxh - xh * jnp.mean(dxh * xh, axis=-1, keepdims=True))
    return dx, dg


def ffn_fwd(h, g, wg_t, wu_t, wd, name):
    t, d = h.shape
    f = wd.shape[0]
    tm = TOKEN_TILE

    def body(h_ref, g_ref, wg_ref, wu_ref, wd_ref, ho_ref, n_ref, sl_ref, p_ref, s_ref):
        x = h_ref[...]
        xh, _ = _rms_stats(x)
        n = (xh * g_ref[...]).astype(BF16)
        n_ref[...] = n
        a = _nt(n, wg_ref[...])
        b = _nt(n, wu_ref[...])
        sig = jax.nn.sigmoid(a)
        sl = a * sig
        sl_ref[...] = sl.astype(BF16)
        p_ref[...] = (b * (sig + sl * (1.0 - sig))).astype(BF16)
        s = (sl * b).astype(BF16)
        s_ref[...] = s
        ho_ref[...] = x + 0.5 * _nn(s, wd_ref[...])

    ho, n, sl, p, s = pl.pallas_call(
        body, name=name, grid=(t // tm,),
        in_specs=[_row_spec(tm, d), _acc_spec((1, d)), _VMEM, _VMEM, _VMEM],
        out_specs=[_row_spec(tm, d), _row_spec(tm, d), _row_spec(tm, f), _row_spec(tm, f), _row_spec(tm, f)],
        out_shape=[jax.ShapeDtypeStruct((t, d), F32), jax.ShapeDtypeStruct((t, d), BF16),
                   jax.ShapeDtypeStruct((t, f), BF16), jax.ShapeDtypeStruct((t, f), BF16),
                   jax.ShapeDtypeStruct((t, f), BF16)],
        compiler_params=_params(("arbitrary",)),
    )(h, g, wg_t, wu_t, wd)
    return ho, n, (sl, p, s)


def ffn_bwd(dh, h, g, acts, wg_t, wu_t, wd, dep, name):
    t, d = h.shape
    f = wd.shape[0]
    tm = TOKEN_TILE
    sl, p, s = acts

    def hidden_body(dh_ref, sl_ref, p_ref, wd_ref, dep_ref, da_ref, db_ref, dhb_ref):
        dhb = (0.5 * dh_ref[...]).astype(BF16)
        dhb_ref[...] = dhb
        ds = _nt(dhb, wd_ref[...])
        da_ref[...] = (ds * p_ref[...].astype(F32)).astype(BF16)
        db_ref[...] = (ds * sl_ref[...].astype(F32)).astype(BF16)

    da, db, dhb = pl.pallas_call(
        hidden_body, name=name + "_h", grid=(t // tm,),
        in_specs=[_row_spec(tm, d), _row_spec(tm, f), _row_spec(tm, f), _VMEM, _ANY],
        out_specs=[_row_spec(tm, f), _row_spec(tm, f), _row_spec(tm, d)],
        out_shape=[jax.ShapeDtypeStruct((t, f), BF16), jax.ShapeDtypeStruct((t, f), BF16),
                   jax.ShapeDtypeStruct((t, d), BF16)],
        compiler_params=_params(("arbitrary",)),
    )(dh, sl, p, wd, dep)

    def input_body(dh_ref, h_ref, g_ref, da_ref, db_ref, wg_ref, wu_ref, dhi_ref, dg_ref):
        dn = _nn(da_ref[...], wg_ref[...]) + _nn(db_ref[...], wu_ref[...])
        xh, r = _rms_stats(h_ref[...])
        dx, dg = _rms_bwd(dn, xh, r, g_ref[...])
        dhi_ref[...] = dh_ref[...] + dx

        @pl.when(pl.program_id(0) == 0)
        def _():
            dg_ref[...] = jnp.zeros_like(dg_ref)

        dg_ref[...] += dg

    dhi, dg = pl.pallas_call(
        input_body, name=name + "_x", grid=(t // tm,),
        in_specs=[_row_spec(tm, d), _row_spec(tm, d), _acc_spec((1, d)), _row_spec(tm, f), _row_spec(tm, f),
                  _VMEM, _VMEM],
        out_specs=[_row_spec(tm, d), _acc_spec((1, d))],
        out_shape=[jax.ShapeDtypeStruct((t, d), F32), jax.ShapeDtypeStruct((1, d), F32)],
        compiler_params=_params(("arbitrary",)),
    )(dh, h, g, da, db, wg_t, wu_t)
    return dhi, da, db, s, dhb, dg


DW_TILE = 256


def tn_matmul(x, y, name, dep=None):
    xs = list(x) if isinstance(x, (list, tuple)) else [x]
    t = xs[0].shape[0]
    n = y.shape[1]
    bm = DW_TILE
    tiles = [a.shape[1] // bm for a in xs]
    offs = [sum(tiles[:k]) for k in range(len(xs))]
    deps = [] if dep is None else [dep]

    def body(*refs):
        y_ref, o_ref = refs[len(xs)], refs[-1]
        i = pl.program_id(0)
        for k in range(len(xs)):
            @pl.when((i >= offs[k]) & (i < offs[k] + tiles[k]))
            def _(k=k):
                o_ref[...] = _tn(refs[k][...], y_ref[...]).astype(BF16)

    def x_spec(k):
        return pl.BlockSpec((t, bm), lambda i: (0, jnp.clip(i - offs[k], 0, tiles[k] - 1)))

    return pl.pallas_call(
        body, name=name, grid=(sum(tiles),),
        in_specs=[x_spec(k) for k in range(len(xs))] + [_VMEM] + [_ANY] * len(deps),
        out_specs=pl.BlockSpec((bm, n), lambda i: (i, 0)),
        out_shape=jax.ShapeDtypeStruct((sum(tiles) * bm, n), BF16),
        compiler_params=_params(("arbitrary",)),
    )(*xs, y, *deps)


def head_fwd_bwd(h, g, tgt):
    t, d = h.shape

    def body(h_ref, g_ref, t_ref, loss_ref, dh_ref, dg_ref):
        i = pl.program_id(0)
        xh, r = _rms_stats(h_ref[...])
        gv = g_ref[...]
        valid = (i > 0).astype(F32)
        e = (xh * gv - t_ref[...]) * valid
        dx, dg = _rms_bwd(e * (1.0 / d), xh, r, gv)
        dh_ref[...] = dx

        @pl.when(i == 0)
        def _():
            dg_ref[...] = jnp.zeros_like(dg_ref)
            loss_ref[...] = jnp.zeros_like(loss_ref)

        dg_ref[...] += dg
        loss_ref[...] += jnp.sum(e * e) * (0.5 / d)

    return pl.pallas_call(
        body, name="head", grid=(t // BLK,),
        in_specs=[_row_spec(BLK, d), _acc_spec((1, d)),
                  pl.BlockSpec((BLK, d), lambda i: (jnp.maximum(i - 1, 0), 0))],
        out_specs=[_acc_spec((1, 128)), _row_spec(BLK, d), _acc_spec((1, d))],
        out_shape=[jax.ShapeDtypeStruct((1, 128), F32), jax.ShapeDtypeStruct((t, d), F32),
                   jax.ShapeDtypeStruct((1, d), F32)],
        compiler_params=_params(("arbitrary",)),
    )(h, g, tgt)


def rope_tables(t):
    pos = jnp.arange(t, dtype=F32) - PAD_FRONT
    inv_freq = ROPE_THETA ** (-jnp.arange(0, ROT_DIM, 2, dtype=F32) / ROT_DIM)
    ang = pos[:, None] * inv_freq[None, :]
    cos, sin = jnp.cos(ang), jnp.sin(ang)
    ones = jnp.ones((t, HEAD_DIM - ROT_DIM), F32)
    cos_h = jnp.concatenate([cos, cos, ones], axis=1)
    sin_h = jnp.concatenate([-sin, sin, 0.0 * ones], axis=1)
    return jnp.concatenate([cos_h, cos_h], axis=1), jnp.concatenate([sin_h, sin_h], axis=1)


def _swap_halves(x):
    n = x.shape[1]
    lane = lax.broadcasted_iota(jnp.int32, x.shape, 1)
    return jnp.where(lane % HEAD_DIM < ROT_DIM // 2, pltpu.roll(x, n - ROT_DIM // 2, 1), pltpu.roll(x, ROT_DIM // 2, 1))


def _rope(x, cos_t, sin_t, sign):
    return x * cos_t + sign * (_swap_halves(x) * sin_t)


def win_fwd(h, g, win_t, cos_t, sin_t, name):
    t, d = h.shape
    tm = TOKEN_TILE

    def body(h_ref, g_ref, w_ref, c_ref, s_ref, n_ref, qkv_ref, u_ref, gates_ref):
        xh, _ = _rms_stats(h_ref[...])
        n = (xh * g_ref[...]).astype(BF16)
        n_ref[...] = n
        z = _nt(n, w_ref[...])
        c, s = c_ref[...], s_ref[...]
        for j in range((ATTN_WIDTH + KV_WIDTH) // 128):
            qkv_ref[:, j * 128:(j + 1) * 128] = _rope(z[:, j * 128:(j + 1) * 128], c, s, 1.0).astype(BF16)
        qkv_ref[:, ATTN_WIDTH + KV_WIDTH:QKV_WIDTH] = z[:, ATTN_WIDTH + KV_WIDTH:QKV_WIDTH].astype(BF16)
        for j in range(N_CHUNK):
            u_ref[j] = z[:, QKV_WIDTH + j * U_CHUNK:QKV_WIDTH + (j + 1) * U_CHUNK]
        gates_ref[...] = z[:, QKV_WIDTH + SSM_WIDTH:].astype(BF16)

    return pl.pallas_call(
        body, name=name, grid=(t // tm,),
        in_specs=[_row_spec(tm, d), _acc_spec((1, d)), _VMEM, _row_spec(tm, 128), _row_spec(tm, 128)],
        out_specs=[_row_spec(tm, d), _row_spec(tm, QKV_WIDTH), _chunked_spec(tm, lambda i: i), _row_spec(tm, 2 * d)],
        out_shape=[jax.ShapeDtypeStruct((t, d), BF16), jax.ShapeDtypeStruct((t, QKV_WIDTH), BF16),
                   jax.ShapeDtypeStruct((N_CHUNK, t, U_CHUNK), F32), jax.ShapeDtypeStruct((t, 2 * d), BF16)],
        compiler_params=_params(("arbitrary",)),
    )(h, g, win_t, cos_t, sin_t)


def win_bwd(dh, h, g, dqkv, du, dgates, win_t, dep, name):
    t, d = h.shape
    tm = TOKEN_TILE

    def body(dh_ref, h_ref, g_ref, dqkv_ref, du_ref, dgt_ref, w_ref, dep_ref, dhi_ref, dg_ref):
        dn = (_nn(dqkv_ref[...], w_ref[0:QKV_WIDTH, :])
              + _nn(du_ref[...], w_ref[QKV_WIDTH:QKV_WIDTH + SSM_WIDTH, :])
              + _nn(dgt_ref[...], w_ref[QKV_WIDTH + SSM_WIDTH:, :]))
        xh, r = _rms_stats(h_ref[...])
        dx, dg = _rms_bwd(dn, xh, r, g_ref[...])
        dhi_ref[...] = dh_ref[...] + dx

        @pl.when(pl.program_id(0) == 0)
        def _():
            dg_ref[...] = jnp.zeros_like(dg_ref)

        dg_ref[...] += dg

    return pl.pallas_call(
        body, name=name, grid=(t // tm,),
        in_specs=[_row_spec(tm, d), _row_spec(tm, d), _acc_spec((1, d)), _row_spec(tm, QKV_WIDTH),
                  _row_spec(tm, SSM_WIDTH), _row_spec(tm, 2 * d), _VMEM, _ANY],
        out_specs=[_row_spec(tm, d), _acc_spec((1, d))],
        out_shape=[jax.ShapeDtypeStruct((t, d), F32), jax.ShapeDtypeStruct((1, d), F32)],
        compiler_params=_params(("arbitrary",)),
    )(dh, h, g, dqkv, du, dgates, win_t, dep)


def _attn_mask(blk):
    q_pos = blk * BLK + lax.broadcasted_iota(jnp.int32, (BLK, 3 * BLK), 0) - PAD_FRONT
    col = lax.broadcasted_iota(jnp.int32, (BLK, 3 * BLK), 1)
    part = col // BLK
    k_pos = jnp.where(part == 0, col, (blk + part - 2) * BLK + (col - part * BLK)) - PAD_FRONT
    dist = q_pos - k_pos
    meta_ok = (part == 0) & (k_pos >= 0) & (dist >= 0)
    band_ok = (part > 0) & (k_pos >= N_META) & (dist >= 0) & (dist < WINDOW)
    return meta_ok | band_ok


def _head_halves(x128, kv):
    x = x128.astype(F32)
    lane = lax.broadcasted_iota(jnp.int32, x.shape, 1)
    swapped = pltpu.roll(x, HEAD_DIM, 1)
    lo, hi = (x, swapped) if kv == 0 else (swapped, x)
    return jnp.where(lane < HEAD_DIM, lo, 0.0).astype(BF16), jnp.where(lane >= HEAD_DIM, hi, 0.0).astype(BF16)


def _gather_keys(meta_ref, prev_ref, cur_ref, lo):
    return jnp.concatenate([meta_ref[:, lo:lo + 128], prev_ref[:, lo:lo + 128], cur_ref[:, lo:lo + 128]], axis=0)


def _pair_lanes(kv):
    return slice(2 * kv * 128, (2 * kv + 1) * 128), slice((2 * kv + 1) * 128, (2 * kv + 2) * 128)


def _stacked_sinks(sink_ref, head):
    row = lax.broadcasted_iota(jnp.int32, (2 * BLK, 1), 0)
    return jnp.where(row < BLK, sink_ref[0, head], sink_ref[0, head + 2])


def _softmax_with_sink(s, mask, sink):
    s = jnp.where(mask, s * (HEAD_DIM ** -0.5), NEG_INF)
    m = jnp.maximum(jnp.max(s, axis=-1, keepdims=True), sink)
    p = jnp.exp(s - m)
    p_sink = jnp.exp(sink - m)
    inv = 1.0 / (jnp.sum(p, axis=-1, keepdims=True) + p_sink)
    return p * inv, p_sink * inv


def attn_fwd(qkv, sinks, name):
    t = qkv.shape[0]
    nb = t // BLK

    def body(sink_ref, meta_ref, prev_ref, cur_ref, o_ref):
        blk = pl.program_id(0)
        mask = _attn_mask(blk)
        mask2 = jnp.concatenate([mask, mask], axis=0)
        k128 = _gather_keys(meta_ref, prev_ref, cur_ref, ATTN_WIDTH)
        v128 = _gather_keys(meta_ref, prev_ref, cur_ref, ATTN_WIDTH + KV_WIDTH)
        for kv in range(2):
            k_lo, k_hi = _head_halves(k128, kv)
            v_lo, v_hi = _head_halves(v128, kv)
            lanes0, lanes1 = _pair_lanes(kv)
            q2 = jnp.concatenate([cur_ref[:, lanes0], cur_ref[:, lanes1]], axis=0)
            p_a, _ = _softmax_with_sink(_nt(q2, k_lo), mask2, _stacked_sinks(sink_ref, 4 * kv))
            p_b, _ = _softmax_with_sink(_nt(q2, k_hi), mask2, _stacked_sinks(sink_ref, 4 * kv + 1))
            o2 = (_nn(p_a.astype(BF16), v_lo) + _nn(p_b.astype(BF16), v_hi)).astype(BF16)
            o_ref[:, lanes0] = o2[0:BLK]
            o_ref[:, lanes1] = o2[BLK:2 * BLK]

    blk_spec = lambda f: pl.BlockSpec((BLK, QKV_WIDTH), f)
    return pl.pallas_call(
        body, name=name, grid=(nb,),
        in_specs=[_SMEM, blk_spec(lambda i: (0, 0)), blk_spec(lambda i: (jnp.maximum(i - 1, 0), 0)),
                  blk_spec(lambda i: (i, 0))],
        out_specs=_row_spec(BLK, ATTN_WIDTH),
        out_shape=jax.ShapeDtypeStruct((t, ATTN_WIDTH), BF16),
        compiler_params=_params(("arbitrary",)),
    )(sinks, qkv, qkv, qkv)


def attn_bwd(qkv, do, sinks, cos_t, sin_t, name):
    t = qkv.shape[0]
    nb = t // BLK

    def body(sink_ref, meta_ref, prev_ref, cur_ref, do_ref, c_ref, s_ref, dqkv_ref, dsink_ref, carry_ref, macc_ref):
        step = pl.program_id(0)
        blk = nb - 1 - step

        @pl.when(step == 0)
        def _():
            dsink_ref[...] = jnp.zeros_like(dsink_ref)
            carry_ref[...] = jnp.zeros_like(carry_ref)
            macc_ref[...] = jnp.zeros_like(macc_ref)

        mask = _attn_mask(blk)
        mask2 = jnp.concatenate([mask, mask], axis=0)
        lane = lax.broadcasted_iota(jnp.int32, (3 * BLK, 128), 1)
        k128 = _gather_keys(meta_ref, prev_ref, cur_ref, ATTN_WIDTH)
        v128 = _gather_keys(meta_ref, prev_ref, cur_ref, ATTN_WIDTH + KV_WIDTH)
        cos_b, sin_b = c_ref[...], s_ref[...]
        dk_heads, dv_heads = [], []
        for kv in range(2):
            k_lo, k_hi = _head_halves(k128, kv)
            v_lo, v_hi = _head_halves(v128, kv)
            lanes0, lanes1 = _pair_lanes(kv)
            q2 = jnp.concatenate([cur_ref[:, lanes0], cur_ref[:, lanes1]], axis=0)
            do2 = jnp.concatenate([do_ref[:, lanes0], do_ref[:, lanes1]], axis=0)
            ds_half, p_half = [], []
            for half, (k_h, v_h) in enumerate(((k_lo, v_lo), (k_hi, v_hi))):
                head = 4 * kv + half
                p, p_sink = _softmax_with_sink(_nt(q2, k_h), mask2, _stacked_sinks(sink_ref, head))
                dp = _nt(do2, v_h)
                dsum = jnp.sum(p * dp, axis=-1, keepdims=True)
                ds_half.append((p * (dp - dsum) * (HEAD_DIM ** -0.5)).astype(BF16))
                p_half.append(p.astype(BF16))
                dsink = p_sink * dsum
                for part, h in ((0, head), (1, head + 2)):
                    total = -jnp.sum(dsink[part * BLK:(part + 1) * BLK], axis=0, keepdims=True)
                    dsink_ref[h:h + 1, :] += jnp.broadcast_to(total, (1, 128))
            dq2 = _nn(ds_half[0], k_lo) + _nn(ds_half[1], k_hi)
            dqkv_ref[:, lanes0] = _rope(dq2[0:BLK], cos_b, sin_b, -1.0).astype(BF16)
            dqkv_ref[:, lanes1] = _rope(dq2[BLK:2 * BLK], cos_b, sin_b, -1.0).astype(BF16)
            dk_acc = jnp.where(lane < HEAD_DIM, _tn(ds_half[0], q2), _tn(ds_half[1], q2))
            dv_acc = jnp.where(lane < HEAD_DIM, _tn(p_half[0], do2), _tn(p_half[1], do2))
            dk_heads.append(dk_acc + pltpu.roll(dk_acc, HEAD_DIM, 1))
            dv_heads.append(dv_acc + pltpu.roll(dv_acc, HEAD_DIM, 1))
        dkv = jnp.concatenate([jnp.where(lane < HEAD_DIM, dk_heads[0], dk_heads[1]),
                               jnp.where(lane < HEAD_DIM, dv_heads[0], dv_heads[1])], axis=1)
        macc_ref[...] += dkv[0:BLK]
        is_last = (blk == 0).astype(F32)
        mine = dkv[2 * BLK:3 * BLK] + carry_ref[...] + is_last * macc_ref[...]
        carry_ref[...] = dkv[BLK:2 * BLK]
        dqkv_ref[:, ATTN_WIDTH:ATTN_WIDTH + KV_WIDTH] = _rope(mine[:, 0:128], cos_b, sin_b, -1.0).astype(BF16)
        dqkv_ref[:, ATTN_WIDTH + KV_WIDTH:QKV_WIDTH] = mine[:, 128:256].astype(BF16)

    rev = lambda i: nb - 1 - i
    blk_spec = lambda f: pl.BlockSpec((BLK, QKV_WIDTH), f)
    return pl.pallas_call(
        body, name=name, grid=(nb,),
        in_specs=[_SMEM, blk_spec(lambda i: (0, 0)), blk_spec(lambda i: (jnp.maximum(rev(i) - 1, 0), 0)),
                  blk_spec(lambda i: (rev(i), 0)), pl.BlockSpec((BLK, ATTN_WIDTH), lambda i: (rev(i), 0)),
                  pl.BlockSpec((BLK, 128), lambda i: (rev(i), 0)), pl.BlockSpec((BLK, 128), lambda i: (rev(i), 0))],
        out_specs=[pl.BlockSpec((BLK, QKV_WIDTH), lambda i: (rev(i), 0)), _acc_spec((N_Q_HEADS, 128))],
        out_shape=[jax.ShapeDtypeStruct((t, QKV_WIDTH), BF16), jax.ShapeDtypeStruct((N_Q_HEADS, 128), F32)],
        scratch_shapes=[pltpu.VMEM((BLK, 256), F32), pltpu.VMEM((BLK, 256), F32)],
        compiler_params=_params(("arbitrary",)),
    )(sinks, qkv, qkv, qkv, do, cos_t, sin_t)


def _cmul(ar, ai, br, bi):
    return ar * br - ai * bi, ar * bi + ai * br


def ssm_prep(a_re, a_im, log_dt, b_re_t, b_im_t, name):
    def body(ar_ref, ai_ref, ldt_ref, br_ref, bi_ref, lr_ref, li_ref, bbr_ref, bbi_ref):
        ar, ai = ar_ref[...], ai_ref[...]
        dt = jnp.exp(ldt_ref[...])
        mag = jnp.exp(ar * dt)
        lr = mag * jnp.cos(ai * dt)
        li = mag * jnp.sin(ai * dt)
        den = ar * ar + ai * ai
        nr = lr - 1.0
        cr = ((nr * ar + li * ai) / den)[:, None, :]
        ci = ((li * ar - nr * ai) / den)[:, None, :]
        br, bi = br_ref[...], bi_ref[...]
        lr_ref[...] = lr
        li_ref[...] = li
        bbr_ref[...] = cr * br - ci * bi
        bbi_ref[...] = cr * bi + ci * br

    gp = jax.ShapeDtypeStruct(a_re.shape, F32)
    gcp = jax.ShapeDtypeStruct(b_re_t.shape, F32)
    return pl.pallas_call(body, name=name, out_shape=[gp, gp, gcp, gcp],
                          in_specs=[_VMEM] * 5, out_specs=[_VMEM] * 4)(a_re, a_im, log_dt, b_re_t, b_im_t)


def ssm_prep_bwd(a_re, a_im, log_dt, b_re_t, b_im_t, dl_re, dl_im, dbb_re, dbb_im, name):
    def body(ar_ref, ai_ref, ldt_ref, br_ref, bi_ref, dlr_ref, dli_ref, dbbr_ref, dbbi_ref,
             dar_ref, dai_ref, dldt_ref, dbr_ref, dbi_ref):
        ar, ai = ar_ref[...], ai_ref[...]
        dt = jnp.exp(ldt_ref[...])
        mag = jnp.exp(ar * dt)
        lr = mag * jnp.cos(ai * dt)
        li = mag * jnp.sin(ai * dt)
        den = ar * ar + ai * ai
        nr = lr - 1.0
        cr = (nr * ar + li * ai) / den
        ci = (li * ar - nr * ai) / den
        br, bi = br_ref[...], bi_ref[...]
        dbbr, dbbi = dbbr_ref[...], dbbi_ref[...]
        dbr_ref[...] = cr[:, None, :] * dbbr + ci[:, None, :] * dbbi
        dbi_ref[...] = cr[:, None, :] * dbbi - ci[:, None, :] * dbbr
        dcr = jnp.sum(br * dbbr + bi * dbbi, axis=1)
        dci = jnp.sum(br * dbbi - bi * dbbr, axis=1)
        d_num_r = dcr / den
        d_num_i = dci / den
        d_den = -(dcr * cr + dci * ci) / den
        d_lr = dlr_ref[...] + d_num_r * ar - d_num_i * ai
        d_li = dli_ref[...] + d_num_r * ai + d_num_i * ar
        d_ar = d_num_r * nr + d_num_i * li + d_den * 2.0 * ar
        d_ai = d_num_r * li - d_num_i * nr + d_den * 2.0 * ai
        d_mag = (d_lr * lr + d_li * li) / mag
        d_theta = d_li * lr - d_lr * li
        d_ardt = d_mag * mag
        dar_ref[...] = d_ar + d_ardt * dt
        dai_ref[...] = d_ai + d_theta * dt
        d_dt = jnp.sum(d_ardt * ar + d_theta * ai, axis=1, keepdims=True)
        dldt_ref[...] = d_dt * dt

    gp = jax.ShapeDtypeStruct(a_re.shape, F32)
    gcp = jax.ShapeDtypeStruct(b_re_t.shape, F32)
    return pl.pallas_call(body, name=name, out_shape=[gp, gp, jax.ShapeDtypeStruct(log_dt.shape, F32), gcp, gcp],
                          in_specs=[_VMEM] * 9, out_specs=[_VMEM] * 5,
                          )(a_re, a_im, log_dt, b_re_t, b_im_t, dl_re, dl_im, dbb_re, dbb_im)


N_CHUNK = 4
U_CHUNK = SSM_WIDTH // N_CHUNK
H_CHUNK = STATE_WIDTH // N_CHUNK
SUB = 8


def _block_diag_b(bb):
    x = bb.reshape(N_CHUNK, 8, SSM_GROUP, 1, SSM_STATE)
    same = (jnp.arange(8)[:, None] == jnp.arange(8)[None, :])[None, :, None, :, None]
    return jnp.where(same, x, 0.0).reshape(N_CHUNK, U_CHUNK, H_CHUNK)


def _block_diag_c(c):
    x = jnp.swapaxes(c.reshape(N_CHUNK, 8, SSM_GROUP, SSM_STATE), 2, 3)[:, :, :, None, :]
    same = (jnp.arange(8)[:, None] == jnp.arange(8)[None, :])[None, :, None, :, None]
    return jnp.where(same, x, 0.0).reshape(N_CHUNK, H_CHUNK, U_CHUNK)


def _diag_of_b(m):
    x = m.reshape(N_CHUNK, 8, SSM_GROUP, 8, SSM_STATE)
    return jnp.stack([x[:, g, :, g, :] for g in range(8)], axis=1).reshape(SSM_GROUPS, SSM_GROUP, SSM_STATE)


def _diag_of_c(m):
    x = m.reshape(N_CHUNK, 8, SSM_STATE, 8, SSM_GROUP)
    d = jnp.stack([x[:, g, :, g, :] for g in range(8)], axis=1)
    return jnp.swapaxes(d, 2, 3).reshape(SSM_GROUPS, SSM_GROUP, SSM_STATE)


def _lambda_tables(lr, li, reverse):
    p1 = (lr, li)
    p2 = _cmul(*p1, *p1)
    p4 = _cmul(*p2, *p2)
    rows = [p1]
    for _ in range(SUB - 1):
        rows.append(_cmul(*rows[-1], *p1))
    if reverse:
        rows = rows[::-1]
    return p1, p2, p4, (jnp.concatenate([r[0] for r in rows], axis=0), jnp.concatenate([r[1] for r in rows], axis=0))


def _scan8(xr, xi, pows, table, cr, ci, reverse):
    row = lax.broadcasted_iota(jnp.int32, xr.shape, 0)
    for d, (pr, pi) in zip((1, 2, 4), pows):
        if reverse:
            sr, si = pltpu.roll(xr, SUB - d, 0), pltpu.roll(xi, SUB - d, 0)
            keep = row < SUB - d
        else:
            sr, si = pltpu.roll(xr, d, 0), pltpu.roll(xi, d, 0)
            keep = row >= d
        sr = jnp.where(keep, sr, 0.0)
        si = jnp.where(keep, si, 0.0)
        xr, xi = xr + pr * sr - pi * si, xi + pr * si + pi * sr
    tr, ti = table
    return xr + tr * cr - ti * ci, xi + tr * ci + ti * cr


def _gelu_and_grad(y):
    k0 = math.sqrt(2.0 / math.pi)
    inner = k0 * (y + 0.044715 * y * y * y)
    th = jnp.tanh(inner)
    g = 0.5 * y * (1.0 + th)
    dg = 0.5 * (1.0 + th) + 0.5 * y * (1.0 - th * th) * k0 * (1.0 + 3.0 * 0.044715 * y * y)
    return g, dg


SCAN_TILE = TOKEN_TILE
SEG = SCAN_TILE // SUB
SCAN_LANES = 512


def _perm_matrix(to_segments):
    a = lax.broadcasted_iota(jnp.int32, (SCAN_TILE, SCAN_TILE), 0)
    b = lax.broadcasted_iota(jnp.int32, (SCAN_TILE, SCAN_TILE), 1)
    rho, time = (a, b) if to_segments else (b, a)
    return (time == (rho % SUB) * SEG + rho // SUB).astype(BF16)


def _chunked_spec(rows, block_of):
    return pl.BlockSpec((N_CHUNK, rows, U_CHUNK), lambda i: (0, block_of(i), 0))


def _load_segments(src_ref, dst_ref):
    for j in range(N_CHUNK):
        for r in range(SEG):
            dst_ref[r * SUB:(r + 1) * SUB, j * U_CHUNK:(j + 1) * U_CHUNK] = src_ref.at[j][pl.ds(r, SUB, stride=SEG), :]


def _power_table(lr, li, pr_ref, pi_ref):
    cur = (lr, li)
    for r in range(SEG):
        pr_ref[r * SUB:(r + 1) * SUB, :] = jnp.broadcast_to(cur[0], (SUB, STATE_WIDTH))
        pi_ref[r * SUB:(r + 1) * SUB, :] = jnp.broadcast_to(cur[1], (SUB, STATE_WIDTH))
        cur = _cmul(*cur, lr, li)


def _table_rows(ref, k, lanes):
    return ref[pl.ds(pl.multiple_of(k * SUB, SUB), SUB), lanes]


def _segment_scan(xr_ref, xi_ref, lanes, lam, table_row, cr_ref, ci_ref, reverse, extra=None):
    lr = jnp.broadcast_to(lam[0], (SUB, SCAN_LANES))
    li = jnp.broadcast_to(lam[1], (SUB, SCAN_LANES))
    row = lax.broadcasted_iota(jnp.int32, (SUB, SCAN_LANES), 0)

    def rows_of(k):
        r = SEG - 1 - k if reverse else k
        return pl.ds(pl.multiple_of(r * SUB, SUB), SUB)

    def first(k, st):
        sr, si = st
        rows = rows_of(k)
        nr = lr * sr - li * si + xr_ref[rows, lanes]
        ni = lr * si + li * sr + xi_ref[rows, lanes]
        xr_ref[rows, lanes] = nr
        xi_ref[rows, lanes] = ni
        return nr, ni

    zero = jnp.zeros((SUB, SCAN_LANES), F32)
    er, ei = lax.fori_loop(0, SEG, first, (zero, zero))
    l16 = table_row(SEG - 1)
    q1, q2, q4, tab = _lambda_tables(l16[0][0:1], l16[1][0:1], reverse)
    c_r, c_i = cr_ref[:, lanes], ci_ref[:, lanes]
    gr, gi = _scan8(er, ei, (q1, q2, q4), tab, c_r, c_i, reverse)
    if reverse:
        cin_r = jnp.where(row == SUB - 1, c_r, pltpu.roll(gr, SUB - 1, 0))
        cin_i = jnp.where(row == SUB - 1, c_i, pltpu.roll(gi, SUB - 1, 0))
        cr_ref[:, lanes] = gr[0:1]
        ci_ref[:, lanes] = gi[0:1]
    else:
        cin_r = jnp.where(row == 0, c_r, pltpu.roll(gr, 1, 0))
        cin_i = jnp.where(row == 0, c_i, pltpu.roll(gi, 1, 0))
        cr_ref[:, lanes] = gr[SUB - 1:SUB]
        ci_ref[:, lanes] = gi[SUB - 1:SUB]

    def second(k, carry):
        rows = rows_of(k)
        tr, ti = table_row(k)
        ar = xr_ref[rows, lanes] + tr * cin_r - ti * cin_i
        ai = xi_ref[rows, lanes] + tr * cin_i + ti * cin_r
        xr_ref[rows, lanes] = ar
        xi_ref[rows, lanes] = ai
        if extra is None:
            return carry
        return extra(rows, carry, ar, ai)

    init = 0 if extra is None else (cin_r, cin_i, zero, zero)
    return lax.fori_loop(0, SEG, second, init)


def ssm_fwd(u, lam_re, lam_im, bb_re, bb_im, cc_re, cc_im, d_skip, name):
    t = u.shape[1]
    tt = SCAN_TILE

    def body(u_ref, lr_ref, li_ref, bbr_ref, bbi_ref, ccr_ref, cci_ref, d_ref, yg_ref, hr_ref, hi_ref,
             cr_ref, ci_ref, pr_ref, pi_ref, up_ref, y_ref):
        @pl.when(pl.program_id(0) == 0)
        def _():
            cr_ref[...] = jnp.zeros_like(cr_ref)
            ci_ref[...] = jnp.zeros_like(ci_ref)
            _power_table(lr_ref[...], li_ref[...], pr_ref, pi_ref)

        _load_segments(u_ref, up_ref)
        ub = up_ref[...].astype(BF16)
        for j in range(N_CHUNK):
            hs = slice(j * H_CHUNK, (j + 1) * H_CHUNK)
            us = slice(j * U_CHUNK, (j + 1) * U_CHUNK)
            hr_ref[:, hs] = _nn(ub[:, us], bbr_ref[j])
            hi_ref[:, hs] = _nn(ub[:, us], bbi_ref[j])
        for c in range(STATE_WIDTH // SCAN_LANES):
            lanes = slice(c * SCAN_LANES, (c + 1) * SCAN_LANES)
            _segment_scan(hr_ref, hi_ref, lanes, (lr_ref[:, lanes], li_ref[:, lanes]),
                          lambda k, lanes=lanes: (_table_rows(pr_ref, k, lanes), _table_rows(pi_ref, k, lanes)),
                          cr_ref, ci_ref, False)
        for j in range(N_CHUNK):
            hs = slice(j * H_CHUNK, (j + 1) * H_CHUNK)
            us = slice(j * U_CHUNK, (j + 1) * U_CHUNK)
            y = (_nn(hr_ref[:, hs].astype(BF16), ccr_ref[j]) - _nn(hi_ref[:, hs].astype(BF16), cci_ref[j])
                 + d_ref[:, us] * up_ref[:, us])
            y_ref[:, us] = _gelu_and_grad(y)[0]
        yg_ref[...] = _nn(_perm_matrix(False), y_ref[...].astype(BF16)).astype(BF16)

    return pl.pallas_call(
        body, name=name, grid=(t // tt,),
        in_specs=[_chunked_spec(tt, lambda i: i), _VMEM, _VMEM, _VMEM, _VMEM, _VMEM, _VMEM, _VMEM],
        out_specs=[_row_spec(tt, SSM_WIDTH), _row_spec(tt, STATE_WIDTH), _row_spec(tt, STATE_WIDTH)],
        out_shape=[jax.ShapeDtypeStruct((t, SSM_WIDTH), BF16), jax.ShapeDtypeStruct((t, STATE_WIDTH), F32),
                   jax.ShapeDtypeStruct((t, STATE_WIDTH), F32)],
        scratch_shapes=[pltpu.VMEM((1, STATE_WIDTH), F32), pltpu.VMEM((1, STATE_WIDTH), F32),
                        pltpu.VMEM((SCAN_TILE, STATE_WIDTH), F32), pltpu.VMEM((SCAN_TILE, STATE_WIDTH), F32),
                        pltpu.VMEM((tt, SSM_WIDTH), F32), pltpu.VMEM((tt, SSM_WIDTH), F32)],
        compiler_params=_params(("arbitrary",)),
    )(u, lam_re, lam_im, bb_re, bb_im, cc_re, cc_im, d_skip)


def ssm_bwd(dyg, u, h_re, h_im, lam_re, lam_im, bb_re, bb_im, cc_re, cc_im, d_skip, name):
    t = u.shape[1]
    tt = SCAN_TILE
    nt = t // tt

    def body(dyg_ref, u_ref, hr_ref, hi_ref, lr_ref, li_ref, bbr_ref, bbi_ref, ccr_ref, cci_ref, d_ref,
             du_ref, dlr_ref, dli_ref, dbbr_ref, dbbi_ref, dccr_ref, dcci_ref, dd_ref,
             ar_ref, ai_ref, cr_ref, ci_ref, pr_ref, pi_ref, up_ref, dy_ref, dup_ref):
        step = pl.program_id(0)
        tile = nt - 1 - step

        @pl.when(step == 0)
        def _():
            for ref in (cr_ref, ci_ref, dlr_ref, dli_ref, dbbr_ref, dbbi_ref, dccr_ref, dcci_ref, dd_ref):
                ref[...] = jnp.zeros_like(ref)
            _power_table(lr_ref[...], li_ref[...], pr_ref, pi_ref)

        _load_segments(u_ref, up_ref)
        _load_segments(dyg_ref, dy_ref)
        uv = up_ref[...]
        ub = uv.astype(BF16)
        dskip = d_ref[...]
        for j in range(N_CHUNK):
            hs = slice(j * H_CHUNK, (j + 1) * H_CHUNK)
            us = slice(j * U_CHUNK, (j + 1) * U_CHUNK)
            hrb = hr_ref[:, hs].astype(BF16)
            hib = hi_ref[:, hs].astype(BF16)
            y = _nn(hrb, ccr_ref[j]) - _nn(hib, cci_ref[j]) + dskip[:, us] * uv[:, us]
            dy = dy_ref[:, us] * _gelu_and_grad(y)[1]
            dy_ref[:, us] = dy
            dyb = dy.astype(BF16)
            dccr_ref[j] += _tn(hrb, dyb)
            dcci_ref[j] -= _tn(hib, dyb)
            ar_ref[:, hs] = _nt(dyb, ccr_ref[j])
            ai_ref[:, hs] = -_nt(dyb, cci_ref[j])
        dd_ref[...] += jnp.sum(dy_ref[...] * uv, axis=0, keepdims=True)

        for c in range(STATE_WIDTH // SCAN_LANES):
            lanes = slice(c * SCAN_LANES, (c + 1) * SCAN_LANES)

            def dlambda(rows, carry, ar, ai, lanes=lanes):
                nr, ni, accr, acci = carry
                hr, hi = hr_ref[rows, lanes], hi_ref[rows, lanes]
                return ar, ai, accr + nr * hr + ni * hi, acci + ni * hr - nr * hi

            _, _, accr, acci = _segment_scan(
                ar_ref, ai_ref, lanes, (lr_ref[:, lanes], -li_ref[:, lanes]),
                lambda k, lanes=lanes: (_table_rows(pr_ref, k, lanes), -_table_rows(pi_ref, k, lanes)),
                cr_ref, ci_ref, True, dlambda)
            dlr_ref[:, lanes] += accr
            dli_ref[:, lanes] += acci

        rho = lax.broadcasted_iota(jnp.int32, (tt, U_CHUNK), 0)
        time = tile * tt + (rho % SUB) * SEG + rho // SUB
        for j in range(N_CHUNK):
            hs = slice(j * H_CHUNK, (j + 1) * H_CHUNK)
            us = slice(j * U_CHUNK, (j + 1) * U_CHUNK)
            arb = ar_ref[:, hs].astype(BF16)
            aib = ai_ref[:, hs].astype(BF16)
            dbbr_ref[j] += _tn(ub[:, us], arb)
            dbbi_ref[j] += _tn(ub[:, us], aib)
            du = _nt(arb, bbr_ref[j]) + _nt(aib, bbi_ref[j]) + dy_ref[:, us] * dskip[:, us]
            dup_ref[:, us] = jnp.where(time >= PAD_FRONT, du, 0.0)
        du_ref[...] = _nn(_perm_matrix(False), dup_ref[...].astype(BF16)).astype(BF16)

    rev = lambda i: (nt - 1 - i, 0)
    full = lambda shape: pl.BlockSpec(shape, lambda i: (0,) * len(shape))
    return pl.pallas_call(
        body, name=name, grid=(nt,),
        in_specs=[_chunked_spec(tt, lambda i: nt - 1 - i), _chunked_spec(tt, lambda i: nt - 1 - i),
                  pl.BlockSpec((tt, STATE_WIDTH), rev), pl.BlockSpec((tt, STATE_WIDTH), rev),
                  _VMEM, _VMEM, _VMEM, _VMEM, _VMEM, _VMEM, _VMEM],
        out_specs=[pl.BlockSpec((tt, SSM_WIDTH), rev), full((SUB, STATE_WIDTH)), full((SUB, STATE_WIDTH)),
                   full((N_CHUNK, U_CHUNK, H_CHUNK)), full((N_CHUNK, U_CHUNK, H_CHUNK)),
                   full((N_CHUNK, H_CHUNK, U_CHUNK)), full((N_CHUNK, H_CHUNK, U_CHUNK)), full((1, SSM_WIDTH))],
        out_shape=[jax.ShapeDtypeStruct((t, SSM_WIDTH), BF16),
                   jax.ShapeDtypeStruct((SUB, STATE_WIDTH), F32), jax.ShapeDtypeStruct((SUB, STATE_WIDTH), F32),
                   jax.ShapeDtypeStruct((N_CHUNK, U_CHUNK, H_CHUNK), F32),
                   jax.ShapeDtypeStruct((N_CHUNK, U_CHUNK, H_CHUNK), F32),
                   jax.ShapeDtypeStruct((N_CHUNK, H_CHUNK, U_CHUNK), F32),
                   jax.ShapeDtypeStruct((N_CHUNK, H_CHUNK, U_CHUNK), F32),
                   jax.ShapeDtypeStruct((1, SSM_WIDTH), F32)],
        scratch_shapes=[pltpu.VMEM((tt, STATE_WIDTH), F32), pltpu.VMEM((tt, STATE_WIDTH), F32),
                        pltpu.VMEM((1, STATE_WIDTH), F32), pltpu.VMEM((1, STATE_WIDTH), F32),
                        pltpu.VMEM((SCAN_TILE, STATE_WIDTH), F32), pltpu.VMEM((SCAN_TILE, STATE_WIDTH), F32),
                        pltpu.VMEM((tt, SSM_WIDTH), F32), pltpu.VMEM((tt, SSM_WIDTH), F32),
                        pltpu.VMEM((tt, SSM_WIDTH), F32)],
        compiler_params=_params(("arbitrary",)),
    )(dyg, u, h_re, h_im, lam_re, lam_im, bb_re, bb_im, cc_re, cc_im, d_skip)


def merge_fwd(h, o, yg, gates, wap_t, wv_t, wgg_t, wout, name):
    t, d = h.shape
    tm = TOKEN_TILE

    def body(h_ref, o_ref, yg_ref, gt_ref, wap_ref, wv_ref, wgg_ref, wout_ref, ho_ref, mg_ref, a_ref, sv_ref, sg_ref):
        att = _nt(o_ref[...], wap_ref[...])
        ygv = yg_ref[...]
        sv = _nt(ygv, wv_ref[...])
        sg = _nt(ygv, wgg_ref[...])
        a_ref[...] = att.astype(BF16)
        sv_ref[...] = sv.astype(BF16)
        sg_ref[...] = sg.astype(BF16)
        merged = (jax.nn.sigmoid(gt_ref[:, 0:d].astype(F32)) * att
                  + jax.nn.sigmoid(gt_ref[:, d:2 * d].astype(F32)) * (sv * jax.nn.sigmoid(sg))).astype(BF16)
        mg_ref[...] = merged
        ho_ref[...] = h_ref[...] + _nn(merged, wout_ref[...])

    return pl.pallas_call(
        body, name=name, grid=(t // tm,),
        in_specs=[_row_spec(tm, d), _row_spec(tm, ATTN_WIDTH), _row_spec(tm, SSM_WIDTH), _row_spec(tm, 2 * d),
                  _VMEM, _VMEM, _VMEM, _VMEM],
        out_specs=[_row_spec(tm, d), _row_spec(tm, d), _row_spec(tm, d), _row_spec(tm, d), _row_spec(tm, d)],
        out_shape=[jax.ShapeDtypeStruct((t, d), F32), jax.ShapeDtypeStruct((t, d), BF16),
                   jax.ShapeDtypeStruct((t, d), BF16), jax.ShapeDtypeStruct((t, d), BF16),
                   jax.ShapeDtypeStruct((t, d), BF16)],
        compiler_params=_params(("arbitrary",)),
    )(h, o, yg, gates, wap_t, wv_t, wgg_t, wout)


def merge_bwd(dh, gates, att, sv, sg, wap_t, wv_t, wgg_t, wout, dep, name):
    t, d = dh.shape
    tm = TOKEN_TILE

    def body(dh_ref, gt_ref, a_ref, sv_ref, sg_ref, wap_ref, wv_ref, wgg_ref, wout_ref, dep_ref,
             dgt_ref, da_ref, dsv_ref, dsg_ref, do_ref, dyg_ref, dhb_ref):
        dhb = dh_ref[...].astype(BF16)
        dhb_ref[...] = dhb
        dm = _nt(dhb, wout_ref[...])
        sig_a = jax.nn.sigmoid(gt_ref[:, 0:d].astype(F32))
        sig_s = jax.nn.sigmoid(gt_ref[:, d:2 * d].astype(F32))
        sig_g = jax.nn.sigmoid(sg_ref[...].astype(F32))
        svv = sv_ref[...].astype(F32)
        dgt_ref[:, 0:d] = (dm * a_ref[...].astype(F32) * sig_a * (1.0 - sig_a)).astype(BF16)
        dgt_ref[:, d:2 * d] = (dm * (svv * sig_g) * sig_s * (1.0 - sig_s)).astype(BF16)
        da = (dm * sig_a).astype(BF16)
        d_s = dm * sig_s
        dsv = (d_s * sig_g).astype(BF16)
        dsg = (d_s * svv * sig_g * (1.0 - sig_g)).astype(BF16)
        da_ref[...] = da
        dsv_ref[...] = dsv
        dsg_ref[...] = dsg
        do_ref[...] = _nn(da, wap_ref[...]).astype(BF16)
        dyg = _nn(dsv, wv_ref[...]) + _nn(dsg, wgg_ref[...])
        for j in range(N_CHUNK):
            dyg_ref[j] = dyg[:, j * U_CHUNK:(j + 1) * U_CHUNK]

    return pl.pallas_call(
        body, name=name, grid=(t // tm,),
        in_specs=[_row_spec(tm, d), _row_spec(tm, 2 * d), _row_spec(tm, d), _row_spec(tm, d), _row_spec(tm, d),
                  _VMEM, _VMEM, _VMEM, _VMEM, _ANY],
        out_specs=[_row_spec(tm, 2 * d), _row_spec(tm, d), _row_spec(tm, d), _row_spec(tm, d),
                   _row_spec(tm, ATTN_WIDTH), _chunked_spec(tm, lambda i: i), _row_spec(tm, d)],
        out_shape=[jax.ShapeDtypeStruct((t, 2 * d), BF16), jax.ShapeDtypeStruct((t, d), BF16),
                   jax.ShapeDtypeStruct((t, d), BF16), jax.ShapeDtypeStruct((t, d), BF16),
                   jax.ShapeDtypeStruct((t, ATTN_WIDTH), BF16), jax.ShapeDtypeStruct((N_CHUNK, t, U_CHUNK), F32),
                   jax.ShapeDtypeStruct((t, d), BF16)],
        compiler_params=_params(("arbitrary",)),
    )(dh, gates, att, sv, sg, wap_t, wv_t, wgg_t, wout, dep)


def _adamw_math(w, g, m, v):
    mn = ADAM_B1 * m + (1.0 - ADAM_B1) * g
    vn = ADAM_B2 * v + (1.0 - ADAM_B2) * (g * g)
    m_hat = mn / (1.0 - ADAM_B1 ** ADAM_STEP)
    v_hat = vn / (1.0 - ADAM_B2 ** ADAM_STEP)
    return -ADAM_LR * (m_hat / (jnp.sqrt(v_hat) + ADAM_EPS) + ADAM_WD * w), mn, vn


def sum_adamw_layer(me, landed, partial, w, m, v, layer, prev, name):
    _, rows, cols = w.shape
    tr = rows // 2 if rows % 32 == 0 else rows
    steps = rows // tr

    def body(me_ref, land_ref, own_ref, w_ref, m_ref, v_ref, *rest):
        go_ref, d_ref, mo_ref, vo_ref = rest[-4:]
        who = me_ref[0]
        gv = land_ref[who ^ 1].astype(F32)
        for p in range(2, N_DEV):
            gv = gv + land_ref[who ^ p].astype(F32)
        gv = gv + own_ref[...].astype(F32)
        go_ref[0] = gv
        d_ref[0], mo_ref[0], vo_ref[0] = _adamw_math(w_ref[0], gv, m_ref[0], v_ref[0])

    spec3 = pl.BlockSpec((1, tr, cols), lambda r, me_ref: (layer, r, 0))
    out = jax.ShapeDtypeStruct(w.shape, F32)
    extra = [] if prev is None else list(prev)
    grid_spec = pltpu.PrefetchScalarGridSpec(
        num_scalar_prefetch=1, grid=(steps,),
        in_specs=[pl.BlockSpec((N_DEV, tr, cols), lambda r, me_ref: (0, r, 0)),
                  pl.BlockSpec((tr, cols), lambda r, me_ref: (me_ref[0] * steps + r, 0)),
                  spec3, spec3, spec3] + [_ANY] * len(extra),
        out_specs=[spec3] * 4)
    return pl.pallas_call(
        body, name=name, grid_spec=grid_spec, out_shape=[out] * 4,
        input_output_aliases={6 + j: j for j in range(len(extra))},
        compiler_params=_params(("arbitrary",)),
    )(me, landed, partial, w, m, v, *extra)


def adamw(w, g, m, v, name, minor_swap=False):
    if minor_swap:
        d, mn, vn = adamw(*[jnp.swapaxes(a, -1, -2) for a in (w, g, m, v)], name)
        return jnp.swapaxes(d, -1, -2), jnp.swapaxes(mn, -1, -2), jnp.swapaxes(vn, -1, -2)
    shape = w.shape
    as2d = lambda a: a.reshape(-1, shape[-1]) if a.ndim >= 2 else a.reshape(1, -1)
    w2, g2, m2, v2 = as2d(w), as2d(g), as2d(m), as2d(v)
    rows, cols = w2.shape
    tr = rows
    for cand in (1024, 704, 512, 256):
        if rows > cand and rows % cand == 0:
            tr = cand
            break

    def body(w_ref, g_ref, m_ref, v_ref, d_ref, mo_ref, vo_ref):
        d_ref[...], mo_ref[...], vo_ref[...] = _adamw_math(w_ref[...], g_ref[...], m_ref[...], v_ref[...])

    spec = _row_spec(tr, cols)
    out = jax.ShapeDtypeStruct((rows, cols), F32)
    d, mn, vn = pl.pallas_call(
        body, name=name, grid=(rows // tr,), in_specs=[spec] * 4, out_specs=[spec] * 3, out_shape=[out] * 3,
        compiler_params=_params(("arbitrary",)),
    )(w2, g2, m2, v2)
    return d.reshape(shape), mn.reshape(shape), vn.reshape(shape)


def _my_index():
    return 4 * lax.axis_index("x") + 2 * lax.axis_index("y") + lax.axis_index("c")


def _peer(p):
    return (lax.axis_index("x") ^ ((p >> 2) & 1), lax.axis_index("y") ^ ((p >> 1) & 1), lax.axis_index("c") ^ (p & 1))


_HBM = pl.BlockSpec(memory_space=pltpu.HBM)
_SEM = pl.BlockSpec(memory_space=pltpu.SEMAPHORE)
_EFFECT = pltpu.SideEffectType.DATAFLOW_SIDE_EFFECTING


class Exchange:
    RELAYED = (2, 4, 6)

    def __init__(self, srcs, scatter, name, relay=False):
        self.n = n = len(srcs)
        self.scatter = scatter
        self.name = name
        self.relayed = relay
        assert not (relay and scatter)
        self.direct = (1,) + self.RELAYED if relay else tuple(range(1, N_DEV))
        widths = sorted({s.shape[1] for s in srcs}, reverse=True)
        self.ncls = len(widths)
        self.cls = [widths.index(s.shape[1]) for s in srcs]
        self.cnts = [s.shape[0] // N_DEV if scatter else s.shape[0] for s in srcs]
        self.totals = [sum(c for c, k in zip(self.cnts, self.cls) if k == w) for w in range(self.ncls)]
        self.sizer = [max((k for k in range(n) if self.cls[k] == w), key=lambda k: self.cnts[k])
                      for w in range(self.ncls)]
        assert all(N_DEV * self.cnts[self.sizer[w]] >= self.totals[w] for w in range(self.ncls))
        if scatter:
            self.land_shapes = [(N_DEV, c, s.shape[1]) for s, c in zip(srcs, self.cnts)]
        else:
            self.land_shapes = [(N_DEV * c, s.shape[1]) for s, c in zip(srcs, self.cnts)]
        self.dtypes = [s.dtype for s in srcs]

    def _block(self, k, who):
        return pl.ds(pl.multiple_of(who * self.cnts[k], 16), self.cnts[k])

    def _sem(self, p, w):
        return (p - 1) * self.ncls + w

    def start(self, srcs, after):
        n = self.n

        def body(*refs):
            src, land = refs[:n], refs[n:2 * n]
            send_sems, recv_sems = refs[2 * n + 1], refs[2 * n + 2]
            token = refs[-1]
            me = _my_index()
            for p in self.direct:
                for k in range(n):
                    if self.scatter:
                        s_ref, d_ref = src[k].at[self._block(k, me ^ p), :], land[k].at[me]
                    else:
                        s_ref, d_ref = src[k], land[k].at[self._block(k, me), :]
                    pltpu.make_async_remote_copy(
                        src_ref=s_ref, dst_ref=d_ref, send_sem=send_sems.at[self._sem(p, self.cls[k])],
                        recv_sem=recv_sems.at[self._sem(p, self.cls[k])], device_id=_peer(p),
                        device_id_type=MESH).start()
            token[...] = jnp.zeros_like(token)

        sems = pltpu.SemaphoreType.DMA(((N_DEV - 1) * self.ncls,))
        thru = [pltpu.HBM(s.shape, s.dtype) for s in srcs] + [pltpu.HBM(shp, dt) for shp, dt in
                                                               zip(self.land_shapes, self.dtypes)]
        lands = [pltpu.with_memory_space_constraint(lax.empty(shp, dt), pltpu.HBM)
                 for shp, dt in zip(self.land_shapes, self.dtypes)]
        out = pl.pallas_call(
            body, name=self.name + "_start",
            in_specs=[_HBM] * (2 * n) + [_ANY],
            out_shape=[sems, sems] + thru + [jax.ShapeDtypeStruct((8, 128), F32)],
            out_specs=[_SEM, _SEM] + [_HBM] * (2 * n) + [_VMEM],
            input_output_aliases={j: 2 + j for j in range(2 * n)},
            compiler_params=pltpu.CompilerParams(has_side_effects=_EFFECT),
        )(*[pltpu.with_memory_space_constraint(s, pltpu.HBM) for s in srcs], *lands, after)
        return out[:-1], out[-1]

    def _span_copy(self, src, land, w, send_sem, recv_sem, p):
        big = src[self.sizer[w]] if self.scatter else land[self.sizer[w]]
        span = big.at[pl.ds(0, self.totals[w]), :]
        return pltpu.make_async_remote_copy(src_ref=span, dst_ref=span, send_sem=send_sem, recv_sem=recv_sem,
                                            device_id=_peer(p), device_id_type=MESH)

    def relay(self, state, after):
        n = self.n
        send_sems, recv_sems = state[0], state[1]
        thru = state[2:]
        after = list(after) if isinstance(after, (list, tuple)) else [after]
        first_out = 2 * n + 2 + len(after)

        def body(*refs):
            land = refs[n:2 * n]
            send_a, recv_a = refs[2 * n], refs[2 * n + 1]
            send_b, recv_b = refs[first_out], refs[first_out + 1]
            refs[-1][...] = jnp.zeros_like(refs[-1])
            me = _my_index()
            for p in self.RELAYED:
                for w in range(self.ncls):
                    self._span_copy(None, land, w, send_a.at[self._sem(p, w)], recv_a.at[self._sem(p, w)], p).wait_recv()
            for j, p in enumerate(self.RELAYED):
                for k in range(n):
                    rows = land[k].at[self._block(k, me ^ p), :]
                    pltpu.make_async_remote_copy(
                        src_ref=rows, dst_ref=rows, send_sem=send_b.at[j * self.ncls + self.cls[k]],
                        recv_sem=recv_b.at[j * self.ncls + self.cls[k]], device_id=_peer(1),
                        device_id_type=MESH).start()

        sems = pltpu.SemaphoreType.DMA((len(self.RELAYED) * self.ncls,))
        out = pl.pallas_call(
            body, name=self.name + "_relay",
            in_specs=[_HBM] * (2 * n) + [_SEM, _SEM] + [_ANY] * len(after),
            out_shape=[sems, sems] + [pltpu.HBM(a.shape, a.dtype) for a in thru] + [jax.ShapeDtypeStruct((8, 128), F32)],
            out_specs=[_SEM, _SEM] + [_HBM] * (2 * n) + [_VMEM],
            input_output_aliases={j: 2 + j for j in range(2 * n)},
            compiler_params=pltpu.CompilerParams(has_side_effects=_EFFECT),
        )(*thru, send_sems, recv_sems, *after)
        return [send_sems, recv_sems] + list(out[2:-1]) + [out[0], out[1]], out[-1]

    def wait(self, state, after):
        n = self.n
        send_sems, recv_sems = state[0], state[1]
        thru = state[2:2 + 2 * n]
        relay_sems = list(state[2 + 2 * n:])
        assert len(relay_sems) == (2 if self.relayed else 0)
        after = list(after) if isinstance(after, (list, tuple)) else [after]

        def body(*refs):
            src, land = refs[:n], refs[n:2 * n]
            send_a, recv_a = refs[2 * n], refs[2 * n + 1]
            for p in self.direct:
                for w in range(self.ncls):
                    copy = self._span_copy(src, land, w, send_a.at[self._sem(p, w)], recv_a.at[self._sem(p, w)], p)
                    copy.wait_send()
                    if not (self.relayed and p in self.RELAYED):
                        copy.wait_recv()
            if self.relayed:
                send_b, recv_b = refs[2 * n + 2], refs[2 * n + 3]
                for j in range(len(self.RELAYED)):
                    for w in range(self.ncls):
                        copy = self._span_copy(src, land, w, send_b.at[j * self.ncls + w],
                                               recv_b.at[j * self.ncls + w], 1)
                        copy.wait_send()
                        copy.wait_recv()

        out = pl.pallas_call(
            body, name=self.name + "_wait",
            in_specs=[_HBM] * (2 * n) + [_SEM] * (2 + len(relay_sems)) + [_ANY] * len(after),
            out_shape=[pltpu.HBM(a.shape, a.dtype) for a in thru], out_specs=[_HBM] * (2 * n),
            input_output_aliases={j: j for j in range(2 * n)},
            compiler_params=pltpu.CompilerParams(has_side_effects=_EFFECT),
        )(*thru, send_sems, recv_sems, *relay_sems, *after)
        return out[:n], out[n:]

    def place(self, lands, srcs):
        n = self.n
        assert not self.scatter

        def body(*refs):
            src, land = refs[n:2 * n], refs[2 * n:3 * n]
            bufs, sems = refs[3 * n:4 * n], refs[-1]
            me = _my_index()
            loads = [pltpu.make_async_copy(src[k], bufs[k], sems.at[k]) for k in range(n)]
            stores = [pltpu.make_async_copy(bufs[k], land[k].at[self._block(k, me), :], sems.at[k]) for k in range(n)]
            for cp in loads:
                cp.start()
            for k in range(n):
                loads[k].wait()
                stores[k].start()
            for cp in stores:
                cp.wait()

        return pl.pallas_call(
            body, name=self.name + "_place", in_specs=[_ANY] * (2 * n), out_specs=[_ANY] * n,
            out_shape=[jax.ShapeDtypeStruct(a.shape, a.dtype) for a in lands],
            input_output_aliases={j: j for j in range(n)},
            scratch_shapes=[pltpu.VMEM(s.shape, s.dtype) for s in srcs] + [pltpu.SemaphoreType.DMA((n,))],
        )(*lands, *srcs)


def sum_slots(slots, name):
    _, rows, cols = slots.shape
    tr = rows
    if rows > 512:
        for cand in (256, 128, 64, 32, 16, 8):
            if rows % cand == 0:
                tr = cand
                break

    def body(s_ref, o_ref):
        acc = s_ref[0].astype(F32)
        for j in range(1, N_DEV):
            acc = acc + s_ref[j].astype(F32)
        o_ref[...] = acc

    return pl.pallas_call(
        body, name=name, grid=(rows // tr,),
        in_specs=[pl.BlockSpec((N_DEV, tr, cols), lambda i: (0, i, 0))], out_specs=_row_spec(tr, cols),
        out_shape=jax.ShapeDtypeStruct((rows, cols), F32), compiler_params=_params(("arbitrary",)),
    )(slots)


BIG_N = ("ffn1_w_down", "w_out", "ffn2_w_down")
SMALL = ("ffn1_norm", "mix_norm", "attn_sinks", "ssm_a_re", "ssm_a_im", "ssm_log_dt", "ssm_b_re", "ssm_b_im",
         "ssm_c_re", "ssm_c_im", "ssm_d", "ffn2_norm", "final_norm")
PARTS = {"ffn1": ("ffn1_w_gate", "ffn1_w_up", "ffn1_w_down"),
         "mix": ("w_in", "w_out", "w_attn_proj", "w_glu_v", "w_glu_g"),
         "ffn2": ("ffn2_w_gate", "ffn2_w_up", "ffn2_w_down")}


def _to_rows(name, a):
    return a if name in BIG_N else jnp.swapaxes(a, -1, -2)


def local_step(x, tgt, get_weights, put_grads, small):
    seq, d = x.shape
    t = PAD_FRONT + N_META + seq
    cos_t, sin_t = rope_tables(t)
    row = lambda a: a.reshape(1, -1)
    tables = []
    for i in range(DEPTH):
        b_re_t = jnp.swapaxes(small["ssm_b_re"][i], 1, 2)
        b_im_t = jnp.swapaxes(small["ssm_b_im"][i], 1, 2)
        lam_re, lam_im, bbar_re, bbar_im = ssm_prep(small["ssm_a_re"][i], small["ssm_a_im"][i],
                                                    small["ssm_log_dt"][i].reshape(-1, 1), b_re_t, b_im_t, f"ssm_prep_{i}")
        tables.append(((b_re_t, b_im_t),
                       (row(lam_re), row(lam_im), _block_diag_b(bbar_re).astype(BF16), _block_diag_b(bbar_im).astype(BF16),
                        _block_diag_c(small["ssm_c_re"][i]).astype(BF16), _block_diag_c(small["ssm_c_im"][i]).astype(BF16),
                        row(small["ssm_d"][i]))))
    early = [cos_t, sin_t] + [a for _, tab in tables for a in tab[2:6]]
    saved = []
    h = None
    for i in range(DEPTH):
        s = {}
        w = dict(get_weights(i, "ffn1", early if i == 0 else h))
        if i == 0:
            h = jnp.concatenate([jnp.zeros((PAD_FRONT, d), F32), w["meta_tokens"], x], axis=0)
        s["h0"] = h
        h, s["n1"], s["acts1"] = ffn_fwd(h, row(small["ffn1_norm"][i]), w["ffn1_w_gate"], w["ffn1_w_up"],
                                               w["ffn1_w_down"], f"ffn1_fwd_{i}")
        s["h1"] = h
        w.update(get_weights(i, "mix", h))
        s["n2"], s["qkv"], s["u"], s["gates"] = win_fwd(h, row(small["mix_norm"][i]), w["w_in"], cos_t, sin_t,
                                                        f"win_fwd_{i}")
        s["b_t"], s["ssm"] = tables[i]
        s["yg"], s["h_re"], s["h_im"] = ssm_fwd(s["u"], *s["ssm"], f"ssm_fwd_{i}")
        s["o"] = attn_fwd(s["qkv"], row(small["attn_sinks"][i]), f"attn_fwd_{i}")
        h, s["merged"], s["att"], s["sv"], s["sg"] = merge_fwd(
            h, s["o"], s["yg"], s["gates"], w["w_attn_proj"], w["w_glu_v"], w["w_glu_g"], w["w_out"],
            f"merge_fwd_{i}")
        s["h2"] = h
        w.update(get_weights(i, "ffn2", h))
        h, s["n3"], s["acts3"] = ffn_fwd(h, row(small["ffn2_norm"][i]), w["ffn2_w_gate"], w["ffn2_w_up"],
                                               w["ffn2_w_down"], f"ffn2_fwd_{i}")
        s["w"] = w
        saved.append(s)

    loss, dh, d_final = head_fwd_bwd(h, row(small["final_norm"]), tgt)
    gs = {k: [None] * DEPTH for k in SMALL if k != "final_norm"}
    dep = loss
    for i in reversed(range(DEPTH)):
        s = saved[i]
        w = s["w"]
        dh, da, db, sact, dhb, dg = ffn_bwd(dh, s["h2"], row(small["ffn2_norm"][i]), s["acts3"], w["ffn2_w_gate"],
                                            w["ffn2_w_up"], w["ffn2_w_down"], dep, f"ffn2_bwd_{i}")
        gs["ffn2_norm"][i] = dg[0]
        dep = put_grads(i, "ffn2", {"ffn2_w_gate": tn_matmul(da, s["n3"], f"ffn2_dwg_{i}"),
                                    "ffn2_w_up": tn_matmul(db, s["n3"], f"ffn2_dwu_{i}"),
                                    "ffn2_w_down": tn_matmul(sact, dhb, f"ffn2_dwd_{i}")})

        dgates, datt, dsv, dsg, do, dyg, dhb = merge_bwd(dh, s["gates"], s["att"], s["sv"], s["sg"], w["w_attn_proj"],
                                                         w["w_glu_v"], w["w_glu_g"], w["w_out"], dep, f"merge_bwd_{i}")
        gmix = {"w_out": tn_matmul(s["merged"], dhb, f"dwout_{i}"),
                "w_attn_proj": tn_matmul(datt, s["o"], f"dwap_{i}"),
                "w_glu_v": tn_matmul(dsv, s["yg"], f"dwv_{i}"),
                "w_glu_g": tn_matmul(dsg, s["yg"], f"dwgg_{i}")}
        dqkv, dsink = attn_bwd(s["qkv"], do, row(small["attn_sinks"][i]), cos_t, sin_t, f"attn_bwd_{i}")
        gs["attn_sinks"][i] = dsink[:, 0]
        du, dl_re, dl_im, dbb_re, dbb_im, dcc_re, dcc_im, dd = ssm_bwd(dyg, s["u"], s["h_re"], s["h_im"], *s["ssm"],
                                                                      f"ssm_bwd_{i}")
        fold = lambda a: jnp.sum(a, axis=0).reshape(SSM_GROUPS, SSM_STATE)
        da_re, da_im, dldt, db_re_t, db_im_t = ssm_prep_bwd(
            small["ssm_a_re"][i], small["ssm_a_im"][i], small["ssm_log_dt"][i].reshape(-1, 1), *s["b_t"],
            fold(dl_re), fold(dl_im), _diag_of_b(dbb_re), _diag_of_b(dbb_im), f"ssm_prep_bwd_{i}")
        gs["ssm_a_re"][i], gs["ssm_a_im"][i], gs["ssm_log_dt"][i] = da_re, da_im, dldt[:, 0]
        gs["ssm_b_re"][i], gs["ssm_b_im"][i] = jnp.swapaxes(db_re_t, 1, 2), jnp.swapaxes(db_im_t, 1, 2)
        gs["ssm_c_re"][i], gs["ssm_c_im"][i] = _diag_of_c(dcc_re), _diag_of_c(dcc_im)
        gs["ssm_d"][i] = dd[0]
        gmix["w_in"] = tn_matmul([dqkv, du, dgates], s["n2"], f"dwin_{i}")
        dep = put_grads(i, "mix", gmix)
        dh, dg = win_bwd(dh, s["h1"], row(small["mix_norm"][i]), dqkv, du, dgates, w["w_in"], dep, f"win_bwd_{i}")
        gs["mix_norm"][i] = dg[0]

        dh, da, db, sact, dhb, dg = ffn_bwd(dh, s["h0"], row(small["ffn1_norm"][i]), s["acts1"], w["ffn1_w_gate"],
                                            w["ffn1_w_up"], w["ffn1_w_down"], dep, f"ffn1_bwd_{i}")
        gs["ffn1_norm"][i] = dg[0]
        if i > 0:
            dep = put_grads(i, "ffn1", {"ffn1_w_gate": tn_matmul(da, s["n1"], f"ffn1_dwg_{i}"),
                                        "ffn1_w_up": tn_matmul(db, s["n1"], f"ffn1_dwu_{i}"),
                                        "ffn1_w_down": tn_matmul(sact, dhb, f"ffn1_dwd_{i}")})
        else:
            for k, xa, ya in (("ffn1_w_down", sact, dhb), ("ffn1_w_gate", da, s["n1"]), ("ffn1_w_up", db, s["n1"])):
                dep = put_grads(i, "ffn1", {k: tn_matmul(xa, ya, f"d_{k}_{i}", dep)})

    gs = {k: jnp.stack(v) for k, v in gs.items()}
    gs["final_norm"] = d_final[0]
    return loss[0, 0], dh[PAD_FRONT + N_META:], dh[PAD_FRONT:PAD_FRONT + N_META], gs, dep


def _pack_rows(arrays, cols):
    flat = jnp.concatenate([a.reshape(-1) for a in arrays])
    rows = -(-flat.shape[0] // cols)
    rows = -(-rows // 16) * 16
    return jnp.pad(flat, (0, rows * cols - flat.shape[0])).reshape(rows, cols)


def _unpack_rows(packed, shapes):
    flat = packed.reshape(-1)
    out, off = [], 0
    for shp in shapes:
        n = math.prod(shp)
        out.append(flat[off:off + n].reshape(shp))
        off += n
    return out


def kernel(x, meta_tokens, ffn1_norm, ffn1_w_gate, ffn1_w_up, ffn1_w_down, mix_norm, w_in, attn_sinks, ssm_a_re, ssm_a_im, ssm_log_dt, ssm_b_re, ssm_b_im, ssm_c_re, ssm_c_im, ssm_d, w_attn_proj, w_glu_v, w_glu_g, w_out, ffn2_norm, ffn2_w_gate, ffn2_w_up, ffn2_w_down, final_norm, loss_target, m_meta_tokens, m_ffn1_norm, m_ffn1_w_gate, m_ffn1_w_up, m_ffn1_w_down, m_mix_norm, m_w_in, m_attn_sinks, m_ssm_a_re, m_ssm_a_im, m_ssm_log_dt, m_ssm_b_re, m_ssm_b_im, m_ssm_c_re, m_ssm_c_im, m_ssm_d, m_w_attn_proj, m_w_glu_v, m_w_glu_g, m_w_out, m_ffn2_norm, m_ffn2_w_gate, m_ffn2_w_up, m_ffn2_w_down, m_final_norm, v_meta_tokens, v_ffn1_norm, v_ffn1_w_gate, v_ffn1_w_up, v_ffn1_w_down, v_mix_norm, v_w_in, v_attn_sinks, v_ssm_a_re, v_ssm_a_im, v_ssm_log_dt, v_ssm_b_re, v_ssm_b_im, v_ssm_c_re, v_ssm_c_im, v_ssm_d, v_w_attn_proj, v_w_glu_v, v_w_glu_g, v_w_out, v_ffn2_norm, v_ffn2_w_gate, v_ffn2_w_up, v_ffn2_w_down, v_final_norm):
    names = ("meta_tokens", "ffn1_norm", "ffn1_w_gate", "ffn1_w_up", "ffn1_w_down", "mix_norm", "w_in", "attn_sinks",
             "ssm_a_re", "ssm_a_im", "ssm_log_dt", "ssm_b_re", "ssm_b_im", "ssm_c_re", "ssm_c_im", "ssm_d",
             "w_attn_proj", "w_glu_v", "w_glu_g", "w_out", "ffn2_norm", "ffn2_w_gate", "ffn2_w_up", "ffn2_w_down",
             "final_norm")
    weights = dict(zip(names, (meta_tokens, ffn1_norm, ffn1_w_gate, ffn1_w_up, ffn1_w_down, mix_norm, w_in, attn_sinks, ssm_a_re, ssm_a_im, ssm_log_dt, ssm_b_re, ssm_b_im, ssm_c_re, ssm_c_im, ssm_d, w_attn_proj, w_glu_v, w_glu_g, w_out, ffn2_norm, ffn2_w_gate, ffn2_w_up, ffn2_w_down, final_norm)))
    moments_m = dict(zip(names, (m_meta_tokens, m_ffn1_norm, m_ffn1_w_gate, m_ffn1_w_up, m_ffn1_w_down, m_mix_norm, m_w_in, m_attn_sinks, m_ssm_a_re, m_ssm_a_im, m_ssm_log_dt, m_ssm_b_re, m_ssm_b_im, m_ssm_c_re, m_ssm_c_im, m_ssm_d, m_w_attn_proj, m_w_glu_v, m_w_glu_g, m_w_out, m_ffn2_norm, m_ffn2_w_gate, m_ffn2_w_up, m_ffn2_w_down, m_final_norm)))
    moments_v = dict(zip(names, (v_meta_tokens, v_ffn1_norm, v_ffn1_w_gate, v_ffn1_w_up, v_ffn1_w_down, v_mix_norm, v_w_in, v_attn_sinks, v_ssm_a_re, v_ssm_a_im, v_ssm_log_dt, v_ssm_b_re, v_ssm_b_im, v_ssm_c_re, v_ssm_c_im, v_ssm_d, v_w_attn_proj, v_w_glu_v, v_w_glu_g, v_w_out, v_ffn2_norm, v_ffn2_w_gate, v_ffn2_w_up, v_ffn2_w_down, v_final_norm)))
    me = _my_index()

    order = [(i, part) for i in range(DEPTH) for part in PARTS]
    gathers = {}
    token = jnp.zeros((8, 128), F32)
    for i, part in order:
        shards = [_to_rows(k, weights[k][i]).astype(BF16) for k in PARTS[part]]
        if (i, part) == order[0]:
            shards.append(meta_tokens)
        ex = Exchange(shards, False, f"gather_{part}_{i}", relay=True)
        state, token = ex.start(shards, token)
        gathers[i, part] = [ex, state, False]
    all_started = token

    def relay(group, after):
        ex, state, relayed = gathers[group]
        if relayed:
            return []
        new_state, relay_token = ex.relay(state, after)
        gathers[group][1:] = [new_state, True]
        return [relay_token]

    def get_weights(i, part, after):
        g = order.index((i, part))
        after = [all_started] + list(after) if g == 0 else [after]
        tokens = relay(order[g], after)
        if g >= 2 and g + 1 < len(order):
            tokens += relay(order[g + 1], after)
        ex, state, _ = gathers[i, part]
        shards, lands = ex.wait(state, after + tokens)
        fulls = ex.place(lands, shards)
        got = dict(zip(PARTS[part], fulls))
        if (i, part) == (0, "ffn1"):
            got["meta_tokens"] = jnp.swapaxes(fulls[-1].reshape(N_DEV, N_META, 128), 0, 1).reshape(N_META, D_MODEL)
        return got

    scatters = []

    def put_grads(i, part, gdict):
        ks = list(gdict)
        srcs = [gdict[k] for k in ks]
        ex = Exchange(srcs, True, f"scatter_{part if len(ks) > 1 else ks[0]}_{i}")
        state, tok = ex.start(srcs, all_started)
        scatters.append((i, ks, ex, state))
        return tok

    small = {k: weights[k] for k in SMALL}
    loss, dx, dmeta, gs, last_started = local_step(x[0], loss_target[0], get_weights, put_grads, small)

    grads, deltas, new_m, new_v = {}, {}, {}, {}
    small_list = [loss.reshape(1), dmeta] + [gs[k] for k in SMALL]
    packed = _pack_rows(small_list, D_MODEL)
    small_ex = Exchange([packed], False, "gather_small")
    small_state, after = small_ex.start([packed], last_started)

    updated = {}
    me_index = jnp.reshape(me, (1,)).astype(jnp.int32)
    for i, ks, ex, state in scatters:
        partials, lands = ex.wait(state, after)
        for k, partial, slots in zip(ks, partials, lands):
            updated[k] = sum_adamw_layer(me_index, slots, partial, _to_rows(k, weights[k]), _to_rows(k, moments_m[k]),
                                         _to_rows(k, moments_v[k]), i, updated.get(k), f"adamw_{k}_{i}")
            after = updated[k][0]
    for k, outs in updated.items():
        grads[k], deltas[k], new_m[k], new_v[k] = [_to_rows(k, a) for a in outs]

    packed_own, packed_all = small_ex.wait(small_state, after)
    (packed_all,) = small_ex.place(packed_all, packed_own)
    total = sum_slots(packed_all.reshape(N_DEV, packed.shape[0], D_MODEL), "sum_small")
    pieces = _unpack_rows(total, [a.shape for a in small_list])
    loss_out = pieces[0][0]
    grads["meta_tokens"] = lax.dynamic_slice_in_dim(pieces[1], me * 128, 128, axis=1)
    for k, p in zip(SMALL, pieces[2:]):
        grads[k] = p
    for k in ("meta_tokens",) + SMALL:
        deltas[k], new_m[k], new_v[k] = adamw(weights[k], grads[k], moments_m[k], moments_v[k], f"adamw_{k}",
                                              minor_swap=k in ("ssm_b_re", "ssm_b_im"))
    return (loss_out, dx[None], *[grads[k] for k in names], *[deltas[k] for k in names],
            *[new_m[k] for k in names], *[new_v[k] for k in names])
```

```python
import math

import jax
import jax.numpy as jnp
from jax import lax
from jax.experimental import pallas as pl
from jax.experimental.pallas import tpu as pltpu

F32 = jnp.float32
BF16 = jnp.bfloat16

D_MODEL = 1024
DEPTH = 2
N_META = 16
HEAD_DIM = 64
N_Q_HEADS = 8
ATTN_WIDTH = 512
KV_WIDTH = 128
QKV_WIDTH = ATTN_WIDTH + 2 * KV_WIDTH
WINDOW = 128
BLK = 128
ROPE_THETA = 500000.0
ROT_DIM = 16
SSM_WIDTH = 512
SSM_GROUP = 16
SSM_GROUPS = 32
SSM_STATE = 64
STATE_WIDTH = SSM_GROUPS * SSM_STATE
D_FF = 2816
IN_WIDTH = 3328
EPS = 1e-6
NEG_INF = -1e30
PAD_FRONT = (-N_META) % BLK
N_DEV = 8

ADAM_LR = 0.001
ADAM_B1 = 0.9
ADAM_B2 = 0.999
ADAM_EPS = 1e-08
ADAM_WD = 0.01
ADAM_STEP = 10

VMEM_LIMIT = 56 * 1024 * 1024
TOKEN_TILE = 384
_VMEM = pl.BlockSpec(memory_space=pltpu.VMEM)
_SMEM = pl.BlockSpec(memory_space=pltpu.SMEM)
_ANY = pl.BlockSpec(memory_space=pl.ANY)
MESH = pl.DeviceIdType.MESH


def _params(sem=None):
    return pltpu.CompilerParams(dimension_semantics=sem, vmem_limit_bytes=VMEM_LIMIT)


def _nt(a, b):
    return lax.dot_general(a, b, (((1,), (1,)), ((), ())), preferred_element_type=F32)


def _nn(a, b):
    return jnp.dot(a, b, preferred_element_type=F32)


def _tn(a, b):
    return lax.dot_general(a, b, (((0,), (0,)), ((), ())), preferred_element_type=F32)


def _row_spec(tm, width):
    return pl.BlockSpec((tm, width), lambda i: (i, 0))


def _acc_spec(shape):
    return pl.BlockSpec(shape, lambda i: (0,) * len(shape))


def _rms_stats(x):
    r = lax.rsqrt(jnp.mean(x * x, axis=-1, keepdims=True) + EPS)
    return x * r, r


def _rms_bwd(dn, xh, r, g):
    dg = jnp.sum(dn * xh, axis=0, keepdims=True)
    dxh = dn * g
    dx = r * (dxh - xh * jnp.mean(dxh * xh, axis=-1, keepdims=True))
    return dx, dg


def ffn_fwd(h, g, wg_t, wu_t, wd, name):
    t, d = h.shape
    f = wd.shape[0]
    tm = TOKEN_TILE

    def body(h_ref, g_ref, wg_ref, wu_ref, wd_ref, ho_ref, n_ref, sl_ref, p_ref, s_ref):
        x = h_ref[...]
        xh, _ = _rms_stats(x)
        n = (xh * g_ref[...]).astype(BF16)
        n_ref[...] = n
        a = _nt(n, wg_ref[...])
        b = _nt(n, wu_ref[...])
        sig = jax.nn.sigmoid(a)
        sl = a * sig
        sl_ref[...] = sl.astype(BF16)
        p_ref[...] = (b * (sig + sl * (1.0 - sig))).astype(BF16)
        s = (sl * b).astype(BF16)
        s_ref[...] = s
        ho_ref[...] = x + 0.5 * _nn(s, wd_ref[...])

    ho, n, sl, p, s = pl.pallas_call(
        body, name=name, grid=(t // tm,),
        in_specs=[_row_spec(tm, d), _acc_spec((1, d)), _VMEM, _VMEM, _VMEM],
        out_specs=[_row_spec(tm, d), _row_spec(tm, d), _row_spec(tm, f), _row_spec(tm, f), _row_spec(tm, f)],
        out_shape=[jax.ShapeDtypeStruct((t, d), F32), jax.ShapeDtypeStruct((t, d), BF16),
                   jax.ShapeDtypeStruct((t, f), BF16), jax.ShapeDtypeStruct((t, f), BF16),
                   jax.ShapeDtypeStruct((t, f), BF16)],
        compiler_params=_params(("arbitrary",)),
    )(h, g, wg_t, wu_t, wd)
    return ho, n, (sl, p, s)


def ffn_bwd(dh, h, g, acts, wg_t, wu_t, wd, dep, name):
    t, d = h.shape
    f = wd.shape[0]
    tm = TOKEN_TILE
    sl, p, s = acts

    def hidden_body(dh_ref, sl_ref, p_ref, wd_ref, dep_ref, da_ref, db_ref, dhb_ref):
        dhb = (0.5 * dh_ref[...]).astype(BF16)
        dhb_ref[...] = dhb
        ds = _nt(dhb, wd_ref[...])
        da_ref[...] = (ds * p_ref[...].astype(F32)).astype(BF16)
        db_ref[...] = (ds * sl_ref[...].astype(F32)).astype(BF16)

    da, db, dhb = pl.pallas_call(
        hidden_body, name=name + "_h", grid=(t // tm,),
        in_specs=[_row_spec(tm, d), _row_spec(tm, f), _row_spec(tm, f), _VMEM, _ANY],
        out_specs=[_row_spec(tm, f), _row_spec(tm, f), _row_spec(tm, d)],
        out_shape=[jax.ShapeDtypeStruct((t, f), BF16), jax.ShapeDtypeStruct((t, f), BF16),
                   jax.ShapeDtypeStruct((t, d), BF16)],
        compiler_params=_params(("arbitrary",)),
    )(dh, sl, p, wd, dep)

    def input_body(dh_ref, h_ref, g_ref, da_ref, db_ref, wg_ref, wu_ref, dhi_ref, dg_ref):
        dn = _nn(da_ref[...], wg_ref[...]) + _nn(db_ref[...], wu_ref[...])
        xh, r = _rms_stats(h_ref[...])
        dx, dg = _rms_bwd(dn, xh, r, g_ref[...])
        dhi_ref[...] = dh_ref[...] + dx

        @pl.when(pl.program_id(0) == 0)
        def _():
            dg_ref[...] = jnp.zeros_like(dg_ref)

        dg_ref[...] += dg

    dhi, dg = pl.pallas_call(
        input_body, name=name + "_x", grid=(t // tm,),
        in_specs=[_row_spec(tm, d), _row_spec(tm, d), _acc_spec((1, d)), _row_spec(tm, f), _row_spec(tm, f),
                  _VMEM, _VMEM],
        out_specs=[_row_spec(tm, d), _acc_spec((1, d))],
        out_shape=[jax.ShapeDtypeStruct((t, d), F32), jax.ShapeDtypeStruct((1, d), F32)],
        compiler_params=_params(("arbitrary",)),
    )(dh, h, g, da, db, wg_t, wu_t)
    return dhi, da, db, s, dhb, dg


DW_TILE = 256


def tn_matmul(x, y, name, dep=None):
    xs = list(x) if isinstance(x, (list, tuple)) else [x]
    t = xs[0].shape[0]
    n = y.shape[1]
    bm = DW_TILE
    tiles = [a.shape[1] // bm for a in xs]
    offs = [sum(tiles[:k]) for k in range(len(xs))]
    deps = [] if dep is None else [dep]

    def body(*refs):
        y_ref, o_ref = refs[len(xs)], refs[-1]
        i = pl.program_id(0)
        for k in range(len(xs)):
            @pl.when((i >= offs[k]) & (i < offs[k] + tiles[k]))
            def _(k=k):
                o_ref[...] = _tn(refs[k][...], y_ref[...]).astype(BF16)

    def x_spec(k):
        return pl.BlockSpec((t, bm), lambda i: (0, jnp.clip(i - offs[k], 0, tiles[k] - 1)))

    return pl.pallas_call(
        body, name=name, grid=(sum(tiles),),
        in_specs=[x_spec(k) for k in range(len(xs))] + [_VMEM] + [_ANY] * len(deps),
        out_specs=pl.BlockSpec((bm, n), lambda i: (i, 0)),
        out_shape=jax.ShapeDtypeStruct((sum(tiles) * bm, n), BF16),
        compiler_params=_params(("arbitrary",)),
    )(*xs, y, *deps)


def head_fwd_bwd(h, g, tgt):
    t, d = h.shape
    tm = TOKEN_TILE
    per_tile = tm // BLK
    last = tgt.shape[0] // BLK - 1

    def body(h_ref, g_ref, *rest):
        t_refs, (loss_ref, dh_ref, dg_ref) = rest[:per_tile], rest[per_tile:]
        i = pl.program_id(0)
        xh, r = _rms_stats(h_ref[...])
        gv = g_ref[...]
        target = jnp.concatenate([ref[...] for ref in t_refs], axis=0)
        row = i * tm + lax.broadcasted_iota(jnp.int32, (tm, 1), 0)
        e = jnp.where(row >= BLK, xh * gv - target, 0.0)
        dx, dg = _rms_bwd(e * (1.0 / d), xh, r, gv)
        dh_ref[...] = dx

        @pl.when(i == 0)
        def _():
            dg_ref[...] = jnp.zeros_like(dg_ref)
            loss_ref[...] = jnp.zeros_like(loss_ref)

        dg_ref[...] += dg
        loss_ref[...] += jnp.sum(e * e) * (0.5 / d)

    def target_spec(k):
        return pl.BlockSpec((BLK, d), lambda i: (jnp.clip(i * per_tile - 1 + k, 0, last), 0))

    return pl.pallas_call(
        body, name="head", grid=(t // tm,),
        in_specs=[_row_spec(tm, d), _acc_spec((1, d))] + [target_spec(k) for k in range(per_tile)],
        out_specs=[_acc_spec((1, 128)), _row_spec(tm, d), _acc_spec((1, d))],
        out_shape=[jax.ShapeDtypeStruct((1, 128), F32), jax.ShapeDtypeStruct((t, d), F32),
                   jax.ShapeDtypeStruct((1, d), F32)],
        compiler_params=_params(("arbitrary",)),
    )(h, g, *[tgt] * per_tile)


def rope_tables(t):
    pos = jnp.arange(t, dtype=F32) - PAD_FRONT
    inv_freq = ROPE_THETA ** (-jnp.arange(0, ROT_DIM, 2, dtype=F32) / ROT_DIM)
    ang = pos[:, None] * inv_freq[None, :]
    cos, sin = jnp.cos(ang), jnp.sin(ang)
    ones = jnp.ones((t, HEAD_DIM - ROT_DIM), F32)
    cos_h = jnp.concatenate([cos, cos, ones], axis=1)
    sin_h = jnp.concatenate([-sin, sin, 0.0 * ones], axis=1)
    return jnp.concatenate([cos_h, cos_h], axis=1), jnp.concatenate([sin_h, sin_h], axis=1)


def _swap_halves(x):
    n = x.shape[1]
    lane = lax.broadcasted_iota(jnp.int32, x.shape, 1)
    return jnp.where(lane % HEAD_DIM < ROT_DIM // 2, pltpu.roll(x, n - ROT_DIM // 2, 1), pltpu.roll(x, ROT_DIM // 2, 1))


def _rope(x, cos_t, sin_t, sign):
    return x * cos_t + sign * (_swap_halves(x) * sin_t)


def win_fwd(h, g, win_t, cos_t, sin_t, name):
    t, d = h.shape
    tm = TOKEN_TILE

    def body(h_ref, g_ref, w_ref, c_ref, s_ref, n_ref, qkv_ref, u_ref, gates_ref):
        xh, _ = _rms_stats(h_ref[...])
        n = (xh * g_ref[...]).astype(BF16)
        n_ref[...] = n
        z = _nt(n, w_ref[...])
        c, s = c_ref[...], s_ref[...]
        for j in range((ATTN_WIDTH + KV_WIDTH) // 128):
            qkv_ref[:, j * 128:(j + 1) * 128] = _rope(z[:, j * 128:(j + 1) * 128], c, s, 1.0).astype(BF16)
        qkv_ref[:, ATTN_WIDTH + KV_WIDTH:QKV_WIDTH] = z[:, ATTN_WIDTH + KV_WIDTH:QKV_WIDTH].astype(BF16)
        for j in range(N_CHUNK):
            u_ref[j] = z[:, QKV_WIDTH + j * U_CHUNK:QKV_WIDTH + (j + 1) * U_CHUNK]
        gates_ref[...] = z[:, QKV_WIDTH + SSM_WIDTH:].astype(BF16)

    return pl.pallas_call(
        body, name=name, grid=(t // tm,),
        in_specs=[_row_spec(tm, d), _acc_spec((1, d)), _VMEM, _row_spec(tm, 128), _row_spec(tm, 128)],
        out_specs=[_row_spec(tm, d), _row_spec(tm, QKV_WIDTH), _chunked_spec(tm, lambda i: i), _row_spec(tm, 2 * d)],
        out_shape=[jax.ShapeDtypeStruct((t, d), BF16), jax.ShapeDtypeStruct((t, QKV_WIDTH), BF16),
                   jax.ShapeDtypeStruct((N_CHUNK, t, U_CHUNK), F32), jax.ShapeDtypeStruct((t, 2 * d), BF16)],
        compiler_params=_params(("arbitrary",)),
    )(h, g, win_t, cos_t, sin_t)


def win_bwd(dh, h, g, dqkv, du, dgates, win_t, dep, name):
    t, d = h.shape
    tm = TOKEN_TILE

    def body(dh_ref, h_ref, g_ref, dqkv_ref, du_ref, dgt_ref, w_ref, dep_ref, dhi_ref, dg_ref):
        dn = (_nn(dqkv_ref[...], w_ref[0:QKV_WIDTH, :])
              + _nn(du_ref[...], w_ref[QKV_WIDTH:QKV_WIDTH + SSM_WIDTH, :])
              + _nn(dgt_ref[...], w_ref[QKV_WIDTH + SSM_WIDTH:, :]))
        xh, r = _rms_stats(h_ref[...])
        dx, dg = _rms_bwd(dn, xh, r, g_ref[...])
        dhi_ref[...] = dh_ref[...] + dx

        @pl.when(pl.program_id(0) == 0)
        def _():
            dg_ref[...] = jnp.zeros_like(dg_ref)

        dg_ref[...] += dg

    return pl.pallas_call(
        body, name=name, grid=(t // tm,),
        in_specs=[_row_spec(tm, d), _row_spec(tm, d), _acc_spec((1, d)), _row_spec(tm, QKV_WIDTH),
                  _row_spec(tm, SSM_WIDTH), _row_spec(tm, 2 * d), _VMEM, _ANY],
        out_specs=[_row_spec(tm, d), _acc_spec((1, d))],
        out_shape=[jax.ShapeDtypeStruct((t, d), F32), jax.ShapeDtypeStruct((1, d), F32)],
        compiler_params=_params(("arbitrary",)),
    )(dh, h, g, dqkv, du, dgates, win_t, dep)


def _attn_mask(blk):
    q_pos = blk * BLK + lax.broadcasted_iota(jnp.int32, (BLK, 3 * BLK), 0) - PAD_FRONT
    col = lax.broadcasted_iota(jnp.int32, (BLK, 3 * BLK), 1)
    part = col // BLK
    k_pos = jnp.where(part == 0, col, (blk + part - 2) * BLK + (col - part * BLK)) - PAD_FRONT
    dist = q_pos - k_pos
    meta_ok = (part == 0) & (k_pos >= 0) & (dist >= 0)
    band_ok = (part > 0) & (k_pos >= N_META) & (dist >= 0) & (dist < WINDOW)
    return meta_ok | band_ok


def _head_halves(x128, kv):
    x = x128.astype(F32)
    lane = lax.broadcasted_iota(jnp.int32, x.shape, 1)
    swapped = pltpu.roll(x, HEAD_DIM, 1)
    lo, hi = (x, swapped) if kv == 0 else (swapped, x)
    return jnp.where(lane < HEAD_DIM, lo, 0.0).astype(BF16), jnp.where(lane >= HEAD_DIM, hi, 0.0).astype(BF16)


def _gather_keys(meta_ref, prev_ref, cur_ref, lo):
    return jnp.concatenate([meta_ref[:, lo:lo + 128], prev_ref[:, lo:lo + 128], cur_ref[:, lo:lo + 128]], axis=0)


def _pair_lanes(kv):
    return slice(2 * kv * 128, (2 * kv + 1) * 128), slice((2 * kv + 1) * 128, (2 * kv + 2) * 128)


def _stacked_sinks(sink_ref, head):
    row = lax.broadcasted_iota(jnp.int32, (2 * BLK, 1), 0)
    return jnp.where(row < BLK, sink_ref[0, head], sink_ref[0, head + 2])


def _softmax_with_sink(s, mask, sink):
    s = jnp.where(mask, s * (HEAD_DIM ** -0.5), NEG_INF)
    m = jnp.maximum(jnp.max(s, axis=-1, keepdims=True), sink)
    p = jnp.exp(s - m)
    p_sink = jnp.exp(sink - m)
    inv = 1.0 / (jnp.sum(p, axis=-1, keepdims=True) + p_sink)
    return p * inv, p_sink * inv


def attn_fwd(qkv, sinks, name):
    t = qkv.shape[0]
    nb = t // BLK

    def body(sink_ref, meta_ref, prev_ref, cur_ref, o_ref):
        blk = pl.program_id(0)
        mask = _attn_mask(blk)
        mask2 = jnp.concatenate([mask, mask], axis=0)
        k128 = _gather_keys(meta_ref, prev_ref, cur_ref, ATTN_WIDTH)
        v128 = _gather_keys(meta_ref, prev_ref, cur_ref, ATTN_WIDTH + KV_WIDTH)
        for kv in range(2):
            k_lo, k_hi = _head_halves(k128, kv)
            v_lo, v_hi = _head_halves(v128, kv)
            lanes0, lanes1 = _pair_lanes(kv)
            q2 = jnp.concatenate([cur_ref[:, lanes0], cur_ref[:, lanes1]], axis=0)
            p_a, _ = _softmax_with_sink(_nt(q2, k_lo), mask2, _stacked_sinks(sink_ref, 4 * kv))
            p_b, _ = _softmax_with_sink(_nt(q2, k_hi), mask2, _stacked_sinks(sink_ref, 4 * kv + 1))
            o2 = (_nn(p_a.astype(BF16), v_lo) + _nn(p_b.astype(BF16), v_hi)).astype(BF16)
            o_ref[:, lanes0] = o2[0:BLK]
            o_ref[:, lanes1] = o2[BLK:2 * BLK]

    blk_spec = lambda f: pl.BlockSpec((BLK, QKV_WIDTH), f)
    return pl.pallas_call(
        body, name=name, grid=(nb,),
        in_specs=[_SMEM, blk_spec(lambda i: (0, 0)), blk_spec(lambda i: (jnp.maximum(i - 1, 0), 0)),
                  blk_spec(lambda i: (i, 0))],
        out_specs=_row_spec(BLK, ATTN_WIDTH),
        out_shape=jax.ShapeDtypeStruct((t, ATTN_WIDTH), BF16),
        compiler_params=_params(("arbitrary",)),
    )(sinks, qkv, qkv, qkv)


def attn_bwd(qkv, do, sinks, cos_t, sin_t, name):
    t = qkv.shape[0]
    nb = t // BLK

    def body(sink_ref, meta_ref, prev_ref, cur_ref, do_ref, c_ref, s_ref, dqkv_ref, dsink_ref, carry_ref, macc_ref):
        step = pl.program_id(0)
        blk = nb - 1 - step

        @pl.when(step == 0)
        def _():
            dsink_ref[...] = jnp.zeros_like(dsink_ref)
            carry_ref[...] = jnp.zeros_like(carry_ref)
            macc_ref[...] = jnp.zeros_like(macc_ref)

        mask = _attn_mask(blk)
        mask2 = jnp.concatenate([mask, mask], axis=0)
        lane = lax.broadcasted_iota(jnp.int32, (3 * BLK, 128), 1)
        k128 = _gather_keys(meta_ref, prev_ref, cur_ref, ATTN_WIDTH)
        v128 = _gather_keys(meta_ref, prev_ref, cur_ref, ATTN_WIDTH + KV_WIDTH)
        cos_b, sin_b = c_ref[...], s_ref[...]
        dk_heads, dv_heads = [], []
        for kv in range(2):
            k_lo, k_hi = _head_halves(k128, kv)
            v_lo, v_hi = _head_halves(v128, kv)
            lanes0, lanes1 = _pair_lanes(kv)
            q2 = jnp.concatenate([cur_ref[:, lanes0], cur_ref[:, lanes1]], axis=0)
            do2 = jnp.concatenate([do_ref[:, lanes0], do_ref[:, lanes1]], axis=0)
            ds_half, p_half = [], []
            for half, (k_h, v_h) in enumerate(((k_lo, v_lo), (k_hi, v_hi))):
                head = 4 * kv + half
                p, p_sink = _softmax_with_sink(_nt(q2, k_h), mask2, _stacked_sinks(sink_ref, head))
                dp = _nt(do2, v_h)
                dsum = jnp.sum(p * dp, axis=-1, keepdims=True)
                ds_half.append((p * (dp - dsum) * (HEAD_DIM ** -0.5)).astype(BF16))
                p_half.append(p.astype(BF16))
                dsink = p_sink * dsum
                for part, h in ((0, head), (1, head + 2)):
                    total = -jnp.sum(dsink[part * BLK:(part + 1) * BLK], axis=0, keepdims=True)
                    dsink_ref[h:h + 1, :] += jnp.broadcast_to(total, (1, 128))
            dq2 = _nn(ds_half[0], k_lo) + _nn(ds_half[1], k_hi)
            dqkv_ref[:, lanes0] = _rope(dq2[0:BLK], cos_b, sin_b, -1.0).astype(BF16)
            dqkv_ref[:, lanes1] = _rope(dq2[BLK:2 * BLK], cos_b, sin_b, -1.0).astype(BF16)
            dk_acc = jnp.where(lane < HEAD_DIM, _tn(ds_half[0], q2), _tn(ds_half[1], q2))
            dv_acc = jnp.where(lane < HEAD_DIM, _tn(p_half[0], do2), _tn(p_half[1], do2))
            dk_heads.append(dk_acc + pltpu.roll(dk_acc, HEAD_DIM, 1))
            dv_heads.append(dv_acc + pltpu.roll(dv_acc, HEAD_DIM, 1))
        dkv = jnp.concatenate([jnp.where(lane < HEAD_DIM, dk_heads[0], dk_heads[1]),
                               jnp.where(lane < HEAD_DIM, dv_heads[0], dv_heads[1])], axis=1)
        macc_ref[...] += dkv[0:BLK]
        is_last = (blk == 0).astype(F32)
        mine = dkv[2 * BLK:3 * BLK] + carry_ref[...] + is_last * macc_ref[...]
        carry_ref[...] = dkv[BLK:2 * BLK]
        dqkv_ref[:, ATTN_WIDTH:ATTN_WIDTH + KV_WIDTH] = _rope(mine[:, 0:128], cos_b, sin_b, -1.0).astype(BF16)
        dqkv_ref[:, ATTN_WIDTH + KV_WIDTH:QKV_WIDTH] = mine[:, 128:256].astype(BF16)

    rev = lambda i: nb - 1 - i
    blk_spec = lambda f: pl.BlockSpec((BLK, QKV_WIDTH), f)
    return pl.pallas_call(
        body, name=name, grid=(nb,),
        in_specs=[_SMEM, blk_spec(lambda i: (0, 0)), blk_spec(lambda i: (jnp.maximum(rev(i) - 1, 0), 0)),
                  blk_spec(lambda i: (rev(i), 0)), pl.BlockSpec((BLK, ATTN_WIDTH), lambda i: (rev(i), 0)),
                  pl.BlockSpec((BLK, 128), lambda i: (rev(i), 0)), pl.BlockSpec((BLK, 128), lambda i: (rev(i), 0))],
        out_specs=[pl.BlockSpec((BLK, QKV_WIDTH), lambda i: (rev(i), 0)), _acc_spec((N_Q_HEADS, 128))],
        out_shape=[jax.ShapeDtypeStruct((t, QKV_WIDTH), BF16), jax.ShapeDtypeStruct((N_Q_HEADS, 128), F32)],
        scratch_shapes=[pltpu.VMEM((BLK, 256), F32), pltpu.VMEM((BLK, 256), F32)],
        compiler_params=_params(("arbitrary",)),
    )(sinks, qkv, qkv, qkv, do, cos_t, sin_t)


def _cmul(ar, ai, br, bi):
    return ar * br - ai * bi, ar * bi + ai * br


def ssm_prep(a_re, a_im, log_dt, b_re_t, b_im_t, name):
    def body(ar_ref, ai_ref, ldt_ref, br_ref, bi_ref, lr_ref, li_ref, bbr_ref, bbi_ref):
        ar, ai = ar_ref[...], ai_ref[...]
        dt = jnp.exp(ldt_ref[...])
        mag = jnp.exp(ar * dt)
        lr = mag * jnp.cos(ai * dt)
        li = mag * jnp.sin(ai * dt)
        den = ar * ar + ai * ai
        nr = lr - 1.0
        cr = ((nr * ar + li * ai) / den)[:, None, :]
        ci = ((li * ar - nr * ai) / den)[:, None, :]
        br, bi = br_ref[...], bi_ref[...]
        lr_ref[...] = lr
        li_ref[...] = li
        bbr_ref[...] = cr * br - ci * bi
        bbi_ref[...] = cr * bi + ci * br

    gp = jax.ShapeDtypeStruct(a_re.shape, F32)
    gcp = jax.ShapeDtypeStruct(b_re_t.shape, F32)
    return pl.pallas_call(body, name=name, out_shape=[gp, gp, gcp, gcp],
                          in_specs=[_VMEM] * 5, out_specs=[_VMEM] * 4)(a_re, a_im, log_dt, b_re_t, b_im_t)


def ssm_prep_bwd(a_re, a_im, log_dt, b_re_t, b_im_t, dl_re, dl_im, dbb_re, dbb_im, name):
    def body(ar_ref, ai_ref, ldt_ref, br_ref, bi_ref, dlr_ref, dli_ref, dbbr_ref, dbbi_ref,
             dar_ref, dai_ref, dldt_ref, dbr_ref, dbi_ref):
        ar, ai = ar_ref[...], ai_ref[...]
        dt = jnp.exp(ldt_ref[...])
        mag = jnp.exp(ar * dt)
        lr = mag * jnp.cos(ai * dt)
        li = mag * jnp.sin(ai * dt)
        den = ar * ar + ai * ai
        nr = lr - 1.0
        cr = (nr * ar + li * ai) / den
        ci = (li * ar - nr * ai) / den
        br, bi = br_ref[...], bi_ref[...]
        dbbr, dbbi = dbbr_ref[...], dbbi_ref[...]
        dbr_ref[...] = cr[:, None, :] * dbbr + ci[:, None, :] * dbbi
        dbi_ref[...] = cr[:, None, :] * dbbi - ci[:, None, :] * dbbr
        dcr = jnp.sum(br * dbbr + bi * dbbi, axis=1)
        dci = jnp.sum(br * dbbi - bi * dbbr, axis=1)
        d_num_r = dcr / den
        d_num_i = dci / den
        d_den = -(dcr * cr + dci * ci) / den
        d_lr = dlr_ref[...] + d_num_r * ar - d_num_i * ai
        d_li = dli_ref[...] + d_num_r * ai + d_num_i * ar
        d_ar = d_num_r * nr + d_num_i * li + d_den * 2.0 * ar
        d_ai = d_num_r * li - d_num_i * nr + d_den * 2.0 * ai
        d_mag = (d_lr * lr + d_li * li) / mag
        d_theta = d_li * lr - d_lr * li
        d_ardt = d_mag * mag
        dar_ref[...] = d_ar + d_ardt * dt
        dai_ref[...] = d_ai + d_theta * dt
        d_dt = jnp.sum(d_ardt * ar + d_theta * ai, axis=1, keepdims=True)
        dldt_ref[...] = d_dt * dt

    gp = jax.ShapeDtypeStruct(a_re.shape, F32)
    gcp = jax.ShapeDtypeStruct(b_re_t.shape, F32)
    return pl.pallas_call(body, name=name, out_shape=[gp, gp, jax.ShapeDtypeStruct(log_dt.shape, F32), gcp, gcp],
                          in_specs=[_VMEM] * 9, out_specs=[_VMEM] * 5,
                          )(a_re, a_im, log_dt, b_re_t, b_im_t, dl_re, dl_im, dbb_re, dbb_im)


N_CHUNK = 4
U_CHUNK = SSM_WIDTH // N_CHUNK
H_CHUNK = STATE_WIDTH // N_CHUNK
SUB = 8


def _block_diag_b(bb):
    x = bb.reshape(N_CHUNK, 8, SSM_GROUP, 1, SSM_STATE)
    same = (jnp.arange(8)[:, None] == jnp.arange(8)[None, :])[None, :, None, :, None]
    return jnp.where(same, x, 0.0).reshape(N_CHUNK, U_CHUNK, H_CHUNK)


def _block_diag_c(c):
    x = jnp.swapaxes(c.reshape(N_CHUNK, 8, SSM_GROUP, SSM_STATE), 2, 3)[:, :, :, None, :]
    same = (jnp.arange(8)[:, None] == jnp.arange(8)[None, :])[None, :, None, :, None]
    return jnp.where(same, x, 0.0).reshape(N_CHUNK, H_CHUNK, U_CHUNK)


def _diag_of_b(m):
    x = m.reshape(N_CHUNK, 8, SSM_GROUP, 8, SSM_STATE)
    return jnp.stack([x[:, g, :, g, :] for g in range(8)], axis=1).reshape(SSM_GROUPS, SSM_GROUP, SSM_STATE)


def _diag_of_c(m):
    x = m.reshape(N_CHUNK, 8, SSM_STATE, 8, SSM_GROUP)
    d = jnp.stack([x[:, g, :, g, :] for g in range(8)], axis=1)
    return jnp.swapaxes(d, 2, 3).reshape(SSM_GROUPS, SSM_GROUP, SSM_STATE)


def _lambda_tables(lr, li, reverse):
    p1 = (lr, li)
    p2 = _cmul(*p1, *p1)
    p4 = _cmul(*p2, *p2)
    rows = [p1]
    for _ in range(SUB - 1):
        rows.append(_cmul(*rows[-1], *p1))
    if reverse:
        rows = rows[::-1]
    return p1, p2, p4, (jnp.concatenate([r[0] for r in rows], axis=0), jnp.concatenate([r[1] for r in rows], axis=0))


def _scan8(xr, xi, pows, table, cr, ci, reverse):
    row = lax.broadcasted_iota(jnp.int32, xr.shape, 0)
    for d, (pr, pi) in zip((1, 2, 4), pows):
        if reverse:
            sr, si = pltpu.roll(xr, SUB - d, 0), pltpu.roll(xi, SUB - d, 0)
            keep = row < SUB - d
        else:
            sr, si = pltpu.roll(xr, d, 0), pltpu.roll(xi, d, 0)
            keep = row >= d
        sr = jnp.where(keep, sr, 0.0)
        si = jnp.where(keep, si, 0.0)
        xr, xi = xr + pr * sr - pi * si, xi + pr * si + pi * sr
    tr, ti = table
    return xr + tr * cr - ti * ci, xi + tr * ci + ti * cr


def _gelu_and_grad(y):
    k0 = math.sqrt(2.0 / math.pi)
    inner = k0 * (y + 0.044715 * y * y * y)
    th = jnp.tanh(inner)
    g = 0.5 * y * (1.0 + th)
    dg = 0.5 * (1.0 + th) + 0.5 * y * (1.0 - th * th) * k0 * (1.0 + 3.0 * 0.044715 * y * y)
    return g, dg


SCAN_TILE = TOKEN_TILE
SEG = SCAN_TILE // SUB
SCAN_LANES = 512


def _perm_matrix(to_segments):
    a = lax.broadcasted_iota(jnp.int32, (SCAN_TILE, SCAN_TILE), 0)
    b = lax.broadcasted_iota(jnp.int32, (SCAN_TILE, SCAN_TILE), 1)
    rho, time = (a, b) if to_segments else (b, a)
    return (time == (rho % SUB) * SEG + rho // SUB).astype(BF16)


def _chunked_spec(rows, block_of):
    return pl.BlockSpec((N_CHUNK, rows, U_CHUNK), lambda i: (0, block_of(i), 0))


def _load_segments(src_ref, dst_ref):
    for j in range(N_CHUNK):
        for r in range(SEG):
            dst_ref[r * SUB:(r + 1) * SUB, j * U_CHUNK:(j + 1) * U_CHUNK] = src_ref.at[j][pl.ds(r, SUB, stride=SEG), :]


def _power_table(lr, li, pr_ref, pi_ref):
    cur = (lr, li)
    for r in range(SEG):
        pr_ref[r * SUB:(r + 1) * SUB, :] = jnp.broadcast_to(cur[0], (SUB, STATE_WIDTH))
        pi_ref[r * SUB:(r + 1) * SUB, :] = jnp.broadcast_to(cur[1], (SUB, STATE_WIDTH))
        cur = _cmul(*cur, lr, li)


def _table_rows(ref, k, lanes):
    return ref[pl.ds(pl.multiple_of(k * SUB, SUB), SUB), lanes]


def _segment_scan(xr_ref, xi_ref, lanes, lam, table_row, cr_ref, ci_ref, reverse, extra=None):
    lr = jnp.broadcast_to(lam[0], (SUB, SCAN_LANES))
    li = jnp.broadcast_to(lam[1], (SUB, SCAN_LANES))
    row = lax.broadcasted_iota(jnp.int32, (SUB, SCAN_LANES), 0)

    def rows_of(k):
        r = SEG - 1 - k if reverse else k
        return pl.ds(pl.multiple_of(r * SUB, SUB), SUB)

    def first(k, st):
        sr, si = st
        rows = rows_of(k)
        nr = lr * sr - li * si + xr_ref[rows, lanes]
        ni = lr * si + li * sr + xi_ref[rows, lanes]
        xr_ref[rows, lanes] = nr
        xi_ref[rows, lanes] = ni
        return nr, ni

    zero = jnp.zeros((SUB, SCAN_LANES), F32)
    er, ei = lax.fori_loop(0, SEG, first, (zero, zero))
    l16 = table_row(SEG - 1)
    q1, q2, q4, tab = _lambda_tables(l16[0][0:1], l16[1][0:1], reverse)
    c_r, c_i = cr_ref[:, lanes], ci_ref[:, lanes]
    gr, gi = _scan8(er, ei, (q1, q2, q4), tab, c_r, c_i, reverse)
    if reverse:
        cin_r = jnp.where(row == SUB - 1, c_r, pltpu.roll(gr, SUB - 1, 0))
        cin_i = jnp.where(row == SUB - 1, c_i, pltpu.roll(gi, SUB - 1, 0))
        cr_ref[:, lanes] = gr[0:1]
        ci_ref[:, lanes] = gi[0:1]
    else:
        cin_r = jnp.where(row == 0, c_r, pltpu.roll(gr, 1, 0))
        cin_i = jnp.where(row == 0, c_i, pltpu.roll(gi, 1, 0))
        cr_ref[:, lanes] = gr[SUB - 1:SUB]
        ci_ref[:, lanes] = gi[SUB - 1:SUB]

    def second(k, carry):
        rows = rows_of(k)
        tr, ti = table_row(k)
        ar = xr_ref[rows, lanes] + tr * cin_r - ti * cin_i
        ai = xi_ref[rows, lanes] + tr * cin_i + ti * cin_r
        xr_ref[rows, lanes] = ar
        xi_ref[rows, lanes] = ai
        if extra is None:
            return carry
        return extra(rows, carry, ar, ai)

    init = 0 if extra is None else (cin_r, cin_i, zero, zero)
    return lax.fori_loop(0, SEG, second, init)


def ssm_fwd(u, lam_re, lam_im, bb_re, bb_im, cc_re, cc_im, d_skip, name):
    t = u.shape[1]
    tt = SCAN_TILE

    def body(u_ref, lr_ref, li_ref, bbr_ref, bbi_ref, ccr_ref, cci_ref, d_ref, yg_ref, hr_ref, hi_ref,
             cr_ref, ci_ref, pr_ref, pi_ref, up_ref, y_ref):
        @pl.when(pl.program_id(0) == 0)
        def _():
            cr_ref[...] = jnp.zeros_like(cr_ref)
            ci_ref[...] = jnp.zeros_like(ci_ref)
            _power_table(lr_ref[...], li_ref[...], pr_ref, pi_ref)

        _load_segments(u_ref, up_ref)
        ub = up_ref[...].astype(BF16)
        for j in range(N_CHUNK):
            hs = slice(j * H_CHUNK, (j + 1) * H_CHUNK)
            us = slice(j * U_CHUNK, (j + 1) * U_CHUNK)
            hr_ref[:, hs] = _nn(ub[:, us], bbr_ref[j])
            hi_ref[:, hs] = _nn(ub[:, us], bbi_ref[j])
        for c in range(STATE_WIDTH // SCAN_LANES):
            lanes = slice(c * SCAN_LANES, (c + 1) * SCAN_LANES)
            _segment_scan(hr_ref, hi_ref, lanes, (lr_ref[:, lanes], li_ref[:, lanes]),
                          lambda k, lanes=lanes: (_table_rows(pr_ref, k, lanes), _table_rows(pi_ref, k, lanes)),
                          cr_ref, ci_ref, False)
        for j in range(N_CHUNK):
            hs = slice(j * H_CHUNK, (j + 1) * H_CHUNK)
            us = slice(j * U_CHUNK, (j + 1) * U_CHUNK)
            y = (_nn(hr_ref[:, hs].astype(BF16), ccr_ref[j]) - _nn(hi_ref[:, hs].astype(BF16), cci_ref[j])
                 + d_ref[:, us] * up_ref[:, us])
            y_ref[:, us] = _gelu_and_grad(y)[0]
        yg_ref[...] = _nn(_perm_matrix(False), y_ref[...].astype(BF16)).astype(BF16)

    return pl.pallas_call(
        body, name=name, grid=(t // tt,),
        in_specs=[_chunked_spec(tt, lambda i: i), _VMEM, _VMEM, _VMEM, _VMEM, _VMEM, _VMEM, _VMEM],
        out_specs=[_row_spec(tt, SSM_WIDTH), _row_spec(tt, STATE_WIDTH), _row_spec(tt, STATE_WIDTH)],
        out_shape=[jax.ShapeDtypeStruct((t, SSM_WIDTH), BF16), jax.ShapeDtypeStruct((t, STATE_WIDTH), F32),
                   jax.ShapeDtypeStruct((t, STATE_WIDTH), F32)],
        scratch_shapes=[pltpu.VMEM((1, STATE_WIDTH), F32), pltpu.VMEM((1, STATE_WIDTH), F32),
                        pltpu.VMEM((SCAN_TILE, STATE_WIDTH), F32), pltpu.VMEM((SCAN_TILE, STATE_WIDTH), F32),
                        pltpu.VMEM((tt, SSM_WIDTH), F32), pltpu.VMEM((tt, SSM_WIDTH), F32)],
        compiler_params=_params(("arbitrary",)),
    )(u, lam_re, lam_im, bb_re, bb_im, cc_re, cc_im, d_skip)


def ssm_bwd(dyg, u, h_re, h_im, lam_re, lam_im, bb_re, bb_im, cc_re, cc_im, d_skip, name):
    t = u.shape[1]
    tt = SCAN_TILE
    nt = t // tt

    def body(dyg_ref, u_ref, hr_ref, hi_ref, lr_ref, li_ref, bbr_ref, bbi_ref, ccr_ref, cci_ref, d_ref,
             du_ref, dlr_ref, dli_ref, dbbr_ref, dbbi_ref, dccr_ref, dcci_ref, dd_ref,
             ar_ref, ai_ref, cr_ref, ci_ref, pr_ref, pi_ref, up_ref, dy_ref, dup_ref):
        step = pl.program_id(0)
        tile = nt - 1 - step

        @pl.when(step == 0)
        def _():
            for ref in (cr_ref, ci_ref, dlr_ref, dli_ref, dbbr_ref, dbbi_ref, dccr_ref, dcci_ref, dd_ref):
                ref[...] = jnp.zeros_like(ref)
            _power_table(lr_ref[...], li_ref[...], pr_ref, pi_ref)

        _load_segments(u_ref, up_ref)
        _load_segments(dyg_ref, dy_ref)
        uv = up_ref[...]
        ub = uv.astype(BF16)
        dskip = d_ref[...]
        for j in range(N_CHUNK):
            hs = slice(j * H_CHUNK, (j + 1) * H_CHUNK)
            us = slice(j * U_CHUNK, (j + 1) * U_CHUNK)
            hrb = hr_ref[:, hs].astype(BF16)
            hib = hi_ref[:, hs].astype(BF16)
            y = _nn(hrb, ccr_ref[j]) - _nn(hib, cci_ref[j]) + dskip[:, us] * uv[:, us]
            dy = dy_ref[:, us] * _gelu_and_grad(y)[1]
            dy_ref[:, us] = dy
            dyb = dy.astype(BF16)
            dccr_ref[j] += _tn(hrb, dyb)
            dcci_ref[j] -= _tn(hib, dyb)
            ar_ref[:, hs] = _nt(dyb, ccr_ref[j])
            ai_ref[:, hs] = -_nt(dyb, cci_ref[j])
        dd_ref[...] += jnp.sum(dy_ref[...] * uv, axis=0, keepdims=True)

        for c in range(STATE_WIDTH // SCAN_LANES):
            lanes = slice(c * SCAN_LANES, (c + 1) * SCAN_LANES)

            def dlambda(rows, carry, ar, ai, lanes=lanes):
                nr, ni, accr, acci = carry
                hr, hi = hr_ref[rows, lanes], hi_ref[rows, lanes]
                return ar, ai, accr + nr * hr + ni * hi, acci + ni * hr - nr * hi

            _, _, accr, acci = _segment_scan(
                ar_ref, ai_ref, lanes, (lr_ref[:, lanes], -li_ref[:, lanes]),
                lambda k, lanes=lanes: (_table_rows(pr_ref, k, lanes), -_table_rows(pi_ref, k, lanes)),
                cr_ref, ci_ref, True, dlambda)
            dlr_ref[:, lanes] += accr
            dli_ref[:, lanes] += acci

        rho = lax.broadcasted_iota(jnp.int32, (tt, U_CHUNK), 0)
        time = tile * tt + (rho % SUB) * SEG + rho // SUB
        for j in range(N_CHUNK):
            hs = slice(j * H_CHUNK, (j + 1) * H_CHUNK)
            us = slice(j * U_CHUNK, (j + 1) * U_CHUNK)
            arb = ar_ref[:, hs].astype(BF16)
            aib = ai_ref[:, hs].astype(BF16)
            dbbr_ref[j] += _tn(ub[:, us], arb)
            dbbi_ref[j] += _tn(ub[:, us], aib)
            du = _nt(arb, bbr_ref[j]) + _nt(aib, bbi_ref[j]) + dy_ref[:, us] * dskip[:, us]
            dup_ref[:, us] = jnp.where(time >= PAD_FRONT, du, 0.0)
        du_ref[...] = _nn(_perm_matrix(False), dup_ref[...].astype(BF16)).astype(BF16)

    rev = lambda i: (nt - 1 - i, 0)
    full = lambda shape: pl.BlockSpec(shape, lambda i: (0,) * len(shape))
    return pl.pallas_call(
        body, name=name, grid=(nt,),
        in_specs=[_chunked_spec(tt, lambda i: nt - 1 - i), _chunked_spec(tt, lambda i: nt - 1 - i),
                  pl.BlockSpec((tt, STATE_WIDTH), rev), pl.BlockSpec((tt, STATE_WIDTH), rev),
                  _VMEM, _VMEM, _VMEM, _VMEM, _VMEM, _VMEM, _VMEM],
        out_specs=[pl.BlockSpec((tt, SSM_WIDTH), rev), full((SUB, STATE_WIDTH)), full((SUB, STATE_WIDTH)),
                   full((N_CHUNK, U_CHUNK, H_CHUNK)), full((N_CHUNK, U_CHUNK, H_CHUNK)),
                   full((N_CHUNK, H_CHUNK, U_CHUNK)), full((N_CHUNK, H_CHUNK, U_CHUNK)), full((1, SSM_WIDTH))],
        out_shape=[jax.ShapeDtypeStruct((t, SSM_WIDTH), BF16),
                   jax.ShapeDtypeStruct((SUB, STATE_WIDTH), F32), jax.ShapeDtypeStruct((SUB, STATE_WIDTH), F32),
                   jax.ShapeDtypeStruct((N_CHUNK, U_CHUNK, H_CHUNK), F32),
                   jax.ShapeDtypeStruct((N_CHUNK, U_CHUNK, H_CHUNK), F32),
                   jax.ShapeDtypeStruct((N_CHUNK, H_CHUNK, U_CHUNK), F32),
                   jax.ShapeDtypeStruct((N_CHUNK, H_CHUNK, U_CHUNK), F32),
                   jax.ShapeDtypeStruct((1, SSM_WIDTH), F32)],
        scratch_shapes=[pltpu.VMEM((tt, STATE_WIDTH), F32), pltpu.VMEM((tt, STATE_WIDTH), F32),
                        pltpu.VMEM((1, STATE_WIDTH), F32), pltpu.VMEM((1, STATE_WIDTH), F32),
                        pltpu.VMEM((SCAN_TILE, STATE_WIDTH), F32), pltpu.VMEM((SCAN_TILE, STATE_WIDTH), F32),
                        pltpu.VMEM((tt, SSM_WIDTH), F32), pltpu.VMEM((tt, SSM_WIDTH), F32),
                        pltpu.VMEM((tt, SSM_WIDTH), F32)],
        compiler_params=_params(("arbitrary",)),
    )(dyg, u, h_re, h_im, lam_re, lam_im, bb_re, bb_im, cc_re, cc_im, d_skip)


def merge_fwd(h, o, yg, gates, wap_t, wv_t, wgg_t, wout, name):
    t, d = h.shape
    tm = TOKEN_TILE

    def body(h_ref, o_ref, yg_ref, gt_ref, wap_ref, wv_ref, wgg_ref, wout_ref, ho_ref, mg_ref, a_ref, sv_ref, sg_ref):
        att = _nt(o_ref[...], wap_ref[...])
        ygv = yg_ref[...]
        sv = _nt(ygv, wv_ref[...])
        sg = _nt(ygv, wgg_ref[...])
        a_ref[...] = att.astype(BF16)
        sv_ref[...] = sv.astype(BF16)
        sg_ref[...] = sg.astype(BF16)
        merged = (jax.nn.sigmoid(gt_ref[:, 0:d].astype(F32)) * att
                  + jax.nn.sigmoid(gt_ref[:, d:2 * d].astype(F32)) * (sv * jax.nn.sigmoid(sg))).astype(BF16)
        mg_ref[...] = merged
        ho_ref[...] = h_ref[...] + _nn(merged, wout_ref[...])

    return pl.pallas_call(
        body, name=name, grid=(t // tm,),
        in_specs=[_row_spec(tm, d), _row_spec(tm, ATTN_WIDTH), _row_spec(tm, SSM_WIDTH), _row_spec(tm, 2 * d),
                  _VMEM, _VMEM, _VMEM, _VMEM],
        out_specs=[_row_spec(tm, d), _row_spec(tm, d), _row_spec(tm, d), _row_spec(tm, d), _row_spec(tm, d)],
        out_shape=[jax.ShapeDtypeStruct((t, d), F32), jax.ShapeDtypeStruct((t, d), BF16),
                   jax.ShapeDtypeStruct((t, d), BF16), jax.ShapeDtypeStruct((t, d), BF16),
                   jax.ShapeDtypeStruct((t, d), BF16)],
        compiler_params=_params(("arbitrary",)),
    )(h, o, yg, gates, wap_t, wv_t, wgg_t, wout)


def merge_bwd(dh, gates, att, sv, sg, wap_t, wv_t, wgg_t, wout, dep, name):
    t, d = dh.shape
    tm = TOKEN_TILE

    def body(dh_ref, gt_ref, a_ref, sv_ref, sg_ref, wap_ref, wv_ref, wgg_ref, wout_ref, dep_ref,
             dgt_ref, da_ref, dsv_ref, dsg_ref, do_ref, dyg_ref, dhb_ref):
        dhb = dh_ref[...].astype(BF16)
        dhb_ref[...] = dhb
        dm = _nt(dhb, wout_ref[...])
        sig_a = jax.nn.sigmoid(gt_ref[:, 0:d].astype(F32))
        sig_s = jax.nn.sigmoid(gt_ref[:, d:2 * d].astype(F32))
        sig_g = jax.nn.sigmoid(sg_ref[...].astype(F32))
        svv = sv_ref[...].astype(F32)
        dgt_ref[:, 0:d] = (dm * a_ref[...].astype(F32) * sig_a * (1.0 - sig_a)).astype(BF16)
        dgt_ref[:, d:2 * d] = (dm * (svv * sig_g) * sig_s * (1.0 - sig_s)).astype(BF16)
        da = (dm * sig_a).astype(BF16)
        d_s = dm * sig_s
        dsv = (d_s * sig_g).astype(BF16)
        dsg = (d_s * svv * sig_g * (1.0 - sig_g)).astype(BF16)
        da_ref[...] = da
        dsv_ref[...] = dsv
        dsg_ref[...] = dsg
        do_ref[...] = _nn(da, wap_ref[...]).astype(BF16)
        dyg = _nn(dsv, wv_ref[...]) + _nn(dsg, wgg_ref[...])
        for j in range(N_CHUNK):
            dyg_ref[j] = dyg[:, j * U_CHUNK:(j + 1) * U_CHUNK]

    return pl.pallas_call(
        body, name=name, grid=(t // tm,),
        in_specs=[_row_spec(tm, d), _row_spec(tm, 2 * d), _row_spec(tm, d), _row_spec(tm, d), _row_spec(tm, d),
                  _VMEM, _VMEM, _VMEM, _VMEM, _ANY],
        out_specs=[_row_spec(tm, 2 * d), _row_spec(tm, d), _row_spec(tm, d), _row_spec(tm, d),
                   _row_spec(tm, ATTN_WIDTH), _chunked_spec(tm, lambda i: i), _row_spec(tm, d)],
        out_shape=[jax.ShapeDtypeStruct((t, 2 * d), BF16), jax.ShapeDtypeStruct((t, d), BF16),
                   jax.ShapeDtypeStruct((t, d), BF16), jax.ShapeDtypeStruct((t, d), BF16),
                   jax.ShapeDtypeStruct((t, ATTN_WIDTH), BF16), jax.ShapeDtypeStruct((N_CHUNK, t, U_CHUNK), F32),
                   jax.ShapeDtypeStruct((t, d), BF16)],
        compiler_params=_params(("arbitrary",)),
    )(dh, gates, att, sv, sg, wap_t, wv_t, wgg_t, wout, dep)


def _adamw_math(w, g, m, v):
    mn = ADAM_B1 * m + (1.0 - ADAM_B1) * g
    vn = ADAM_B2 * v + (1.0 - ADAM_B2) * (g * g)
    m_hat = mn / (1.0 - ADAM_B1 ** ADAM_STEP)
    v_hat = vn / (1.0 - ADAM_B2 ** ADAM_STEP)
    return -ADAM_LR * (m_hat / (jnp.sqrt(v_hat) + ADAM_EPS) + ADAM_WD * w), mn, vn


def sum_adamw_layer(me, landed, partial, w, m, v, layer, prev, name):
    _, rows, cols = w.shape
    tr = rows // 2 if rows % 32 == 0 else rows
    steps = rows // tr

    def body(me_ref, land_ref, own_ref, w_ref, m_ref, v_ref, *rest):
        go_ref, d_ref, mo_ref, vo_ref = rest[-4:]
        who = me_ref[0]
        gv = land_ref[who ^ 1].astype(F32)
        for p in range(2, N_DEV):
            gv = gv + land_ref[who ^ p].astype(F32)
        gv = gv + own_ref[...].astype(F32)
        go_ref[0] = gv
        d_ref[0], mo_ref[0], vo_ref[0] = _adamw_math(w_ref[0], gv, m_ref[0], v_ref[0])

    spec3 = pl.BlockSpec((1, tr, cols), lambda r, me_ref: (layer, r, 0))
    out = jax.ShapeDtypeStruct(w.shape, F32)
    extra = [] if prev is None else list(prev)
    grid_spec = pltpu.PrefetchScalarGridSpec(
        num_scalar_prefetch=1, grid=(steps,),
        in_specs=[pl.BlockSpec((N_DEV, tr, cols), lambda r, me_ref: (0, r, 0)),
                  pl.BlockSpec((tr, cols), lambda r, me_ref: (me_ref[0] * steps + r, 0)),
                  spec3, spec3, spec3] + [_ANY] * len(extra),
        out_specs=[spec3] * 4)
    return pl.pallas_call(
        body, name=name, grid_spec=grid_spec, out_shape=[out] * 4,
        input_output_aliases={6 + j: j for j in range(len(extra))},
        compiler_params=_params(("arbitrary",)),
    )(me, landed, partial, w, m, v, *extra)


def adamw(w, g, m, v, name, minor_swap=False):
    if minor_swap:
        d, mn, vn = adamw(*[jnp.swapaxes(a, -1, -2) for a in (w, g, m, v)], name)
        return jnp.swapaxes(d, -1, -2), jnp.swapaxes(mn, -1, -2), jnp.swapaxes(vn, -1, -2)
    shape = w.shape
    as2d = lambda a: a.reshape(-1, shape[-1]) if a.ndim >= 2 else a.reshape(1, -1)
    w2, g2, m2, v2 = as2d(w), as2d(g), as2d(m), as2d(v)
    rows, cols = w2.shape
    tr = rows
    for cand in (1024, 704, 512, 256):
        if rows > cand and rows % cand == 0:
            tr = cand
            break

    def body(w_ref, g_ref, m_ref, v_ref, d_ref, mo_ref, vo_ref):
        d_ref[...], mo_ref[...], vo_ref[...] = _adamw_math(w_ref[...], g_ref[...], m_ref[...], v_ref[...])

    spec = _row_spec(tr, cols)
    out = jax.ShapeDtypeStruct((rows, cols), F32)
    d, mn, vn = pl.pallas_call(
        body, name=name, grid=(rows // tr,), in_specs=[spec] * 4, out_specs=[spec] * 3, out_shape=[out] * 3,
        compiler_params=_params(("arbitrary",)),
    )(w2, g2, m2, v2)
    return d.reshape(shape), mn.reshape(shape), vn.reshape(shape)


def _my_index():
    return 4 * lax.axis_index("x") + 2 * lax.axis_index("y") + lax.axis_index("c")


def _peer(p):
    return (lax.axis_index("x") ^ ((p >> 2) & 1), lax.axis_index("y") ^ ((p >> 1) & 1), lax.axis_index("c") ^ (p & 1))


_HBM = pl.BlockSpec(memory_space=pltpu.HBM)
_SEM = pl.BlockSpec(memory_space=pltpu.SEMAPHORE)
_EFFECT = pltpu.SideEffectType.DATAFLOW_SIDE_EFFECTING


class Exchange:
    RELAYED = (2, 4, 6)

    def __init__(self, srcs, scatter, name, relay=False):
        self.n = n = len(srcs)
        self.scatter = scatter
        self.name = name
        self.relayed = relay
        assert not (relay and scatter)
        self.direct = (1,) + self.RELAYED if relay else tuple(range(1, N_DEV))
        widths = sorted({s.shape[1] for s in srcs}, reverse=True)
        self.ncls = len(widths)
        self.cls = [widths.index(s.shape[1]) for s in srcs]
        self.cnts = [s.shape[0] // N_DEV if scatter else s.shape[0] for s in srcs]
        self.totals = [sum(c for c, k in zip(self.cnts, self.cls) if k == w) for w in range(self.ncls)]
        self.sizer = [max((k for k in range(n) if self.cls[k] == w), key=lambda k: self.cnts[k])
                      for w in range(self.ncls)]
        assert all(N_DEV * self.cnts[self.sizer[w]] >= self.totals[w] for w in range(self.ncls))
        if scatter:
            self.land_shapes = [(N_DEV, c, s.shape[1]) for s, c in zip(srcs, self.cnts)]
        else:
            self.land_shapes = [(N_DEV * c, s.shape[1]) for s, c in zip(srcs, self.cnts)]
        self.dtypes = [s.dtype for s in srcs]

    def _block(self, k, who):
        return pl.ds(pl.multiple_of(who * self.cnts[k], 16), self.cnts[k])

    def _sem(self, p, w):
        return (p - 1) * self.ncls + w

    def start(self, srcs, after):
        n = self.n

        def body(*refs):
            src, land = refs[:n], refs[n:2 * n]
            send_sems, recv_sems = refs[2 * n + 1], refs[2 * n + 2]
            token = refs[-1]
            me = _my_index()
            for p in self.direct:
                for k in range(n):
                    if self.scatter:
                        s_ref, d_ref = src[k].at[self._block(k, me ^ p), :], land[k].at[me]
                    else:
                        s_ref, d_ref = src[k], land[k].at[self._block(k, me), :]
                    pltpu.make_async_remote_copy(
                        src_ref=s_ref, dst_ref=d_ref, send_sem=send_sems.at[self._sem(p, self.cls[k])],
                        recv_sem=recv_sems.at[self._sem(p, self.cls[k])], device_id=_peer(p),
                        device_id_type=MESH).start()
            token[...] = jnp.zeros_like(token)

        sems = pltpu.SemaphoreType.DMA(((N_DEV - 1) * self.ncls,))
        thru = [pltpu.HBM(s.shape, s.dtype) for s in srcs] + [pltpu.HBM(shp, dt) for shp, dt in
                                                               zip(self.land_shapes, self.dtypes)]
        lands = [pltpu.with_memory_space_constraint(lax.empty(shp, dt), pltpu.HBM)
                 for shp, dt in zip(self.land_shapes, self.dtypes)]
        out = pl.pallas_call(
            body, name=self.name + "_start",
            in_specs=[_HBM] * (2 * n) + [_ANY],
            out_shape=[sems, sems] + thru + [jax.ShapeDtypeStruct((8, 128), F32)],
            out_specs=[_SEM, _SEM] + [_HBM] * (2 * n) + [_VMEM],
            input_output_aliases={j: 2 + j for j in range(2 * n)},
            compiler_params=pltpu.CompilerParams(has_side_effects=_EFFECT),
        )(*[pltpu.with_memory_space_constraint(s, pltpu.HBM) for s in srcs], *lands, after)
        return out[:-1], out[-1]

    def _span_copy(self, src, land, w, send_sem, recv_sem, p):
        big = src[self.sizer[w]] if self.scatter else land[self.sizer[w]]
        span = big.at[pl.ds(0, self.totals[w]), :]
        return pltpu.make_async_remote_copy(src_ref=span, dst_ref=span, send_sem=send_sem, recv_sem=recv_sem,
                                            device_id=_peer(p), device_id_type=MESH)

    def relay(self, state, after):
        n = self.n
        send_sems, recv_sems = state[0], state[1]
        thru = state[2:]
        after = list(after) if isinstance(after, (list, tuple)) else [after]
        first_out = 2 * n + 2 + len(after)

        def body(*refs):
            land = refs[n:2 * n]
            send_a, recv_a = refs[2 * n], refs[2 * n + 1]
            send_b, recv_b = refs[first_out], refs[first_out + 1]
            refs[-1][...] = jnp.zeros_like(refs[-1])
            me = _my_index()
            for p in self.RELAYED:
                for w in range(self.ncls):
                    self._span_copy(None, land, w, send_a.at[self._sem(p, w)], recv_a.at[self._sem(p, w)], p).wait_recv()
            for j, p in enumerate(self.RELAYED):
                for k in range(n):
                    rows = land[k].at[self._block(k, me ^ p), :]
                    pltpu.make_async_remote_copy(
                        src_ref=rows, dst_ref=rows, send_sem=send_b.at[j * self.ncls + self.cls[k]],
                        recv_sem=recv_b.at[j * self.ncls + self.cls[k]], device_id=_peer(1),
                        device_id_type=MESH).start()

        sems = pltpu.SemaphoreType.DMA((len(self.RELAYED) * self.ncls,))
        out = pl.pallas_call(
            body, name=self.name + "_relay",
            in_specs=[_HBM] * (2 * n) + [_SEM, _SEM] + [_ANY] * len(after),
            out_shape=[sems, sems] + [pltpu.HBM(a.shape, a.dtype) for a in thru] + [jax.ShapeDtypeStruct((8, 128), F32)],
            out_specs=[_SEM, _SEM] + [_HBM] * (2 * n) + [_VMEM],
            input_output_aliases={j: 2 + j for j in range(2 * n)},
            compiler_params=pltpu.CompilerParams(has_side_effects=_EFFECT),
        )(*thru, send_sems, recv_sems, *after)
        return [send_sems, recv_sems] + list(out[2:-1]) + [out[0], out[1]], out[-1]

    def wait(self, state, after):
        n = self.n
        send_sems, recv_sems = state[0], state[1]
        thru = state[2:2 + 2 * n]
        relay_sems = list(state[2 + 2 * n:])
        assert len(relay_sems) == (2 if self.relayed else 0)
        after = list(after) if isinstance(after, (list, tuple)) else [after]

        def body(*refs):
            src, land = refs[:n], refs[n:2 * n]
            send_a, recv_a = refs[2 * n], refs[2 * n + 1]
            for p in self.direct:
                for w in range(self.ncls):
                    copy = self._span_copy(src, land, w, send_a.at[self._sem(p, w)], recv_a.at[self._sem(p, w)], p)
                    copy.wait_send()
                    if not (self.relayed and p in self.RELAYED):
                        copy.wait_recv()
            if self.relayed:
                send_b, recv_b = refs[2 * n + 2], refs[2 * n + 3]
                for j in range(len(self.RELAYED)):
                    for w in range(self.ncls):
                        copy = self._span_copy(src, land, w, send_b.at[j * self.ncls + w],
                                               recv_b.at[j * self.ncls + w], 1)
                        copy.wait_send()
                        copy.wait_recv()

        out = pl.pallas_call(
            body, name=self.name + "_wait",
            in_specs=[_HBM] * (2 * n) + [_SEM] * (2 + len(relay_sems)) + [_ANY] * len(after),
            out_shape=[pltpu.HBM(a.shape, a.dtype) for a in thru], out_specs=[_HBM] * (2 * n),
            input_output_aliases={j: j for j in range(2 * n)},
            compiler_params=pltpu.CompilerParams(has_side_effects=_EFFECT),
        )(*thru, send_sems, recv_sems, *relay_sems, *after)
        return out[:n], out[n:]

    def place(self, lands, srcs):
        n = self.n
        assert not self.scatter

        def body(*refs):
            src, land = refs[n:2 * n], refs[2 * n:3 * n]
            bufs, sems = refs[3 * n:4 * n], refs[-1]
            me = _my_index()
            loads = [pltpu.make_async_copy(src[k], bufs[k], sems.at[k]) for k in range(n)]
            stores = [pltpu.make_async_copy(bufs[k], land[k].at[self._block(k, me), :], sems.at[k]) for k in range(n)]
            for cp in loads:
                cp.start()
            for k in range(n):
                loads[k].wait()
                stores[k].start()
            for cp in stores:
                cp.wait()

        return pl.pallas_call(
            body, name=self.name + "_place", in_specs=[_ANY] * (2 * n), out_specs=[_ANY] * n,
            out_shape=[jax.ShapeDtypeStruct(a.shape, a.dtype) for a in lands],
            input_output_aliases={j: j for j in range(n)},
            scratch_shapes=[pltpu.VMEM(s.shape, s.dtype) for s in srcs] + [pltpu.SemaphoreType.DMA((n,))],
        )(*lands, *srcs)


def sum_slots(slots, name):
    _, rows, cols = slots.shape
    tr = rows
    if rows > 512:
        for cand in (256, 128, 64, 32, 16, 8):
            if rows % cand == 0:
                tr = cand
                break

    def body(s_ref, o_ref):
        acc = s_ref[0].astype(F32)
        for j in range(1, N_DEV):
            acc = acc + s_ref[j].astype(F32)
        o_ref[...] = acc

    return pl.pallas_call(
        body, name=name, grid=(rows // tr,),
        in_specs=[pl.BlockSpec((N_DEV, tr, cols), lambda i: (0, i, 0))], out_specs=_row_spec(tr, cols),
        out_shape=jax.ShapeDtypeStruct((rows, cols), F32), compiler_params=_params(("arbitrary",)),
    )(slots)


BIG_N = ("ffn1_w_down", "w_out", "ffn2_w_down")
SMALL = ("ffn1_norm", "mix_norm", "attn_sinks", "ssm_a_re", "ssm_a_im", "ssm_log_dt", "ssm_b_re", "ssm_b_im",
         "ssm_c_re", "ssm_c_im", "ssm_d", "ffn2_norm", "final_norm")
PARTS = {"ffn1": ("ffn1_w_gate", "ffn1_w_up", "ffn1_w_down"),
         "mix": ("w_in", "w_out", "w_attn_proj", "w_glu_v", "w_glu_g"),
         "ffn2": ("ffn2_w_gate", "ffn2_w_up", "ffn2_w_down")}


def _to_rows(name, a):
    return a if name in BIG_N else jnp.swapaxes(a, -1, -2)


def local_step(x, tgt, get_weights, put_grads, small):
    seq, d = x.shape
    t = PAD_FRONT + N_META + seq
    cos_t, sin_t = rope_tables(t)
    row = lambda a: a.reshape(1, -1)
    tables = []
    for i in range(DEPTH):
        b_re_t = jnp.swapaxes(small["ssm_b_re"][i], 1, 2)
        b_im_t = jnp.swapaxes(small["ssm_b_im"][i], 1, 2)
        lam_re, lam_im, bbar_re, bbar_im = ssm_prep(small["ssm_a_re"][i], small["ssm_a_im"][i],
                                                    small["ssm_log_dt"][i].reshape(-1, 1), b_re_t, b_im_t, f"ssm_prep_{i}")
        tables.append(((b_re_t, b_im_t),
                       (row(lam_re), row(lam_im), _block_diag_b(bbar_re).astype(BF16), _block_diag_b(bbar_im).astype(BF16),
                        _block_diag_c(small["ssm_c_re"][i]).astype(BF16), _block_diag_c(small["ssm_c_im"][i]).astype(BF16),
                        row(small["ssm_d"][i]))))
    early = [cos_t, sin_t] + [a for _, tab in tables for a in tab[2:6]]
    saved = []
    h = None
    for i in range(DEPTH):
        s = {}
        w = dict(get_weights(i, "ffn1", early if i == 0 else h))
        if i == 0:
            h = jnp.concatenate([jnp.zeros((PAD_FRONT, d), F32), w["meta_tokens"], x], axis=0)
        s["h0"] = h
        h, s["n1"], s["acts1"] = ffn_fwd(h, row(small["ffn1_norm"][i]), w["ffn1_w_gate"], w["ffn1_w_up"],
                                               w["ffn1_w_down"], f"ffn1_fwd_{i}")
        s["h1"] = h
        w.update(get_weights(i, "mix", h))
        s["n2"], s["qkv"], s["u"], s["gates"] = win_fwd(h, row(small["mix_norm"][i]), w["w_in"], cos_t, sin_t,
                                                        f"win_fwd_{i}")
        s["b_t"], s["ssm"] = tables[i]
        s["yg"], s["h_re"], s["h_im"] = ssm_fwd(s["u"], *s["ssm"], f"ssm_fwd_{i}")
        s["o"] = attn_fwd(s["qkv"], row(small["attn_sinks"][i]), f"attn_fwd_{i}")
        h, s["merged"], s["att"], s["sv"], s["sg"] = merge_fwd(
            h, s["o"], s["yg"], s["gates"], w["w_attn_proj"], w["w_glu_v"], w["w_glu_g"], w["w_out"],
            f"merge_fwd_{i}")
        s["h2"] = h
        w.update(get_weights(i, "ffn2", h))
        h, s["n3"], s["acts3"] = ffn_fwd(h, row(small["ffn2_norm"][i]), w["ffn2_w_gate"], w["ffn2_w_up"],
                                               w["ffn2_w_down"], f"ffn2_fwd_{i}")
        s["w"] = w
        saved.append(s)

    loss, dh, d_final = head_fwd_bwd(h, row(small["final_norm"]), tgt)
    gs = {k: [None] * DEPTH for k in SMALL if k != "final_norm"}
    dep = loss
    for i in reversed(range(DEPTH)):
        s = saved[i]
        w = s["w"]
        dh, da, db, sact, dhb, dg = ffn_bwd(dh, s["h2"], row(small["ffn2_norm"][i]), s["acts3"], w["ffn2_w_gate"],
                                            w["ffn2_w_up"], w["ffn2_w_down"], dep, f"ffn2_bwd_{i}")
        gs["ffn2_norm"][i] = dg[0]
        dep = put_grads(i, "ffn2", {"ffn2_w_gate": tn_matmul(da, s["n3"], f"ffn2_dwg_{i}"),
                                    "ffn2_w_up": tn_matmul(db, s["n3"], f"ffn2_dwu_{i}"),
                                    "ffn2_w_down": tn_matmul(sact, dhb, f"ffn2_dwd_{i}")})

        dgates, datt, dsv, dsg, do, dyg, dhb = merge_bwd(dh, s["gates"], s["att"], s["sv"], s["sg"], w["w_attn_proj"],
                                                         w["w_glu_v"], w["w_glu_g"], w["w_out"], dep, f"merge_bwd_{i}")
        gmix = {"w_out": tn_matmul(s["merged"], dhb, f"dwout_{i}"),
                "w_attn_proj": tn_matmul(datt, s["o"], f"dwap_{i}"),
                "w_glu_v": tn_matmul(dsv, s["yg"], f"dwv_{i}"),
                "w_glu_g": tn_matmul(dsg, s["yg"], f"dwgg_{i}")}
        dqkv, dsink = attn_bwd(s["qkv"], do, row(small["attn_sinks"][i]), cos_t, sin_t, f"attn_bwd_{i}")
        gs["attn_sinks"][i] = dsink[:, 0]
        du, dl_re, dl_im, dbb_re, dbb_im, dcc_re, dcc_im, dd = ssm_bwd(dyg, s["u"], s["h_re"], s["h_im"], *s["ssm"],
                                                                      f"ssm_bwd_{i}")
        fold = lambda a: jnp.sum(a, axis=0).reshape(SSM_GROUPS, SSM_STATE)
        da_re, da_im, dldt, db_re_t, db_im_t = ssm_prep_bwd(
            small["ssm_a_re"][i], small["ssm_a_im"][i], small["ssm_log_dt"][i].reshape(-1, 1), *s["b_t"],
            fold(dl_re), fold(dl_im), _diag_of_b(dbb_re), _diag_of_b(dbb_im), f"ssm_prep_bwd_{i}")
        gs["ssm_a_re"][i], gs["ssm_a_im"][i], gs["ssm_log_dt"][i] = da_re, da_im, dldt[:, 0]
        gs["ssm_b_re"][i], gs["ssm_b_im"][i] = jnp.swapaxes(db_re_t, 1, 2), jnp.swapaxes(db_im_t, 1, 2)
        gs["ssm_c_re"][i], gs["ssm_c_im"][i] = _diag_of_c(dcc_re), _diag_of_c(dcc_im)
        gs["ssm_d"][i] = dd[0]
        gmix["w_in"] = tn_matmul([dqkv, du, dgates], s["n2"], f"dwin_{i}")
        dep = put_grads(i, "mix", gmix)
        dh, dg = win_bwd(dh, s["h1"], row(small["mix_norm"][i]), dqkv, du, dgates, w["w_in"], dep, f"win_bwd_{i}")
        gs["mix_norm"][i] = dg[0]

        dh, da, db, sact, dhb, dg = ffn_bwd(dh, s["h0"], row(small["ffn1_norm"][i]), s["acts1"], w["ffn1_w_gate"],
                                            w["ffn1_w_up"], w["ffn1_w_down"], dep, f"ffn1_bwd_{i}")
        gs["ffn1_norm"][i] = dg[0]
        if i > 0:
            dep = put_grads(i, "ffn1", {"ffn1_w_gate": tn_matmul(da, s["n1"], f"ffn1_dwg_{i}"),
                                        "ffn1_w_up": tn_matmul(db, s["n1"], f"ffn1_dwu_{i}"),
                                        "ffn1_w_down": tn_matmul(sact, dhb, f"ffn1_dwd_{i}")})
        else:
            for k, xa, ya in (("ffn1_w_down", sact, dhb), ("ffn1_w_gate", da, s["n1"]), ("ffn1_w_up", db, s["n1"])):
                dep = put_grads(i, "ffn1", {k: tn_matmul(xa, ya, f"d_{k}_{i}", dep)})

    gs = {k: jnp.stack(v) for k, v in gs.items()}
    gs["final_norm"] = d_final[0]
    return loss[0, 0], dh[PAD_FRONT + N_META:], dh[PAD_FRONT:PAD_FRONT + N_META], gs, dep


def _pack_rows(arrays, cols):
    flat = jnp.concatenate([a.reshape(-1) for a in arrays])
    rows = -(-flat.shape[0] // cols)
    rows = -(-rows // 16) * 16
    return jnp.pad(flat, (0, rows * cols - flat.shape[0])).reshape(rows, cols)


def _unpack_rows(packed, shapes):
    flat = packed.reshape(-1)
    out, off = [], 0
    for shp in shapes:
        n = math.prod(shp)
        out.append(flat[off:off + n].reshape(shp))
        off += n
    return out


def kernel(x, meta_tokens, ffn1_norm, ffn1_w_gate, ffn1_w_up, ffn1_w_down, mix_norm, w_in, attn_sinks, ssm_a_re, ssm_a_im, ssm_log_dt, ssm_b_re, ssm_b_im, ssm_c_re, ssm_c_im, ssm_d, w_attn_proj, w_glu_v, w_glu_g, w_out, ffn2_norm, ffn2_w_gate, ffn2_w_up, ffn2_w_down, final_norm, loss_target, m_meta_tokens, m_ffn1_norm, m_ffn1_w_gate, m_ffn1_w_up, m_ffn1_w_down, m_mix_norm, m_w_in, m_attn_sinks, m_ssm_a_re, m_ssm_a_im, m_ssm_log_dt, m_ssm_b_re, m_ssm_b_im, m_ssm_c_re, m_ssm_c_im, m_ssm_d, m_w_attn_proj, m_w_glu_v, m_w_glu_g, m_w_out, m_ffn2_norm, m_ffn2_w_gate, m_ffn2_w_up, m_ffn2_w_down, m_final_norm, v_meta_tokens, v_ffn1_norm, v_ffn1_w_gate, v_ffn1_w_up, v_ffn1_w_down, v_mix_norm, v_w_in, v_attn_sinks, v_ssm_a_re, v_ssm_a_im, v_ssm_log_dt, v_ssm_b_re, v_ssm_b_im, v_ssm_c_re, v_ssm_c_im, v_ssm_d, v_w_attn_proj, v_w_glu_v, v_w_glu_g, v_w_out, v_ffn2_norm, v_ffn2_w_gate, v_ffn2_w_up, v_ffn2_w_down, v_final_norm):
    names = ("meta_tokens", "ffn1_norm", "ffn1_w_gate", "ffn1_w_up", "ffn1_w_down", "mix_norm", "w_in", "attn_sinks",
             "ssm_a_re", "ssm_a_im", "ssm_log_dt", "ssm_b_re", "ssm_b_im", "ssm_c_re", "ssm_c_im", "ssm_d",
             "w_attn_proj", "w_glu_v", "w_glu_g", "w_out", "ffn2_norm", "ffn2_w_gate", "ffn2_w_up", "ffn2_w_down",
             "final_norm")
    weights = dict(zip(names, (meta_tokens, ffn1_norm, ffn1_w_gate, ffn1_w_up, ffn1_w_down, mix_norm, w_in, attn_sinks, ssm_a_re, ssm_a_im, ssm_log_dt, ssm_b_re, ssm_b_im, ssm_c_re, ssm_c_im, ssm_d, w_attn_proj, w_glu_v, w_glu_g, w_out, ffn2_norm, ffn2_w_gate, ffn2_w_up, ffn2_w_down, final_norm)))
    moments_m = dict(zip(names, (m_meta_tokens, m_ffn1_norm, m_ffn1_w_gate, m_ffn1_w_up, m_ffn1_w_down, m_mix_norm, m_w_in, m_attn_sinks, m_ssm_a_re, m_ssm_a_im, m_ssm_log_dt, m_ssm_b_re, m_ssm_b_im, m_ssm_c_re, m_ssm_c_im, m_ssm_d, m_w_attn_proj, m_w_glu_v, m_w_glu_g, m_w_out, m_ffn2_norm, m_ffn2_w_gate, m_ffn2_w_up, m_ffn2_w_down, m_final_norm)))
    moments_v = dict(zip(names, (v_meta_tokens, v_ffn1_norm, v_ffn1_w_gate, v_ffn1_w_up, v_ffn1_w_down, v_mix_norm, v_w_in, v_attn_sinks, v_ssm_a_re, v_ssm_a_im, v_ssm_log_dt, v_ssm_b_re, v_ssm_b_im, v_ssm_c_re, v_ssm_c_im, v_ssm_d, v_w_attn_proj, v_w_glu_v, v_w_glu_g, v_w_out, v_ffn2_norm, v_ffn2_w_gate, v_ffn2_w_up, v_ffn2_w_down, v_final_norm)))
    me = _my_index()

    order = [(i, part) for i in range(DEPTH) for part in PARTS]
    gathers = {}
    token = jnp.zeros((8, 128), F32)
    for i, part in order:
        shards = [_to_rows(k, weights[k][i]).astype(BF16) for k in PARTS[part]]
        if (i, part) == order[0]:
            shards.append(meta_tokens)
        ex = Exchange(shards, False, f"gather_{part}_{i}", relay=True)
        state, token = ex.start(shards, token)
        gathers[i, part] = [ex, state, False]
    all_started = token

    def relay(group, after):
        ex, state, relayed = gathers[group]
        if relayed:
            return []
        new_state, relay_token = ex.relay(state, after)
        gathers[group][1:] = [new_state, True]
        return [relay_token]

    def get_weights(i, part, after):
        g = order.index((i, part))
        after = [all_started] + list(after) if g == 0 else [after]
        tokens = relay(order[g], after)
        if g >= 2 and g + 1 < len(order):
            tokens += relay(order[g + 1], after)
        ex, state, _ = gathers[i, part]
        shards, lands = ex.wait(state, after + tokens)
        fulls = ex.place(lands, shards)
        got = dict(zip(PARTS[part], fulls))
        if (i, part) == (0, "ffn1"):
            got["meta_tokens"] = jnp.swapaxes(fulls[-1].reshape(N_DEV, N_META, 128), 0, 1).reshape(N_META, D_MODEL)
        return got

    scatters = []

    def put_grads(i, part, gdict):
        ks = list(gdict)
        srcs = [gdict[k] for k in ks]
        ex = Exchange(srcs, True, f"scatter_{part if len(ks) > 1 else ks[0]}_{i}")
        state, tok = ex.start(srcs, all_started)
        scatters.append((i, ks, ex, state))
        return tok

    small = {k: weights[k] for k in SMALL}
    loss, dx, dmeta, gs, last_started = local_step(x[0], loss_target[0], get_weights, put_grads, small)

    grads, deltas, new_m, new_v = {}, {}, {}, {}
    small_list = [loss.reshape(1), dmeta] + [gs[k] for k in SMALL]
    packed = _pack_rows(small_list, D_MODEL)
    small_ex = Exchange([packed], False, "gather_small", relay=True)
    small_state, after = small_ex.start([packed], last_started)

    updated = {}
    me_index = jnp.reshape(me, (1,)).astype(jnp.int32)
    for i, ks, ex, state in scatters:
        partials, lands = ex.wait(state, after)
        for k, partial, slots in zip(ks, partials, lands):
            updated[k] = sum_adamw_layer(me_index, slots, partial, _to_rows(k, weights[k]), _to_rows(k, moments_m[k]),
                                         _to_rows(k, moments_v[k]), i, updated.get(k), f"adamw_{k}_{i}")
            after = updated[k][0]
    for k, outs in updated.items():
        grads[k], deltas[k], new_m[k], new_v[k] = [_to_rows(k, a) for a in outs]

    small_state, relayed = small_ex.relay(small_state, after)
    packed_own, packed_all = small_ex.wait(small_state, [after, relayed])
    (packed_all,) = small_ex.place(packed_all, packed_own)
    total = sum_slots(packed_all.reshape(N_DEV, packed.shape[0], D_MODEL), "sum_small")
    pieces = _unpack_rows(total, [a.shape for a in small_list])
    loss_out = pieces[0][0]
    grads["meta_tokens"] = lax.dynamic_slice_in_dim(pieces[1], me * 128, 128, axis=1)
    for k, p in zip(SMALL, pieces[2:]):
        grads[k] = p
    for k in ("meta_tokens",) + SMALL:
        deltas[k], new_m[k], new_v[k] = adamw(weights[k], grads[k], moments_m[k], moments_v[k], f"adamw_{k}",
                                              minor_swap=k in ("ssm_b_re", "ssm_b_im"))
    return (loss_out, dx[None], *[grads[k] for k in names], *[deltas[k] for k in names],
            *[new_m[k] for k in names], *[new_v[k] for k in names])
```

```python
import math

import jax
import jax.numpy as jnp
from jax import lax
from jax.experimental import pallas as pl
from jax.experimental.pallas import tpu as pltpu

F32 = jnp.float32
BF16 = jnp.bfloat16

D_MODEL = 1024
DEPTH = 2
N_META = 16
HEAD_DIM = 64
N_Q_HEADS = 8
ATTN_WIDTH = 512
KV_WIDTH = 128
QKV_WIDTH = ATTN_WIDTH + 2 * KV_WIDTH
WINDOW = 128
BLK = 128
ROPE_THETA = 500000.0
ROT_DIM = 16
SSM_WIDTH = 512
SSM_GROUP = 16
SSM_GROUPS = 32
SSM_STATE = 64
STATE_WIDTH = SSM_GROUPS * SSM_STATE
D_FF = 2816
IN_WIDTH = 3328
EPS = 1e-6
NEG_INF = -1e30
PAD_FRONT = (-N_META) % BLK
N_DEV = 8

ADAM_LR = 0.001
ADAM_B1 = 0.9
ADAM_B2 = 0.999
ADAM_EPS = 1e-08
ADAM_WD = 0.01
ADAM_STEP = 10

VMEM_LIMIT = 56 * 1024 * 1024
TOKEN_TILE = 384
_VMEM = pl.BlockSpec(memory_space=pltpu.VMEM)
_SMEM = pl.BlockSpec(memory_space=pltpu.SMEM)
_ANY = pl.BlockSpec(memory_space=pl.ANY)
MESH = pl.DeviceIdType.MESH


def _params(sem=None):
    return pltpu.CompilerParams(dimension_semantics=sem, vmem_limit_bytes=VMEM_LIMIT)


def _nt(a, b):
    return lax.dot_general(a, b, (((1,), (1,)), ((), ())), preferred_element_type=F32)


def _nn(a, b):
    return jnp.dot(a, b, preferred_element_type=F32)


def _tn(a, b):
    return lax.dot_general(a, b, (((0,), (0,)), ((), ())), preferred_element_type=F32)


def _row_spec(tm, width):
    return pl.BlockSpec((tm, width), lambda i: (i, 0))


def _acc_spec(shape):
    return pl.BlockSpec(shape, lambda i: (0,) * len(shape))


def _rms_stats(x):
    r = lax.rsqrt(jnp.mean(x * x, axis=-1, keepdims=True) + EPS)
    return x * r, r


def _rms_bwd(dn, xh, r, g):
    dg = jnp.sum(dn * xh, axis=0, keepdims=True)
    dxh = dn * g
    dx = r * (dxh - xh * jnp.mean(dxh * xh, axis=-1, keepdims=True))
    return dx, dg


def ffn_fwd(h, g, wg_t, wu_t, wd, name):
    t, d = h.shape
    f = wd.shape[0]
    tm = TOKEN_TILE

    def body(h_ref, g_ref, wg_ref, wu_ref, wd_ref, ho_ref, n_ref, sl_ref, p_ref, s_ref):
        x = h_ref[...]
        xh, _ = _rms_stats(x)
        n = (xh * g_ref[...]).astype(BF16)
        n_ref[...] = n
        a = _nt(n, wg_ref[...])
        b = _nt(n, wu_ref[...])
        sig = jax.nn.sigmoid(a)
        sl = a * sig
        sl_ref[...] = sl.astype(BF16)
        p_ref[...] = (b * (sig + sl * (1.0 - sig))).astype(BF16)
        s = (sl * b).astype(BF16)
        s_ref[...] = s
        ho_ref[...] = x + 0.5 * _nn(s, wd_ref[...])

    ho, n, sl, p, s = pl.pallas_call(
        body, name=name, grid=(t // tm,),
        in_specs=[_row_spec(tm, d), _acc_spec((1, d)), _VMEM, _VMEM, _VMEM],
        out_specs=[_row_spec(tm, d), _row_spec(tm, d), _row_spec(tm, f), _row_spec(tm, f), _row_spec(tm, f)],
        out_shape=[jax.ShapeDtypeStruct((t, d), F32), jax.ShapeDtypeStruct((t, d), BF16),
                   jax.ShapeDtypeStruct((t, f), BF16), jax.ShapeDtypeStruct((t, f), BF16),
                   jax.ShapeDtypeStruct((t, f), BF16)],
        compiler_params=_params(("arbitrary",)),
    )(h, g, wg_t, wu_t, wd)
    return ho, n, (sl, p, s)


def ffn_bwd(dh, h, g, acts, wg_t, wu_t, wd, dep, name):
    t, d = h.shape
    f = wd.shape[0]
    tm = TOKEN_TILE
    sl, p, s = acts

    def hidden_body(dh_ref, sl_ref, p_ref, wd_ref, dep_ref, da_ref, db_ref, dhb_ref):
        dhb = (0.5 * dh_ref[...]).astype(BF16)
        dhb_ref[...] = dhb
        ds = _nt(dhb, wd_ref[...])
        da_ref[...] = (ds * p_ref[...].astype(F32)).astype(BF16)
        db_ref[...] = (ds * sl_ref[...].astype(F32)).astype(BF16)

    da, db, dhb = pl.pallas_call(
        hidden_body, name=name + "_h", grid=(t // tm,),
        in_specs=[_row_spec(tm, d), _row_spec(tm, f), _row_spec(tm, f), _VMEM, _ANY],
        out_specs=[_row_spec(tm, f), _row_spec(tm, f), _row_spec(tm, d)],
        out_shape=[jax.ShapeDtypeStruct((t, f), BF16), jax.ShapeDtypeStruct((t, f), BF16),
                   jax.ShapeDtypeStruct((t, d), BF16)],
        compiler_params=_params(("arbitrary",)),
    )(dh, sl, p, wd, dep)

    def input_body(dh_ref, h_ref, g_ref, da_ref, db_ref, wg_ref, wu_ref, dhi_ref, dg_ref):
        dn = _nn(da_ref[...], wg_ref[...]) + _nn(db_ref[...], wu_ref[...])
        xh, r = _rms_stats(h_ref[...])
        dx, dg = _rms_bwd(dn, xh, r, g_ref[...])
        dhi_ref[...] = dh_ref[...] + dx

        @pl.when(pl.program_id(0) == 0)
        def _():
            dg_ref[...] = jnp.zeros_like(dg_ref)

        dg_ref[...] += dg

    dhi, dg = pl.pallas_call(
        input_body, name=name + "_x", grid=(t // tm,),
        in_specs=[_row_spec(tm, d), _row_spec(tm, d), _acc_spec((1, d)), _row_spec(tm, f), _row_spec(tm, f),
                  _VMEM, _VMEM],
        out_specs=[_row_spec(tm, d), _acc_spec((1, d))],
        out_shape=[jax.ShapeDtypeStruct((t, d), F32), jax.ShapeDtypeStruct((1, d), F32)],
        compiler_params=_params(("arbitrary",)),
    )(dh, h, g, da, db, wg_t, wu_t)
    return dhi, da, db, s, dhb, dg


DW_TILE = 256


def tn_matmul(x, y, name, dep=None):
    xs = list(x) if isinstance(x, (list, tuple)) else [x]
    t = xs[0].shape[0]
    n = y.shape[1]
    bm = DW_TILE
    tiles = [a.shape[1] // bm for a in xs]
    offs = [sum(tiles[:k]) for k in range(len(xs))]
    deps = [] if dep is None else [dep]

    def body(*refs):
        y_ref, o_ref = refs[len(xs)], refs[-1]
        i = pl.program_id(0)
        for k in range(len(xs)):
            @pl.when((i >= offs[k]) & (i < offs[k] + tiles[k]))
            def _(k=k):
                o_ref[...] = _tn(refs[k][...], y_ref[...]).astype(BF16)

    def x_spec(k):
        return pl.BlockSpec((t, bm), lambda i: (0, jnp.clip(i - offs[k], 0, tiles[k] - 1)))

    return pl.pallas_call(
        body, name=name, grid=(sum(tiles),),
        in_specs=[x_spec(k) for k in range(len(xs))] + [_VMEM] + [_ANY] * len(deps),
        out_specs=pl.BlockSpec((bm, n), lambda i: (i, 0)),
        out_shape=jax.ShapeDtypeStruct((sum(tiles) * bm, n), BF16),
        compiler_params=_params(("arbitrary",)),
    )(*xs, y, *deps)


def head_fwd_bwd(h, g, tgt):
    t, d = h.shape
    tm = TOKEN_TILE
    per_tile = tm // BLK
    last = tgt.shape[0] // BLK - 1

    def body(h_ref, g_ref, *rest):
        t_refs, (loss_ref, dh_ref, dg_ref) = rest[:per_tile], rest[per_tile:]
        i = pl.program_id(0)
        xh, r = _rms_stats(h_ref[...])
        gv = g_ref[...]
        target = jnp.concatenate([ref[...] for ref in t_refs], axis=0)
        row = i * tm + lax.broadcasted_iota(jnp.int32, (tm, 1), 0)
        e = jnp.where(row >= BLK, xh * gv - target, 0.0)
        dx, dg = _rms_bwd(e * (1.0 / d), xh, r, gv)
        dh_ref[...] = dx

        @pl.when(i == 0)
        def _():
            dg_ref[...] = jnp.zeros_like(dg_ref)
            loss_ref[...] = jnp.zeros_like(loss_ref)

        dg_ref[...] += dg
        loss_ref[...] += jnp.sum(e * e) * (0.5 / d)

    def target_spec(k):
        return pl.BlockSpec((BLK, d), lambda i: (jnp.clip(i * per_tile - 1 + k, 0, last), 0))

    return pl.pallas_call(
        body, name="head", grid=(t // tm,),
        in_specs=[_row_spec(tm, d), _acc_spec((1, d))] + [target_spec(k) for k in range(per_tile)],
        out_specs=[_acc_spec((1, 128)), _row_spec(tm, d), _acc_spec((1, d))],
        out_shape=[jax.ShapeDtypeStruct((1, 128), F32), jax.ShapeDtypeStruct((t, d), F32),
                   jax.ShapeDtypeStruct((1, d), F32)],
        compiler_params=_params(("arbitrary",)),
    )(h, g, *[tgt] * per_tile)


def rope_tables(t):
    pos = jnp.arange(t, dtype=F32) - PAD_FRONT
    inv_freq = ROPE_THETA ** (-jnp.arange(0, ROT_DIM, 2, dtype=F32) / ROT_DIM)
    ang = pos[:, None] * inv_freq[None, :]
    cos, sin = jnp.cos(ang), jnp.sin(ang)
    ones = jnp.ones((t, HEAD_DIM - ROT_DIM), F32)
    cos_h = jnp.concatenate([cos, cos, ones], axis=1)
    sin_h = jnp.concatenate([-sin, sin, 0.0 * ones], axis=1)
    return jnp.concatenate([cos_h, cos_h], axis=1), jnp.concatenate([sin_h, sin_h], axis=1)


def _swap_halves(x):
    n = x.shape[1]
    lane = lax.broadcasted_iota(jnp.int32, x.shape, 1)
    return jnp.where(lane % HEAD_DIM < ROT_DIM // 2, pltpu.roll(x, n - ROT_DIM // 2, 1), pltpu.roll(x, ROT_DIM // 2, 1))


def _rope(x, cos_t, sin_t, sign):
    return x * cos_t + sign * (_swap_halves(x) * sin_t)


def win_fwd(h, g, win_t, cos_t, sin_t, name):
    t, d = h.shape
    tm = TOKEN_TILE

    def body(h_ref, g_ref, w_ref, c_ref, s_ref, n_ref, qkv_ref, u_ref, gates_ref):
        xh, _ = _rms_stats(h_ref[...])
        n = (xh * g_ref[...]).astype(BF16)
        n_ref[...] = n
        z = _nt(n, w_ref[...])
        c, s = c_ref[...], s_ref[...]
        for j in range((ATTN_WIDTH + KV_WIDTH) // 128):
            qkv_ref[:, j * 128:(j + 1) * 128] = _rope(z[:, j * 128:(j + 1) * 128], c, s, 1.0).astype(BF16)
        qkv_ref[:, ATTN_WIDTH + KV_WIDTH:QKV_WIDTH] = z[:, ATTN_WIDTH + KV_WIDTH:QKV_WIDTH].astype(BF16)
        for j in range(N_CHUNK):
            u_ref[j] = z[:, QKV_WIDTH + j * U_CHUNK:QKV_WIDTH + (j + 1) * U_CHUNK]
        gates_ref[...] = z[:, QKV_WIDTH + SSM_WIDTH:].astype(BF16)

    return pl.pallas_call(
        body, name=name, grid=(t // tm,),
        in_specs=[_row_spec(tm, d), _acc_spec((1, d)), _VMEM, _row_spec(tm, 128), _row_spec(tm, 128)],
        out_specs=[_row_spec(tm, d), _row_spec(tm, QKV_WIDTH), _chunked_spec(tm, lambda i: i), _row_spec(tm, 2 * d)],
        out_shape=[jax.ShapeDtypeStruct((t, d), BF16), jax.ShapeDtypeStruct((t, QKV_WIDTH), BF16),
                   jax.ShapeDtypeStruct((N_CHUNK, t, U_CHUNK), F32), jax.ShapeDtypeStruct((t, 2 * d), BF16)],
        compiler_params=_params(("arbitrary",)),
    )(h, g, win_t, cos_t, sin_t)


def win_bwd(dh, h, g, dqkv, du, dgates, win_t, dep, name):
    t, d = h.shape
    tm = TOKEN_TILE

    def body(dh_ref, h_ref, g_ref, dqkv_ref, du_ref, dgt_ref, w_ref, dep_ref, dhi_ref, dg_ref):
        dn = (_nn(dqkv_ref[...], w_ref[0:QKV_WIDTH, :])
              + _nn(du_ref[...], w_ref[QKV_WIDTH:QKV_WIDTH + SSM_WIDTH, :])
              + _nn(dgt_ref[...], w_ref[QKV_WIDTH + SSM_WIDTH:, :]))
        xh, r = _rms_stats(h_ref[...])
        dx, dg = _rms_bwd(dn, xh, r, g_ref[...])
        dhi_ref[...] = dh_ref[...] + dx

        @pl.when(pl.program_id(0) == 0)
        def _():
            dg_ref[...] = jnp.zeros_like(dg_ref)

        dg_ref[...] += dg

    return pl.pallas_call(
        body, name=name, grid=(t // tm,),
        in_specs=[_row_spec(tm, d), _row_spec(tm, d), _acc_spec((1, d)), _row_spec(tm, QKV_WIDTH),
                  _row_spec(tm, SSM_WIDTH), _row_spec(tm, 2 * d), _VMEM, _ANY],
        out_specs=[_row_spec(tm, d), _acc_spec((1, d))],
        out_shape=[jax.ShapeDtypeStruct((t, d), F32), jax.ShapeDtypeStruct((1, d), F32)],
        compiler_params=_params(("arbitrary",)),
    )(dh, h, g, dqkv, du, dgates, win_t, dep)


def _attn_mask(blk):
    q_pos = blk * BLK + lax.broadcasted_iota(jnp.int32, (BLK, 3 * BLK), 0) - PAD_FRONT
    col = lax.broadcasted_iota(jnp.int32, (BLK, 3 * BLK), 1)
    part = col // BLK
    k_pos = jnp.where(part == 0, col, (blk + part - 2) * BLK + (col - part * BLK)) - PAD_FRONT
    dist = q_pos - k_pos
    meta_ok = (part == 0) & (k_pos >= 0) & (dist >= 0)
    band_ok = (part > 0) & (k_pos >= N_META) & (dist >= 0) & (dist < WINDOW)
    return meta_ok | band_ok


def _head_halves(x128, kv):
    x = x128.astype(F32)
    lane = lax.broadcasted_iota(jnp.int32, x.shape, 1)
    swapped = pltpu.roll(x, HEAD_DIM, 1)
    lo, hi = (x, swapped) if kv == 0 else (swapped, x)
    return jnp.where(lane < HEAD_DIM, lo, 0.0).astype(BF16), jnp.where(lane >= HEAD_DIM, hi, 0.0).astype(BF16)


def _gather_keys(meta_ref, prev_ref, cur_ref, lo):
    return jnp.concatenate([meta_ref[:, lo:lo + 128], prev_ref[:, lo:lo + 128], cur_ref[:, lo:lo + 128]], axis=0)


def _pair_lanes(kv):
    return slice(2 * kv * 128, (2 * kv + 1) * 128), slice((2 * kv + 1) * 128, (2 * kv + 2) * 128)


def _stacked_sinks(sink_ref, head):
    row = lax.broadcasted_iota(jnp.int32, (2 * BLK, 1), 0)
    return jnp.where(row < BLK, sink_ref[0, head], sink_ref[0, head + 2])


def _softmax_with_sink(s, mask, sink):
    s = jnp.where(mask, s * (HEAD_DIM ** -0.5), NEG_INF)
    m = jnp.maximum(jnp.max(s, axis=-1, keepdims=True), sink)
    p = jnp.exp(s - m)
    p_sink = jnp.exp(sink - m)
    inv = 1.0 / (jnp.sum(p, axis=-1, keepdims=True) + p_sink)
    return p * inv, p_sink * inv


def attn_fwd(qkv, sinks, name):
    t = qkv.shape[0]
    nb = t // BLK

    def body(sink_ref, meta_ref, prev_ref, cur_ref, o_ref):
        blk = pl.program_id(0)
        mask = _attn_mask(blk)
        mask2 = jnp.concatenate([mask, mask], axis=0)
        k128 = _gather_keys(meta_ref, prev_ref, cur_ref, ATTN_WIDTH)
        v128 = _gather_keys(meta_ref, prev_ref, cur_ref, ATTN_WIDTH + KV_WIDTH)
        for kv in range(2):
            k_lo, k_hi = _head_halves(k128, kv)
            v_lo, v_hi = _head_halves(v128, kv)
            lanes0, lanes1 = _pair_lanes(kv)
            q2 = jnp.concatenate([cur_ref[:, lanes0], cur_ref[:, lanes1]], axis=0)
            p_a, _ = _softmax_with_sink(_nt(q2, k_lo), mask2, _stacked_sinks(sink_ref, 4 * kv))
            p_b, _ = _softmax_with_sink(_nt(q2, k_hi), mask2, _stacked_sinks(sink_ref, 4 * kv + 1))
            o2 = (_nn(p_a.astype(BF16), v_lo) + _nn(p_b.astype(BF16), v_hi)).astype(BF16)
            o_ref[:, lanes0] = o2[0:BLK]
            o_ref[:, lanes1] = o2[BLK:2 * BLK]

    blk_spec = lambda f: pl.BlockSpec((BLK, QKV_WIDTH), f)
    return pl.pallas_call(
        body, name=name, grid=(nb,),
        in_specs=[_SMEM, blk_spec(lambda i: (0, 0)), blk_spec(lambda i: (jnp.maximum(i - 1, 0), 0)),
                  blk_spec(lambda i: (i, 0))],
        out_specs=_row_spec(BLK, ATTN_WIDTH),
        out_shape=jax.ShapeDtypeStruct((t, ATTN_WIDTH), BF16),
        compiler_params=_params(("arbitrary",)),
    )(sinks, qkv, qkv, qkv)


def attn_bwd(qkv, do, sinks, cos_t, sin_t, name):
    t = qkv.shape[0]
    nb = t // BLK

    def body(sink_ref, meta_ref, prev_ref, cur_ref, do_ref, c_ref, s_ref, dqkv_ref, dsink_ref, carry_ref, macc_ref):
        step = pl.program_id(0)
        blk = nb - 1 - step

        @pl.when(step == 0)
        def _():
            dsink_ref[...] = jnp.zeros_like(dsink_ref)
            carry_ref[...] = jnp.zeros_like(carry_ref)
            macc_ref[...] = jnp.zeros_like(macc_ref)

        mask = _attn_mask(blk)
        mask2 = jnp.concatenate([mask, mask], axis=0)
        lane = lax.broadcasted_iota(jnp.int32, (3 * BLK, 128), 1)
        k128 = _gather_keys(meta_ref, prev_ref, cur_ref, ATTN_WIDTH)
        v128 = _gather_keys(meta_ref, prev_ref, cur_ref, ATTN_WIDTH + KV_WIDTH)
        cos_b, sin_b = c_ref[...], s_ref[...]
        dk_heads, dv_heads = [], []
        for kv in range(2):
            k_lo, k_hi = _head_halves(k128, kv)
            v_lo, v_hi = _head_halves(v128, kv)
            lanes0, lanes1 = _pair_lanes(kv)
            q2 = jnp.concatenate([cur_ref[:, lanes0], cur_ref[:, lanes1]], axis=0)
            do2 = jnp.concatenate([do_ref[:, lanes0], do_ref[:, lanes1]], axis=0)
            ds_half, p_half = [], []
            for half, (k_h, v_h) in enumerate(((k_lo, v_lo), (k_hi, v_hi))):
                head = 4 * kv + half
                p, p_sink = _softmax_with_sink(_nt(q2, k_h), mask2, _stacked_sinks(sink_ref, head))
                dp = _nt(do2, v_h)
                dsum = jnp.sum(p * dp, axis=-1, keepdims=True)
                ds_half.append((p * (dp - dsum) * (HEAD_DIM ** -0.5)).astype(BF16))
                p_half.append(p.astype(BF16))
                dsink = p_sink * dsum
                for part, h in ((0, head), (1, head + 2)):
                    total = -jnp.sum(dsink[part * BLK:(part + 1) * BLK], axis=0, keepdims=True)
                    dsink_ref[h:h + 1, :] += jnp.broadcast_to(total, (1, 128))
            dq2 = _nn(ds_half[0], k_lo) + _nn(ds_half[1], k_hi)
            dqkv_ref[:, lanes0] = _rope(dq2[0:BLK], cos_b, sin_b, -1.0).astype(BF16)
            dqkv_ref[:, lanes1] = _rope(dq2[BLK:2 * BLK], cos_b, sin_b, -1.0).astype(BF16)
            dk_acc = jnp.where(lane < HEAD_DIM, _tn(ds_half[0], q2), _tn(ds_half[1], q2))
            dv_acc = jnp.where(lane < HEAD_DIM, _tn(p_half[0], do2), _tn(p_half[1], do2))
            dk_heads.append(dk_acc + pltpu.roll(dk_acc, HEAD_DIM, 1))
            dv_heads.append(dv_acc + pltpu.roll(dv_acc, HEAD_DIM, 1))
        dkv = jnp.concatenate([jnp.where(lane < HEAD_DIM, dk_heads[0], dk_heads[1]),
                               jnp.where(lane < HEAD_DIM, dv_heads[0], dv_heads[1])], axis=1)
        macc_ref[...] += dkv[0:BLK]
        is_last = (blk == 0).astype(F32)
        mine = dkv[2 * BLK:3 * BLK] + carry_ref[...] + is_last * macc_ref[...]
        carry_ref[...] = dkv[BLK:2 * BLK]
        dqkv_ref[:, ATTN_WIDTH:ATTN_WIDTH + KV_WIDTH] = _rope(mine[:, 0:128], cos_b, sin_b, -1.0).astype(BF16)
        dqkv_ref[:, ATTN_WIDTH + KV_WIDTH:QKV_WIDTH] = mine[:, 128:256].astype(BF16)

    rev = lambda i: nb - 1 - i
    blk_spec = lambda f: pl.BlockSpec((BLK, QKV_WIDTH), f)
    return pl.pallas_call(
        body, name=name, grid=(nb,),
        in_specs=[_SMEM, blk_spec(lambda i: (0, 0)), blk_spec(lambda i: (jnp.maximum(rev(i) - 1, 0), 0)),
                  blk_spec(lambda i: (rev(i), 0)), pl.BlockSpec((BLK, ATTN_WIDTH), lambda i: (rev(i), 0)),
                  pl.BlockSpec((BLK, 128), lambda i: (rev(i), 0)), pl.BlockSpec((BLK, 128), lambda i: (rev(i), 0))],
        out_specs=[pl.BlockSpec((BLK, QKV_WIDTH), lambda i: (rev(i), 0)), _acc_spec((N_Q_HEADS, 128))],
        out_shape=[jax.ShapeDtypeStruct((t, QKV_WIDTH), BF16), jax.ShapeDtypeStruct((N_Q_HEADS, 128), F32)],
        scratch_shapes=[pltpu.VMEM((BLK, 256), F32), pltpu.VMEM((BLK, 256), F32)],
        compiler_params=_params(("arbitrary",)),
    )(sinks, qkv, qkv, qkv, do, cos_t, sin_t)


def _cmul(ar, ai, br, bi):
    return ar * br - ai * bi, ar * bi + ai * br


def ssm_prep(a_re, a_im, log_dt, b_re_t, b_im_t, name):
    def body(ar_ref, ai_ref, ldt_ref, br_ref, bi_ref, lr_ref, li_ref, bbr_ref, bbi_ref):
        ar, ai = ar_ref[...], ai_ref[...]
        dt = jnp.exp(ldt_ref[...])
        mag = jnp.exp(ar * dt)
        lr = mag * jnp.cos(ai * dt)
        li = mag * jnp.sin(ai * dt)
        den = ar * ar + ai * ai
        nr = lr - 1.0
        cr = ((nr * ar + li * ai) / den)[:, None, :]
        ci = ((li * ar - nr * ai) / den)[:, None, :]
        br, bi = br_ref[...], bi_ref[...]
        lr_ref[...] = lr
        li_ref[...] = li
        bbr_ref[...] = cr * br - ci * bi
        bbi_ref[...] = cr * bi + ci * br

    gp = jax.ShapeDtypeStruct(a_re.shape, F32)
    gcp = jax.ShapeDtypeStruct(b_re_t.shape, F32)
    return pl.pallas_call(body, name=name, out_shape=[gp, gp, gcp, gcp],
                          in_specs=[_VMEM] * 5, out_specs=[_VMEM] * 4)(a_re, a_im, log_dt, b_re_t, b_im_t)


def ssm_prep_bwd(a_re, a_im, log_dt, b_re_t, b_im_t, dl_re, dl_im, dbb_re, dbb_im, name):
    def body(ar_ref, ai_ref, ldt_ref, br_ref, bi_ref, dlr_ref, dli_ref, dbbr_ref, dbbi_ref,
             dar_ref, dai_ref, dldt_ref, dbr_ref, dbi_ref):
        ar, ai = ar_ref[...], ai_ref[...]
        dt = jnp.exp(ldt_ref[...])
        mag = jnp.exp(ar * dt)
        lr = mag * jnp.cos(ai * dt)
        li = mag * jnp.sin(ai * dt)
        den = ar * ar + ai * ai
        nr = lr - 1.0
        cr = (nr * ar + li * ai) / den
        ci = (li * ar - nr * ai) / den
        br, bi = br_ref[...], bi_ref[...]
        dbbr, dbbi = dbbr_ref[...], dbbi_ref[...]
        dbr_ref[...] = cr[:, None, :] * dbbr + ci[:, None, :] * dbbi
        dbi_ref[...] = cr[:, None, :] * dbbi - ci[:, None, :] * dbbr
        dcr = jnp.sum(br * dbbr + bi * dbbi, axis=1)
        dci = jnp.sum(br * dbbi - bi * dbbr, axis=1)
        d_num_r = dcr / den
        d_num_i = dci / den
        d_den = -(dcr * cr + dci * ci) / den
        d_lr = dlr_ref[...] + d_num_r * ar - d_num_i * ai
        d_li = dli_ref[...] + d_num_r * ai + d_num_i * ar
        d_ar = d_num_r * nr + d_num_i * li + d_den * 2.0 * ar
        d_ai = d_num_r * li - d_num_i * nr + d_den * 2.0 * ai
        d_mag = (d_lr * lr + d_li * li) / mag
        d_theta = d_li * lr - d_lr * li
        d_ardt = d_mag * mag
        dar_ref[...] = d_ar + d_ardt * dt
        dai_ref[...] = d_ai + d_theta * dt
        d_dt = jnp.sum(d_ardt * ar + d_theta * ai, axis=1, keepdims=True)
        dldt_ref[...] = d_dt * dt

    gp = jax.ShapeDtypeStruct(a_re.shape, F32)
    gcp = jax.ShapeDtypeStruct(b_re_t.shape, F32)
    return pl.pallas_call(body, name=name, out_shape=[gp, gp, jax.ShapeDtypeStruct(log_dt.shape, F32), gcp, gcp],
                          in_specs=[_VMEM] * 9, out_specs=[_VMEM] * 5,
                          )(a_re, a_im, log_dt, b_re_t, b_im_t, dl_re, dl_im, dbb_re, dbb_im)


N_CHUNK = 4
U_CHUNK = SSM_WIDTH // N_CHUNK
H_CHUNK = STATE_WIDTH // N_CHUNK
SUB = 8


def _block_diag_b(bb):
    x = bb.reshape(N_CHUNK, 8, SSM_GROUP, 1, SSM_STATE)
    same = (jnp.arange(8)[:, None] == jnp.arange(8)[None, :])[None, :, None, :, None]
    return jnp.where(same, x, 0.0).reshape(N_CHUNK, U_CHUNK, H_CHUNK)


def _block_diag_c(c):
    x = jnp.swapaxes(c.reshape(N_CHUNK, 8, SSM_GROUP, SSM_STATE), 2, 3)[:, :, :, None, :]
    same = (jnp.arange(8)[:, None] == jnp.arange(8)[None, :])[None, :, None, :, None]
    return jnp.where(same, x, 0.0).reshape(N_CHUNK, H_CHUNK, U_CHUNK)


def _diag_of_b(m):
    x = m.reshape(N_CHUNK, 8, SSM_GROUP, 8, SSM_STATE)
    return jnp.stack([x[:, g, :, g, :] for g in range(8)], axis=1).reshape(SSM_GROUPS, SSM_GROUP, SSM_STATE)


def _diag_of_c(m):
    x = m.reshape(N_CHUNK, 8, SSM_STATE, 8, SSM_GROUP)
    d = jnp.stack([x[:, g, :, g, :] for g in range(8)], axis=1)
    return jnp.swapaxes(d, 2, 3).reshape(SSM_GROUPS, SSM_GROUP, SSM_STATE)


def _lambda_tables(lr, li, reverse):
    p1 = (lr, li)
    p2 = _cmul(*p1, *p1)
    p4 = _cmul(*p2, *p2)
    rows = [p1]
    for _ in range(SUB - 1):
        rows.append(_cmul(*rows[-1], *p1))
    if reverse:
        rows = rows[::-1]
    return p1, p2, p4, (jnp.concatenate([r[0] for r in rows], axis=0), jnp.concatenate([r[1] for r in rows], axis=0))


def _scan8(xr, xi, pows, table, cr, ci, reverse):
    row = lax.broadcasted_iota(jnp.int32, xr.shape, 0)
    for d, (pr, pi) in zip((1, 2, 4), pows):
        if reverse:
            sr, si = pltpu.roll(xr, SUB - d, 0), pltpu.roll(xi, SUB - d, 0)
            keep = row < SUB - d
        else:
            sr, si = pltpu.roll(xr, d, 0), pltpu.roll(xi, d, 0)
            keep = row >= d
        sr = jnp.where(keep, sr, 0.0)
        si = jnp.where(keep, si, 0.0)
        xr, xi = xr + pr * sr - pi * si, xi + pr * si + pi * sr
    tr, ti = table
    return xr + tr * cr - ti * ci, xi + tr * ci + ti * cr


def _gelu_and_grad(y):
    k0 = math.sqrt(2.0 / math.pi)
    inner = k0 * (y + 0.044715 * y * y * y)
    th = jnp.tanh(inner)
    g = 0.5 * y * (1.0 + th)
    dg = 0.5 * (1.0 + th) + 0.5 * y * (1.0 - th * th) * k0 * (1.0 + 3.0 * 0.044715 * y * y)
    return g, dg


SCAN_TILE = TOKEN_TILE
SEG = SCAN_TILE // SUB
SCAN_LANES = 512


def _perm_matrix(to_segments):
    a = lax.broadcasted_iota(jnp.int32, (SCAN_TILE, SCAN_TILE), 0)
    b = lax.broadcasted_iota(jnp.int32, (SCAN_TILE, SCAN_TILE), 1)
    rho, time = (a, b) if to_segments else (b, a)
    return (time == (rho % SUB) * SEG + rho // SUB).astype(BF16)


def _chunked_spec(rows, block_of):
    return pl.BlockSpec((N_CHUNK, rows, U_CHUNK), lambda i: (0, block_of(i), 0))


def _load_segments(src_ref, dst_ref):
    for j in range(N_CHUNK):
        for r in range(SEG):
            dst_ref[r * SUB:(r + 1) * SUB, j * U_CHUNK:(j + 1) * U_CHUNK] = src_ref.at[j][pl.ds(r, SUB, stride=SEG), :]


def _power_table(lr, li, pr_ref, pi_ref):
    cur = (lr, li)
    for r in range(SEG):
        pr_ref[r * SUB:(r + 1) * SUB, :] = jnp.broadcast_to(cur[0], (SUB, STATE_WIDTH))
        pi_ref[r * SUB:(r + 1) * SUB, :] = jnp.broadcast_to(cur[1], (SUB, STATE_WIDTH))
        cur = _cmul(*cur, lr, li)


def _table_rows(ref, k, lanes):
    return ref[pl.ds(pl.multiple_of(k * SUB, SUB), SUB), lanes]


def _segment_scan(xr_ref, xi_ref, lanes, lam, table_row, cr_ref, ci_ref, reverse, extra=None):
    lr = jnp.broadcast_to(lam[0], (SUB, SCAN_LANES))
    li = jnp.broadcast_to(lam[1], (SUB, SCAN_LANES))
    row = lax.broadcasted_iota(jnp.int32, (SUB, SCAN_LANES), 0)

    def rows_of(k):
        r = SEG - 1 - k if reverse else k
        return pl.ds(pl.multiple_of(r * SUB, SUB), SUB)

    def first(k, st):
        sr, si = st
        rows = rows_of(k)
        nr = lr * sr - li * si + xr_ref[rows, lanes]
        ni = lr * si + li * sr + xi_ref[rows, lanes]
        xr_ref[rows, lanes] = nr
        xi_ref[rows, lanes] = ni
        return nr, ni

    zero = jnp.zeros((SUB, SCAN_LANES), F32)
    er, ei = lax.fori_loop(0, SEG, first, (zero, zero))
    l16 = table_row(SEG - 1)
    q1, q2, q4, tab = _lambda_tables(l16[0][0:1], l16[1][0:1], reverse)
    c_r, c_i = cr_ref[:, lanes], ci_ref[:, lanes]
    gr, gi = _scan8(er, ei, (q1, q2, q4), tab, c_r, c_i, reverse)
    if reverse:
        cin_r = jnp.where(row == SUB - 1, c_r, pltpu.roll(gr, SUB - 1, 0))
        cin_i = jnp.where(row == SUB - 1, c_i, pltpu.roll(gi, SUB - 1, 0))
        cr_ref[:, lanes] = gr[0:1]
        ci_ref[:, lanes] = gi[0:1]
    else:
        cin_r = jnp.where(row == 0, c_r, pltpu.roll(gr, 1, 0))
        cin_i = jnp.where(row == 0, c_i, pltpu.roll(gi, 1, 0))
        cr_ref[:, lanes] = gr[SUB - 1:SUB]
        ci_ref[:, lanes] = gi[SUB - 1:SUB]

    def second(k, carry):
        rows = rows_of(k)
        tr, ti = table_row(k)
        ar = xr_ref[rows, lanes] + tr * cin_r - ti * cin_i
        ai = xi_ref[rows, lanes] + tr * cin_i + ti * cin_r
        xr_ref[rows, lanes] = ar
        xi_ref[rows, lanes] = ai
        if extra is None:
            return carry
        return extra(rows, carry, ar, ai)

    init = 0 if extra is None else (cin_r, cin_i, zero, zero)
    return lax.fori_loop(0, SEG, second, init)


def ssm_fwd(u, lam_re, lam_im, bb_re, bb_im, cc_re, cc_im, d_skip, name):
    t = u.shape[1]
    tt = SCAN_TILE

    def body(u_ref, lr_ref, li_ref, bbr_ref, bbi_ref, ccr_ref, cci_ref, d_ref, yg_ref, hr_ref, hi_ref,
             cr_ref, ci_ref, pr_ref, pi_ref, up_ref, y_ref):
        @pl.when(pl.program_id(0) == 0)
        def _():
            cr_ref[...] = jnp.zeros_like(cr_ref)
            ci_ref[...] = jnp.zeros_like(ci_ref)
            _power_table(lr_ref[...], li_ref[...], pr_ref, pi_ref)

        _load_segments(u_ref, up_ref)
        ub = up_ref[...].astype(BF16)
        for j in range(N_CHUNK):
            hs = slice(j * H_CHUNK, (j + 1) * H_CHUNK)
            us = slice(j * U_CHUNK, (j + 1) * U_CHUNK)
            hr_ref[:, hs] = _nn(ub[:, us], bbr_ref[j])
            hi_ref[:, hs] = _nn(ub[:, us], bbi_ref[j])
        for c in range(STATE_WIDTH // SCAN_LANES):
            lanes = slice(c * SCAN_LANES, (c + 1) * SCAN_LANES)
            _segment_scan(hr_ref, hi_ref, lanes, (lr_ref[:, lanes], li_ref[:, lanes]),
                          lambda k, lanes=lanes: (_table_rows(pr_ref, k, lanes), _table_rows(pi_ref, k, lanes)),
                          cr_ref, ci_ref, False)
        for j in range(N_CHUNK):
            hs = slice(j * H_CHUNK, (j + 1) * H_CHUNK)
            us = slice(j * U_CHUNK, (j + 1) * U_CHUNK)
            y = (_nn(hr_ref[:, hs].astype(BF16), ccr_ref[j]) - _nn(hi_ref[:, hs].astype(BF16), cci_ref[j])
                 + d_ref[:, us] * up_ref[:, us])
            y_ref[:, us] = _gelu_and_grad(y)[0]
        yg_ref[...] = _nn(_perm_matrix(False), y_ref[...].astype(BF16)).astype(BF16)

    return pl.pallas_call(
        body, name=name, grid=(t // tt,),
        in_specs=[_chunked_spec(tt, lambda i: i), _VMEM, _VMEM, _VMEM, _VMEM, _VMEM, _VMEM, _VMEM],
        out_specs=[_row_spec(tt, SSM_WIDTH), _row_spec(tt, STATE_WIDTH), _row_spec(tt, STATE_WIDTH)],
        out_shape=[jax.ShapeDtypeStruct((t, SSM_WIDTH), BF16), jax.ShapeDtypeStruct((t, STATE_WIDTH), F32),
                   jax.ShapeDtypeStruct((t, STATE_WIDTH), F32)],
        scratch_shapes=[pltpu.VMEM((1, STATE_WIDTH), F32), pltpu.VMEM((1, STATE_WIDTH), F32),
                        pltpu.VMEM((SCAN_TILE, STATE_WIDTH), F32), pltpu.VMEM((SCAN_TILE, STATE_WIDTH), F32),
                        pltpu.VMEM((tt, SSM_WIDTH), F32), pltpu.VMEM((tt, SSM_WIDTH), F32)],
        compiler_params=_params(("arbitrary",)),
    )(u, lam_re, lam_im, bb_re, bb_im, cc_re, cc_im, d_skip)


def ssm_bwd(dyg, u, h_re, h_im, lam_re, lam_im, bb_re, bb_im, cc_re, cc_im, d_skip, name):
    t = u.shape[1]
    tt = SCAN_TILE
    nt = t // tt

    def body(dyg_ref, u_ref, hr_ref, hi_ref, lr_ref, li_ref, bbr_ref, bbi_ref, ccr_ref, cci_ref, d_ref,
             du_ref, dlr_ref, dli_ref, dbbr_ref, dbbi_ref, dccr_ref, dcci_ref, dd_ref,
             ar_ref, ai_ref, cr_ref, ci_ref, pr_ref, pi_ref, up_ref, dy_ref, dup_ref):
        step = pl.program_id(0)
        tile = nt - 1 - step

        @pl.when(step == 0)
        def _():
            for ref in (cr_ref, ci_ref, dlr_ref, dli_ref, dbbr_ref, dbbi_ref, dccr_ref, dcci_ref, dd_ref):
                ref[...] = jnp.zeros_like(ref)
            _power_table(lr_ref[...], li_ref[...], pr_ref, pi_ref)

        _load_segments(u_ref, up_ref)
        _load_segments(dyg_ref, dy_ref)
        uv = up_ref[...]
        ub = uv.astype(BF16)
        dskip = d_ref[...]
        for j in range(N_CHUNK):
            hs = slice(j * H_CHUNK, (j + 1) * H_CHUNK)
            us = slice(j * U_CHUNK, (j + 1) * U_CHUNK)
            hrb = hr_ref[:, hs].astype(BF16)
            hib = hi_ref[:, hs].astype(BF16)
            y = _nn(hrb, ccr_ref[j]) - _nn(hib, cci_ref[j]) + dskip[:, us] * uv[:, us]
            dy = dy_ref[:, us] * _gelu_and_grad(y)[1]
            dy_ref[:, us] = dy
            dyb = dy.astype(BF16)
            dccr_ref[j] += _tn(hrb, dyb)
            dcci_ref[j] -= _tn(hib, dyb)
            ar_ref[:, hs] = _nt(dyb, ccr_ref[j])
            ai_ref[:, hs] = -_nt(dyb, cci_ref[j])
        dd_ref[...] += jnp.sum(dy_ref[...] * uv, axis=0, keepdims=True)

        for c in range(STATE_WIDTH // SCAN_LANES):
            lanes = slice(c * SCAN_LANES, (c + 1) * SCAN_LANES)

            def dlambda(rows, carry, ar, ai, lanes=lanes):
                nr, ni, accr, acci = carry
                hr, hi = hr_ref[rows, lanes], hi_ref[rows, lanes]
                return ar, ai, accr + nr * hr + ni * hi, acci + ni * hr - nr * hi

            _, _, accr, acci = _segment_scan(
                ar_ref, ai_ref, lanes, (lr_ref[:, lanes], -li_ref[:, lanes]),
                lambda k, lanes=lanes: (_table_rows(pr_ref, k, lanes), -_table_rows(pi_ref, k, lanes)),
                cr_ref, ci_ref, True, dlambda)
            dlr_ref[:, lanes] += accr
            dli_ref[:, lanes] += acci

        rho = lax.broadcasted_iota(jnp.int32, (tt, U_CHUNK), 0)
        time = tile * tt + (rho % SUB) * SEG + rho // SUB
        for j in range(N_CHUNK):
            hs = slice(j * H_CHUNK, (j + 1) * H_CHUNK)
            us = slice(j * U_CHUNK, (j + 1) * U_CHUNK)
            arb = ar_ref[:, hs].astype(BF16)
            aib = ai_ref[:, hs].astype(BF16)
            dbbr_ref[j] += _tn(ub[:, us], arb)
            dbbi_ref[j] += _tn(ub[:, us], aib)
            du = _nt(arb, bbr_ref[j]) + _nt(aib, bbi_ref[j]) + dy_ref[:, us] * dskip[:, us]
            dup_ref[:, us] = jnp.where(time >= PAD_FRONT, du, 0.0)
        du_ref[...] = _nn(_perm_matrix(False), dup_ref[...].astype(BF16)).astype(BF16)

    rev = lambda i: (nt - 1 - i, 0)
    full = lambda shape: pl.BlockSpec(shape, lambda i: (0,) * len(shape))
    return pl.pallas_call(
        body, name=name, grid=(nt,),
        in_specs=[_chunked_spec(tt, lambda i: nt - 1 - i), _chunked_spec(tt, lambda i: nt - 1 - i),
                  pl.BlockSpec((tt, STATE_WIDTH), rev), pl.BlockSpec((tt, STATE_WIDTH), rev),
                  _VMEM, _VMEM, _VMEM, _VMEM, _VMEM, _VMEM, _VMEM],
        out_specs=[pl.BlockSpec((tt, SSM_WIDTH), rev), full((SUB, STATE_WIDTH)), full((SUB, STATE_WIDTH)),
                   full((N_CHUNK, U_CHUNK, H_CHUNK)), full((N_CHUNK, U_CHUNK, H_CHUNK)),
                   full((N_CHUNK, H_CHUNK, U_CHUNK)), full((N_CHUNK, H_CHUNK, U_CHUNK)), full((1, SSM_WIDTH))],
        out_shape=[jax.ShapeDtypeStruct((t, SSM_WIDTH), BF16),
                   jax.ShapeDtypeStruct((SUB, STATE_WIDTH), F32), jax.ShapeDtypeStruct((SUB, STATE_WIDTH), F32),
                   jax.ShapeDtypeStruct((N_CHUNK, U_CHUNK, H_CHUNK), F32),
                   jax.ShapeDtypeStruct((N_CHUNK, U_CHUNK, H_CHUNK), F32),
                   jax.ShapeDtypeStruct((N_CHUNK, H_CHUNK, U_CHUNK), F32),
                   jax.ShapeDtypeStruct((N_CHUNK, H_CHUNK, U_CHUNK), F32),
                   jax.ShapeDtypeStruct((1, SSM_WIDTH), F32)],
        scratch_shapes=[pltpu.VMEM((tt, STATE_WIDTH), F32), pltpu.VMEM((tt, STATE_WIDTH), F32),
                        pltpu.VMEM((1, STATE_WIDTH), F32), pltpu.VMEM((1, STATE_WIDTH), F32),
                        pltpu.VMEM((SCAN_TILE, STATE_WIDTH), F32), pltpu.VMEM((SCAN_TILE, STATE_WIDTH), F32),
                        pltpu.VMEM((tt, SSM_WIDTH), F32), pltpu.VMEM((tt, SSM_WIDTH), F32),
                        pltpu.VMEM((tt, SSM_WIDTH), F32)],
        compiler_params=_params(("arbitrary",)),
    )(dyg, u, h_re, h_im, lam_re, lam_im, bb_re, bb_im, cc_re, cc_im, d_skip)


def merge_fwd(h, o, yg, gates, wap_t, wv_t, wgg_t, wout, name):
    t, d = h.shape
    tm = TOKEN_TILE

    def body(h_ref, o_ref, yg_ref, gt_ref, wap_ref, wv_ref, wgg_ref, wout_ref, ho_ref, mg_ref, a_ref, sv_ref, sg_ref):
        att = _nt(o_ref[...], wap_ref[...])
        ygv = yg_ref[...]
        sv = _nt(ygv, wv_ref[...])
        sg = _nt(ygv, wgg_ref[...])
        a_ref[...] = att.astype(BF16)
        sv_ref[...] = sv.astype(BF16)
        sg_ref[...] = sg.astype(BF16)
        merged = (jax.nn.sigmoid(gt_ref[:, 0:d].astype(F32)) * att
                  + jax.nn.sigmoid(gt_ref[:, d:2 * d].astype(F32)) * (sv * jax.nn.sigmoid(sg))).astype(BF16)
        mg_ref[...] = merged
        ho_ref[...] = h_ref[...] + _nn(merged, wout_ref[...])

    return pl.pallas_call(
        body, name=name, grid=(t // tm,),
        in_specs=[_row_spec(tm, d), _row_spec(tm, ATTN_WIDTH), _row_spec(tm, SSM_WIDTH), _row_spec(tm, 2 * d),
                  _VMEM, _VMEM, _VMEM, _VMEM],
        out_specs=[_row_spec(tm, d), _row_spec(tm, d), _row_spec(tm, d), _row_spec(tm, d), _row_spec(tm, d)],
        out_shape=[jax.ShapeDtypeStruct((t, d), F32), jax.ShapeDtypeStruct((t, d), BF16),
                   jax.ShapeDtypeStruct((t, d), BF16), jax.ShapeDtypeStruct((t, d), BF16),
                   jax.ShapeDtypeStruct((t, d), BF16)],
        compiler_params=_params(("arbitrary",)),
    )(h, o, yg, gates, wap_t, wv_t, wgg_t, wout)


def merge_bwd(dh, gates, att, sv, sg, wap_t, wv_t, wgg_t, wout, dep, name):
    t, d = dh.shape
    tm = TOKEN_TILE

    def body(dh_ref, gt_ref, a_ref, sv_ref, sg_ref, wap_ref, wv_ref, wgg_ref, wout_ref, dep_ref,
             dgt_ref, da_ref, dsv_ref, dsg_ref, do_ref, dyg_ref, dhb_ref):
        dhb = dh_ref[...].astype(BF16)
        dhb_ref[...] = dhb
        dm = _nt(dhb, wout_ref[...])
        sig_a = jax.nn.sigmoid(gt_ref[:, 0:d].astype(F32))
        sig_s = jax.nn.sigmoid(gt_ref[:, d:2 * d].astype(F32))
        sig_g = jax.nn.sigmoid(sg_ref[...].astype(F32))
        svv = sv_ref[...].astype(F32)
        dgt_ref[:, 0:d] = (dm * a_ref[...].astype(F32) * sig_a * (1.0 - sig_a)).astype(BF16)
        dgt_ref[:, d:2 * d] = (dm * (svv * sig_g) * sig_s * (1.0 - sig_s)).astype(BF16)
        da = (dm * sig_a).astype(BF16)
        d_s = dm * sig_s
        dsv = (d_s * sig_g).astype(BF16)
        dsg = (d_s * svv * sig_g * (1.0 - sig_g)).astype(BF16)
        da_ref[...] = da
        dsv_ref[...] = dsv
        dsg_ref[...] = dsg
        do_ref[...] = _nn(da, wap_ref[...]).astype(BF16)
        dyg = _nn(dsv, wv_ref[...]) + _nn(dsg, wgg_ref[...])
        for j in range(N_CHUNK):
            dyg_ref[j] = dyg[:, j * U_CHUNK:(j + 1) * U_CHUNK]

    return pl.pallas_call(
        body, name=name, grid=(t // tm,),
        in_specs=[_row_spec(tm, d), _row_spec(tm, 2 * d), _row_spec(tm, d), _row_spec(tm, d), _row_spec(tm, d),
                  _VMEM, _VMEM, _VMEM, _VMEM, _ANY],
        out_specs=[_row_spec(tm, 2 * d), _row_spec(tm, d), _row_spec(tm, d), _row_spec(tm, d),
                   _row_spec(tm, ATTN_WIDTH), _chunked_spec(tm, lambda i: i), _row_spec(tm, d)],
        out_shape=[jax.ShapeDtypeStruct((t, 2 * d), BF16), jax.ShapeDtypeStruct((t, d), BF16),
                   jax.ShapeDtypeStruct((t, d), BF16), jax.ShapeDtypeStruct((t, d), BF16),
                   jax.ShapeDtypeStruct((t, ATTN_WIDTH), BF16), jax.ShapeDtypeStruct((N_CHUNK, t, U_CHUNK), F32),
                   jax.ShapeDtypeStruct((t, d), BF16)],
        compiler_params=_params(("arbitrary",)),
    )(dh, gates, att, sv, sg, wap_t, wv_t, wgg_t, wout, dep)


def _adamw_math(w, g, m, v):
    mn = ADAM_B1 * m + (1.0 - ADAM_B1) * g
    vn = ADAM_B2 * v + (1.0 - ADAM_B2) * (g * g)
    m_hat = mn / (1.0 - ADAM_B1 ** ADAM_STEP)
    v_hat = vn / (1.0 - ADAM_B2 ** ADAM_STEP)
    return -ADAM_LR * (m_hat / (jnp.sqrt(v_hat) + ADAM_EPS) + ADAM_WD * w), mn, vn


def sum_adamw_layer(me, landed, partial, w, m, v, layer, prev, name):
    _, rows, cols = w.shape
    tr = rows // 2 if rows % 32 == 0 else rows
    steps = rows // tr

    def body(me_ref, land_ref, own_ref, w_ref, m_ref, v_ref, *rest):
        go_ref, d_ref, mo_ref, vo_ref = rest[-4:]
        who = me_ref[0]
        gv = land_ref[who ^ 1].astype(F32)
        for p in range(2, N_DEV):
            gv = gv + land_ref[who ^ p].astype(F32)
        gv = gv + own_ref[...].astype(F32)
        go_ref[0] = gv
        d_ref[0], mo_ref[0], vo_ref[0] = _adamw_math(w_ref[0], gv, m_ref[0], v_ref[0])

    spec3 = pl.BlockSpec((1, tr, cols), lambda r, me_ref: (layer, r, 0))
    out = jax.ShapeDtypeStruct(w.shape, F32)
    extra = [] if prev is None else list(prev)
    grid_spec = pltpu.PrefetchScalarGridSpec(
        num_scalar_prefetch=1, grid=(steps,),
        in_specs=[pl.BlockSpec((N_DEV, tr, cols), lambda r, me_ref: (0, r, 0)),
                  pl.BlockSpec((tr, cols), lambda r, me_ref: (me_ref[0] * steps + r, 0)),
                  spec3, spec3, spec3] + [_ANY] * len(extra),
        out_specs=[spec3] * 4)
    return pl.pallas_call(
        body, name=name, grid_spec=grid_spec, out_shape=[out] * 4,
        input_output_aliases={6 + j: j for j in range(len(extra))},
        compiler_params=_params(("arbitrary",)),
    )(me, landed, partial, w, m, v, *extra)


def adamw_small(params, name):
    def as2d(a, swap):
        a = jnp.swapaxes(a, -1, -2) if swap else a
        return a.reshape(-1, a.shape[-1]) if a.ndim >= 2 else a.reshape(1, -1)

    n = len(params)
    flat = [as2d(a, swap) for w, g, m, v, swap in params for a in (w, g, m, v)]

    def body(*refs):
        ins, outs = refs[:4 * n], refs[4 * n:]
        for k in range(n):
            w_ref, g_ref, m_ref, v_ref = ins[4 * k:4 * k + 4]
            outs[3 * k][...], outs[3 * k + 1][...], outs[3 * k + 2][...] = _adamw_math(
                w_ref[...], g_ref[...], m_ref[...], v_ref[...])

    outs = pl.pallas_call(
        body, name=name, in_specs=[_VMEM] * (4 * n), out_specs=[_VMEM] * (3 * n),
        out_shape=[jax.ShapeDtypeStruct(flat[4 * k].shape, F32) for k in range(n) for _ in range(3)],
        compiler_params=_params(),
    )(*flat)

    def restore(a, like, swap):
        shape = jnp.swapaxes(like, -1, -2).shape if swap else like.shape
        a = a.reshape(shape)
        return jnp.swapaxes(a, -1, -2) if swap else a

    return [tuple(restore(outs[3 * k + j], params[k][0], params[k][4]) for j in range(3)) for k in range(n)]


def _my_index():
    return 4 * lax.axis_index("x") + 2 * lax.axis_index("y") + lax.axis_index("c")


def _peer(p):
    return (lax.axis_index("x") ^ ((p >> 2) & 1), lax.axis_index("y") ^ ((p >> 1) & 1), lax.axis_index("c") ^ (p & 1))


_HBM = pl.BlockSpec(memory_space=pltpu.HBM)
_SEM = pl.BlockSpec(memory_space=pltpu.SEMAPHORE)
_EFFECT = pltpu.SideEffectType.DATAFLOW_SIDE_EFFECTING


class Exchange:
    RELAYED = (2, 4, 6)

    def __init__(self, srcs, scatter, name, relay=False):
        self.n = n = len(srcs)
        self.scatter = scatter
        self.name = name
        self.relayed = relay
        assert not (relay and scatter)
        self.direct = (1,) + self.RELAYED if relay else tuple(range(1, N_DEV))
        widths = sorted({s.shape[1] for s in srcs}, reverse=True)
        self.ncls = len(widths)
        self.cls = [widths.index(s.shape[1]) for s in srcs]
        self.cnts = [s.shape[0] // N_DEV if scatter else s.shape[0] for s in srcs]
        self.totals = [sum(c for c, k in zip(self.cnts, self.cls) if k == w) for w in range(self.ncls)]
        self.sizer = [max((k for k in range(n) if self.cls[k] == w), key=lambda k: self.cnts[k])
                      for w in range(self.ncls)]
        assert all(N_DEV * self.cnts[self.sizer[w]] >= self.totals[w] for w in range(self.ncls))
        if scatter:
            self.land_shapes = [(N_DEV, c, s.shape[1]) for s, c in zip(srcs, self.cnts)]
        else:
            self.land_shapes = [(N_DEV * c, s.shape[1]) for s, c in zip(srcs, self.cnts)]
        self.dtypes = [s.dtype for s in srcs]

    def _block(self, k, who):
        return pl.ds(pl.multiple_of(who * self.cnts[k], 16), self.cnts[k])

    def _sem(self, p, w):
        return (p - 1) * self.ncls + w

    def start(self, srcs, after):
        n = self.n

        def body(*refs):
            src, land = refs[:n], refs[n:2 * n]
            send_sems, recv_sems = refs[2 * n + 1], refs[2 * n + 2]
            token = refs[-1]
            me = _my_index()
            for p in self.direct:
                for k in range(n):
                    if self.scatter:
                        s_ref, d_ref = src[k].at[self._block(k, me ^ p), :], land[k].at[me]
                    else:
                        s_ref, d_ref = src[k], land[k].at[self._block(k, me), :]
                    pltpu.make_async_remote_copy(
                        src_ref=s_ref, dst_ref=d_ref, send_sem=send_sems.at[self._sem(p, self.cls[k])],
                        recv_sem=recv_sems.at[self._sem(p, self.cls[k])], device_id=_peer(p),
                        device_id_type=MESH).start()
            token[...] = jnp.zeros_like(token)

        sems = pltpu.SemaphoreType.DMA(((N_DEV - 1) * self.ncls,))
        thru = [pltpu.HBM(s.shape, s.dtype) for s in srcs] + [pltpu.HBM(shp, dt) for shp, dt in
                                                               zip(self.land_shapes, self.dtypes)]
        lands = [pltpu.with_memory_space_constraint(lax.empty(shp, dt), pltpu.HBM)
                 for shp, dt in zip(self.land_shapes, self.dtypes)]
        out = pl.pallas_call(
            body, name=self.name + "_start",
            in_specs=[_HBM] * (2 * n) + [_ANY],
            out_shape=[sems, sems] + thru + [jax.ShapeDtypeStruct((8, 128), F32)],
            out_specs=[_SEM, _SEM] + [_HBM] * (2 * n) + [_VMEM],
            input_output_aliases={j: 2 + j for j in range(2 * n)},
            compiler_params=pltpu.CompilerParams(has_side_effects=_EFFECT),
        )(*[pltpu.with_memory_space_constraint(s, pltpu.HBM) for s in srcs], *lands, after)
        return out[:-1], out[-1]

    def _span_copy(self, src, land, w, send_sem, recv_sem, p):
        big = src[self.sizer[w]] if self.scatter else land[self.sizer[w]]
        span = big.at[pl.ds(0, self.totals[w]), :]
        return pltpu.make_async_remote_copy(src_ref=span, dst_ref=span, send_sem=send_sem, recv_sem=recv_sem,
                                            device_id=_peer(p), device_id_type=MESH)

    def relay(self, state, after):
        n = self.n
        send_sems, recv_sems = state[0], state[1]
        thru = state[2:]
        after = list(after) if isinstance(after, (list, tuple)) else [after]
        first_out = 2 * n + 2 + len(after)

        def body(*refs):
            land = refs[n:2 * n]
            send_a, recv_a = refs[2 * n], refs[2 * n + 1]
            send_b, recv_b = refs[first_out], refs[first_out + 1]
            refs[-1][...] = jnp.zeros_like(refs[-1])
            me = _my_index()
            for p in self.RELAYED:
                for w in range(self.ncls):
                    self._span_copy(None, land, w, send_a.at[self._sem(p, w)], recv_a.at[self._sem(p, w)], p).wait_recv()
            for j, p in enumerate(self.RELAYED):
                for k in range(n):
                    rows = land[k].at[self._block(k, me ^ p), :]
                    pltpu.make_async_remote_copy(
                        src_ref=rows, dst_ref=rows, send_sem=send_b.at[j * self.ncls + self.cls[k]],
                        recv_sem=recv_b.at[j * self.ncls + self.cls[k]], device_id=_peer(1),
                        device_id_type=MESH).start()

        sems = pltpu.SemaphoreType.DMA((len(self.RELAYED) * self.ncls,))
        out = pl.pallas_call(
            body, name=self.name + "_relay",
            in_specs=[_HBM] * (2 * n) + [_SEM, _SEM] + [_ANY] * len(after),
            out_shape=[sems, sems] + [pltpu.HBM(a.shape, a.dtype) for a in thru] + [jax.ShapeDtypeStruct((8, 128), F32)],
            out_specs=[_SEM, _SEM] + [_HBM] * (2 * n) + [_VMEM],
            input_output_aliases={j: 2 + j for j in range(2 * n)},
            compiler_params=pltpu.CompilerParams(has_side_effects=_EFFECT),
        )(*thru, send_sems, recv_sems, *after)
        return [send_sems, recv_sems] + list(out[2:-1]) + [out[0], out[1]], out[-1]

    def wait(self, state, after):
        n = self.n
        send_sems, recv_sems = state[0], state[1]
        thru = state[2:2 + 2 * n]
        relay_sems = list(state[2 + 2 * n:])
        assert len(relay_sems) == (2 if self.relayed else 0)
        after = list(after) if isinstance(after, (list, tuple)) else [after]

        def body(*refs):
            src, land = refs[:n], refs[n:2 * n]
            send_a, recv_a = refs[2 * n], refs[2 * n + 1]
            for p in self.direct:
                for w in range(self.ncls):
                    copy = self._span_copy(src, land, w, send_a.at[self._sem(p, w)], recv_a.at[self._sem(p, w)], p)
                    copy.wait_send()
                    if not (self.relayed and p in self.RELAYED):
                        copy.wait_recv()
            if self.relayed:
                send_b, recv_b = refs[2 * n + 2], refs[2 * n + 3]
                for j in range(len(self.RELAYED)):
                    for w in range(self.ncls):
                        copy = self._span_copy(src, land, w, send_b.at[j * self.ncls + w],
                                               recv_b.at[j * self.ncls + w], 1)
                        copy.wait_send()
                        copy.wait_recv()

        out = pl.pallas_call(
            body, name=self.name + "_wait",
            in_specs=[_HBM] * (2 * n) + [_SEM] * (2 + len(relay_sems)) + [_ANY] * len(after),
            out_shape=[pltpu.HBM(a.shape, a.dtype) for a in thru], out_specs=[_HBM] * (2 * n),
            input_output_aliases={j: j for j in range(2 * n)},
            compiler_params=pltpu.CompilerParams(has_side_effects=_EFFECT),
        )(*thru, send_sems, recv_sems, *relay_sems, *after)
        return out[:n], out[n:]

    def place(self, lands, srcs):
        n = self.n
        assert not self.scatter

        def body(*refs):
            src, land = refs[n:2 * n], refs[2 * n:3 * n]
            bufs, sems = refs[3 * n:4 * n], refs[-1]
            me = _my_index()
            loads = [pltpu.make_async_copy(src[k], bufs[k], sems.at[k]) for k in range(n)]
            stores = [pltpu.make_async_copy(bufs[k], land[k].at[self._block(k, me), :], sems.at[k]) for k in range(n)]
            for cp in loads:
                cp.start()
            for k in range(n):
                loads[k].wait()
                stores[k].start()
            for cp in stores:
                cp.wait()

        return pl.pallas_call(
            body, name=self.name + "_place", in_specs=[_ANY] * (2 * n), out_specs=[_ANY] * n,
            out_shape=[jax.ShapeDtypeStruct(a.shape, a.dtype) for a in lands],
            input_output_aliases={j: j for j in range(n)},
            scratch_shapes=[pltpu.VMEM(s.shape, s.dtype) for s in srcs] + [pltpu.SemaphoreType.DMA((n,))],
        )(*lands, *srcs)


def sum_slots(slots, name):
    _, rows, cols = slots.shape
    tr = rows
    if rows > 512:
        for cand in (256, 128, 64, 32, 16, 8):
            if rows % cand == 0:
                tr = cand
                break

    def body(s_ref, o_ref):
        acc = s_ref[0].astype(F32)
        for j in range(1, N_DEV):
            acc = acc + s_ref[j].astype(F32)
        o_ref[...] = acc

    return pl.pallas_call(
        body, name=name, grid=(rows // tr,),
        in_specs=[pl.BlockSpec((N_DEV, tr, cols), lambda i: (0, i, 0))], out_specs=_row_spec(tr, cols),
        out_shape=jax.ShapeDtypeStruct((rows, cols), F32), compiler_params=_params(("arbitrary",)),
    )(slots)


BIG_N = ("ffn1_w_down", "w_out", "ffn2_w_down")
SMALL = ("ffn1_norm", "mix_norm", "attn_sinks", "ssm_a_re", "ssm_a_im", "ssm_log_dt", "ssm_b_re", "ssm_b_im",
         "ssm_c_re", "ssm_c_im", "ssm_d", "ffn2_norm", "final_norm")
PARTS = {"ffn1": ("ffn1_w_gate", "ffn1_w_up", "ffn1_w_down"),
         "mix": ("w_in", "w_out", "w_attn_proj", "w_glu_v", "w_glu_g"),
         "ffn2": ("ffn2_w_gate", "ffn2_w_up", "ffn2_w_down")}


def _to_rows(name, a):
    return a if name in BIG_N else jnp.swapaxes(a, -1, -2)


def local_step(x, tgt, get_weights, put_grads, small):
    seq, d = x.shape
    t = PAD_FRONT + N_META + seq
    cos_t, sin_t = rope_tables(t)
    row = lambda a: a.reshape(1, -1)
    tables = []
    for i in range(DEPTH):
        b_re_t = jnp.swapaxes(small["ssm_b_re"][i], 1, 2)
        b_im_t = jnp.swapaxes(small["ssm_b_im"][i], 1, 2)
        lam_re, lam_im, bbar_re, bbar_im = ssm_prep(small["ssm_a_re"][i], small["ssm_a_im"][i],
                                                    small["ssm_log_dt"][i].reshape(-1, 1), b_re_t, b_im_t, f"ssm_prep_{i}")
        tables.append(((b_re_t, b_im_t),
                       (row(lam_re), row(lam_im), _block_diag_b(bbar_re).astype(BF16), _block_diag_b(bbar_im).astype(BF16),
                        _block_diag_c(small["ssm_c_re"][i]).astype(BF16), _block_diag_c(small["ssm_c_im"][i]).astype(BF16),
                        row(small["ssm_d"][i]))))
    early = [cos_t, sin_t] + [a for _, tab in tables for a in tab[2:6]]
    saved = []
    h = None
    for i in range(DEPTH):
        s = {}
        w = dict(get_weights(i, "ffn1", early if i == 0 else h))
        if i == 0:
            h = jnp.concatenate([jnp.zeros((PAD_FRONT, d), F32), w["meta_tokens"], x], axis=0)
        s["h0"] = h
        h, s["n1"], s["acts1"] = ffn_fwd(h, row(small["ffn1_norm"][i]), w["ffn1_w_gate"], w["ffn1_w_up"],
                                               w["ffn1_w_down"], f"ffn1_fwd_{i}")
        s["h1"] = h
        w.update(get_weights(i, "mix", h))
        s["n2"], s["qkv"], s["u"], s["gates"] = win_fwd(h, row(small["mix_norm"][i]), w["w_in"], cos_t, sin_t,
                                                        f"win_fwd_{i}")
        s["b_t"], s["ssm"] = tables[i]
        s["yg"], s["h_re"], s["h_im"] = ssm_fwd(s["u"], *s["ssm"], f"ssm_fwd_{i}")
        s["o"] = attn_fwd(s["qkv"], row(small["attn_sinks"][i]), f"attn_fwd_{i}")
        h, s["merged"], s["att"], s["sv"], s["sg"] = merge_fwd(
            h, s["o"], s["yg"], s["gates"], w["w_attn_proj"], w["w_glu_v"], w["w_glu_g"], w["w_out"],
            f"merge_fwd_{i}")
        s["h2"] = h
        w.update(get_weights(i, "ffn2", h))
        h, s["n3"], s["acts3"] = ffn_fwd(h, row(small["ffn2_norm"][i]), w["ffn2_w_gate"], w["ffn2_w_up"],
                                               w["ffn2_w_down"], f"ffn2_fwd_{i}")
        s["w"] = w
        saved.append(s)

    loss, dh, d_final = head_fwd_bwd(h, row(small["final_norm"]), tgt)
    gs = {k: [None] * DEPTH for k in SMALL if k != "final_norm"}
    dep = loss
    for i in reversed(range(DEPTH)):
        s = saved[i]
        w = s["w"]
        dh, da, db, sact, dhb, dg = ffn_bwd(dh, s["h2"], row(small["ffn2_norm"][i]), s["acts3"], w["ffn2_w_gate"],
                                            w["ffn2_w_up"], w["ffn2_w_down"], dep, f"ffn2_bwd_{i}")
        gs["ffn2_norm"][i] = dg[0]
        dep = put_grads(i, "ffn2", {"ffn2_w_gate": tn_matmul(da, s["n3"], f"ffn2_dwg_{i}"),
                                    "ffn2_w_up": tn_matmul(db, s["n3"], f"ffn2_dwu_{i}"),
                                    "ffn2_w_down": tn_matmul(sact, dhb, f"ffn2_dwd_{i}")})

        dgates, datt, dsv, dsg, do, dyg, dhb = merge_bwd(dh, s["gates"], s["att"], s["sv"], s["sg"], w["w_attn_proj"],
                                                         w["w_glu_v"], w["w_glu_g"], w["w_out"], dep, f"merge_bwd_{i}")
        gmix = {"w_out": tn_matmul(s["merged"], dhb, f"dwout_{i}"),
                "w_attn_proj": tn_matmul(datt, s["o"], f"dwap_{i}"),
                "w_glu_v": tn_matmul(dsv, s["yg"], f"dwv_{i}"),
                "w_glu_g": tn_matmul(dsg, s["yg"], f"dwgg_{i}")}
        dqkv, dsink = attn_bwd(s["qkv"], do, row(small["attn_sinks"][i]), cos_t, sin_t, f"attn_bwd_{i}")
        gs["attn_sinks"][i] = dsink[:, 0]
        du, dl_re, dl_im, dbb_re, dbb_im, dcc_re, dcc_im, dd = ssm_bwd(dyg, s["u"], s["h_re"], s["h_im"], *s["ssm"],
                                                                      f"ssm_bwd_{i}")
        fold = lambda a: jnp.sum(a, axis=0).reshape(SSM_GROUPS, SSM_STATE)
        da_re, da_im, dldt, db_re_t, db_im_t = ssm_prep_bwd(
            small["ssm_a_re"][i], small["ssm_a_im"][i], small["ssm_log_dt"][i].reshape(-1, 1), *s["b_t"],
            fold(dl_re), fold(dl_im), _diag_of_b(dbb_re), _diag_of_b(dbb_im), f"ssm_prep_bwd_{i}")
        gs["ssm_a_re"][i], gs["ssm_a_im"][i], gs["ssm_log_dt"][i] = da_re, da_im, dldt[:, 0]
        gs["ssm_b_re"][i], gs["ssm_b_im"][i] = jnp.swapaxes(db_re_t, 1, 2), jnp.swapaxes(db_im_t, 1, 2)
        gs["ssm_c_re"][i], gs["ssm_c_im"][i] = _diag_of_c(dcc_re), _diag_of_c(dcc_im)
        gs["ssm_d"][i] = dd[0]
        gmix["w_in"] = tn_matmul([dqkv, du, dgates], s["n2"], f"dwin_{i}")
        dep = put_grads(i, "mix", gmix)
        dh, dg = win_bwd(dh, s["h1"], row(small["mix_norm"][i]), dqkv, du, dgates, w["w_in"], dep, f"win_bwd_{i}")
        gs["mix_norm"][i] = dg[0]

        dh, da, db, sact, dhb, dg = ffn_bwd(dh, s["h0"], row(small["ffn1_norm"][i]), s["acts1"], w["ffn1_w_gate"],
                                            w["ffn1_w_up"], w["ffn1_w_down"], dep, f"ffn1_bwd_{i}")
        gs["ffn1_norm"][i] = dg[0]
        if i > 0:
            dep = put_grads(i, "ffn1", {"ffn1_w_gate": tn_matmul(da, s["n1"], f"ffn1_dwg_{i}"),
                                        "ffn1_w_up": tn_matmul(db, s["n1"], f"ffn1_dwu_{i}"),
                                        "ffn1_w_down": tn_matmul(sact, dhb, f"ffn1_dwd_{i}")})
        else:
            for k, xa, ya in (("ffn1_w_down", sact, dhb), ("ffn1_w_gate", da, s["n1"]), ("ffn1_w_up", db, s["n1"])):
                dep = put_grads(i, "ffn1", {k: tn_matmul(xa, ya, f"d_{k}_{i}", dep)})

    gs = {k: jnp.stack(v) for k, v in gs.items()}
    gs["final_norm"] = d_final[0]
    return loss[0, 0], dh[PAD_FRONT + N_META:], dh[PAD_FRONT:PAD_FRONT + N_META], gs, dep


def _pack_rows(arrays, cols):
    flat = jnp.concatenate([a.reshape(-1) for a in arrays])
    rows = -(-flat.shape[0] // cols)
    rows = -(-rows // 16) * 16
    return jnp.pad(flat, (0, rows * cols - flat.shape[0])).reshape(rows, cols)


def _unpack_rows(packed, shapes):
    flat = packed.reshape(-1)
    out, off = [], 0
    for shp in shapes:
        n = math.prod(shp)
        out.append(flat[off:off + n].reshape(shp))
        off += n
    return out


def kernel(x, meta_tokens, ffn1_norm, ffn1_w_gate, ffn1_w_up, ffn1_w_down, mix_norm, w_in, attn_sinks, ssm_a_re, ssm_a_im, ssm_log_dt, ssm_b_re, ssm_b_im, ssm_c_re, ssm_c_im, ssm_d, w_attn_proj, w_glu_v, w_glu_g, w_out, ffn2_norm, ffn2_w_gate, ffn2_w_up, ffn2_w_down, final_norm, loss_target, m_meta_tokens, m_ffn1_norm, m_ffn1_w_gate, m_ffn1_w_up, m_ffn1_w_down, m_mix_norm, m_w_in, m_attn_sinks, m_ssm_a_re, m_ssm_a_im, m_ssm_log_dt, m_ssm_b_re, m_ssm_b_im, m_ssm_c_re, m_ssm_c_im, m_ssm_d, m_w_attn_proj, m_w_glu_v, m_w_glu_g, m_w_out, m_ffn2_norm, m_ffn2_w_gate, m_ffn2_w_up, m_ffn2_w_down, m_final_norm, v_meta_tokens, v_ffn1_norm, v_ffn1_w_gate, v_ffn1_w_up, v_ffn1_w_down, v_mix_norm, v_w_in, v_attn_sinks, v_ssm_a_re, v_ssm_a_im, v_ssm_log_dt, v_ssm_b_re, v_ssm_b_im, v_ssm_c_re, v_ssm_c_im, v_ssm_d, v_w_attn_proj, v_w_glu_v, v_w_glu_g, v_w_out, v_ffn2_norm, v_ffn2_w_gate, v_ffn2_w_up, v_ffn2_w_down, v_final_norm):
    names = ("meta_tokens", "ffn1_norm", "ffn1_w_gate", "ffn1_w_up", "ffn1_w_down", "mix_norm", "w_in", "attn_sinks",
             "ssm_a_re", "ssm_a_im", "ssm_log_dt", "ssm_b_re", "ssm_b_im", "ssm_c_re", "ssm_c_im", "ssm_d",
             "w_attn_proj", "w_glu_v", "w_glu_g", "w_out", "ffn2_norm", "ffn2_w_gate", "ffn2_w_up", "ffn2_w_down",
             "final_norm")
    weights = dict(zip(names, (meta_tokens, ffn1_norm, ffn1_w_gate, ffn1_w_up, ffn1_w_down, mix_norm, w_in, attn_sinks, ssm_a_re, ssm_a_im, ssm_log_dt, ssm_b_re, ssm_b_im, ssm_c_re, ssm_c_im, ssm_d, w_attn_proj, w_glu_v, w_glu_g, w_out, ffn2_norm, ffn2_w_gate, ffn2_w_up, ffn2_w_down, final_norm)))
    moments_m = dict(zip(names, (m_meta_tokens, m_ffn1_norm, m_ffn1_w_gate, m_ffn1_w_up, m_ffn1_w_down, m_mix_norm, m_w_in, m_attn_sinks, m_ssm_a_re, m_ssm_a_im, m_ssm_log_dt, m_ssm_b_re, m_ssm_b_im, m_ssm_c_re, m_ssm_c_im, m_ssm_d, m_w_attn_proj, m_w_glu_v, m_w_glu_g, m_w_out, m_ffn2_norm, m_ffn2_w_gate, m_ffn2_w_up, m_ffn2_w_down, m_final_norm)))
    moments_v = dict(zip(names, (v_meta_tokens, v_ffn1_norm, v_ffn1_w_gate, v_ffn1_w_up, v_ffn1_w_down, v_mix_norm, v_w_in, v_attn_sinks, v_ssm_a_re, v_ssm_a_im, v_ssm_log_dt, v_ssm_b_re, v_ssm_b_im, v_ssm_c_re, v_ssm_c_im, v_ssm_d, v_w_attn_proj, v_w_glu_v, v_w_glu_g, v_w_out, v_ffn2_norm, v_ffn2_w_gate, v_ffn2_w_up, v_ffn2_w_down, v_final_norm)))
    me = _my_index()

    order = [(i, part) for i in range(DEPTH) for part in PARTS]
    gathers = {}
    token = jnp.zeros((8, 128), F32)
    for i, part in order:
        shards = [_to_rows(k, weights[k][i]).astype(BF16) for k in PARTS[part]]
        if (i, part) == order[0]:
            shards.append(meta_tokens)
        ex = Exchange(shards, False, f"gather_{part}_{i}", relay=True)
        state, token = ex.start(shards, token)
        gathers[i, part] = [ex, state, False]
    all_started = token

    def relay(group, after):
        ex, state, relayed = gathers[group]
        if relayed:
            return []
        new_state, relay_token = ex.relay(state, after)
        gathers[group][1:] = [new_state, True]
        return [relay_token]

    def get_weights(i, part, after):
        g = order.index((i, part))
        after = [all_started] + list(after) if g == 0 else [after]
        tokens = relay(order[g], after)
        if g >= 2 and g + 1 < len(order):
            tokens += relay(order[g + 1], after)
        ex, state, _ = gathers[i, part]
        shards, lands = ex.wait(state, after + tokens)
        fulls = ex.place(lands, shards)
        got = dict(zip(PARTS[part], fulls))
        if (i, part) == (0, "ffn1"):
            got["meta_tokens"] = jnp.swapaxes(fulls[-1].reshape(N_DEV, N_META, 128), 0, 1).reshape(N_META, D_MODEL)
        return got

    scatters = []

    def put_grads(i, part, gdict):
        ks = list(gdict)
        srcs = [gdict[k] for k in ks]
        ex = Exchange(srcs, True, f"scatter_{part if len(ks) > 1 else ks[0]}_{i}")
        state, tok = ex.start(srcs, all_started)
        scatters.append((i, ks, ex, state))
        return tok

    small = {k: weights[k] for k in SMALL}
    loss, dx, dmeta, gs, last_started = local_step(x[0], loss_target[0], get_weights, put_grads, small)

    grads, deltas, new_m, new_v = {}, {}, {}, {}
    small_list = [loss.reshape(1), dmeta] + [gs[k] for k in SMALL]
    packed = _pack_rows(small_list, D_MODEL)
    small_ex = Exchange([packed], False, "gather_small", relay=True)
    small_state, after = small_ex.start([packed], last_started)

    updated = {}
    me_index = jnp.reshape(me, (1,)).astype(jnp.int32)
    for i, ks, ex, state in scatters:
        partials, lands = ex.wait(state, after)
        for k, partial, slots in zip(ks, partials, lands):
            updated[k] = sum_adamw_layer(me_index, slots, partial, _to_rows(k, weights[k]), _to_rows(k, moments_m[k]),
                                         _to_rows(k, moments_v[k]), i, updated.get(k), f"adamw_{k}_{i}")
            after = updated[k][0]
    for k, outs in updated.items():
        grads[k], deltas[k], new_m[k], new_v[k] = [_to_rows(k, a) for a in outs]

    small_state, relayed = small_ex.relay(small_state, after)
    packed_own, packed_all = small_ex.wait(small_state, [after, relayed])
    (packed_all,) = small_ex.place(packed_all, packed_own)
    total = sum_slots(packed_all.reshape(N_DEV, packed.shape[0], D_MODEL), "sum_small")
    pieces = _unpack_rows(total, [a.shape for a in small_list])
    loss_out = pieces[0][0]
    grads["meta_tokens"] = lax.dynamic_slice_in_dim(pieces[1], me * 128, 128, axis=1)
    for k, p in zip(SMALL, pieces[2:]):
        grads[k] = p
    small_names = ("meta_tokens",) + SMALL
    updates = adamw_small([(weights[k], grads[k], moments_m[k], moments_v[k], k in ("ssm_b_re", "ssm_b_im"))
                           for k in small_names], "adamw_small")
    for k, (d, mn, vn) in zip(small_names, updates):
        deltas[k], new_m[k], new_v[k] = d, mn, vn
    return (loss_out, dx[None], *[grads[k] for k in names], *[deltas[k] for k in names],
            *[new_m[k] for k in names], *[new_v[k] for k in names])
```

```python
import math

import jax
import jax.numpy as jnp
from jax import lax
from jax.experimental import pallas as pl
from jax.experimental.pallas import tpu as pltpu

F32 = jnp.float32
BF16 = jnp.bfloat16

D_MODEL = 1024
DEPTH = 2
N_META = 16
HEAD_DIM = 64
N_Q_HEADS = 8
ATTN_WIDTH = 512
KV_WIDTH = 128
QKV_WIDTH = ATTN_WIDTH + 2 * KV_WIDTH
WINDOW = 128
BLK = 128
ROPE_THETA = 500000.0
ROT_DIM = 16
SSM_WIDTH = 512
SSM_GROUP = 16
SSM_GROUPS = 32
SSM_STATE = 64
STATE_WIDTH = SSM_GROUPS * SSM_STATE
D_FF = 2816
IN_WIDTH = 3328
EPS = 1e-6
NEG_INF = -1e30
PAD_FRONT = (-N_META) % BLK
N_DEV = 8

ADAM_LR = 0.001
ADAM_B1 = 0.9
ADAM_B2 = 0.999
ADAM_EPS = 1e-08
ADAM_WD = 0.01
ADAM_STEP = 10

VMEM_LIMIT = 56 * 1024 * 1024
TOKEN_TILE = 384
_VMEM = pl.BlockSpec(memory_space=pltpu.VMEM)
_SMEM = pl.BlockSpec(memory_space=pltpu.SMEM)
_ANY = pl.BlockSpec(memory_space=pl.ANY)
MESH = pl.DeviceIdType.MESH


def _params(sem=None):
    return pltpu.CompilerParams(dimension_semantics=sem, vmem_limit_bytes=VMEM_LIMIT)


def _nt(a, b):
    return lax.dot_general(a, b, (((1,), (1,)), ((), ())), preferred_element_type=F32)


def _nn(a, b):
    return jnp.dot(a, b, preferred_element_type=F32)


def _tn(a, b):
    return lax.dot_general(a, b, (((0,), (0,)), ((), ())), preferred_element_type=F32)


def _row_spec(tm, width):
    return pl.BlockSpec((tm, width), lambda i: (i, 0))


def _acc_spec(shape):
    return pl.BlockSpec(shape, lambda i: (0,) * len(shape))


def _rms_stats(x):
    r = lax.rsqrt(jnp.mean(x * x, axis=-1, keepdims=True) + EPS)
    return x * r, r


def _rms_bwd(dn, xh, r, g):
    dg = jnp.sum(dn * xh, axis=0, keepdims=True)
    dxh = dn * g
    dx = r * (dxh - xh * jnp.mean(dxh * xh, axis=-1, keepdims=True))
    return dx, dg


def ffn_fwd(h, g, wg_t, wu_t, wd, name):
    t, d = h.shape
    f = wd.shape[0]
    tm = TOKEN_TILE

    def body(h_ref, g_ref, wg_ref, wu_ref, wd_ref, ho_ref, n_ref, sl_ref, p_ref, s_ref):
        x = h_ref[...]
        xh, _ = _rms_stats(x)
        n = (xh * g_ref[...]).astype(BF16)
        n_ref[...] = n
        a = _nt(n, wg_ref[...])
        b = _nt(n, wu_ref[...])
        sig = jax.nn.sigmoid(a)
        sl = a * sig
        sl_ref[...] = sl.astype(BF16)
        p_ref[...] = (b * (sig + sl * (1.0 - sig))).astype(BF16)
        s = (sl * b).astype(BF16)
        s_ref[...] = s
        ho_ref[...] = x + 0.5 * _nn(s, wd_ref[...])

    ho, n, sl, p, s = pl.pallas_call(
        body, name=name, grid=(t // tm,),
        in_specs=[_row_spec(tm, d), _acc_spec((1, d)), _VMEM, _VMEM, _VMEM],
        out_specs=[_row_spec(tm, d), _row_spec(tm, d), _row_spec(tm, f), _row_spec(tm, f), _row_spec(tm, f)],
        out_shape=[jax.ShapeDtypeStruct((t, d), F32), jax.ShapeDtypeStruct((t, d), BF16),
                   jax.ShapeDtypeStruct((t, f), BF16), jax.ShapeDtypeStruct((t, f), BF16),
                   jax.ShapeDtypeStruct((t, f), BF16)],
        compiler_params=_params(("arbitrary",)),
    )(h, g, wg_t, wu_t, wd)
    return ho, n, (sl, p, s)


def ffn_bwd(dh, h, g, acts, wg_t, wu_t, wd, dep, name):
    t, d = h.shape
    f = wd.shape[0]
    tm = TOKEN_TILE
    sl, p, s = acts

    def hidden_body(dh_ref, sl_ref, p_ref, wd_ref, dep_ref, da_ref, db_ref, dhb_ref):
        dhb = (0.5 * dh_ref[...]).astype(BF16)
        dhb_ref[...] = dhb
        ds = _nt(dhb, wd_ref[...])
        da_ref[...] = (ds * p_ref[...].astype(F32)).astype(BF16)
        db_ref[...] = (ds * sl_ref[...].astype(F32)).astype(BF16)

    da, db, dhb = pl.pallas_call(
        hidden_body, name=name + "_h", grid=(t // tm,),
        in_specs=[_row_spec(tm, d), _row_spec(tm, f), _row_spec(tm, f), _VMEM, _ANY],
        out_specs=[_row_spec(tm, f), _row_spec(tm, f), _row_spec(tm, d)],
        out_shape=[jax.ShapeDtypeStruct((t, f), BF16), jax.ShapeDtypeStruct((t, f), BF16),
                   jax.ShapeDtypeStruct((t, d), BF16)],
        compiler_params=_params(("arbitrary",)),
    )(dh, sl, p, wd, dep)

    def input_body(dh_ref, h_ref, g_ref, da_ref, db_ref, wg_ref, wu_ref, dhi_ref, dg_ref):
        dn = _nn(da_ref[...], wg_ref[...]) + _nn(db_ref[...], wu_ref[...])
        xh, r = _rms_stats(h_ref[...])
        dx, dg = _rms_bwd(dn, xh, r, g_ref[...])
        dhi_ref[...] = dh_ref[...] + dx

        @pl.when(pl.program_id(0) == 0)
        def _():
            dg_ref[...] = jnp.zeros_like(dg_ref)

        dg_ref[...] += dg

    dhi, dg = pl.pallas_call(
        input_body, name=name + "_x", grid=(t // tm,),
        in_specs=[_row_spec(tm, d), _row_spec(tm, d), _acc_spec((1, d)), _row_spec(tm, f), _row_spec(tm, f),
                  _VMEM, _VMEM],
        out_specs=[_row_spec(tm, d), _acc_spec((1, d))],
        out_shape=[jax.ShapeDtypeStruct((t, d), F32), jax.ShapeDtypeStruct((1, d), F32)],
        compiler_params=_params(("arbitrary",)),
    )(dh, h, g, da, db, wg_t, wu_t)
    return dhi, da, db, s, dhb, dg


DW_TILE = 256


def tn_matmul(x, y, name, dep=None):
    xs = list(x) if isinstance(x, (list, tuple)) else [x]
    t = xs[0].shape[0]
    n = y.shape[1]
    bm = DW_TILE
    tiles = [a.shape[1] // bm for a in xs]
    offs = [sum(tiles[:k]) for k in range(len(xs))]
    deps = [] if dep is None else [dep]

    def body(*refs):
        y_ref, o_ref = refs[len(xs)], refs[-1]
        i = pl.program_id(0)
        for k in range(len(xs)):
            @pl.when((i >= offs[k]) & (i < offs[k] + tiles[k]))
            def _(k=k):
                o_ref[...] = _tn(refs[k][...], y_ref[...]).astype(BF16)

    def x_spec(k):
        return pl.BlockSpec((t, bm), lambda i: (0, jnp.clip(i - offs[k], 0, tiles[k] - 1)))

    return pl.pallas_call(
        body, name=name, grid=(sum(tiles),),
        in_specs=[x_spec(k) for k in range(len(xs))] + [_VMEM] + [_ANY] * len(deps),
        out_specs=pl.BlockSpec((bm, n), lambda i: (i, 0)),
        out_shape=jax.ShapeDtypeStruct((sum(tiles) * bm, n), BF16),
        compiler_params=_params(("arbitrary",)),
    )(*xs, y, *deps)


def head_fwd_bwd(h, g, tgt):
    t, d = h.shape
    tm = TOKEN_TILE
    per_tile = tm // BLK
    last = tgt.shape[0] // BLK - 1

    def body(h_ref, g_ref, *rest):
        t_refs, (loss_ref, dh_ref, dg_ref) = rest[:per_tile], rest[per_tile:]
        i = pl.program_id(0)
        xh, r = _rms_stats(h_ref[...])
        gv = g_ref[...]
        target = jnp.concatenate([ref[...] for ref in t_refs], axis=0)
        row = i * tm + lax.broadcasted_iota(jnp.int32, (tm, 1), 0)
        e = jnp.where(row >= BLK, xh * gv - target, 0.0)
        dx, dg = _rms_bwd(e * (1.0 / d), xh, r, gv)
        dh_ref[...] = dx

        @pl.when(i == 0)
        def _():
            dg_ref[...] = jnp.zeros_like(dg_ref)
            loss_ref[...] = jnp.zeros_like(loss_ref)

        dg_ref[...] += dg
        loss_ref[...] += jnp.sum(e * e) * (0.5 / d)

    def target_spec(k):
        return pl.BlockSpec((BLK, d), lambda i: (jnp.clip(i * per_tile - 1 + k, 0, last), 0))

    return pl.pallas_call(
        body, name="head", grid=(t // tm,),
        in_specs=[_row_spec(tm, d), _acc_spec((1, d))] + [target_spec(k) for k in range(per_tile)],
        out_specs=[_acc_spec((1, 128)), _row_spec(tm, d), _acc_spec((1, d))],
        out_shape=[jax.ShapeDtypeStruct((1, 128), F32), jax.ShapeDtypeStruct((t, d), F32),
                   jax.ShapeDtypeStruct((1, d), F32)],
        compiler_params=_params(("arbitrary",)),
    )(h, g, *[tgt] * per_tile)


def rope_tables(t):
    pos = jnp.arange(t, dtype=F32) - PAD_FRONT
    inv_freq = ROPE_THETA ** (-jnp.arange(0, ROT_DIM, 2, dtype=F32) / ROT_DIM)
    ang = pos[:, None] * inv_freq[None, :]
    cos, sin = jnp.cos(ang), jnp.sin(ang)
    ones = jnp.ones((t, HEAD_DIM - ROT_DIM), F32)
    cos_h = jnp.concatenate([cos, cos, ones], axis=1)
    sin_h = jnp.concatenate([-sin, sin, 0.0 * ones], axis=1)
    return jnp.concatenate([cos_h, cos_h], axis=1), jnp.concatenate([sin_h, sin_h], axis=1)


def _swap_halves(x):
    n = x.shape[1]
    lane = lax.broadcasted_iota(jnp.int32, x.shape, 1)
    return jnp.where(lane % HEAD_DIM < ROT_DIM // 2, pltpu.roll(x, n - ROT_DIM // 2, 1), pltpu.roll(x, ROT_DIM // 2, 1))


def _rope(x, cos_t, sin_t, sign):
    return x * cos_t + sign * (_swap_halves(x) * sin_t)


def win_fwd(h, g, win_t, cos_t, sin_t, name):
    t, d = h.shape
    tm = TOKEN_TILE

    def body(h_ref, g_ref, w_ref, c_ref, s_ref, n_ref, qkv_ref, u_ref, gates_ref):
        xh, _ = _rms_stats(h_ref[...])
        n = (xh * g_ref[...]).astype(BF16)
        n_ref[...] = n
        z = _nt(n, w_ref[...])
        c, s = c_ref[...], s_ref[...]
        for j in range((ATTN_WIDTH + KV_WIDTH) // 128):
            qkv_ref[:, j * 128:(j + 1) * 128] = _rope(z[:, j * 128:(j + 1) * 128], c, s, 1.0).astype(BF16)
        qkv_ref[:, ATTN_WIDTH + KV_WIDTH:QKV_WIDTH] = z[:, ATTN_WIDTH + KV_WIDTH:QKV_WIDTH].astype(BF16)
        for j in range(N_CHUNK):
            u_ref[j] = z[:, QKV_WIDTH + j * U_CHUNK:QKV_WIDTH + (j + 1) * U_CHUNK]
        gates_ref[...] = z[:, QKV_WIDTH + SSM_WIDTH:].astype(BF16)

    return pl.pallas_call(
        body, name=name, grid=(t // tm,),
        in_specs=[_row_spec(tm, d), _acc_spec((1, d)), _VMEM, _row_spec(tm, 128), _row_spec(tm, 128)],
        out_specs=[_row_spec(tm, d), _row_spec(tm, QKV_WIDTH), _chunked_spec(tm, lambda i: i), _row_spec(tm, 2 * d)],
        out_shape=[jax.ShapeDtypeStruct((t, d), BF16), jax.ShapeDtypeStruct((t, QKV_WIDTH), BF16),
                   jax.ShapeDtypeStruct((N_CHUNK, t, U_CHUNK), F32), jax.ShapeDtypeStruct((t, 2 * d), BF16)],
        compiler_params=_params(("arbitrary",)),
    )(h, g, win_t, cos_t, sin_t)


def win_bwd(dh, h, g, dqkv, du, dgates, win_t, dep, name):
    t, d = h.shape
    tm = TOKEN_TILE

    def body(dh_ref, h_ref, g_ref, dqkv_ref, du_ref, dgt_ref, w_ref, dep_ref, dhi_ref, dg_ref):
        dn = (_nn(dqkv_ref[...], w_ref[0:QKV_WIDTH, :])
              + _nn(du_ref[...], w_ref[QKV_WIDTH:QKV_WIDTH + SSM_WIDTH, :])
              + _nn(dgt_ref[...], w_ref[QKV_WIDTH + SSM_WIDTH:, :]))
        xh, r = _rms_stats(h_ref[...])
        dx, dg = _rms_bwd(dn, xh, r, g_ref[...])
        dhi_ref[...] = dh_ref[...] + dx

        @pl.when(pl.program_id(0) == 0)
        def _():
            dg_ref[...] = jnp.zeros_like(dg_ref)

        dg_ref[...] += dg

    return pl.pallas_call(
        body, name=name, grid=(t // tm,),
        in_specs=[_row_spec(tm, d), _row_spec(tm, d), _acc_spec((1, d)), _row_spec(tm, QKV_WIDTH),
                  _row_spec(tm, SSM_WIDTH), _row_spec(tm, 2 * d), _VMEM, _ANY],
        out_specs=[_row_spec(tm, d), _acc_spec((1, d))],
        out_shape=[jax.ShapeDtypeStruct((t, d), F32), jax.ShapeDtypeStruct((1, d), F32)],
        compiler_params=_params(("arbitrary",)),
    )(dh, h, g, dqkv, du, dgates, win_t, dep)


def _attn_mask(blk):
    q_pos = blk * BLK + lax.broadcasted_iota(jnp.int32, (BLK, 3 * BLK), 0) - PAD_FRONT
    col = lax.broadcasted_iota(jnp.int32, (BLK, 3 * BLK), 1)
    part = col // BLK
    k_pos = jnp.where(part == 0, col, (blk + part - 2) * BLK + (col - part * BLK)) - PAD_FRONT
    dist = q_pos - k_pos
    meta_ok = (part == 0) & (k_pos >= 0) & (dist >= 0)
    band_ok = (part > 0) & (k_pos >= N_META) & (dist >= 0) & (dist < WINDOW)
    return meta_ok | band_ok


def _head_halves(x128, kv):
    x = x128.astype(F32)
    lane = lax.broadcasted_iota(jnp.int32, x.shape, 1)
    swapped = pltpu.roll(x, HEAD_DIM, 1)
    lo, hi = (x, swapped) if kv == 0 else (swapped, x)
    return jnp.where(lane < HEAD_DIM, lo, 0.0).astype(BF16), jnp.where(lane >= HEAD_DIM, hi, 0.0).astype(BF16)


def _gather_keys(meta_ref, prev_ref, cur_ref, lo):
    return jnp.concatenate([meta_ref[:, lo:lo + 128], prev_ref[:, lo:lo + 128], cur_ref[:, lo:lo + 128]], axis=0)


def _pair_lanes(kv):
    return slice(2 * kv * 128, (2 * kv + 1) * 128), slice((2 * kv + 1) * 128, (2 * kv + 2) * 128)


def _stacked_sinks(sink_ref, head):
    row = lax.broadcasted_iota(jnp.int32, (2 * BLK, 1), 0)
    return jnp.where(row < BLK, sink_ref[0, head], sink_ref[0, head + 2])


def _softmax_with_sink(s, mask, sink):
    s = jnp.where(mask, s * (HEAD_DIM ** -0.5), NEG_INF)
    m = jnp.maximum(jnp.max(s, axis=-1, keepdims=True), sink)
    p = jnp.exp(s - m)
    p_sink = jnp.exp(sink - m)
    inv = 1.0 / (jnp.sum(p, axis=-1, keepdims=True) + p_sink)
    return p * inv, p_sink * inv


def attn_fwd(qkv, sinks, name):
    t = qkv.shape[0]
    nb = t // BLK

    def body(sink_ref, meta_ref, prev_ref, cur_ref, o_ref):
        blk = pl.program_id(0)
        mask = _attn_mask(blk)
        mask2 = jnp.concatenate([mask, mask], axis=0)
        k128 = _gather_keys(meta_ref, prev_ref, cur_ref, ATTN_WIDTH)
        v128 = _gather_keys(meta_ref, prev_ref, cur_ref, ATTN_WIDTH + KV_WIDTH)
        for kv in range(2):
            k_lo, k_hi = _head_halves(k128, kv)
            v_lo, v_hi = _head_halves(v128, kv)
            lanes0, lanes1 = _pair_lanes(kv)
            q2 = jnp.concatenate([cur_ref[:, lanes0], cur_ref[:, lanes1]], axis=0)
            p_a, _ = _softmax_with_sink(_nt(q2, k_lo), mask2, _stacked_sinks(sink_ref, 4 * kv))
            p_b, _ = _softmax_with_sink(_nt(q2, k_hi), mask2, _stacked_sinks(sink_ref, 4 * kv + 1))
            o2 = (_nn(p_a.astype(BF16), v_lo) + _nn(p_b.astype(BF16), v_hi)).astype(BF16)
            o_ref[:, lanes0] = o2[0:BLK]
            o_ref[:, lanes1] = o2[BLK:2 * BLK]

    blk_spec = lambda f: pl.BlockSpec((BLK, QKV_WIDTH), f)
    return pl.pallas_call(
        body, name=name, grid=(nb,),
        in_specs=[_SMEM, blk_spec(lambda i: (0, 0)), blk_spec(lambda i: (jnp.maximum(i - 1, 0), 0)),
                  blk_spec(lambda i: (i, 0))],
        out_specs=_row_spec(BLK, ATTN_WIDTH),
        out_shape=jax.ShapeDtypeStruct((t, ATTN_WIDTH), BF16),
        compiler_params=_params(("arbitrary",)),
    )(sinks, qkv, qkv, qkv)


def attn_bwd(qkv, do, sinks, cos_t, sin_t, name):
    t = qkv.shape[0]
    nb = t // BLK

    def body(sink_ref, meta_ref, prev_ref, cur_ref, do_ref, c_ref, s_ref, dqkv_ref, dsink_ref, carry_ref, macc_ref):
        step = pl.program_id(0)
        blk = nb - 1 - step

        @pl.when(step == 0)
        def _():
            dsink_ref[...] = jnp.zeros_like(dsink_ref)
            carry_ref[...] = jnp.zeros_like(carry_ref)
            macc_ref[...] = jnp.zeros_like(macc_ref)

        mask = _attn_mask(blk)
        mask2 = jnp.concatenate([mask, mask], axis=0)
        lane = lax.broadcasted_iota(jnp.int32, (3 * BLK, 128), 1)
        k128 = _gather_keys(meta_ref, prev_ref, cur_ref, ATTN_WIDTH)
        v128 = _gather_keys(meta_ref, prev_ref, cur_ref, ATTN_WIDTH + KV_WIDTH)
        cos_b, sin_b = c_ref[...], s_ref[...]
        dk_heads, dv_heads = [], []
        for kv in range(2):
            k_lo, k_hi = _head_halves(k128, kv)
            v_lo, v_hi = _head_halves(v128, kv)
            lanes0, lanes1 = _pair_lanes(kv)
            q2 = jnp.concatenate([cur_ref[:, lanes0], cur_ref[:, lanes1]], axis=0)
            do2 = jnp.concatenate([do_ref[:, lanes0], do_ref[:, lanes1]], axis=0)
            ds_half, p_half = [], []
            for half, (k_h, v_h) in enumerate(((k_lo, v_lo), (k_hi, v_hi))):
                head = 4 * kv + half
                p, p_sink = _softmax_with_sink(_nt(q2, k_h), mask2, _stacked_sinks(sink_ref, head))
                dp = _nt(do2, v_h)
                dsum = jnp.sum(p * dp, axis=-1, keepdims=True)
                ds_half.append((p * (dp - dsum) * (HEAD_DIM ** -0.5)).astype(BF16))
                p_half.append(p.astype(BF16))
                dsink = p_sink * dsum
                for part, h in ((0, head), (1, head + 2)):
                    total = -jnp.sum(dsink[part * BLK:(part + 1) * BLK], axis=0, keepdims=True)
                    dsink_ref[h:h + 1, :] += jnp.broadcast_to(total, (1, 128))
            dq2 = _nn(ds_half[0], k_lo) + _nn(ds_half[1], k_hi)
            dqkv_ref[:, lanes0] = _rope(dq2[0:BLK], cos_b, sin_b, -1.0).astype(BF16)
            dqkv_ref[:, lanes1] = _rope(dq2[BLK:2 * BLK], cos_b, sin_b, -1.0).astype(BF16)
            dk_acc = jnp.where(lane < HEAD_DIM, _tn(ds_half[0], q2), _tn(ds_half[1], q2))
            dv_acc = jnp.where(lane < HEAD_DIM, _tn(p_half[0], do2), _tn(p_half[1], do2))
            dk_heads.append(dk_acc + pltpu.roll(dk_acc, HEAD_DIM, 1))
            dv_heads.append(dv_acc + pltpu.roll(dv_acc, HEAD_DIM, 1))
        dkv = jnp.concatenate([jnp.where(lane < HEAD_DIM, dk_heads[0], dk_heads[1]),
                               jnp.where(lane < HEAD_DIM, dv_heads[0], dv_heads[1])], axis=1)
        macc_ref[...] += dkv[0:BLK]
        is_last = (blk == 0).astype(F32)
        mine = dkv[2 * BLK:3 * BLK] + carry_ref[...] + is_last * macc_ref[...]
        carry_ref[...] = dkv[BLK:2 * BLK]
        dqkv_ref[:, ATTN_WIDTH:ATTN_WIDTH + KV_WIDTH] = _rope(mine[:, 0:128], cos_b, sin_b, -1.0).astype(BF16)
        dqkv_ref[:, ATTN_WIDTH + KV_WIDTH:QKV_WIDTH] = mine[:, 128:256].astype(BF16)

    rev = lambda i: nb - 1 - i
    blk_spec = lambda f: pl.BlockSpec((BLK, QKV_WIDTH), f)
    return pl.pallas_call(
        body, name=name, grid=(nb,),
        in_specs=[_SMEM, blk_spec(lambda i: (0, 0)), blk_spec(lambda i: (jnp.maximum(rev(i) - 1, 0), 0)),
                  blk_spec(lambda i: (rev(i), 0)), pl.BlockSpec((BLK, ATTN_WIDTH), lambda i: (rev(i), 0)),
                  pl.BlockSpec((BLK, 128), lambda i: (rev(i), 0)), pl.BlockSpec((BLK, 128), lambda i: (rev(i), 0))],
        out_specs=[pl.BlockSpec((BLK, QKV_WIDTH), lambda i: (rev(i), 0)), _acc_spec((N_Q_HEADS, 128))],
        out_shape=[jax.ShapeDtypeStruct((t, QKV_WIDTH), BF16), jax.ShapeDtypeStruct((N_Q_HEADS, 128), F32)],
        scratch_shapes=[pltpu.VMEM((BLK, 256), F32), pltpu.VMEM((BLK, 256), F32)],
        compiler_params=_params(("arbitrary",)),
    )(sinks, qkv, qkv, qkv, do, cos_t, sin_t)


def _cmul(ar, ai, br, bi):
    return ar * br - ai * bi, ar * bi + ai * br


def ssm_prep(a_re, a_im, log_dt, b_re_t, b_im_t, name):
    def body(ar_ref, ai_ref, ldt_ref, br_ref, bi_ref, lr_ref, li_ref, bbr_ref, bbi_ref):
        ar, ai = ar_ref[...], ai_ref[...]
        dt = jnp.exp(ldt_ref[...])
        mag = jnp.exp(ar * dt)
        lr = mag * jnp.cos(ai * dt)
        li = mag * jnp.sin(ai * dt)
        den = ar * ar + ai * ai
        nr = lr - 1.0
        cr = ((nr * ar + li * ai) / den)[:, None, :]
        ci = ((li * ar - nr * ai) / den)[:, None, :]
        br, bi = br_ref[...], bi_ref[...]
        lr_ref[...] = lr
        li_ref[...] = li
        bbr_ref[...] = cr * br - ci * bi
        bbi_ref[...] = cr * bi + ci * br

    gp = jax.ShapeDtypeStruct(a_re.shape, F32)
    gcp = jax.ShapeDtypeStruct(b_re_t.shape, F32)
    return pl.pallas_call(body, name=name, out_shape=[gp, gp, gcp, gcp],
                          in_specs=[_VMEM] * 5, out_specs=[_VMEM] * 4)(a_re, a_im, log_dt, b_re_t, b_im_t)


def ssm_prep_bwd(a_re, a_im, log_dt, b_re_t, b_im_t, dl_re, dl_im, dbb_re, dbb_im, name):
    def body(ar_ref, ai_ref, ldt_ref, br_ref, bi_ref, dlr_ref, dli_ref, dbbr_ref, dbbi_ref,
             dar_ref, dai_ref, dldt_ref, dbr_ref, dbi_ref):
        ar, ai = ar_ref[...], ai_ref[...]
        dt = jnp.exp(ldt_ref[...])
        mag = jnp.exp(ar * dt)
        lr = mag * jnp.cos(ai * dt)
        li = mag * jnp.sin(ai * dt)
        den = ar * ar + ai * ai
        nr = lr - 1.0
        cr = (nr * ar + li * ai) / den
        ci = (li * ar - nr * ai) / den
        br, bi = br_ref[...], bi_ref[...]
        dbbr, dbbi = dbbr_ref[...], dbbi_ref[...]
        dbr_ref[...] = cr[:, None, :] * dbbr + ci[:, None, :] * dbbi
        dbi_ref[...] = cr[:, None, :] * dbbi - ci[:, None, :] * dbbr
        dcr = jnp.sum(br * dbbr + bi * dbbi, axis=1)
        dci = jnp.sum(br * dbbi - bi * dbbr, axis=1)
        d_num_r = dcr / den
        d_num_i = dci / den
        d_den = -(dcr * cr + dci * ci) / den
        d_lr = dlr_ref[...] + d_num_r * ar - d_num_i * ai
        d_li = dli_ref[...] + d_num_r * ai + d_num_i * ar
        d_ar = d_num_r * nr + d_num_i * li + d_den * 2.0 * ar
        d_ai = d_num_r * li - d_num_i * nr + d_den * 2.0 * ai
        d_mag = (d_lr * lr + d_li * li) / mag
        d_theta = d_li * lr - d_lr * li
        d_ardt = d_mag * mag
        dar_ref[...] = d_ar + d_ardt * dt
        dai_ref[...] = d_ai + d_theta * dt
        d_dt = jnp.sum(d_ardt * ar + d_theta * ai, axis=1, keepdims=True)
        dldt_ref[...] = d_dt * dt

    gp = jax.ShapeDtypeStruct(a_re.shape, F32)
    gcp = jax.ShapeDtypeStruct(b_re_t.shape, F32)
    return pl.pallas_call(body, name=name, out_shape=[gp, gp, jax.ShapeDtypeStruct(log_dt.shape, F32), gcp, gcp],
                          in_specs=[_VMEM] * 9, out_specs=[_VMEM] * 5,
                          )(a_re, a_im, log_dt, b_re_t, b_im_t, dl_re, dl_im, dbb_re, dbb_im)


N_CHUNK = 4
U_CHUNK = SSM_WIDTH // N_CHUNK
H_CHUNK = STATE_WIDTH // N_CHUNK
SUB = 8


def _block_diag_b(bb):
    x = bb.reshape(N_CHUNK, 8, SSM_GROUP, 1, SSM_STATE)
    same = (jnp.arange(8)[:, None] == jnp.arange(8)[None, :])[None, :, None, :, None]
    return jnp.where(same, x, 0.0).reshape(N_CHUNK, U_CHUNK, H_CHUNK)


def _block_diag_c(c):
    x = jnp.swapaxes(c.reshape(N_CHUNK, 8, SSM_GROUP, SSM_STATE), 2, 3)[:, :, :, None, :]
    same = (jnp.arange(8)[:, None] == jnp.arange(8)[None, :])[None, :, None, :, None]
    return jnp.where(same, x, 0.0).reshape(N_CHUNK, H_CHUNK, U_CHUNK)


def _diag_of_b(m):
    x = m.reshape(N_CHUNK, 8, SSM_GROUP, 8, SSM_STATE)
    return jnp.stack([x[:, g, :, g, :] for g in range(8)], axis=1).reshape(SSM_GROUPS, SSM_GROUP, SSM_STATE)


def _diag_of_c(m):
    x = m.reshape(N_CHUNK, 8, SSM_STATE, 8, SSM_GROUP)
    d = jnp.stack([x[:, g, :, g, :] for g in range(8)], axis=1)
    return jnp.swapaxes(d, 2, 3).reshape(SSM_GROUPS, SSM_GROUP, SSM_STATE)


def _lambda_tables(lr, li, reverse):
    p1 = (lr, li)
    p2 = _cmul(*p1, *p1)
    p4 = _cmul(*p2, *p2)
    rows = [p1]
    for _ in range(SUB - 1):
        rows.append(_cmul(*rows[-1], *p1))
    if reverse:
        rows = rows[::-1]
    return p1, p2, p4, (jnp.concatenate([r[0] for r in rows], axis=0), jnp.concatenate([r[1] for r in rows], axis=0))


def _scan8(xr, xi, pows, table, cr, ci, reverse):
    row = lax.broadcasted_iota(jnp.int32, xr.shape, 0)
    for d, (pr, pi) in zip((1, 2, 4), pows):
        if reverse:
            sr, si = pltpu.roll(xr, SUB - d, 0), pltpu.roll(xi, SUB - d, 0)
            keep = row < SUB - d
        else:
            sr, si = pltpu.roll(xr, d, 0), pltpu.roll(xi, d, 0)
            keep = row >= d
        sr = jnp.where(keep, sr, 0.0)
        si = jnp.where(keep, si, 0.0)
        xr, xi = xr + pr * sr - pi * si, xi + pr * si + pi * sr
    tr, ti = table
    return xr + tr * cr - ti * ci, xi + tr * ci + ti * cr


def _gelu_and_grad(y):
    k0 = math.sqrt(2.0 / math.pi)
    inner = k0 * (y + 0.044715 * y * y * y)
    th = jnp.tanh(inner)
    g = 0.5 * y * (1.0 + th)
    dg = 0.5 * (1.0 + th) + 0.5 * y * (1.0 - th * th) * k0 * (1.0 + 3.0 * 0.044715 * y * y)
    return g, dg


SCAN_TILE = TOKEN_TILE
SEG = SCAN_TILE // SUB
SCAN_LANES = 512


def _perm_matrix(to_segments):
    a = lax.broadcasted_iota(jnp.int32, (SCAN_TILE, SCAN_TILE), 0)
    b = lax.broadcasted_iota(jnp.int32, (SCAN_TILE, SCAN_TILE), 1)
    rho, time = (a, b) if to_segments else (b, a)
    return (time == (rho % SUB) * SEG + rho // SUB).astype(BF16)


def _chunked_spec(rows, block_of):
    return pl.BlockSpec((N_CHUNK, rows, U_CHUNK), lambda i: (0, block_of(i), 0))


def _load_segments(src_ref, dst_ref):
    for j in range(N_CHUNK):
        for r in range(SEG):
            dst_ref[r * SUB:(r + 1) * SUB, j * U_CHUNK:(j + 1) * U_CHUNK] = src_ref.at[j][pl.ds(r, SUB, stride=SEG), :]


def _power_table(lr, li, pr_ref, pi_ref):
    cur = (lr, li)
    for r in range(SEG):
        pr_ref[r * SUB:(r + 1) * SUB, :] = jnp.broadcast_to(cur[0], (SUB, STATE_WIDTH))
        pi_ref[r * SUB:(r + 1) * SUB, :] = jnp.broadcast_to(cur[1], (SUB, STATE_WIDTH))
        cur = _cmul(*cur, lr, li)


def _table_rows(ref, k, lanes):
    return ref[pl.ds(pl.multiple_of(k * SUB, SUB), SUB), lanes]


def _segment_scan(xr_ref, xi_ref, lanes, lam, table_row, cr_ref, ci_ref, reverse, extra=None):
    lr = jnp.broadcast_to(lam[0], (SUB, SCAN_LANES))
    li = jnp.broadcast_to(lam[1], (SUB, SCAN_LANES))
    row = lax.broadcasted_iota(jnp.int32, (SUB, SCAN_LANES), 0)

    def rows_of(k):
        r = SEG - 1 - k if reverse else k
        return pl.ds(pl.multiple_of(r * SUB, SUB), SUB)

    def first(k, st):
        sr, si = st
        rows = rows_of(k)
        nr = lr * sr - li * si + xr_ref[rows, lanes]
        ni = lr * si + li * sr + xi_ref[rows, lanes]
        xr_ref[rows, lanes] = nr
        xi_ref[rows, lanes] = ni
        return nr, ni

    zero = jnp.zeros((SUB, SCAN_LANES), F32)
    er, ei = lax.fori_loop(0, SEG, first, (zero, zero))
    l16 = table_row(SEG - 1)
    q1, q2, q4, tab = _lambda_tables(l16[0][0:1], l16[1][0:1], reverse)
    c_r, c_i = cr_ref[:, lanes], ci_ref[:, lanes]
    gr, gi = _scan8(er, ei, (q1, q2, q4), tab, c_r, c_i, reverse)
    if reverse:
        cin_r = jnp.where(row == SUB - 1, c_r, pltpu.roll(gr, SUB - 1, 0))
        cin_i = jnp.where(row == SUB - 1, c_i, pltpu.roll(gi, SUB - 1, 0))
        cr_ref[:, lanes] = gr[0:1]
        ci_ref[:, lanes] = gi[0:1]
    else:
        cin_r = jnp.where(row == 0, c_r, pltpu.roll(gr, 1, 0))
        cin_i = jnp.where(row == 0, c_i, pltpu.roll(gi, 1, 0))
        cr_ref[:, lanes] = gr[SUB - 1:SUB]
        ci_ref[:, lanes] = gi[SUB - 1:SUB]

    def second(k, carry):
        rows = rows_of(k)
        tr, ti = table_row(k)
        ar = xr_ref[rows, lanes] + tr * cin_r - ti * cin_i
        ai = xi_ref[rows, lanes] + tr * cin_i + ti * cin_r
        xr_ref[rows, lanes] = ar
        xi_ref[rows, lanes] = ai
        if extra is None:
            return carry
        return extra(rows, carry, ar, ai)

    init = 0 if extra is None else (cin_r, cin_i, zero, zero)
    return lax.fori_loop(0, SEG, second, init)


def ssm_fwd(u, lam_re, lam_im, bb_re, bb_im, cc_re, cc_im, d_skip, name):
    t = u.shape[1]
    tt = SCAN_TILE

    def body(u_ref, lr_ref, li_ref, bbr_ref, bbi_ref, ccr_ref, cci_ref, d_ref, yg_ref, hr_ref, hi_ref,
             cr_ref, ci_ref, pr_ref, pi_ref, up_ref, y_ref):
        @pl.when(pl.program_id(0) == 0)
        def _():
            cr_ref[...] = jnp.zeros_like(cr_ref)
            ci_ref[...] = jnp.zeros_like(ci_ref)
            _power_table(lr_ref[...], li_ref[...], pr_ref, pi_ref)

        _load_segments(u_ref, up_ref)
        ub = up_ref[...].astype(BF16)
        for j in range(N_CHUNK):
            hs = slice(j * H_CHUNK, (j + 1) * H_CHUNK)
            us = slice(j * U_CHUNK, (j + 1) * U_CHUNK)
            hr_ref[:, hs] = _nn(ub[:, us], bbr_ref[j])
            hi_ref[:, hs] = _nn(ub[:, us], bbi_ref[j])
        for c in range(STATE_WIDTH // SCAN_LANES):
            lanes = slice(c * SCAN_LANES, (c + 1) * SCAN_LANES)
            _segment_scan(hr_ref, hi_ref, lanes, (lr_ref[:, lanes], li_ref[:, lanes]),
                          lambda k, lanes=lanes: (_table_rows(pr_ref, k, lanes), _table_rows(pi_ref, k, lanes)),
                          cr_ref, ci_ref, False)
        for j in range(N_CHUNK):
            hs = slice(j * H_CHUNK, (j + 1) * H_CHUNK)
            us = slice(j * U_CHUNK, (j + 1) * U_CHUNK)
            y = (_nn(hr_ref[:, hs].astype(BF16), ccr_ref[j]) - _nn(hi_ref[:, hs].astype(BF16), cci_ref[j])
                 + d_ref[:, us] * up_ref[:, us])
            y_ref[:, us] = _gelu_and_grad(y)[0]
        yg_ref[...] = _nn(_perm_matrix(False), y_ref[...].astype(BF16)).astype(BF16)

    return pl.pallas_call(
        body, name=name, grid=(t // tt,),
        in_specs=[_chunked_spec(tt, lambda i: i), _VMEM, _VMEM, _VMEM, _VMEM, _VMEM, _VMEM, _VMEM],
        out_specs=[_row_spec(tt, SSM_WIDTH), _row_spec(tt, STATE_WIDTH), _row_spec(tt, STATE_WIDTH)],
        out_shape=[jax.ShapeDtypeStruct((t, SSM_WIDTH), BF16), jax.ShapeDtypeStruct((t, STATE_WIDTH), F32),
                   jax.ShapeDtypeStruct((t, STATE_WIDTH), F32)],
        scratch_shapes=[pltpu.VMEM((1, STATE_WIDTH), F32), pltpu.VMEM((1, STATE_WIDTH), F32),
                        pltpu.VMEM((SCAN_TILE, STATE_WIDTH), F32), pltpu.VMEM((SCAN_TILE, STATE_WIDTH), F32),
                        pltpu.VMEM((tt, SSM_WIDTH), F32), pltpu.VMEM((tt, SSM_WIDTH), F32)],
        compiler_params=_params(("arbitrary",)),
    )(u, lam_re, lam_im, bb_re, bb_im, cc_re, cc_im, d_skip)


def ssm_bwd(dyg, u, h_re, h_im, lam_re, lam_im, bb_re, bb_im, cc_re, cc_im, d_skip, name):
    t = u.shape[1]
    tt = SCAN_TILE
    nt = t // tt

    def body(dyg_ref, u_ref, hr_ref, hi_ref, lr_ref, li_ref, bbr_ref, bbi_ref, ccr_ref, cci_ref, d_ref,
             du_ref, dlr_ref, dli_ref, dbbr_ref, dbbi_ref, dccr_ref, dcci_ref, dd_ref,
             ar_ref, ai_ref, cr_ref, ci_ref, pr_ref, pi_ref, up_ref, dy_ref, dup_ref):
        step = pl.program_id(0)
        tile = nt - 1 - step

        @pl.when(step == 0)
        def _():
            for ref in (cr_ref, ci_ref, dlr_ref, dli_ref, dbbr_ref, dbbi_ref, dccr_ref, dcci_ref, dd_ref):
                ref[...] = jnp.zeros_like(ref)
            _power_table(lr_ref[...], li_ref[...], pr_ref, pi_ref)

        _load_segments(u_ref, up_ref)
        _load_segments(dyg_ref, dy_ref)
        uv = up_ref[...]
        ub = uv.astype(BF16)
        dskip = d_ref[...]
        for j in range(N_CHUNK):
            hs = slice(j * H_CHUNK, (j + 1) * H_CHUNK)
            us = slice(j * U_CHUNK, (j + 1) * U_CHUNK)
            hrb = hr_ref[:, hs].astype(BF16)
            hib = hi_ref[:, hs].astype(BF16)
            y = _nn(hrb, ccr_ref[j]) - _nn(hib, cci_ref[j]) + dskip[:, us] * uv[:, us]
            dy = dy_ref[:, us] * _gelu_and_grad(y)[1]
            dy_ref[:, us] = dy
            dyb = dy.astype(BF16)
            dccr_ref[j] += _tn(hrb, dyb)
            dcci_ref[j] -= _tn(hib, dyb)
            ar_ref[:, hs] = _nt(dyb, ccr_ref[j])
            ai_ref[:, hs] = -_nt(dyb, cci_ref[j])
        dd_ref[...] += jnp.sum(dy_ref[...] * uv, axis=0, keepdims=True)

        for c in range(STATE_WIDTH // SCAN_LANES):
            lanes = slice(c * SCAN_LANES, (c + 1) * SCAN_LANES)

            def dlambda(rows, carry, ar, ai, lanes=lanes):
                nr, ni, accr, acci = carry
                hr, hi = hr_ref[rows, lanes], hi_ref[rows, lanes]
                return ar, ai, accr + nr * hr + ni * hi, acci + ni * hr - nr * hi

            _, _, accr, acci = _segment_scan(
                ar_ref, ai_ref, lanes, (lr_ref[:, lanes], -li_ref[:, lanes]),
                lambda k, lanes=lanes: (_table_rows(pr_ref, k, lanes), -_table_rows(pi_ref, k, lanes)),
                cr_ref, ci_ref, True, dlambda)
            dlr_ref[:, lanes] += accr
            dli_ref[:, lanes] += acci

        rho = lax.broadcasted_iota(jnp.int32, (tt, U_CHUNK), 0)
        time = tile * tt + (rho % SUB) * SEG + rho // SUB
        for j in range(N_CHUNK):
            hs = slice(j * H_CHUNK, (j + 1) * H_CHUNK)
            us = slice(j * U_CHUNK, (j + 1) * U_CHUNK)
            arb = ar_ref[:, hs].astype(BF16)
            aib = ai_ref[:, hs].astype(BF16)
            dbbr_ref[j] += _tn(ub[:, us], arb)
            dbbi_ref[j] += _tn(ub[:, us], aib)
            du = _nt(arb, bbr_ref[j]) + _nt(aib, bbi_ref[j]) + dy_ref[:, us] * dskip[:, us]
            dup_ref[:, us] = jnp.where(time >= PAD_FRONT, du, 0.0)
        du_ref[...] = _nn(_perm_matrix(False), dup_ref[...].astype(BF16)).astype(BF16)

    rev = lambda i: (nt - 1 - i, 0)
    full = lambda shape: pl.BlockSpec(shape, lambda i: (0,) * len(shape))
    return pl.pallas_call(
        body, name=name, grid=(nt,),
        in_specs=[_chunked_spec(tt, lambda i: nt - 1 - i), _chunked_spec(tt, lambda i: nt - 1 - i),
                  pl.BlockSpec((tt, STATE_WIDTH), rev), pl.BlockSpec((tt, STATE_WIDTH), rev),
                  _VMEM, _VMEM, _VMEM, _VMEM, _VMEM, _VMEM, _VMEM],
        out_specs=[pl.BlockSpec((tt, SSM_WIDTH), rev), full((SUB, STATE_WIDTH)), full((SUB, STATE_WIDTH)),
                   full((N_CHUNK, U_CHUNK, H_CHUNK)), full((N_CHUNK, U_CHUNK, H_CHUNK)),
                   full((N_CHUNK, H_CHUNK, U_CHUNK)), full((N_CHUNK, H_CHUNK, U_CHUNK)), full((1, SSM_WIDTH))],
        out_shape=[jax.ShapeDtypeStruct((t, SSM_WIDTH), BF16),
                   jax.ShapeDtypeStruct((SUB, STATE_WIDTH), F32), jax.ShapeDtypeStruct((SUB, STATE_WIDTH), F32),
                   jax.ShapeDtypeStruct((N_CHUNK, U_CHUNK, H_CHUNK), F32),
                   jax.ShapeDtypeStruct((N_CHUNK, U_CHUNK, H_CHUNK), F32),
                   jax.ShapeDtypeStruct((N_CHUNK, H_CHUNK, U_CHUNK), F32),
                   jax.ShapeDtypeStruct((N_CHUNK, H_CHUNK, U_CHUNK), F32),
                   jax.ShapeDtypeStruct((1, SSM_WIDTH), F32)],
        scratch_shapes=[pltpu.VMEM((tt, STATE_WIDTH), F32), pltpu.VMEM((tt, STATE_WIDTH), F32),
                        pltpu.VMEM((1, STATE_WIDTH), F32), pltpu.VMEM((1, STATE_WIDTH), F32),
                        pltpu.VMEM((SCAN_TILE, STATE_WIDTH), F32), pltpu.VMEM((SCAN_TILE, STATE_WIDTH), F32),
                        pltpu.VMEM((tt, SSM_WIDTH), F32), pltpu.VMEM((tt, SSM_WIDTH), F32),
                        pltpu.VMEM((tt, SSM_WIDTH), F32)],
        compiler_params=_params(("arbitrary",)),
    )(dyg, u, h_re, h_im, lam_re, lam_im, bb_re, bb_im, cc_re, cc_im, d_skip)


def merge_fwd(h, o, yg, gates, wap_t, wv_t, wgg_t, wout, name):
    t, d = h.shape
    tm = TOKEN_TILE

    def body(h_ref, o_ref, yg_ref, gt_ref, wap_ref, wv_ref, wgg_ref, wout_ref, ho_ref, mg_ref, a_ref, sv_ref, sg_ref):
        att = _nt(o_ref[...], wap_ref[...])
        ygv = yg_ref[...]
        sv = _nt(ygv, wv_ref[...])
        sg = _nt(ygv, wgg_ref[...])
        a_ref[...] = att.astype(BF16)
        sv_ref[...] = sv.astype(BF16)
        sg_ref[...] = sg.astype(BF16)
        merged = (jax.nn.sigmoid(gt_ref[:, 0:d].astype(F32)) * att
                  + jax.nn.sigmoid(gt_ref[:, d:2 * d].astype(F32)) * (sv * jax.nn.sigmoid(sg))).astype(BF16)
        mg_ref[...] = merged
        ho_ref[...] = h_ref[...] + _nn(merged, wout_ref[...])

    return pl.pallas_call(
        body, name=name, grid=(t // tm,),
        in_specs=[_row_spec(tm, d), _row_spec(tm, ATTN_WIDTH), _row_spec(tm, SSM_WIDTH), _row_spec(tm, 2 * d),
                  _VMEM, _VMEM, _VMEM, _VMEM],
        out_specs=[_row_spec(tm, d), _row_spec(tm, d), _row_spec(tm, d), _row_spec(tm, d), _row_spec(tm, d)],
        out_shape=[jax.ShapeDtypeStruct((t, d), F32), jax.ShapeDtypeStruct((t, d), BF16),
                   jax.ShapeDtypeStruct((t, d), BF16), jax.ShapeDtypeStruct((t, d), BF16),
                   jax.ShapeDtypeStruct((t, d), BF16)],
        compiler_params=_params(("arbitrary",)),
    )(h, o, yg, gates, wap_t, wv_t, wgg_t, wout)


def merge_bwd(dh, gates, att, sv, sg, wap_t, wv_t, wgg_t, wout, dep, name):
    t, d = dh.shape
    tm = TOKEN_TILE

    def body(dh_ref, gt_ref, a_ref, sv_ref, sg_ref, wap_ref, wv_ref, wgg_ref, wout_ref, dep_ref,
             dgt_ref, da_ref, dsv_ref, dsg_ref, do_ref, dyg_ref, dhb_ref):
        dhb = dh_ref[...].astype(BF16)
        dhb_ref[...] = dhb
        dm = _nt(dhb, wout_ref[...])
        sig_a = jax.nn.sigmoid(gt_ref[:, 0:d].astype(F32))
        sig_s = jax.nn.sigmoid(gt_ref[:, d:2 * d].astype(F32))
        sig_g = jax.nn.sigmoid(sg_ref[...].astype(F32))
        svv = sv_ref[...].astype(F32)
        dgt_ref[:, 0:d] = (dm * a_ref[...].astype(F32) * sig_a * (1.0 - sig_a)).astype(BF16)
        dgt_ref[:, d:2 * d] = (dm * (svv * sig_g) * sig_s * (1.0 - sig_s)).astype(BF16)
        da = (dm * sig_a).astype(BF16)
        d_s = dm * sig_s
        dsv = (d_s * sig_g).astype(BF16)
        dsg = (d_s * svv * sig_g * (1.0 - sig_g)).astype(BF16)
        da_ref[...] = da
        dsv_ref[...] = dsv
        dsg_ref[...] = dsg
        do_ref[...] = _nn(da, wap_ref[...]).astype(BF16)
        dyg = _nn(dsv, wv_ref[...]) + _nn(dsg, wgg_ref[...])
        for j in range(N_CHUNK):
            dyg_ref[j] = dyg[:, j * U_CHUNK:(j + 1) * U_CHUNK]

    return pl.pallas_call(
        body, name=name, grid=(t // tm,),
        in_specs=[_row_spec(tm, d), _row_spec(tm, 2 * d), _row_spec(tm, d), _row_spec(tm, d), _row_spec(tm, d),
                  _VMEM, _VMEM, _VMEM, _VMEM, _ANY],
        out_specs=[_row_spec(tm, 2 * d), _row_spec(tm, d), _row_spec(tm, d), _row_spec(tm, d),
                   _row_spec(tm, ATTN_WIDTH), _chunked_spec(tm, lambda i: i), _row_spec(tm, d)],
        out_shape=[jax.ShapeDtypeStruct((t, 2 * d), BF16), jax.ShapeDtypeStruct((t, d), BF16),
                   jax.ShapeDtypeStruct((t, d), BF16), jax.ShapeDtypeStruct((t, d), BF16),
                   jax.ShapeDtypeStruct((t, ATTN_WIDTH), BF16), jax.ShapeDtypeStruct((N_CHUNK, t, U_CHUNK), F32),
                   jax.ShapeDtypeStruct((t, d), BF16)],
        compiler_params=_params(("arbitrary",)),
    )(dh, gates, att, sv, sg, wap_t, wv_t, wgg_t, wout, dep)


def _adamw_math(w, g, m, v):
    mn = ADAM_B1 * m + (1.0 - ADAM_B1) * g
    vn = ADAM_B2 * v + (1.0 - ADAM_B2) * (g * g)
    m_hat = mn / (1.0 - ADAM_B1 ** ADAM_STEP)
    v_hat = vn / (1.0 - ADAM_B2 ** ADAM_STEP)
    return -ADAM_LR * (m_hat / (jnp.sqrt(v_hat) + ADAM_EPS) + ADAM_WD * w), mn, vn


def sum_adamw_layer(me, landed, partial, w, m, v, layer, prev, name, transposed=False):
    _, rows, cols = landed.shape
    tr = rows // 2 if rows % 32 == 0 and not transposed else rows
    steps = rows // tr

    def body(me_ref, land_ref, own_ref, w_ref, m_ref, v_ref, *rest):
        go_ref, d_ref, mo_ref, vo_ref = rest[-4:]
        who = me_ref[0]
        gv = land_ref[who ^ 1].astype(F32)
        for p in range(2, N_DEV):
            gv = gv + land_ref[who ^ p].astype(F32)
        gv = gv + own_ref[...].astype(F32)
        if transposed:
            gv = gv.T
        go_ref[0] = gv
        d_ref[0], mo_ref[0], vo_ref[0] = _adamw_math(w_ref[0], gv, m_ref[0], v_ref[0])

    if transposed:
        spec3 = pl.BlockSpec((1, cols, rows), lambda r, me_ref: (layer, 0, 0))
    else:
        spec3 = pl.BlockSpec((1, tr, cols), lambda r, me_ref: (layer, r, 0))
    out = jax.ShapeDtypeStruct(w.shape, F32)
    extra = [] if prev is None else list(prev)
    grid_spec = pltpu.PrefetchScalarGridSpec(
        num_scalar_prefetch=1, grid=(steps,),
        in_specs=[pl.BlockSpec((N_DEV, tr, cols), lambda r, me_ref: (0, r, 0)),
                  pl.BlockSpec((tr, cols), lambda r, me_ref: (me_ref[0] * steps + r, 0)),
                  spec3, spec3, spec3] + [_ANY] * len(extra),
        out_specs=[spec3] * 4)
    return pl.pallas_call(
        body, name=name, grid_spec=grid_spec, out_shape=[out] * 4,
        input_output_aliases={6 + j: j for j in range(len(extra))},
        compiler_params=_params(("arbitrary",)),
    )(me, landed, partial, w, m, v, *extra)


def adamw_small(params, name):
    def as2d(a, swap):
        a = jnp.swapaxes(a, -1, -2) if swap else a
        return a.reshape(-1, a.shape[-1]) if a.ndim >= 2 else a.reshape(1, -1)

    n = len(params)
    flat = [as2d(a, swap) for w, g, m, v, swap in params for a in (w, g, m, v)]

    def body(*refs):
        ins, outs = refs[:4 * n], refs[4 * n:]
        for k in range(n):
            w_ref, g_ref, m_ref, v_ref = ins[4 * k:4 * k + 4]
            outs[3 * k][...], outs[3 * k + 1][...], outs[3 * k + 2][...] = _adamw_math(
                w_ref[...], g_ref[...], m_ref[...], v_ref[...])

    outs = pl.pallas_call(
        body, name=name, in_specs=[_VMEM] * (4 * n), out_specs=[_VMEM] * (3 * n),
        out_shape=[jax.ShapeDtypeStruct(flat[4 * k].shape, F32) for k in range(n) for _ in range(3)],
        compiler_params=_params(),
    )(*flat)

    def restore(a, like, swap):
        shape = jnp.swapaxes(like, -1, -2).shape if swap else like.shape
        a = a.reshape(shape)
        return jnp.swapaxes(a, -1, -2) if swap else a

    return [tuple(restore(outs[3 * k + j], params[k][0], params[k][4]) for j in range(3)) for k in range(n)]


def _my_index():
    return 4 * lax.axis_index("x") + 2 * lax.axis_index("y") + lax.axis_index("c")


def _peer(p):
    return (lax.axis_index("x") ^ ((p >> 2) & 1), lax.axis_index("y") ^ ((p >> 1) & 1), lax.axis_index("c") ^ (p & 1))


_HBM = pl.BlockSpec(memory_space=pltpu.HBM)
_SEM = pl.BlockSpec(memory_space=pltpu.SEMAPHORE)
_EFFECT = pltpu.SideEffectType.DATAFLOW_SIDE_EFFECTING


class Exchange:
    RELAYED = (2, 4, 6)

    def __init__(self, srcs, scatter, name, relay=False):
        self.n = n = len(srcs)
        self.scatter = scatter
        self.name = name
        self.relayed = relay
        assert not (relay and scatter)
        self.direct = (1,) + self.RELAYED if relay else tuple(range(1, N_DEV))
        widths = sorted({s.shape[1] for s in srcs}, reverse=True)
        self.ncls = len(widths)
        self.cls = [widths.index(s.shape[1]) for s in srcs]
        self.cnts = [s.shape[0] // N_DEV if scatter else s.shape[0] for s in srcs]
        self.totals = [sum(c for c, k in zip(self.cnts, self.cls) if k == w) for w in range(self.ncls)]
        self.sizer = [max((k for k in range(n) if self.cls[k] == w), key=lambda k: self.cnts[k])
                      for w in range(self.ncls)]
        assert all(N_DEV * self.cnts[self.sizer[w]] >= self.totals[w] for w in range(self.ncls))
        if scatter:
            self.land_shapes = [(N_DEV, c, s.shape[1]) for s, c in zip(srcs, self.cnts)]
        else:
            self.land_shapes = [(N_DEV * c, s.shape[1]) for s, c in zip(srcs, self.cnts)]
        self.dtypes = [s.dtype for s in srcs]

    def _block(self, k, who):
        return pl.ds(pl.multiple_of(who * self.cnts[k], 16), self.cnts[k])

    def _sem(self, p, w):
        return (p - 1) * self.ncls + w

    def start(self, srcs, after):
        n = self.n

        def body(*refs):
            src, land = refs[:n], refs[n:2 * n]
            send_sems, recv_sems = refs[2 * n + 1], refs[2 * n + 2]
            token = refs[-1]
            me = _my_index()
            for p in self.direct:
                for k in range(n):
                    if self.scatter:
                        s_ref, d_ref = src[k].at[self._block(k, me ^ p), :], land[k].at[me]
                    else:
                        s_ref, d_ref = src[k], land[k].at[self._block(k, me), :]
                    pltpu.make_async_remote_copy(
                        src_ref=s_ref, dst_ref=d_ref, send_sem=send_sems.at[self._sem(p, self.cls[k])],
                        recv_sem=recv_sems.at[self._sem(p, self.cls[k])], device_id=_peer(p),
                        device_id_type=MESH).start()
            token[...] = jnp.zeros_like(token)

        sems = pltpu.SemaphoreType.DMA(((N_DEV - 1) * self.ncls,))
        thru = [pltpu.HBM(s.shape, s.dtype) for s in srcs] + [pltpu.HBM(shp, dt) for shp, dt in
                                                               zip(self.land_shapes, self.dtypes)]
        lands = [pltpu.with_memory_space_constraint(lax.empty(shp, dt), pltpu.HBM)
                 for shp, dt in zip(self.land_shapes, self.dtypes)]
        out = pl.pallas_call(
            body, name=self.name + "_start",
            in_specs=[_HBM] * (2 * n) + [_ANY],
            out_shape=[sems, sems] + thru + [jax.ShapeDtypeStruct((8, 128), F32)],
            out_specs=[_SEM, _SEM] + [_HBM] * (2 * n) + [_VMEM],
            input_output_aliases={j: 2 + j for j in range(2 * n)},
            compiler_params=pltpu.CompilerParams(has_side_effects=_EFFECT),
        )(*[pltpu.with_memory_space_constraint(s, pltpu.HBM) for s in srcs], *lands, after)
        return out[:-1], out[-1]

    def _span_copy(self, src, land, w, send_sem, recv_sem, p):
        big = src[self.sizer[w]] if self.scatter else land[self.sizer[w]]
        span = big.at[pl.ds(0, self.totals[w]), :]
        return pltpu.make_async_remote_copy(src_ref=span, dst_ref=span, send_sem=send_sem, recv_sem=recv_sem,
                                            device_id=_peer(p), device_id_type=MESH)

    def relay(self, state, after):
        n = self.n
        send_sems, recv_sems = state[0], state[1]
        thru = state[2:]
        after = list(after) if isinstance(after, (list, tuple)) else [after]
        first_out = 2 * n + 2 + len(after)

        def body(*refs):
            land = refs[n:2 * n]
            send_a, recv_a = refs[2 * n], refs[2 * n + 1]
            send_b, recv_b = refs[first_out], refs[first_out + 1]
            refs[-1][...] = jnp.zeros_like(refs[-1])
            me = _my_index()
            for p in self.RELAYED:
                for w in range(self.ncls):
                    self._span_copy(None, land, w, send_a.at[self._sem(p, w)], recv_a.at[self._sem(p, w)], p).wait_recv()
            for j, p in enumerate(self.RELAYED):
                for k in range(n):
                    rows = land[k].at[self._block(k, me ^ p), :]
                    pltpu.make_async_remote_copy(
                        src_ref=rows, dst_ref=rows, send_sem=send_b.at[j * self.ncls + self.cls[k]],
                        recv_sem=recv_b.at[j * self.ncls + self.cls[k]], device_id=_peer(1),
                        device_id_type=MESH).start()

        sems = pltpu.SemaphoreType.DMA((len(self.RELAYED) * self.ncls,))
        out = pl.pallas_call(
            body, name=self.name + "_relay",
            in_specs=[_HBM] * (2 * n) + [_SEM, _SEM] + [_ANY] * len(after),
            out_shape=[sems, sems] + [pltpu.HBM(a.shape, a.dtype) for a in thru] + [jax.ShapeDtypeStruct((8, 128), F32)],
            out_specs=[_SEM, _SEM] + [_HBM] * (2 * n) + [_VMEM],
            input_output_aliases={j: 2 + j for j in range(2 * n)},
            compiler_params=pltpu.CompilerParams(has_side_effects=_EFFECT),
        )(*thru, send_sems, recv_sems, *after)
        return [send_sems, recv_sems] + list(out[2:-1]) + [out[0], out[1]], out[-1]

    def wait(self, state, after):
        n = self.n
        send_sems, recv_sems = state[0], state[1]
        thru = state[2:2 + 2 * n]
        relay_sems = list(state[2 + 2 * n:])
        assert len(relay_sems) == (2 if self.relayed else 0)
        after = list(after) if isinstance(after, (list, tuple)) else [after]

        def body(*refs):
            src, land = refs[:n], refs[n:2 * n]
            send_a, recv_a = refs[2 * n], refs[2 * n + 1]
            for p in self.direct:
                for w in range(self.ncls):
                    copy = self._span_copy(src, land, w, send_a.at[self._sem(p, w)], recv_a.at[self._sem(p, w)], p)
                    copy.wait_send()
                    if not (self.relayed and p in self.RELAYED):
                        copy.wait_recv()
            if self.relayed:
                send_b, recv_b = refs[2 * n + 2], refs[2 * n + 3]
                for j in range(len(self.RELAYED)):
                    for w in range(self.ncls):
                        copy = self._span_copy(src, land, w, send_b.at[j * self.ncls + w],
                                               recv_b.at[j * self.ncls + w], 1)
                        copy.wait_send()
                        copy.wait_recv()

        out = pl.pallas_call(
            body, name=self.name + "_wait",
            in_specs=[_HBM] * (2 * n) + [_SEM] * (2 + len(relay_sems)) + [_ANY] * len(after),
            out_shape=[pltpu.HBM(a.shape, a.dtype) for a in thru], out_specs=[_HBM] * (2 * n),
            input_output_aliases={j: j for j in range(2 * n)},
            compiler_params=pltpu.CompilerParams(has_side_effects=_EFFECT),
        )(*thru, send_sems, recv_sems, *relay_sems, *after)
        return out[:n], out[n:]

    def place(self, lands, srcs):
        n = self.n
        assert not self.scatter

        def body(*refs):
            src, land = refs[n:2 * n], refs[2 * n:3 * n]
            bufs, sems = refs[3 * n:4 * n], refs[-1]
            me = _my_index()
            loads = [pltpu.make_async_copy(src[k], bufs[k], sems.at[k]) for k in range(n)]
            stores = [pltpu.make_async_copy(bufs[k], land[k].at[self._block(k, me), :], sems.at[k]) for k in range(n)]
            for cp in loads:
                cp.start()
            for k in range(n):
                loads[k].wait()
                stores[k].start()
            for cp in stores:
                cp.wait()

        return pl.pallas_call(
            body, name=self.name + "_place", in_specs=[_ANY] * (2 * n), out_specs=[_ANY] * n,
            out_shape=[jax.ShapeDtypeStruct(a.shape, a.dtype) for a in lands],
            input_output_aliases={j: j for j in range(n)},
            scratch_shapes=[pltpu.VMEM(s.shape, s.dtype) for s in srcs] + [pltpu.SemaphoreType.DMA((n,))],
        )(*lands, *srcs)


def sum_slots(slots, name):
    _, rows, cols = slots.shape
    tr = rows
    if rows > 512:
        for cand in (256, 128, 64, 32, 16, 8):
            if rows % cand == 0:
                tr = cand
                break

    def body(s_ref, o_ref):
        acc = s_ref[0].astype(F32)
        for j in range(1, N_DEV):
            acc = acc + s_ref[j].astype(F32)
        o_ref[...] = acc

    return pl.pallas_call(
        body, name=name, grid=(rows // tr,),
        in_specs=[pl.BlockSpec((N_DEV, tr, cols), lambda i: (0, i, 0))], out_specs=_row_spec(tr, cols),
        out_shape=jax.ShapeDtypeStruct((rows, cols), F32), compiler_params=_params(("arbitrary",)),
    )(slots)


BIG_N = ("ffn1_w_down", "w_out", "ffn2_w_down")
SMALL = ("ffn1_norm", "mix_norm", "attn_sinks", "ssm_a_re", "ssm_a_im", "ssm_log_dt", "ssm_b_re", "ssm_b_im",
         "ssm_c_re", "ssm_c_im", "ssm_d", "ffn2_norm", "final_norm")
PARTS = {"ffn1": ("ffn1_w_gate", "ffn1_w_up", "ffn1_w_down"),
         "mix": ("w_in", "w_out", "w_attn_proj", "w_glu_v", "w_glu_g"),
         "ffn2": ("ffn2_w_gate", "ffn2_w_up", "ffn2_w_down")}


def _to_rows(name, a):
    return a if name in BIG_N else jnp.swapaxes(a, -1, -2)


def local_step(x, tgt, get_weights, put_grads, small):
    seq, d = x.shape
    t = PAD_FRONT + N_META + seq
    cos_t, sin_t = rope_tables(t)
    row = lambda a: a.reshape(1, -1)
    tables = []
    for i in range(DEPTH):
        b_re_t = jnp.swapaxes(small["ssm_b_re"][i], 1, 2)
        b_im_t = jnp.swapaxes(small["ssm_b_im"][i], 1, 2)
        lam_re, lam_im, bbar_re, bbar_im = ssm_prep(small["ssm_a_re"][i], small["ssm_a_im"][i],
                                                    small["ssm_log_dt"][i].reshape(-1, 1), b_re_t, b_im_t, f"ssm_prep_{i}")
        tables.append(((b_re_t, b_im_t),
                       (row(lam_re), row(lam_im), _block_diag_b(bbar_re).astype(BF16), _block_diag_b(bbar_im).astype(BF16),
                        _block_diag_c(small["ssm_c_re"][i]).astype(BF16), _block_diag_c(small["ssm_c_im"][i]).astype(BF16),
                        row(small["ssm_d"][i]))))
    early = [cos_t, sin_t] + [a for _, tab in tables for a in tab[2:6]]
    saved = []
    h = None
    for i in range(DEPTH):
        s = {}
        w = dict(get_weights(i, "ffn1", early if i == 0 else h))
        if i == 0:
            h = jnp.concatenate([jnp.zeros((PAD_FRONT, d), F32), w["meta_tokens"], x], axis=0)
        s["h0"] = h
        h, s["n1"], s["acts1"] = ffn_fwd(h, row(small["ffn1_norm"][i]), w["ffn1_w_gate"], w["ffn1_w_up"],
                                               w["ffn1_w_down"], f"ffn1_fwd_{i}")
        s["h1"] = h
        w.update(get_weights(i, "mix", h))
        s["n2"], s["qkv"], s["u"], s["gates"] = win_fwd(h, row(small["mix_norm"][i]), w["w_in"], cos_t, sin_t,
                                                        f"win_fwd_{i}")
        s["b_t"], s["ssm"] = tables[i]
        s["yg"], s["h_re"], s["h_im"] = ssm_fwd(s["u"], *s["ssm"], f"ssm_fwd_{i}")
        s["o"] = attn_fwd(s["qkv"], row(small["attn_sinks"][i]), f"attn_fwd_{i}")
        h, s["merged"], s["att"], s["sv"], s["sg"] = merge_fwd(
            h, s["o"], s["yg"], s["gates"], w["w_attn_proj"], w["w_glu_v"], w["w_glu_g"], w["w_out"],
            f"merge_fwd_{i}")
        s["h2"] = h
        w.update(get_weights(i, "ffn2", h))
        h, s["n3"], s["acts3"] = ffn_fwd(h, row(small["ffn2_norm"][i]), w["ffn2_w_gate"], w["ffn2_w_up"],
                                               w["ffn2_w_down"], f"ffn2_fwd_{i}")
        s["w"] = w
        saved.append(s)

    loss, dh, d_final = head_fwd_bwd(h, row(small["final_norm"]), tgt)
    gs = {k: [None] * DEPTH for k in SMALL if k != "final_norm"}
    dep = loss
    for i in reversed(range(DEPTH)):
        s = saved[i]
        w = s["w"]
        dh, da, db, sact, dhb, dg = ffn_bwd(dh, s["h2"], row(small["ffn2_norm"][i]), s["acts3"], w["ffn2_w_gate"],
                                            w["ffn2_w_up"], w["ffn2_w_down"], dep, f"ffn2_bwd_{i}")
        gs["ffn2_norm"][i] = dg[0]
        dep = put_grads(i, "ffn2", {"ffn2_w_gate": tn_matmul(da, s["n3"], f"ffn2_dwg_{i}"),
                                    "ffn2_w_up": tn_matmul(db, s["n3"], f"ffn2_dwu_{i}"),
                                    "ffn2_w_down": tn_matmul(sact, dhb, f"ffn2_dwd_{i}")})

        dgates, datt, dsv, dsg, do, dyg, dhb = merge_bwd(dh, s["gates"], s["att"], s["sv"], s["sg"], w["w_attn_proj"],
                                                         w["w_glu_v"], w["w_glu_g"], w["w_out"], dep, f"merge_bwd_{i}")
        gmix = {"w_out": tn_matmul(s["merged"], dhb, f"dwout_{i}"),
                "w_attn_proj": tn_matmul(datt, s["o"], f"dwap_{i}"),
                "w_glu_v": tn_matmul(dsv, s["yg"], f"dwv_{i}"),
                "w_glu_g": tn_matmul(dsg, s["yg"], f"dwgg_{i}")}
        dqkv, dsink = attn_bwd(s["qkv"], do, row(small["attn_sinks"][i]), cos_t, sin_t, f"attn_bwd_{i}")
        gs["attn_sinks"][i] = dsink[:, 0]
        du, dl_re, dl_im, dbb_re, dbb_im, dcc_re, dcc_im, dd = ssm_bwd(dyg, s["u"], s["h_re"], s["h_im"], *s["ssm"],
                                                                      f"ssm_bwd_{i}")
        fold = lambda a: jnp.sum(a, axis=0).reshape(SSM_GROUPS, SSM_STATE)
        da_re, da_im, dldt, db_re_t, db_im_t = ssm_prep_bwd(
            small["ssm_a_re"][i], small["ssm_a_im"][i], small["ssm_log_dt"][i].reshape(-1, 1), *s["b_t"],
            fold(dl_re), fold(dl_im), _diag_of_b(dbb_re), _diag_of_b(dbb_im), f"ssm_prep_bwd_{i}")
        gs["ssm_a_re"][i], gs["ssm_a_im"][i], gs["ssm_log_dt"][i] = da_re, da_im, dldt[:, 0]
        gs["ssm_b_re"][i], gs["ssm_b_im"][i] = jnp.swapaxes(db_re_t, 1, 2), jnp.swapaxes(db_im_t, 1, 2)
        gs["ssm_c_re"][i], gs["ssm_c_im"][i] = _diag_of_c(dcc_re), _diag_of_c(dcc_im)
        gs["ssm_d"][i] = dd[0]
        gmix["w_in"] = tn_matmul([dqkv, du, dgates], s["n2"], f"dwin_{i}")
        dep = put_grads(i, "mix", gmix)
        dh, dg = win_bwd(dh, s["h1"], row(small["mix_norm"][i]), dqkv, du, dgates, w["w_in"], dep, f"win_bwd_{i}")
        gs["mix_norm"][i] = dg[0]

        dh, da, db, sact, dhb, dg = ffn_bwd(dh, s["h0"], row(small["ffn1_norm"][i]), s["acts1"], w["ffn1_w_gate"],
                                            w["ffn1_w_up"], w["ffn1_w_down"], dep, f"ffn1_bwd_{i}")
        gs["ffn1_norm"][i] = dg[0]
        if i > 0:
            dep = put_grads(i, "ffn1", {"ffn1_w_gate": tn_matmul(da, s["n1"], f"ffn1_dwg_{i}"),
                                        "ffn1_w_up": tn_matmul(db, s["n1"], f"ffn1_dwu_{i}"),
                                        "ffn1_w_down": tn_matmul(sact, dhb, f"ffn1_dwd_{i}")})
        else:
            for k, xa, ya in (("ffn1_w_down", sact, dhb), ("ffn1_w_gate", da, s["n1"]), ("ffn1_w_up", db, s["n1"])):
                dep = put_grads(i, "ffn1", {k: tn_matmul(xa, ya, f"d_{k}_{i}", dep)})

    gs = {k: jnp.stack(v) for k, v in gs.items()}
    gs["final_norm"] = d_final[0]
    return loss[0, 0], dh[PAD_FRONT + N_META:], dh[PAD_FRONT:PAD_FRONT + N_META], gs, dep


def _pack_rows(arrays, cols):
    flat = jnp.concatenate([a.reshape(-1) for a in arrays])
    rows = -(-flat.shape[0] // cols)
    rows = -(-rows // 16) * 16
    return jnp.pad(flat, (0, rows * cols - flat.shape[0])).reshape(rows, cols)


def _unpack_rows(packed, shapes):
    flat = packed.reshape(-1)
    out, off = [], 0
    for shp in shapes:
        n = math.prod(shp)
        out.append(flat[off:off + n].reshape(shp))
        off += n
    return out


def kernel(x, meta_tokens, ffn1_norm, ffn1_w_gate, ffn1_w_up, ffn1_w_down, mix_norm, w_in, attn_sinks, ssm_a_re, ssm_a_im, ssm_log_dt, ssm_b_re, ssm_b_im, ssm_c_re, ssm_c_im, ssm_d, w_attn_proj, w_glu_v, w_glu_g, w_out, ffn2_norm, ffn2_w_gate, ffn2_w_up, ffn2_w_down, final_norm, loss_target, m_meta_tokens, m_ffn1_norm, m_ffn1_w_gate, m_ffn1_w_up, m_ffn1_w_down, m_mix_norm, m_w_in, m_attn_sinks, m_ssm_a_re, m_ssm_a_im, m_ssm_log_dt, m_ssm_b_re, m_ssm_b_im, m_ssm_c_re, m_ssm_c_im, m_ssm_d, m_w_attn_proj, m_w_glu_v, m_w_glu_g, m_w_out, m_ffn2_norm, m_ffn2_w_gate, m_ffn2_w_up, m_ffn2_w_down, m_final_norm, v_meta_tokens, v_ffn1_norm, v_ffn1_w_gate, v_ffn1_w_up, v_ffn1_w_down, v_mix_norm, v_w_in, v_attn_sinks, v_ssm_a_re, v_ssm_a_im, v_ssm_log_dt, v_ssm_b_re, v_ssm_b_im, v_ssm_c_re, v_ssm_c_im, v_ssm_d, v_w_attn_proj, v_w_glu_v, v_w_glu_g, v_w_out, v_ffn2_norm, v_ffn2_w_gate, v_ffn2_w_up, v_ffn2_w_down, v_final_norm):
    names = ("meta_tokens", "ffn1_norm", "ffn1_w_gate", "ffn1_w_up", "ffn1_w_down", "mix_norm", "w_in", "attn_sinks",
             "ssm_a_re", "ssm_a_im", "ssm_log_dt", "ssm_b_re", "ssm_b_im", "ssm_c_re", "ssm_c_im", "ssm_d",
             "w_attn_proj", "w_glu_v", "w_glu_g", "w_out", "ffn2_norm", "ffn2_w_gate", "ffn2_w_up", "ffn2_w_down",
             "final_norm")
    weights = dict(zip(names, (meta_tokens, ffn1_norm, ffn1_w_gate, ffn1_w_up, ffn1_w_down, mix_norm, w_in, attn_sinks, ssm_a_re, ssm_a_im, ssm_log_dt, ssm_b_re, ssm_b_im, ssm_c_re, ssm_c_im, ssm_d, w_attn_proj, w_glu_v, w_glu_g, w_out, ffn2_norm, ffn2_w_gate, ffn2_w_up, ffn2_w_down, final_norm)))
    moments_m = dict(zip(names, (m_meta_tokens, m_ffn1_norm, m_ffn1_w_gate, m_ffn1_w_up, m_ffn1_w_down, m_mix_norm, m_w_in, m_attn_sinks, m_ssm_a_re, m_ssm_a_im, m_ssm_log_dt, m_ssm_b_re, m_ssm_b_im, m_ssm_c_re, m_ssm_c_im, m_ssm_d, m_w_attn_proj, m_w_glu_v, m_w_glu_g, m_w_out, m_ffn2_norm, m_ffn2_w_gate, m_ffn2_w_up, m_ffn2_w_down, m_final_norm)))
    moments_v = dict(zip(names, (v_meta_tokens, v_ffn1_norm, v_ffn1_w_gate, v_ffn1_w_up, v_ffn1_w_down, v_mix_norm, v_w_in, v_attn_sinks, v_ssm_a_re, v_ssm_a_im, v_ssm_log_dt, v_ssm_b_re, v_ssm_b_im, v_ssm_c_re, v_ssm_c_im, v_ssm_d, v_w_attn_proj, v_w_glu_v, v_w_glu_g, v_w_out, v_ffn2_norm, v_ffn2_w_gate, v_ffn2_w_up, v_ffn2_w_down, v_final_norm)))
    me = _my_index()

    order = [(i, part) for i in range(DEPTH) for part in PARTS]
    gathers = {}
    token = jnp.zeros((8, 128), F32)
    for i, part in order:
        shards = [_to_rows(k, weights[k][i]).astype(BF16) for k in PARTS[part]]
        if (i, part) == order[0]:
            shards.append(meta_tokens)
        ex = Exchange(shards, False, f"gather_{part}_{i}", relay=True)
        state, token = ex.start(shards, token)
        gathers[i, part] = [ex, state, False]
    all_started = token

    def relay(group, after):
        ex, state, relayed = gathers[group]
        if relayed:
            return []
        new_state, relay_token = ex.relay(state, after)
        gathers[group][1:] = [new_state, True]
        return [relay_token]

    def get_weights(i, part, after):
        g = order.index((i, part))
        after = [all_started] + list(after) if g == 0 else [after]
        tokens = relay(order[g], after)
        if g >= 2 and g + 1 < len(order):
            tokens += relay(order[g + 1], after)
        ex, state, _ = gathers[i, part]
        shards, lands = ex.wait(state, after + tokens)
        fulls = ex.place(lands, shards)
        got = dict(zip(PARTS[part], fulls))
        if (i, part) == (0, "ffn1"):
            got["meta_tokens"] = jnp.swapaxes(fulls[-1].reshape(N_DEV, N_META, 128), 0, 1).reshape(N_META, D_MODEL)
        return got

    scatters = []

    def put_grads(i, part, gdict):
        ks = list(gdict)
        srcs = [gdict[k] for k in ks]
        ex = Exchange(srcs, True, f"scatter_{part if len(ks) > 1 else ks[0]}_{i}")
        state, tok = ex.start(srcs, all_started)
        scatters.append((i, ks, ex, state))
        return tok

    small = {k: weights[k] for k in SMALL}
    loss, dx, dmeta, gs, last_started = local_step(x[0], loss_target[0], get_weights, put_grads, small)

    grads, deltas, new_m, new_v = {}, {}, {}, {}
    small_list = [loss.reshape(1), dmeta] + [gs[k] for k in SMALL]
    packed = _pack_rows(small_list, D_MODEL)
    small_ex = Exchange([packed], False, "gather_small", relay=True)
    small_state, after = small_ex.start([packed], last_started)

    updated = {}
    me_index = jnp.reshape(me, (1,)).astype(jnp.int32)
    for i, ks, ex, state in scatters:
        partials, lands = ex.wait(state, after)
        for k, partial, slots in zip(ks, partials, lands):
            own_layout = weights[k].shape[-1] % 128 == 0 and k not in BIG_N
            view = (lambda a: a) if own_layout else (lambda a: _to_rows(k, a))
            updated[k] = sum_adamw_layer(me_index, slots, partial, view(weights[k]), view(moments_m[k]),
                                         view(moments_v[k]), i, updated.get(k), f"adamw_{k}_{i}", transposed=own_layout)
            after = updated[k][0]
    for k, outs in updated.items():
        own_layout = weights[k].shape[-1] % 128 == 0 and k not in BIG_N
        grads[k], deltas[k], new_m[k], new_v[k] = [a if own_layout else _to_rows(k, a) for a in outs]

    small_state, relayed = small_ex.relay(small_state, after)
    packed_own, packed_all = small_ex.wait(small_state, [after, relayed])
    (packed_all,) = small_ex.place(packed_all, packed_own)
    total = sum_slots(packed_all.reshape(N_DEV, packed.shape[0], D_MODEL), "sum_small")
    pieces = _unpack_rows(total, [a.shape for a in small_list])
    loss_out = pieces[0][0]
    grads["meta_tokens"] = lax.dynamic_slice_in_dim(pieces[1], me * 128, 128, axis=1)
    for k, p in zip(SMALL, pieces[2:]):
        grads[k] = p
    small_names = ("meta_tokens",) + SMALL
    updates = adamw_small([(weights[k], grads[k], moments_m[k], moments_v[k], k in ("ssm_b_re", "ssm_b_im"))
                           for k in small_names], "adamw_small")
    for k, (d, mn, vn) in zip(small_names, updates):
        deltas[k], new_m[k], new_v[k] = d, mn, vn
    return (loss_out, dx[None], *[grads[k] for k in names], *[deltas[k] for k in names],
            *[new_m[k] for k in names], *[new_v[k] for k in names])
```

```python
import math

import jax
import jax.numpy as jnp
from jax import lax
from jax.experimental import pallas as pl
from jax.experimental.pallas import tpu as pltpu

F32 = jnp.float32
BF16 = jnp.bfloat16

D_MODEL = 1024
DEPTH = 2
N_META = 16
HEAD_DIM = 64
N_Q_HEADS = 8
ATTN_WIDTH = 512
KV_WIDTH = 128
QKV_WIDTH = ATTN_WIDTH + 2 * KV_WIDTH
WINDOW = 128
BLK = 128
ROPE_THETA = 500000.0
ROT_DIM = 16
SSM_WIDTH = 512
SSM_GROUP = 16
SSM_GROUPS = 32
SSM_STATE = 64
STATE_WIDTH = SSM_GROUPS * SSM_STATE
D_FF = 2816
IN_WIDTH = 3328
EPS = 1e-6
NEG_INF = -1e30
PAD_FRONT = (-N_META) % BLK
N_DEV = 8

ADAM_LR = 0.001
ADAM_B1 = 0.9
ADAM_B2 = 0.999
ADAM_EPS = 1e-08
ADAM_WD = 0.01
ADAM_STEP = 10

VMEM_LIMIT = 56 * 1024 * 1024
TOKEN_TILE = 384
_VMEM = pl.BlockSpec(memory_space=pltpu.VMEM)
_SMEM = pl.BlockSpec(memory_space=pltpu.SMEM)
_ANY = pl.BlockSpec(memory_space=pl.ANY)
MESH = pl.DeviceIdType.MESH


def _params(sem=None):
    return pltpu.CompilerParams(dimension_semantics=sem, vmem_limit_bytes=VMEM_LIMIT)


def _nt(a, b):
    return lax.dot_general(a, b, (((1,), (1,)), ((), ())), preferred_element_type=F32)


def _nn(a, b):
    return jnp.dot(a, b, preferred_element_type=F32)


def _tn(a, b):
    return lax.dot_general(a, b, (((0,), (0,)), ((), ())), preferred_element_type=F32)


def _row_spec(tm, width):
    return pl.BlockSpec((tm, width), lambda i: (i, 0))


def _acc_spec(shape):
    return pl.BlockSpec(shape, lambda i: (0,) * len(shape))


def _rms_stats(x):
    r = lax.rsqrt(jnp.mean(x * x, axis=-1, keepdims=True) + EPS)
    return x * r, r


def _rms_bwd(dn, xh, r, g):
    dg = jnp.sum(dn * xh, axis=0, keepdims=True)
    dxh = dn * g
    dx = r * (dxh - xh * jnp.mean(dxh * xh, axis=-1, keepdims=True))
    return dx, dg


def ffn_fwd(h, g, wg_t, wu_t, wd, name, first=None):
    f = wd.shape[0]
    tm = TOKEN_TILE
    if first is None:
        t, d = h.shape
        rows_in, row_specs = [h], [_row_spec(tm, d)]
    else:
        x_in, head = first
        d = x_in.shape[1]
        t = x_in.shape[0] + BLK
        per_tile, last = tm // BLK, x_in.shape[0] // BLK - 1
        rows_in = [head] + [x_in] * per_tile
        row_specs = [_acc_spec((BLK, d))] + [
            pl.BlockSpec((BLK, d), lambda i, k=k: (jnp.clip(i * per_tile - 1 + k, 0, last), 0)) for k in range(per_tile)]
    n_in = len(rows_in)

    def body(*refs):
        g_ref, wg_ref, wu_ref, wd_ref = refs[n_in:n_in + 4]
        ho_ref, n_ref, sl_ref, p_ref, s_ref = refs[-5:]
        if first is None:
            x = refs[0][...]
        else:
            blocks = [ref[...] for ref in refs[1:n_in]]
            blocks[0] = jnp.where(pl.program_id(0) == 0, refs[0][...], blocks[0])
            x = jnp.concatenate(blocks, axis=0)
            refs[n_in + 4][...] = x
        xh, _ = _rms_stats(x)
        n = (xh * g_ref[...]).astype(BF16)
        n_ref[...] = n
        a = _nt(n, wg_ref[...])
        b = _nt(n, wu_ref[...])
        sig = jax.nn.sigmoid(a)
        sl = a * sig
        sl_ref[...] = sl.astype(BF16)
        p_ref[...] = (b * (sig + sl * (1.0 - sig))).astype(BF16)
        s = (sl * b).astype(BF16)
        s_ref[...] = s
        ho_ref[...] = x + 0.5 * _nn(s, wd_ref[...])

    extra_specs = [] if first is None else [_row_spec(tm, d)]
    extra_shapes = [] if first is None else [jax.ShapeDtypeStruct((t, d), F32)]
    *h_in, ho, n, sl, p, s = pl.pallas_call(
        body, name=name, grid=(t // tm,),
        in_specs=row_specs + [_acc_spec((1, d)), _VMEM, _VMEM, _VMEM],
        out_specs=extra_specs + [_row_spec(tm, d), _row_spec(tm, d), _row_spec(tm, f), _row_spec(tm, f),
                                 _row_spec(tm, f)],
        out_shape=extra_shapes + [jax.ShapeDtypeStruct((t, d), F32), jax.ShapeDtypeStruct((t, d), BF16),
                                  jax.ShapeDtypeStruct((t, f), BF16), jax.ShapeDtypeStruct((t, f), BF16),
                                  jax.ShapeDtypeStruct((t, f), BF16)],
        compiler_params=_params(("arbitrary",)),
    )(*rows_in, g, wg_t, wu_t, wd)
    return (*h_in, ho, n, (sl, p, s))


def ffn_bwd(dh, h, g, acts, wg_t, wu_t, wd, dep, name):
    t, d = h.shape
    f = wd.shape[0]
    tm = TOKEN_TILE
    sl, p, s = acts

    def hidden_body(dh_ref, sl_ref, p_ref, wd_ref, dep_ref, da_ref, db_ref, dhb_ref):
        dhb = (0.5 * dh_ref[...]).astype(BF16)
        dhb_ref[...] = dhb
        ds = _nt(dhb, wd_ref[...])
        da_ref[...] = (ds * p_ref[...].astype(F32)).astype(BF16)
        db_ref[...] = (ds * sl_ref[...].astype(F32)).astype(BF16)

    da, db, dhb = pl.pallas_call(
        hidden_body, name=name + "_h", grid=(t // tm,),
        in_specs=[_row_spec(tm, d), _row_spec(tm, f), _row_spec(tm, f), _VMEM, _ANY],
        out_specs=[_row_spec(tm, f), _row_spec(tm, f), _row_spec(tm, d)],
        out_shape=[jax.ShapeDtypeStruct((t, f), BF16), jax.ShapeDtypeStruct((t, f), BF16),
                   jax.ShapeDtypeStruct((t, d), BF16)],
        compiler_params=_params(("arbitrary",)),
    )(dh, sl, p, wd, dep)

    def input_body(dh_ref, h_ref, g_ref, da_ref, db_ref, wg_ref, wu_ref, dhi_ref, dg_ref):
        dn = _nn(da_ref[...], wg_ref[...]) + _nn(db_ref[...], wu_ref[...])
        xh, r = _rms_stats(h_ref[...])
        dx, dg = _rms_bwd(dn, xh, r, g_ref[...])
        dhi_ref[...] = dh_ref[...] + dx

        @pl.when(pl.program_id(0) == 0)
        def _():
            dg_ref[...] = jnp.zeros_like(dg_ref)

        dg_ref[...] += dg

    dhi, dg = pl.pallas_call(
        input_body, name=name + "_x", grid=(t // tm,),
        in_specs=[_row_spec(tm, d), _row_spec(tm, d), _acc_spec((1, d)), _row_spec(tm, f), _row_spec(tm, f),
                  _VMEM, _VMEM],
        out_specs=[_row_spec(tm, d), _acc_spec((1, d))],
        out_shape=[jax.ShapeDtypeStruct((t, d), F32), jax.ShapeDtypeStruct((1, d), F32)],
        compiler_params=_params(("arbitrary",)),
    )(dh, h, g, da, db, wg_t, wu_t)
    return dhi, da, db, s, dhb, dg


DW_TILE = 256


def tn_matmul(x, y, name, dep=None):
    xs = list(x) if isinstance(x, (list, tuple)) else [x]
    t = xs[0].shape[0]
    n = y.shape[1]
    bm = DW_TILE
    tiles = [a.shape[1] // bm for a in xs]
    offs = [sum(tiles[:k]) for k in range(len(xs))]
    deps = [] if dep is None else [dep]

    def body(*refs):
        y_ref, o_ref = refs[len(xs)], refs[-1]
        i = pl.program_id(0)
        for k in range(len(xs)):
            @pl.when((i >= offs[k]) & (i < offs[k] + tiles[k]))
            def _(k=k):
                o_ref[...] = _tn(refs[k][...], y_ref[...]).astype(BF16)

    def x_spec(k):
        return pl.BlockSpec((t, bm), lambda i: (0, jnp.clip(i - offs[k], 0, tiles[k] - 1)))

    return pl.pallas_call(
        body, name=name, grid=(sum(tiles),),
        in_specs=[x_spec(k) for k in range(len(xs))] + [_VMEM] + [_ANY] * len(deps),
        out_specs=pl.BlockSpec((bm, n), lambda i: (i, 0)),
        out_shape=jax.ShapeDtypeStruct((sum(tiles) * bm, n), BF16),
        compiler_params=_params(("arbitrary",)),
    )(*xs, y, *deps)


def head_fwd_bwd(h, g, tgt):
    t, d = h.shape
    tm = TOKEN_TILE
    per_tile = tm // BLK
    last = tgt.shape[0] // BLK - 1

    def body(h_ref, g_ref, *rest):
        t_refs, (loss_ref, dh_ref, dg_ref) = rest[:per_tile], rest[per_tile:]
        i = pl.program_id(0)
        xh, r = _rms_stats(h_ref[...])
        gv = g_ref[...]
        target = jnp.concatenate([ref[...] for ref in t_refs], axis=0)
        row = i * tm + lax.broadcasted_iota(jnp.int32, (tm, 1), 0)
        e = jnp.where(row >= BLK, xh * gv - target, 0.0)
        dx, dg = _rms_bwd(e * (1.0 / d), xh, r, gv)
        dh_ref[...] = dx

        @pl.when(i == 0)
        def _():
            dg_ref[...] = jnp.zeros_like(dg_ref)
            loss_ref[...] = jnp.zeros_like(loss_ref)

        dg_ref[...] += dg
        loss_ref[...] += jnp.sum(e * e) * (0.5 / d)

    def target_spec(k):
        return pl.BlockSpec((BLK, d), lambda i: (jnp.clip(i * per_tile - 1 + k, 0, last), 0))

    return pl.pallas_call(
        body, name="head", grid=(t // tm,),
        in_specs=[_row_spec(tm, d), _acc_spec((1, d))] + [target_spec(k) for k in range(per_tile)],
        out_specs=[_acc_spec((1, 128)), _row_spec(tm, d), _acc_spec((1, d))],
        out_shape=[jax.ShapeDtypeStruct((1, 128), F32), jax.ShapeDtypeStruct((t, d), F32),
                   jax.ShapeDtypeStruct((1, d), F32)],
        compiler_params=_params(("arbitrary",)),
    )(h, g, *[tgt] * per_tile)


def rope_tables(t):
    pos = jnp.arange(t, dtype=F32) - PAD_FRONT
    inv_freq = ROPE_THETA ** (-jnp.arange(0, ROT_DIM, 2, dtype=F32) / ROT_DIM)
    ang = pos[:, None] * inv_freq[None, :]
    cos, sin = jnp.cos(ang), jnp.sin(ang)
    ones = jnp.ones((t, HEAD_DIM - ROT_DIM), F32)
    cos_h = jnp.concatenate([cos, cos, ones], axis=1)
    sin_h = jnp.concatenate([-sin, sin, 0.0 * ones], axis=1)
    return jnp.concatenate([cos_h, cos_h], axis=1), jnp.concatenate([sin_h, sin_h], axis=1)


def _swap_halves(x):
    n = x.shape[1]
    lane = lax.broadcasted_iota(jnp.int32, x.shape, 1)
    return jnp.where(lane % HEAD_DIM < ROT_DIM // 2, pltpu.roll(x, n - ROT_DIM // 2, 1), pltpu.roll(x, ROT_DIM // 2, 1))


def _rope(x, cos_t, sin_t, sign):
    return x * cos_t + sign * (_swap_halves(x) * sin_t)


def win_fwd(h, g, win_t, cos_t, sin_t, name):
    t, d = h.shape
    tm = TOKEN_TILE

    def body(h_ref, g_ref, w_ref, c_ref, s_ref, n_ref, qkv_ref, u_ref, gates_ref):
        xh, _ = _rms_stats(h_ref[...])
        n = (xh * g_ref[...]).astype(BF16)
        n_ref[...] = n
        z = _nt(n, w_ref[...])
        c, s = c_ref[...], s_ref[...]
        for j in range((ATTN_WIDTH + KV_WIDTH) // 128):
            qkv_ref[:, j * 128:(j + 1) * 128] = _rope(z[:, j * 128:(j + 1) * 128], c, s, 1.0).astype(BF16)
        qkv_ref[:, ATTN_WIDTH + KV_WIDTH:QKV_WIDTH] = z[:, ATTN_WIDTH + KV_WIDTH:QKV_WIDTH].astype(BF16)
        for j in range(N_CHUNK):
            u_ref[j] = z[:, QKV_WIDTH + j * U_CHUNK:QKV_WIDTH + (j + 1) * U_CHUNK]
        gates_ref[...] = z[:, QKV_WIDTH + SSM_WIDTH:].astype(BF16)

    return pl.pallas_call(
        body, name=name, grid=(t // tm,),
        in_specs=[_row_spec(tm, d), _acc_spec((1, d)), _VMEM, _row_spec(tm, 128), _row_spec(tm, 128)],
        out_specs=[_row_spec(tm, d), _row_spec(tm, QKV_WIDTH), _chunked_spec(tm, lambda i: i), _row_spec(tm, 2 * d)],
        out_shape=[jax.ShapeDtypeStruct((t, d), BF16), jax.ShapeDtypeStruct((t, QKV_WIDTH), BF16),
                   jax.ShapeDtypeStruct((N_CHUNK, t, U_CHUNK), F32), jax.ShapeDtypeStruct((t, 2 * d), BF16)],
        compiler_params=_params(("arbitrary",)),
    )(h, g, win_t, cos_t, sin_t)


def win_bwd(dh, h, g, dqkv, du, dgates, win_t, dep, name):
    t, d = h.shape
    tm = TOKEN_TILE

    def body(dh_ref, h_ref, g_ref, dqkv_ref, du_ref, dgt_ref, w_ref, dep_ref, dhi_ref, dg_ref):
        dn = (_nn(dqkv_ref[...], w_ref[0:QKV_WIDTH, :])
              + _nn(du_ref[...], w_ref[QKV_WIDTH:QKV_WIDTH + SSM_WIDTH, :])
              + _nn(dgt_ref[...], w_ref[QKV_WIDTH + SSM_WIDTH:, :]))
        xh, r = _rms_stats(h_ref[...])
        dx, dg = _rms_bwd(dn, xh, r, g_ref[...])
        dhi_ref[...] = dh_ref[...] + dx

        @pl.when(pl.program_id(0) == 0)
        def _():
            dg_ref[...] = jnp.zeros_like(dg_ref)

        dg_ref[...] += dg

    return pl.pallas_call(
        body, name=name, grid=(t // tm,),
        in_specs=[_row_spec(tm, d), _row_spec(tm, d), _acc_spec((1, d)), _row_spec(tm, QKV_WIDTH),
                  _row_spec(tm, SSM_WIDTH), _row_spec(tm, 2 * d), _VMEM, _ANY],
        out_specs=[_row_spec(tm, d), _acc_spec((1, d))],
        out_shape=[jax.ShapeDtypeStruct((t, d), F32), jax.ShapeDtypeStruct((1, d), F32)],
        compiler_params=_params(("arbitrary",)),
    )(dh, h, g, dqkv, du, dgates, win_t, dep)


def _attn_mask(blk):
    q_pos = blk * BLK + lax.broadcasted_iota(jnp.int32, (BLK, 3 * BLK), 0) - PAD_FRONT
    col = lax.broadcasted_iota(jnp.int32, (BLK, 3 * BLK), 1)
    part = col // BLK
    k_pos = jnp.where(part == 0, col, (blk + part - 2) * BLK + (col - part * BLK)) - PAD_FRONT
    dist = q_pos - k_pos
    meta_ok = (part == 0) & (k_pos >= 0) & (dist >= 0)
    band_ok = (part > 0) & (k_pos >= N_META) & (dist >= 0) & (dist < WINDOW)
    return meta_ok | band_ok


def _head_halves(x128, kv):
    x = x128.astype(F32)
    lane = lax.broadcasted_iota(jnp.int32, x.shape, 1)
    swapped = pltpu.roll(x, HEAD_DIM, 1)
    lo, hi = (x, swapped) if kv == 0 else (swapped, x)
    return jnp.where(lane < HEAD_DIM, lo, 0.0).astype(BF16), jnp.where(lane >= HEAD_DIM, hi, 0.0).astype(BF16)


def _gather_keys(meta_ref, prev_ref, cur_ref, lo):
    return jnp.concatenate([meta_ref[:, lo:lo + 128], prev_ref[:, lo:lo + 128], cur_ref[:, lo:lo + 128]], axis=0)


def _pair_lanes(kv):
    return slice(2 * kv * 128, (2 * kv + 1) * 128), slice((2 * kv + 1) * 128, (2 * kv + 2) * 128)


def _stacked_sinks(sink_ref, head):
    row = lax.broadcasted_iota(jnp.int32, (2 * BLK, 1), 0)
    return jnp.where(row < BLK, sink_ref[0, head], sink_ref[0, head + 2])


def _softmax_with_sink(s, mask, sink):
    s = jnp.where(mask, s * (HEAD_DIM ** -0.5), NEG_INF)
    m = jnp.maximum(jnp.max(s, axis=-1, keepdims=True), sink)
    p = jnp.exp(s - m)
    p_sink = jnp.exp(sink - m)
    inv = 1.0 / (jnp.sum(p, axis=-1, keepdims=True) + p_sink)
    return p * inv, p_sink * inv


def attn_fwd(qkv, sinks, name):
    t = qkv.shape[0]
    nb = t // BLK

    def body(sink_ref, meta_ref, prev_ref, cur_ref, o_ref):
        blk = pl.program_id(0)
        mask = _attn_mask(blk)
        mask2 = jnp.concatenate([mask, mask], axis=0)
        k128 = _gather_keys(meta_ref, prev_ref, cur_ref, ATTN_WIDTH)
        v128 = _gather_keys(meta_ref, prev_ref, cur_ref, ATTN_WIDTH + KV_WIDTH)
        for kv in range(2):
            k_lo, k_hi = _head_halves(k128, kv)
            v_lo, v_hi = _head_halves(v128, kv)
            lanes0, lanes1 = _pair_lanes(kv)
            q2 = jnp.concatenate([cur_ref[:, lanes0], cur_ref[:, lanes1]], axis=0)
            p_a, _ = _softmax_with_sink(_nt(q2, k_lo), mask2, _stacked_sinks(sink_ref, 4 * kv))
            p_b, _ = _softmax_with_sink(_nt(q2, k_hi), mask2, _stacked_sinks(sink_ref, 4 * kv + 1))
            o2 = (_nn(p_a.astype(BF16), v_lo) + _nn(p_b.astype(BF16), v_hi)).astype(BF16)
            o_ref[:, lanes0] = o2[0:BLK]
            o_ref[:, lanes1] = o2[BLK:2 * BLK]

    blk_spec = lambda f: pl.BlockSpec((BLK, QKV_WIDTH), f)
    return pl.pallas_call(
        body, name=name, grid=(nb,),
        in_specs=[_SMEM, blk_spec(lambda i: (0, 0)), blk_spec(lambda i: (jnp.maximum(i - 1, 0), 0)),
                  blk_spec(lambda i: (i, 0))],
        out_specs=_row_spec(BLK, ATTN_WIDTH),
        out_shape=jax.ShapeDtypeStruct((t, ATTN_WIDTH), BF16),
        compiler_params=_params(("arbitrary",)),
    )(sinks, qkv, qkv, qkv)


def attn_bwd(qkv, do, sinks, cos_t, sin_t, name):
    t = qkv.shape[0]
    nb = t // BLK

    def body(sink_ref, meta_ref, prev_ref, cur_ref, do_ref, c_ref, s_ref, dqkv_ref, dsink_ref, carry_ref, macc_ref):
        step = pl.program_id(0)
        blk = nb - 1 - step

        @pl.when(step == 0)
        def _():
            dsink_ref[...] = jnp.zeros_like(dsink_ref)
            carry_ref[...] = jnp.zeros_like(carry_ref)
            macc_ref[...] = jnp.zeros_like(macc_ref)

        mask = _attn_mask(blk)
        mask2 = jnp.concatenate([mask, mask], axis=0)
        lane = lax.broadcasted_iota(jnp.int32, (3 * BLK, 128), 1)
        k128 = _gather_keys(meta_ref, prev_ref, cur_ref, ATTN_WIDTH)
        v128 = _gather_keys(meta_ref, prev_ref, cur_ref, ATTN_WIDTH + KV_WIDTH)
        cos_b, sin_b = c_ref[...], s_ref[...]
        dk_heads, dv_heads = [], []
        for kv in range(2):
            k_lo, k_hi = _head_halves(k128, kv)
            v_lo, v_hi = _head_halves(v128, kv)
            lanes0, lanes1 = _pair_lanes(kv)
            q2 = jnp.concatenate([cur_ref[:, lanes0], cur_ref[:, lanes1]], axis=0)
            do2 = jnp.concatenate([do_ref[:, lanes0], do_ref[:, lanes1]], axis=0)
            ds_half, p_half = [], []
            for half, (k_h, v_h) in enumerate(((k_lo, v_lo), (k_hi, v_hi))):
                head = 4 * kv + half
                p, p_sink = _softmax_with_sink(_nt(q2, k_h), mask2, _stacked_sinks(sink_ref, head))
                dp = _nt(do2, v_h)
                dsum = jnp.sum(p * dp, axis=-1, keepdims=True)
                ds_half.append((p * (dp - dsum) * (HEAD_DIM ** -0.5)).astype(BF16))
                p_half.append(p.astype(BF16))
                dsink = p_sink * dsum
                for part, h in ((0, head), (1, head + 2)):
                    total = -jnp.sum(dsink[part * BLK:(part + 1) * BLK], axis=0, keepdims=True)
                    dsink_ref[h:h + 1, :] += jnp.broadcast_to(total, (1, 128))
            dq2 = _nn(ds_half[0], k_lo) + _nn(ds_half[1], k_hi)
            dqkv_ref[:, lanes0] = _rope(dq2[0:BLK], cos_b, sin_b, -1.0).astype(BF16)
            dqkv_ref[:, lanes1] = _rope(dq2[BLK:2 * BLK], cos_b, sin_b, -1.0).astype(BF16)
            dk_acc = jnp.where(lane < HEAD_DIM, _tn(ds_half[0], q2), _tn(ds_half[1], q2))
            dv_acc = jnp.where(lane < HEAD_DIM, _tn(p_half[0], do2), _tn(p_half[1], do2))
            dk_heads.append(dk_acc + pltpu.roll(dk_acc, HEAD_DIM, 1))
            dv_heads.append(dv_acc + pltpu.roll(dv_acc, HEAD_DIM, 1))
        dkv = jnp.concatenate([jnp.where(lane < HEAD_DIM, dk_heads[0], dk_heads[1]),
                               jnp.where(lane < HEAD_DIM, dv_heads[0], dv_heads[1])], axis=1)
        macc_ref[...] += dkv[0:BLK]
        is_last = (blk == 0).astype(F32)
        mine = dkv[2 * BLK:3 * BLK] + carry_ref[...] + is_last * macc_ref[...]
        carry_ref[...] = dkv[BLK:2 * BLK]
        dqkv_ref[:, ATTN_WIDTH:ATTN_WIDTH + KV_WIDTH] = _rope(mine[:, 0:128], cos_b, sin_b, -1.0).astype(BF16)
        dqkv_ref[:, ATTN_WIDTH + KV_WIDTH:QKV_WIDTH] = mine[:, 128:256].astype(BF16)

    rev = lambda i: nb - 1 - i
    blk_spec = lambda f: pl.BlockSpec((BLK, QKV_WIDTH), f)
    return pl.pallas_call(
        body, name=name, grid=(nb,),
        in_specs=[_SMEM, blk_spec(lambda i: (0, 0)), blk_spec(lambda i: (jnp.maximum(rev(i) - 1, 0), 0)),
                  blk_spec(lambda i: (rev(i), 0)), pl.BlockSpec((BLK, ATTN_WIDTH), lambda i: (rev(i), 0)),
                  pl.BlockSpec((BLK, 128), lambda i: (rev(i), 0)), pl.BlockSpec((BLK, 128), lambda i: (rev(i), 0))],
        out_specs=[pl.BlockSpec((BLK, QKV_WIDTH), lambda i: (rev(i), 0)), _acc_spec((N_Q_HEADS, 128))],
        out_shape=[jax.ShapeDtypeStruct((t, QKV_WIDTH), BF16), jax.ShapeDtypeStruct((N_Q_HEADS, 128), F32)],
        scratch_shapes=[pltpu.VMEM((BLK, 256), F32), pltpu.VMEM((BLK, 256), F32)],
        compiler_params=_params(("arbitrary",)),
    )(sinks, qkv, qkv, qkv, do, cos_t, sin_t)


def _cmul(ar, ai, br, bi):
    return ar * br - ai * bi, ar * bi + ai * br


def ssm_prep(a_re, a_im, log_dt, b_re_t, b_im_t, name):
    def body(ar_ref, ai_ref, ldt_ref, br_ref, bi_ref, lr_ref, li_ref, bbr_ref, bbi_ref):
        ar, ai = ar_ref[...], ai_ref[...]
        dt = jnp.exp(ldt_ref[...])
        mag = jnp.exp(ar * dt)
        lr = mag * jnp.cos(ai * dt)
        li = mag * jnp.sin(ai * dt)
        den = ar * ar + ai * ai
        nr = lr - 1.0
        cr = ((nr * ar + li * ai) / den)[:, None, :]
        ci = ((li * ar - nr * ai) / den)[:, None, :]
        br, bi = br_ref[...], bi_ref[...]
        lr_ref[...] = lr
        li_ref[...] = li
        bbr_ref[...] = cr * br - ci * bi
        bbi_ref[...] = cr * bi + ci * br

    gp = jax.ShapeDtypeStruct(a_re.shape, F32)
    gcp = jax.ShapeDtypeStruct(b_re_t.shape, F32)
    return pl.pallas_call(body, name=name, out_shape=[gp, gp, gcp, gcp],
                          in_specs=[_VMEM] * 5, out_specs=[_VMEM] * 4)(a_re, a_im, log_dt, b_re_t, b_im_t)


def ssm_prep_bwd(a_re, a_im, log_dt, b_re_t, b_im_t, dl_re, dl_im, dbb_re, dbb_im, name):
    def body(ar_ref, ai_ref, ldt_ref, br_ref, bi_ref, dlr_ref, dli_ref, dbbr_ref, dbbi_ref,
             dar_ref, dai_ref, dldt_ref, dbr_ref, dbi_ref):
        ar, ai = ar_ref[...], ai_ref[...]
        dt = jnp.exp(ldt_ref[...])
        mag = jnp.exp(ar * dt)
        lr = mag * jnp.cos(ai * dt)
        li = mag * jnp.sin(ai * dt)
        den = ar * ar + ai * ai
        nr = lr - 1.0
        cr = (nr * ar + li * ai) / den
        ci = (li * ar - nr * ai) / den
        br, bi = br_ref[...], bi_ref[...]
        dbbr, dbbi = dbbr_ref[...], dbbi_ref[...]
        dbr_ref[...] = cr[:, None, :] * dbbr + ci[:, None, :] * dbbi
        dbi_ref[...] = cr[:, None, :] * dbbi - ci[:, None, :] * dbbr
        dcr = jnp.sum(br * dbbr + bi * dbbi, axis=1)
        dci = jnp.sum(br * dbbi - bi * dbbr, axis=1)
        d_num_r = dcr / den
        d_num_i = dci / den
        d_den = -(dcr * cr + dci * ci) / den
        d_lr = dlr_ref[...] + d_num_r * ar - d_num_i * ai
        d_li = dli_ref[...] + d_num_r * ai + d_num_i * ar
        d_ar = d_num_r * nr + d_num_i * li + d_den * 2.0 * ar
        d_ai = d_num_r * li - d_num_i * nr + d_den * 2.0 * ai
        d_mag = (d_lr * lr + d_li * li) / mag
        d_theta = d_li * lr - d_lr * li
        d_ardt = d_mag * mag
        dar_ref[...] = d_ar + d_ardt * dt
        dai_ref[...] = d_ai + d_theta * dt
        d_dt = jnp.sum(d_ardt * ar + d_theta * ai, axis=1, keepdims=True)
        dldt_ref[...] = d_dt * dt

    gp = jax.ShapeDtypeStruct(a_re.shape, F32)
    gcp = jax.ShapeDtypeStruct(b_re_t.shape, F32)
    return pl.pallas_call(body, name=name, out_shape=[gp, gp, jax.ShapeDtypeStruct(log_dt.shape, F32), gcp, gcp],
                          in_specs=[_VMEM] * 9, out_specs=[_VMEM] * 5,
                          )(a_re, a_im, log_dt, b_re_t, b_im_t, dl_re, dl_im, dbb_re, dbb_im)


N_CHUNK = 4
U_CHUNK = SSM_WIDTH // N_CHUNK
H_CHUNK = STATE_WIDTH // N_CHUNK
SUB = 8


def _block_diag_b(bb):
    x = bb.reshape(N_CHUNK, 8, SSM_GROUP, 1, SSM_STATE)
    same = (jnp.arange(8)[:, None] == jnp.arange(8)[None, :])[None, :, None, :, None]
    return jnp.where(same, x, 0.0).reshape(N_CHUNK, U_CHUNK, H_CHUNK)


def _block_diag_c(c):
    x = jnp.swapaxes(c.reshape(N_CHUNK, 8, SSM_GROUP, SSM_STATE), 2, 3)[:, :, :, None, :]
    same = (jnp.arange(8)[:, None] == jnp.arange(8)[None, :])[None, :, None, :, None]
    return jnp.where(same, x, 0.0).reshape(N_CHUNK, H_CHUNK, U_CHUNK)


def _diag_of_b(m):
    x = m.reshape(N_CHUNK, 8, SSM_GROUP, 8, SSM_STATE)
    return jnp.stack([x[:, g, :, g, :] for g in range(8)], axis=1).reshape(SSM_GROUPS, SSM_GROUP, SSM_STATE)


def _diag_of_c(m):
    x = m.reshape(N_CHUNK, 8, SSM_STATE, 8, SSM_GROUP)
    d = jnp.stack([x[:, g, :, g, :] for g in range(8)], axis=1)
    return jnp.swapaxes(d, 2, 3).reshape(SSM_GROUPS, SSM_GROUP, SSM_STATE)


def _lambda_tables(lr, li, reverse):
    p1 = (lr, li)
    p2 = _cmul(*p1, *p1)
    p4 = _cmul(*p2, *p2)
    rows = [p1]
    for _ in range(SUB - 1):
        rows.append(_cmul(*rows[-1], *p1))
    if reverse:
        rows = rows[::-1]
    return p1, p2, p4, (jnp.concatenate([r[0] for r in rows], axis=0), jnp.concatenate([r[1] for r in rows], axis=0))


def _scan8(xr, xi, pows, table, cr, ci, reverse):
    row = lax.broadcasted_iota(jnp.int32, xr.shape, 0)
    for d, (pr, pi) in zip((1, 2, 4), pows):
        if reverse:
            sr, si = pltpu.roll(xr, SUB - d, 0), pltpu.roll(xi, SUB - d, 0)
            keep = row < SUB - d
        else:
            sr, si = pltpu.roll(xr, d, 0), pltpu.roll(xi, d, 0)
            keep = row >= d
        sr = jnp.where(keep, sr, 0.0)
        si = jnp.where(keep, si, 0.0)
        xr, xi = xr + pr * sr - pi * si, xi + pr * si + pi * sr
    tr, ti = table
    return xr + tr * cr - ti * ci, xi + tr * ci + ti * cr


def _gelu_and_grad(y):
    k0 = math.sqrt(2.0 / math.pi)
    inner = k0 * (y + 0.044715 * y * y * y)
    th = jnp.tanh(inner)
    g = 0.5 * y * (1.0 + th)
    dg = 0.5 * (1.0 + th) + 0.5 * y * (1.0 - th * th) * k0 * (1.0 + 3.0 * 0.044715 * y * y)
    return g, dg


SCAN_TILE = TOKEN_TILE
SEG = SCAN_TILE // SUB
SCAN_LANES = 512


def _perm_matrix(to_segments):
    a = lax.broadcasted_iota(jnp.int32, (SCAN_TILE, SCAN_TILE), 0)
    b = lax.broadcasted_iota(jnp.int32, (SCAN_TILE, SCAN_TILE), 1)
    rho, time = (a, b) if to_segments else (b, a)
    return (time == (rho % SUB) * SEG + rho // SUB).astype(BF16)


def _chunked_spec(rows, block_of):
    return pl.BlockSpec((N_CHUNK, rows, U_CHUNK), lambda i: (0, block_of(i), 0))


def _load_segments(src_ref, dst_ref):
    for j in range(N_CHUNK):
        for r in range(SEG):
            dst_ref[r * SUB:(r + 1) * SUB, j * U_CHUNK:(j + 1) * U_CHUNK] = src_ref.at[j][pl.ds(r, SUB, stride=SEG), :]


def _power_table(lr, li, pr_ref, pi_ref):
    cur = (lr, li)
    for r in range(SEG):
        pr_ref[r * SUB:(r + 1) * SUB, :] = jnp.broadcast_to(cur[0], (SUB, STATE_WIDTH))
        pi_ref[r * SUB:(r + 1) * SUB, :] = jnp.broadcast_to(cur[1], (SUB, STATE_WIDTH))
        cur = _cmul(*cur, lr, li)


def _table_rows(ref, k, lanes):
    return ref[pl.ds(pl.multiple_of(k * SUB, SUB), SUB), lanes]


def _segment_scan(xr_ref, xi_ref, lanes, lam, table_row, cr_ref, ci_ref, reverse, extra=None):
    lr = jnp.broadcast_to(lam[0], (SUB, SCAN_LANES))
    li = jnp.broadcast_to(lam[1], (SUB, SCAN_LANES))
    row = lax.broadcasted_iota(jnp.int32, (SUB, SCAN_LANES), 0)

    def rows_of(k):
        r = SEG - 1 - k if reverse else k
        return pl.ds(pl.multiple_of(r * SUB, SUB), SUB)

    def first(k, st):
        sr, si = st
        rows = rows_of(k)
        nr = lr * sr - li * si + xr_ref[rows, lanes]
        ni = lr * si + li * sr + xi_ref[rows, lanes]
        xr_ref[rows, lanes] = nr
        xi_ref[rows, lanes] = ni
        return nr, ni

    zero = jnp.zeros((SUB, SCAN_LANES), F32)
    er, ei = lax.fori_loop(0, SEG, first, (zero, zero))
    l16 = table_row(SEG - 1)
    q1, q2, q4, tab = _lambda_tables(l16[0][0:1], l16[1][0:1], reverse)
    c_r, c_i = cr_ref[:, lanes], ci_ref[:, lanes]
    gr, gi = _scan8(er, ei, (q1, q2, q4), tab, c_r, c_i, reverse)
    if reverse:
        cin_r = jnp.where(row == SUB - 1, c_r, pltpu.roll(gr, SUB - 1, 0))
        cin_i = jnp.where(row == SUB - 1, c_i, pltpu.roll(gi, SUB - 1, 0))
        cr_ref[:, lanes] = gr[0:1]
        ci_ref[:, lanes] = gi[0:1]
    else:
        cin_r = jnp.where(row == 0, c_r, pltpu.roll(gr, 1, 0))
        cin_i = jnp.where(row == 0, c_i, pltpu.roll(gi, 1, 0))
        cr_ref[:, lanes] = gr[SUB - 1:SUB]
        ci_ref[:, lanes] = gi[SUB - 1:SUB]

    def second(k, carry):
        rows = rows_of(k)
        tr, ti = table_row(k)
        ar = xr_ref[rows, lanes] + tr * cin_r - ti * cin_i
        ai = xi_ref[rows, lanes] + tr * cin_i + ti * cin_r
        xr_ref[rows, lanes] = ar
        xi_ref[rows, lanes] = ai
        if extra is None:
            return carry
        return extra(rows, carry, ar, ai)

    init = 0 if extra is None else (cin_r, cin_i, zero, zero)
    return lax.fori_loop(0, SEG, second, init)


def ssm_fwd(u, lam_re, lam_im, bb_re, bb_im, cc_re, cc_im, d_skip, name):
    t = u.shape[1]
    tt = SCAN_TILE

    def body(u_ref, lr_ref, li_ref, bbr_ref, bbi_ref, ccr_ref, cci_ref, d_ref, yg_ref, hr_ref, hi_ref,
             cr_ref, ci_ref, pr_ref, pi_ref, up_ref, y_ref):
        @pl.when(pl.program_id(0) == 0)
        def _():
            cr_ref[...] = jnp.zeros_like(cr_ref)
            ci_ref[...] = jnp.zeros_like(ci_ref)
            _power_table(lr_ref[...], li_ref[...], pr_ref, pi_ref)

        _load_segments(u_ref, up_ref)
        ub = up_ref[...].astype(BF16)
        for j in range(N_CHUNK):
            hs = slice(j * H_CHUNK, (j + 1) * H_CHUNK)
            us = slice(j * U_CHUNK, (j + 1) * U_CHUNK)
            hr_ref[:, hs] = _nn(ub[:, us], bbr_ref[j])
            hi_ref[:, hs] = _nn(ub[:, us], bbi_ref[j])
        for c in range(STATE_WIDTH // SCAN_LANES):
            lanes = slice(c * SCAN_LANES, (c + 1) * SCAN_LANES)
            _segment_scan(hr_ref, hi_ref, lanes, (lr_ref[:, lanes], li_ref[:, lanes]),
                          lambda k, lanes=lanes: (_table_rows(pr_ref, k, lanes), _table_rows(pi_ref, k, lanes)),
                          cr_ref, ci_ref, False)
        for j in range(N_CHUNK):
            hs = slice(j * H_CHUNK, (j + 1) * H_CHUNK)
            us = slice(j * U_CHUNK, (j + 1) * U_CHUNK)
            y = (_nn(hr_ref[:, hs].astype(BF16), ccr_ref[j]) - _nn(hi_ref[:, hs].astype(BF16), cci_ref[j])
                 + d_ref[:, us] * up_ref[:, us])
            y_ref[:, us] = _gelu_and_grad(y)[0]
        yg_ref[...] = _nn(_perm_matrix(False), y_ref[...].astype(BF16)).astype(BF16)

    return pl.pallas_call(
        body, name=name, grid=(t // tt,),
        in_specs=[_chunked_spec(tt, lambda i: i), _VMEM, _VMEM, _VMEM, _VMEM, _VMEM, _VMEM, _VMEM],
        out_specs=[_row_spec(tt, SSM_WIDTH), _row_spec(tt, STATE_WIDTH), _row_spec(tt, STATE_WIDTH)],
        out_shape=[jax.ShapeDtypeStruct((t, SSM_WIDTH), BF16), jax.ShapeDtypeStruct((t, STATE_WIDTH), F32),
                   jax.ShapeDtypeStruct((t, STATE_WIDTH), F32)],
        scratch_shapes=[pltpu.VMEM((1, STATE_WIDTH), F32), pltpu.VMEM((1, STATE_WIDTH), F32),
                        pltpu.VMEM((SCAN_TILE, STATE_WIDTH), F32), pltpu.VMEM((SCAN_TILE, STATE_WIDTH), F32),
                        pltpu.VMEM((tt, SSM_WIDTH), F32), pltpu.VMEM((tt, SSM_WIDTH), F32)],
        compiler_params=_params(("arbitrary",)),
    )(u, lam_re, lam_im, bb_re, bb_im, cc_re, cc_im, d_skip)


def ssm_bwd(dyg, u, h_re, h_im, lam_re, lam_im, bb_re, bb_im, cc_re, cc_im, d_skip, name):
    t = u.shape[1]
    tt = SCAN_TILE
    nt = t // tt

    def body(dyg_ref, u_ref, hr_ref, hi_ref, lr_ref, li_ref, bbr_ref, bbi_ref, ccr_ref, cci_ref, d_ref,
             du_ref, dlr_ref, dli_ref, dbbr_ref, dbbi_ref, dccr_ref, dcci_ref, dd_ref,
             ar_ref, ai_ref, cr_ref, ci_ref, pr_ref, pi_ref, up_ref, dy_ref, dup_ref):
        step = pl.program_id(0)
        tile = nt - 1 - step

        @pl.when(step == 0)
        def _():
            for ref in (cr_ref, ci_ref, dlr_ref, dli_ref, dbbr_ref, dbbi_ref, dccr_ref, dcci_ref, dd_ref):
                ref[...] = jnp.zeros_like(ref)
            _power_table(lr_ref[...], li_ref[...], pr_ref, pi_ref)

        _load_segments(u_ref, up_ref)
        _load_segments(dyg_ref, dy_ref)
        uv = up_ref[...]
        ub = uv.astype(BF16)
        dskip = d_ref[...]
        for j in range(N_CHUNK):
            hs = slice(j * H_CHUNK, (j + 1) * H_CHUNK)
            us = slice(j * U_CHUNK, (j + 1) * U_CHUNK)
            hrb = hr_ref[:, hs].astype(BF16)
            hib = hi_ref[:, hs].astype(BF16)
            y = _nn(hrb, ccr_ref[j]) - _nn(hib, cci_ref[j]) + dskip[:, us] * uv[:, us]
            dy = dy_ref[:, us] * _gelu_and_grad(y)[1]
            dy_ref[:, us] = dy
            dyb = dy.astype(BF16)
            dccr_ref[j] += _tn(hrb, dyb)
            dcci_ref[j] -= _tn(hib, dyb)
            ar_ref[:, hs] = _nt(dyb, ccr_ref[j])
            ai_ref[:, hs] = -_nt(dyb, cci_ref[j])
        dd_ref[...] += jnp.sum(dy_ref[...] * uv, axis=0, keepdims=True)

        for c in range(STATE_WIDTH // SCAN_LANES):
            lanes = slice(c * SCAN_LANES, (c + 1) * SCAN_LANES)

            def dlambda(rows, carry, ar, ai, lanes=lanes):
                nr, ni, accr, acci = carry
                hr, hi = hr_ref[rows, lanes], hi_ref[rows, lanes]
                return ar, ai, accr + nr * hr + ni * hi, acci + ni * hr - nr * hi

            _, _, accr, acci = _segment_scan(
                ar_ref, ai_ref, lanes, (lr_ref[:, lanes], -li_ref[:, lanes]),
                lambda k, lanes=lanes: (_table_rows(pr_ref, k, lanes), -_table_rows(pi_ref, k, lanes)),
                cr_ref, ci_ref, True, dlambda)
            dlr_ref[:, lanes] += accr
            dli_ref[:, lanes] += acci

        rho = lax.broadcasted_iota(jnp.int32, (tt, U_CHUNK), 0)
        time = tile * tt + (rho % SUB) * SEG + rho // SUB
        for j in range(N_CHUNK):
            hs = slice(j * H_CHUNK, (j + 1) * H_CHUNK)
            us = slice(j * U_CHUNK, (j + 1) * U_CHUNK)
            arb = ar_ref[:, hs].astype(BF16)
            aib = ai_ref[:, hs].astype(BF16)
            dbbr_ref[j] += _tn(ub[:, us], arb)
            dbbi_ref[j] += _tn(ub[:, us], aib)
            du = _nt(arb, bbr_ref[j]) + _nt(aib, bbi_ref[j]) + dy_ref[:, us] * dskip[:, us]
            dup_ref[:, us] = jnp.where(time >= PAD_FRONT, du, 0.0)
        du_ref[...] = _nn(_perm_matrix(False), dup_ref[...].astype(BF16)).astype(BF16)

    rev = lambda i: (nt - 1 - i, 0)
    full = lambda shape: pl.BlockSpec(shape, lambda i: (0,) * len(shape))
    return pl.pallas_call(
        body, name=name, grid=(nt,),
        in_specs=[_chunked_spec(tt, lambda i: nt - 1 - i), _chunked_spec(tt, lambda i: nt - 1 - i),
                  pl.BlockSpec((tt, STATE_WIDTH), rev), pl.BlockSpec((tt, STATE_WIDTH), rev),
                  _VMEM, _VMEM, _VMEM, _VMEM, _VMEM, _VMEM, _VMEM],
        out_specs=[pl.BlockSpec((tt, SSM_WIDTH), rev), full((SUB, STATE_WIDTH)), full((SUB, STATE_WIDTH)),
                   full((N_CHUNK, U_CHUNK, H_CHUNK)), full((N_CHUNK, U_CHUNK, H_CHUNK)),
                   full((N_CHUNK, H_CHUNK, U_CHUNK)), full((N_CHUNK, H_CHUNK, U_CHUNK)), full((1, SSM_WIDTH))],
        out_shape=[jax.ShapeDtypeStruct((t, SSM_WIDTH), BF16),
                   jax.ShapeDtypeStruct((SUB, STATE_WIDTH), F32), jax.ShapeDtypeStruct((SUB, STATE_WIDTH), F32),
                   jax.ShapeDtypeStruct((N_CHUNK, U_CHUNK, H_CHUNK), F32),
                   jax.ShapeDtypeStruct((N_CHUNK, U_CHUNK, H_CHUNK), F32),
                   jax.ShapeDtypeStruct((N_CHUNK, H_CHUNK, U_CHUNK), F32),
                   jax.ShapeDtypeStruct((N_CHUNK, H_CHUNK, U_CHUNK), F32),
                   jax.ShapeDtypeStruct((1, SSM_WIDTH), F32)],
        scratch_shapes=[pltpu.VMEM((tt, STATE_WIDTH), F32), pltpu.VMEM((tt, STATE_WIDTH), F32),
                        pltpu.VMEM((1, STATE_WIDTH), F32), pltpu.VMEM((1, STATE_WIDTH), F32),
                        pltpu.VMEM((SCAN_TILE, STATE_WIDTH), F32), pltpu.VMEM((SCAN_TILE, STATE_WIDTH), F32),
                        pltpu.VMEM((tt, SSM_WIDTH), F32), pltpu.VMEM((tt, SSM_WIDTH), F32),
                        pltpu.VMEM((tt, SSM_WIDTH), F32)],
        compiler_params=_params(("arbitrary",)),
    )(dyg, u, h_re, h_im, lam_re, lam_im, bb_re, bb_im, cc_re, cc_im, d_skip)


def merge_fwd(h, o, yg, gates, wap_t, wv_t, wgg_t, wout, name):
    t, d = h.shape
    tm = TOKEN_TILE

    def body(h_ref, o_ref, yg_ref, gt_ref, wap_ref, wv_ref, wgg_ref, wout_ref, ho_ref, mg_ref, a_ref, sv_ref, sg_ref):
        att = _nt(o_ref[...], wap_ref[...])
        ygv = yg_ref[...]
        sv = _nt(ygv, wv_ref[...])
        sg = _nt(ygv, wgg_ref[...])
        a_ref[...] = att.astype(BF16)
        sv_ref[...] = sv.astype(BF16)
        sg_ref[...] = sg.astype(BF16)
        merged = (jax.nn.sigmoid(gt_ref[:, 0:d].astype(F32)) * att
                  + jax.nn.sigmoid(gt_ref[:, d:2 * d].astype(F32)) * (sv * jax.nn.sigmoid(sg))).astype(BF16)
        mg_ref[...] = merged
        ho_ref[...] = h_ref[...] + _nn(merged, wout_ref[...])

    return pl.pallas_call(
        body, name=name, grid=(t // tm,),
        in_specs=[_row_spec(tm, d), _row_spec(tm, ATTN_WIDTH), _row_spec(tm, SSM_WIDTH), _row_spec(tm, 2 * d),
                  _VMEM, _VMEM, _VMEM, _VMEM],
        out_specs=[_row_spec(tm, d), _row_spec(tm, d), _row_spec(tm, d), _row_spec(tm, d), _row_spec(tm, d)],
        out_shape=[jax.ShapeDtypeStruct((t, d), F32), jax.ShapeDtypeStruct((t, d), BF16),
                   jax.ShapeDtypeStruct((t, d), BF16), jax.ShapeDtypeStruct((t, d), BF16),
                   jax.ShapeDtypeStruct((t, d), BF16)],
        compiler_params=_params(("arbitrary",)),
    )(h, o, yg, gates, wap_t, wv_t, wgg_t, wout)


def merge_bwd(dh, gates, att, sv, sg, wap_t, wv_t, wgg_t, wout, dep, name):
    t, d = dh.shape
    tm = TOKEN_TILE

    def body(dh_ref, gt_ref, a_ref, sv_ref, sg_ref, wap_ref, wv_ref, wgg_ref, wout_ref, dep_ref,
             dgt_ref, da_ref, dsv_ref, dsg_ref, do_ref, dyg_ref, dhb_ref):
        dhb = dh_ref[...].astype(BF16)
        dhb_ref[...] = dhb
        dm = _nt(dhb, wout_ref[...])
        sig_a = jax.nn.sigmoid(gt_ref[:, 0:d].astype(F32))
        sig_s = jax.nn.sigmoid(gt_ref[:, d:2 * d].astype(F32))
        sig_g = jax.nn.sigmoid(sg_ref[...].astype(F32))
        svv = sv_ref[...].astype(F32)
        dgt_ref[:, 0:d] = (dm * a_ref[...].astype(F32) * sig_a * (1.0 - sig_a)).astype(BF16)
        dgt_ref[:, d:2 * d] = (dm * (svv * sig_g) * sig_s * (1.0 - sig_s)).astype(BF16)
        da = (dm * sig_a).astype(BF16)
        d_s = dm * sig_s
        dsv = (d_s * sig_g).astype(BF16)
        dsg = (d_s * svv * sig_g * (1.0 - sig_g)).astype(BF16)
        da_ref[...] = da
        dsv_ref[...] = dsv
        dsg_ref[...] = dsg
        do_ref[...] = _nn(da, wap_ref[...]).astype(BF16)
        dyg = _nn(dsv, wv_ref[...]) + _nn(dsg, wgg_ref[...])
        for j in range(N_CHUNK):
            dyg_ref[j] = dyg[:, j * U_CHUNK:(j + 1) * U_CHUNK]

    return pl.pallas_call(
        body, name=name, grid=(t // tm,),
        in_specs=[_row_spec(tm, d), _row_spec(tm, 2 * d), _row_spec(tm, d), _row_spec(tm, d), _row_spec(tm, d),
                  _VMEM, _VMEM, _VMEM, _VMEM, _ANY],
        out_specs=[_row_spec(tm, 2 * d), _row_spec(tm, d), _row_spec(tm, d), _row_spec(tm, d),
                   _row_spec(tm, ATTN_WIDTH), _chunked_spec(tm, lambda i: i), _row_spec(tm, d)],
        out_shape=[jax.ShapeDtypeStruct((t, 2 * d), BF16), jax.ShapeDtypeStruct((t, d), BF16),
                   jax.ShapeDtypeStruct((t, d), BF16), jax.ShapeDtypeStruct((t, d), BF16),
                   jax.ShapeDtypeStruct((t, ATTN_WIDTH), BF16), jax.ShapeDtypeStruct((N_CHUNK, t, U_CHUNK), F32),
                   jax.ShapeDtypeStruct((t, d), BF16)],
        compiler_params=_params(("arbitrary",)),
    )(dh, gates, att, sv, sg, wap_t, wv_t, wgg_t, wout, dep)


def _adamw_math(w, g, m, v):
    mn = ADAM_B1 * m + (1.0 - ADAM_B1) * g
    vn = ADAM_B2 * v + (1.0 - ADAM_B2) * (g * g)
    m_hat = mn / (1.0 - ADAM_B1 ** ADAM_STEP)
    v_hat = vn / (1.0 - ADAM_B2 ** ADAM_STEP)
    return -ADAM_LR * (m_hat / (jnp.sqrt(v_hat) + ADAM_EPS) + ADAM_WD * w), mn, vn


def sum_adamw_layer(me, landed, partial, w, m, v, layer, prev, name, transposed=False):
    _, rows, cols = landed.shape
    tr = rows // 2 if rows % 32 == 0 and not transposed else rows
    steps = rows // tr

    def body(me_ref, land_ref, own_ref, w_ref, m_ref, v_ref, *rest):
        go_ref, d_ref, mo_ref, vo_ref = rest[-4:]
        who = me_ref[0]
        gv = land_ref[who ^ 1].astype(F32)
        for p in range(2, N_DEV):
            gv = gv + land_ref[who ^ p].astype(F32)
        gv = gv + own_ref[...].astype(F32)
        if transposed:
            gv = gv.T
        go_ref[0] = gv
        d_ref[0], mo_ref[0], vo_ref[0] = _adamw_math(w_ref[0], gv, m_ref[0], v_ref[0])

    if transposed:
        spec3 = pl.BlockSpec((1, cols, rows), lambda r, me_ref: (layer, 0, 0))
    else:
        spec3 = pl.BlockSpec((1, tr, cols), lambda r, me_ref: (layer, r, 0))
    out = jax.ShapeDtypeStruct(w.shape, F32)
    extra = [] if prev is None else list(prev)
    grid_spec = pltpu.PrefetchScalarGridSpec(
        num_scalar_prefetch=1, grid=(steps,),
        in_specs=[pl.BlockSpec((N_DEV, tr, cols), lambda r, me_ref: (0, r, 0)),
                  pl.BlockSpec((tr, cols), lambda r, me_ref: (me_ref[0] * steps + r, 0)),
                  spec3, spec3, spec3] + [_ANY] * len(extra),
        out_specs=[spec3] * 4)
    return pl.pallas_call(
        body, name=name, grid_spec=grid_spec, out_shape=[out] * 4,
        input_output_aliases={6 + j: j for j in range(len(extra))},
        compiler_params=_params(("arbitrary",)),
    )(me, landed, partial, w, m, v, *extra)


def adamw_small(params, name):
    def as2d(a, swap):
        a = jnp.swapaxes(a, -1, -2) if swap else a
        return a.reshape(-1, a.shape[-1]) if a.ndim >= 2 else a.reshape(1, -1)

    n = len(params)
    flat = [as2d(a, swap) for w, g, m, v, swap in params for a in (w, g, m, v)]

    def body(*refs):
        ins, outs = refs[:4 * n], refs[4 * n:]
        for k in range(n):
            w_ref, g_ref, m_ref, v_ref = ins[4 * k:4 * k + 4]
            outs[3 * k][...], outs[3 * k + 1][...], outs[3 * k + 2][...] = _adamw_math(
                w_ref[...], g_ref[...], m_ref[...], v_ref[...])

    outs = pl.pallas_call(
        body, name=name, in_specs=[_VMEM] * (4 * n), out_specs=[_VMEM] * (3 * n),
        out_shape=[jax.ShapeDtypeStruct(flat[4 * k].shape, F32) for k in range(n) for _ in range(3)],
        compiler_params=_params(),
    )(*flat)

    def restore(a, like, swap):
        shape = jnp.swapaxes(like, -1, -2).shape if swap else like.shape
        a = a.reshape(shape)
        return jnp.swapaxes(a, -1, -2) if swap else a

    return [tuple(restore(outs[3 * k + j], params[k][0], params[k][4]) for j in range(3)) for k in range(n)]


def _my_index():
    return 4 * lax.axis_index("x") + 2 * lax.axis_index("y") + lax.axis_index("c")


def _peer(p):
    return (lax.axis_index("x") ^ ((p >> 2) & 1), lax.axis_index("y") ^ ((p >> 1) & 1), lax.axis_index("c") ^ (p & 1))


_HBM = pl.BlockSpec(memory_space=pltpu.HBM)
_SEM = pl.BlockSpec(memory_space=pltpu.SEMAPHORE)
_EFFECT = pltpu.SideEffectType.DATAFLOW_SIDE_EFFECTING


class Exchange:
    RELAYED = (2, 4, 6)

    def __init__(self, srcs, scatter, name, relay=False):
        self.n = n = len(srcs)
        self.scatter = scatter
        self.name = name
        self.relayed = relay
        assert not (relay and scatter)
        self.direct = (1,) + self.RELAYED if relay else tuple(range(1, N_DEV))
        widths = sorted({s.shape[1] for s in srcs}, reverse=True)
        self.ncls = len(widths)
        self.cls = [widths.index(s.shape[1]) for s in srcs]
        self.cnts = [s.shape[0] // N_DEV if scatter else s.shape[0] for s in srcs]
        self.totals = [sum(c for c, k in zip(self.cnts, self.cls) if k == w) for w in range(self.ncls)]
        self.sizer = [max((k for k in range(n) if self.cls[k] == w), key=lambda k: self.cnts[k])
                      for w in range(self.ncls)]
        assert all(N_DEV * self.cnts[self.sizer[w]] >= self.totals[w] for w in range(self.ncls))
        if scatter:
            self.land_shapes = [(N_DEV, c, s.shape[1]) for s, c in zip(srcs, self.cnts)]
        else:
            self.land_shapes = [(N_DEV * c, s.shape[1]) for s, c in zip(srcs, self.cnts)]
        self.dtypes = [s.dtype for s in srcs]

    def _block(self, k, who):
        return pl.ds(pl.multiple_of(who * self.cnts[k], 16), self.cnts[k])

    def _sem(self, p, w):
        return (p - 1) * self.ncls + w

    def start(self, srcs, after):
        n = self.n

        def body(*refs):
            src, land = refs[:n], refs[n:2 * n]
            send_sems, recv_sems = refs[2 * n + 1], refs[2 * n + 2]
            token = refs[-1]
            me = _my_index()
            for p in self.direct:
                for k in range(n):
                    if self.scatter:
                        s_ref, d_ref = src[k].at[self._block(k, me ^ p), :], land[k].at[me]
                    else:
                        s_ref, d_ref = src[k], land[k].at[self._block(k, me), :]
                    pltpu.make_async_remote_copy(
                        src_ref=s_ref, dst_ref=d_ref, send_sem=send_sems.at[self._sem(p, self.cls[k])],
                        recv_sem=recv_sems.at[self._sem(p, self.cls[k])], device_id=_peer(p),
                        device_id_type=MESH).start()
            token[...] = jnp.zeros_like(token)

        sems = pltpu.SemaphoreType.DMA(((N_DEV - 1) * self.ncls,))
        thru = [pltpu.HBM(s.shape, s.dtype) for s in srcs] + [pltpu.HBM(shp, dt) for shp, dt in
                                                               zip(self.land_shapes, self.dtypes)]
        lands = [pltpu.with_memory_space_constraint(lax.empty(shp, dt), pltpu.HBM)
                 for shp, dt in zip(self.land_shapes, self.dtypes)]
        out = pl.pallas_call(
            body, name=self.name + "_start",
            in_specs=[_HBM] * (2 * n) + [_ANY],
            out_shape=[sems, sems] + thru + [jax.ShapeDtypeStruct((8, 128), F32)],
            out_specs=[_SEM, _SEM] + [_HBM] * (2 * n) + [_VMEM],
            input_output_aliases={j: 2 + j for j in range(2 * n)},
            compiler_params=pltpu.CompilerParams(has_side_effects=_EFFECT),
        )(*[pltpu.with_memory_space_constraint(s, pltpu.HBM) for s in srcs], *lands, after)
        return out[:-1], out[-1]

    def _span_copy(self, src, land, w, send_sem, recv_sem, p):
        big = src[self.sizer[w]] if self.scatter else land[self.sizer[w]]
        span = big.at[pl.ds(0, self.totals[w]), :]
        return pltpu.make_async_remote_copy(src_ref=span, dst_ref=span, send_sem=send_sem, recv_sem=recv_sem,
                                            device_id=_peer(p), device_id_type=MESH)

    def relay(self, state, after):
        n = self.n
        send_sems, recv_sems = state[0], state[1]
        thru = state[2:]
        after = list(after) if isinstance(after, (list, tuple)) else [after]
        first_out = 2 * n + 2 + len(after)

        def body(*refs):
            land = refs[n:2 * n]
            send_a, recv_a = refs[2 * n], refs[2 * n + 1]
            send_b, recv_b = refs[first_out], refs[first_out + 1]
            refs[-1][...] = jnp.zeros_like(refs[-1])
            me = _my_index()
            for p in self.RELAYED:
                for w in range(self.ncls):
                    self._span_copy(None, land, w, send_a.at[self._sem(p, w)], recv_a.at[self._sem(p, w)], p).wait_recv()
            for j, p in enumerate(self.RELAYED):
                for k in range(n):
                    rows = land[k].at[self._block(k, me ^ p), :]
                    pltpu.make_async_remote_copy(
                        src_ref=rows, dst_ref=rows, send_sem=send_b.at[j * self.ncls + self.cls[k]],
                        recv_sem=recv_b.at[j * self.ncls + self.cls[k]], device_id=_peer(1),
                        device_id_type=MESH).start()

        sems = pltpu.SemaphoreType.DMA((len(self.RELAYED) * self.ncls,))
        out = pl.pallas_call(
            body, name=self.name + "_relay",
            in_specs=[_HBM] * (2 * n) + [_SEM, _SEM] + [_ANY] * len(after),
            out_shape=[sems, sems] + [pltpu.HBM(a.shape, a.dtype) for a in thru] + [jax.ShapeDtypeStruct((8, 128), F32)],
            out_specs=[_SEM, _SEM] + [_HBM] * (2 * n) + [_VMEM],
            input_output_aliases={j: 2 + j for j in range(2 * n)},
            compiler_params=pltpu.CompilerParams(has_side_effects=_EFFECT),
        )(*thru, send_sems, recv_sems, *after)
        return [send_sems, recv_sems] + list(out[2:-1]) + [out[0], out[1]], out[-1]

    def wait(self, state, after):
        n = self.n
        send_sems, recv_sems = state[0], state[1]
        thru = state[2:2 + 2 * n]
        relay_sems = list(state[2 + 2 * n:])
        assert len(relay_sems) == (2 if self.relayed else 0)
        after = list(after) if isinstance(after, (list, tuple)) else [after]

        def body(*refs):
            src, land = refs[:n], refs[n:2 * n]
            send_a, recv_a = refs[2 * n], refs[2 * n + 1]
            for p in self.direct:
                for w in range(self.ncls):
                    copy = self._span_copy(src, land, w, send_a.at[self._sem(p, w)], recv_a.at[self._sem(p, w)], p)
                    copy.wait_send()
                    if not (self.relayed and p in self.RELAYED):
                        copy.wait_recv()
            if self.relayed:
                send_b, recv_b = refs[2 * n + 2], refs[2 * n + 3]
                for j in range(len(self.RELAYED)):
                    for w in range(self.ncls):
                        copy = self._span_copy(src, land, w, send_b.at[j * self.ncls + w],
                                               recv_b.at[j * self.ncls + w], 1)
                        copy.wait_send()
                        copy.wait_recv()

        out = pl.pallas_call(
            body, name=self.name + "_wait",
            in_specs=[_HBM] * (2 * n) + [_SEM] * (2 + len(relay_sems)) + [_ANY] * len(after),
            out_shape=[pltpu.HBM(a.shape, a.dtype) for a in thru], out_specs=[_HBM] * (2 * n),
            input_output_aliases={j: j for j in range(2 * n)},
            compiler_params=pltpu.CompilerParams(has_side_effects=_EFFECT),
        )(*thru, send_sems, recv_sems, *relay_sems, *after)
        return out[:n], out[n:]

    def place(self, lands, srcs):
        n = self.n
        assert not self.scatter

        def body(*refs):
            src, land = refs[n:2 * n], refs[2 * n:3 * n]
            bufs, sems = refs[3 * n:4 * n], refs[-1]
            me = _my_index()
            loads = [pltpu.make_async_copy(src[k], bufs[k], sems.at[k]) for k in range(n)]
            stores = [pltpu.make_async_copy(bufs[k], land[k].at[self._block(k, me), :], sems.at[k]) for k in range(n)]
            for cp in loads:
                cp.start()
            for k in range(n):
                loads[k].wait()
                stores[k].start()
            for cp in stores:
                cp.wait()

        return pl.pallas_call(
            body, name=self.name + "_place", in_specs=[_ANY] * (2 * n), out_specs=[_ANY] * n,
            out_shape=[jax.ShapeDtypeStruct(a.shape, a.dtype) for a in lands],
            input_output_aliases={j: j for j in range(n)},
            scratch_shapes=[pltpu.VMEM(s.shape, s.dtype) for s in srcs] + [pltpu.SemaphoreType.DMA((n,))],
        )(*lands, *srcs)


def sum_slots(slots, name):
    _, rows, cols = slots.shape
    tr = rows
    if rows > 512:
        for cand in (256, 128, 64, 32, 16, 8):
            if rows % cand == 0:
                tr = cand
                break

    def body(s_ref, o_ref):
        acc = s_ref[0].astype(F32)
        for j in range(1, N_DEV):
            acc = acc + s_ref[j].astype(F32)
        o_ref[...] = acc

    return pl.pallas_call(
        body, name=name, grid=(rows // tr,),
        in_specs=[pl.BlockSpec((N_DEV, tr, cols), lambda i: (0, i, 0))], out_specs=_row_spec(tr, cols),
        out_shape=jax.ShapeDtypeStruct((rows, cols), F32), compiler_params=_params(("arbitrary",)),
    )(slots)


BIG_N = ("ffn1_w_down", "w_out", "ffn2_w_down")
SMALL = ("ffn1_norm", "mix_norm", "attn_sinks", "ssm_a_re", "ssm_a_im", "ssm_log_dt", "ssm_b_re", "ssm_b_im",
         "ssm_c_re", "ssm_c_im", "ssm_d", "ffn2_norm", "final_norm")
PARTS = {"ffn1": ("ffn1_w_gate", "ffn1_w_up", "ffn1_w_down"),
         "mix": ("w_in", "w_out", "w_attn_proj", "w_glu_v", "w_glu_g"),
         "ffn2": ("ffn2_w_gate", "ffn2_w_up", "ffn2_w_down")}


def _to_rows(name, a):
    return a if name in BIG_N else jnp.swapaxes(a, -1, -2)


def local_step(x, tgt, get_weights, put_grads, small):
    seq, d = x.shape
    t = PAD_FRONT + N_META + seq
    cos_t, sin_t = rope_tables(t)
    row = lambda a: a.reshape(1, -1)
    tables = []
    for i in range(DEPTH):
        b_re_t = jnp.swapaxes(small["ssm_b_re"][i], 1, 2)
        b_im_t = jnp.swapaxes(small["ssm_b_im"][i], 1, 2)
        lam_re, lam_im, bbar_re, bbar_im = ssm_prep(small["ssm_a_re"][i], small["ssm_a_im"][i],
                                                    small["ssm_log_dt"][i].reshape(-1, 1), b_re_t, b_im_t, f"ssm_prep_{i}")
        tables.append(((b_re_t, b_im_t),
                       (row(lam_re), row(lam_im), _block_diag_b(bbar_re).astype(BF16), _block_diag_b(bbar_im).astype(BF16),
                        _block_diag_c(small["ssm_c_re"][i]).astype(BF16), _block_diag_c(small["ssm_c_im"][i]).astype(BF16),
                        row(small["ssm_d"][i]))))
    early = [cos_t, sin_t] + [a for _, tab in tables for a in tab[2:6]]
    saved = []
    h = None
    for i in range(DEPTH):
        s = {}
        w = dict(get_weights(i, "ffn1", early if i == 0 else h))
        if i == 0:
            head = jnp.concatenate([jnp.zeros((PAD_FRONT, d), F32), w["meta_tokens"]], axis=0)
            s["h0"], h, s["n1"], s["acts1"] = ffn_fwd(None, row(small["ffn1_norm"][i]), w["ffn1_w_gate"],
                                                      w["ffn1_w_up"], w["ffn1_w_down"], f"ffn1_fwd_{i}", first=(x, head))
        else:
            s["h0"] = h
            h, s["n1"], s["acts1"] = ffn_fwd(h, row(small["ffn1_norm"][i]), w["ffn1_w_gate"], w["ffn1_w_up"],
                                             w["ffn1_w_down"], f"ffn1_fwd_{i}")
        s["h1"] = h
        w.update(get_weights(i, "mix", h))
        s["n2"], s["qkv"], s["u"], s["gates"] = win_fwd(h, row(small["mix_norm"][i]), w["w_in"], cos_t, sin_t,
                                                        f"win_fwd_{i}")
        s["b_t"], s["ssm"] = tables[i]
        s["yg"], s["h_re"], s["h_im"] = ssm_fwd(s["u"], *s["ssm"], f"ssm_fwd_{i}")
        s["o"] = attn_fwd(s["qkv"], row(small["attn_sinks"][i]), f"attn_fwd_{i}")
        h, s["merged"], s["att"], s["sv"], s["sg"] = merge_fwd(
            h, s["o"], s["yg"], s["gates"], w["w_attn_proj"], w["w_glu_v"], w["w_glu_g"], w["w_out"],
            f"merge_fwd_{i}")
        s["h2"] = h
        w.update(get_weights(i, "ffn2", h))
        h, s["n3"], s["acts3"] = ffn_fwd(h, row(small["ffn2_norm"][i]), w["ffn2_w_gate"], w["ffn2_w_up"],
                                               w["ffn2_w_down"], f"ffn2_fwd_{i}")
        s["w"] = w
        saved.append(s)

    loss, dh, d_final = head_fwd_bwd(h, row(small["final_norm"]), tgt)
    gs = {k: [None] * DEPTH for k in SMALL if k != "final_norm"}
    dep = loss
    for i in reversed(range(DEPTH)):
        s = saved[i]
        w = s["w"]
        dh, da, db, sact, dhb, dg = ffn_bwd(dh, s["h2"], row(small["ffn2_norm"][i]), s["acts3"], w["ffn2_w_gate"],
                                            w["ffn2_w_up"], w["ffn2_w_down"], dep, f"ffn2_bwd_{i}")
        gs["ffn2_norm"][i] = dg[0]
        dep = put_grads(i, "ffn2", {"ffn2_w_gate": tn_matmul(da, s["n3"], f"ffn2_dwg_{i}"),
                                    "ffn2_w_up": tn_matmul(db, s["n3"], f"ffn2_dwu_{i}"),
                                    "ffn2_w_down": tn_matmul(sact, dhb, f"ffn2_dwd_{i}")})

        dgates, datt, dsv, dsg, do, dyg, dhb = merge_bwd(dh, s["gates"], s["att"], s["sv"], s["sg"], w["w_attn_proj"],
                                                         w["w_glu_v"], w["w_glu_g"], w["w_out"], dep, f"merge_bwd_{i}")
        gmix = {"w_out": tn_matmul(s["merged"], dhb, f"dwout_{i}"),
                "w_attn_proj": tn_matmul(datt, s["o"], f"dwap_{i}"),
                "w_glu_v": tn_matmul(dsv, s["yg"], f"dwv_{i}"),
                "w_glu_g": tn_matmul(dsg, s["yg"], f"dwgg_{i}")}
        dqkv, dsink = attn_bwd(s["qkv"], do, row(small["attn_sinks"][i]), cos_t, sin_t, f"attn_bwd_{i}")
        gs["attn_sinks"][i] = dsink[:, 0]
        du, dl_re, dl_im, dbb_re, dbb_im, dcc_re, dcc_im, dd = ssm_bwd(dyg, s["u"], s["h_re"], s["h_im"], *s["ssm"],
                                                                      f"ssm_bwd_{i}")
        fold = lambda a: jnp.sum(a, axis=0).reshape(SSM_GROUPS, SSM_STATE)
        da_re, da_im, dldt, db_re_t, db_im_t = ssm_prep_bwd(
            small["ssm_a_re"][i], small["ssm_a_im"][i], small["ssm_log_dt"][i].reshape(-1, 1), *s["b_t"],
            fold(dl_re), fold(dl_im), _diag_of_b(dbb_re), _diag_of_b(dbb_im), f"ssm_prep_bwd_{i}")
        gs["ssm_a_re"][i], gs["ssm_a_im"][i], gs["ssm_log_dt"][i] = da_re, da_im, dldt[:, 0]
        gs["ssm_b_re"][i], gs["ssm_b_im"][i] = jnp.swapaxes(db_re_t, 1, 2), jnp.swapaxes(db_im_t, 1, 2)
        gs["ssm_c_re"][i], gs["ssm_c_im"][i] = _diag_of_c(dcc_re), _diag_of_c(dcc_im)
        gs["ssm_d"][i] = dd[0]
        gmix["w_in"] = tn_matmul([dqkv, du, dgates], s["n2"], f"dwin_{i}")
        dep = put_grads(i, "mix", gmix)
        dh, dg = win_bwd(dh, s["h1"], row(small["mix_norm"][i]), dqkv, du, dgates, w["w_in"], dep, f"win_bwd_{i}")
        gs["mix_norm"][i] = dg[0]

        dh, da, db, sact, dhb, dg = ffn_bwd(dh, s["h0"], row(small["ffn1_norm"][i]), s["acts1"], w["ffn1_w_gate"],
                                            w["ffn1_w_up"], w["ffn1_w_down"], dep, f"ffn1_bwd_{i}")
        gs["ffn1_norm"][i] = dg[0]
        if i > 0:
            dep = put_grads(i, "ffn1", {"ffn1_w_gate": tn_matmul(da, s["n1"], f"ffn1_dwg_{i}"),
                                        "ffn1_w_up": tn_matmul(db, s["n1"], f"ffn1_dwu_{i}"),
                                        "ffn1_w_down": tn_matmul(sact, dhb, f"ffn1_dwd_{i}")})
        else:
            for k, xa, ya in (("ffn1_w_down", sact, dhb), ("ffn1_w_gate", da, s["n1"]), ("ffn1_w_up", db, s["n1"])):
                dep = put_grads(i, "ffn1", {k: tn_matmul(xa, ya, f"d_{k}_{i}", dep)})

    gs = {k: jnp.stack(v) for k, v in gs.items()}
    gs["final_norm"] = d_final[0]
    return loss[0, 0], dh[PAD_FRONT + N_META:], dh[PAD_FRONT:PAD_FRONT + N_META], gs, dep


def _pack_rows(arrays, cols):
    flat = jnp.concatenate([a.reshape(-1) for a in arrays])
    rows = -(-flat.shape[0] // cols)
    rows = -(-rows // 16) * 16
    return jnp.pad(flat, (0, rows * cols - flat.shape[0])).reshape(rows, cols)


def _unpack_rows(packed, shapes):
    flat = packed.reshape(-1)
    out, off = [], 0
    for shp in shapes:
        n = math.prod(shp)
        out.append(flat[off:off + n].reshape(shp))
        off += n
    return out


def kernel(x, meta_tokens, ffn1_norm, ffn1_w_gate, ffn1_w_up, ffn1_w_down, mix_norm, w_in, attn_sinks, ssm_a_re, ssm_a_im, ssm_log_dt, ssm_b_re, ssm_b_im, ssm_c_re, ssm_c_im, ssm_d, w_attn_proj, w_glu_v, w_glu_g, w_out, ffn2_norm, ffn2_w_gate, ffn2_w_up, ffn2_w_down, final_norm, loss_target, m_meta_tokens, m_ffn1_norm, m_ffn1_w_gate, m_ffn1_w_up, m_ffn1_w_down, m_mix_norm, m_w_in, m_attn_sinks, m_ssm_a_re, m_ssm_a_im, m_ssm_log_dt, m_ssm_b_re, m_ssm_b_im, m_ssm_c_re, m_ssm_c_im, m_ssm_d, m_w_attn_proj, m_w_glu_v, m_w_glu_g, m_w_out, m_ffn2_norm, m_ffn2_w_gate, m_ffn2_w_up, m_ffn2_w_down, m_final_norm, v_meta_tokens, v_ffn1_norm, v_ffn1_w_gate, v_ffn1_w_up, v_ffn1_w_down, v_mix_norm, v_w_in, v_attn_sinks, v_ssm_a_re, v_ssm_a_im, v_ssm_log_dt, v_ssm_b_re, v_ssm_b_im, v_ssm_c_re, v_ssm_c_im, v_ssm_d, v_w_attn_proj, v_w_glu_v, v_w_glu_g, v_w_out, v_ffn2_norm, v_ffn2_w_gate, v_ffn2_w_up, v_ffn2_w_down, v_final_norm):
    names = ("meta_tokens", "ffn1_norm", "ffn1_w_gate", "ffn1_w_up", "ffn1_w_down", "mix_norm", "w_in", "attn_sinks",
             "ssm_a_re", "ssm_a_im", "ssm_log_dt", "ssm_b_re", "ssm_b_im", "ssm_c_re", "ssm_c_im", "ssm_d",
             "w_attn_proj", "w_glu_v", "w_glu_g", "w_out", "ffn2_norm", "ffn2_w_gate", "ffn2_w_up", "ffn2_w_down",
             "final_norm")
    weights = dict(zip(names, (meta_tokens, ffn1_norm, ffn1_w_gate, ffn1_w_up, ffn1_w_down, mix_norm, w_in, attn_sinks, ssm_a_re, ssm_a_im, ssm_log_dt, ssm_b_re, ssm_b_im, ssm_c_re, ssm_c_im, ssm_d, w_attn_proj, w_glu_v, w_glu_g, w_out, ffn2_norm, ffn2_w_gate, ffn2_w_up, ffn2_w_down, final_norm)))
    moments_m = dict(zip(names, (m_meta_tokens, m_ffn1_norm, m_ffn1_w_gate, m_ffn1_w_up, m_ffn1_w_down, m_mix_norm, m_w_in, m_attn_sinks, m_ssm_a_re, m_ssm_a_im, m_ssm_log_dt, m_ssm_b_re, m_ssm_b_im, m_ssm_c_re, m_ssm_c_im, m_ssm_d, m_w_attn_proj, m_w_glu_v, m_w_glu_g, m_w_out, m_ffn2_norm, m_ffn2_w_gate, m_ffn2_w_up, m_ffn2_w_down, m_final_norm)))
    moments_v = dict(zip(names, (v_meta_tokens, v_ffn1_norm, v_ffn1_w_gate, v_ffn1_w_up, v_ffn1_w_down, v_mix_norm, v_w_in, v_attn_sinks, v_ssm_a_re, v_ssm_a_im, v_ssm_log_dt, v_ssm_b_re, v_ssm_b_im, v_ssm_c_re, v_ssm_c_im, v_ssm_d, v_w_attn_proj, v_w_glu_v, v_w_glu_g, v_w_out, v_ffn2_norm, v_ffn2_w_gate, v_ffn2_w_up, v_ffn2_w_down, v_final_norm)))
    me = _my_index()

    order = [(i, part) for i in range(DEPTH) for part in PARTS]
    gathers = {}
    token = jnp.zeros((8, 128), F32)
    for i, part in order:
        shards = [_to_rows(k, weights[k][i]).astype(BF16) for k in PARTS[part]]
        if (i, part) == order[0]:
            shards.append(meta_tokens)
        ex = Exchange(shards, False, f"gather_{part}_{i}", relay=True)
        state, token = ex.start(shards, token)
        gathers[i, part] = [ex, state, False]
    all_started = token

    def relay(group, after):
        ex, state, relayed = gathers[group]
        if relayed:
            return []
        new_state, relay_token = ex.relay(state, after)
        gathers[group][1:] = [new_state, True]
        return [relay_token]

    def get_weights(i, part, after):
        g = order.index((i, part))
        after = [all_started] + list(after) if g == 0 else [after]
        tokens = relay(order[g], after)
        if g >= 2 and g + 1 < len(order):
            tokens += relay(order[g + 1], after)
        ex, state, _ = gathers[i, part]
        shards, lands = ex.wait(state, after + tokens)
        fulls = ex.place(lands, shards)
        got = dict(zip(PARTS[part], fulls))
        if (i, part) == (0, "ffn1"):
            got["meta_tokens"] = jnp.swapaxes(fulls[-1].reshape(N_DEV, N_META, 128), 0, 1).reshape(N_META, D_MODEL)
        return got

    scatters = []

    def put_grads(i, part, gdict):
        ks = list(gdict)
        srcs = [gdict[k] for k in ks]
        ex = Exchange(srcs, True, f"scatter_{part if len(ks) > 1 else ks[0]}_{i}")
        state, tok = ex.start(srcs, all_started)
        scatters.append((i, ks, ex, state))
        return tok

    small = {k: weights[k] for k in SMALL}
    loss, dx, dmeta, gs, last_started = local_step(x[0], loss_target[0], get_weights, put_grads, small)

    grads, deltas, new_m, new_v = {}, {}, {}, {}
    small_list = [loss.reshape(1), dmeta] + [gs[k] for k in SMALL]
    packed = _pack_rows(small_list, D_MODEL)
    small_ex = Exchange([packed], False, "gather_small", relay=True)
    small_state, after = small_ex.start([packed], last_started)

    updated = {}
    me_index = jnp.reshape(me, (1,)).astype(jnp.int32)
    for i, ks, ex, state in scatters:
        partials, lands = ex.wait(state, after)
        for k, partial, slots in zip(ks, partials, lands):
            own_layout = weights[k].shape[-1] % 128 == 0 and k not in BIG_N
            view = (lambda a: a) if own_layout else (lambda a: _to_rows(k, a))
            updated[k] = sum_adamw_layer(me_index, slots, partial, view(weights[k]), view(moments_m[k]),
                                         view(moments_v[k]), i, updated.get(k), f"adamw_{k}_{i}", transposed=own_layout)
            after = updated[k][0]
    for k, outs in updated.items():
        own_layout = weights[k].shape[-1] % 128 == 0 and k not in BIG_N
        grads[k], deltas[k], new_m[k], new_v[k] = [a if own_layout else _to_rows(k, a) for a in outs]

    small_state, relayed = small_ex.relay(small_state, after)
    packed_own, packed_all = small_ex.wait(small_state, [after, relayed])
    (packed_all,) = small_ex.place(packed_all, packed_own)
    total = sum_slots(packed_all.reshape(N_DEV, packed.shape[0], D_MODEL), "sum_small")
    pieces = _unpack_rows(total, [a.shape for a in small_list])
    loss_out = pieces[0][0]
    grads["meta_tokens"] = lax.dynamic_slice_in_dim(pieces[1], me * 128, 128, axis=1)
    for k, p in zip(SMALL, pieces[2:]):
        grads[k] = p
    small_names = ("meta_tokens",) + SMALL
    updates = adamw_small([(weights[k], grads[k], moments_m[k], moments_v[k], k in ("ssm_b_re", "ssm_b_im"))
                           for k in small_names], "adamw_small")
    for k, (d, mn, vn) in zip(small_names, updates):
        deltas[k], new_m[k], new_v[k] = d, mn, vn
    return (loss_out, dx[None], *[grads[k] for k in names], *[deltas[k] for k in names],
            *[new_m[k] for k in names], *[new_v[k] for k in names])
```

```python
import math

import jax
import jax.numpy as jnp
from jax import lax
from jax.experimental import pallas as pl
from jax.experimental.pallas import tpu as pltpu

F32 = jnp.float32
BF16 = jnp.bfloat16

D_MODEL = 1024
DEPTH = 2
N_META = 16
HEAD_DIM = 64
N_Q_HEADS = 8
ATTN_WIDTH = 512
KV_WIDTH = 128
QKV_WIDTH = ATTN_WIDTH + 2 * KV_WIDTH
WINDOW = 128
BLK = 128
ROPE_THETA = 500000.0
ROT_DIM = 16
SSM_WIDTH = 512
SSM_GROUP = 16
SSM_GROUPS = 32
SSM_STATE = 64
STATE_WIDTH = SSM_GROUPS * SSM_STATE
D_FF = 2816
IN_WIDTH = 3328
EPS = 1e-6
NEG_INF = -1e30
PAD_FRONT = (-N_META) % BLK
N_DEV = 8

ADAM_LR = 0.001
ADAM_B1 = 0.9
ADAM_B2 = 0.999
ADAM_EPS = 1e-08
ADAM_WD = 0.01
ADAM_STEP = 10

VMEM_LIMIT = 56 * 1024 * 1024
TOKEN_TILE = 384
_VMEM = pl.BlockSpec(memory_space=pltpu.VMEM)
_SMEM = pl.BlockSpec(memory_space=pltpu.SMEM)
_ANY = pl.BlockSpec(memory_space=pl.ANY)
MESH = pl.DeviceIdType.MESH


def _params(sem=None):
    return pltpu.CompilerParams(dimension_semantics=sem, vmem_limit_bytes=VMEM_LIMIT)


def _nt(a, b):
    return lax.dot_general(a, b, (((1,), (1,)), ((), ())), preferred_element_type=F32)


def _nn(a, b):
    return jnp.dot(a, b, preferred_element_type=F32)


def _tn(a, b):
    return lax.dot_general(a, b, (((0,), (0,)), ((), ())), preferred_element_type=F32)


def _row_spec(tm, width):
    return pl.BlockSpec((tm, width), lambda i: (i, 0))


def _acc_spec(shape):
    return pl.BlockSpec(shape, lambda i: (0,) * len(shape))


def _rms_stats(x):
    r = lax.rsqrt(jnp.mean(x * x, axis=-1, keepdims=True) + EPS)
    return x * r, r


def _rms_bwd(dn, xh, r, g):
    dg = jnp.sum(dn * xh, axis=0, keepdims=True)
    dxh = dn * g
    dx = r * (dxh - xh * jnp.mean(dxh * xh, axis=-1, keepdims=True))
    return dx, dg


def ffn_fwd(h, g, wg_t, wu_t, wd, name, first=None):
    f = wd.shape[0]
    tm = TOKEN_TILE
    if first is None:
        t, d = h.shape
        rows_in, row_specs = [h], [_row_spec(tm, d)]
    else:
        x_in, head = first
        d = x_in.shape[1]
        t = x_in.shape[0] + BLK
        per_tile, last = tm // BLK, x_in.shape[0] // BLK - 1
        rows_in = [head] + [x_in] * per_tile
        row_specs = [_acc_spec((BLK, d))] + [
            pl.BlockSpec((BLK, d), lambda i, k=k: (jnp.clip(i * per_tile - 1 + k, 0, last), 0)) for k in range(per_tile)]
    n_in = len(rows_in)

    def body(*refs):
        g_ref, wg_ref, wu_ref, wd_ref = refs[n_in:n_in + 4]
        ho_ref, n_ref, sl_ref, p_ref, s_ref = refs[-5:]
        if first is None:
            x = refs[0][...]
        else:
            blocks = [ref[...] for ref in refs[1:n_in]]
            blocks[0] = jnp.where(pl.program_id(0) == 0, refs[0][...], blocks[0])
            x = jnp.concatenate(blocks, axis=0)
            refs[n_in + 4][...] = x
        xh, _ = _rms_stats(x)
        n = (xh * g_ref[...]).astype(BF16)
        n_ref[...] = n
        a = _nt(n, wg_ref[...])
        b = _nt(n, wu_ref[...])
        sig = jax.nn.sigmoid(a)
        sl = a * sig
        sl_ref[...] = sl.astype(BF16)
        p_ref[...] = (b * (sig + sl * (1.0 - sig))).astype(BF16)
        s = (sl * b).astype(BF16)
        s_ref[...] = s
        ho_ref[...] = x + 0.5 * _nn(s, wd_ref[...])

    extra_specs = [] if first is None else [_row_spec(tm, d)]
    extra_shapes = [] if first is None else [jax.ShapeDtypeStruct((t, d), F32)]
    *h_in, ho, n, sl, p, s = pl.pallas_call(
        body, name=name, grid=(t // tm,),
        in_specs=row_specs + [_acc_spec((1, d)), _VMEM, _VMEM, _VMEM],
        out_specs=extra_specs + [_row_spec(tm, d), _row_spec(tm, d), _row_spec(tm, f), _row_spec(tm, f),
                                 _row_spec(tm, f)],
        out_shape=extra_shapes + [jax.ShapeDtypeStruct((t, d), F32), jax.ShapeDtypeStruct((t, d), BF16),
                                  jax.ShapeDtypeStruct((t, f), BF16), jax.ShapeDtypeStruct((t, f), BF16),
                                  jax.ShapeDtypeStruct((t, f), BF16)],
        compiler_params=_params(("arbitrary",)),
    )(*rows_in, g, wg_t, wu_t, wd)
    return (*h_in, ho, n, (sl, p, s))


def ffn_bwd(dh, h, g, acts, wg_t, wu_t, wd, dep, name):
    t, d = h.shape
    f = wd.shape[0]
    tm = TOKEN_TILE
    sl, p, s = acts

    def hidden_body(dh_ref, sl_ref, p_ref, wd_ref, dep_ref, da_ref, db_ref, dhb_ref):
        dhb = (0.5 * dh_ref[...]).astype(BF16)
        dhb_ref[...] = dhb
        ds = _nt(dhb, wd_ref[...])
        da_ref[...] = (ds * p_ref[...].astype(F32)).astype(BF16)
        db_ref[...] = (ds * sl_ref[...].astype(F32)).astype(BF16)

    da, db, dhb = pl.pallas_call(
        hidden_body, name=name + "_h", grid=(t // tm,),
        in_specs=[_row_spec(tm, d), _row_spec(tm, f), _row_spec(tm, f), _VMEM, _ANY],
        out_specs=[_row_spec(tm, f), _row_spec(tm, f), _row_spec(tm, d)],
        out_shape=[jax.ShapeDtypeStruct((t, f), BF16), jax.ShapeDtypeStruct((t, f), BF16),
                   jax.ShapeDtypeStruct((t, d), BF16)],
        compiler_params=_params(("arbitrary",)),
    )(dh, sl, p, wd, dep)

    def input_body(dh_ref, h_ref, g_ref, da_ref, db_ref, wg_ref, wu_ref, dhi_ref, dg_ref):
        dn = _nn(da_ref[...], wg_ref[...]) + _nn(db_ref[...], wu_ref[...])
        xh, r = _rms_stats(h_ref[...])
        dx, dg = _rms_bwd(dn, xh, r, g_ref[...])
        dhi_ref[...] = dh_ref[...] + dx

        @pl.when(pl.program_id(0) == 0)
        def _():
            dg_ref[...] = jnp.zeros_like(dg_ref)

        dg_ref[...] += dg

    dhi, dg = pl.pallas_call(
        input_body, name=name + "_x", grid=(t // tm,),
        in_specs=[_row_spec(tm, d), _row_spec(tm, d), _acc_spec((1, d)), _row_spec(tm, f), _row_spec(tm, f),
                  _VMEM, _VMEM],
        out_specs=[_row_spec(tm, d), _acc_spec((1, d))],
        out_shape=[jax.ShapeDtypeStruct((t, d), F32), jax.ShapeDtypeStruct((1, d), F32)],
        compiler_params=_params(("arbitrary",)),
    )(dh, h, g, da, db, wg_t, wu_t)
    return dhi, da, db, s, dhb, dg


DW_TILE = 256


def tn_matmul(x, y, name, dep=None):
    xs = list(x) if isinstance(x, (list, tuple)) else [x]
    t = xs[0].shape[0]
    n = y.shape[1]
    bm = DW_TILE
    tiles = [a.shape[1] // bm for a in xs]
    offs = [sum(tiles[:k]) for k in range(len(xs))]
    deps = [] if dep is None else [dep]

    def body(*refs):
        y_ref, o_ref = refs[len(xs)], refs[-1]
        i = pl.program_id(0)
        for k in range(len(xs)):
            @pl.when((i >= offs[k]) & (i < offs[k] + tiles[k]))
            def _(k=k):
                o_ref[...] = _tn(refs[k][...], y_ref[...]).astype(BF16)

    def x_spec(k):
        return pl.BlockSpec((t, bm), lambda i: (0, jnp.clip(i - offs[k], 0, tiles[k] - 1)))

    return pl.pallas_call(
        body, name=name, grid=(sum(tiles),),
        in_specs=[x_spec(k) for k in range(len(xs))] + [_VMEM] + [_ANY] * len(deps),
        out_specs=pl.BlockSpec((bm, n), lambda i: (i, 0)),
        out_shape=jax.ShapeDtypeStruct((sum(tiles) * bm, n), BF16),
        compiler_params=_params(("arbitrary",)),
    )(*xs, y, *deps)


def head_fwd_bwd(h, g, tgt):
    t, d = h.shape
    tm = TOKEN_TILE
    per_tile = tm // BLK
    last = tgt.shape[0] // BLK - 1

    def body(h_ref, g_ref, *rest):
        t_refs, (loss_ref, dh_ref, dg_ref) = rest[:per_tile], rest[per_tile:]
        i = pl.program_id(0)
        xh, r = _rms_stats(h_ref[...])
        gv = g_ref[...]
        target = jnp.concatenate([ref[...] for ref in t_refs], axis=0)
        row = i * tm + lax.broadcasted_iota(jnp.int32, (tm, 1), 0)
        e = jnp.where(row >= BLK, xh * gv - target, 0.0)
        dx, dg = _rms_bwd(e * (1.0 / d), xh, r, gv)
        dh_ref[...] = dx

        @pl.when(i == 0)
        def _():
            dg_ref[...] = jnp.zeros_like(dg_ref)
            loss_ref[...] = jnp.zeros_like(loss_ref)

        dg_ref[...] += dg
        loss_ref[...] += jnp.sum(e * e) * (0.5 / d)

    def target_spec(k):
        return pl.BlockSpec((BLK, d), lambda i: (jnp.clip(i * per_tile - 1 + k, 0, last), 0))

    return pl.pallas_call(
        body, name="head", grid=(t // tm,),
        in_specs=[_row_spec(tm, d), _acc_spec((1, d))] + [target_spec(k) for k in range(per_tile)],
        out_specs=[_acc_spec((1, 128)), _row_spec(tm, d), _acc_spec((1, d))],
        out_shape=[jax.ShapeDtypeStruct((1, 128), F32), jax.ShapeDtypeStruct((t, d), F32),
                   jax.ShapeDtypeStruct((1, d), F32)],
        compiler_params=_params(("arbitrary",)),
    )(h, g, *[tgt] * per_tile)


def rope_tables(t):
    pos = jnp.arange(t, dtype=F32) - PAD_FRONT
    inv_freq = ROPE_THETA ** (-jnp.arange(0, ROT_DIM, 2, dtype=F32) / ROT_DIM)
    ang = pos[:, None] * inv_freq[None, :]
    cos, sin = jnp.cos(ang), jnp.sin(ang)
    ones = jnp.ones((t, HEAD_DIM - ROT_DIM), F32)
    cos_h = jnp.concatenate([cos, cos, ones], axis=1)
    sin_h = jnp.concatenate([-sin, sin, 0.0 * ones], axis=1)
    return jnp.concatenate([cos_h, cos_h], axis=1), jnp.concatenate([sin_h, sin_h], axis=1)


def _swap_halves(x):
    n = x.shape[1]
    lane = lax.broadcasted_iota(jnp.int32, x.shape, 1)
    return jnp.where(lane % HEAD_DIM < ROT_DIM // 2, pltpu.roll(x, n - ROT_DIM // 2, 1), pltpu.roll(x, ROT_DIM // 2, 1))


def _rope(x, cos_t, sin_t, sign):
    return x * cos_t + sign * (_swap_halves(x) * sin_t)


def win_fwd(h, g, win_t, cos_t, sin_t, name):
    t, d = h.shape
    tm = TOKEN_TILE

    def body(h_ref, g_ref, w_ref, c_ref, s_ref, n_ref, qkv_ref, u_ref, gates_ref):
        xh, _ = _rms_stats(h_ref[...])
        n = (xh * g_ref[...]).astype(BF16)
        n_ref[...] = n
        z = _nt(n, w_ref[...])
        c, s = c_ref[...], s_ref[...]
        for j in range((ATTN_WIDTH + KV_WIDTH) // 128):
            qkv_ref[:, j * 128:(j + 1) * 128] = _rope(z[:, j * 128:(j + 1) * 128], c, s, 1.0).astype(BF16)
        qkv_ref[:, ATTN_WIDTH + KV_WIDTH:QKV_WIDTH] = z[:, ATTN_WIDTH + KV_WIDTH:QKV_WIDTH].astype(BF16)
        for j in range(N_CHUNK):
            u_ref[j] = z[:, QKV_WIDTH + j * U_CHUNK:QKV_WIDTH + (j + 1) * U_CHUNK]
        gates_ref[...] = z[:, QKV_WIDTH + SSM_WIDTH:].astype(BF16)

    return pl.pallas_call(
        body, name=name, grid=(t // tm,),
        in_specs=[_row_spec(tm, d), _acc_spec((1, d)), _VMEM, _row_spec(tm, 128), _row_spec(tm, 128)],
        out_specs=[_row_spec(tm, d), _row_spec(tm, QKV_WIDTH), _chunked_spec(tm, lambda i: i), _row_spec(tm, 2 * d)],
        out_shape=[jax.ShapeDtypeStruct((t, d), BF16), jax.ShapeDtypeStruct((t, QKV_WIDTH), BF16),
                   jax.ShapeDtypeStruct((N_CHUNK, t, U_CHUNK), F32), jax.ShapeDtypeStruct((t, 2 * d), BF16)],
        compiler_params=_params(("arbitrary",)),
    )(h, g, win_t, cos_t, sin_t)


def win_bwd(dh, h, g, dqkv, du, dgates, win_t, dep, name):
    t, d = h.shape
    tm = TOKEN_TILE

    def body(dh_ref, h_ref, g_ref, dqkv_ref, du_ref, dgt_ref, w_ref, dep_ref, dhi_ref, dg_ref):
        dn = (_nn(dqkv_ref[...], w_ref[0:QKV_WIDTH, :])
              + _nn(du_ref[...], w_ref[QKV_WIDTH:QKV_WIDTH + SSM_WIDTH, :])
              + _nn(dgt_ref[...], w_ref[QKV_WIDTH + SSM_WIDTH:, :]))
        xh, r = _rms_stats(h_ref[...])
        dx, dg = _rms_bwd(dn, xh, r, g_ref[...])
        dhi_ref[...] = dh_ref[...] + dx

        @pl.when(pl.program_id(0) == 0)
        def _():
            dg_ref[...] = jnp.zeros_like(dg_ref)

        dg_ref[...] += dg

    return pl.pallas_call(
        body, name=name, grid=(t // tm,),
        in_specs=[_row_spec(tm, d), _row_spec(tm, d), _acc_spec((1, d)), _row_spec(tm, QKV_WIDTH),
                  _row_spec(tm, SSM_WIDTH), _row_spec(tm, 2 * d), _VMEM, _ANY],
        out_specs=[_row_spec(tm, d), _acc_spec((1, d))],
        out_shape=[jax.ShapeDtypeStruct((t, d), F32), jax.ShapeDtypeStruct((1, d), F32)],
        compiler_params=_params(("arbitrary",)),
    )(dh, h, g, dqkv, du, dgates, win_t, dep)


def _attn_mask(blk):
    q_pos = blk * BLK + lax.broadcasted_iota(jnp.int32, (BLK, 3 * BLK), 0) - PAD_FRONT
    col = lax.broadcasted_iota(jnp.int32, (BLK, 3 * BLK), 1)
    part = col // BLK
    k_pos = jnp.where(part == 0, col, (blk + part - 2) * BLK + (col - part * BLK)) - PAD_FRONT
    dist = q_pos - k_pos
    meta_ok = (part == 0) & (k_pos >= 0) & (dist >= 0)
    band_ok = (part > 0) & (k_pos >= N_META) & (dist >= 0) & (dist < WINDOW)
    return meta_ok | band_ok


def _head_halves(x128, kv):
    x = x128.astype(F32)
    lane = lax.broadcasted_iota(jnp.int32, x.shape, 1)
    swapped = pltpu.roll(x, HEAD_DIM, 1)
    lo, hi = (x, swapped) if kv == 0 else (swapped, x)
    return jnp.where(lane < HEAD_DIM, lo, 0.0).astype(BF16), jnp.where(lane >= HEAD_DIM, hi, 0.0).astype(BF16)


def _gather_keys(meta_ref, prev_ref, cur_ref, lo):
    return jnp.concatenate([meta_ref[:, lo:lo + 128], prev_ref[:, lo:lo + 128], cur_ref[:, lo:lo + 128]], axis=0)


def _pair_lanes(kv):
    return slice(2 * kv * 128, (2 * kv + 1) * 128), slice((2 * kv + 1) * 128, (2 * kv + 2) * 128)


def _stacked_sinks(sink_ref, head):
    row = lax.broadcasted_iota(jnp.int32, (2 * BLK, 1), 0)
    return jnp.where(row < BLK, sink_ref[0, head], sink_ref[0, head + 2])


def _softmax_with_sink(s, mask, sink):
    s = jnp.where(mask, s * (HEAD_DIM ** -0.5), NEG_INF)
    m = jnp.maximum(jnp.max(s, axis=-1, keepdims=True), sink)
    p = jnp.exp(s - m)
    p_sink = jnp.exp(sink - m)
    inv = 1.0 / (jnp.sum(p, axis=-1, keepdims=True) + p_sink)
    return p * inv, p_sink * inv


def attn_fwd(qkv, sinks, name):
    t = qkv.shape[0]
    nb = t // BLK

    def body(sink_ref, meta_ref, prev_ref, cur_ref, o_ref):
        blk = pl.program_id(0)
        mask = _attn_mask(blk)
        mask2 = jnp.concatenate([mask, mask], axis=0)
        k128 = _gather_keys(meta_ref, prev_ref, cur_ref, ATTN_WIDTH)
        v128 = _gather_keys(meta_ref, prev_ref, cur_ref, ATTN_WIDTH + KV_WIDTH)
        for kv in range(2):
            k_lo, k_hi = _head_halves(k128, kv)
            v_lo, v_hi = _head_halves(v128, kv)
            lanes0, lanes1 = _pair_lanes(kv)
            q2 = jnp.concatenate([cur_ref[:, lanes0], cur_ref[:, lanes1]], axis=0)
            p_a, _ = _softmax_with_sink(_nt(q2, k_lo), mask2, _stacked_sinks(sink_ref, 4 * kv))
            p_b, _ = _softmax_with_sink(_nt(q2, k_hi), mask2, _stacked_sinks(sink_ref, 4 * kv + 1))
            o2 = (_nn(p_a.astype(BF16), v_lo) + _nn(p_b.astype(BF16), v_hi)).astype(BF16)
            o_ref[:, lanes0] = o2[0:BLK]
            o_ref[:, lanes1] = o2[BLK:2 * BLK]

    blk_spec = lambda f: pl.BlockSpec((BLK, QKV_WIDTH), f)
    return pl.pallas_call(
        body, name=name, grid=(nb,),
        in_specs=[_SMEM, blk_spec(lambda i: (0, 0)), blk_spec(lambda i: (jnp.maximum(i - 1, 0), 0)),
                  blk_spec(lambda i: (i, 0))],
        out_specs=_row_spec(BLK, ATTN_WIDTH),
        out_shape=jax.ShapeDtypeStruct((t, ATTN_WIDTH), BF16),
        compiler_params=_params(("arbitrary",)),
    )(sinks, qkv, qkv, qkv)


def attn_bwd(qkv, do, sinks, cos_t, sin_t, name):
    t = qkv.shape[0]
    nb = t // BLK

    def body(sink_ref, meta_ref, prev_ref, cur_ref, do_ref, c_ref, s_ref, dqkv_ref, dsink_ref, carry_ref, macc_ref):
        step = pl.program_id(0)
        blk = nb - 1 - step

        @pl.when(step == 0)
        def _():
            dsink_ref[...] = jnp.zeros_like(dsink_ref)
            carry_ref[...] = jnp.zeros_like(carry_ref)
            macc_ref[...] = jnp.zeros_like(macc_ref)

        mask = _attn_mask(blk)
        mask2 = jnp.concatenate([mask, mask], axis=0)
        lane = lax.broadcasted_iota(jnp.int32, (3 * BLK, 128), 1)
        k128 = _gather_keys(meta_ref, prev_ref, cur_ref, ATTN_WIDTH)
        v128 = _gather_keys(meta_ref, prev_ref, cur_ref, ATTN_WIDTH + KV_WIDTH)
        cos_b, sin_b = c_ref[...], s_ref[...]
        dk_heads, dv_heads = [], []
        for kv in range(2):
            k_lo, k_hi = _head_halves(k128, kv)
            v_lo, v_hi = _head_halves(v128, kv)
            lanes0, lanes1 = _pair_lanes(kv)
            q2 = jnp.concatenate([cur_ref[:, lanes0], cur_ref[:, lanes1]], axis=0)
            do2 = jnp.concatenate([do_ref[:, lanes0], do_ref[:, lanes1]], axis=0)
            ds_half, p_half = [], []
            for half, (k_h, v_h) in enumerate(((k_lo, v_lo), (k_hi, v_hi))):
                head = 4 * kv + half
                p, p_sink = _softmax_with_sink(_nt(q2, k_h), mask2, _stacked_sinks(sink_ref, head))
                dp = _nt(do2, v_h)
                dsum = jnp.sum(p * dp, axis=-1, keepdims=True)
                ds_half.append((p * (dp - dsum) * (HEAD_DIM ** -0.5)).astype(BF16))
                p_half.append(p.astype(BF16))
                dsink = p_sink * dsum
                for part, h in ((0, head), (1, head + 2)):
                    total = -jnp.sum(dsink[part * BLK:(part + 1) * BLK], axis=0, keepdims=True)
                    dsink_ref[h:h + 1, :] += jnp.broadcast_to(total, (1, 128))
            dq2 = _nn(ds_half[0], k_lo) + _nn(ds_half[1], k_hi)
            dqkv_ref[:, lanes0] = _rope(dq2[0:BLK], cos_b, sin_b, -1.0).astype(BF16)
            dqkv_ref[:, lanes1] = _rope(dq2[BLK:2 * BLK], cos_b, sin_b, -1.0).astype(BF16)
            dk_acc = jnp.where(lane < HEAD_DIM, _tn(ds_half[0], q2), _tn(ds_half[1], q2))
            dv_acc = jnp.where(lane < HEAD_DIM, _tn(p_half[0], do2), _tn(p_half[1], do2))
            dk_heads.append(dk_acc + pltpu.roll(dk_acc, HEAD_DIM, 1))
            dv_heads.append(dv_acc + pltpu.roll(dv_acc, HEAD_DIM, 1))
        dkv = jnp.concatenate([jnp.where(lane < HEAD_DIM, dk_heads[0], dk_heads[1]),
                               jnp.where(lane < HEAD_DIM, dv_heads[0], dv_heads[1])], axis=1)
        macc_ref[...] += dkv[0:BLK]
        is_last = (blk == 0).astype(F32)
        mine = dkv[2 * BLK:3 * BLK] + carry_ref[...] + is_last * macc_ref[...]
        carry_ref[...] = dkv[BLK:2 * BLK]
        dqkv_ref[:, ATTN_WIDTH:ATTN_WIDTH + KV_WIDTH] = _rope(mine[:, 0:128], cos_b, sin_b, -1.0).astype(BF16)
        dqkv_ref[:, ATTN_WIDTH + KV_WIDTH:QKV_WIDTH] = mine[:, 128:256].astype(BF16)

    rev = lambda i: nb - 1 - i
    blk_spec = lambda f: pl.BlockSpec((BLK, QKV_WIDTH), f)
    return pl.pallas_call(
        body, name=name, grid=(nb,),
        in_specs=[_SMEM, blk_spec(lambda i: (0, 0)), blk_spec(lambda i: (jnp.maximum(rev(i) - 1, 0), 0)),
                  blk_spec(lambda i: (rev(i), 0)), pl.BlockSpec((BLK, ATTN_WIDTH), lambda i: (rev(i), 0)),
                  pl.BlockSpec((BLK, 128), lambda i: (rev(i), 0)), pl.BlockSpec((BLK, 128), lambda i: (rev(i), 0))],
        out_specs=[pl.BlockSpec((BLK, QKV_WIDTH), lambda i: (rev(i), 0)), _acc_spec((N_Q_HEADS, 128))],
        out_shape=[jax.ShapeDtypeStruct((t, QKV_WIDTH), BF16), jax.ShapeDtypeStruct((N_Q_HEADS, 128), F32)],
        scratch_shapes=[pltpu.VMEM((BLK, 256), F32), pltpu.VMEM((BLK, 256), F32)],
        compiler_params=_params(("arbitrary",)),
    )(sinks, qkv, qkv, qkv, do, cos_t, sin_t)


def _cmul(ar, ai, br, bi):
    return ar * br - ai * bi, ar * bi + ai * br


def ssm_prep(a_re, a_im, log_dt, b_re_t, b_im_t, name):
    def body(ar_ref, ai_ref, ldt_ref, br_ref, bi_ref, lr_ref, li_ref, bbr_ref, bbi_ref):
        ar, ai = ar_ref[...], ai_ref[...]
        dt = jnp.exp(ldt_ref[...])
        mag = jnp.exp(ar * dt)
        lr = mag * jnp.cos(ai * dt)
        li = mag * jnp.sin(ai * dt)
        den = ar * ar + ai * ai
        nr = lr - 1.0
        cr = ((nr * ar + li * ai) / den)[:, None, :]
        ci = ((li * ar - nr * ai) / den)[:, None, :]
        br, bi = br_ref[...], bi_ref[...]
        lr_ref[...] = lr
        li_ref[...] = li
        bbr_ref[...] = cr * br - ci * bi
        bbi_ref[...] = cr * bi + ci * br

    gp = jax.ShapeDtypeStruct(a_re.shape, F32)
    gcp = jax.ShapeDtypeStruct(b_re_t.shape, F32)
    return pl.pallas_call(body, name=name, out_shape=[gp, gp, gcp, gcp],
                          in_specs=[_VMEM] * 5, out_specs=[_VMEM] * 4)(a_re, a_im, log_dt, b_re_t, b_im_t)


def ssm_prep_bwd(a_re, a_im, log_dt, b_re_t, b_im_t, dl_re, dl_im, dbb_re, dbb_im, name):
    def body(ar_ref, ai_ref, ldt_ref, br_ref, bi_ref, dlr_ref, dli_ref, dbbr_ref, dbbi_ref,
             dar_ref, dai_ref, dldt_ref, dbr_ref, dbi_ref):
        ar, ai = ar_ref[...], ai_ref[...]
        dt = jnp.exp(ldt_ref[...])
        mag = jnp.exp(ar * dt)
        lr = mag * jnp.cos(ai * dt)
        li = mag * jnp.sin(ai * dt)
        den = ar * ar + ai * ai
        nr = lr - 1.0
        cr = (nr * ar + li * ai) / den
        ci = (li * ar - nr * ai) / den
        br, bi = br_ref[...], bi_ref[...]
        dbbr, dbbi = dbbr_ref[...], dbbi_ref[...]
        dbr_ref[...] = cr[:, None, :] * dbbr + ci[:, None, :] * dbbi
        dbi_ref[...] = cr[:, None, :] * dbbi - ci[:, None, :] * dbbr
        dcr = jnp.sum(br * dbbr + bi * dbbi, axis=1)
        dci = jnp.sum(br * dbbi - bi * dbbr, axis=1)
        d_num_r = dcr / den
        d_num_i = dci / den
        d_den = -(dcr * cr + dci * ci) / den
        d_lr = dlr_ref[...] + d_num_r * ar - d_num_i * ai
        d_li = dli_ref[...] + d_num_r * ai + d_num_i * ar
        d_ar = d_num_r * nr + d_num_i * li + d_den * 2.0 * ar
        d_ai = d_num_r * li - d_num_i * nr + d_den * 2.0 * ai
        d_mag = (d_lr * lr + d_li * li) / mag
        d_theta = d_li * lr - d_lr * li
        d_ardt = d_mag * mag
        dar_ref[...] = d_ar + d_ardt * dt
        dai_ref[...] = d_ai + d_theta * dt
        d_dt = jnp.sum(d_ardt * ar + d_theta * ai, axis=1, keepdims=True)
        dldt_ref[...] = d_dt * dt

    gp = jax.ShapeDtypeStruct(a_re.shape, F32)
    gcp = jax.ShapeDtypeStruct(b_re_t.shape, F32)
    return pl.pallas_call(body, name=name, out_shape=[gp, gp, jax.ShapeDtypeStruct(log_dt.shape, F32), gcp, gcp],
                          in_specs=[_VMEM] * 9, out_specs=[_VMEM] * 5,
                          )(a_re, a_im, log_dt, b_re_t, b_im_t, dl_re, dl_im, dbb_re, dbb_im)


N_CHUNK = 4
U_CHUNK = SSM_WIDTH // N_CHUNK
H_CHUNK = STATE_WIDTH // N_CHUNK
SUB = 8


def _block_diag_b(bb):
    x = bb.reshape(N_CHUNK, 8, SSM_GROUP, 1, SSM_STATE)
    same = (jnp.arange(8)[:, None] == jnp.arange(8)[None, :])[None, :, None, :, None]
    return jnp.where(same, x, 0.0).reshape(N_CHUNK, U_CHUNK, H_CHUNK)


def _block_diag_c(c):
    x = jnp.swapaxes(c.reshape(N_CHUNK, 8, SSM_GROUP, SSM_STATE), 2, 3)[:, :, :, None, :]
    same = (jnp.arange(8)[:, None] == jnp.arange(8)[None, :])[None, :, None, :, None]
    return jnp.where(same, x, 0.0).reshape(N_CHUNK, H_CHUNK, U_CHUNK)


def _diag_of_b(m):
    x = m.reshape(N_CHUNK, 8, SSM_GROUP, 8, SSM_STATE)
    return jnp.stack([x[:, g, :, g, :] for g in range(8)], axis=1).reshape(SSM_GROUPS, SSM_GROUP, SSM_STATE)


def _diag_of_c(m):
    x = m.reshape(N_CHUNK, 8, SSM_STATE, 8, SSM_GROUP)
    d = jnp.stack([x[:, g, :, g, :] for g in range(8)], axis=1)
    return jnp.swapaxes(d, 2, 3).reshape(SSM_GROUPS, SSM_GROUP, SSM_STATE)


def _lambda_tables(lr, li, reverse):
    p1 = (lr, li)
    p2 = _cmul(*p1, *p1)
    p4 = _cmul(*p2, *p2)
    rows = [p1]
    for _ in range(SUB - 1):
        rows.append(_cmul(*rows[-1], *p1))
    if reverse:
        rows = rows[::-1]
    return p1, p2, p4, (jnp.concatenate([r[0] for r in rows], axis=0), jnp.concatenate([r[1] for r in rows], axis=0))


def _scan8(xr, xi, pows, table, cr, ci, reverse):
    row = lax.broadcasted_iota(jnp.int32, xr.shape, 0)
    for d, (pr, pi) in zip((1, 2, 4), pows):
        if reverse:
            sr, si = pltpu.roll(xr, SUB - d, 0), pltpu.roll(xi, SUB - d, 0)
            keep = row < SUB - d
        else:
            sr, si = pltpu.roll(xr, d, 0), pltpu.roll(xi, d, 0)
            keep = row >= d
        sr = jnp.where(keep, sr, 0.0)
        si = jnp.where(keep, si, 0.0)
        xr, xi = xr + pr * sr - pi * si, xi + pr * si + pi * sr
    tr, ti = table
    return xr + tr * cr - ti * ci, xi + tr * ci + ti * cr


def _gelu_and_grad(y):
    k0 = math.sqrt(2.0 / math.pi)
    inner = k0 * (y + 0.044715 * y * y * y)
    th = jnp.tanh(inner)
    g = 0.5 * y * (1.0 + th)
    dg = 0.5 * (1.0 + th) + 0.5 * y * (1.0 - th * th) * k0 * (1.0 + 3.0 * 0.044715 * y * y)
    return g, dg


SCAN_TILE = TOKEN_TILE
SEG = SCAN_TILE // SUB
SCAN_LANES = 1024


def _perm_matrix(to_segments):
    a = lax.broadcasted_iota(jnp.int32, (SCAN_TILE, SCAN_TILE), 0)
    b = lax.broadcasted_iota(jnp.int32, (SCAN_TILE, SCAN_TILE), 1)
    rho, time = (a, b) if to_segments else (b, a)
    return (time == (rho % SUB) * SEG + rho // SUB).astype(BF16)


def _chunked_spec(rows, block_of):
    return pl.BlockSpec((N_CHUNK, rows, U_CHUNK), lambda i: (0, block_of(i), 0))


def _load_segments(src_ref, dst_ref):
    for j in range(N_CHUNK):
        for r in range(SEG):
            dst_ref[r * SUB:(r + 1) * SUB, j * U_CHUNK:(j + 1) * U_CHUNK] = src_ref.at[j][pl.ds(r, SUB, stride=SEG), :]


def _power_table(lr, li, pr_ref, pi_ref):
    cur = (lr, li)
    for r in range(SEG):
        pr_ref[r * SUB:(r + 1) * SUB, :] = jnp.broadcast_to(cur[0], (SUB, STATE_WIDTH))
        pi_ref[r * SUB:(r + 1) * SUB, :] = jnp.broadcast_to(cur[1], (SUB, STATE_WIDTH))
        cur = _cmul(*cur, lr, li)


def _table_rows(ref, k, lanes):
    return ref[pl.ds(pl.multiple_of(k * SUB, SUB), SUB), lanes]


def _segment_scan(xr_ref, xi_ref, lanes, lam, table_row, cr_ref, ci_ref, reverse, extra=None):
    lr = jnp.broadcast_to(lam[0], (SUB, SCAN_LANES))
    li = jnp.broadcast_to(lam[1], (SUB, SCAN_LANES))
    row = lax.broadcasted_iota(jnp.int32, (SUB, SCAN_LANES), 0)

    def rows_of(k):
        r = SEG - 1 - k if reverse else k
        return pl.ds(pl.multiple_of(r * SUB, SUB), SUB)

    def first(k, st):
        sr, si = st
        rows = rows_of(k)
        nr = lr * sr - li * si + xr_ref[rows, lanes]
        ni = lr * si + li * sr + xi_ref[rows, lanes]
        xr_ref[rows, lanes] = nr
        xi_ref[rows, lanes] = ni
        return nr, ni

    zero = jnp.zeros((SUB, SCAN_LANES), F32)
    er, ei = lax.fori_loop(0, SEG, first, (zero, zero))
    l16 = table_row(SEG - 1)
    q1, q2, q4, tab = _lambda_tables(l16[0][0:1], l16[1][0:1], reverse)
    c_r, c_i = cr_ref[:, lanes], ci_ref[:, lanes]
    gr, gi = _scan8(er, ei, (q1, q2, q4), tab, c_r, c_i, reverse)
    if reverse:
        cin_r = jnp.where(row == SUB - 1, c_r, pltpu.roll(gr, SUB - 1, 0))
        cin_i = jnp.where(row == SUB - 1, c_i, pltpu.roll(gi, SUB - 1, 0))
        cr_ref[:, lanes] = gr[0:1]
        ci_ref[:, lanes] = gi[0:1]
    else:
        cin_r = jnp.where(row == 0, c_r, pltpu.roll(gr, 1, 0))
        cin_i = jnp.where(row == 0, c_i, pltpu.roll(gi, 1, 0))
        cr_ref[:, lanes] = gr[SUB - 1:SUB]
        ci_ref[:, lanes] = gi[SUB - 1:SUB]

    def second(k, carry):
        rows = rows_of(k)
        tr, ti = table_row(k)
        ar = xr_ref[rows, lanes] + tr * cin_r - ti * cin_i
        ai = xi_ref[rows, lanes] + tr * cin_i + ti * cin_r
        xr_ref[rows, lanes] = ar
        xi_ref[rows, lanes] = ai
        if extra is None:
            return carry
        return extra(rows, carry, ar, ai)

    init = 0 if extra is None else (cin_r, cin_i, zero, zero)
    return lax.fori_loop(0, SEG, second, init)


def ssm_fwd(u, lam_re, lam_im, bb_re, bb_im, cc_re, cc_im, d_skip, name):
    t = u.shape[1]
    tt = SCAN_TILE

    def body(u_ref, lr_ref, li_ref, bbr_ref, bbi_ref, ccr_ref, cci_ref, d_ref, yg_ref, hr_ref, hi_ref,
             cr_ref, ci_ref, pr_ref, pi_ref, up_ref, y_ref):
        @pl.when(pl.program_id(0) == 0)
        def _():
            cr_ref[...] = jnp.zeros_like(cr_ref)
            ci_ref[...] = jnp.zeros_like(ci_ref)
            _power_table(lr_ref[...], li_ref[...], pr_ref, pi_ref)

        _load_segments(u_ref, up_ref)
        ub = up_ref[...].astype(BF16)
        for j in range(N_CHUNK):
            hs = slice(j * H_CHUNK, (j + 1) * H_CHUNK)
            us = slice(j * U_CHUNK, (j + 1) * U_CHUNK)
            hr_ref[:, hs] = _nn(ub[:, us], bbr_ref[j])
            hi_ref[:, hs] = _nn(ub[:, us], bbi_ref[j])
        for c in range(STATE_WIDTH // SCAN_LANES):
            lanes = slice(c * SCAN_LANES, (c + 1) * SCAN_LANES)
            _segment_scan(hr_ref, hi_ref, lanes, (lr_ref[:, lanes], li_ref[:, lanes]),
                          lambda k, lanes=lanes: (_table_rows(pr_ref, k, lanes), _table_rows(pi_ref, k, lanes)),
                          cr_ref, ci_ref, False)
        for j in range(N_CHUNK):
            hs = slice(j * H_CHUNK, (j + 1) * H_CHUNK)
            us = slice(j * U_CHUNK, (j + 1) * U_CHUNK)
            y = (_nn(hr_ref[:, hs].astype(BF16), ccr_ref[j]) - _nn(hi_ref[:, hs].astype(BF16), cci_ref[j])
                 + d_ref[:, us] * up_ref[:, us])
            y_ref[:, us] = _gelu_and_grad(y)[0]
        yg_ref[...] = _nn(_perm_matrix(False), y_ref[...].astype(BF16)).astype(BF16)

    return pl.pallas_call(
        body, name=name, grid=(t // tt,),
        in_specs=[_chunked_spec(tt, lambda i: i), _VMEM, _VMEM, _VMEM, _VMEM, _VMEM, _VMEM, _VMEM],
        out_specs=[_row_spec(tt, SSM_WIDTH), _row_spec(tt, STATE_WIDTH), _row_spec(tt, STATE_WIDTH)],
        out_shape=[jax.ShapeDtypeStruct((t, SSM_WIDTH), BF16), jax.ShapeDtypeStruct((t, STATE_WIDTH), F32),
                   jax.ShapeDtypeStruct((t, STATE_WIDTH), F32)],
        scratch_shapes=[pltpu.VMEM((1, STATE_WIDTH), F32), pltpu.VMEM((1, STATE_WIDTH), F32),
                        pltpu.VMEM((SCAN_TILE, STATE_WIDTH), F32), pltpu.VMEM((SCAN_TILE, STATE_WIDTH), F32),
                        pltpu.VMEM((tt, SSM_WIDTH), F32), pltpu.VMEM((tt, SSM_WIDTH), F32)],
        compiler_params=_params(("arbitrary",)),
    )(u, lam_re, lam_im, bb_re, bb_im, cc_re, cc_im, d_skip)


def ssm_bwd(dyg, u, h_re, h_im, lam_re, lam_im, bb_re, bb_im, cc_re, cc_im, d_skip, name):
    t = u.shape[1]
    tt = SCAN_TILE
    nt = t // tt

    def body(dyg_ref, u_ref, hr_ref, hi_ref, lr_ref, li_ref, bbr_ref, bbi_ref, ccr_ref, cci_ref, d_ref,
             du_ref, dlr_ref, dli_ref, dbbr_ref, dbbi_ref, dccr_ref, dcci_ref, dd_ref,
             ar_ref, ai_ref, cr_ref, ci_ref, pr_ref, pi_ref, up_ref, dy_ref, dup_ref):
        step = pl.program_id(0)
        tile = nt - 1 - step

        @pl.when(step == 0)
        def _():
            for ref in (cr_ref, ci_ref, dlr_ref, dli_ref, dbbr_ref, dbbi_ref, dccr_ref, dcci_ref, dd_ref):
                ref[...] = jnp.zeros_like(ref)
            _power_table(lr_ref[...], li_ref[...], pr_ref, pi_ref)

        _load_segments(u_ref, up_ref)
        _load_segments(dyg_ref, dy_ref)
        uv = up_ref[...]
        ub = uv.astype(BF16)
        dskip = d_ref[...]
        for j in range(N_CHUNK):
            hs = slice(j * H_CHUNK, (j + 1) * H_CHUNK)
            us = slice(j * U_CHUNK, (j + 1) * U_CHUNK)
            hrb = hr_ref[:, hs].astype(BF16)
            hib = hi_ref[:, hs].astype(BF16)
            y = _nn(hrb, ccr_ref[j]) - _nn(hib, cci_ref[j]) + dskip[:, us] * uv[:, us]
            dy = dy_ref[:, us] * _gelu_and_grad(y)[1]
            dy_ref[:, us] = dy
            dyb = dy.astype(BF16)
            dccr_ref[j] += _tn(hrb, dyb)
            dcci_ref[j] -= _tn(hib, dyb)
            ar_ref[:, hs] = _nt(dyb, ccr_ref[j])
            ai_ref[:, hs] = -_nt(dyb, cci_ref[j])
        dd_ref[...] += jnp.sum(dy_ref[...] * uv, axis=0, keepdims=True)

        for c in range(STATE_WIDTH // SCAN_LANES):
            lanes = slice(c * SCAN_LANES, (c + 1) * SCAN_LANES)

            def dlambda(rows, carry, ar, ai, lanes=lanes):
                nr, ni, accr, acci = carry
                hr, hi = hr_ref[rows, lanes], hi_ref[rows, lanes]
                return ar, ai, accr + nr * hr + ni * hi, acci + ni * hr - nr * hi

            _, _, accr, acci = _segment_scan(
                ar_ref, ai_ref, lanes, (lr_ref[:, lanes], -li_ref[:, lanes]),
                lambda k, lanes=lanes: (_table_rows(pr_ref, k, lanes), -_table_rows(pi_ref, k, lanes)),
                cr_ref, ci_ref, True, dlambda)
            dlr_ref[:, lanes] += accr
            dli_ref[:, lanes] += acci

        rho = lax.broadcasted_iota(jnp.int32, (tt, U_CHUNK), 0)
        time = tile * tt + (rho % SUB) * SEG + rho // SUB
        for j in range(N_CHUNK):
            hs = slice(j * H_CHUNK, (j + 1) * H_CHUNK)
            us = slice(j * U_CHUNK, (j + 1) * U_CHUNK)
            arb = ar_ref[:, hs].astype(BF16)
            aib = ai_ref[:, hs].astype(BF16)
            dbbr_ref[j] += _tn(ub[:, us], arb)
            dbbi_ref[j] += _tn(ub[:, us], aib)
            du = _nt(arb, bbr_ref[j]) + _nt(aib, bbi_ref[j]) + dy_ref[:, us] * dskip[:, us]
            dup_ref[:, us] = jnp.where(time >= PAD_FRONT, du, 0.0)
        du_ref[...] = _nn(_perm_matrix(False), dup_ref[...].astype(BF16)).astype(BF16)

    rev = lambda i: (nt - 1 - i, 0)
    full = lambda shape: pl.BlockSpec(shape, lambda i: (0,) * len(shape))
    return pl.pallas_call(
        body, name=name, grid=(nt,),
        in_specs=[_chunked_spec(tt, lambda i: nt - 1 - i), _chunked_spec(tt, lambda i: nt - 1 - i),
                  pl.BlockSpec((tt, STATE_WIDTH), rev), pl.BlockSpec((tt, STATE_WIDTH), rev),
                  _VMEM, _VMEM, _VMEM, _VMEM, _VMEM, _VMEM, _VMEM],
        out_specs=[pl.BlockSpec((tt, SSM_WIDTH), rev), full((SUB, STATE_WIDTH)), full((SUB, STATE_WIDTH)),
                   full((N_CHUNK, U_CHUNK, H_CHUNK)), full((N_CHUNK, U_CHUNK, H_CHUNK)),
                   full((N_CHUNK, H_CHUNK, U_CHUNK)), full((N_CHUNK, H_CHUNK, U_CHUNK)), full((1, SSM_WIDTH))],
        out_shape=[jax.ShapeDtypeStruct((t, SSM_WIDTH), BF16),
                   jax.ShapeDtypeStruct((SUB, STATE_WIDTH), F32), jax.ShapeDtypeStruct((SUB, STATE_WIDTH), F32),
                   jax.ShapeDtypeStruct((N_CHUNK, U_CHUNK, H_CHUNK), F32),
                   jax.ShapeDtypeStruct((N_CHUNK, U_CHUNK, H_CHUNK), F32),
                   jax.ShapeDtypeStruct((N_CHUNK, H_CHUNK, U_CHUNK), F32),
                   jax.ShapeDtypeStruct((N_CHUNK, H_CHUNK, U_CHUNK), F32),
                   jax.ShapeDtypeStruct((1, SSM_WIDTH), F32)],
        scratch_shapes=[pltpu.VMEM((tt, STATE_WIDTH), F32), pltpu.VMEM((tt, STATE_WIDTH), F32),
                        pltpu.VMEM((1, STATE_WIDTH), F32), pltpu.VMEM((1, STATE_WIDTH), F32),
                        pltpu.VMEM((SCAN_TILE, STATE_WIDTH), F32), pltpu.VMEM((SCAN_TILE, STATE_WIDTH), F32),
                        pltpu.VMEM((tt, SSM_WIDTH), F32), pltpu.VMEM((tt, SSM_WIDTH), F32),
                        pltpu.VMEM((tt, SSM_WIDTH), F32)],
        compiler_params=_params(("arbitrary",)),
    )(dyg, u, h_re, h_im, lam_re, lam_im, bb_re, bb_im, cc_re, cc_im, d_skip)


def merge_fwd(h, o, yg, gates, wap_t, wv_t, wgg_t, wout, name):
    t, d = h.shape
    tm = TOKEN_TILE

    def body(h_ref, o_ref, yg_ref, gt_ref, wap_ref, wv_ref, wgg_ref, wout_ref, ho_ref, mg_ref, a_ref, sv_ref, sg_ref):
        att = _nt(o_ref[...], wap_ref[...])
        ygv = yg_ref[...]
        sv = _nt(ygv, wv_ref[...])
        sg = _nt(ygv, wgg_ref[...])
        a_ref[...] = att.astype(BF16)
        sv_ref[...] = sv.astype(BF16)
        sg_ref[...] = sg.astype(BF16)
        merged = (jax.nn.sigmoid(gt_ref[:, 0:d].astype(F32)) * att
                  + jax.nn.sigmoid(gt_ref[:, d:2 * d].astype(F32)) * (sv * jax.nn.sigmoid(sg))).astype(BF16)
        mg_ref[...] = merged
        ho_ref[...] = h_ref[...] + _nn(merged, wout_ref[...])

    return pl.pallas_call(
        body, name=name, grid=(t // tm,),
        in_specs=[_row_spec(tm, d), _row_spec(tm, ATTN_WIDTH), _row_spec(tm, SSM_WIDTH), _row_spec(tm, 2 * d),
                  _VMEM, _VMEM, _VMEM, _VMEM],
        out_specs=[_row_spec(tm, d), _row_spec(tm, d), _row_spec(tm, d), _row_spec(tm, d), _row_spec(tm, d)],
        out_shape=[jax.ShapeDtypeStruct((t, d), F32), jax.ShapeDtypeStruct((t, d), BF16),
                   jax.ShapeDtypeStruct((t, d), BF16), jax.ShapeDtypeStruct((t, d), BF16),
                   jax.ShapeDtypeStruct((t, d), BF16)],
        compiler_params=_params(("arbitrary",)),
    )(h, o, yg, gates, wap_t, wv_t, wgg_t, wout)


def merge_bwd(dh, gates, att, sv, sg, wap_t, wv_t, wgg_t, wout, dep, name):
    t, d = dh.shape
    tm = TOKEN_TILE

    def body(dh_ref, gt_ref, a_ref, sv_ref, sg_ref, wap_ref, wv_ref, wgg_ref, wout_ref, dep_ref,
             dgt_ref, da_ref, dsv_ref, dsg_ref, do_ref, dyg_ref, dhb_ref):
        dhb = dh_ref[...].astype(BF16)
        dhb_ref[...] = dhb
        dm = _nt(dhb, wout_ref[...])
        sig_a = jax.nn.sigmoid(gt_ref[:, 0:d].astype(F32))
        sig_s = jax.nn.sigmoid(gt_ref[:, d:2 * d].astype(F32))
        sig_g = jax.nn.sigmoid(sg_ref[...].astype(F32))
        svv = sv_ref[...].astype(F32)
        dgt_ref[:, 0:d] = (dm * a_ref[...].astype(F32) * sig_a * (1.0 - sig_a)).astype(BF16)
        dgt_ref[:, d:2 * d] = (dm * (svv * sig_g) * sig_s * (1.0 - sig_s)).astype(BF16)
        da = (dm * sig_a).astype(BF16)
        d_s = dm * sig_s
        dsv = (d_s * sig_g).astype(BF16)
        dsg = (d_s * svv * sig_g * (1.0 - sig_g)).astype(BF16)
        da_ref[...] = da
        dsv_ref[...] = dsv
        dsg_ref[...] = dsg
        do_ref[...] = _nn(da, wap_ref[...]).astype(BF16)
        dyg = _nn(dsv, wv_ref[...]) + _nn(dsg, wgg_ref[...])
        for j in range(N_CHUNK):
            dyg_ref[j] = dyg[:, j * U_CHUNK:(j + 1) * U_CHUNK]

    return pl.pallas_call(
        body, name=name, grid=(t // tm,),
        in_specs=[_row_spec(tm, d), _row_spec(tm, 2 * d), _row_spec(tm, d), _row_spec(tm, d), _row_spec(tm, d),
                  _VMEM, _VMEM, _VMEM, _VMEM, _ANY],
        out_specs=[_row_spec(tm, 2 * d), _row_spec(tm, d), _row_spec(tm, d), _row_spec(tm, d),
                   _row_spec(tm, ATTN_WIDTH), _chunked_spec(tm, lambda i: i), _row_spec(tm, d)],
        out_shape=[jax.ShapeDtypeStruct((t, 2 * d), BF16), jax.ShapeDtypeStruct((t, d), BF16),
                   jax.ShapeDtypeStruct((t, d), BF16), jax.ShapeDtypeStruct((t, d), BF16),
                   jax.ShapeDtypeStruct((t, ATTN_WIDTH), BF16), jax.ShapeDtypeStruct((N_CHUNK, t, U_CHUNK), F32),
                   jax.ShapeDtypeStruct((t, d), BF16)],
        compiler_params=_params(("arbitrary",)),
    )(dh, gates, att, sv, sg, wap_t, wv_t, wgg_t, wout, dep)


def _adamw_math(w, g, m, v):
    mn = ADAM_B1 * m + (1.0 - ADAM_B1) * g
    vn = ADAM_B2 * v + (1.0 - ADAM_B2) * (g * g)
    m_hat = mn / (1.0 - ADAM_B1 ** ADAM_STEP)
    v_hat = vn / (1.0 - ADAM_B2 ** ADAM_STEP)
    return -ADAM_LR * (m_hat / (jnp.sqrt(v_hat) + ADAM_EPS) + ADAM_WD * w), mn, vn


def sum_adamw_layer(me, landed, partial, w, m, v, layer, prev, name, transposed=False):
    _, rows, cols = landed.shape
    tr = rows // 2 if rows % 32 == 0 and not transposed else rows
    steps = rows // tr

    def body(me_ref, land_ref, own_ref, w_ref, m_ref, v_ref, *rest):
        go_ref, d_ref, mo_ref, vo_ref = rest[-4:]
        who = me_ref[0]
        gv = land_ref[who ^ 1].astype(F32)
        for p in range(2, N_DEV):
            gv = gv + land_ref[who ^ p].astype(F32)
        gv = gv + own_ref[...].astype(F32)
        if transposed:
            gv = gv.T
        go_ref[0] = gv
        d_ref[0], mo_ref[0], vo_ref[0] = _adamw_math(w_ref[0], gv, m_ref[0], v_ref[0])

    if transposed:
        spec3 = pl.BlockSpec((1, cols, rows), lambda r, me_ref: (layer, 0, 0))
    else:
        spec3 = pl.BlockSpec((1, tr, cols), lambda r, me_ref: (layer, r, 0))
    out = jax.ShapeDtypeStruct(w.shape, F32)
    extra = [] if prev is None else list(prev)
    grid_spec = pltpu.PrefetchScalarGridSpec(
        num_scalar_prefetch=1, grid=(steps,),
        in_specs=[pl.BlockSpec((N_DEV, tr, cols), lambda r, me_ref: (0, r, 0)),
                  pl.BlockSpec((tr, cols), lambda r, me_ref: (me_ref[0] * steps + r, 0)),
                  spec3, spec3, spec3] + [_ANY] * len(extra),
        out_specs=[spec3] * 4)
    return pl.pallas_call(
        body, name=name, grid_spec=grid_spec, out_shape=[out] * 4,
        input_output_aliases={6 + j: j for j in range(len(extra))},
        compiler_params=_params(("arbitrary",)),
    )(me, landed, partial, w, m, v, *extra)


def adamw_small(params, name):
    def as2d(a, swap):
        a = jnp.swapaxes(a, -1, -2) if swap else a
        return a.reshape(-1, a.shape[-1]) if a.ndim >= 2 else a.reshape(1, -1)

    n = len(params)
    flat = [as2d(a, swap) for w, g, m, v, swap in params for a in (w, g, m, v)]

    def body(*refs):
        ins, outs = refs[:4 * n], refs[4 * n:]
        for k in range(n):
            w_ref, g_ref, m_ref, v_ref = ins[4 * k:4 * k + 4]
            outs[3 * k][...], outs[3 * k + 1][...], outs[3 * k + 2][...] = _adamw_math(
                w_ref[...], g_ref[...], m_ref[...], v_ref[...])

    outs = pl.pallas_call(
        body, name=name, in_specs=[_VMEM] * (4 * n), out_specs=[_VMEM] * (3 * n),
        out_shape=[jax.ShapeDtypeStruct(flat[4 * k].shape, F32) for k in range(n) for _ in range(3)],
        compiler_params=_params(),
    )(*flat)

    def restore(a, like, swap):
        shape = jnp.swapaxes(like, -1, -2).shape if swap else like.shape
        a = a.reshape(shape)
        return jnp.swapaxes(a, -1, -2) if swap else a

    return [tuple(restore(outs[3 * k + j], params[k][0], params[k][4]) for j in range(3)) for k in range(n)]


def _my_index():
    return 4 * lax.axis_index("x") + 2 * lax.axis_index("y") + lax.axis_index("c")


def _peer(p):
    return (lax.axis_index("x") ^ ((p >> 2) & 1), lax.axis_index("y") ^ ((p >> 1) & 1), lax.axis_index("c") ^ (p & 1))


_HBM = pl.BlockSpec(memory_space=pltpu.HBM)
_SEM = pl.BlockSpec(memory_space=pltpu.SEMAPHORE)
_EFFECT = pltpu.SideEffectType.DATAFLOW_SIDE_EFFECTING


class Exchange:
    RELAYED = (2, 4, 6)

    def __init__(self, srcs, scatter, name, relay=False):
        self.n = n = len(srcs)
        self.scatter = scatter
        self.name = name
        self.relayed = relay
        assert not (relay and scatter)
        self.direct = (1,) + self.RELAYED if relay else tuple(range(1, N_DEV))
        widths = sorted({s.shape[1] for s in srcs}, reverse=True)
        self.ncls = len(widths)
        self.cls = [widths.index(s.shape[1]) for s in srcs]
        self.cnts = [s.shape[0] // N_DEV if scatter else s.shape[0] for s in srcs]
        self.totals = [sum(c for c, k in zip(self.cnts, self.cls) if k == w) for w in range(self.ncls)]
        self.sizer = [max((k for k in range(n) if self.cls[k] == w), key=lambda k: self.cnts[k])
                      for w in range(self.ncls)]
        assert all(N_DEV * self.cnts[self.sizer[w]] >= self.totals[w] for w in range(self.ncls))
        if scatter:
            self.land_shapes = [(N_DEV, c, s.shape[1]) for s, c in zip(srcs, self.cnts)]
        else:
            self.land_shapes = [(N_DEV * c, s.shape[1]) for s, c in zip(srcs, self.cnts)]
        self.dtypes = [s.dtype for s in srcs]

    def _block(self, k, who):
        return pl.ds(pl.multiple_of(who * self.cnts[k], 16), self.cnts[k])

    def _sem(self, p, w):
        return (p - 1) * self.ncls + w

    def start(self, srcs, after):
        n = self.n

        def body(*refs):
            src, land = refs[:n], refs[n:2 * n]
            send_sems, recv_sems = refs[2 * n + 1], refs[2 * n + 2]
            token = refs[-1]
            me = _my_index()
            for p in self.direct:
                for k in range(n):
                    if self.scatter:
                        s_ref, d_ref = src[k].at[self._block(k, me ^ p), :], land[k].at[me]
                    else:
                        s_ref, d_ref = src[k], land[k].at[self._block(k, me), :]
                    pltpu.make_async_remote_copy(
                        src_ref=s_ref, dst_ref=d_ref, send_sem=send_sems.at[self._sem(p, self.cls[k])],
                        recv_sem=recv_sems.at[self._sem(p, self.cls[k])], device_id=_peer(p),
                        device_id_type=MESH).start()
            token[...] = jnp.zeros_like(token)

        sems = pltpu.SemaphoreType.DMA(((N_DEV - 1) * self.ncls,))
        thru = [pltpu.HBM(s.shape, s.dtype) for s in srcs] + [pltpu.HBM(shp, dt) for shp, dt in
                                                               zip(self.land_shapes, self.dtypes)]
        lands = [pltpu.with_memory_space_constraint(lax.empty(shp, dt), pltpu.HBM)
                 for shp, dt in zip(self.land_shapes, self.dtypes)]
        out = pl.pallas_call(
            body, name=self.name + "_start",
            in_specs=[_HBM] * (2 * n) + [_ANY],
            out_shape=[sems, sems] + thru + [jax.ShapeDtypeStruct((8, 128), F32)],
            out_specs=[_SEM, _SEM] + [_HBM] * (2 * n) + [_VMEM],
            input_output_aliases={j: 2 + j for j in range(2 * n)},
            compiler_params=pltpu.CompilerParams(has_side_effects=_EFFECT),
        )(*[pltpu.with_memory_space_constraint(s, pltpu.HBM) for s in srcs], *lands, after)
        return out[:-1], out[-1]

    def _span_copy(self, src, land, w, send_sem, recv_sem, p):
        big = src[self.sizer[w]] if self.scatter else land[self.sizer[w]]
        span = big.at[pl.ds(0, self.totals[w]), :]
        return pltpu.make_async_remote_copy(src_ref=span, dst_ref=span, send_sem=send_sem, recv_sem=recv_sem,
                                            device_id=_peer(p), device_id_type=MESH)

    def relay(self, state, after):
        n = self.n
        send_sems, recv_sems = state[0], state[1]
        thru = state[2:]
        after = list(after) if isinstance(after, (list, tuple)) else [after]
        first_out = 2 * n + 2 + len(after)

        def body(*refs):
            land = refs[n:2 * n]
            send_a, recv_a = refs[2 * n], refs[2 * n + 1]
            send_b, recv_b = refs[first_out], refs[first_out + 1]
            refs[-1][...] = jnp.zeros_like(refs[-1])
            me = _my_index()
            for p in self.RELAYED:
                for w in range(self.ncls):
                    self._span_copy(None, land, w, send_a.at[self._sem(p, w)], recv_a.at[self._sem(p, w)], p).wait_recv()
            for j, p in enumerate(self.RELAYED):
                for k in range(n):
                    rows = land[k].at[self._block(k, me ^ p), :]
                    pltpu.make_async_remote_copy(
                        src_ref=rows, dst_ref=rows, send_sem=send_b.at[j * self.ncls + self.cls[k]],
                        recv_sem=recv_b.at[j * self.ncls + self.cls[k]], device_id=_peer(1),
                        device_id_type=MESH).start()

        sems = pltpu.SemaphoreType.DMA((len(self.RELAYED) * self.ncls,))
        out = pl.pallas_call(
            body, name=self.name + "_relay",
            in_specs=[_HBM] * (2 * n) + [_SEM, _SEM] + [_ANY] * len(after),
            out_shape=[sems, sems] + [pltpu.HBM(a.shape, a.dtype) for a in thru] + [jax.ShapeDtypeStruct((8, 128), F32)],
            out_specs=[_SEM, _SEM] + [_HBM] * (2 * n) + [_VMEM],
            input_output_aliases={j: 2 + j for j in range(2 * n)},
            compiler_params=pltpu.CompilerParams(has_side_effects=_EFFECT),
        )(*thru, send_sems, recv_sems, *after)
        return [send_sems, recv_sems] + list(out[2:-1]) + [out[0], out[1]], out[-1]

    def wait(self, state, after):
        n = self.n
        send_sems, recv_sems = state[0], state[1]
        thru = state[2:2 + 2 * n]
        relay_sems = list(state[2 + 2 * n:])
        assert len(relay_sems) == (2 if self.relayed else 0)
        after = list(after) if isinstance(after, (list, tuple)) else [after]

        def body(*refs):
            src, land = refs[:n], refs[n:2 * n]
            send_a, recv_a = refs[2 * n], refs[2 * n + 1]
            for p in self.direct:
                for w in range(self.ncls):
                    copy = self._span_copy(src, land, w, send_a.at[self._sem(p, w)], recv_a.at[self._sem(p, w)], p)
                    copy.wait_send()
                    if not (self.relayed and p in self.RELAYED):
                        copy.wait_recv()
            if self.relayed:
                send_b, recv_b = refs[2 * n + 2], refs[2 * n + 3]
                for j in range(len(self.RELAYED)):
                    for w in range(self.ncls):
                        copy = self._span_copy(src, land, w, send_b.at[j * self.ncls + w],
                                               recv_b.at[j * self.ncls + w], 1)
                        copy.wait_send()
                        copy.wait_recv()

        out = pl.pallas_call(
            body, name=self.name + "_wait",
            in_specs=[_HBM] * (2 * n) + [_SEM] * (2 + len(relay_sems)) + [_ANY] * len(after),
            out_shape=[pltpu.HBM(a.shape, a.dtype) for a in thru], out_specs=[_HBM] * (2 * n),
            input_output_aliases={j: j for j in range(2 * n)},
            compiler_params=pltpu.CompilerParams(has_side_effects=_EFFECT),
        )(*thru, send_sems, recv_sems, *relay_sems, *after)
        return out[:n], out[n:]

    def place(self, lands, srcs):
        n = self.n
        assert not self.scatter

        def body(*refs):
            src, land = refs[n:2 * n], refs[2 * n:3 * n]
            bufs, sems = refs[3 * n:4 * n], refs[-1]
            me = _my_index()
            loads = [pltpu.make_async_copy(src[k], bufs[k], sems.at[k]) for k in range(n)]
            stores = [pltpu.make_async_copy(bufs[k], land[k].at[self._block(k, me), :], sems.at[k]) for k in range(n)]
            for cp in loads:
                cp.start()
            for k in range(n):
                loads[k].wait()
                stores[k].start()
            for cp in stores:
                cp.wait()

        return pl.pallas_call(
            body, name=self.name + "_place", in_specs=[_ANY] * (2 * n), out_specs=[_ANY] * n,
            out_shape=[jax.ShapeDtypeStruct(a.shape, a.dtype) for a in lands],
            input_output_aliases={j: j for j in range(n)},
            scratch_shapes=[pltpu.VMEM(s.shape, s.dtype) for s in srcs] + [pltpu.SemaphoreType.DMA((n,))],
        )(*lands, *srcs)


def sum_slots(slots, name):
    _, rows, cols = slots.shape
    tr = rows
    if rows > 512:
        for cand in (256, 128, 64, 32, 16, 8):
            if rows % cand == 0:
                tr = cand
                break

    def body(s_ref, o_ref):
        acc = s_ref[0].astype(F32)
        for j in range(1, N_DEV):
            acc = acc + s_ref[j].astype(F32)
        o_ref[...] = acc

    return pl.pallas_call(
        body, name=name, grid=(rows // tr,),
        in_specs=[pl.BlockSpec((N_DEV, tr, cols), lambda i: (0, i, 0))], out_specs=_row_spec(tr, cols),
        out_shape=jax.ShapeDtypeStruct((rows, cols), F32), compiler_params=_params(("arbitrary",)),
    )(slots)


BIG_N = ("ffn1_w_down", "w_out", "ffn2_w_down")
SMALL = ("ffn1_norm", "mix_norm", "attn_sinks", "ssm_a_re", "ssm_a_im", "ssm_log_dt", "ssm_b_re", "ssm_b_im",
         "ssm_c_re", "ssm_c_im", "ssm_d", "ffn2_norm", "final_norm")
PARTS = {"ffn1": ("ffn1_w_gate", "ffn1_w_up", "ffn1_w_down"),
         "mix": ("w_in", "w_out", "w_attn_proj", "w_glu_v", "w_glu_g"),
         "ffn2": ("ffn2_w_gate", "ffn2_w_up", "ffn2_w_down")}


def _to_rows(name, a):
    return a if name in BIG_N else jnp.swapaxes(a, -1, -2)


def local_step(x, tgt, get_weights, put_grads, small):
    seq, d = x.shape
    t = PAD_FRONT + N_META + seq
    cos_t, sin_t = rope_tables(t)
    row = lambda a: a.reshape(1, -1)
    tables = []
    for i in range(DEPTH):
        b_re_t = jnp.swapaxes(small["ssm_b_re"][i], 1, 2)
        b_im_t = jnp.swapaxes(small["ssm_b_im"][i], 1, 2)
        lam_re, lam_im, bbar_re, bbar_im = ssm_prep(small["ssm_a_re"][i], small["ssm_a_im"][i],
                                                    small["ssm_log_dt"][i].reshape(-1, 1), b_re_t, b_im_t, f"ssm_prep_{i}")
        tables.append(((b_re_t, b_im_t),
                       (row(lam_re), row(lam_im), _block_diag_b(bbar_re).astype(BF16), _block_diag_b(bbar_im).astype(BF16),
                        _block_diag_c(small["ssm_c_re"][i]).astype(BF16), _block_diag_c(small["ssm_c_im"][i]).astype(BF16),
                        row(small["ssm_d"][i]))))
    early = [cos_t, sin_t] + [a for _, tab in tables for a in tab[2:6]]
    saved = []
    h = None
    for i in range(DEPTH):
        s = {}
        w = dict(get_weights(i, "ffn1", early if i == 0 else h))
        if i == 0:
            head = jnp.concatenate([jnp.zeros((PAD_FRONT, d), F32), w["meta_tokens"]], axis=0)
            s["h0"], h, s["n1"], s["acts1"] = ffn_fwd(None, row(small["ffn1_norm"][i]), w["ffn1_w_gate"],
                                                      w["ffn1_w_up"], w["ffn1_w_down"], f"ffn1_fwd_{i}", first=(x, head))
        else:
            s["h0"] = h
            h, s["n1"], s["acts1"] = ffn_fwd(h, row(small["ffn1_norm"][i]), w["ffn1_w_gate"], w["ffn1_w_up"],
                                             w["ffn1_w_down"], f"ffn1_fwd_{i}")
        s["h1"] = h
        w.update(get_weights(i, "mix", h))
        s["n2"], s["qkv"], s["u"], s["gates"] = win_fwd(h, row(small["mix_norm"][i]), w["w_in"], cos_t, sin_t,
                                                        f"win_fwd_{i}")
        s["b_t"], s["ssm"] = tables[i]
        s["yg"], s["h_re"], s["h_im"] = ssm_fwd(s["u"], *s["ssm"], f"ssm_fwd_{i}")
        s["o"] = attn_fwd(s["qkv"], row(small["attn_sinks"][i]), f"attn_fwd_{i}")
        h, s["merged"], s["att"], s["sv"], s["sg"] = merge_fwd(
            h, s["o"], s["yg"], s["gates"], w["w_attn_proj"], w["w_glu_v"], w["w_glu_g"], w["w_out"],
            f"merge_fwd_{i}")
        s["h2"] = h
        w.update(get_weights(i, "ffn2", h))
        h, s["n3"], s["acts3"] = ffn_fwd(h, row(small["ffn2_norm"][i]), w["ffn2_w_gate"], w["ffn2_w_up"],
                                               w["ffn2_w_down"], f"ffn2_fwd_{i}")
        s["w"] = w
        saved.append(s)

    loss, dh, d_final = head_fwd_bwd(h, row(small["final_norm"]), tgt)
    gs = {k: [None] * DEPTH for k in SMALL if k != "final_norm"}
    dep = loss
    for i in reversed(range(DEPTH)):
        s = saved[i]
        w = s["w"]
        dh, da, db, sact, dhb, dg = ffn_bwd(dh, s["h2"], row(small["ffn2_norm"][i]), s["acts3"], w["ffn2_w_gate"],
                                            w["ffn2_w_up"], w["ffn2_w_down"], dep, f"ffn2_bwd_{i}")
        gs["ffn2_norm"][i] = dg[0]
        dep = put_grads(i, "ffn2", {"ffn2_w_gate": tn_matmul(da, s["n3"], f"ffn2_dwg_{i}"),
                                    "ffn2_w_up": tn_matmul(db, s["n3"], f"ffn2_dwu_{i}"),
                                    "ffn2_w_down": tn_matmul(sact, dhb, f"ffn2_dwd_{i}")})

        dgates, datt, dsv, dsg, do, dyg, dhb = merge_bwd(dh, s["gates"], s["att"], s["sv"], s["sg"], w["w_attn_proj"],
                                                         w["w_glu_v"], w["w_glu_g"], w["w_out"], dep, f"merge_bwd_{i}")
        gmix = {"w_out": tn_matmul(s["merged"], dhb, f"dwout_{i}"),
                "w_attn_proj": tn_matmul(datt, s["o"], f"dwap_{i}"),
                "w_glu_v": tn_matmul(dsv, s["yg"], f"dwv_{i}"),
                "w_glu_g": tn_matmul(dsg, s["yg"], f"dwgg_{i}")}
        dqkv, dsink = attn_bwd(s["qkv"], do, row(small["attn_sinks"][i]), cos_t, sin_t, f"attn_bwd_{i}")
        gs["attn_sinks"][i] = dsink[:, 0]
        du, dl_re, dl_im, dbb_re, dbb_im, dcc_re, dcc_im, dd = ssm_bwd(dyg, s["u"], s["h_re"], s["h_im"], *s["ssm"],
                                                                      f"ssm_bwd_{i}")
        fold = lambda a: jnp.sum(a, axis=0).reshape(SSM_GROUPS, SSM_STATE)
        da_re, da_im, dldt, db_re_t, db_im_t = ssm_prep_bwd(
            small["ssm_a_re"][i], small["ssm_a_im"][i], small["ssm_log_dt"][i].reshape(-1, 1), *s["b_t"],
            fold(dl_re), fold(dl_im), _diag_of_b(dbb_re), _diag_of_b(dbb_im), f"ssm_prep_bwd_{i}")
        gs["ssm_a_re"][i], gs["ssm_a_im"][i], gs["ssm_log_dt"][i] = da_re, da_im, dldt[:, 0]
        gs["ssm_b_re"][i], gs["ssm_b_im"][i] = jnp.swapaxes(db_re_t, 1, 2), jnp.swapaxes(db_im_t, 1, 2)
        gs["ssm_c_re"][i], gs["ssm_c_im"][i] = _diag_of_c(dcc_re), _diag_of_c(dcc_im)
        gs["ssm_d"][i] = dd[0]
        gmix["w_in"] = tn_matmul([dqkv, du, dgates], s["n2"], f"dwin_{i}")
        dep = put_grads(i, "mix", gmix)
        dh, dg = win_bwd(dh, s["h1"], row(small["mix_norm"][i]), dqkv, du, dgates, w["w_in"], dep, f"win_bwd_{i}")
        gs["mix_norm"][i] = dg[0]

        dh, da, db, sact, dhb, dg = ffn_bwd(dh, s["h0"], row(small["ffn1_norm"][i]), s["acts1"], w["ffn1_w_gate"],
                                            w["ffn1_w_up"], w["ffn1_w_down"], dep, f"ffn1_bwd_{i}")
        gs["ffn1_norm"][i] = dg[0]
        if i > 0:
            dep = put_grads(i, "ffn1", {"ffn1_w_gate": tn_matmul(da, s["n1"], f"ffn1_dwg_{i}"),
                                        "ffn1_w_up": tn_matmul(db, s["n1"], f"ffn1_dwu_{i}"),
                                        "ffn1_w_down": tn_matmul(sact, dhb, f"ffn1_dwd_{i}")})
        else:
            for k, xa, ya in (("ffn1_w_down", sact, dhb), ("ffn1_w_gate", da, s["n1"]), ("ffn1_w_up", db, s["n1"])):
                dep = put_grads(i, "ffn1", {k: tn_matmul(xa, ya, f"d_{k}_{i}", dep)})

    gs = {k: jnp.stack(v) for k, v in gs.items()}
    gs["final_norm"] = d_final[0]
    return loss[0, 0], dh[PAD_FRONT + N_META:], dh[PAD_FRONT:PAD_FRONT + N_META], gs, dep


def _pack_rows(arrays, cols):
    flat = jnp.concatenate([a.reshape(-1) for a in arrays])
    rows = -(-flat.shape[0] // cols)
    rows = -(-rows // 16) * 16
    return jnp.pad(flat, (0, rows * cols - flat.shape[0])).reshape(rows, cols)


def _unpack_rows(packed, shapes):
    flat = packed.reshape(-1)
    out, off = [], 0
    for shp in shapes:
        n = math.prod(shp)
        out.append(flat[off:off + n].reshape(shp))
        off += n
    return out


def kernel(x, meta_tokens, ffn1_norm, ffn1_w_gate, ffn1_w_up, ffn1_w_down, mix_norm, w_in, attn_sinks, ssm_a_re, ssm_a_im, ssm_log_dt, ssm_b_re, ssm_b_im, ssm_c_re, ssm_c_im, ssm_d, w_attn_proj, w_glu_v, w_glu_g, w_out, ffn2_norm, ffn2_w_gate, ffn2_w_up, ffn2_w_down, final_norm, loss_target, m_meta_tokens, m_ffn1_norm, m_ffn1_w_gate, m_ffn1_w_up, m_ffn1_w_down, m_mix_norm, m_w_in, m_attn_sinks, m_ssm_a_re, m_ssm_a_im, m_ssm_log_dt, m_ssm_b_re, m_ssm_b_im, m_ssm_c_re, m_ssm_c_im, m_ssm_d, m_w_attn_proj, m_w_glu_v, m_w_glu_g, m_w_out, m_ffn2_norm, m_ffn2_w_gate, m_ffn2_w_up, m_ffn2_w_down, m_final_norm, v_meta_tokens, v_ffn1_norm, v_ffn1_w_gate, v_ffn1_w_up, v_ffn1_w_down, v_mix_norm, v_w_in, v_attn_sinks, v_ssm_a_re, v_ssm_a_im, v_ssm_log_dt, v_ssm_b_re, v_ssm_b_im, v_ssm_c_re, v_ssm_c_im, v_ssm_d, v_w_attn_proj, v_w_glu_v, v_w_glu_g, v_w_out, v_ffn2_norm, v_ffn2_w_gate, v_ffn2_w_up, v_ffn2_w_down, v_final_norm):
    names = ("meta_tokens", "ffn1_norm", "ffn1_w_gate", "ffn1_w_up", "ffn1_w_down", "mix_norm", "w_in", "attn_sinks",
             "ssm_a_re", "ssm_a_im", "ssm_log_dt", "ssm_b_re", "ssm_b_im", "ssm_c_re", "ssm_c_im", "ssm_d",
             "w_attn_proj", "w_glu_v", "w_glu_g", "w_out", "ffn2_norm", "ffn2_w_gate", "ffn2_w_up", "ffn2_w_down",
             "final_norm")
    weights = dict(zip(names, (meta_tokens, ffn1_norm, ffn1_w_gate, ffn1_w_up, ffn1_w_down, mix_norm, w_in, attn_sinks, ssm_a_re, ssm_a_im, ssm_log_dt, ssm_b_re, ssm_b_im, ssm_c_re, ssm_c_im, ssm_d, w_attn_proj, w_glu_v, w_glu_g, w_out, ffn2_norm, ffn2_w_gate, ffn2_w_up, ffn2_w_down, final_norm)))
    moments_m = dict(zip(names, (m_meta_tokens, m_ffn1_norm, m_ffn1_w_gate, m_ffn1_w_up, m_ffn1_w_down, m_mix_norm, m_w_in, m_attn_sinks, m_ssm_a_re, m_ssm_a_im, m_ssm_log_dt, m_ssm_b_re, m_ssm_b_im, m_ssm_c_re, m_ssm_c_im, m_ssm_d, m_w_attn_proj, m_w_glu_v, m_w_glu_g, m_w_out, m_ffn2_norm, m_ffn2_w_gate, m_ffn2_w_up, m_ffn2_w_down, m_final_norm)))
    moments_v = dict(zip(names, (v_meta_tokens, v_ffn1_norm, v_ffn1_w_gate, v_ffn1_w_up, v_ffn1_w_down, v_mix_norm, v_w_in, v_attn_sinks, v_ssm_a_re, v_ssm_a_im, v_ssm_log_dt, v_ssm_b_re, v_ssm_b_im, v_ssm_c_re, v_ssm_c_im, v_ssm_d, v_w_attn_proj, v_w_glu_v, v_w_glu_g, v_w_out, v_ffn2_norm, v_ffn2_w_gate, v_ffn2_w_up, v_ffn2_w_down, v_final_norm)))
    me = _my_index()

    order = [(i, part) for i in range(DEPTH) for part in PARTS]
    gathers = {}
    token = jnp.zeros((8, 128), F32)
    for i, part in order:
        shards = [_to_rows(k, weights[k][i]).astype(BF16) for k in PARTS[part]]
        if (i, part) == order[0]:
            shards.append(meta_tokens)
        ex = Exchange(shards, False, f"gather_{part}_{i}", relay=True)
        state, token = ex.start(shards, token)
        gathers[i, part] = [ex, state, False]
    all_started = token

    def relay(group, after):
        ex, state, relayed = gathers[group]
        if relayed:
            return []
        new_state, relay_token = ex.relay(state, after)
        gathers[group][1:] = [new_state, True]
        return [relay_token]

    def get_weights(i, part, after):
        g = order.index((i, part))
        after = [all_started] + list(after) if g == 0 else [after]
        tokens = relay(order[g], after)
        if g >= 2 and g + 1 < len(order):
            tokens += relay(order[g + 1], after)
        ex, state, _ = gathers[i, part]
        shards, lands = ex.wait(state, after + tokens)
        fulls = ex.place(lands, shards)
        got = dict(zip(PARTS[part], fulls))
        if (i, part) == (0, "ffn1"):
            got["meta_tokens"] = jnp.swapaxes(fulls[-1].reshape(N_DEV, N_META, 128), 0, 1).reshape(N_META, D_MODEL)
        return got

    scatters = []

    def put_grads(i, part, gdict):
        ks = list(gdict)
        srcs = [gdict[k] for k in ks]
        ex = Exchange(srcs, True, f"scatter_{part if len(ks) > 1 else ks[0]}_{i}")
        state, tok = ex.start(srcs, all_started)
        scatters.append((i, ks, ex, state))
        return tok

    small = {k: weights[k] for k in SMALL}
    loss, dx, dmeta, gs, last_started = local_step(x[0], loss_target[0], get_weights, put_grads, small)

    grads, deltas, new_m, new_v = {}, {}, {}, {}
    small_list = [loss.reshape(1), dmeta] + [gs[k] for k in SMALL]
    packed = _pack_rows(small_list, D_MODEL)
    small_ex = Exchange([packed], False, "gather_small", relay=True)
    small_state, after = small_ex.start([packed], last_started)

    updated = {}
    me_index = jnp.reshape(me, (1,)).astype(jnp.int32)
    for i, ks, ex, state in scatters:
        partials, lands = ex.wait(state, after)
        for k, partial, slots in zip(ks, partials, lands):
            own_layout = weights[k].shape[-1] % 128 == 0 and k not in BIG_N
            view = (lambda a: a) if own_layout else (lambda a: _to_rows(k, a))
            updated[k] = sum_adamw_layer(me_index, slots, partial, view(weights[k]), view(moments_m[k]),
                                         view(moments_v[k]), i, updated.get(k), f"adamw_{k}_{i}", transposed=own_layout)
            after = updated[k][0]
    for k, outs in updated.items():
        own_layout = weights[k].shape[-1] % 128 == 0 and k not in BIG_N
        grads[k], deltas[k], new_m[k], new_v[k] = [a if own_layout else _to_rows(k, a) for a in outs]

    small_state, relayed = small_ex.relay(small_state, after)
    packed_own, packed_all = small_ex.wait(small_state, [after, relayed])
    (packed_all,) = small_ex.place(packed_all, packed_own)
    total = sum_slots(packed_all.reshape(N_DEV, packed.shape[0], D_MODEL), "sum_small")
    pieces = _unpack_rows(total, [a.shape for a in small_list])
    loss_out = pieces[0][0]
    grads["meta_tokens"] = lax.dynamic_slice_in_dim(pieces[1], me * 128, 128, axis=1)
    for k, p in zip(SMALL, pieces[2:]):
        grads[k] = p
    small_names = ("meta_tokens",) + SMALL
    updates = adamw_small([(weights[k], grads[k], moments_m[k], moments_v[k], k in ("ssm_b_re", "ssm_b_im"))
                           for k in small_names], "adamw_small")
    for k, (d, mn, vn) in zip(small_names, updates):
        deltas[k], new_m[k], new_v[k] = d, mn, vn
    return (loss_out, dx[None], *[grads[k] for k in names], *[deltas[k] for k in names],
            *[new_m[k] for k in names], *[new_v[k] for k in names])
```

```python
import math

import jax
import jax.numpy as jnp
from jax import lax
from jax.experimental import pallas as pl
from jax.experimental.pallas import tpu as pltpu

F32 = jnp.float32
BF16 = jnp.bfloat16

D_MODEL = 1024
DEPTH = 2
N_META = 16
HEAD_DIM = 64
N_Q_HEADS = 8
ATTN_WIDTH = 512
KV_WIDTH = 128
QKV_WIDTH = ATTN_WIDTH + 2 * KV_WIDTH
WINDOW = 128
BLK = 128
ROPE_THETA = 500000.0
ROT_DIM = 16
SSM_WIDTH = 512
SSM_GROUP = 16
SSM_GROUPS = 32
SSM_STATE = 64
STATE_WIDTH = SSM_GROUPS * SSM_STATE
D_FF = 2816
IN_WIDTH = 3328
EPS = 1e-6
NEG_INF = -1e30
PAD_FRONT = (-N_META) % BLK
N_DEV = 8

ADAM_LR = 0.001
ADAM_B1 = 0.9
ADAM_B2 = 0.999
ADAM_EPS = 1e-08
ADAM_WD = 0.01
ADAM_STEP = 10

VMEM_LIMIT = 56 * 1024 * 1024
TOKEN_TILE = 384
FF_COLS = 256
_VMEM = pl.BlockSpec(memory_space=pltpu.VMEM)
_SMEM = pl.BlockSpec(memory_space=pltpu.SMEM)
_ANY = pl.BlockSpec(memory_space=pl.ANY)
MESH = pl.DeviceIdType.MESH


def _params(sem=None):
    return pltpu.CompilerParams(dimension_semantics=sem, vmem_limit_bytes=VMEM_LIMIT)


def _nt(a, b):
    return lax.dot_general(a, b, (((1,), (1,)), ((), ())), preferred_element_type=F32)


def _nn(a, b):
    return jnp.dot(a, b, preferred_element_type=F32)


def _tn(a, b):
    return lax.dot_general(a, b, (((0,), (0,)), ((), ())), preferred_element_type=F32)


def _row_spec(tm, width):
    return pl.BlockSpec((tm, width), lambda i: (i, 0))


def _acc_spec(shape):
    return pl.BlockSpec(shape, lambda i: (0,) * len(shape))


def _rms_stats(x):
    r = lax.rsqrt(jnp.mean(x * x, axis=-1, keepdims=True) + EPS)
    return x * r, r


def _rms_bwd(dn, xh, r, g):
    dg = jnp.sum(dn * xh, axis=0, keepdims=True)
    dxh = dn * g
    dx = r * (dxh - xh * jnp.mean(dxh * xh, axis=-1, keepdims=True))
    return dx, dg


def ffn_fwd(h, g, wg_t, wu_t, wd, name, first=None):
    f = wd.shape[0]
    tm = TOKEN_TILE
    if first is None:
        t, d = h.shape
        rows_in, row_specs = [h], [_row_spec(tm, d)]
    else:
        x_in, head = first
        d = x_in.shape[1]
        t = x_in.shape[0] + BLK
        per_tile, last = tm // BLK, x_in.shape[0] // BLK - 1
        rows_in = [head] + [x_in] * per_tile
        row_specs = [_acc_spec((BLK, d))] + [
            pl.BlockSpec((BLK, d), lambda i, k=k: (jnp.clip(i * per_tile - 1 + k, 0, last), 0)) for k in range(per_tile)]
    n_in = len(rows_in)

    def body(*refs):
        g_ref, wg_ref, wu_ref, wd_ref = refs[n_in:n_in + 4]
        ho_ref, n_ref, sl_ref, p_ref, s_ref = refs[-5:]
        if first is None:
            x = refs[0][...]
        else:
            blocks = [ref[...] for ref in refs[1:n_in]]
            blocks[0] = jnp.where(pl.program_id(0) == 0, refs[0][...], blocks[0])
            x = jnp.concatenate(blocks, axis=0)
            refs[n_in + 4][...] = x
        xh, _ = _rms_stats(x)
        n = (xh * g_ref[...]).astype(BF16)
        n_ref[...] = n
        a = _nt(n, wg_ref[...])
        b = _nt(n, wu_ref[...])
        for c in range(f // FF_COLS):
            cols = slice(c * FF_COLS, (c + 1) * FF_COLS)
            ac, bc = a[:, cols], b[:, cols]
            sig = jax.nn.sigmoid(ac)
            sl = ac * sig
            sl_ref[:, cols] = sl.astype(BF16)
            p_ref[:, cols] = (bc * (sig + sl * (1.0 - sig))).astype(BF16)
            s_ref[:, cols] = (sl * bc).astype(BF16)
        ho_ref[...] = x + 0.5 * _nn(s_ref[...], wd_ref[...])

    extra_specs = [] if first is None else [_row_spec(tm, d)]
    extra_shapes = [] if first is None else [jax.ShapeDtypeStruct((t, d), F32)]
    *h_in, ho, n, sl, p, s = pl.pallas_call(
        body, name=name, grid=(t // tm,),
        in_specs=row_specs + [_acc_spec((1, d)), _VMEM, _VMEM, _VMEM],
        out_specs=extra_specs + [_row_spec(tm, d), _row_spec(tm, d), _row_spec(tm, f), _row_spec(tm, f),
                                 _row_spec(tm, f)],
        out_shape=extra_shapes + [jax.ShapeDtypeStruct((t, d), F32), jax.ShapeDtypeStruct((t, d), BF16),
                                  jax.ShapeDtypeStruct((t, f), BF16), jax.ShapeDtypeStruct((t, f), BF16),
                                  jax.ShapeDtypeStruct((t, f), BF16)],
        compiler_params=_params(("arbitrary",)),
    )(*rows_in, g, wg_t, wu_t, wd)
    return (*h_in, ho, n, (sl, p, s))


def ffn_bwd(dh, h, g, acts, wg_t, wu_t, wd, dep, name):
    t, d = h.shape
    f = wd.shape[0]
    tm = TOKEN_TILE
    sl, p, s = acts

    def hidden_body(dh_ref, sl_ref, p_ref, wd_ref, dep_ref, da_ref, db_ref, dhb_ref):
        dhb = (0.5 * dh_ref[...]).astype(BF16)
        dhb_ref[...] = dhb
        ds = _nt(dhb, wd_ref[...])
        da_ref[...] = (ds * p_ref[...].astype(F32)).astype(BF16)
        db_ref[...] = (ds * sl_ref[...].astype(F32)).astype(BF16)

    da, db, dhb = pl.pallas_call(
        hidden_body, name=name + "_h", grid=(t // tm,),
        in_specs=[_row_spec(tm, d), _row_spec(tm, f), _row_spec(tm, f), _VMEM, _ANY],
        out_specs=[_row_spec(tm, f), _row_spec(tm, f), _row_spec(tm, d)],
        out_shape=[jax.ShapeDtypeStruct((t, f), BF16), jax.ShapeDtypeStruct((t, f), BF16),
                   jax.ShapeDtypeStruct((t, d), BF16)],
        compiler_params=_params(("arbitrary",)),
    )(dh, sl, p, wd, dep)

    def input_body(dh_ref, h_ref, g_ref, da_ref, db_ref, wg_ref, wu_ref, dhi_ref, dg_ref):
        dn = _nn(da_ref[...], wg_ref[...]) + _nn(db_ref[...], wu_ref[...])
        xh, r = _rms_stats(h_ref[...])
        dx, dg = _rms_bwd(dn, xh, r, g_ref[...])
        dhi_ref[...] = dh_ref[...] + dx

        @pl.when(pl.program_id(0) == 0)
        def _():
            dg_ref[...] = jnp.zeros_like(dg_ref)

        dg_ref[...] += dg

    dhi, dg = pl.pallas_call(
        input_body, name=name + "_x", grid=(t // tm,),
        in_specs=[_row_spec(tm, d), _row_spec(tm, d), _acc_spec((1, d)), _row_spec(tm, f), _row_spec(tm, f),
                  _VMEM, _VMEM],
        out_specs=[_row_spec(tm, d), _acc_spec((1, d))],
        out_shape=[jax.ShapeDtypeStruct((t, d), F32), jax.ShapeDtypeStruct((1, d), F32)],
        compiler_params=_params(("arbitrary",)),
    )(dh, h, g, da, db, wg_t, wu_t)
    return dhi, da, db, s, dhb, dg


DW_TILE = 256


def tn_matmul(x, y, name, dep=None):
    xs = list(x) if isinstance(x, (list, tuple)) else [x]
    t = xs[0].shape[0]
    n = y.shape[1]
    bm = DW_TILE
    tiles = [a.shape[1] // bm for a in xs]
    offs = [sum(tiles[:k]) for k in range(len(xs))]
    deps = [] if dep is None else [dep]

    def body(*refs):
        y_ref, o_ref = refs[len(xs)], refs[-1]
        i = pl.program_id(0)
        for k in range(len(xs)):
            @pl.when((i >= offs[k]) & (i < offs[k] + tiles[k]))
            def _(k=k):
                o_ref[...] = _tn(refs[k][...], y_ref[...]).astype(BF16)

    def x_spec(k):
        return pl.BlockSpec((t, bm), lambda i: (0, jnp.clip(i - offs[k], 0, tiles[k] - 1)))

    return pl.pallas_call(
        body, name=name, grid=(sum(tiles),),
        in_specs=[x_spec(k) for k in range(len(xs))] + [_VMEM] + [_ANY] * len(deps),
        out_specs=pl.BlockSpec((bm, n), lambda i: (i, 0)),
        out_shape=jax.ShapeDtypeStruct((sum(tiles) * bm, n), BF16),
        compiler_params=_params(("arbitrary",)),
    )(*xs, y, *deps)


def head_fwd_bwd(h, g, tgt):
    t, d = h.shape
    tm = TOKEN_TILE
    per_tile = tm // BLK
    last = tgt.shape[0] // BLK - 1

    def body(h_ref, g_ref, *rest):
        t_refs, (loss_ref, dh_ref, dg_ref) = rest[:per_tile], rest[per_tile:]
        i = pl.program_id(0)
        xh, r = _rms_stats(h_ref[...])
        gv = g_ref[...]
        target = jnp.concatenate([ref[...] for ref in t_refs], axis=0)
        row = i * tm + lax.broadcasted_iota(jnp.int32, (tm, 1), 0)
        e = jnp.where(row >= BLK, xh * gv - target, 0.0)
        dx, dg = _rms_bwd(e * (1.0 / d), xh, r, gv)
        dh_ref[...] = dx

        @pl.when(i == 0)
        def _():
            dg_ref[...] = jnp.zeros_like(dg_ref)
            loss_ref[...] = jnp.zeros_like(loss_ref)

        dg_ref[...] += dg
        loss_ref[...] += jnp.sum(e * e) * (0.5 / d)

    def target_spec(k):
        return pl.BlockSpec((BLK, d), lambda i: (jnp.clip(i * per_tile - 1 + k, 0, last), 0))

    return pl.pallas_call(
        body, name="head", grid=(t // tm,),
        in_specs=[_row_spec(tm, d), _acc_spec((1, d))] + [target_spec(k) for k in range(per_tile)],
        out_specs=[_acc_spec((1, 128)), _row_spec(tm, d), _acc_spec((1, d))],
        out_shape=[jax.ShapeDtypeStruct((1, 128), F32), jax.ShapeDtypeStruct((t, d), F32),
                   jax.ShapeDtypeStruct((1, d), F32)],
        compiler_params=_params(("arbitrary",)),
    )(h, g, *[tgt] * per_tile)


def rope_tables(t):
    pos = jnp.arange(t, dtype=F32) - PAD_FRONT
    inv_freq = ROPE_THETA ** (-jnp.arange(0, ROT_DIM, 2, dtype=F32) / ROT_DIM)
    ang = pos[:, None] * inv_freq[None, :]
    cos, sin = jnp.cos(ang), jnp.sin(ang)
    ones = jnp.ones((t, HEAD_DIM - ROT_DIM), F32)
    cos_h = jnp.concatenate([cos, cos, ones], axis=1)
    sin_h = jnp.concatenate([-sin, sin, 0.0 * ones], axis=1)
    return jnp.concatenate([cos_h, cos_h], axis=1), jnp.concatenate([sin_h, sin_h], axis=1)


def _swap_halves(x):
    n = x.shape[1]
    lane = lax.broadcasted_iota(jnp.int32, x.shape, 1)
    return jnp.where(lane % HEAD_DIM < ROT_DIM // 2, pltpu.roll(x, n - ROT_DIM // 2, 1), pltpu.roll(x, ROT_DIM // 2, 1))


def _rope(x, cos_t, sin_t, sign):
    return x * cos_t + sign * (_swap_halves(x) * sin_t)


def win_fwd(h, g, win_t, cos_t, sin_t, name):
    t, d = h.shape
    tm = TOKEN_TILE

    def body(h_ref, g_ref, w_ref, c_ref, s_ref, n_ref, qkv_ref, u_ref, gates_ref):
        xh, _ = _rms_stats(h_ref[...])
        n = (xh * g_ref[...]).astype(BF16)
        n_ref[...] = n
        z = _nt(n, w_ref[...])
        c, s = c_ref[...], s_ref[...]
        for j in range((ATTN_WIDTH + KV_WIDTH) // 128):
            qkv_ref[:, j * 128:(j + 1) * 128] = _rope(z[:, j * 128:(j + 1) * 128], c, s, 1.0).astype(BF16)
        qkv_ref[:, ATTN_WIDTH + KV_WIDTH:QKV_WIDTH] = z[:, ATTN_WIDTH + KV_WIDTH:QKV_WIDTH].astype(BF16)
        for j in range(N_CHUNK):
            u_ref[j] = z[:, QKV_WIDTH + j * U_CHUNK:QKV_WIDTH + (j + 1) * U_CHUNK]
        gates_ref[...] = z[:, QKV_WIDTH + SSM_WIDTH:].astype(BF16)

    return pl.pallas_call(
        body, name=name, grid=(t // tm,),
        in_specs=[_row_spec(tm, d), _acc_spec((1, d)), _VMEM, _row_spec(tm, 128), _row_spec(tm, 128)],
        out_specs=[_row_spec(tm, d), _row_spec(tm, QKV_WIDTH), _chunked_spec(tm, lambda i: i), _row_spec(tm, 2 * d)],
        out_shape=[jax.ShapeDtypeStruct((t, d), BF16), jax.ShapeDtypeStruct((t, QKV_WIDTH), BF16),
                   jax.ShapeDtypeStruct((N_CHUNK, t, U_CHUNK), F32), jax.ShapeDtypeStruct((t, 2 * d), BF16)],
        compiler_params=_params(("arbitrary",)),
    )(h, g, win_t, cos_t, sin_t)


def win_bwd(dh, h, g, dqkv, du, dgates, win_t, dep, name):
    t, d = h.shape
    tm = TOKEN_TILE

    def body(dh_ref, h_ref, g_ref, dqkv_ref, du_ref, dgt_ref, w_ref, dep_ref, dhi_ref, dg_ref):
        dn = (_nn(dqkv_ref[...], w_ref[0:QKV_WIDTH, :])
              + _nn(du_ref[...], w_ref[QKV_WIDTH:QKV_WIDTH + SSM_WIDTH, :])
              + _nn(dgt_ref[...], w_ref[QKV_WIDTH + SSM_WIDTH:, :]))
        xh, r = _rms_stats(h_ref[...])
        dx, dg = _rms_bwd(dn, xh, r, g_ref[...])
        dhi_ref[...] = dh_ref[...] + dx

        @pl.when(pl.program_id(0) == 0)
        def _():
            dg_ref[...] = jnp.zeros_like(dg_ref)

        dg_ref[...] += dg

    return pl.pallas_call(
        body, name=name, grid=(t // tm,),
        in_specs=[_row_spec(tm, d), _row_spec(tm, d), _acc_spec((1, d)), _row_spec(tm, QKV_WIDTH),
                  _row_spec(tm, SSM_WIDTH), _row_spec(tm, 2 * d), _VMEM, _ANY],
        out_specs=[_row_spec(tm, d), _acc_spec((1, d))],
        out_shape=[jax.ShapeDtypeStruct((t, d), F32), jax.ShapeDtypeStruct((1, d), F32)],
        compiler_params=_params(("arbitrary",)),
    )(dh, h, g, dqkv, du, dgates, win_t, dep)


def _attn_mask(blk):
    q_pos = blk * BLK + lax.broadcasted_iota(jnp.int32, (BLK, 3 * BLK), 0) - PAD_FRONT
    col = lax.broadcasted_iota(jnp.int32, (BLK, 3 * BLK), 1)
    part = col // BLK
    k_pos = jnp.where(part == 0, col, (blk + part - 2) * BLK + (col - part * BLK)) - PAD_FRONT
    dist = q_pos - k_pos
    meta_ok = (part == 0) & (k_pos >= 0) & (dist >= 0)
    band_ok = (part > 0) & (k_pos >= N_META) & (dist >= 0) & (dist < WINDOW)
    return meta_ok | band_ok


def _head_halves(x128, kv):
    x = x128.astype(F32)
    lane = lax.broadcasted_iota(jnp.int32, x.shape, 1)
    swapped = pltpu.roll(x, HEAD_DIM, 1)
    lo, hi = (x, swapped) if kv == 0 else (swapped, x)
    return jnp.where(lane < HEAD_DIM, lo, 0.0).astype(BF16), jnp.where(lane >= HEAD_DIM, hi, 0.0).astype(BF16)


def _gather_keys(meta_ref, prev_ref, cur_ref, lo):
    return jnp.concatenate([meta_ref[:, lo:lo + 128], prev_ref[:, lo:lo + 128], cur_ref[:, lo:lo + 128]], axis=0)


def _pair_lanes(kv):
    return slice(2 * kv * 128, (2 * kv + 1) * 128), slice((2 * kv + 1) * 128, (2 * kv + 2) * 128)


def _stacked_sinks(sink_ref, head):
    row = lax.broadcasted_iota(jnp.int32, (2 * BLK, 1), 0)
    return jnp.where(row < BLK, sink_ref[0, head], sink_ref[0, head + 2])


def _softmax_with_sink(s, mask, sink):
    s = jnp.where(mask, s * (HEAD_DIM ** -0.5), NEG_INF)
    m = jnp.maximum(jnp.max(s, axis=-1, keepdims=True), sink)
    p = jnp.exp(s - m)
    p_sink = jnp.exp(sink - m)
    inv = 1.0 / (jnp.sum(p, axis=-1, keepdims=True) + p_sink)
    return p * inv, p_sink * inv


def attn_fwd(qkv, sinks, name):
    t = qkv.shape[0]
    nb = t // BLK

    def body(sink_ref, meta_ref, prev_ref, cur_ref, o_ref):
        blk = pl.program_id(0)
        mask = _attn_mask(blk)
        mask2 = jnp.concatenate([mask, mask], axis=0)
        k128 = _gather_keys(meta_ref, prev_ref, cur_ref, ATTN_WIDTH)
        v128 = _gather_keys(meta_ref, prev_ref, cur_ref, ATTN_WIDTH + KV_WIDTH)
        for kv in range(2):
            k_lo, k_hi = _head_halves(k128, kv)
            v_lo, v_hi = _head_halves(v128, kv)
            lanes0, lanes1 = _pair_lanes(kv)
            q2 = jnp.concatenate([cur_ref[:, lanes0], cur_ref[:, lanes1]], axis=0)
            p_a, _ = _softmax_with_sink(_nt(q2, k_lo), mask2, _stacked_sinks(sink_ref, 4 * kv))
            p_b, _ = _softmax_with_sink(_nt(q2, k_hi), mask2, _stacked_sinks(sink_ref, 4 * kv + 1))
            o2 = (_nn(p_a.astype(BF16), v_lo) + _nn(p_b.astype(BF16), v_hi)).astype(BF16)
            o_ref[:, lanes0] = o2[0:BLK]
            o_ref[:, lanes1] = o2[BLK:2 * BLK]

    blk_spec = lambda f: pl.BlockSpec((BLK, QKV_WIDTH), f)
    return pl.pallas_call(
        body, name=name, grid=(nb,),
        in_specs=[_SMEM, blk_spec(lambda i: (0, 0)), blk_spec(lambda i: (jnp.maximum(i - 1, 0), 0)),
                  blk_spec(lambda i: (i, 0))],
        out_specs=_row_spec(BLK, ATTN_WIDTH),
        out_shape=jax.ShapeDtypeStruct((t, ATTN_WIDTH), BF16),
        compiler_params=_params(("arbitrary",)),
    )(sinks, qkv, qkv, qkv)


def attn_bwd(qkv, do, sinks, cos_t, sin_t, name):
    t = qkv.shape[0]
    nb = t // BLK

    def body(sink_ref, meta_ref, prev_ref, cur_ref, do_ref, c_ref, s_ref, dqkv_ref, dsink_ref, carry_ref, macc_ref):
        step = pl.program_id(0)
        blk = nb - 1 - step

        @pl.when(step == 0)
        def _():
            dsink_ref[...] = jnp.zeros_like(dsink_ref)
            carry_ref[...] = jnp.zeros_like(carry_ref)
            macc_ref[...] = jnp.zeros_like(macc_ref)

        mask = _attn_mask(blk)
        mask2 = jnp.concatenate([mask, mask], axis=0)
        lane = lax.broadcasted_iota(jnp.int32, (3 * BLK, 128), 1)
        k128 = _gather_keys(meta_ref, prev_ref, cur_ref, ATTN_WIDTH)
        v128 = _gather_keys(meta_ref, prev_ref, cur_ref, ATTN_WIDTH + KV_WIDTH)
        cos_b, sin_b = c_ref[...], s_ref[...]
        dk_heads, dv_heads = [], []
        for kv in range(2):
            k_lo, k_hi = _head_halves(k128, kv)
            v_lo, v_hi = _head_halves(v128, kv)
            lanes0, lanes1 = _pair_lanes(kv)
            q2 = jnp.concatenate([cur_ref[:, lanes0], cur_ref[:, lanes1]], axis=0)
            do2 = jnp.concatenate([do_ref[:, lanes0], do_ref[:, lanes1]], axis=0)
            ds_half, p_half = [], []
            for half, (k_h, v_h) in enumerate(((k_lo, v_lo), (k_hi, v_hi))):
                head = 4 * kv + half
                p, p_sink = _softmax_with_sink(_nt(q2, k_h), mask2, _stacked_sinks(sink_ref, head))
                dp = _nt(do2, v_h)
                dsum = jnp.sum(p * dp, axis=-1, keepdims=True)
                ds_half.append((p * (dp - dsum) * (HEAD_DIM ** -0.5)).astype(BF16))
                p_half.append(p.astype(BF16))
                dsink = p_sink * dsum
                for part, h in ((0, head), (1, head + 2)):
                    total = -jnp.sum(dsink[part * BLK:(part + 1) * BLK], axis=0, keepdims=True)
                    dsink_ref[h:h + 1, :] += jnp.broadcast_to(total, (1, 128))
            dq2 = _nn(ds_half[0], k_lo) + _nn(ds_half[1], k_hi)
            dqkv_ref[:, lanes0] = _rope(dq2[0:BLK], cos_b, sin_b, -1.0).astype(BF16)
            dqkv_ref[:, lanes1] = _rope(dq2[BLK:2 * BLK], cos_b, sin_b, -1.0).astype(BF16)
            dk_acc = jnp.where(lane < HEAD_DIM, _tn(ds_half[0], q2), _tn(ds_half[1], q2))
            dv_acc = jnp.where(lane < HEAD_DIM, _tn(p_half[0], do2), _tn(p_half[1], do2))
            dk_heads.append(dk_acc + pltpu.roll(dk_acc, HEAD_DIM, 1))
            dv_heads.append(dv_acc + pltpu.roll(dv_acc, HEAD_DIM, 1))
        dkv = jnp.concatenate([jnp.where(lane < HEAD_DIM, dk_heads[0], dk_heads[1]),
                               jnp.where(lane < HEAD_DIM, dv_heads[0], dv_heads[1])], axis=1)
        macc_ref[...] += dkv[0:BLK]
        is_last = (blk == 0).astype(F32)
        mine = dkv[2 * BLK:3 * BLK] + carry_ref[...] + is_last * macc_ref[...]
        carry_ref[...] = dkv[BLK:2 * BLK]
        dqkv_ref[:, ATTN_WIDTH:ATTN_WIDTH + KV_WIDTH] = _rope(mine[:, 0:128], cos_b, sin_b, -1.0).astype(BF16)
        dqkv_ref[:, ATTN_WIDTH + KV_WIDTH:QKV_WIDTH] = mine[:, 128:256].astype(BF16)

    rev = lambda i: nb - 1 - i
    blk_spec = lambda f: pl.BlockSpec((BLK, QKV_WIDTH), f)
    return pl.pallas_call(
        body, name=name, grid=(nb,),
        in_specs=[_SMEM, blk_spec(lambda i: (0, 0)), blk_spec(lambda i: (jnp.maximum(rev(i) - 1, 0), 0)),
                  blk_spec(lambda i: (rev(i), 0)), pl.BlockSpec((BLK, ATTN_WIDTH), lambda i: (rev(i), 0)),
                  pl.BlockSpec((BLK, 128), lambda i: (rev(i), 0)), pl.BlockSpec((BLK, 128), lambda i: (rev(i), 0))],
        out_specs=[pl.BlockSpec((BLK, QKV_WIDTH), lambda i: (rev(i), 0)), _acc_spec((N_Q_HEADS, 128))],
        out_shape=[jax.ShapeDtypeStruct((t, QKV_WIDTH), BF16), jax.ShapeDtypeStruct((N_Q_HEADS, 128), F32)],
        scratch_shapes=[pltpu.VMEM((BLK, 256), F32), pltpu.VMEM((BLK, 256), F32)],
        compiler_params=_params(("arbitrary",)),
    )(sinks, qkv, qkv, qkv, do, cos_t, sin_t)


def _cmul(ar, ai, br, bi):
    return ar * br - ai * bi, ar * bi + ai * br


def ssm_prep(a_re, a_im, log_dt, b_re_t, b_im_t, name):
    def body(ar_ref, ai_ref, ldt_ref, br_ref, bi_ref, lr_ref, li_ref, bbr_ref, bbi_ref):
        ar, ai = ar_ref[...], ai_ref[...]
        dt = jnp.exp(ldt_ref[...])
        mag = jnp.exp(ar * dt)
        lr = mag * jnp.cos(ai * dt)
        li = mag * jnp.sin(ai * dt)
        den = ar * ar + ai * ai
        nr = lr - 1.0
        cr = ((nr * ar + li * ai) / den)[:, None, :]
        ci = ((li * ar - nr * ai) / den)[:, None, :]
        br, bi = br_ref[...], bi_ref[...]
        lr_ref[...] = lr
        li_ref[...] = li
        bbr_ref[...] = cr * br - ci * bi
        bbi_ref[...] = cr * bi + ci * br

    gp = jax.ShapeDtypeStruct(a_re.shape, F32)
    gcp = jax.ShapeDtypeStruct(b_re_t.shape, F32)
    return pl.pallas_call(body, name=name, out_shape=[gp, gp, gcp, gcp],
                          in_specs=[_VMEM] * 5, out_specs=[_VMEM] * 4)(a_re, a_im, log_dt, b_re_t, b_im_t)


def ssm_prep_bwd(a_re, a_im, log_dt, b_re_t, b_im_t, dl_re, dl_im, dbb_re, dbb_im, name):
    def body(ar_ref, ai_ref, ldt_ref, br_ref, bi_ref, dlr_ref, dli_ref, dbbr_ref, dbbi_ref,
             dar_ref, dai_ref, dldt_ref, dbr_ref, dbi_ref):
        ar, ai = ar_ref[...], ai_ref[...]
        dt = jnp.exp(ldt_ref[...])
        mag = jnp.exp(ar * dt)
        lr = mag * jnp.cos(ai * dt)
        li = mag * jnp.sin(ai * dt)
        den = ar * ar + ai * ai
        nr = lr - 1.0
        cr = (nr * ar + li * ai) / den
        ci = (li * ar - nr * ai) / den
        br, bi = br_ref[...], bi_ref[...]
        dbbr, dbbi = dbbr_ref[...], dbbi_ref[...]
        dbr_ref[...] = cr[:, None, :] * dbbr + ci[:, None, :] * dbbi
        dbi_ref[...] = cr[:, None, :] * dbbi - ci[:, None, :] * dbbr
        dcr = jnp.sum(br * dbbr + bi * dbbi, axis=1)
        dci = jnp.sum(br * dbbi - bi * dbbr, axis=1)
        d_num_r = dcr / den
        d_num_i = dci / den
        d_den = -(dcr * cr + dci * ci) / den
        d_lr = dlr_ref[...] + d_num_r * ar - d_num_i * ai
        d_li = dli_ref[...] + d_num_r * ai + d_num_i * ar
        d_ar = d_num_r * nr + d_num_i * li + d_den * 2.0 * ar
        d_ai = d_num_r * li - d_num_i * nr + d_den * 2.0 * ai
        d_mag = (d_lr * lr + d_li * li) / mag
        d_theta = d_li * lr - d_lr * li
        d_ardt = d_mag * mag
        dar_ref[...] = d_ar + d_ardt * dt
        dai_ref[...] = d_ai + d_theta * dt
        d_dt = jnp.sum(d_ardt * ar + d_theta * ai, axis=1, keepdims=True)
        dldt_ref[...] = d_dt * dt

    gp = jax.ShapeDtypeStruct(a_re.shape, F32)
    gcp = jax.ShapeDtypeStruct(b_re_t.shape, F32)
    return pl.pallas_call(body, name=name, out_shape=[gp, gp, jax.ShapeDtypeStruct(log_dt.shape, F32), gcp, gcp],
                          in_specs=[_VMEM] * 9, out_specs=[_VMEM] * 5,
                          )(a_re, a_im, log_dt, b_re_t, b_im_t, dl_re, dl_im, dbb_re, dbb_im)


N_CHUNK = 4
U_CHUNK = SSM_WIDTH // N_CHUNK
H_CHUNK = STATE_WIDTH // N_CHUNK
SUB = 8


def _block_diag_b(bb):
    x = bb.reshape(N_CHUNK, 8, SSM_GROUP, 1, SSM_STATE)
    same = (jnp.arange(8)[:, None] == jnp.arange(8)[None, :])[None, :, None, :, None]
    return jnp.where(same, x, 0.0).reshape(N_CHUNK, U_CHUNK, H_CHUNK)


def _block_diag_c(c):
    x = jnp.swapaxes(c.reshape(N_CHUNK, 8, SSM_GROUP, SSM_STATE), 2, 3)[:, :, :, None, :]
    same = (jnp.arange(8)[:, None] == jnp.arange(8)[None, :])[None, :, None, :, None]
    return jnp.where(same, x, 0.0).reshape(N_CHUNK, H_CHUNK, U_CHUNK)


def _diag_of_b(m):
    x = m.reshape(N_CHUNK, 8, SSM_GROUP, 8, SSM_STATE)
    return jnp.stack([x[:, g, :, g, :] for g in range(8)], axis=1).reshape(SSM_GROUPS, SSM_GROUP, SSM_STATE)


def _diag_of_c(m):
    x = m.reshape(N_CHUNK, 8, SSM_STATE, 8, SSM_GROUP)
    d = jnp.stack([x[:, g, :, g, :] for g in range(8)], axis=1)
    return jnp.swapaxes(d, 2, 3).reshape(SSM_GROUPS, SSM_GROUP, SSM_STATE)


def _lambda_tables(lr, li, reverse):
    p1 = (lr, li)
    p2 = _cmul(*p1, *p1)
    p4 = _cmul(*p2, *p2)
    rows = [p1]
    for _ in range(SUB - 1):
        rows.append(_cmul(*rows[-1], *p1))
    if reverse:
        rows = rows[::-1]
    return p1, p2, p4, (jnp.concatenate([r[0] for r in rows], axis=0), jnp.concatenate([r[1] for r in rows], axis=0))


def _scan8(xr, xi, pows, table, cr, ci, reverse):
    row = lax.broadcasted_iota(jnp.int32, xr.shape, 0)
    for d, (pr, pi) in zip((1, 2, 4), pows):
        if reverse:
            sr, si = pltpu.roll(xr, SUB - d, 0), pltpu.roll(xi, SUB - d, 0)
            keep = row < SUB - d
        else:
            sr, si = pltpu.roll(xr, d, 0), pltpu.roll(xi, d, 0)
            keep = row >= d
        sr = jnp.where(keep, sr, 0.0)
        si = jnp.where(keep, si, 0.0)
        xr, xi = xr + pr * sr - pi * si, xi + pr * si + pi * sr
    tr, ti = table
    return xr + tr * cr - ti * ci, xi + tr * ci + ti * cr


def _gelu_and_grad(y):
    k0 = math.sqrt(2.0 / math.pi)
    inner = k0 * (y + 0.044715 * y * y * y)
    th = jnp.tanh(inner)
    g = 0.5 * y * (1.0 + th)
    dg = 0.5 * (1.0 + th) + 0.5 * y * (1.0 - th * th) * k0 * (1.0 + 3.0 * 0.044715 * y * y)
    return g, dg


SCAN_TILE = TOKEN_TILE
SEG = SCAN_TILE // SUB
SCAN_LANES = 1024


def _perm_matrix(to_segments):
    a = lax.broadcasted_iota(jnp.int32, (SCAN_TILE, SCAN_TILE), 0)
    b = lax.broadcasted_iota(jnp.int32, (SCAN_TILE, SCAN_TILE), 1)
    rho, time = (a, b) if to_segments else (b, a)
    return (time == (rho % SUB) * SEG + rho // SUB).astype(BF16)


def _chunked_spec(rows, block_of):
    return pl.BlockSpec((N_CHUNK, rows, U_CHUNK), lambda i: (0, block_of(i), 0))


def _load_segments(src_ref, dst_ref):
    for j in range(N_CHUNK):
        for r in range(SEG):
            dst_ref[r * SUB:(r + 1) * SUB, j * U_CHUNK:(j + 1) * U_CHUNK] = src_ref.at[j][pl.ds(r, SUB, stride=SEG), :]


def _power_table(lr, li, pr_ref, pi_ref):
    cur = (lr, li)
    for r in range(SEG):
        pr_ref[r * SUB:(r + 1) * SUB, :] = jnp.broadcast_to(cur[0], (SUB, STATE_WIDTH))
        pi_ref[r * SUB:(r + 1) * SUB, :] = jnp.broadcast_to(cur[1], (SUB, STATE_WIDTH))
        cur = _cmul(*cur, lr, li)


def _table_rows(ref, k, lanes):
    return ref[pl.ds(pl.multiple_of(k * SUB, SUB), SUB), lanes]


def _segment_scan(xr_ref, xi_ref, lanes, lam, table_row, cr_ref, ci_ref, reverse, extra=None):
    lr = jnp.broadcast_to(lam[0], (SUB, SCAN_LANES))
    li = jnp.broadcast_to(lam[1], (SUB, SCAN_LANES))
    row = lax.broadcasted_iota(jnp.int32, (SUB, SCAN_LANES), 0)

    def rows_of(k):
        r = SEG - 1 - k if reverse else k
        return pl.ds(pl.multiple_of(r * SUB, SUB), SUB)

    def first(k, st):
        sr, si = st
        rows = rows_of(k)
        nr = lr * sr - li * si + xr_ref[rows, lanes]
        ni = lr * si + li * sr + xi_ref[rows, lanes]
        xr_ref[rows, lanes] = nr
        xi_ref[rows, lanes] = ni
        return nr, ni

    zero = jnp.zeros((SUB, SCAN_LANES), F32)
    er, ei = lax.fori_loop(0, SEG, first, (zero, zero))
    l16 = table_row(SEG - 1)
    q1, q2, q4, tab = _lambda_tables(l16[0][0:1], l16[1][0:1], reverse)
    c_r, c_i = cr_ref[:, lanes], ci_ref[:, lanes]
    gr, gi = _scan8(er, ei, (q1, q2, q4), tab, c_r, c_i, reverse)
    if reverse:
        cin_r = jnp.where(row == SUB - 1, c_r, pltpu.roll(gr, SUB - 1, 0))
        cin_i = jnp.where(row == SUB - 1, c_i, pltpu.roll(gi, SUB - 1, 0))
        cr_ref[:, lanes] = gr[0:1]
        ci_ref[:, lanes] = gi[0:1]
    else:
        cin_r = jnp.where(row == 0, c_r, pltpu.roll(gr, 1, 0))
        cin_i = jnp.where(row == 0, c_i, pltpu.roll(gi, 1, 0))
        cr_ref[:, lanes] = gr[SUB - 1:SUB]
        ci_ref[:, lanes] = gi[SUB - 1:SUB]

    def second(k, carry):
        rows = rows_of(k)
        tr, ti = table_row(k)
        ar = xr_ref[rows, lanes] + tr * cin_r - ti * cin_i
        ai = xi_ref[rows, lanes] + tr * cin_i + ti * cin_r
        xr_ref[rows, lanes] = ar
        xi_ref[rows, lanes] = ai
        if extra is None:
            return carry
        return extra(rows, carry, ar, ai)

    init = 0 if extra is None else (cin_r, cin_i, zero, zero)
    return lax.fori_loop(0, SEG, second, init)


def ssm_fwd(u, lam_re, lam_im, bb_re, bb_im, cc_re, cc_im, d_skip, name):
    t = u.shape[1]
    tt = SCAN_TILE

    def body(u_ref, lr_ref, li_ref, bbr_ref, bbi_ref, ccr_ref, cci_ref, d_ref, yg_ref, hr_ref, hi_ref,
             cr_ref, ci_ref, pr_ref, pi_ref, up_ref, y_ref):
        @pl.when(pl.program_id(0) == 0)
        def _():
            cr_ref[...] = jnp.zeros_like(cr_ref)
            ci_ref[...] = jnp.zeros_like(ci_ref)
            _power_table(lr_ref[...], li_ref[...], pr_ref, pi_ref)

        _load_segments(u_ref, up_ref)
        ub = up_ref[...].astype(BF16)
        for j in range(N_CHUNK):
            hs = slice(j * H_CHUNK, (j + 1) * H_CHUNK)
            us = slice(j * U_CHUNK, (j + 1) * U_CHUNK)
            hr_ref[:, hs] = _nn(ub[:, us], bbr_ref[j])
            hi_ref[:, hs] = _nn(ub[:, us], bbi_ref[j])
        for c in range(STATE_WIDTH // SCAN_LANES):
            lanes = slice(c * SCAN_LANES, (c + 1) * SCAN_LANES)
            _segment_scan(hr_ref, hi_ref, lanes, (lr_ref[:, lanes], li_ref[:, lanes]),
                          lambda k, lanes=lanes: (_table_rows(pr_ref, k, lanes), _table_rows(pi_ref, k, lanes)),
                          cr_ref, ci_ref, False)
        for j in range(N_CHUNK):
            hs = slice(j * H_CHUNK, (j + 1) * H_CHUNK)
            us = slice(j * U_CHUNK, (j + 1) * U_CHUNK)
            y = (_nn(hr_ref[:, hs].astype(BF16), ccr_ref[j]) - _nn(hi_ref[:, hs].astype(BF16), cci_ref[j])
                 + d_ref[:, us] * up_ref[:, us])
            y_ref[:, us] = _gelu_and_grad(y)[0]
        yg_ref[...] = _nn(_perm_matrix(False), y_ref[...].astype(BF16)).astype(BF16)

    return pl.pallas_call(
        body, name=name, grid=(t // tt,),
        in_specs=[_chunked_spec(tt, lambda i: i), _VMEM, _VMEM, _VMEM, _VMEM, _VMEM, _VMEM, _VMEM],
        out_specs=[_row_spec(tt, SSM_WIDTH), _row_spec(tt, STATE_WIDTH), _row_spec(tt, STATE_WIDTH)],
        out_shape=[jax.ShapeDtypeStruct((t, SSM_WIDTH), BF16), jax.ShapeDtypeStruct((t, STATE_WIDTH), F32),
                   jax.ShapeDtypeStruct((t, STATE_WIDTH), F32)],
        scratch_shapes=[pltpu.VMEM((1, STATE_WIDTH), F32), pltpu.VMEM((1, STATE_WIDTH), F32),
                        pltpu.VMEM((SCAN_TILE, STATE_WIDTH), F32), pltpu.VMEM((SCAN_TILE, STATE_WIDTH), F32),
                        pltpu.VMEM((tt, SSM_WIDTH), F32), pltpu.VMEM((tt, SSM_WIDTH), F32)],
        compiler_params=_params(("arbitrary",)),
    )(u, lam_re, lam_im, bb_re, bb_im, cc_re, cc_im, d_skip)


def ssm_bwd(dyg, u, h_re, h_im, lam_re, lam_im, bb_re, bb_im, cc_re, cc_im, d_skip, name):
    t = u.shape[1]
    tt = SCAN_TILE
    nt = t // tt

    def body(dyg_ref, u_ref, hr_ref, hi_ref, lr_ref, li_ref, bbr_ref, bbi_ref, ccr_ref, cci_ref, d_ref,
             du_ref, dlr_ref, dli_ref, dbbr_ref, dbbi_ref, dccr_ref, dcci_ref, dd_ref,
             ar_ref, ai_ref, cr_ref, ci_ref, pr_ref, pi_ref, up_ref, dy_ref, dup_ref):
        step = pl.program_id(0)
        tile = nt - 1 - step

        @pl.when(step == 0)
        def _():
            for ref in (cr_ref, ci_ref, dlr_ref, dli_ref, dbbr_ref, dbbi_ref, dccr_ref, dcci_ref, dd_ref):
                ref[...] = jnp.zeros_like(ref)
            _power_table(lr_ref[...], li_ref[...], pr_ref, pi_ref)

        _load_segments(u_ref, up_ref)
        _load_segments(dyg_ref, dy_ref)
        uv = up_ref[...]
        ub = uv.astype(BF16)
        dskip = d_ref[...]
        for j in range(N_CHUNK):
            hs = slice(j * H_CHUNK, (j + 1) * H_CHUNK)
            us = slice(j * U_CHUNK, (j + 1) * U_CHUNK)
            hrb = hr_ref[:, hs].astype(BF16)
            hib = hi_ref[:, hs].astype(BF16)
            y = _nn(hrb, ccr_ref[j]) - _nn(hib, cci_ref[j]) + dskip[:, us] * uv[:, us]
            dy = dy_ref[:, us] * _gelu_and_grad(y)[1]
            dy_ref[:, us] = dy
            dyb = dy.astype(BF16)
            dccr_ref[j] += _tn(hrb, dyb)
            dcci_ref[j] -= _tn(hib, dyb)
            ar_ref[:, hs] = _nt(dyb, ccr_ref[j])
            ai_ref[:, hs] = -_nt(dyb, cci_ref[j])
        dd_ref[...] += jnp.sum(dy_ref[...] * uv, axis=0, keepdims=True)

        for c in range(STATE_WIDTH // SCAN_LANES):
            lanes = slice(c * SCAN_LANES, (c + 1) * SCAN_LANES)

            def dlambda(rows, carry, ar, ai, lanes=lanes):
                nr, ni, accr, acci = carry
                hr, hi = hr_ref[rows, lanes], hi_ref[rows, lanes]
                return ar, ai, accr + nr * hr + ni * hi, acci + ni * hr - nr * hi

            _, _, accr, acci = _segment_scan(
                ar_ref, ai_ref, lanes, (lr_ref[:, lanes], -li_ref[:, lanes]),
                lambda k, lanes=lanes: (_table_rows(pr_ref, k, lanes), -_table_rows(pi_ref, k, lanes)),
                cr_ref, ci_ref, True, dlambda)
            dlr_ref[:, lanes] += accr
            dli_ref[:, lanes] += acci

        rho = lax.broadcasted_iota(jnp.int32, (tt, U_CHUNK), 0)
        time = tile * tt + (rho % SUB) * SEG + rho // SUB
        for j in range(N_CHUNK):
            hs = slice(j * H_CHUNK, (j + 1) * H_CHUNK)
            us = slice(j * U_CHUNK, (j + 1) * U_CHUNK)
            arb = ar_ref[:, hs].astype(BF16)
            aib = ai_ref[:, hs].astype(BF16)
            dbbr_ref[j] += _tn(ub[:, us], arb)
            dbbi_ref[j] += _tn(ub[:, us], aib)
            du = _nt(arb, bbr_ref[j]) + _nt(aib, bbi_ref[j]) + dy_ref[:, us] * dskip[:, us]
            dup_ref[:, us] = jnp.where(time >= PAD_FRONT, du, 0.0)
        du_ref[...] = _nn(_perm_matrix(False), dup_ref[...].astype(BF16)).astype(BF16)

    rev = lambda i: (nt - 1 - i, 0)
    full = lambda shape: pl.BlockSpec(shape, lambda i: (0,) * len(shape))
    return pl.pallas_call(
        body, name=name, grid=(nt,),
        in_specs=[_chunked_spec(tt, lambda i: nt - 1 - i), _chunked_spec(tt, lambda i: nt - 1 - i),
                  pl.BlockSpec((tt, STATE_WIDTH), rev), pl.BlockSpec((tt, STATE_WIDTH), rev),
                  _VMEM, _VMEM, _VMEM, _VMEM, _VMEM, _VMEM, _VMEM],
        out_specs=[pl.BlockSpec((tt, SSM_WIDTH), rev), full((SUB, STATE_WIDTH)), full((SUB, STATE_WIDTH)),
                   full((N_CHUNK, U_CHUNK, H_CHUNK)), full((N_CHUNK, U_CHUNK, H_CHUNK)),
                   full((N_CHUNK, H_CHUNK, U_CHUNK)), full((N_CHUNK, H_CHUNK, U_CHUNK)), full((1, SSM_WIDTH))],
        out_shape=[jax.ShapeDtypeStruct((t, SSM_WIDTH), BF16),
                   jax.ShapeDtypeStruct((SUB, STATE_WIDTH), F32), jax.ShapeDtypeStruct((SUB, STATE_WIDTH), F32),
                   jax.ShapeDtypeStruct((N_CHUNK, U_CHUNK, H_CHUNK), F32),
                   jax.ShapeDtypeStruct((N_CHUNK, U_CHUNK, H_CHUNK), F32),
                   jax.ShapeDtypeStruct((N_CHUNK, H_CHUNK, U_CHUNK), F32),
                   jax.ShapeDtypeStruct((N_CHUNK, H_CHUNK, U_CHUNK), F32),
                   jax.ShapeDtypeStruct((1, SSM_WIDTH), F32)],
        scratch_shapes=[pltpu.VMEM((tt, STATE_WIDTH), F32), pltpu.VMEM((tt, STATE_WIDTH), F32),
                        pltpu.VMEM((1, STATE_WIDTH), F32), pltpu.VMEM((1, STATE_WIDTH), F32),
                        pltpu.VMEM((SCAN_TILE, STATE_WIDTH), F32), pltpu.VMEM((SCAN_TILE, STATE_WIDTH), F32),
                        pltpu.VMEM((tt, SSM_WIDTH), F32), pltpu.VMEM((tt, SSM_WIDTH), F32),
                        pltpu.VMEM((tt, SSM_WIDTH), F32)],
        compiler_params=_params(("arbitrary",)),
    )(dyg, u, h_re, h_im, lam_re, lam_im, bb_re, bb_im, cc_re, cc_im, d_skip)


def merge_fwd(h, o, yg, gates, wap_t, wv_t, wgg_t, wout, name):
    t, d = h.shape
    tm = TOKEN_TILE

    def body(h_ref, o_ref, yg_ref, gt_ref, wap_ref, wv_ref, wgg_ref, wout_ref, ho_ref, mg_ref, a_ref, sv_ref, sg_ref):
        att = _nt(o_ref[...], wap_ref[...])
        ygv = yg_ref[...]
        sv = _nt(ygv, wv_ref[...])
        sg = _nt(ygv, wgg_ref[...])
        a_ref[...] = att.astype(BF16)
        sv_ref[...] = sv.astype(BF16)
        sg_ref[...] = sg.astype(BF16)
        merged = (jax.nn.sigmoid(gt_ref[:, 0:d].astype(F32)) * att
                  + jax.nn.sigmoid(gt_ref[:, d:2 * d].astype(F32)) * (sv * jax.nn.sigmoid(sg))).astype(BF16)
        mg_ref[...] = merged
        ho_ref[...] = h_ref[...] + _nn(merged, wout_ref[...])

    return pl.pallas_call(
        body, name=name, grid=(t // tm,),
        in_specs=[_row_spec(tm, d), _row_spec(tm, ATTN_WIDTH), _row_spec(tm, SSM_WIDTH), _row_spec(tm, 2 * d),
                  _VMEM, _VMEM, _VMEM, _VMEM],
        out_specs=[_row_spec(tm, d), _row_spec(tm, d), _row_spec(tm, d), _row_spec(tm, d), _row_spec(tm, d)],
        out_shape=[jax.ShapeDtypeStruct((t, d), F32), jax.ShapeDtypeStruct((t, d), BF16),
                   jax.ShapeDtypeStruct((t, d), BF16), jax.ShapeDtypeStruct((t, d), BF16),
                   jax.ShapeDtypeStruct((t, d), BF16)],
        compiler_params=_params(("arbitrary",)),
    )(h, o, yg, gates, wap_t, wv_t, wgg_t, wout)


def merge_bwd(dh, gates, att, sv, sg, wap_t, wv_t, wgg_t, wout, dep, name):
    t, d = dh.shape
    tm = TOKEN_TILE

    def body(dh_ref, gt_ref, a_ref, sv_ref, sg_ref, wap_ref, wv_ref, wgg_ref, wout_ref, dep_ref,
             dgt_ref, da_ref, dsv_ref, dsg_ref, do_ref, dyg_ref, dhb_ref):
        dhb = dh_ref[...].astype(BF16)
        dhb_ref[...] = dhb
        dm = _nt(dhb, wout_ref[...])
        sig_a = jax.nn.sigmoid(gt_ref[:, 0:d].astype(F32))
        sig_s = jax.nn.sigmoid(gt_ref[:, d:2 * d].astype(F32))
        sig_g = jax.nn.sigmoid(sg_ref[...].astype(F32))
        svv = sv_ref[...].astype(F32)
        dgt_ref[:, 0:d] = (dm * a_ref[...].astype(F32) * sig_a * (1.0 - sig_a)).astype(BF16)
        dgt_ref[:, d:2 * d] = (dm * (svv * sig_g) * sig_s * (1.0 - sig_s)).astype(BF16)
        da = (dm * sig_a).astype(BF16)
        d_s = dm * sig_s
        dsv = (d_s * sig_g).astype(BF16)
        dsg = (d_s * svv * sig_g * (1.0 - sig_g)).astype(BF16)
        da_ref[...] = da
        dsv_ref[...] = dsv
        dsg_ref[...] = dsg
        do_ref[...] = _nn(da, wap_ref[...]).astype(BF16)
        dyg = _nn(dsv, wv_ref[...]) + _nn(dsg, wgg_ref[...])
        for j in range(N_CHUNK):
            dyg_ref[j] = dyg[:, j * U_CHUNK:(j + 1) * U_CHUNK]

    return pl.pallas_call(
        body, name=name, grid=(t // tm,),
        in_specs=[_row_spec(tm, d), _row_spec(tm, 2 * d), _row_spec(tm, d), _row_spec(tm, d), _row_spec(tm, d),
                  _VMEM, _VMEM, _VMEM, _VMEM, _ANY],
        out_specs=[_row_spec(tm, 2 * d), _row_spec(tm, d), _row_spec(tm, d), _row_spec(tm, d),
                   _row_spec(tm, ATTN_WIDTH), _chunked_spec(tm, lambda i: i), _row_spec(tm, d)],
        out_shape=[jax.ShapeDtypeStruct((t, 2 * d), BF16), jax.ShapeDtypeStruct((t, d), BF16),
                   jax.ShapeDtypeStruct((t, d), BF16), jax.ShapeDtypeStruct((t, d), BF16),
                   jax.ShapeDtypeStruct((t, ATTN_WIDTH), BF16), jax.ShapeDtypeStruct((N_CHUNK, t, U_CHUNK), F32),
                   jax.ShapeDtypeStruct((t, d), BF16)],
        compiler_params=_params(("arbitrary",)),
    )(dh, gates, att, sv, sg, wap_t, wv_t, wgg_t, wout, dep)


def _adamw_math(w, g, m, v):
    mn = ADAM_B1 * m + (1.0 - ADAM_B1) * g
    vn = ADAM_B2 * v + (1.0 - ADAM_B2) * (g * g)
    m_hat = mn / (1.0 - ADAM_B1 ** ADAM_STEP)
    v_hat = vn / (1.0 - ADAM_B2 ** ADAM_STEP)
    return -ADAM_LR * (m_hat / (jnp.sqrt(v_hat) + ADAM_EPS) + ADAM_WD * w), mn, vn


def sum_adamw_layer(me, landed, partial, w, m, v, layer, prev, name, transposed=False):
    _, rows, cols = landed.shape
    tr = rows // 2 if rows % 32 == 0 and not transposed else rows
    steps = rows // tr

    def body(me_ref, land_ref, own_ref, w_ref, m_ref, v_ref, *rest):
        go_ref, d_ref, mo_ref, vo_ref = rest[-4:]
        who = me_ref[0]
        gv = land_ref[who ^ 1].astype(F32)
        for p in range(2, N_DEV):
            gv = gv + land_ref[who ^ p].astype(F32)
        gv = gv + own_ref[...].astype(F32)
        if transposed:
            gv = gv.T
        go_ref[0] = gv
        d_ref[0], mo_ref[0], vo_ref[0] = _adamw_math(w_ref[0], gv, m_ref[0], v_ref[0])

    if transposed:
        spec3 = pl.BlockSpec((1, cols, rows), lambda r, me_ref: (layer, 0, 0))
    else:
        spec3 = pl.BlockSpec((1, tr, cols), lambda r, me_ref: (layer, r, 0))
    out = jax.ShapeDtypeStruct(w.shape, F32)
    extra = [] if prev is None else list(prev)
    grid_spec = pltpu.PrefetchScalarGridSpec(
        num_scalar_prefetch=1, grid=(steps,),
        in_specs=[pl.BlockSpec((N_DEV, tr, cols), lambda r, me_ref: (0, r, 0)),
                  pl.BlockSpec((tr, cols), lambda r, me_ref: (me_ref[0] * steps + r, 0)),
                  spec3, spec3, spec3] + [_ANY] * len(extra),
        out_specs=[spec3] * 4)
    return pl.pallas_call(
        body, name=name, grid_spec=grid_spec, out_shape=[out] * 4,
        input_output_aliases={6 + j: j for j in range(len(extra))},
        compiler_params=_params(("arbitrary",)),
    )(me, landed, partial, w, m, v, *extra)


def adamw_small(params, name):
    def as2d(a, swap):
        a = jnp.swapaxes(a, -1, -2) if swap else a
        return a.reshape(-1, a.shape[-1]) if a.ndim >= 2 else a.reshape(1, -1)

    n = len(params)
    flat = [as2d(a, swap) for w, g, m, v, swap in params for a in (w, g, m, v)]

    def body(*refs):
        ins, outs = refs[:4 * n], refs[4 * n:]
        for k in range(n):
            w_ref, g_ref, m_ref, v_ref = ins[4 * k:4 * k + 4]
            outs[3 * k][...], outs[3 * k + 1][...], outs[3 * k + 2][...] = _adamw_math(
                w_ref[...], g_ref[...], m_ref[...], v_ref[...])

    outs = pl.pallas_call(
        body, name=name, in_specs=[_VMEM] * (4 * n), out_specs=[_VMEM] * (3 * n),
        out_shape=[jax.ShapeDtypeStruct(flat[4 * k].shape, F32) for k in range(n) for _ in range(3)],
        compiler_params=_params(),
    )(*flat)

    def restore(a, like, swap):
        shape = jnp.swapaxes(like, -1, -2).shape if swap else like.shape
        a = a.reshape(shape)
        return jnp.swapaxes(a, -1, -2) if swap else a

    return [tuple(restore(outs[3 * k + j], params[k][0], params[k][4]) for j in range(3)) for k in range(n)]


def _my_index():
    return 4 * lax.axis_index("x") + 2 * lax.axis_index("y") + lax.axis_index("c")


def _peer(p):
    return (lax.axis_index("x") ^ ((p >> 2) & 1), lax.axis_index("y") ^ ((p >> 1) & 1), lax.axis_index("c") ^ (p & 1))


_HBM = pl.BlockSpec(memory_space=pltpu.HBM)
_SEM = pl.BlockSpec(memory_space=pltpu.SEMAPHORE)
_EFFECT = pltpu.SideEffectType.DATAFLOW_SIDE_EFFECTING


class Exchange:
    RELAYED = (2, 4, 6)

    def __init__(self, srcs, scatter, name, relay=False):
        self.n = n = len(srcs)
        self.scatter = scatter
        self.name = name
        self.relayed = relay
        assert not (relay and scatter)
        self.direct = (1,) + self.RELAYED if relay else tuple(range(1, N_DEV))
        widths = sorted({s.shape[1] for s in srcs}, reverse=True)
        self.ncls = len(widths)
        self.cls = [widths.index(s.shape[1]) for s in srcs]
        self.cnts = [s.shape[0] // N_DEV if scatter else s.shape[0] for s in srcs]
        self.totals = [sum(c for c, k in zip(self.cnts, self.cls) if k == w) for w in range(self.ncls)]
        self.sizer = [max((k for k in range(n) if self.cls[k] == w), key=lambda k: self.cnts[k])
                      for w in range(self.ncls)]
        assert all(N_DEV * self.cnts[self.sizer[w]] >= self.totals[w] for w in range(self.ncls))
        if scatter:
            self.land_shapes = [(N_DEV, c, s.shape[1]) for s, c in zip(srcs, self.cnts)]
        else:
            self.land_shapes = [(N_DEV * c, s.shape[1]) for s, c in zip(srcs, self.cnts)]
        self.dtypes = [s.dtype for s in srcs]

    def _block(self, k, who):
        return pl.ds(pl.multiple_of(who * self.cnts[k], 16), self.cnts[k])

    def _sem(self, p, w):
        return (p - 1) * self.ncls + w

    def start(self, srcs, after):
        n = self.n

        def body(*refs):
            src, land = refs[:n], refs[n:2 * n]
            send_sems, recv_sems = refs[2 * n + 1], refs[2 * n + 2]
            token = refs[-1]
            me = _my_index()
            for p in self.direct:
                for k in range(n):
                    if self.scatter:
                        s_ref, d_ref = src[k].at[self._block(k, me ^ p), :], land[k].at[me]
                    else:
                        s_ref, d_ref = src[k], land[k].at[self._block(k, me), :]
                    pltpu.make_async_remote_copy(
                        src_ref=s_ref, dst_ref=d_ref, send_sem=send_sems.at[self._sem(p, self.cls[k])],
                        recv_sem=recv_sems.at[self._sem(p, self.cls[k])], device_id=_peer(p),
                        device_id_type=MESH).start()
            token[...] = jnp.zeros_like(token)

        sems = pltpu.SemaphoreType.DMA(((N_DEV - 1) * self.ncls,))
        thru = [pltpu.HBM(s.shape, s.dtype) for s in srcs] + [pltpu.HBM(shp, dt) for shp, dt in
                                                               zip(self.land_shapes, self.dtypes)]
        lands = [pltpu.with_memory_space_constraint(lax.empty(shp, dt), pltpu.HBM)
                 for shp, dt in zip(self.land_shapes, self.dtypes)]
        out = pl.pallas_call(
            body, name=self.name + "_start",
            in_specs=[_HBM] * (2 * n) + [_ANY],
            out_shape=[sems, sems] + thru + [jax.ShapeDtypeStruct((8, 128), F32)],
            out_specs=[_SEM, _SEM] + [_HBM] * (2 * n) + [_VMEM],
            input_output_aliases={j: 2 + j for j in range(2 * n)},
            compiler_params=pltpu.CompilerParams(has_side_effects=_EFFECT),
        )(*[pltpu.with_memory_space_constraint(s, pltpu.HBM) for s in srcs], *lands, after)
        return out[:-1], out[-1]

    def _span_copy(self, src, land, w, send_sem, recv_sem, p):
        big = src[self.sizer[w]] if self.scatter else land[self.sizer[w]]
        span = big.at[pl.ds(0, self.totals[w]), :]
        return pltpu.make_async_remote_copy(src_ref=span, dst_ref=span, send_sem=send_sem, recv_sem=recv_sem,
                                            device_id=_peer(p), device_id_type=MESH)

    def relay(self, state, after):
        n = self.n
        send_sems, recv_sems = state[0], state[1]
        thru = state[2:]
        after = list(after) if isinstance(after, (list, tuple)) else [after]
        first_out = 2 * n + 2 + len(after)

        def body(*refs):
            land = refs[n:2 * n]
            send_a, recv_a = refs[2 * n], refs[2 * n + 1]
            send_b, recv_b = refs[first_out], refs[first_out + 1]
            refs[-1][...] = jnp.zeros_like(refs[-1])
            me = _my_index()
            for p in self.RELAYED:
                for w in range(self.ncls):
                    self._span_copy(None, land, w, send_a.at[self._sem(p, w)], recv_a.at[self._sem(p, w)], p).wait_recv()
            for j, p in enumerate(self.RELAYED):
                for k in range(n):
                    rows = land[k].at[self._block(k, me ^ p), :]
                    pltpu.make_async_remote_copy(
                        src_ref=rows, dst_ref=rows, send_sem=send_b.at[j * self.ncls + self.cls[k]],
                        recv_sem=recv_b.at[j * self.ncls + self.cls[k]], device_id=_peer(1),
                        device_id_type=MESH).start()

        sems = pltpu.SemaphoreType.DMA((len(self.RELAYED) * self.ncls,))
        out = pl.pallas_call(
            body, name=self.name + "_relay",
            in_specs=[_HBM] * (2 * n) + [_SEM, _SEM] + [_ANY] * len(after),
            out_shape=[sems, sems] + [pltpu.HBM(a.shape, a.dtype) for a in thru] + [jax.ShapeDtypeStruct((8, 128), F32)],
            out_specs=[_SEM, _SEM] + [_HBM] * (2 * n) + [_VMEM],
            input_output_aliases={j: 2 + j for j in range(2 * n)},
            compiler_params=pltpu.CompilerParams(has_side_effects=_EFFECT),
        )(*thru, send_sems, recv_sems, *after)
        return [send_sems, recv_sems] + list(out[2:-1]) + [out[0], out[1]], out[-1]

    def wait(self, state, after):
        n = self.n
        send_sems, recv_sems = state[0], state[1]
        thru = state[2:2 + 2 * n]
        relay_sems = list(state[2 + 2 * n:])
        assert len(relay_sems) == (2 if self.relayed else 0)
        after = list(after) if isinstance(after, (list, tuple)) else [after]

        def body(*refs):
            src, land = refs[:n], refs[n:2 * n]
            send_a, recv_a = refs[2 * n], refs[2 * n + 1]
            for p in self.direct:
                for w in range(self.ncls):
                    copy = self._span_copy(src, land, w, send_a.at[self._sem(p, w)], recv_a.at[self._sem(p, w)], p)
                    copy.wait_send()
                    if not (self.relayed and p in self.RELAYED):
                        copy.wait_recv()
            if self.relayed:
                send_b, recv_b = refs[2 * n + 2], refs[2 * n + 3]
                for j in range(len(self.RELAYED)):
                    for w in range(self.ncls):
                        copy = self._span_copy(src, land, w, send_b.at[j * self.ncls + w],
                                               recv_b.at[j * self.ncls + w], 1)
                        copy.wait_send()
                        copy.wait_recv()

        out = pl.pallas_call(
            body, name=self.name + "_wait",
            in_specs=[_HBM] * (2 * n) + [_SEM] * (2 + len(relay_sems)) + [_ANY] * len(after),
            out_shape=[pltpu.HBM(a.shape, a.dtype) for a in thru], out_specs=[_HBM] * (2 * n),
            input_output_aliases={j: j for j in range(2 * n)},
            compiler_params=pltpu.CompilerParams(has_side_effects=_EFFECT),
        )(*thru, send_sems, recv_sems, *relay_sems, *after)
        return out[:n], out[n:]

    def place(self, lands, srcs):
        n = self.n
        assert not self.scatter

        def body(*refs):
            src, land = refs[n:2 * n], refs[2 * n:3 * n]
            bufs, sems = refs[3 * n:4 * n], refs[-1]
            me = _my_index()
            loads = [pltpu.make_async_copy(src[k], bufs[k], sems.at[k]) for k in range(n)]
            stores = [pltpu.make_async_copy(bufs[k], land[k].at[self._block(k, me), :], sems.at[k]) for k in range(n)]
            for cp in loads:
                cp.start()
            for k in range(n):
                loads[k].wait()
                stores[k].start()
            for cp in stores:
                cp.wait()

        return pl.pallas_call(
            body, name=self.name + "_place", in_specs=[_ANY] * (2 * n), out_specs=[_ANY] * n,
            out_shape=[jax.ShapeDtypeStruct(a.shape, a.dtype) for a in lands],
            input_output_aliases={j: j for j in range(n)},
            scratch_shapes=[pltpu.VMEM(s.shape, s.dtype) for s in srcs] + [pltpu.SemaphoreType.DMA((n,))],
        )(*lands, *srcs)


def sum_slots(slots, name):
    _, rows, cols = slots.shape
    tr = rows
    if rows > 512:
        for cand in (256, 128, 64, 32, 16, 8):
            if rows % cand == 0:
                tr = cand
                break

    def body(s_ref, o_ref):
        acc = s_ref[0].astype(F32)
        for j in range(1, N_DEV):
            acc = acc + s_ref[j].astype(F32)
        o_ref[...] = acc

    return pl.pallas_call(
        body, name=name, grid=(rows // tr,),
        in_specs=[pl.BlockSpec((N_DEV, tr, cols), lambda i: (0, i, 0))], out_specs=_row_spec(tr, cols),
        out_shape=jax.ShapeDtypeStruct((rows, cols), F32), compiler_params=_params(("arbitrary",)),
    )(slots)


BIG_N = ("ffn1_w_down", "w_out", "ffn2_w_down")
SMALL = ("ffn1_norm", "mix_norm", "attn_sinks", "ssm_a_re", "ssm_a_im", "ssm_log_dt", "ssm_b_re", "ssm_b_im",
         "ssm_c_re", "ssm_c_im", "ssm_d", "ffn2_norm", "final_norm")
PARTS = {"ffn1": ("ffn1_w_gate", "ffn1_w_up", "ffn1_w_down"),
         "mix": ("w_in", "w_out", "w_attn_proj", "w_glu_v", "w_glu_g"),
         "ffn2": ("ffn2_w_gate", "ffn2_w_up", "ffn2_w_down")}


def _to_rows(name, a):
    return a if name in BIG_N else jnp.swapaxes(a, -1, -2)


def local_step(x, tgt, get_weights, put_grads, small):
    seq, d = x.shape
    t = PAD_FRONT + N_META + seq
    cos_t, sin_t = rope_tables(t)
    row = lambda a: a.reshape(1, -1)
    tables = []
    for i in range(DEPTH):
        b_re_t = jnp.swapaxes(small["ssm_b_re"][i], 1, 2)
        b_im_t = jnp.swapaxes(small["ssm_b_im"][i], 1, 2)
        lam_re, lam_im, bbar_re, bbar_im = ssm_prep(small["ssm_a_re"][i], small["ssm_a_im"][i],
                                                    small["ssm_log_dt"][i].reshape(-1, 1), b_re_t, b_im_t, f"ssm_prep_{i}")
        tables.append(((b_re_t, b_im_t),
                       (row(lam_re), row(lam_im), _block_diag_b(bbar_re).astype(BF16), _block_diag_b(bbar_im).astype(BF16),
                        _block_diag_c(small["ssm_c_re"][i]).astype(BF16), _block_diag_c(small["ssm_c_im"][i]).astype(BF16),
                        row(small["ssm_d"][i]))))
    early = [cos_t, sin_t] + [a for _, tab in tables for a in tab[2:6]]
    saved = []
    h = None
    for i in range(DEPTH):
        s = {}
        w = dict(get_weights(i, "ffn1", early if i == 0 else h))
        if i == 0:
            head = jnp.concatenate([jnp.zeros((PAD_FRONT, d), F32), w["meta_tokens"]], axis=0)
            s["h0"], h, s["n1"], s["acts1"] = ffn_fwd(None, row(small["ffn1_norm"][i]), w["ffn1_w_gate"],
                                                      w["ffn1_w_up"], w["ffn1_w_down"], f"ffn1_fwd_{i}", first=(x, head))
        else:
            s["h0"] = h
            h, s["n1"], s["acts1"] = ffn_fwd(h, row(small["ffn1_norm"][i]), w["ffn1_w_gate"], w["ffn1_w_up"],
                                             w["ffn1_w_down"], f"ffn1_fwd_{i}")
        s["h1"] = h
        w.update(get_weights(i, "mix", h))
        s["n2"], s["qkv"], s["u"], s["gates"] = win_fwd(h, row(small["mix_norm"][i]), w["w_in"], cos_t, sin_t,
                                                        f"win_fwd_{i}")
        s["b_t"], s["ssm"] = tables[i]
        s["yg"], s["h_re"], s["h_im"] = ssm_fwd(s["u"], *s["ssm"], f"ssm_fwd_{i}")
        s["o"] = attn_fwd(s["qkv"], row(small["attn_sinks"][i]), f"attn_fwd_{i}")
        h, s["merged"], s["att"], s["sv"], s["sg"] = merge_fwd(
            h, s["o"], s["yg"], s["gates"], w["w_attn_proj"], w["w_glu_v"], w["w_glu_g"], w["w_out"],
            f"merge_fwd_{i}")
        s["h2"] = h
        w.update(get_weights(i, "ffn2", h))
        h, s["n3"], s["acts3"] = ffn_fwd(h, row(small["ffn2_norm"][i]), w["ffn2_w_gate"], w["ffn2_w_up"],
                                               w["ffn2_w_down"], f"ffn2_fwd_{i}")
        s["w"] = w
        saved.append(s)

    loss, dh, d_final = head_fwd_bwd(h, row(small["final_norm"]), tgt)
    gs = {k: [None] * DEPTH for k in SMALL if k != "final_norm"}
    dep = loss
    for i in reversed(range(DEPTH)):
        s = saved[i]
        w = s["w"]
        dh, da, db, sact, dhb, dg = ffn_bwd(dh, s["h2"], row(small["ffn2_norm"][i]), s["acts3"], w["ffn2_w_gate"],
                                            w["ffn2_w_up"], w["ffn2_w_down"], dep, f"ffn2_bwd_{i}")
        gs["ffn2_norm"][i] = dg[0]
        dep = put_grads(i, "ffn2", {"ffn2_w_gate": tn_matmul(da, s["n3"], f"ffn2_dwg_{i}"),
                                    "ffn2_w_up": tn_matmul(db, s["n3"], f"ffn2_dwu_{i}"),
                                    "ffn2_w_down": tn_matmul(sact, dhb, f"ffn2_dwd_{i}")})

        dgates, datt, dsv, dsg, do, dyg, dhb = merge_bwd(dh, s["gates"], s["att"], s["sv"], s["sg"], w["w_attn_proj"],
                                                         w["w_glu_v"], w["w_glu_g"], w["w_out"], dep, f"merge_bwd_{i}")
        gmix = {"w_out": tn_matmul(s["merged"], dhb, f"dwout_{i}"),
                "w_attn_proj": tn_matmul(datt, s["o"], f"dwap_{i}"),
                "w_glu_v": tn_matmul(dsv, s["yg"], f"dwv_{i}"),
                "w_glu_g": tn_matmul(dsg, s["yg"], f"dwgg_{i}")}
        dqkv, dsink = attn_bwd(s["qkv"], do, row(small["attn_sinks"][i]), cos_t, sin_t, f"attn_bwd_{i}")
        gs["attn_sinks"][i] = dsink[:, 0]
        du, dl_re, dl_im, dbb_re, dbb_im, dcc_re, dcc_im, dd = ssm_bwd(dyg, s["u"], s["h_re"], s["h_im"], *s["ssm"],
                                                                      f"ssm_bwd_{i}")
        fold = lambda a: jnp.sum(a, axis=0).reshape(SSM_GROUPS, SSM_STATE)
        da_re, da_im, dldt, db_re_t, db_im_t = ssm_prep_bwd(
            small["ssm_a_re"][i], small["ssm_a_im"][i], small["ssm_log_dt"][i].reshape(-1, 1), *s["b_t"],
            fold(dl_re), fold(dl_im), _diag_of_b(dbb_re), _diag_of_b(dbb_im), f"ssm_prep_bwd_{i}")
        gs["ssm_a_re"][i], gs["ssm_a_im"][i], gs["ssm_log_dt"][i] = da_re, da_im, dldt[:, 0]
        gs["ssm_b_re"][i], gs["ssm_b_im"][i] = jnp.swapaxes(db_re_t, 1, 2), jnp.swapaxes(db_im_t, 1, 2)
        gs["ssm_c_re"][i], gs["ssm_c_im"][i] = _diag_of_c(dcc_re), _diag_of_c(dcc_im)
        gs["ssm_d"][i] = dd[0]
        gmix["w_in"] = tn_matmul([dqkv, du, dgates], s["n2"], f"dwin_{i}")
        dep = put_grads(i, "mix", gmix)
        dh, dg = win_bwd(dh, s["h1"], row(small["mix_norm"][i]), dqkv, du, dgates, w["w_in"], dep, f"win_bwd_{i}")
        gs["mix_norm"][i] = dg[0]

        dh, da, db, sact, dhb, dg = ffn_bwd(dh, s["h0"], row(small["ffn1_norm"][i]), s["acts1"], w["ffn1_w_gate"],
                                            w["ffn1_w_up"], w["ffn1_w_down"], dep, f"ffn1_bwd_{i}")
        gs["ffn1_norm"][i] = dg[0]
        if i > 0:
            dep = put_grads(i, "ffn1", {"ffn1_w_gate": tn_matmul(da, s["n1"], f"ffn1_dwg_{i}"),
                                        "ffn1_w_up": tn_matmul(db, s["n1"], f"ffn1_dwu_{i}"),
                                        "ffn1_w_down": tn_matmul(sact, dhb, f"ffn1_dwd_{i}")})
        else:
            for k, xa, ya in (("ffn1_w_down", sact, dhb), ("ffn1_w_gate", da, s["n1"]), ("ffn1_w_up", db, s["n1"])):
                dep = put_grads(i, "ffn1", {k: tn_matmul(xa, ya, f"d_{k}_{i}", dep)})

    gs = {k: jnp.stack(v) for k, v in gs.items()}
    gs["final_norm"] = d_final[0]
    return loss[0, 0], dh[PAD_FRONT + N_META:], dh[PAD_FRONT:PAD_FRONT + N_META], gs, dep


def _pack_rows(arrays, cols):
    flat = jnp.concatenate([a.reshape(-1) for a in arrays])
    rows = -(-flat.shape[0] // cols)
    rows = -(-rows // 16) * 16
    return jnp.pad(flat, (0, rows * cols - flat.shape[0])).reshape(rows, cols)


def _unpack_rows(packed, shapes):
    flat = packed.reshape(-1)
    out, off = [], 0
    for shp in shapes:
        n = math.prod(shp)
        out.append(flat[off:off + n].reshape(shp))
        off += n
    return out


def kernel(x, meta_tokens, ffn1_norm, ffn1_w_gate, ffn1_w_up, ffn1_w_down, mix_norm, w_in, attn_sinks, ssm_a_re, ssm_a_im, ssm_log_dt, ssm_b_re, ssm_b_im, ssm_c_re, ssm_c_im, ssm_d, w_attn_proj, w_glu_v, w_glu_g, w_out, ffn2_norm, ffn2_w_gate, ffn2_w_up, ffn2_w_down, final_norm, loss_target, m_meta_tokens, m_ffn1_norm, m_ffn1_w_gate, m_ffn1_w_up, m_ffn1_w_down, m_mix_norm, m_w_in, m_attn_sinks, m_ssm_a_re, m_ssm_a_im, m_ssm_log_dt, m_ssm_b_re, m_ssm_b_im, m_ssm_c_re, m_ssm_c_im, m_ssm_d, m_w_attn_proj, m_w_glu_v, m_w_glu_g, m_w_out, m_ffn2_norm, m_ffn2_w_gate, m_ffn2_w_up, m_ffn2_w_down, m_final_norm, v_meta_tokens, v_ffn1_norm, v_ffn1_w_gate, v_ffn1_w_up, v_ffn1_w_down, v_mix_norm, v_w_in, v_attn_sinks, v_ssm_a_re, v_ssm_a_im, v_ssm_log_dt, v_ssm_b_re, v_ssm_b_im, v_ssm_c_re, v_ssm_c_im, v_ssm_d, v_w_attn_proj, v_w_glu_v, v_w_glu_g, v_w_out, v_ffn2_norm, v_ffn2_w_gate, v_ffn2_w_up, v_ffn2_w_down, v_final_norm):
    names = ("meta_tokens", "ffn1_norm", "ffn1_w_gate", "ffn1_w_up", "ffn1_w_down", "mix_norm", "w_in", "attn_sinks",
             "ssm_a_re", "ssm_a_im", "ssm_log_dt", "ssm_b_re", "ssm_b_im", "ssm_c_re", "ssm_c_im", "ssm_d",
             "w_attn_proj", "w_glu_v", "w_glu_g", "w_out", "ffn2_norm", "ffn2_w_gate", "ffn2_w_up", "ffn2_w_down",
             "final_norm")
    weights = dict(zip(names, (meta_tokens, ffn1_norm, ffn1_w_gate, ffn1_w_up, ffn1_w_down, mix_norm, w_in, attn_sinks, ssm_a_re, ssm_a_im, ssm_log_dt, ssm_b_re, ssm_b_im, ssm_c_re, ssm_c_im, ssm_d, w_attn_proj, w_glu_v, w_glu_g, w_out, ffn2_norm, ffn2_w_gate, ffn2_w_up, ffn2_w_down, final_norm)))
    moments_m = dict(zip(names, (m_meta_tokens, m_ffn1_norm, m_ffn1_w_gate, m_ffn1_w_up, m_ffn1_w_down, m_mix_norm, m_w_in, m_attn_sinks, m_ssm_a_re, m_ssm_a_im, m_ssm_log_dt, m_ssm_b_re, m_ssm_b_im, m_ssm_c_re, m_ssm_c_im, m_ssm_d, m_w_attn_proj, m_w_glu_v, m_w_glu_g, m_w_out, m_ffn2_norm, m_ffn2_w_gate, m_ffn2_w_up, m_ffn2_w_down, m_final_norm)))
    moments_v = dict(zip(names, (v_meta_tokens, v_ffn1_norm, v_ffn1_w_gate, v_ffn1_w_up, v_ffn1_w_down, v_mix_norm, v_w_in, v_attn_sinks, v_ssm_a_re, v_ssm_a_im, v_ssm_log_dt, v_ssm_b_re, v_ssm_b_im, v_ssm_c_re, v_ssm_c_im, v_ssm_d, v_w_attn_proj, v_w_glu_v, v_w_glu_g, v_w_out, v_ffn2_norm, v_ffn2_w_gate, v_ffn2_w_up, v_ffn2_w_down, v_final_norm)))
    me = _my_index()

    order = [(i, part) for i in range(DEPTH) for part in PARTS]
    gathers = {}
    token = jnp.zeros((8, 128), F32)
    for i, part in order:
        shards = [_to_rows(k, weights[k][i]).astype(BF16) for k in PARTS[part]]
        if (i, part) == order[0]:
            shards.append(meta_tokens)
        ex = Exchange(shards, False, f"gather_{part}_{i}", relay=True)
        state, token = ex.start(shards, token)
        gathers[i, part] = [ex, state, False]
    all_started = token

    def relay(group, after):
        ex, state, relayed = gathers[group]
        if relayed:
            return []
        new_state, relay_token = ex.relay(state, after)
        gathers[group][1:] = [new_state, True]
        return [relay_token]

    def get_weights(i, part, after):
        g = order.index((i, part))
        after = [all_started] + list(after) if g == 0 else [after]
        tokens = relay(order[g], after)
        if g >= 2 and g + 1 < len(order):
            tokens += relay(order[g + 1], after)
        ex, state, _ = gathers[i, part]
        shards, lands = ex.wait(state, after + tokens)
        fulls = ex.place(lands, shards)
        got = dict(zip(PARTS[part], fulls))
        if (i, part) == (0, "ffn1"):
            got["meta_tokens"] = jnp.swapaxes(fulls[-1].reshape(N_DEV, N_META, 128), 0, 1).reshape(N_META, D_MODEL)
        return got

    scatters = []

    def put_grads(i, part, gdict):
        ks = list(gdict)
        srcs = [gdict[k] for k in ks]
        ex = Exchange(srcs, True, f"scatter_{part if len(ks) > 1 else ks[0]}_{i}")
        state, tok = ex.start(srcs, all_started)
        scatters.append((i, ks, ex, state))
        return tok

    small = {k: weights[k] for k in SMALL}
    loss, dx, dmeta, gs, last_started = local_step(x[0], loss_target[0], get_weights, put_grads, small)

    grads, deltas, new_m, new_v = {}, {}, {}, {}
    small_list = [loss.reshape(1), dmeta] + [gs[k] for k in SMALL]
    packed = _pack_rows(small_list, D_MODEL)
    small_ex = Exchange([packed], False, "gather_small", relay=True)
    small_state, after = small_ex.start([packed], last_started)

    updated = {}
    me_index = jnp.reshape(me, (1,)).astype(jnp.int32)
    for i, ks, ex, state in scatters:
        partials, lands = ex.wait(state, after)
        for k, partial, slots in zip(ks, partials, lands):
            own_layout = weights[k].shape[-1] % 128 == 0 and k not in BIG_N
            view = (lambda a: a) if own_layout else (lambda a: _to_rows(k, a))
            updated[k] = sum_adamw_layer(me_index, slots, partial, view(weights[k]), view(moments_m[k]),
                                         view(moments_v[k]), i, updated.get(k), f"adamw_{k}_{i}", transposed=own_layout)
            after = updated[k][0]
    for k, outs in updated.items():
        own_layout = weights[k].shape[-1] % 128 == 0 and k not in BIG_N
        grads[k], deltas[k], new_m[k], new_v[k] = [a if own_layout else _to_rows(k, a) for a in outs]

    small_state, relayed = small_ex.relay(small_state, after)
    packed_own, packed_all = small_ex.wait(small_state, [after, relayed])
    (packed_all,) = small_ex.place(packed_all, packed_own)
    total = sum_slots(packed_all.reshape(N_DEV, packed.shape[0], D_MODEL), "sum_small")
    pieces = _unpack_rows(total, [a.shape for a in small_list])
    loss_out = pieces[0][0]
    grads["meta_tokens"] = lax.dynamic_slice_in_dim(pieces[1], me * 128, 128, axis=1)
    for k, p in zip(SMALL, pieces[2:]):
        grads[k] = p
    small_names = ("meta_tokens",) + SMALL
    updates = adamw_small([(weights[k], grads[k], moments_m[k], moments_v[k], k in ("ssm_b_re", "ssm_b_im"))
                           for k in small_names], "adamw_small")
    for k, (d, mn, vn) in zip(small_names, updates):
        deltas[k], new_m[k], new_v[k] = d, mn, vn
    return (loss_out, dx[None], *[grads[k] for k in names], *[deltas[k] for k in names],
            *[new_m[k] for k in names], *[new_v[k] for k in names])
```
